```python
import jax, jax.numpy as jnp
from jax import lax
import numpy as np

D_MODEL = 1024
BATCH = 8
SEQ = 2048
DEPTH = 4

N_A_LAYERS = DEPTH // 2
N_B_LAYERS = DEPTH - N_A_LAYERS
EPS = 1e-5
N_MOD = 9
D_FF = 2816
FFN_HALF = 0.5
SSM_EXPAND = 2
D_INNER = SSM_EXPAND * D_MODEL
SSM_HEADDIM = 64
SSM_HEADS = D_INNER // SSM_HEADDIM
SSM_GROUPS = 8
SSM_STATE = 128
CONV_WIDTH = 4
CHUNK = 128
CONV_DIM = D_INNER + 2 * SSM_GROUPS * SSM_STATE
IN_PROJ_DIM = D_INNER + CONV_DIM + SSM_HEADS
ATT_HEADS = 16
KV_HEADS = 4
HEAD_DIM = 64
WINDOW = 128

kernel_name = "yoco_mamba2_swa_sink_macaron_adaln"


def rmsnorm(x, w):
    xf = x.astype(jnp.float32)
    y = xf * lax.rsqrt(jnp.mean(xf * xf, axis=-1, keepdims=True) + EPS)
    return (y * w.astype(jnp.float32)).astype(x.dtype)


def modulate(h, shift, scale):
    return h * (1 + scale[:, None, :]) + shift[:, None, :]


def swiglu(h, w_gu, w_down):
    g, u = jnp.split(h @ w_gu, 2, axis=-1)
    return (jax.nn.silu(g) * u) @ w_down


def causal_dwconv(u, w, b):
    y = lax.conv_general_dilated(u, w[:, None, :], window_strides=(1,),
                                 padding=[(CONV_WIDTH - 1, 0)],
                                 dimension_numbers=("NWC", "WIO", "NWC"),
                                 feature_group_count=u.shape[-1])
    return y + b


def ssd_scan(x, dt, A, Bm, Cm):
    b, l, h, p = x.shape
    g, n = Bm.shape[-2:]
    r = h // g
    nc = l // CHUNK
    X = (x * dt[..., None]).reshape(b, nc, CHUNK, g, r, p)
    a = jnp.moveaxis((dt * A).reshape(b, nc, CHUNK, g, r), 2, -1)
    a_cs = jnp.cumsum(a, axis=-1)
    Bc = Bm.reshape(b, nc, CHUNK, g, n)
    Cc = Cm.reshape(b, nc, CHUNK, g, n)
    idx = jnp.arange(CHUNK)
    causal = idx[:, None] >= idx[None, :]
    seg = a_cs[..., :, None] - a_cs[..., None, :]
    Lmat = jnp.exp(jnp.where(causal, seg, -jnp.inf))
    CB = jnp.einsum("bclgn,bcsgn->bcgls", Cc, Bc)
    y_diag = jnp.einsum("bcgrls,bcsgrp->bclgrp", CB[:, :, :, None] * Lmat, X)
    decay_to_end = jnp.exp(a_cs[..., -1:] - a_cs)
    chunk_states = jnp.einsum("bclgn,bcgrl,bclgrp->bcgrpn", Bc, decay_to_end, X)
    chunk_decay = jnp.exp(a_cs[..., -1])

    def step(state, inp):
        s_c, d_c = inp
        return state * d_c[..., None, None] + s_c, state

    init = jnp.zeros((b, g, r, p, n), jnp.float32)
    _, prev = lax.scan(step, init, (jnp.moveaxis(chunk_states, 1, 0),
                                    jnp.moveaxis(chunk_decay, 1, 0)))
    prev = jnp.moveaxis(prev, 0, 1)
    y_off = jnp.einsum("bclgn,bcgrpn,bcgrl->bclgrp", Cc, prev, jnp.exp(a_cs))
    return (y_diag + y_off).reshape(b, l, h, p)


def mamba2_mixer(h, w_in, conv_w, conv_b, dt_bias, a_log, d_skip, norm_w, w_out):
    b, l, _ = h.shape
    f32 = jnp.float32
    z, xbc, dt = jnp.split(h @ w_in, [D_INNER, D_INNER + CONV_DIM], axis=-1)
    xbc = jax.nn.silu(causal_dwconv(xbc, conv_w, conv_b))
    xs, Bm, Cm = jnp.split(xbc, [D_INNER, D_INNER + SSM_GROUPS * SSM_STATE], axis=-1)
    dt = jax.nn.softplus(dt.astype(f32) + dt_bias.astype(f32))
    A = -jnp.exp(a_log.astype(f32))
    xs_h = xs.astype(f32).reshape(b, l, SSM_HEADS, SSM_HEADDIM)
    y = ssd_scan(xs_h, dt, A,
                 Bm.astype(f32).reshape(b, l, SSM_GROUPS, SSM_STATE),
                 Cm.astype(f32).reshape(b, l, SSM_GROUPS, SSM_STATE))
    y = y + d_skip.astype(f32)[:, None] * xs_h
    y = y.reshape(b, l, D_INNER) * jax.nn.silu(z.astype(f32))
    yg = y.reshape(b, l, SSM_GROUPS, D_INNER // SSM_GROUPS)
    yg = yg * lax.rsqrt(jnp.mean(yg * yg, axis=-1, keepdims=True) + EPS)
    y = yg.reshape(b, l, D_INNER) * norm_w.astype(f32)
    return y.astype(h.dtype) @ w_out


def to_bands(t):
    b, l = t.shape[:2]
    nb = l // WINDOW
    blocks = t.reshape(b, nb, WINDOW, *t.shape[2:])
    prev = jnp.pad(blocks[:, :-1], ((0, 0), (1, 0)) + ((0, 0),) * (blocks.ndim - 2))
    return jnp.concatenate([prev, blocks], axis=2)


def shared_kv(x, norm_w, shift, scale, w_kv, b_kv):
    b, l, _ = x.shape
    h = modulate(rmsnorm(x, norm_w), shift, scale)
    k, v = jnp.split(h @ w_kv + b_kv, 2, axis=-1)
    k = k.reshape(b, l, KV_HEADS, HEAD_DIM)
    v = v.reshape(b, l, KV_HEADS, HEAD_DIM)
    return to_bands(k), to_bands(v)


def swa_sink_attention(h, k_band, v_band, w_q, b_q, sinks, w_o, b_o):
    b, l, _ = h.shape
    nb = l // WINDOW
    r = ATT_HEADS // KV_HEADS
    f32 = jnp.float32
    q = (h @ w_q + b_q).reshape(b, nb, WINDOW, KV_HEADS, r, HEAD_DIM)
    s = jnp.einsum("bnqkrd,bnskd->bkrnqs", q.astype(f32), k_band.astype(f32)) * HEAD_DIM ** -0.5
    qpos = jnp.arange(WINDOW)[:, None] + WINDOW
    kpos = jnp.arange(2 * WINDOW)[None, :]
    local = (kpos <= qpos) & (kpos > qpos - WINDOW)
    mask = local[None] & ((jnp.arange(nb)[:, None, None] > 0) | (kpos[None] >= WINDOW))
    logits = jnp.where(mask, s, -jnp.inf)
    sink = sinks.astype(f32).reshape(1, KV_HEADS, r, 1, 1)
    m = jnp.maximum(jnp.max(logits, axis=-1), sink)
    p = jnp.exp(logits - m[..., None])
    denom = jnp.sum(p, axis=-1) + jnp.exp(sink - m)
    probs = (p / denom[..., None]).astype(v_band.dtype)
    o = jnp.einsum("bkrnqs,bnskd->bnqkrd", probs, v_band)
    return o.reshape(b, l, ATT_HEADS * HEAD_DIM) @ w_o + b_o


def _fwd_setup_inputs(seed: int = 0) -> dict:
    key = jax.random.key(seed)
    ks = jax.random.split(key, 32)
    f32 = jnp.float32
    nrm = lambda k, shape, s: jax.random.normal(k, shape, f32) * s
    D = D_MODEL
    dt0 = jnp.exp(jax.random.uniform(ks[10], (N_A_LAYERS, SSM_HEADS), f32,
                                     np.log(1e-3), np.log(1e-1)))
    dt_bias = dt0 + jnp.log(-jnp.expm1(-dt0))
    return {
        "x": nrm(ks[0], (BATCH, SEQ, D), 1.0),
        "c": nrm(ks[1], (BATCH, D), 1.0),
        "ffn_norm_w": 1.0 + nrm(ks[2], (DEPTH, 2, D), 0.02),
        "ffn_w_gu": nrm(ks[3], (DEPTH, 2, D, 2 * D_FF), D ** -0.5),
        "ffn_w_down": nrm(ks[4], (DEPTH, 2, D_FF, D), D_FF ** -0.5),
        "mod_w": nrm(ks[5], (DEPTH, D, N_MOD * D), 0.5 * D ** -0.5),
        "mod_b": nrm(ks[6], (DEPTH, N_MOD * D), 0.02),
        "mix_norm_w": 1.0 + nrm(ks[7], (DEPTH, D), 0.02),
        "ssm_w_in": nrm(ks[8], (N_A_LAYERS, D, IN_PROJ_DIM), D ** -0.5),
        "ssm_conv_w": nrm(ks[9], (N_A_LAYERS, CONV_WIDTH, CONV_DIM), CONV_WIDTH ** -0.5),
        "ssm_conv_b": nrm(ks[11], (N_A_LAYERS, CONV_DIM), 0.02),
        "ssm_dt_bias": dt_bias,
        "ssm_a_log": jnp.log(jax.random.uniform(ks[12], (N_A_LAYERS, SSM_HEADS), f32, 1.0, 16.0)),
        "ssm_d": 1.0 + nrm(ks[13], (N_A_LAYERS, SSM_HEADS), 0.1),
        "ssm_norm_w": 1.0 + nrm(ks[14], (N_A_LAYERS, D_INNER), 0.02),
        "ssm_w_out": nrm(ks[15], (N_A_LAYERS, D_INNER, D), D_INNER ** -0.5),
        "kv_norm_w": 1.0 + nrm(ks[16], (D,), 0.02),
        "kv_mod_w": nrm(ks[17], (D, 2 * D), 0.5 * D ** -0.5),
        "kv_mod_b": nrm(ks[18], (2 * D,), 0.02),
        "w_kv": nrm(ks[19], (D, 2 * KV_HEADS * HEAD_DIM), D ** -0.5),
        "b_kv": nrm(ks[20], (2 * KV_HEADS * HEAD_DIM,), 0.02),
        "attn_w_q": nrm(ks[21], (N_B_LAYERS, D, ATT_HEADS * HEAD_DIM), D ** -0.5),
        "attn_b_q": nrm(ks[22], (N_B_LAYERS, ATT_HEADS * HEAD_DIM), 0.02),
        "attn_sinks": nrm(ks[23], (N_B_LAYERS, ATT_HEADS), 0.5),
        "attn_w_o": nrm(ks[24], (N_B_LAYERS, ATT_HEADS * HEAD_DIM, D), (ATT_HEADS * HEAD_DIM) ** -0.5),
        "attn_b_o": nrm(ks[25], (N_B_LAYERS, D), 0.02),
        "final_norm_w": 1.0 + nrm(ks[26], (D,), 0.02),
    }


def _fwd_reference(x, c, ffn_norm_w, ffn_w_gu, ffn_w_down, mod_w, mod_b, mix_norm_w,
              ssm_w_in, ssm_conv_w, ssm_conv_b, ssm_dt_bias, ssm_a_log, ssm_d,
              ssm_norm_w, ssm_w_out, kv_norm_w, kv_mod_w, kv_mod_b, w_kv, b_kv,
              attn_w_q, attn_b_q, attn_sinks, attn_w_o, attn_b_o, final_norm_w):
    c_act = jax.nn.silu(c)
    k_band = v_band = None
    for i in range(DEPTH):
        if i == N_A_LAYERS:
            kv_shift, kv_scale = jnp.split(c_act @ kv_mod_w + kv_mod_b, 2, axis=-1)
            k_band, v_band = shared_kv(x, kv_norm_w, kv_shift, kv_scale, w_kv, b_kv)
        mod = c_act @ mod_w[i] + mod_b[i]
        sh1, sc1, g1, shm, scm, gm, sh2, sc2, g2 = jnp.split(mod, N_MOD, axis=-1)
        h = modulate(rmsnorm(x, ffn_norm_w[i, 0]), sh1, sc1)
        x = x + FFN_HALF * g1[:, None, :] * swiglu(h, ffn_w_gu[i, 0], ffn_w_down[i, 0])
        h = modulate(rmsnorm(x, mix_norm_w[i]), shm, scm)
        if i < N_A_LAYERS:
            j = i
            y = mamba2_mixer(h, ssm_w_in[j], ssm_conv_w[j], ssm_conv_b[j], ssm_dt_bias[j],
                             ssm_a_log[j], ssm_d[j], ssm_norm_w[j], ssm_w_out[j])
        else:
            j = i - N_A_LAYERS
            y = swa_sink_attention(h, k_band, v_band, attn_w_q[j], attn_b_q[j],
                                   attn_sinks[j], attn_w_o[j], attn_b_o[j])
        x = x + gm[:, None, :] * y
        h = modulate(rmsnorm(x, ffn_norm_w[i, 1]), sh2, sc2)
        x = x + FFN_HALF * g2[:, None, :] * swiglu(h, ffn_w_gu[i, 1], ffn_w_down[i, 1])
    return rmsnorm(x, final_norm_w)


import jax as _jax
import jax.numpy as _jnp

TWIN_FORMAT = 'train_step'
FWD_PARAMS = ['x', 'c', 'ffn_norm_w', 'ffn_w_gu', 'ffn_w_down', 'mod_w', 'mod_b', 'mix_norm_w', 'ssm_w_in', 'ssm_conv_w', 'ssm_conv_b', 'ssm_dt_bias', 'ssm_a_log', 'ssm_d', 'ssm_norm_w', 'ssm_w_out', 'kv_norm_w', 'kv_mod_w', 'kv_mod_b', 'w_kv', 'b_kv', 'attn_w_q', 'attn_b_q', 'attn_sinks', 'attn_w_o', 'attn_b_o', 'final_norm_w']
TWIN_WEIGHTS = ['ffn_norm_w', 'ffn_w_gu', 'ffn_w_down', 'mod_w', 'mod_b', 'mix_norm_w', 'ssm_w_in', 'ssm_conv_w', 'ssm_conv_b', 'ssm_dt_bias', 'ssm_a_log', 'ssm_d', 'ssm_norm_w', 'ssm_w_out', 'kv_norm_w', 'kv_mod_w', 'kv_mod_b', 'w_kv', 'b_kv', 'attn_w_q', 'attn_b_q', 'attn_sinks', 'attn_w_o', 'attn_b_o', 'final_norm_w']
TWIN_DIFF_INPUT = 'x'
TWIN_INPUTS = ['x', 'c', 'ffn_norm_w', 'ffn_w_gu', 'ffn_w_down', 'mod_w', 'mod_b', 'mix_norm_w', 'ssm_w_in', 'ssm_conv_w', 'ssm_conv_b', 'ssm_dt_bias', 'ssm_a_log', 'ssm_d', 'ssm_norm_w', 'ssm_w_out', 'kv_norm_w', 'kv_mod_w', 'kv_mod_b', 'w_kv', 'b_kv', 'attn_w_q', 'attn_b_q', 'attn_sinks', 'attn_w_o', 'attn_b_o', 'final_norm_w', 'loss_target', 'm_ffn_norm_w', 'm_ffn_w_gu', 'm_ffn_w_down', 'm_mod_w', 'm_mod_b', 'm_mix_norm_w', 'm_ssm_w_in', 'm_ssm_conv_w', 'm_ssm_conv_b', 'm_ssm_dt_bias', 'm_ssm_a_log', 'm_ssm_d', 'm_ssm_norm_w', 'm_ssm_w_out', 'm_kv_norm_w', 'm_kv_mod_w', 'm_kv_mod_b', 'm_w_kv', 'm_b_kv', 'm_attn_w_q', 'm_attn_b_q', 'm_attn_sinks', 'm_attn_w_o', 'm_attn_b_o', 'm_final_norm_w', 'v_ffn_norm_w', 'v_ffn_w_gu', 'v_ffn_w_down', 'v_mod_w', 'v_mod_b', 'v_mix_norm_w', 'v_ssm_w_in', 'v_ssm_conv_w', 'v_ssm_conv_b', 'v_ssm_dt_bias', 'v_ssm_a_log', 'v_ssm_d', 'v_ssm_norm_w', 'v_ssm_w_out', 'v_kv_norm_w', 'v_kv_mod_w', 'v_kv_mod_b', 'v_w_kv', 'v_b_kv', 'v_attn_w_q', 'v_attn_b_q', 'v_attn_sinks', 'v_attn_w_o', 'v_attn_b_o', 'v_final_norm_w']
TWIN_OUTPUTS = ['loss', 'grad_x', 'grad_ffn_norm_w', 'grad_ffn_w_gu', 'grad_ffn_w_down', 'grad_mod_w', 'grad_mod_b', 'grad_mix_norm_w', 'grad_ssm_w_in', 'grad_ssm_conv_w', 'grad_ssm_conv_b', 'grad_ssm_dt_bias', 'grad_ssm_a_log', 'grad_ssm_d', 'grad_ssm_norm_w', 'grad_ssm_w_out', 'grad_kv_norm_w', 'grad_kv_mod_w', 'grad_kv_mod_b', 'grad_w_kv', 'grad_b_kv', 'grad_attn_w_q', 'grad_attn_b_q', 'grad_attn_sinks', 'grad_attn_w_o', 'grad_attn_b_o', 'grad_final_norm_w', 'delta_ffn_norm_w', 'delta_ffn_w_gu', 'delta_ffn_w_down', 'delta_mod_w', 'delta_mod_b', 'delta_mix_norm_w', 'delta_ssm_w_in', 'delta_ssm_conv_w', 'delta_ssm_conv_b', 'delta_ssm_dt_bias', 'delta_ssm_a_log', 'delta_ssm_d', 'delta_ssm_norm_w', 'delta_ssm_w_out', 'delta_kv_norm_w', 'delta_kv_mod_w', 'delta_kv_mod_b', 'delta_w_kv', 'delta_b_kv', 'delta_attn_w_q', 'delta_attn_b_q', 'delta_attn_sinks', 'delta_attn_w_o', 'delta_attn_b_o', 'delta_final_norm_w', 'new_m_ffn_norm_w', 'new_m_ffn_w_gu', 'new_m_ffn_w_down', 'new_m_mod_w', 'new_m_mod_b', 'new_m_mix_norm_w', 'new_m_ssm_w_in', 'new_m_ssm_conv_w', 'new_m_ssm_conv_b', 'new_m_ssm_dt_bias', 'new_m_ssm_a_log', 'new_m_ssm_d', 'new_m_ssm_norm_w', 'new_m_ssm_w_out', 'new_m_kv_norm_w', 'new_m_kv_mod_w', 'new_m_kv_mod_b', 'new_m_w_kv', 'new_m_b_kv', 'new_m_attn_w_q', 'new_m_attn_b_q', 'new_m_attn_sinks', 'new_m_attn_w_o', 'new_m_attn_b_o', 'new_m_final_norm_w', 'new_v_ffn_norm_w', 'new_v_ffn_w_gu', 'new_v_ffn_w_down', 'new_v_mod_w', 'new_v_mod_b', 'new_v_mix_norm_w', 'new_v_ssm_w_in', 'new_v_ssm_conv_w', 'new_v_ssm_conv_b', 'new_v_ssm_dt_bias', 'new_v_ssm_a_log', 'new_v_ssm_d', 'new_v_ssm_norm_w', 'new_v_ssm_w_out', 'new_v_kv_norm_w', 'new_v_kv_mod_w', 'new_v_kv_mod_b', 'new_v_w_kv', 'new_v_b_kv', 'new_v_attn_w_q', 'new_v_attn_b_q', 'new_v_attn_sinks', 'new_v_attn_w_o', 'new_v_attn_b_o', 'new_v_final_norm_w']
TWIN_LEAF_KINDS = {'loss': 'loss', 'grad_x': 'grad_x', 'grad_ffn_norm_w': 'grad_w', 'grad_ffn_w_gu': 'grad_w', 'grad_ffn_w_down': 'grad_w', 'grad_mod_w': 'grad_w', 'grad_mod_b': 'grad_w', 'grad_mix_norm_w': 'grad_w', 'grad_ssm_w_in': 'grad_w', 'grad_ssm_conv_w': 'grad_w', 'grad_ssm_conv_b': 'grad_w', 'grad_ssm_dt_bias': 'grad_w', 'grad_ssm_a_log': 'grad_w', 'grad_ssm_d': 'grad_w', 'grad_ssm_norm_w': 'grad_w', 'grad_ssm_w_out': 'grad_w', 'grad_kv_norm_w': 'grad_w', 'grad_kv_mod_w': 'grad_w', 'grad_kv_mod_b': 'grad_w', 'grad_w_kv': 'grad_w', 'grad_b_kv': 'grad_w', 'grad_attn_w_q': 'grad_w', 'grad_attn_b_q': 'grad_w', 'grad_attn_sinks': 'grad_w', 'grad_attn_w_o': 'grad_w', 'grad_attn_b_o': 'grad_w', 'grad_final_norm_w': 'grad_w', 'delta_ffn_norm_w': 'delta_w', 'delta_ffn_w_gu': 'delta_w', 'delta_ffn_w_down': 'delta_w', 'delta_mod_w': 'delta_w', 'delta_mod_b': 'delta_w', 'delta_mix_norm_w': 'delta_w', 'delta_ssm_w_in': 'delta_w', 'delta_ssm_conv_w': 'delta_w', 'delta_ssm_conv_b': 'delta_w', 'delta_ssm_dt_bias': 'delta_w', 'delta_ssm_a_log': 'delta_w', 'delta_ssm_d': 'delta_w', 'delta_ssm_norm_w': 'delta_w', 'delta_ssm_w_out': 'delta_w', 'delta_kv_norm_w': 'delta_w', 'delta_kv_mod_w': 'delta_w', 'delta_kv_mod_b': 'delta_w', 'delta_w_kv': 'delta_w', 'delta_b_kv': 'delta_w', 'delta_attn_w_q': 'delta_w', 'delta_attn_b_q': 'delta_w', 'delta_attn_sinks': 'delta_w', 'delta_attn_w_o': 'delta_w', 'delta_attn_b_o': 'delta_w', 'delta_final_norm_w': 'delta_w', 'new_m_ffn_norm_w': 'new_m', 'new_m_ffn_w_gu': 'new_m', 'new_m_ffn_w_down': 'new_m', 'new_m_mod_w': 'new_m', 'new_m_mod_b': 'new_m', 'new_m_mix_norm_w': 'new_m', 'new_m_ssm_w_in': 'new_m', 'new_m_ssm_conv_w': 'new_m', 'new_m_ssm_conv_b': 'new_m', 'new_m_ssm_dt_bias': 'new_m', 'new_m_ssm_a_log': 'new_m', 'new_m_ssm_d': 'new_m', 'new_m_ssm_norm_w': 'new_m', 'new_m_ssm_w_out': 'new_m', 'new_m_kv_norm_w': 'new_m', 'new_m_kv_mod_w': 'new_m', 'new_m_kv_mod_b': 'new_m', 'new_m_w_kv': 'new_m', 'new_m_b_kv': 'new_m', 'new_m_attn_w_q': 'new_m', 'new_m_attn_b_q': 'new_m', 'new_m_attn_sinks': 'new_m', 'new_m_attn_w_o': 'new_m', 'new_m_attn_b_o': 'new_m', 'new_m_final_norm_w': 'new_m', 'new_v_ffn_norm_w': 'new_v', 'new_v_ffn_w_gu': 'new_v', 'new_v_ffn_w_down': 'new_v', 'new_v_mod_w': 'new_v', 'new_v_mod_b': 'new_v', 'new_v_mix_norm_w': 'new_v', 'new_v_ssm_w_in': 'new_v', 'new_v_ssm_conv_w': 'new_v', 'new_v_ssm_conv_b': 'new_v', 'new_v_ssm_dt_bias': 'new_v', 'new_v_ssm_a_log': 'new_v', 'new_v_ssm_d': 'new_v', 'new_v_ssm_norm_w': 'new_v', 'new_v_ssm_w_out': 'new_v', 'new_v_kv_norm_w': 'new_v', 'new_v_kv_mod_w': 'new_v', 'new_v_kv_mod_b': 'new_v', 'new_v_w_kv': 'new_v', 'new_v_b_kv': 'new_v', 'new_v_attn_w_q': 'new_v', 'new_v_attn_b_q': 'new_v', 'new_v_attn_sinks': 'new_v', 'new_v_attn_w_o': 'new_v', 'new_v_attn_b_o': 'new_v', 'new_v_final_norm_w': 'new_v'}


def _forward(args):
    return _fwd_reference(*[args[k] for k in FWD_PARAMS])


def _output_shape():
    out = _jax.eval_shape(lambda: _forward(_fwd_setup_inputs(0)))
    return out.shape, out.dtype

N_MICROBATCH = 1
ADAM_LR = 0.001
ADAM_B1 = 0.9
ADAM_B2 = 0.999
ADAM_EPS = 1e-08
ADAM_WD = 0.01
ADAM_STEP = 10
PER_EXAMPLE_BATCH_AXIS = {'x': 0, 'c': 0, 'loss_target': 0}
SHARED_INPUTS = []
_WEIGHT_DTYPES = {'ffn_norm_w': _jnp.float32, 'ffn_w_gu': _jnp.float32, 'ffn_w_down': _jnp.float32, 'mod_w': _jnp.float32, 'mod_b': _jnp.float32, 'mix_norm_w': _jnp.float32, 'ssm_w_in': _jnp.float32, 'ssm_conv_w': _jnp.float32, 'ssm_conv_b': _jnp.float32, 'ssm_dt_bias': _jnp.float32, 'ssm_a_log': _jnp.float32, 'ssm_d': _jnp.float32, 'ssm_norm_w': _jnp.float32, 'ssm_w_out': _jnp.float32, 'kv_norm_w': _jnp.float32, 'kv_mod_w': _jnp.float32, 'kv_mod_b': _jnp.float32, 'w_kv': _jnp.float32, 'b_kv': _jnp.float32, 'attn_w_q': _jnp.float32, 'attn_b_q': _jnp.float32, 'attn_sinks': _jnp.float32, 'attn_w_o': _jnp.float32, 'attn_b_o': _jnp.float32, 'final_norm_w': _jnp.float32}
MOMENT_SCALE = {'ffn_norm_w': 1.851340e-02, 'ffn_w_gu': 8.164225e-03, 'ffn_w_down': 1.332157e-02, 'mod_w': 2.805185e-02, 'mod_b': 4.570695e-02, 'mix_norm_w': 3.953518e-02, 'ssm_w_in': 2.371902e-02, 'ssm_conv_w': 2.054200e-02, 'ssm_conv_b': 2.465526e-02, 'ssm_dt_bias': 6.146926e-02, 'ssm_a_log': 7.961834e-02, 'ssm_d': 1.185990e-01, 'ssm_norm_w': 2.743125e-02, 'ssm_w_out': 3.833606e-02, 'kv_norm_w': 1.672498e-02, 'kv_mod_w': 2.536056e-02, 'kv_mod_b': 4.699774e-02, 'w_kv': 3.068884e-02, 'b_kv': 9.487309e-02, 'attn_w_q': 7.702242e-03, 'attn_b_q': 7.185514e-03, 'attn_sinks': 5.700992e-03, 'attn_w_o': 1.367754e-02, 'attn_b_o': 3.648657e-02, 'final_norm_w': 1.604681e+01}


def _to_microbatches(a, axis):
    t = _jnp.moveaxis(a, axis, 0)
    t = t.reshape((N_MICROBATCH, t.shape[0] // N_MICROBATCH) + t.shape[1:])
    return _jnp.moveaxis(t, 1, axis + 1)


def setup_inputs(seed: int = 0) -> dict:
    inp = _fwd_setup_inputs(seed)
    key = _jax.random.fold_in(_jax.random.key(seed), 7919)
    shape, _ = _output_shape()
    out = dict(inp)
    out["loss_target"] = _jax.random.normal(_jax.random.fold_in(key, 0), shape, _jnp.float32)
    for i, name in enumerate(TWIN_WEIGHTS):
        w = inp[name].astype(_jnp.float32)
        if MOMENT_SCALE is None:
            s = _jnp.sqrt(_jnp.mean(_jnp.square(w)) + 1e-30)
        else:
            s = MOMENT_SCALE[name]
        km, kv = _jax.random.split(_jax.random.fold_in(key, i + 1))
        out[name] = w
        out["m_" + name] = s * _jax.random.normal(km, w.shape, _jnp.float32)
        out["v_" + name] = (s * s) * _jax.random.uniform(kv, w.shape, _jnp.float32, 0.5, 1.5)
    if N_MICROBATCH > 1:
        for name, axis in PER_EXAMPLE_BATCH_AXIS.items():
            out[name] = _to_microbatches(out[name], axis)
    return {'x': out['x'], 'c': out['c'], 'ffn_norm_w': out['ffn_norm_w'], 'ffn_w_gu': out['ffn_w_gu'], 'ffn_w_down': out['ffn_w_down'], 'mod_w': out['mod_w'], 'mod_b': out['mod_b'], 'mix_norm_w': out['mix_norm_w'], 'ssm_w_in': out['ssm_w_in'], 'ssm_conv_w': out['ssm_conv_w'], 'ssm_conv_b': out['ssm_conv_b'], 'ssm_dt_bias': out['ssm_dt_bias'], 'ssm_a_log': out['ssm_a_log'], 'ssm_d': out['ssm_d'], 'ssm_norm_w': out['ssm_norm_w'], 'ssm_w_out': out['ssm_w_out'], 'kv_norm_w': out['kv_norm_w'], 'kv_mod_w': out['kv_mod_w'], 'kv_mod_b': out['kv_mod_b'], 'w_kv': out['w_kv'], 'b_kv': out['b_kv'], 'attn_w_q': out['attn_w_q'], 'attn_b_q': out['attn_b_q'], 'attn_sinks': out['attn_sinks'], 'attn_w_o': out['attn_w_o'], 'attn_b_o': out['attn_b_o'], 'final_norm_w': out['final_norm_w'], 'loss_target': out['loss_target'], 'm_ffn_norm_w': out['m_ffn_norm_w'], 'm_ffn_w_gu': out['m_ffn_w_gu'], 'm_ffn_w_down': out['m_ffn_w_down'], 'm_mod_w': out['m_mod_w'], 'm_mod_b': out['m_mod_b'], 'm_mix_norm_w': out['m_mix_norm_w'], 'm_ssm_w_in': out['m_ssm_w_in'], 'm_ssm_conv_w': out['m_ssm_conv_w'], 'm_ssm_conv_b': out['m_ssm_conv_b'], 'm_ssm_dt_bias': out['m_ssm_dt_bias'], 'm_ssm_a_log': out['m_ssm_a_log'], 'm_ssm_d': out['m_ssm_d'], 'm_ssm_norm_w': out['m_ssm_norm_w'], 'm_ssm_w_out': out['m_ssm_w_out'], 'm_kv_norm_w': out['m_kv_norm_w'], 'm_kv_mod_w': out['m_kv_mod_w'], 'm_kv_mod_b': out['m_kv_mod_b'], 'm_w_kv': out['m_w_kv'], 'm_b_kv': out['m_b_kv'], 'm_attn_w_q': out['m_attn_w_q'], 'm_attn_b_q': out['m_attn_b_q'], 'm_attn_sinks': out['m_attn_sinks'], 'm_attn_w_o': out['m_attn_w_o'], 'm_attn_b_o': out['m_attn_b_o'], 'm_final_norm_w': out['m_final_norm_w'], 'v_ffn_norm_w': out['v_ffn_norm_w'], 'v_ffn_w_gu': out['v_ffn_w_gu'], 'v_ffn_w_down': out['v_ffn_w_down'], 'v_mod_w': out['v_mod_w'], 'v_mod_b': out['v_mod_b'], 'v_mix_norm_w': out['v_mix_norm_w'], 'v_ssm_w_in': out['v_ssm_w_in'], 'v_ssm_conv_w': out['v_ssm_conv_w'], 'v_ssm_conv_b': out['v_ssm_conv_b'], 'v_ssm_dt_bias': out['v_ssm_dt_bias'], 'v_ssm_a_log': out['v_ssm_a_log'], 'v_ssm_d': out['v_ssm_d'], 'v_ssm_norm_w': out['v_ssm_norm_w'], 'v_ssm_w_out': out['v_ssm_w_out'], 'v_kv_norm_w': out['v_kv_norm_w'], 'v_kv_mod_w': out['v_kv_mod_w'], 'v_kv_mod_b': out['v_kv_mod_b'], 'v_w_kv': out['v_w_kv'], 'v_b_kv': out['v_b_kv'], 'v_attn_w_q': out['v_attn_w_q'], 'v_attn_b_q': out['v_attn_b_q'], 'v_attn_sinks': out['v_attn_sinks'], 'v_attn_w_o': out['v_attn_w_o'], 'v_attn_b_o': out['v_attn_b_o'], 'v_final_norm_w': out['v_final_norm_w']}


def _loss(weights, diff, rest, loss_target):
    with _jax.named_scope("forward"):
        args = {**rest, TWIN_DIFF_INPUT: diff, **{k: w.astype(_WEIGHT_DTYPES[k]) for k, w in weights.items()}}
        y = _forward(args)
    with _jax.named_scope("loss_head"):
        err = _jnp.square(y.astype(_jnp.float32) - loss_target)
        return 0.5 * _jnp.sum(_jnp.mean(err, axis=-1)) if err.ndim else 0.5 * err


def _adamw(w, g, m, v):
    m = ADAM_B1 * m + (1.0 - ADAM_B1) * g
    v = ADAM_B2 * v + (1.0 - ADAM_B2) * _jnp.square(g)
    m_hat = m / (1.0 - ADAM_B1 ** ADAM_STEP)
    v_hat = v / (1.0 - ADAM_B2 ** ADAM_STEP)
    delta = -ADAM_LR * (m_hat / (_jnp.sqrt(v_hat) + ADAM_EPS) + ADAM_WD * w)
    return delta, m, v


def reference(x, c, ffn_norm_w, ffn_w_gu, ffn_w_down, mod_w, mod_b, mix_norm_w, ssm_w_in, ssm_conv_w, ssm_conv_b, ssm_dt_bias, ssm_a_log, ssm_d, ssm_norm_w, ssm_w_out, kv_norm_w, kv_mod_w, kv_mod_b, w_kv, b_kv, attn_w_q, attn_b_q, attn_sinks, attn_w_o, attn_b_o, final_norm_w, loss_target, m_ffn_norm_w, m_ffn_w_gu, m_ffn_w_down, m_mod_w, m_mod_b, m_mix_norm_w, m_ssm_w_in, m_ssm_conv_w, m_ssm_conv_b, m_ssm_dt_bias, m_ssm_a_log, m_ssm_d, m_ssm_norm_w, m_ssm_w_out, m_kv_norm_w, m_kv_mod_w, m_kv_mod_b, m_w_kv, m_b_kv, m_attn_w_q, m_attn_b_q, m_attn_sinks, m_attn_w_o, m_attn_b_o, m_final_norm_w, v_ffn_norm_w, v_ffn_w_gu, v_ffn_w_down, v_mod_w, v_mod_b, v_mix_norm_w, v_ssm_w_in, v_ssm_conv_w, v_ssm_conv_b, v_ssm_dt_bias, v_ssm_a_log, v_ssm_d, v_ssm_norm_w, v_ssm_w_out, v_kv_norm_w, v_kv_mod_w, v_kv_mod_b, v_w_kv, v_b_kv, v_attn_w_q, v_attn_b_q, v_attn_sinks, v_attn_w_o, v_attn_b_o, v_final_norm_w):
    given = dict(x=x, c=c, ffn_norm_w=ffn_norm_w, ffn_w_gu=ffn_w_gu, ffn_w_down=ffn_w_down, mod_w=mod_w, mod_b=mod_b, mix_norm_w=mix_norm_w, ssm_w_in=ssm_w_in, ssm_conv_w=ssm_conv_w, ssm_conv_b=ssm_conv_b, ssm_dt_bias=ssm_dt_bias, ssm_a_log=ssm_a_log, ssm_d=ssm_d, ssm_norm_w=ssm_norm_w, ssm_w_out=ssm_w_out, kv_norm_w=kv_norm_w, kv_mod_w=kv_mod_w, kv_mod_b=kv_mod_b, w_kv=w_kv, b_kv=b_kv, attn_w_q=attn_w_q, attn_b_q=attn_b_q, attn_sinks=attn_sinks, attn_w_o=attn_w_o, attn_b_o=attn_b_o, final_norm_w=final_norm_w, loss_target=loss_target, m_ffn_norm_w=m_ffn_norm_w, m_ffn_w_gu=m_ffn_w_gu, m_ffn_w_down=m_ffn_w_down, m_mod_w=m_mod_w, m_mod_b=m_mod_b, m_mix_norm_w=m_mix_norm_w, m_ssm_w_in=m_ssm_w_in, m_ssm_conv_w=m_ssm_conv_w, m_ssm_conv_b=m_ssm_conv_b, m_ssm_dt_bias=m_ssm_dt_bias, m_ssm_a_log=m_ssm_a_log, m_ssm_d=m_ssm_d, m_ssm_norm_w=m_ssm_norm_w, m_ssm_w_out=m_ssm_w_out, m_kv_norm_w=m_kv_norm_w, m_kv_mod_w=m_kv_mod_w, m_kv_mod_b=m_kv_mod_b, m_w_kv=m_w_kv, m_b_kv=m_b_kv, m_attn_w_q=m_attn_w_q, m_attn_b_q=m_attn_b_q, m_attn_sinks=m_attn_sinks, m_attn_w_o=m_attn_w_o, m_attn_b_o=m_attn_b_o, m_final_norm_w=m_final_norm_w, v_ffn_norm_w=v_ffn_norm_w, v_ffn_w_gu=v_ffn_w_gu, v_ffn_w_down=v_ffn_w_down, v_mod_w=v_mod_w, v_mod_b=v_mod_b, v_mix_norm_w=v_mix_norm_w, v_ssm_w_in=v_ssm_w_in, v_ssm_conv_w=v_ssm_conv_w, v_ssm_conv_b=v_ssm_conv_b, v_ssm_dt_bias=v_ssm_dt_bias, v_ssm_a_log=v_ssm_a_log, v_ssm_d=v_ssm_d, v_ssm_norm_w=v_ssm_norm_w, v_ssm_w_out=v_ssm_w_out, v_kv_norm_w=v_kv_norm_w, v_kv_mod_w=v_kv_mod_w, v_kv_mod_b=v_kv_mod_b, v_w_kv=v_w_kv, v_b_kv=v_b_kv, v_attn_w_q=v_attn_w_q, v_attn_b_q=v_attn_b_q, v_attn_sinks=v_attn_sinks, v_attn_w_o=v_attn_w_o, v_attn_b_o=v_attn_b_o, v_final_norm_w=v_final_norm_w)
    weights = {n: given[n] for n in TWIN_WEIGHTS}
    shared = {n: given[n] for n in SHARED_INPUTS}
    per_example = {n: given[n] for n in ['x', 'c']}
    grad_fn = _jax.value_and_grad(_loss, argnums=(0, 1))

    def one_microbatch(ex, loss_target):
        ex = dict(ex)
        diff = ex.pop(TWIN_DIFF_INPUT)
        return grad_fn(weights, diff, {**shared, **ex}, loss_target)

    if N_MICROBATCH == 1:
        loss, (grad_w, grad_x) = one_microbatch(per_example, given["loss_target"])
    else:
        def body(carry, xs):
            loss_sum, grad_sum = carry
            l_k, (gw_k, gx_k) = one_microbatch(xs[0], xs[1])
            with _jax.named_scope("update"):
                return (loss_sum + l_k, _jax.tree.map(_jnp.add, grad_sum, gw_k)), gx_k

        init = (_jnp.zeros((), _jnp.float32), _jax.tree.map(_jnp.zeros_like, weights))
        (loss, grad_w), grad_x = _jax.lax.scan(body, init, (per_example, given["loss_target"]))
    with _jax.named_scope("update"):
        delta_w, new_m, new_v = {}, {}, {}
        for n in TWIN_WEIGHTS:
            delta_w[n], new_m[n], new_v[n] = _adamw(weights[n], grad_w[n], given["m_" + n], given["v_" + n])
    return (loss, grad_x, *[grad_w[n] for n in TWIN_WEIGHTS], *[delta_w[n] for n in TWIN_WEIGHTS],
            *[new_m[n] for n in TWIN_WEIGHTS], *[new_v[n] for n in TWIN_WEIGHTS])
```

```python
import functools

import jax
import jax.numpy as jnp
from jax import lax
from jax.experimental import pallas as pl
from jax.experimental.pallas import tpu as pltpu

F32 = jnp.float32
BF = jnp.bfloat16
MESH = pl.DeviceIdType.MESH

N_DEV = 8
D_MODEL = 1024
DEPTH = 4
N_A = 2
EPS = 1e-5
N_MOD = 9
D_FF = 2816
D_INNER = 2048
SSM_HEADDIM = 64
SSM_HEADS = 32
SSM_GROUPS = 8
SSM_STATE = 128
CONV_WIDTH = 4
CHUNK = 128
CONV_DIM = D_INNER + 2 * SSM_GROUPS * SSM_STATE
IN_PROJ = D_INNER + CONV_DIM + SSM_HEADS
IN_PROJ_PAD = D_INNER + CONV_DIM + 128
ATT_HEADS = 16
KV_HEADS = 4
HEAD_DIM = 64
WINDOW = 128
KV_DIM = 2 * KV_HEADS * HEAD_DIM

ADAM_LR = 0.001
ADAM_B1 = 0.9
ADAM_B2 = 0.999
ADAM_EPS = 1e-08
ADAM_WD = 0.01
ADAM_STEP = 10

VMEM_LIMIT = 48 * 2 ** 20
ADAMW_VMEM_BUDGET = 24 * 2 ** 20
NEG = -1e30


def _call(body, name, grid, in_specs, out_specs, out_shape, scratch=()):
    return pl.pallas_call(
        body, name=name, grid=grid, in_specs=in_specs, out_specs=out_specs, out_shape=out_shape,
        scratch_shapes=list(scratch),
        compiler_params=pltpu.CompilerParams(vmem_limit_bytes=VMEM_LIMIT))


def _tile(n, cap):
    t = (cap // 128) * 128
    while t >= 128:
        if n % t == 0:
            return t
        t -= 128
    return n


def _sds(shape, dtype):
    return jax.ShapeDtypeStruct(shape, dtype)


def _sigmoid(v):
    return 1.0 / (1.0 + jnp.exp(-v))


def _dot(a, b, dims):
    return lax.dot_general(a, b, (dims, ((), ())), preferred_element_type=F32)


def _dot_nn(a, b):
    return _dot(a.astype(BF), b.astype(BF), ((1,), (0,)))


def _dot_nt(a, b):
    return _dot(a.astype(BF), b.astype(BF), ((1,), (1,)))


def _dot_tn(a, b):
    return _dot(a.astype(BF), b.astype(BF), ((0,), (0,)))


def matmul(a, b, mode, out_dtype, name, bias=None, res=None, gate=None, coef=1.0):
    if mode == "nn":
        (M, K), (_, N) = a.shape, b.shape
    elif mode == "nt":
        (M, K), (N, _) = a.shape, b.shape
    else:
        (K, M), (_, N) = a.shape, b.shape
    cap_n = 512 if K > 4096 else 1024
    tm = _tile(M, 512)
    tn = _tile(N, cap_n)
    if mode == "nn":
        a_spec = pl.BlockSpec((tm, K), lambda i, j: (i, 0))
        b_spec = pl.BlockSpec((K, tn), lambda i, j: (0, j))
        fn = _dot_nn
    elif mode == "nt":
        a_spec = pl.BlockSpec((tm, K), lambda i, j: (i, 0))
        b_spec = pl.BlockSpec((tn, K), lambda i, j: (j, 0))
        fn = _dot_nt
    else:
        a_spec = pl.BlockSpec((K, tm), lambda i, j: (0, i))
        b_spec = pl.BlockSpec((K, tn), lambda i, j: (0, j))
        fn = _dot_tn
    has_bias, has_res = bias is not None, res is not None
    o_spec = pl.BlockSpec((tm, tn), lambda i, j: (i, j))
    v_spec = pl.BlockSpec((1, tn), lambda i, j: (0, j))
    in_specs, args = [a_spec, b_spec], [a, b]
    if has_bias:
        in_specs.append(v_spec)
        args.append(bias)
    if has_res:
        in_specs += [o_spec, v_spec]
        args += [res, gate]

    def body(*refs):
        a_ref, b_ref = refs[0], refs[1]
        k = 2
        y = fn(a_ref[...], b_ref[...])
        if has_bias:
            y = y + refs[k][...]
            k += 1
        if has_res:
            res_ref, gate_ref = refs[k], refs[k + 1]
            refs[k + 2][...] = y.astype(out_dtype)
            refs[k + 3][...] = res_ref[...] + coef * gate_ref[...] * y
        else:
            refs[k][...] = y.astype(out_dtype)

    if has_res:
        out_shape = (_sds((M, N), out_dtype), _sds((M, N), F32))
        out_specs = (o_spec, o_spec)
    else:
        out_shape = _sds((M, N), out_dtype)
        out_specs = o_spec
    return _call(body, name, (M // tm, N // tn), in_specs, out_specs, out_shape)(*args)


def norm_mod(x, nw, sh, sc, name):
    L, D = x.shape
    tm = _tile(L, 512)

    def body(x_ref, nw_ref, sh_ref, sc_ref, h_ref):
        xf = x_ref[...]
        r = lax.rsqrt(jnp.mean(xf * xf, axis=-1, keepdims=True) + EPS)
        n = xf * r * nw_ref[...]
        h_ref[...] = (n * (1.0 + sc_ref[...]) + sh_ref[...]).astype(BF)

    row = pl.BlockSpec((tm, D), lambda i: (i, 0))
    vec = pl.BlockSpec((1, D), lambda i: (0, 0))
    return _call(body, name, (L // tm,), [row, vec, vec, vec], row, _sds((L, D), BF))(x, nw, sh, sc)


def norm_mod_bwd(x, dh, dres, nw, sc, name):
    L, D = x.shape
    tm = _tile(L, 512)

    def body(x_ref, dh_ref, dres_ref, nw_ref, sc_ref, dx_ref, acc_ref):
        @pl.when(pl.program_id(0) == 0)
        def _():
            acc_ref[...] = jnp.zeros_like(acc_ref)

        xf = x_ref[...]
        dhf = dh_ref[...].astype(F32)
        r = lax.rsqrt(jnp.mean(xf * xf, axis=-1, keepdims=True) + EPS)
        xhat = xf * r
        nwv = nw_ref[...]
        dn = dhf * (1.0 + sc_ref[...])
        dxhat = dn * nwv
        proj = jnp.mean(dxhat * xhat, axis=-1, keepdims=True)
        dx_ref[...] = dres_ref[...] + r * (dxhat - xhat * proj)
        acc_ref[0:1, :] += jnp.sum(dhf, axis=0, keepdims=True)
        acc_ref[1:2, :] += jnp.sum(dhf * xhat * nwv, axis=0, keepdims=True)
        acc_ref[2:3, :] += jnp.sum(dn * xhat, axis=0, keepdims=True)

    row = pl.BlockSpec((tm, D), lambda i: (i, 0))
    vec = pl.BlockSpec((1, D), lambda i: (0, 0))
    acc = pl.BlockSpec((8, D), lambda i: (0, 0))
    return _call(body, name, (L // tm,), [row, row, row, vec, vec], (row, acc),
                 (_sds((L, D), F32), _sds((8, D), F32)))(x, dh, dres, nw, sc)


def final_loss(x, nw, target):
    L, D = x.shape
    tm = _tile(L, 512)

    def body(x_ref, nw_ref, t_ref, dx_ref, acc_ref):
        @pl.when(pl.program_id(0) == 0)
        def _():
            acc_ref[...] = jnp.zeros_like(acc_ref)

        xf = x_ref[...]
        r = lax.rsqrt(jnp.mean(xf * xf, axis=-1, keepdims=True) + EPS)
        xhat = xf * r
        nwv = nw_ref[...]
        err = xhat * nwv - t_ref[...]
        dy = err * (1.0 / D)
        dxhat = dy * nwv
        proj = jnp.mean(dxhat * xhat, axis=-1, keepdims=True)
        dx_ref[...] = r * (dxhat - xhat * proj)
        acc_ref[0:1, :] += jnp.sum(dy * xhat, axis=0, keepdims=True)
        acc_ref[1:2, :] += jnp.sum(err * err, axis=0, keepdims=True)

    row = pl.BlockSpec((tm, D), lambda i: (i, 0))
    vec = pl.BlockSpec((1, D), lambda i: (0, 0))
    acc = pl.BlockSpec((8, D), lambda i: (0, 0))
    return _call(body, "final_loss", (L // tm,), [row, vec, row], (row, acc),
                 (_sds((L, D), F32), _sds((8, D), F32)))(x, nw, target)


def resid_gate_bwd(dxo, y, gate, coef, name):
    L, D = dxo.shape
    tm = _tile(L, 512)

    def body(dxo_ref, y_ref, g_ref, dy_ref, acc_ref):
        @pl.when(pl.program_id(0) == 0)
        def _():
            acc_ref[...] = jnp.zeros_like(acc_ref)

        d = dxo_ref[...]
        dy = coef * g_ref[...] * d
        dy_ref[...] = dy.astype(BF)
        acc_ref[0:1, :] += coef * jnp.sum(d * y_ref[...].astype(F32), axis=0, keepdims=True)
        acc_ref[1:2, :] += jnp.sum(dy, axis=0, keepdims=True)

    row = pl.BlockSpec((tm, D), lambda i: (i, 0))
    vec = pl.BlockSpec((1, D), lambda i: (0, 0))
    acc = pl.BlockSpec((8, D), lambda i: (0, 0))
    return _call(body, name, (L // tm,), [row, row, vec], (row, acc),
                 (_sds((L, D), BF), _sds((8, D), F32)))(dxo, y, gate)


def swiglu(gu):
    L = gu.shape[0]
    tm, tn = _tile(L, 512), _tile(D_FF, 512)
    nj = D_FF // tn

    def body(g_ref, u_ref, a_ref):
        g = g_ref[...].astype(F32)
        a_ref[...] = (g * _sigmoid(g) * u_ref[...].astype(F32)).astype(BF)

    return _call(body, "swiglu", (L // tm, nj),
                 [pl.BlockSpec((tm, tn), lambda i, j: (i, j)), pl.BlockSpec((tm, tn), lambda i, j: (i, j + nj))],
                 pl.BlockSpec((tm, tn), lambda i, j: (i, j)), _sds((L, D_FF), BF))(gu, gu)


def swiglu_bwd(da, gu):
    L = gu.shape[0]
    tm, tn = _tile(L, 512), _tile(D_FF, 512)
    nj = D_FF // tn

    def body(da_ref, g_ref, u_ref, o_ref):
        g = g_ref[...].astype(F32)
        u = u_ref[...].astype(F32)
        d = da_ref[...].astype(F32)
        s = _sigmoid(g)
        dg = d * u * s * (1.0 + g * (1.0 - s))
        du = d * g * s
        o_ref[...] = jnp.where(pl.program_id(1) < nj, dg, du).astype(BF)

    return _call(body, "swiglu_bwd", (L // tm, 2 * nj),
                 [pl.BlockSpec((tm, tn), lambda i, j: (i, j % nj)),
                  pl.BlockSpec((tm, tn), lambda i, j: (i, j % nj)),
                  pl.BlockSpec((tm, tn), lambda i, j: (i, nj + j % nj))],
                 pl.BlockSpec((tm, tn), lambda i, j: (i, j)), _sds((L, 2 * D_FF), BF))(da, gu, gu)


def _shift_rows(cur, other, k, down):
    n = cur.shape[0]
    rows = lax.broadcasted_iota(jnp.int32, cur.shape, 0)
    if down:
        return jnp.where(rows < k, pltpu.roll(other, k, 0), pltpu.roll(cur, k, 0))
    return jnp.where(rows >= n - k, pltpu.roll(other, n - k, 0), pltpu.roll(cur, n - k, 0))


def _conv_pre(cur, prev, w_ref, b_ref):
    s = cur * w_ref[CONV_WIDTH - 1:CONV_WIDTH, :] + b_ref[...]
    for k in range(1, CONV_WIDTH):
        s = s + _shift_rows(cur, prev, k, True) * w_ref[CONV_WIDTH - 1 - k:CONV_WIDTH - k, :]
    return s


_XBC_COL0 = D_INNER // 512


def conv_fwd(zx, w, b):
    L = zx.shape[0]
    tm, tc = _tile(L, 256), 512

    def body(cur_ref, prev_ref, w_ref, b_ref, o_ref):
        cur = cur_ref[...]
        prev = jnp.where(pl.program_id(1) > 0, prev_ref[...], 0.0)
        s = _conv_pre(cur, prev, w_ref, b_ref)
        o_ref[...] = s * _sigmoid(s)

    return _call(body, "conv_fwd", (CONV_DIM // tc, L // tm),
                 [pl.BlockSpec((tm, tc), lambda j, i: (i, _XBC_COL0 + j)),
                  pl.BlockSpec((tm, tc), lambda j, i: (jnp.maximum(i - 1, 0), _XBC_COL0 + j)),
                  pl.BlockSpec((CONV_WIDTH, tc), lambda j, i: (0, j)),
                  pl.BlockSpec((1, tc), lambda j, i: (0, j))],
                 pl.BlockSpec((tm, tc), lambda j, i: (i, j)), _sds((L, CONV_DIM), F32))(zx, zx, w, b)


def conv_bwd_act(dxc, zx, w, b):
    L = zx.shape[0]
    tm, tc = _tile(L, 256), 512

    def body(d_ref, cur_ref, prev_ref, w_ref, b_ref, o_ref):
        cur = cur_ref[...]
        prev = jnp.where(pl.program_id(1) > 0, prev_ref[...], 0.0)
        s = _conv_pre(cur, prev, w_ref, b_ref)
        sg = _sigmoid(s)
        o_ref[...] = d_ref[...] * sg * (1.0 + s * (1.0 - sg))

    return _call(body, "conv_bwd_act", (CONV_DIM // tc, L // tm),
                 [pl.BlockSpec((tm, tc), lambda j, i: (i, j)),
                  pl.BlockSpec((tm, tc), lambda j, i: (i, _XBC_COL0 + j)),
                  pl.BlockSpec((tm, tc), lambda j, i: (jnp.maximum(i - 1, 0), _XBC_COL0 + j)),
                  pl.BlockSpec((CONV_WIDTH, tc), lambda j, i: (0, j)),
                  pl.BlockSpec((1, tc), lambda j, i: (0, j))],
                 pl.BlockSpec((tm, tc), lambda j, i: (i, j)), _sds((L, CONV_DIM), F32))(dxc, zx, zx, w, b)


def conv_bwd(ds, zx, w):
    L = zx.shape[0]
    tm, tc = _tile(L, 256), 512
    nblk = L // tm

    def body(ds_ref, dsn_ref, cur_ref, prev_ref, w_ref, du_ref, acc_ref):
        i = pl.program_id(1)

        @pl.when(i == 0)
        def _():
            acc_ref[...] = jnp.zeros_like(acc_ref)

        ds_c = ds_ref[...]
        ds_n = jnp.where(i < nblk - 1, dsn_ref[...], 0.0)
        cur = cur_ref[...]
        prev = jnp.where(i > 0, prev_ref[...], 0.0)
        du = ds_c * w_ref[CONV_WIDTH - 1:CONV_WIDTH, :]
        acc_ref[CONV_WIDTH - 1:CONV_WIDTH, :] += jnp.sum(ds_c * cur, axis=0, keepdims=True)
        for k in range(1, CONV_WIDTH):
            du = du + _shift_rows(ds_c, ds_n, k, False) * w_ref[CONV_WIDTH - 1 - k:CONV_WIDTH - k, :]
            acc_ref[CONV_WIDTH - 1 - k:CONV_WIDTH - k, :] += jnp.sum(
                ds_c * _shift_rows(cur, prev, k, True), axis=0, keepdims=True)
        acc_ref[CONV_WIDTH:CONV_WIDTH + 1, :] += jnp.sum(ds_c, axis=0, keepdims=True)
        du_ref[...] = du.astype(BF)

    return _call(body, "conv_bwd", (CONV_DIM // tc, nblk),
                 [pl.BlockSpec((tm, tc), lambda j, i: (i, j)),
                  pl.BlockSpec((tm, tc), lambda j, i: (jnp.minimum(i + 1, nblk - 1), j)),
                  pl.BlockSpec((tm, tc), lambda j, i: (i, _XBC_COL0 + j)),
                  pl.BlockSpec((tm, tc), lambda j, i: (jnp.maximum(i - 1, 0), _XBC_COL0 + j)),
                  pl.BlockSpec((CONV_WIDTH, tc), lambda j, i: (0, j))],
                 (pl.BlockSpec((tm, tc), lambda j, i: (i, j)), pl.BlockSpec((8, tc), lambda j, i: (0, j))),
                 (_sds((L, CONV_DIM), BF), _sds((8, CONV_DIM), F32)))(ds, ds, zx, zx, w)


_DT_COL = (D_INNER + CONV_DIM) // 128


def dt_prep(zx, bias_pad, alog_pad):
    L = zx.shape[0]

    def body(raw_ref, b_ref, al_ref, dt_ref, acs_ref):
        v = raw_ref[...] + b_ref[...]
        dt = jnp.maximum(v, 0.0) + jnp.log(1.0 + jnp.exp(-jnp.abs(v)))
        dt_ref[...] = dt
        acs = dt * (-jnp.exp(al_ref[...]))
        rows = lax.broadcasted_iota(jnp.int32, acs.shape, 0)
        s = 1
        while s < CHUNK:
            acs = acs + jnp.where(rows >= s, pltpu.roll(acs, s, 0), 0.0)
            s *= 2
        acs_ref[...] = acs

    blk = pl.BlockSpec((CHUNK, 128), lambda i: (i, 0))
    vec = pl.BlockSpec((1, 128), lambda i: (0, 0))
    return _call(body, "dt_prep", (L // CHUNK,),
                 [pl.BlockSpec((CHUNK, 128), lambda i: (i, _DT_COL)), vec, vec], (blk, blk),
                 (_sds((L, 128), F32), _sds((L, 128), F32)))(zx, bias_pad, alog_pad)


def dt_bwd(ddt, da, dt, zx, bias_pad, alog_pad):
    L = zx.shape[0]
    tm = _tile(L, 512)

    def body(ddt_ref, da_ref, dt_ref, raw_ref, b_ref, al_ref, o_ref, acc_ref):
        @pl.when(pl.program_id(0) == 0)
        def _():
            acc_ref[...] = jnp.zeros_like(acc_ref)

        A = -jnp.exp(al_ref[...])
        dav = da_ref[...]
        dd = ddt_ref[...] + dav * A
        draw = dd * _sigmoid(raw_ref[...] + b_ref[...])
        o_ref[...] = draw.astype(BF)
        acc_ref[0:1, :] += jnp.sum(draw, axis=0, keepdims=True)
        acc_ref[1:2, :] += jnp.sum(dav * dt_ref[...], axis=0, keepdims=True) * A

    blk = pl.BlockSpec((tm, 128), lambda i: (i, 0))
    vec = pl.BlockSpec((1, 128), lambda i: (0, 0))
    return _call(body, "dt_bwd", (L // tm,),
                 [blk, blk, blk, pl.BlockSpec((tm, 128), lambda i: (i, _DT_COL)), vec, vec],
                 (blk, pl.BlockSpec((8, 128), lambda i: (0, 0))),
                 (_sds((L, 128), BF), _sds((8, 128), F32)))(ddt, da, dt, zx, bias_pad, alog_pad)


_HPG = SSM_HEADS // SSM_GROUPS
_GW = _HPG * SSM_HEADDIM
_B_COL0 = D_INNER // SSM_STATE
_C_COL0 = (D_INNER + SSM_GROUPS * SSM_STATE) // SSM_STATE


def _ssd_head(x, dtc, ac, ar, r, causal):
    xh = x[:, SSM_HEADDIM * r:SSM_HEADDIM * (r + 1)]
    acol = ac[:, r:r + 1]
    arow = ar[r:r + 1, :]
    alast = ar[r:r + 1, CHUNK - 1:CHUNK]
    lm = jnp.exp(jnp.where(causal, acol - arow, NEG))
    return xh, xh * dtc[:, r:r + 1], acol, alast, lm


def ssd_fwd(xc, dt_g, acs_g, acsT_g, d_exp):
    L = xc.shape[0]
    nc = L // CHUNK

    def body(x_ref, b_ref, c_ref, dt_ref, ac_ref, ar_ref, d_ref, y_ref, pst_ref, st_ref):
        @pl.when(pl.program_id(1) == 0)
        def _():
            st_ref[...] = jnp.zeros_like(st_ref)

        x, Bm, Cm = x_ref[...], b_ref[...], c_ref[...]
        dtc, ac, ar = dt_ref[...], ac_ref[...], ar_ref[...]
        causal = lax.broadcasted_iota(jnp.int32, (CHUNK, CHUNK), 0) >= lax.broadcasted_iota(jnp.int32, (CHUNK, CHUNK), 1)
        CB = _dot_nt(Cm, Bm)
        for r in range(_HPG):
            xh, xd, acol, alast, lm = _ssd_head(x, dtc, ac, ar, r, causal)
            P = st_ref[r]
            y = _dot_nn(CB * lm, xd) + jnp.exp(acol) * _dot_nt(Cm, P)
            y_ref[:, SSM_HEADDIM * r:SSM_HEADDIM * (r + 1)] = y + d_ref[:, SSM_HEADDIM * r:SSM_HEADDIM * (r + 1)] * xh
            pst_ref[r] = P
            st_ref[r] = P * jnp.exp(alast) + _dot_tn(xd * jnp.exp(alast - acol), Bm)

    return _call(
        body, "ssd_fwd", (SSM_GROUPS, nc),
        [pl.BlockSpec((CHUNK, _GW), lambda g, c: (c, g)),
         pl.BlockSpec((CHUNK, SSM_STATE), lambda g, c: (c, _B_COL0 + g)),
         pl.BlockSpec((CHUNK, SSM_STATE), lambda g, c: (c, _C_COL0 + g)),
         pl.BlockSpec((None, CHUNK, _HPG), lambda g, c: (g, c, 0)),
         pl.BlockSpec((None, CHUNK, _HPG), lambda g, c: (g, c, 0)),
         pl.BlockSpec((None, _HPG, CHUNK), lambda g, c: (g, 0, c)),
         pl.BlockSpec((None, 1, _GW), lambda g, c: (g, 0, 0))],
        (pl.BlockSpec((CHUNK, _GW), lambda g, c: (c, g)),
         pl.BlockSpec((None, None, _HPG, SSM_HEADDIM, SSM_STATE), lambda g, c: (c, g, 0, 0, 0))),
        (_sds((L, D_INNER), F32), _sds((nc, SSM_GROUPS, _HPG, SSM_HEADDIM, SSM_STATE), F32)),
        scratch=[pltpu.VMEM((_HPG, SSM_HEADDIM, SSM_STATE), F32)],
    )(xc, xc, xc, dt_g, acs_g, acsT_g, d_exp)


def ssd_bwd(dy, xc, dt_g, acs_g, acsT_g, pst, d_exp):
    L = xc.shape[0]
    nc = L // CHUNK

    def body(dy_ref, x_ref, b_ref, c_ref, dt_ref, ac_ref, ar_ref, pst_ref, d_ref,
             dx_ref, db_ref, dc_ref, ddt_ref, da_ref, dd_ref, dp_ref):
        @pl.when(pl.program_id(1) == 0)
        def _():
            dp_ref[...] = jnp.zeros_like(dp_ref)
            dd_ref[...] = jnp.zeros_like(dd_ref)

        dyv, x, Bm, Cm = dy_ref[...], x_ref[...], b_ref[...], c_ref[...]
        dtc, ac, ar = dt_ref[...], ac_ref[...], ar_ref[...]
        ri = lax.broadcasted_iota(jnp.int32, (CHUNK, CHUNK), 0)
        ci = lax.broadcasted_iota(jnp.int32, (CHUNK, CHUNK), 1)
        causal = ri >= ci
        lane4 = lax.broadcasted_iota(jnp.int32, (CHUNK, _HPG), 1)
        CB = _dot_nt(Cm, Bm)
        dB = jnp.zeros((CHUNK, SSM_STATE), F32)
        dC = jnp.zeros((CHUNK, SSM_STATE), F32)
        dCB = jnp.zeros((CHUNK, CHUNK), F32)
        ddt_blk = jnp.zeros((CHUNK, _HPG), F32)
        da_blk = jnp.zeros((CHUNK, _HPG), F32)
        for r in range(_HPG):
            sl = slice(SSM_HEADDIM * r, SSM_HEADDIM * (r + 1))
            xh, xd, acol, alast, lm = _ssd_head(x, dtc, ac, ar, r, causal)
            dyh = dyv[:, sl]
            P = pst_ref[r]
            dPn = dp_ref[r]
            eA = jnp.exp(acol)
            cd = jnp.exp(alast)
            dte = jnp.exp(alast - acol)
            G = CB * lm
            Z = _dot_nt(Cm, P)
            dZ = eA * dyh
            dC = dC + _dot_nn(dZ, P)
            dp_ref[r] = dPn * cd + _dot_tn(dZ, Cm)
            dA_col = jnp.sum(dZ * Z, axis=1, keepdims=True)
            BdS = _dot_nt(Bm, dPn)
            dxd = dte * BdS
            dB = dB + dte * _dot_nn(xd, dPn)
            t = jnp.sum(xd * BdS, axis=1, keepdims=True) * dte
            dA_col = dA_col - t
            dA_last = jnp.sum(t, axis=0, keepdims=True) + jnp.sum(
                jnp.sum(dPn * P, axis=1, keepdims=True), axis=0, keepdims=True) * cd
            dG = _dot_nt(dyh, xd)
            dxd = dxd + _dot_tn(G, dyh)
            dCB = dCB + dG * lm
            W = dG * G
            dA_col = dA_col + jnp.sum(W, axis=1, keepdims=True)
            dA_row = jnp.sum(jnp.where(ri == ci, dA_col, 0.0), axis=0, keepdims=True) - jnp.sum(W, axis=0, keepdims=True)
            da_col = jnp.sum(jnp.where(ci >= ri, dA_row, 0.0), axis=1, keepdims=True) + dA_last
            da_blk = jnp.where(lane4 == r, da_col, da_blk)
            ddt_blk = jnp.where(lane4 == r, jnp.sum(dxd * xh, axis=1, keepdims=True), ddt_blk)
            dx_ref[:, sl] = dxd * dtc[:, r:r + 1] + d_ref[:, sl] * dyh
        dc_ref[...] = dC + _dot_nn(dCB, Bm)
        db_ref[...] = dB + _dot_tn(dCB, Cm)
        ddt_ref[...] = ddt_blk
        da_ref[...] = da_blk
        dd_ref[...] += jnp.sum(dyv * x, axis=0, keepdims=True)

    rc = lambda g, c: (nc - 1 - c, g)
    small = pl.BlockSpec((None, CHUNK, _HPG), lambda g, c: (g, nc - 1 - c, 0))
    return _call(
        body, "ssd_bwd", (SSM_GROUPS, nc),
        [pl.BlockSpec((CHUNK, _GW), rc),
         pl.BlockSpec((CHUNK, _GW), rc),
         pl.BlockSpec((CHUNK, SSM_STATE), lambda g, c: (nc - 1 - c, _B_COL0 + g)),
         pl.BlockSpec((CHUNK, SSM_STATE), lambda g, c: (nc - 1 - c, _C_COL0 + g)),
         small, small,
         pl.BlockSpec((None, _HPG, CHUNK), lambda g, c: (g, 0, nc - 1 - c)),
         pl.BlockSpec((None, None, _HPG, SSM_HEADDIM, SSM_STATE), lambda g, c: (nc - 1 - c, g, 0, 0, 0)),
         pl.BlockSpec((None, 1, _GW), lambda g, c: (g, 0, 0))],
        (pl.BlockSpec((CHUNK, _GW), rc),
         pl.BlockSpec((CHUNK, SSM_STATE), rc),
         pl.BlockSpec((CHUNK, SSM_STATE), rc),
         small, small,
         pl.BlockSpec((None, 1, _GW), lambda g, c: (g, 0, 0))),
        (_sds((L, D_INNER), F32), _sds((L, SSM_GROUPS * SSM_STATE), F32), _sds((L, SSM_GROUPS * SSM_STATE), F32),
         _sds((SSM_GROUPS, L, _HPG), F32), _sds((SSM_GROUPS, L, _HPG), F32), _sds((SSM_GROUPS, 1, _GW), F32)),
        scratch=[pltpu.VMEM((_HPG, SSM_HEADDIM, SSM_STATE), F32)],
    )(dy, xc, xc, xc, dt_g, acs_g, acsT_g, pst, d_exp)


_NGW = D_INNER // SSM_GROUPS


def gate_norm(y, zx, nw):
    L = y.shape[0]
    tm = _tile(L, 256)

    def body(y_ref, z_ref, nw_ref, o_ref):
        for g in range(SSM_GROUPS):
            sl = slice(_NGW * g, _NGW * (g + 1))
            z = z_ref[:, sl]
            y2 = y_ref[:, sl] * (z * _sigmoid(z))
            r = lax.rsqrt(jnp.mean(y2 * y2, axis=-1, keepdims=True) + EPS)
            o_ref[:, sl] = (y2 * r * nw_ref[:, sl]).astype(BF)

    row = pl.BlockSpec((tm, D_INNER), lambda i: (i, 0))
    return _call(body, "gate_norm", (L // tm,), [row, row, pl.BlockSpec((1, D_INNER), lambda i: (0, 0))],
                 row, _sds((L, D_INNER), BF))(y, zx, nw)


def gate_norm_bwd(dyn, y, zx, nw):
    L = y.shape[0]
    tm = _tile(L, 256)

    def body(d_ref, y_ref, z_ref, nw_ref, dy_ref, dz_ref, acc_ref):
        @pl.when(pl.program_id(0) == 0)
        def _():
            acc_ref[...] = jnp.zeros_like(acc_ref)

        for g in range(SSM_GROUPS):
            sl = slice(_NGW * g, _NGW * (g + 1))
            z = z_ref[:, sl]
            yv = y_ref[:, sl]
            sg = _sigmoid(z)
            sz = z * sg
            y2 = yv * sz
            r = lax.rsqrt(jnp.mean(y2 * y2, axis=-1, keepdims=True) + EPS)
            yh = y2 * r
            d = d_ref[:, sl]
            dn = d * nw_ref[:, sl]
            dy2 = r * (dn - yh * jnp.mean(dn * yh, axis=-1, keepdims=True))
            dy_ref[:, sl] = dy2 * sz
            dz_ref[:, sl] = (dy2 * yv * sg * (1.0 + z * (1.0 - sg))).astype(BF)
            acc_ref[0:1, sl] += jnp.sum(d * yh, axis=0, keepdims=True)

    row = pl.BlockSpec((tm, D_INNER), lambda i: (i, 0))
    return _call(body, "gate_norm_bwd", (L // tm,), [row, row, row, pl.BlockSpec((1, D_INNER), lambda i: (0, 0))],
                 (row, row, pl.BlockSpec((8, D_INNER), lambda i: (0, 0))),
                 (_sds((L, D_INNER), F32), _sds((L, D_INNER), BF), _sds((8, D_INNER), F32)))(dyn, y, zx, nw)


_SCALE = HEAD_DIM ** -0.5
_REP = ATT_HEADS // KV_HEADS
_V_OFF = KV_HEADS * HEAD_DIM


def _attn_probs(qh, kp, kc, sink, first):
    rows = lax.broadcasted_iota(jnp.int32, (WINDOW, WINDOW), 0)
    cols = lax.broadcasted_iota(jnp.int32, (WINDOW, WINDOW), 1)
    sp = jnp.where(jnp.logical_and(cols > rows, jnp.logical_not(first)), _dot_nt(qh, kp) * _SCALE, NEG)
    sc = jnp.where(cols <= rows, _dot_nt(qh, kc) * _SCALE, NEG)
    m = jnp.maximum(jnp.maximum(jnp.max(sp, axis=1, keepdims=True), jnp.max(sc, axis=1, keepdims=True)), sink)
    pp = jnp.exp(sp - m)
    pc = jnp.exp(sc - m)
    ps = jnp.exp(sink - m)
    inv = 1.0 / (jnp.sum(pp, axis=1, keepdims=True) + jnp.sum(pc, axis=1, keepdims=True) + ps)
    return pp * inv, pc * inv, ps * inv


def attn_fwd(q, kv, sinks_pad):
    L = q.shape[0]
    nb = L // WINDOW

    def body(q_ref, kc_ref, kp_ref, s_ref, o_ref):
        first = pl.program_id(0) == 0
        for h in range(ATT_HEADS):
            k = h // _REP
            ks = slice(HEAD_DIM * k, HEAD_DIM * (k + 1))
            vs = slice(_V_OFF + HEAD_DIM * k, _V_OFF + HEAD_DIM * (k + 1))
            hs = slice(HEAD_DIM * h, HEAD_DIM * (h + 1))
            pp, pc, _ = _attn_probs(q_ref[:, hs], kp_ref[:, ks], kc_ref[:, ks], s_ref[:, h:h + 1], first)
            o_ref[:, hs] = (_dot_nn(pp, kp_ref[:, vs]) + _dot_nn(pc, kc_ref[:, vs])).astype(BF)

    qspec = pl.BlockSpec((WINDOW, D_MODEL), lambda i: (i, 0))
    return _call(body, "attn_fwd", (nb,),
                 [qspec, pl.BlockSpec((WINDOW, KV_DIM), lambda i: (i, 0)),
                  pl.BlockSpec((WINDOW, KV_DIM), lambda i: (jnp.maximum(i - 1, 0), 0)),
                  pl.BlockSpec((1, 128), lambda i: (0, 0))],
                 qspec, _sds((L, D_MODEL), BF))(q, kv, kv, sinks_pad)


def attn_bwd(q, kv, do, sinks_pad):
    L = q.shape[0]
    nb = L // WINDOW

    def body(q_ref, kc_ref, kp_ref, do_ref, s_ref, dq_ref, dc_ref, dp_ref, acc_ref):
        first = pl.program_id(0) == 0

        @pl.when(first)
        def _():
            acc_ref[...] = jnp.zeros_like(acc_ref)

        lane = lax.broadcasted_iota(jnp.int32, (1, 128), 1)
        dsink = jnp.zeros((1, 128), F32)
        for k in range(KV_HEADS):
            ks = slice(HEAD_DIM * k, HEAD_DIM * (k + 1))
            vs = slice(_V_OFF + HEAD_DIM * k, _V_OFF + HEAD_DIM * (k + 1))
            kp, kc, vp, vc = kp_ref[:, ks], kc_ref[:, ks], kp_ref[:, vs], kc_ref[:, vs]
            dkp = jnp.zeros((WINDOW, HEAD_DIM), F32)
            dkc = jnp.zeros((WINDOW, HEAD_DIM), F32)
            dvp = jnp.zeros((WINDOW, HEAD_DIM), F32)
            dvc = jnp.zeros((WINDOW, HEAD_DIM), F32)
            for r in range(_REP):
                h = k * _REP + r
                hs = slice(HEAD_DIM * h, HEAD_DIM * (h + 1))
                qh = q_ref[:, hs]
                doh = do_ref[:, hs]
                pp, pc, ps = _attn_probs(qh, kp, kc, s_ref[:, h:h + 1], first)
                dpp = _dot_nt(doh, vp)
                dpc = _dot_nt(doh, vc)
                delta = jnp.sum(pp * dpp, axis=1, keepdims=True) + jnp.sum(pc * dpc, axis=1, keepdims=True)
                dsp = pp * (dpp - delta) * _SCALE
                dsc = pc * (dpc - delta) * _SCALE
                dq_ref[:, hs] = _dot_nn(dsp, kp) + _dot_nn(dsc, kc)
                dkp = dkp + _dot_tn(dsp, qh)
                dkc = dkc + _dot_tn(dsc, qh)
                dvp = dvp + _dot_tn(pp, doh)
                dvc = dvc + _dot_tn(pc, doh)
                dsink = dsink + jnp.where(lane == h, -jnp.sum(ps * delta, axis=0, keepdims=True), 0.0)
            dp_ref[:, ks] = dkp
            dc_ref[:, ks] = dkc
            dp_ref[:, vs] = dvp
            dc_ref[:, vs] = dvc
        acc_ref[0:1, :] += jnp.sum(dq_ref[...], axis=0, keepdims=True)
        acc_ref[1:2, 0:128] += dsink

    qspec = pl.BlockSpec((WINDOW, D_MODEL), lambda i: (i, 0))
    kspec = pl.BlockSpec((WINDOW, KV_DIM), lambda i: (i, 0))
    return _call(body, "attn_bwd", (nb,),
                 [qspec, kspec, pl.BlockSpec((WINDOW, KV_DIM), lambda i: (jnp.maximum(i - 1, 0), 0)), qspec,
                  pl.BlockSpec((1, 128), lambda i: (0, 0))],
                 (qspec, kspec, kspec, pl.BlockSpec((8, D_MODEL), lambda i: (0, 0))),
                 (_sds((L, D_MODEL), F32), _sds((L, KV_DIM), F32), _sds((L, KV_DIM), F32), _sds((8, D_MODEL), F32)),
                 )(q, kv, kv, do, sinks_pad)


def kv_grad_combine(parts):
    L = parts[0][0].shape[0]
    nb = L // WINDOW
    n = len(parts)

    def body(*refs):
        i = pl.program_id(0)
        o_ref, acc_ref = refs[2 * n], refs[2 * n + 1]

        @pl.when(i == 0)
        def _():
            acc_ref[...] = jnp.zeros_like(acc_ref)

        tot = refs[0][...]
        nxt = refs[1][...]
        for a in range(1, n):
            tot = tot + refs[2 * a][...]
            nxt = nxt + refs[2 * a + 1][...]
        tot = tot + jnp.where(i < nb - 1, nxt, 0.0)
        o_ref[...] = tot
        acc_ref[0:1, :] += jnp.sum(tot, axis=0, keepdims=True)

    cur = pl.BlockSpec((WINDOW, KV_DIM), lambda i: (i, 0))
    nxt = pl.BlockSpec((WINDOW, KV_DIM), lambda i: (jnp.minimum(i + 1, nb - 1), 0))
    args = [t for p in parts for t in p]
    return _call(body, "kv_grad_combine", (nb,), [cur, nxt] * n,
                 (cur, pl.BlockSpec((8, KV_DIM), lambda i: (0, 0))),
                 (_sds((L, KV_DIM), F32), _sds((8, KV_DIM), F32)))(*args)


def mod_fwd(c_all, w, b, name):
    n, _, C = w.shape

    def body(c_ref, w_ref, b_ref, o_ref, ca_ref):
        cv = c_ref[...]
        ca = cv * _sigmoid(cv)
        ca_ref[...] = ca
        o_ref[...] = _dot(ca, w_ref[...], ((1,), (0,))) + b_ref[...]

    return _call(body, name, (n,),
                 [pl.BlockSpec((N_DEV, D_MODEL), lambda i: (0, 0)),
                  pl.BlockSpec((None, D_MODEL, C), lambda i: (i, 0, 0)),
                  pl.BlockSpec((None, 1, C), lambda i: (i, 0, 0))],
                 (pl.BlockSpec((None, N_DEV, C), lambda i: (i, 0, 0)), pl.BlockSpec((N_DEV, D_MODEL), lambda i: (0, 0))),
                 (_sds((n, N_DEV, C), F32), _sds((N_DEV, D_MODEL), F32)))(c_all, w, b)


def mod_wgrad(c_act_t, dmod, name):
    n, _, C = dmod.shape
    tr = 256

    def body(ct_ref, d_ref, o_ref):
        acc = ct_ref[:, 0:1] * d_ref[0:1, :]
        for bidx in range(1, N_DEV):
            acc = acc + ct_ref[:, bidx:bidx + 1] * d_ref[bidx:bidx + 1, :]
        o_ref[...] = acc

    return _call(body, name, (n, D_MODEL // tr),
                 [pl.BlockSpec((tr, N_DEV), lambda i, j: (j, 0)),
                  pl.BlockSpec((None, N_DEV, C), lambda i, j: (i, 0, 0))],
                 pl.BlockSpec((None, tr, C), lambda i, j: (i, j, 0)), _sds((n, D_MODEL, C), F32))(c_act_t, dmod)


def _my_pos():
    return lax.axis_index("x"), lax.axis_index("y"), lax.axis_index("c")


def small_all_gather(v):
    m_per, n = v.shape

    def body(x_ref, out_ref, send_sems, recv_sems, local_sem):
        x, y, c = _my_pos()
        me, sibling = (x, y, c), (x, y, 1 - c)
        chips = [(1 - x, y), (x, 1 - y), (1 - x, 1 - y)]

        def rows(px, py, pc):
            return out_ref.at[pl.ds((4 * px + 2 * py + pc) * m_per, m_per), :]

        def copy(k, block, to, src=None):
            return pltpu.make_async_remote_copy(
                src_ref=rows(*block) if src is None else src, dst_ref=rows(*block),
                send_sem=send_sems.at[k], recv_sem=recv_sems.at[k], device_id=to, device_id_type=MESH)

        mine = pltpu.make_async_copy(x_ref, rows(*me), local_sem)
        mine.start()
        first = [copy(0, me, sibling, src=x_ref)]
        first += [copy(1 + j, me, (*chip, c), src=x_ref) for j, chip in enumerate(chips)]
        for cp in first:
            cp.start()
        passed = [copy(4 + j, (*chip, c), sibling) for j, chip in enumerate(chips)]
        for j, chip in enumerate(chips):
            copy(1 + j, (*chip, c), me).wait_recv()
            passed[j].start()
        copy(0, sibling, me).wait_recv()
        for j, chip in enumerate(chips):
            copy(4 + j, (*chip, 1 - c), me).wait_recv()
        for cp in first + passed:
            cp.wait_send()
        mine.wait()

    return pl.pallas_call(
        body, name="small_all_gather",
        out_shape=_sds((N_DEV * m_per, n), v.dtype),
        in_specs=[pl.BlockSpec(memory_space=pltpu.VMEM)],
        out_specs=pl.BlockSpec(memory_space=pltpu.VMEM),
        scratch_shapes=[pltpu.SemaphoreType.DMA((7,)), pltpu.SemaphoreType.DMA((7,)), pltpu.SemaphoreType.DMA],
        compiler_params=pltpu.CompilerParams(vmem_limit_bytes=VMEM_LIMIT),
    )(v)


def big_all_gather(arrs):
    n = len(arrs)

    def body(*refs):
        ins, outs = refs[:n], refs[n:2 * n]
        send_sems, recv_sems, local_sems = refs[2 * n], refs[2 * n + 1], refs[2 * n + 2]
        x, y, c = _my_pos()
        me, sibling = (x, y, c), (x, y, 1 - c)
        chips = [(1 - x, y), (x, 1 - y), (1 - x, 1 - y)]

        def slot(a, px, py, pc):
            return outs[a].at[4 * px + 2 * py + pc]

        def copy(a, k, block, to, src=None):
            return pltpu.make_async_remote_copy(
                src_ref=slot(a, *block) if src is None else src, dst_ref=slot(a, *block),
                send_sem=send_sems.at[7 * a + k], recv_sem=recv_sems.at[7 * a + k], device_id=to, device_id_type=MESH)

        mine = [pltpu.make_async_copy(ins[a], slot(a, *me), local_sems.at[a]) for a in range(n)]
        for cp in mine:
            cp.start()
        first = []
        for a in range(n):
            first.append(copy(a, 0, me, sibling, src=ins[a]))
            first += [copy(a, 1 + j, me, (*chip, c), src=ins[a]) for j, chip in enumerate(chips)]
        for cp in first:
            cp.start()
        passed = []
        for a in range(n):
            for j, chip in enumerate(chips):
                copy(a, 1 + j, (*chip, c), me).wait_recv()
                fwd = copy(a, 4 + j, (*chip, c), sibling)
                fwd.start()
                passed.append(fwd)
        for a in range(n):
            copy(a, 0, sibling, me).wait_recv()
            for j, chip in enumerate(chips):
                copy(a, 4 + j, (*chip, 1 - c), me).wait_recv()
        for cp in first + passed:
            cp.wait_send()
        for cp in mine:
            cp.wait()

    hbm = pl.BlockSpec(memory_space=pltpu.HBM)
    return pl.pallas_call(
        body, name="big_all_gather",
        out_shape=[_sds((N_DEV,) + a.shape, a.dtype) for a in arrs],
        in_specs=[hbm] * n, out_specs=[hbm] * n,
        scratch_shapes=[pltpu.SemaphoreType.DMA((7 * n,)), pltpu.SemaphoreType.DMA((7 * n,)),
                        pltpu.SemaphoreType.DMA((n,))],
    )(*arrs)


def grad_exchange(arrs):
    n = len(arrs)
    flips = [(fx, fy, fc) for fx in (0, 1) for fy in (0, 1) for fc in (0, 1)][1:]

    def body(*refs):
        ins, outs = refs[:n], refs[n:2 * n]
        send_sems, recv_sems, local_sems = refs[2 * n], refs[2 * n + 1], refs[2 * n + 2]
        x, y, c = _my_pos()
        me = 4 * x + 2 * y + c
        copies = []
        for a in range(n):
            for k, (fx, fy, fc) in enumerate(flips):
                px, py, pc = (1 - x if fx else x), (1 - y if fy else y), (1 - c if fc else c)
                copies.append(pltpu.make_async_remote_copy(
                    src_ref=ins[a].at[4 * px + 2 * py + pc], dst_ref=outs[a].at[me],
                    send_sem=send_sems.at[7 * a + k], recv_sem=recv_sems.at[7 * a + k],
                    device_id=(px, py, pc), device_id_type=MESH))
        mine = [pltpu.make_async_copy(ins[a].at[me], outs[a].at[me], local_sems.at[a]) for a in range(n)]
        for cp in mine + copies:
            cp.start()
        for cp in copies:
            cp.wait()
        for cp in mine:
            cp.wait()

    hbm = pl.BlockSpec(memory_space=pltpu.HBM)
    return pl.pallas_call(
        body, name="grad_exchange",
        out_shape=[_sds(a.shape, a.dtype) for a in arrs],
        in_specs=[hbm] * n, out_specs=[hbm] * n,
        scratch_shapes=[pltpu.SemaphoreType.DMA((7 * n,)), pltpu.SemaphoreType.DMA((7 * n,)),
                        pltpu.SemaphoreType.DMA((n,))],
    )(*arrs)


def adamw(parts, w, m, v, name):
    R, C = w.shape
    n_parts = parts.shape[0]
    row_bytes = 2 * (n_parts * C * parts.dtype.itemsize + 7 * C * 4)
    tr = R
    for cand in (512, 256, 128, 64, 32, 16):
        if R % cand == 0 and R > cand and cand * row_bytes <= ADAMW_VMEM_BUDGET:
            tr = cand
            break
    c1 = 1.0 / (1.0 - ADAM_B1 ** ADAM_STEP)
    c2 = 1.0 / (1.0 - ADAM_B2 ** ADAM_STEP)

    def body(p_ref, w_ref, m_ref, v_ref, g_ref, d_ref, nm_ref, nv_ref):
        g = p_ref[0].astype(F32)
        for k in range(1, n_parts):
            g = g + p_ref[k].astype(F32)
        nm = ADAM_B1 * m_ref[...] + (1.0 - ADAM_B1) * g
        nv = ADAM_B2 * v_ref[...] + (1.0 - ADAM_B2) * (g * g)
        g_ref[...] = g
        nm_ref[...] = nm
        nv_ref[...] = nv
        d_ref[...] = -ADAM_LR * ((nm * c1) / (jnp.sqrt(nv * c2) + ADAM_EPS) + ADAM_WD * w_ref[...])

    blk = pl.BlockSpec((tr, C), lambda i: (i, 0))
    return _call(body, name, (R // tr,), [pl.BlockSpec((n_parts, tr, C), lambda i: (0, i, 0)), blk, blk, blk],
                 (blk, blk, blk, blk), tuple(_sds((R, C), F32) for _ in range(4)))(parts, w, m, v)


def _ffn_fwd(x, nw, sh, sc, g, w_gu, w_dn):
    h = norm_mod(x, nw, sh, sc, "ffn_norm")
    gu = matmul(h, w_gu, "nn", BF, "ffn_gu")
    a = swiglu(gu)
    y, xn = matmul(a, w_dn, "nn", BF, "ffn_down", res=x, gate=g, coef=0.5)
    return xn, (x, h, gu, a, y)


def _ffn_bwd(dxo, saved, nw, sc, g, w_gu, w_dn):
    x, h, gu, a, y = saved
    dy, acc1 = resid_gate_bwd(dxo, y, g, 0.5, "ffn_gate_bwd")
    d_wdn = matmul(a, dy, "tn", BF, "ffn_down_wgrad")
    da = matmul(dy, w_dn, "nt", BF, "ffn_down_dgrad")
    dgu = swiglu_bwd(da, gu)
    d_wgu = matmul(h, dgu, "tn", BF, "ffn_gu_wgrad")
    dh = matmul(dgu, w_gu, "nt", F32, "ffn_gu_dgrad")
    dx, acc2 = norm_mod_bwd(x, dh, dxo, nw, sc, "ffn_norm_bwd")
    return dx, d_wgu, d_wdn, (acc2[0], acc2[1], acc1[0]), acc2[2]


def _group_layout(a):
    L = a.shape[0]
    return a[:, :SSM_HEADS].reshape(L, SSM_GROUPS, _HPG).transpose(1, 0, 2)


def _ungroup_layout(a):
    L = a.shape[1]
    return jnp.pad(a.transpose(1, 0, 2).reshape(L, SSM_HEADS), ((0, 0), (0, 128 - SSM_HEADS)))


def _pad_row(vec, n=128):
    return jnp.pad(vec.reshape(1, -1), ((0, 0), (0, n - vec.shape[-1])))


def _mamba_fwd(x, nw, sh, sc, g, p):
    h = norm_mod(x, nw, sh, sc, "mix_norm")
    zx = matmul(h, p["w_in"], "nn", F32, "ssm_in")
    xc = conv_fwd(zx, p["conv_w"], p["conv_b"])
    dt, acs = dt_prep(zx, p["dt_bias"], p["a_log"])
    dt_g, acs_g = _group_layout(dt), _group_layout(acs)
    acs_t = acs_g.transpose(0, 2, 1)
    y, pst = ssd_fwd(xc, dt_g, acs_g, acs_t, p["d_exp"])
    yn = gate_norm(y, zx, p["norm_w"])
    yo, xn = matmul(yn, p["w_out"], "nn", BF, "ssm_out", res=x, gate=g, coef=1.0)
    return xn, (x, h, zx, xc, dt, dt_g, acs_g, acs_t, y, pst, yn, yo)


def _mamba_bwd(dxo, saved, nw, sc, g, p):
    x, h, zx, xc, dt, dt_g, acs_g, acs_t, y, pst, yn, yo = saved
    dyo, acc1 = resid_gate_bwd(dxo, yo, g, 1.0, "mix_gate_bwd")
    d_wout = matmul(yn, dyo, "tn", BF, "ssm_out_wgrad")
    dyn = matmul(dyo, p["w_out"], "nt", F32, "ssm_out_dgrad")
    dy, dz, accn = gate_norm_bwd(dyn, y, zx, p["norm_w"])
    dxs, dB, dC, ddt_g, da_g, dd = ssd_bwd(dy, xc, dt_g, acs_g, acs_t, pst, p["d_exp"])
    dxc = jnp.concatenate([dxs, dB, dC], axis=1)
    ds = conv_bwd_act(dxc, zx, p["conv_w"], p["conv_b"])
    du, accc = conv_bwd(ds, zx, p["conv_w"])
    draw, accdt = dt_bwd(_ungroup_layout(ddt_g), _ungroup_layout(da_g), dt, zx, p["dt_bias"], p["a_log"])
    dzx = jnp.concatenate([dz, du, draw], axis=1)
    d_win = matmul(h, dzx, "tn", BF, "ssm_in_wgrad")[:, :IN_PROJ]
    dh = matmul(dzx, p["w_in"], "nt", F32, "ssm_in_dgrad")
    dx, acc2 = norm_mod_bwd(x, dh, dxo, nw, sc, "mix_norm_bwd")
    small = dict(conv_w=accc[:CONV_WIDTH], conv_b=accc[CONV_WIDTH], dt_bias=accdt[0, :SSM_HEADS],
                 a_log=accdt[1, :SSM_HEADS], d=dd.reshape(SSM_HEADS, SSM_HEADDIM).sum(-1), norm_w=accn[0])
    return dx, d_win, d_wout, (acc2[0], acc2[1], acc1[0]), acc2[2], small


def _attn_layer_fwd(x, nw, sh, sc, g, p, kv):
    h = norm_mod(x, nw, sh, sc, "mix_norm")
    q = matmul(h, p["w_q"], "nn", F32, "attn_q", bias=p["b_q"])
    o = attn_fwd(q, kv, p["sinks"])
    yo, xn = matmul(o, p["w_o"], "nn", BF, "attn_o", bias=p["b_o"], res=x, gate=g, coef=1.0)
    return xn, (x, h, q, o, yo)


def _attn_layer_bwd(dxo, saved, nw, sc, g, p, kv):
    x, h, q, o, yo = saved
    dyo, acc1 = resid_gate_bwd(dxo, yo, g, 1.0, "mix_gate_bwd")
    d_wo = matmul(o, dyo, "tn", BF, "attn_o_wgrad")
    do = matmul(dyo, p["w_o"], "nt", F32, "attn_o_dgrad")
    dq, dkv_c, dkv_p, acca = attn_bwd(q, kv, do, p["sinks"])
    d_wq = matmul(h, dq, "tn", BF, "attn_q_wgrad")
    dh = matmul(dq, p["w_q"], "nt", F32, "attn_q_dgrad")
    dx, acc2 = norm_mod_bwd(x, dh, dxo, nw, sc, "mix_norm_bwd")
    small = dict(b_q=acca[0], sinks=acca[1, :ATT_HEADS], b_o=acc1[1])
    return dx, d_wq, d_wo, (acc2[0], acc2[1], acc1[0]), acc2[2], small, (dkv_c, dkv_p)


def _pack_rows(pieces):
    rows, spans, off = [], [], 0
    for a in pieces:
        flat = a.reshape(-1).astype(F32)
        n = -(-flat.shape[0] // D_MODEL)
        rows.append(jnp.pad(flat, (0, n * D_MODEL - flat.shape[0])).reshape(n, D_MODEL))
        spans.append((off, a.shape))
        off += n
    pad = -off % 8
    if pad:
        rows.append(jnp.zeros((pad, D_MODEL), F32))
    return jnp.concatenate(rows, axis=0), spans, off + pad


def _unpack_rows(g, spans):
    out = []
    for off, shape in spans:
        size = 1
        for s in shape:
            size *= s
        n = -(-size // D_MODEL)
        out.append(g[:, off:off + n].reshape(N_DEV, n * D_MODEL)[:, :size].reshape((N_DEV,) + tuple(shape)))
    return out


def _unshard_last(g):
    nd = g.ndim
    perm = tuple(range(1, nd - 1)) + (0, nd - 1)
    t = g.transpose(perm)
    return t.reshape(t.shape[:-2] + (N_DEV * g.shape[-1],))


def _shard_last(a, me):
    s = a.shape[-1] // N_DEV
    return lax.dynamic_slice_in_dim(a, me * s, s, axis=a.ndim - 1)


def kernel(x, c, ffn_norm_w, ffn_w_gu, ffn_w_down, mod_w, mod_b, mix_norm_w, ssm_w_in, ssm_conv_w, ssm_conv_b, ssm_dt_bias, ssm_a_log, ssm_d, ssm_norm_w, ssm_w_out, kv_norm_w, kv_mod_w, kv_mod_b, w_kv, b_kv, attn_w_q, attn_b_q, attn_sinks, attn_w_o, attn_b_o, final_norm_w, loss_target, m_ffn_norm_w, m_ffn_w_gu, m_ffn_w_down, m_mod_w, m_mod_b, m_mix_norm_w, m_ssm_w_in, m_ssm_conv_w, m_ssm_conv_b, m_ssm_dt_bias, m_ssm_a_log, m_ssm_d, m_ssm_norm_w, m_ssm_w_out, m_kv_norm_w, m_kv_mod_w, m_kv_mod_b, m_w_kv, m_b_kv, m_attn_w_q, m_attn_b_q, m_attn_sinks, m_attn_w_o, m_attn_b_o, m_final_norm_w, v_ffn_norm_w, v_ffn_w_gu, v_ffn_w_down, v_mod_w, v_mod_b, v_mix_norm_w, v_ssm_w_in, v_ssm_conv_w, v_ssm_conv_b, v_ssm_dt_bias, v_ssm_a_log, v_ssm_d, v_ssm_norm_w, v_ssm_w_out, v_kv_norm_w, v_kv_mod_w, v_kv_mod_b, v_w_kv, v_b_kv, v_attn_w_q, v_attn_b_q, v_attn_sinks, v_attn_w_o, v_attn_b_o, v_final_norm_w):
    D = D_MODEL
    me = 4 * lax.axis_index("x") + 2 * lax.axis_index("y") + lax.axis_index("c")
    xs = x[0]
    target = loss_target[0]
    mod_cols = mod_w.shape[-1]
    kvm_cols = kv_mod_w.shape[-1]

    packed, spans, _ = _pack_rows([c, ffn_norm_w, ssm_conv_w, ssm_conv_b, ssm_norm_w])
    nrow = packed.shape[0]
    g1 = small_all_gather(packed).reshape(N_DEV, nrow, D)
    c_all, fnw_g, cw_g, cb_g, snw_g = _unpack_rows(g1, spans)
    c_all = c_all.reshape(N_DEV, D)
    ffn_nw = _unshard_last(fnw_g)
    conv_w = _unshard_last(cw_g)
    conv_b = _unshard_last(cb_g)
    ssm_nw = _unshard_last(snw_g)

    mod_b_loc = lax.dynamic_slice_in_dim(mod_b, me * mod_cols, mod_cols, axis=1).reshape(DEPTH, 1, mod_cols)
    kvb_loc = lax.dynamic_slice_in_dim(kv_mod_b, me * kvm_cols, kvm_cols, axis=0).reshape(1, 1, kvm_cols)
    modp, c_act = mod_fwd(c_all, mod_w, mod_b_loc, "mod_fwd")
    kvmp, _ = mod_fwd(c_all, kv_mod_w.reshape(1, D, kvm_cols), kvb_loc, "kv_mod_fwd")
    packed2, spans2, _ = _pack_rows([modp, kvmp])
    nrow2 = packed2.shape[0]
    g2 = small_all_gather(packed2).reshape(N_DEV, nrow2, D)
    modp_g, kvmp_g = _unpack_rows(g2, spans2)
    mod_all = modp_g.transpose(1, 2, 0, 3).reshape(DEPTH, N_DEV, N_MOD * D)
    kvm_all = kvmp_g.transpose(1, 2, 0, 3).reshape(N_DEV, 2 * D)
    mod_me = lax.dynamic_index_in_dim(mod_all, me, axis=1, keepdims=False).reshape(DEPTH, N_MOD, 1, D)
    kvm_me = lax.dynamic_index_in_dim(kvm_all, me, axis=0, keepdims=False).reshape(2, 1, D)

    p_gu = ffn_w_gu.astype(BF).reshape(DEPTH * 2 * D, -1)
    r_dn = ffn_w_down.shape[2]
    r_out = ssm_w_out.shape[1]
    r_at = attn_w_q.shape[1]
    p_rows = jnp.concatenate([ffn_w_down.astype(BF).reshape(-1, D), ssm_w_out.astype(BF).reshape(-1, D),
                              attn_w_q.astype(BF).reshape(-1, D), attn_w_o.astype(BF).reshape(-1, D)], axis=0)
    p_in = ssm_w_in.astype(BF).reshape(N_A * D, -1)
    p_kv = w_kv.astype(BF)
    g_gu, g_rows, g_in, g_kv = big_all_gather([p_gu, p_rows, p_in, p_kv])
    g_gu = g_gu.reshape(N_DEV, DEPTH, 2, D, -1)
    g_in = g_in.reshape(N_DEV, N_A, D, -1)
    off_out = DEPTH * 2 * r_dn
    off_q = off_out + N_A * r_out
    off_o = off_q + N_A * r_at

    def w_gu_full(i, j):
        return g_gu[:, i, j].transpose(1, 0, 2).reshape(D, 2 * D_FF)

    def w_dn_full(i, j):
        o = (2 * i + j) * r_dn
        return g_rows[:, o:o + r_dn].reshape(D_FF, D)

    def rows_full(off, j, r):
        return g_rows[:, off + j * r:off + (j + 1) * r].reshape(N_DEV * r, D)

    w_kv_full = g_kv.reshape(D, KV_DIM)

    def mamba_params(j):
        w_in = jnp.pad(g_in[:, j].transpose(1, 0, 2).reshape(D, IN_PROJ), ((0, 0), (0, IN_PROJ_PAD - IN_PROJ)))
        return dict(w_in=w_in, w_out=rows_full(off_out, j, r_out), conv_w=conv_w[j], conv_b=conv_b[j].reshape(1, -1),
                    dt_bias=_pad_row(ssm_dt_bias[j]), a_log=_pad_row(ssm_a_log[j]),
                    d_exp=jnp.repeat(ssm_d[j], SSM_HEADDIM).reshape(SSM_GROUPS, 1, _GW),
                    norm_w=ssm_nw[j].reshape(1, -1))

    def attn_params(j):
        return dict(w_q=rows_full(off_q, j, r_at), w_o=rows_full(off_o, j, r_at), b_q=attn_b_q[j].reshape(1, -1),
                    b_o=attn_b_o[j].reshape(1, -1), sinks=_pad_row(attn_sinks[j]))

    saved = []
    kv = None
    kv_saved = None
    xcur = xs
    for i in range(DEPTH):
        md = mod_me[i]
        if i == N_A:
            h_kv = norm_mod(xcur, kv_norm_w.reshape(1, D), kvm_me[0], kvm_me[1], "kv_norm")
            kv = matmul(h_kv, w_kv_full, "nn", F32, "kv_proj", bias=b_kv.reshape(1, -1))
            kv_saved = (xcur, h_kv)
        x1, s1 = _ffn_fwd(xcur, ffn_nw[i, 0].reshape(1, D), md[0], md[1], md[2], w_gu_full(i, 0), w_dn_full(i, 0))
        if i < N_A:
            pm = mamba_params(i)
            x2, s2 = _mamba_fwd(x1, mix_norm_w[i].reshape(1, D), md[3], md[4], md[5], pm)
        else:
            pm = attn_params(i - N_A)
            x2, s2 = _attn_layer_fwd(x1, mix_norm_w[i].reshape(1, D), md[3], md[4], md[5], pm, kv)
        x3, s3 = _ffn_fwd(x2, ffn_nw[i, 1].reshape(1, D), md[6], md[7], md[8], w_gu_full(i, 1), w_dn_full(i, 1))
        saved.append((s1, s2, s3, pm))
        xcur = x3

    dx, accf = final_loss(xcur, final_norm_w.reshape(1, D), target)
    d_gu = [[None, None] for _ in range(DEPTH)]
    d_dn = [[None, None] for _ in range(DEPTH)]
    d_mod = [None] * DEPTH
    d_ffn_nw = [[None, None] for _ in range(DEPTH)]
    d_mix_nw = [None] * DEPTH
    d_in, d_out, d_q, d_o = [None] * N_A, [None] * N_A, [None] * N_A, [None] * N_A
    sm_m, sm_a = [None] * N_A, [None] * N_A
    kv_parts = [None] * N_A
    d_kv_w = d_kvm = d_kv_nw = d_bkv = None
    for i in reversed(range(DEPTH)):
        md = mod_me[i]
        s1, s2, s3, pm = saved[i]
        dx, d_gu[i][1], d_dn[i][1], m2, d_ffn_nw[i][1] = _ffn_bwd(
            dx, s3, ffn_nw[i, 1].reshape(1, D), md[7], md[8], w_gu_full(i, 1), w_dn_full(i, 1))
        if i < N_A:
            dx, d_in[i], d_out[i], mm_, d_mix_nw[i], sm_m[i] = _mamba_bwd(
                dx, s2, mix_norm_w[i].reshape(1, D), md[4], md[5], pm)
        else:
            j = i - N_A
            dx, d_q[j], d_o[j], mm_, d_mix_nw[i], sm_a[j], kv_parts[j] = _attn_layer_bwd(
                dx, s2, mix_norm_w[i].reshape(1, D), md[4], md[5], pm, kv)
        dx, d_gu[i][0], d_dn[i][0], m1, d_ffn_nw[i][0] = _ffn_bwd(
            dx, s1, ffn_nw[i, 0].reshape(1, D), md[1], md[2], w_gu_full(i, 0), w_dn_full(i, 0))
        d_mod[i] = jnp.concatenate(list(m1) + list(mm_) + list(m2), axis=0)
        if i == N_A:
            x_kv, h_kv = kv_saved
            dkv, acck = kv_grad_combine(kv_parts)
            d_bkv = acck[0]
            d_kv_w = matmul(h_kv, dkv, "tn", BF, "kv_wgrad")
            dh_kv = matmul(dkv, w_kv_full, "nt", F32, "kv_dgrad")
            dx, acc_kv = norm_mod_bwd(x_kv, dh_kv, dx, kv_norm_w.reshape(1, D), kvm_me[1], "kv_norm_bwd")
            d_kvm = jnp.concatenate([acc_kv[0], acc_kv[1]], axis=0)
            d_kv_nw = acc_kv[2]
    grad_x = dx.reshape(x.shape)

    e_gu = jnp.stack([jnp.stack(r, 0) for r in d_gu], 0)
    e_gu = e_gu.reshape(DEPTH * 2 * D, N_DEV, -1).transpose(1, 0, 2)
    e_rows = jnp.concatenate(
        [t.reshape(N_DEV, -1, D) for r in d_dn for t in r] + [t.reshape(N_DEV, -1, D) for t in d_out]
        + [t.reshape(N_DEV, -1, D) for t in d_q] + [t.reshape(N_DEV, -1, D) for t in d_o], axis=1)
    e_in = jnp.stack(d_in, 0).reshape(N_A * D, N_DEV, -1).transpose(1, 0, 2)
    e_kv = d_kv_w.reshape(N_DEV, -1, KV_DIM)
    r_gu, r_rows, r_in, r_kv = grad_exchange([e_gu, e_rows, e_in, e_kv])

    small_list = [
        jnp.stack(d_mod, 0), d_kvm,
        jnp.stack([jnp.stack(r, 0) for r in d_ffn_nw], 0),
        jnp.stack(d_mix_nw, 0),
        jnp.stack([s["conv_w"] for s in sm_m], 0), jnp.stack([s["conv_b"] for s in sm_m], 0),
        jnp.stack([s["dt_bias"] for s in sm_m], 0), jnp.stack([s["a_log"] for s in sm_m], 0),
        jnp.stack([s["d"] for s in sm_m], 0), jnp.stack([s["norm_w"] for s in sm_m], 0),
        d_kv_nw, d_bkv,
        jnp.stack([s["b_q"] for s in sm_a], 0), jnp.stack([s["sinks"] for s in sm_a], 0),
        jnp.stack([s["b_o"] for s in sm_a], 0), accf[0], accf[1],
    ]
    packed3, spans3, _ = _pack_rows(small_list)
    nrow3 = packed3.shape[0]
    g3 = small_all_gather(packed3).reshape(N_DEV, nrow3, D)
    (p_mod, p_kvm, p_fnw, p_mnw, p_cw, p_cb, p_dtb, p_al, p_d, p_snw, p_kvnw, p_bkv, p_bq, p_sk, p_bo, p_fin,
     p_loss) = _unpack_rows(g3, spans3)

    loss = 0.5 / D * jnp.sum(p_loss)

    c_act_t = c_act.T
    dmod_loc = _shard_last(p_mod, me).transpose(1, 0, 2)
    dkvm_loc = _shard_last(p_kvm, me).reshape(1, N_DEV, kvm_cols)
    gp_mod_w = mod_wgrad(c_act_t, dmod_loc, "mod_wgrad")
    gp_kvm_w = mod_wgrad(c_act_t, dkvm_loc, "kv_mod_wgrad")[0]

    def as_parts_single(a):
        return a[None]

    def upd(name, parts, w, m, v):
        shp = w.shape
        c_last = shp[-1]
        out = adamw(parts.reshape(parts.shape[0], -1, c_last), w.reshape(-1, c_last), m.reshape(-1, c_last),
                    v.reshape(-1, c_last), "adamw_" + name)
        return tuple(o.reshape(shp) for o in out)

    res = {}
    res["ffn_norm_w"] = upd("ffn_norm_w", _shard_last(p_fnw, me), ffn_norm_w, m_ffn_norm_w, v_ffn_norm_w)
    res["ffn_w_gu"] = upd("ffn_w_gu", r_gu, ffn_w_gu, m_ffn_w_gu, v_ffn_w_gu)
    n_dn = DEPTH * 2 * r_dn
    res["ffn_w_down"] = upd("ffn_w_down", r_rows[:, :n_dn], ffn_w_down, m_ffn_w_down, v_ffn_w_down)
    res["mod_w"] = upd("mod_w", as_parts_single(gp_mod_w), mod_w, m_mod_w, v_mod_w)
    res["mod_b"] = upd("mod_b", p_mod, mod_b, m_mod_b, v_mod_b)
    res["mix_norm_w"] = upd("mix_norm_w", p_mnw, mix_norm_w, m_mix_norm_w, v_mix_norm_w)
    res["ssm_w_in"] = upd("ssm_w_in", r_in, ssm_w_in, m_ssm_w_in, v_ssm_w_in)
    res["ssm_conv_w"] = upd("ssm_conv_w", _shard_last(p_cw, me), ssm_conv_w, m_ssm_conv_w, v_ssm_conv_w)
    res["ssm_conv_b"] = upd("ssm_conv_b", _shard_last(p_cb, me), ssm_conv_b, m_ssm_conv_b, v_ssm_conv_b)
    res["ssm_dt_bias"] = upd("ssm_dt_bias", p_dtb, ssm_dt_bias, m_ssm_dt_bias, v_ssm_dt_bias)
    res["ssm_a_log"] = upd("ssm_a_log", p_al, ssm_a_log, m_ssm_a_log, v_ssm_a_log)
    res["ssm_d"] = upd("ssm_d", p_d, ssm_d, m_ssm_d, v_ssm_d)
    res["ssm_norm_w"] = upd("ssm_norm_w", _shard_last(p_snw, me), ssm_norm_w, m_ssm_norm_w, v_ssm_norm_w)
    res["ssm_w_out"] = upd("ssm_w_out", r_rows[:, off_out:off_q], ssm_w_out, m_ssm_w_out, v_ssm_w_out)
    res["kv_norm_w"] = upd("kv_norm_w", p_kvnw.reshape(N_DEV, 1, D), kv_norm_w.reshape(1, D),
                           m_kv_norm_w.reshape(1, D), v_kv_norm_w.reshape(1, D))
    res["kv_mod_w"] = upd("kv_mod_w", as_parts_single(gp_kvm_w), kv_mod_w, m_kv_mod_w, v_kv_mod_w)
    res["kv_mod_b"] = upd("kv_mod_b", p_kvm.reshape(N_DEV, 1, 2 * D), kv_mod_b.reshape(1, -1),
                          m_kv_mod_b.reshape(1, -1), v_kv_mod_b.reshape(1, -1))
    res["w_kv"] = upd("w_kv", r_kv, w_kv, m_w_kv, v_w_kv)
    res["b_kv"] = upd("b_kv", p_bkv.reshape(N_DEV, 1, KV_DIM), b_kv.reshape(1, -1), m_b_kv.reshape(1, -1),
                      v_b_kv.reshape(1, -1))
    res["attn_w_q"] = upd("attn_w_q", r_rows[:, off_q:off_o], attn_w_q, m_attn_w_q, v_attn_w_q)
    res["attn_b_q"] = upd("attn_b_q", p_bq, attn_b_q, m_attn_b_q, v_attn_b_q)
    res["attn_sinks"] = upd("attn_sinks", p_sk, attn_sinks, m_attn_sinks, v_attn_sinks)
    res["attn_w_o"] = upd("attn_w_o", r_rows[:, off_o:], attn_w_o, m_attn_w_o, v_attn_w_o)
    res["attn_b_o"] = upd("attn_b_o", p_bo, attn_b_o, m_attn_b_o, v_attn_b_o)
    res["final_norm_w"] = upd("final_norm_w", p_fin.reshape(N_DEV, 1, D), final_norm_w.reshape(1, D),
                              m_final_norm_w.reshape(1, D), v_final_norm_w.reshape(1, D))

    names = ["ffn_norm_w", "ffn_w_gu", "ffn_w_down", "mod_w", "mod_b", "mix_norm_w", "ssm_w_in", "ssm_conv_w",
             "ssm_conv_b", "ssm_dt_bias", "ssm_a_log", "ssm_d", "ssm_norm_w", "ssm_w_out", "kv_norm_w", "kv_mod_w",
             "kv_mod_b", "w_kv", "b_kv", "attn_w_q", "attn_b_q", "attn_sinks", "attn_w_o", "attn_b_o", "final_norm_w"]
    vec_shapes = {"kv_norm_w": (D,), "kv_mod_b": (2 * D,), "b_kv": (KV_DIM,), "final_norm_w": (D,)}
    outs = [loss, grad_x]
    for k in range(4):
        for nme in names:
            t = res[nme][k]
            if nme in vec_shapes:
                t = t.reshape(vec_shapes[nme])
            outs.append(t)
    return tuple(outs)
```

```python
import functools

import jax
import jax.numpy as jnp
from jax import lax
from jax.experimental import pallas as pl
from jax.experimental.pallas import tpu as pltpu

F32 = jnp.float32
BF = jnp.bfloat16
MESH = pl.DeviceIdType.MESH

N_DEV = 8
D_MODEL = 1024
DEPTH = 4
N_A = 2
EPS = 1e-5
N_MOD = 9
D_FF = 2816
D_INNER = 2048
SSM_HEADDIM = 64
SSM_HEADS = 32
SSM_GROUPS = 8
SSM_STATE = 128
CONV_WIDTH = 4
CHUNK = 128
CONV_DIM = D_INNER + 2 * SSM_GROUPS * SSM_STATE
IN_PROJ = D_INNER + CONV_DIM + SSM_HEADS
IN_PROJ_PAD = D_INNER + CONV_DIM + 128
ATT_HEADS = 16
KV_HEADS = 4
HEAD_DIM = 64
WINDOW = 128
KV_DIM = 2 * KV_HEADS * HEAD_DIM

ADAM_LR = 0.001
ADAM_B1 = 0.9
ADAM_B2 = 0.999
ADAM_EPS = 1e-08
ADAM_WD = 0.01
ADAM_STEP = 10

VMEM_LIMIT = 48 * 2 ** 20
ADAMW_VMEM_BUDGET = 24 * 2 ** 20
NEG = -1e30


def _call(body, name, grid, in_specs, out_specs, out_shape, scratch=()):
    return pl.pallas_call(
        body, name=name, grid=grid, in_specs=in_specs, out_specs=out_specs, out_shape=out_shape,
        scratch_shapes=list(scratch),
        compiler_params=pltpu.CompilerParams(vmem_limit_bytes=VMEM_LIMIT))


def _tile(n, cap):
    t = (cap // 128) * 128
    while t >= 128:
        if n % t == 0:
            return t
        t -= 128
    return n


def _sds(shape, dtype):
    return jax.ShapeDtypeStruct(shape, dtype)


def _sigmoid(v):
    return 1.0 / (1.0 + jnp.exp(-v))


def _dot(a, b, dims):
    return lax.dot_general(a, b, (dims, ((), ())), preferred_element_type=F32)


def _dot_nn(a, b):
    return _dot(a.astype(BF), b.astype(BF), ((1,), (0,)))


def _dot_nt(a, b):
    return _dot(a.astype(BF), b.astype(BF), ((1,), (1,)))


def _dot_tn(a, b):
    return _dot(a.astype(BF), b.astype(BF), ((0,), (0,)))


def matmul(a, b, mode, out_dtype, name, bias=None, res=None, gate=None, coef=1.0):
    if mode == "nn":
        (M, K), (_, N) = a.shape, b.shape
    elif mode == "nt":
        (M, K), (N, _) = a.shape, b.shape
    else:
        (K, M), (_, N) = a.shape, b.shape
    cap_n = 512 if K > 4096 else 1024
    tm = _tile(M, 512)
    tn = _tile(N, cap_n)
    if mode == "nn":
        a_spec = pl.BlockSpec((tm, K), lambda i, j: (i, 0))
        b_spec = pl.BlockSpec((K, tn), lambda i, j: (0, j))
        fn = _dot_nn
    elif mode == "nt":
        a_spec = pl.BlockSpec((tm, K), lambda i, j: (i, 0))
        b_spec = pl.BlockSpec((tn, K), lambda i, j: (j, 0))
        fn = _dot_nt
    else:
        a_spec = pl.BlockSpec((K, tm), lambda i, j: (0, i))
        b_spec = pl.BlockSpec((K, tn), lambda i, j: (0, j))
        fn = _dot_tn
    has_bias, has_res = bias is not None, res is not None
    o_spec = pl.BlockSpec((tm, tn), lambda i, j: (i, j))
    v_spec = pl.BlockSpec((1, tn), lambda i, j: (0, j))
    in_specs, args = [a_spec, b_spec], [a, b]
    if has_bias:
        in_specs.append(v_spec)
        args.append(bias)
    if has_res:
        in_specs += [o_spec, v_spec]
        args += [res, gate]

    def body(*refs):
        a_ref, b_ref = refs[0], refs[1]
        k = 2
        y = fn(a_ref[...], b_ref[...])
        if has_bias:
            y = y + refs[k][...]
            k += 1
        if has_res:
            res_ref, gate_ref = refs[k], refs[k + 1]
            refs[k + 2][...] = y.astype(out_dtype)
            refs[k + 3][...] = res_ref[...] + coef * gate_ref[...] * y
        else:
            refs[k][...] = y.astype(out_dtype)

    if has_res:
        out_shape = (_sds((M, N), out_dtype), _sds((M, N), F32))
        out_specs = (o_spec, o_spec)
    else:
        out_shape = _sds((M, N), out_dtype)
        out_specs = o_spec
    return _call(body, name, (M // tm, N // tn), in_specs, out_specs, out_shape)(*args)


def norm_mod(x, nw, sh, sc, name):
    L, D = x.shape
    tm = _tile(L, 512)

    def body(x_ref, nw_ref, sh_ref, sc_ref, h_ref):
        xf = x_ref[...]
        r = lax.rsqrt(jnp.mean(xf * xf, axis=-1, keepdims=True) + EPS)
        n = xf * r * nw_ref[...]
        h_ref[...] = (n * (1.0 + sc_ref[...]) + sh_ref[...]).astype(BF)

    row = pl.BlockSpec((tm, D), lambda i: (i, 0))
    vec = pl.BlockSpec((1, D), lambda i: (0, 0))
    return _call(body, name, (L // tm,), [row, vec, vec, vec], row, _sds((L, D), BF))(x, nw, sh, sc)


def norm_mod_bwd(x, dh, dres, nw, sc, name):
    L, D = x.shape
    tm = _tile(L, 512)

    def body(x_ref, dh_ref, dres_ref, nw_ref, sc_ref, dx_ref, acc_ref):
        @pl.when(pl.program_id(0) == 0)
        def _():
            acc_ref[...] = jnp.zeros_like(acc_ref)

        xf = x_ref[...]
        dhf = dh_ref[...].astype(F32)
        r = lax.rsqrt(jnp.mean(xf * xf, axis=-1, keepdims=True) + EPS)
        xhat = xf * r
        nwv = nw_ref[...]
        dn = dhf * (1.0 + sc_ref[...])
        dxhat = dn * nwv
        proj = jnp.mean(dxhat * xhat, axis=-1, keepdims=True)
        dx_ref[...] = dres_ref[...] + r * (dxhat - xhat * proj)
        acc_ref[0:1, :] += jnp.sum(dhf, axis=0, keepdims=True)
        acc_ref[1:2, :] += jnp.sum(dhf * xhat * nwv, axis=0, keepdims=True)
        acc_ref[2:3, :] += jnp.sum(dn * xhat, axis=0, keepdims=True)

    row = pl.BlockSpec((tm, D), lambda i: (i, 0))
    vec = pl.BlockSpec((1, D), lambda i: (0, 0))
    acc = pl.BlockSpec((8, D), lambda i: (0, 0))
    return _call(body, name, (L // tm,), [row, row, row, vec, vec], (row, acc),
                 (_sds((L, D), F32), _sds((8, D), F32)))(x, dh, dres, nw, sc)


def final_loss(x, nw, target):
    L, D = x.shape
    tm = _tile(L, 512)

    def body(x_ref, nw_ref, t_ref, dx_ref, acc_ref):
        @pl.when(pl.program_id(0) == 0)
        def _():
            acc_ref[...] = jnp.zeros_like(acc_ref)

        xf = x_ref[...]
        r = lax.rsqrt(jnp.mean(xf * xf, axis=-1, keepdims=True) + EPS)
        xhat = xf * r
        nwv = nw_ref[...]
        err = xhat * nwv - t_ref[...]
        dy = err * (1.0 / D)
        dxhat = dy * nwv
        proj = jnp.mean(dxhat * xhat, axis=-1, keepdims=True)
        dx_ref[...] = r * (dxhat - xhat * proj)
        acc_ref[0:1, :] += jnp.sum(dy * xhat, axis=0, keepdims=True)
        acc_ref[1:2, :] += jnp.sum(err * err, axis=0, keepdims=True)

    row = pl.BlockSpec((tm, D), lambda i: (i, 0))
    vec = pl.BlockSpec((1, D), lambda i: (0, 0))
    acc = pl.BlockSpec((8, D), lambda i: (0, 0))
    return _call(body, "final_loss", (L // tm,), [row, vec, row], (row, acc),
                 (_sds((L, D), F32), _sds((8, D), F32)))(x, nw, target)


def resid_gate_bwd(dxo, y, gate, coef, name):
    L, D = dxo.shape
    tm = _tile(L, 512)

    def body(dxo_ref, y_ref, g_ref, dy_ref, acc_ref):
        @pl.when(pl.program_id(0) == 0)
        def _():
            acc_ref[...] = jnp.zeros_like(acc_ref)

        d = dxo_ref[...]
        dy = coef * g_ref[...] * d
        dy_ref[...] = dy.astype(BF)
        acc_ref[0:1, :] += coef * jnp.sum(d * y_ref[...].astype(F32), axis=0, keepdims=True)
        acc_ref[1:2, :] += jnp.sum(dy, axis=0, keepdims=True)

    row = pl.BlockSpec((tm, D), lambda i: (i, 0))
    vec = pl.BlockSpec((1, D), lambda i: (0, 0))
    acc = pl.BlockSpec((8, D), lambda i: (0, 0))
    return _call(body, name, (L // tm,), [row, row, vec], (row, acc),
                 (_sds((L, D), BF), _sds((8, D), F32)))(dxo, y, gate)


def swiglu(gu):
    L = gu.shape[0]
    tm, tn = _tile(L, 512), _tile(D_FF, 512)
    nj = D_FF // tn

    def body(g_ref, u_ref, a_ref):
        g = g_ref[...].astype(F32)
        a_ref[...] = (g * _sigmoid(g) * u_ref[...].astype(F32)).astype(BF)

    return _call(body, "swiglu", (L // tm, nj),
                 [pl.BlockSpec((tm, tn), lambda i, j: (i, j)), pl.BlockSpec((tm, tn), lambda i, j: (i, j + nj))],
                 pl.BlockSpec((tm, tn), lambda i, j: (i, j)), _sds((L, D_FF), BF))(gu, gu)


def swiglu_bwd(da, gu):
    L = gu.shape[0]
    tm, tn = _tile(L, 512), _tile(D_FF, 512)
    nj = D_FF // tn

    def body(da_ref, g_ref, u_ref, o_ref):
        g = g_ref[...].astype(F32)
        u = u_ref[...].astype(F32)
        d = da_ref[...].astype(F32)
        s = _sigmoid(g)
        dg = d * u * s * (1.0 + g * (1.0 - s))
        du = d * g * s
        o_ref[...] = jnp.where(pl.program_id(1) < nj, dg, du).astype(BF)

    return _call(body, "swiglu_bwd", (L // tm, 2 * nj),
                 [pl.BlockSpec((tm, tn), lambda i, j: (i, j % nj)),
                  pl.BlockSpec((tm, tn), lambda i, j: (i, j % nj)),
                  pl.BlockSpec((tm, tn), lambda i, j: (i, nj + j % nj))],
                 pl.BlockSpec((tm, tn), lambda i, j: (i, j)), _sds((L, 2 * D_FF), BF))(da, gu, gu)


def _shift_rows(cur, other, k, down):
    n = cur.shape[0]
    rows = lax.broadcasted_iota(jnp.int32, cur.shape, 0)
    if down:
        return jnp.where(rows < k, pltpu.roll(other, k, 0), pltpu.roll(cur, k, 0))
    return jnp.where(rows >= n - k, pltpu.roll(other, n - k, 0), pltpu.roll(cur, n - k, 0))


def _conv_pre(cur, prev, w_ref, b_ref):
    s = cur * w_ref[CONV_WIDTH - 1:CONV_WIDTH, :] + b_ref[...]
    for k in range(1, CONV_WIDTH):
        s = s + _shift_rows(cur, prev, k, True) * w_ref[CONV_WIDTH - 1 - k:CONV_WIDTH - k, :]
    return s


_XBC_COL0 = D_INNER // 512


def conv_fwd(zx, w, b):
    L = zx.shape[0]
    tm, tc = _tile(L, 256), 512

    def body(cur_ref, prev_ref, w_ref, b_ref, o_ref):
        cur = cur_ref[...]
        prev = jnp.where(pl.program_id(1) > 0, prev_ref[...], 0.0)
        s = _conv_pre(cur, prev, w_ref, b_ref)
        o_ref[...] = s * _sigmoid(s)

    return _call(body, "conv_fwd", (CONV_DIM // tc, L // tm),
                 [pl.BlockSpec((tm, tc), lambda j, i: (i, _XBC_COL0 + j)),
                  pl.BlockSpec((tm, tc), lambda j, i: (jnp.maximum(i - 1, 0), _XBC_COL0 + j)),
                  pl.BlockSpec((CONV_WIDTH, tc), lambda j, i: (0, j)),
                  pl.BlockSpec((1, tc), lambda j, i: (0, j))],
                 pl.BlockSpec((tm, tc), lambda j, i: (i, j)), _sds((L, CONV_DIM), F32))(zx, zx, w, b)


def conv_bwd_act(dxc, zx, w, b):
    L = zx.shape[0]
    tm, tc = _tile(L, 256), 512

    def body(d_ref, cur_ref, prev_ref, w_ref, b_ref, o_ref):
        cur = cur_ref[...]
        prev = jnp.where(pl.program_id(1) > 0, prev_ref[...], 0.0)
        s = _conv_pre(cur, prev, w_ref, b_ref)
        sg = _sigmoid(s)
        o_ref[...] = d_ref[...] * sg * (1.0 + s * (1.0 - sg))

    return _call(body, "conv_bwd_act", (CONV_DIM // tc, L // tm),
                 [pl.BlockSpec((tm, tc), lambda j, i: (i, j)),
                  pl.BlockSpec((tm, tc), lambda j, i: (i, _XBC_COL0 + j)),
                  pl.BlockSpec((tm, tc), lambda j, i: (jnp.maximum(i - 1, 0), _XBC_COL0 + j)),
                  pl.BlockSpec((CONV_WIDTH, tc), lambda j, i: (0, j)),
                  pl.BlockSpec((1, tc), lambda j, i: (0, j))],
                 pl.BlockSpec((tm, tc), lambda j, i: (i, j)), _sds((L, CONV_DIM), F32))(dxc, zx, zx, w, b)


def conv_bwd(ds, zx, w):
    L = zx.shape[0]
    tm, tc = _tile(L, 256), 512
    nblk = L // tm

    def body(ds_ref, dsn_ref, cur_ref, prev_ref, w_ref, du_ref, acc_ref):
        i = pl.program_id(1)

        @pl.when(i == 0)
        def _():
            acc_ref[...] = jnp.zeros_like(acc_ref)

        ds_c = ds_ref[...]
        ds_n = jnp.where(i < nblk - 1, dsn_ref[...], 0.0)
        cur = cur_ref[...]
        prev = jnp.where(i > 0, prev_ref[...], 0.0)
        du = ds_c * w_ref[CONV_WIDTH - 1:CONV_WIDTH, :]
        acc_ref[CONV_WIDTH - 1:CONV_WIDTH, :] += jnp.sum(ds_c * cur, axis=0, keepdims=True)
        for k in range(1, CONV_WIDTH):
            du = du + _shift_rows(ds_c, ds_n, k, False) * w_ref[CONV_WIDTH - 1 - k:CONV_WIDTH - k, :]
            acc_ref[CONV_WIDTH - 1 - k:CONV_WIDTH - k, :] += jnp.sum(
                ds_c * _shift_rows(cur, prev, k, True), axis=0, keepdims=True)
        acc_ref[CONV_WIDTH:CONV_WIDTH + 1, :] += jnp.sum(ds_c, axis=0, keepdims=True)
        du_ref[...] = du.astype(BF)

    return _call(body, "conv_bwd", (CONV_DIM // tc, nblk),
                 [pl.BlockSpec((tm, tc), lambda j, i: (i, j)),
                  pl.BlockSpec((tm, tc), lambda j, i: (jnp.minimum(i + 1, nblk - 1), j)),
                  pl.BlockSpec((tm, tc), lambda j, i: (i, _XBC_COL0 + j)),
                  pl.BlockSpec((tm, tc), lambda j, i: (jnp.maximum(i - 1, 0), _XBC_COL0 + j)),
                  pl.BlockSpec((CONV_WIDTH, tc), lambda j, i: (0, j))],
                 (pl.BlockSpec((tm, tc), lambda j, i: (i, j)), pl.BlockSpec((8, tc), lambda j, i: (0, j))),
                 (_sds((L, CONV_DIM), BF), _sds((8, CONV_DIM), F32)))(ds, ds, zx, zx, w)


_DT_COL = (D_INNER + CONV_DIM) // 128


def dt_prep(zx, bias_pad, alog_pad):
    L = zx.shape[0]

    def body(raw_ref, b_ref, al_ref, dt_ref, acs_ref):
        v = raw_ref[...] + b_ref[...]
        dt = jnp.maximum(v, 0.0) + jnp.log(1.0 + jnp.exp(-jnp.abs(v)))
        dt_ref[...] = dt
        acs = dt * (-jnp.exp(al_ref[...]))
        rows = lax.broadcasted_iota(jnp.int32, acs.shape, 0)
        s = 1
        while s < CHUNK:
            acs = acs + jnp.where(rows >= s, pltpu.roll(acs, s, 0), 0.0)
            s *= 2
        acs_ref[...] = acs

    blk = pl.BlockSpec((CHUNK, 128), lambda i: (i, 0))
    vec = pl.BlockSpec((1, 128), lambda i: (0, 0))
    return _call(body, "dt_prep", (L // CHUNK,),
                 [pl.BlockSpec((CHUNK, 128), lambda i: (i, _DT_COL)), vec, vec], (blk, blk),
                 (_sds((L, 128), F32), _sds((L, 128), F32)))(zx, bias_pad, alog_pad)


def dt_bwd(ddt, da, dt, zx, bias_pad, alog_pad):
    L = zx.shape[0]
    tm = _tile(L, 512)

    def body(ddt_ref, da_ref, dt_ref, raw_ref, b_ref, al_ref, o_ref, acc_ref):
        @pl.when(pl.program_id(0) == 0)
        def _():
            acc_ref[...] = jnp.zeros_like(acc_ref)

        A = -jnp.exp(al_ref[...])
        dav = da_ref[...]
        dd = ddt_ref[...] + dav * A
        draw = dd * _sigmoid(raw_ref[...] + b_ref[...])
        o_ref[...] = draw.astype(BF)
        acc_ref[0:1, :] += jnp.sum(draw, axis=0, keepdims=True)
        acc_ref[1:2, :] += jnp.sum(dav * dt_ref[...], axis=0, keepdims=True) * A

    blk = pl.BlockSpec((tm, 128), lambda i: (i, 0))
    vec = pl.BlockSpec((1, 128), lambda i: (0, 0))
    return _call(body, "dt_bwd", (L // tm,),
                 [blk, blk, blk, pl.BlockSpec((tm, 128), lambda i: (i, _DT_COL)), vec, vec],
                 (blk, pl.BlockSpec((8, 128), lambda i: (0, 0))),
                 (_sds((L, 128), BF), _sds((8, 128), F32)))(ddt, da, dt, zx, bias_pad, alog_pad)


_HPG = SSM_HEADS // SSM_GROUPS
_GW = _HPG * SSM_HEADDIM
_B_COL0 = D_INNER // SSM_STATE
_C_COL0 = (D_INNER + SSM_GROUPS * SSM_STATE) // SSM_STATE


def _ssd_head(x, dtc, ac, ar, r, causal):
    xh = x[:, SSM_HEADDIM * r:SSM_HEADDIM * (r + 1)]
    acol = ac[:, r:r + 1]
    arow = ar[r:r + 1, :]
    alast = ar[r:r + 1, CHUNK - 1:CHUNK]
    lm = jnp.exp(jnp.where(causal, acol - arow, NEG))
    return xh, xh * dtc[:, r:r + 1], acol, alast, lm


def ssd_fwd(xc, dt_g, acs_g, acsT_g, d_exp):
    L = xc.shape[0]
    nc = L // CHUNK

    def body(x_ref, b_ref, c_ref, dt_ref, ac_ref, ar_ref, d_ref, y_ref, pst_ref, st_ref):
        @pl.when(pl.program_id(1) == 0)
        def _():
            st_ref[...] = jnp.zeros_like(st_ref)

        x, Bm, Cm = x_ref[...], b_ref[...], c_ref[...]
        dtc, ac, ar = dt_ref[...], ac_ref[...], ar_ref[...]
        causal = lax.broadcasted_iota(jnp.int32, (CHUNK, CHUNK), 0) >= lax.broadcasted_iota(jnp.int32, (CHUNK, CHUNK), 1)
        CB = _dot_nt(Cm, Bm)
        for r in range(_HPG):
            xh, xd, acol, alast, lm = _ssd_head(x, dtc, ac, ar, r, causal)
            P = st_ref[r]
            y = _dot_nn(CB * lm, xd) + jnp.exp(acol) * _dot_nt(Cm, P)
            y_ref[:, SSM_HEADDIM * r:SSM_HEADDIM * (r + 1)] = y + d_ref[:, SSM_HEADDIM * r:SSM_HEADDIM * (r + 1)] * xh
            pst_ref[r] = P
            st_ref[r] = P * jnp.exp(alast) + _dot_tn(xd * jnp.exp(alast - acol), Bm)

    return _call(
        body, "ssd_fwd", (SSM_GROUPS, nc),
        [pl.BlockSpec((CHUNK, _GW), lambda g, c: (c, g)),
         pl.BlockSpec((CHUNK, SSM_STATE), lambda g, c: (c, _B_COL0 + g)),
         pl.BlockSpec((CHUNK, SSM_STATE), lambda g, c: (c, _C_COL0 + g)),
         pl.BlockSpec((None, CHUNK, _HPG), lambda g, c: (g, c, 0)),
         pl.BlockSpec((None, CHUNK, _HPG), lambda g, c: (g, c, 0)),
         pl.BlockSpec((None, _HPG, CHUNK), lambda g, c: (g, 0, c)),
         pl.BlockSpec((None, 1, _GW), lambda g, c: (g, 0, 0))],
        (pl.BlockSpec((CHUNK, _GW), lambda g, c: (c, g)),
         pl.BlockSpec((None, None, _HPG, SSM_HEADDIM, SSM_STATE), lambda g, c: (c, g, 0, 0, 0))),
        (_sds((L, D_INNER), F32), _sds((nc, SSM_GROUPS, _HPG, SSM_HEADDIM, SSM_STATE), F32)),
        scratch=[pltpu.VMEM((_HPG, SSM_HEADDIM, SSM_STATE), F32)],
    )(xc, xc, xc, dt_g, acs_g, acsT_g, d_exp)


def ssd_bwd(dy, xc, dt_g, acs_g, acsT_g, pst, d_exp):
    L = xc.shape[0]
    nc = L // CHUNK

    def body(dy_ref, x_ref, b_ref, c_ref, dt_ref, ac_ref, ar_ref, pst_ref, d_ref,
             dx_ref, db_ref, dc_ref, ddt_ref, da_ref, dd_ref, dp_ref):
        @pl.when(pl.program_id(1) == 0)
        def _():
            dp_ref[...] = jnp.zeros_like(dp_ref)
            dd_ref[...] = jnp.zeros_like(dd_ref)

        dyv, x, Bm, Cm = dy_ref[...], x_ref[...], b_ref[...], c_ref[...]
        dtc, ac, ar = dt_ref[...], ac_ref[...], ar_ref[...]
        ri = lax.broadcasted_iota(jnp.int32, (CHUNK, CHUNK), 0)
        ci = lax.broadcasted_iota(jnp.int32, (CHUNK, CHUNK), 1)
        causal = ri >= ci
        lane4 = lax.broadcasted_iota(jnp.int32, (CHUNK, _HPG), 1)
        CB = _dot_nt(Cm, Bm)
        dB = jnp.zeros((CHUNK, SSM_STATE), F32)
        dC = jnp.zeros((CHUNK, SSM_STATE), F32)
        dCB = jnp.zeros((CHUNK, CHUNK), F32)
        ddt_blk = jnp.zeros((CHUNK, _HPG), F32)
        da_blk = jnp.zeros((CHUNK, _HPG), F32)
        for r in range(_HPG):
            sl = slice(SSM_HEADDIM * r, SSM_HEADDIM * (r + 1))
            xh, xd, acol, alast, lm = _ssd_head(x, dtc, ac, ar, r, causal)
            dyh = dyv[:, sl]
            P = pst_ref[r]
            dPn = dp_ref[r]
            eA = jnp.exp(acol)
            cd = jnp.exp(alast)
            dte = jnp.exp(alast - acol)
            G = CB * lm
            Z = _dot_nt(Cm, P)
            dZ = eA * dyh
            dC = dC + _dot_nn(dZ, P)
            dp_ref[r] = dPn * cd + _dot_tn(dZ, Cm)
            dA_col = jnp.sum(dZ * Z, axis=1, keepdims=True)
            BdS = _dot_nt(Bm, dPn)
            dxd = dte * BdS
            dB = dB + dte * _dot_nn(xd, dPn)
            t = jnp.sum(xd * BdS, axis=1, keepdims=True) * dte
            dA_col = dA_col - t
            dA_last = jnp.sum(t, axis=0, keepdims=True) + jnp.sum(
                jnp.sum(dPn * P, axis=1, keepdims=True), axis=0, keepdims=True) * cd
            dG = _dot_nt(dyh, xd)
            dxd = dxd + _dot_tn(G, dyh)
            dCB = dCB + dG * lm
            W = dG * G
            dA_col = dA_col + jnp.sum(W, axis=1, keepdims=True)
            dA_row = jnp.sum(jnp.where(ri == ci, dA_col, 0.0), axis=0, keepdims=True) - jnp.sum(W, axis=0, keepdims=True)
            da_col = jnp.sum(jnp.where(ci >= ri, dA_row, 0.0), axis=1, keepdims=True) + dA_last
            da_blk = jnp.where(lane4 == r, da_col, da_blk)
            ddt_blk = jnp.where(lane4 == r, jnp.sum(dxd * xh, axis=1, keepdims=True), ddt_blk)
            dx_ref[:, sl] = dxd * dtc[:, r:r + 1] + d_ref[:, sl] * dyh
        dc_ref[...] = dC + _dot_nn(dCB, Bm)
        db_ref[...] = dB + _dot_tn(dCB, Cm)
        ddt_ref[...] = ddt_blk
        da_ref[...] = da_blk
        dd_ref[...] += jnp.sum(dyv * x, axis=0, keepdims=True)

    rc = lambda g, c: (nc - 1 - c, g)
    small = pl.BlockSpec((None, CHUNK, _HPG), lambda g, c: (g, nc - 1 - c, 0))
    return _call(
        body, "ssd_bwd", (SSM_GROUPS, nc),
        [pl.BlockSpec((CHUNK, _GW), rc),
         pl.BlockSpec((CHUNK, _GW), rc),
         pl.BlockSpec((CHUNK, SSM_STATE), lambda g, c: (nc - 1 - c, _B_COL0 + g)),
         pl.BlockSpec((CHUNK, SSM_STATE), lambda g, c: (nc - 1 - c, _C_COL0 + g)),
         small, small,
         pl.BlockSpec((None, _HPG, CHUNK), lambda g, c: (g, 0, nc - 1 - c)),
         pl.BlockSpec((None, None, _HPG, SSM_HEADDIM, SSM_STATE), lambda g, c: (nc - 1 - c, g, 0, 0, 0)),
         pl.BlockSpec((None, 1, _GW), lambda g, c: (g, 0, 0))],
        (pl.BlockSpec((CHUNK, _GW), rc),
         pl.BlockSpec((CHUNK, SSM_STATE), rc),
         pl.BlockSpec((CHUNK, SSM_STATE), rc),
         small, small,
         pl.BlockSpec((None, 1, _GW), lambda g, c: (g, 0, 0))),
        (_sds((L, D_INNER), F32), _sds((L, SSM_GROUPS * SSM_STATE), F32), _sds((L, SSM_GROUPS * SSM_STATE), F32),
         _sds((SSM_GROUPS, L, _HPG), F32), _sds((SSM_GROUPS, L, _HPG), F32), _sds((SSM_GROUPS, 1, _GW), F32)),
        scratch=[pltpu.VMEM((_HPG, SSM_HEADDIM, SSM_STATE), F32)],
    )(dy, xc, xc, xc, dt_g, acs_g, acsT_g, pst, d_exp)


_NGW = D_INNER // SSM_GROUPS


def gate_norm(y, zx, nw):
    L = y.shape[0]
    tm = _tile(L, 256)

    def body(y_ref, z_ref, nw_ref, o_ref):
        for g in range(SSM_GROUPS):
            sl = slice(_NGW * g, _NGW * (g + 1))
            z = z_ref[:, sl]
            y2 = y_ref[:, sl] * (z * _sigmoid(z))
            r = lax.rsqrt(jnp.mean(y2 * y2, axis=-1, keepdims=True) + EPS)
            o_ref[:, sl] = (y2 * r * nw_ref[:, sl]).astype(BF)

    row = pl.BlockSpec((tm, D_INNER), lambda i: (i, 0))
    return _call(body, "gate_norm", (L // tm,), [row, row, pl.BlockSpec((1, D_INNER), lambda i: (0, 0))],
                 row, _sds((L, D_INNER), BF))(y, zx, nw)


def gate_norm_bwd(dyn, y, zx, nw):
    L = y.shape[0]
    tm = _tile(L, 256)

    def body(d_ref, y_ref, z_ref, nw_ref, dy_ref, dz_ref, acc_ref):
        @pl.when(pl.program_id(0) == 0)
        def _():
            acc_ref[...] = jnp.zeros_like(acc_ref)

        for g in range(SSM_GROUPS):
            sl = slice(_NGW * g, _NGW * (g + 1))
            z = z_ref[:, sl]
            yv = y_ref[:, sl]
            sg = _sigmoid(z)
            sz = z * sg
            y2 = yv * sz
            r = lax.rsqrt(jnp.mean(y2 * y2, axis=-1, keepdims=True) + EPS)
            yh = y2 * r
            d = d_ref[:, sl]
            dn = d * nw_ref[:, sl]
            dy2 = r * (dn - yh * jnp.mean(dn * yh, axis=-1, keepdims=True))
            dy_ref[:, sl] = dy2 * sz
            dz_ref[:, sl] = (dy2 * yv * sg * (1.0 + z * (1.0 - sg))).astype(BF)
            acc_ref[0:1, sl] += jnp.sum(d * yh, axis=0, keepdims=True)

    row = pl.BlockSpec((tm, D_INNER), lambda i: (i, 0))
    return _call(body, "gate_norm_bwd", (L // tm,), [row, row, row, pl.BlockSpec((1, D_INNER), lambda i: (0, 0))],
                 (row, row, pl.BlockSpec((8, D_INNER), lambda i: (0, 0))),
                 (_sds((L, D_INNER), F32), _sds((L, D_INNER), BF), _sds((8, D_INNER), F32)))(dyn, y, zx, nw)


_SCALE = HEAD_DIM ** -0.5
_REP = ATT_HEADS // KV_HEADS
_V_OFF = KV_HEADS * HEAD_DIM


def _attn_probs(qh, kp, kc, sink, first):
    rows = lax.broadcasted_iota(jnp.int32, (WINDOW, WINDOW), 0)
    cols = lax.broadcasted_iota(jnp.int32, (WINDOW, WINDOW), 1)
    sp = jnp.where(jnp.logical_and(cols > rows, jnp.logical_not(first)), _dot_nt(qh, kp) * _SCALE, NEG)
    sc = jnp.where(cols <= rows, _dot_nt(qh, kc) * _SCALE, NEG)
    m = jnp.maximum(jnp.maximum(jnp.max(sp, axis=1, keepdims=True), jnp.max(sc, axis=1, keepdims=True)), sink)
    pp = jnp.exp(sp - m)
    pc = jnp.exp(sc - m)
    ps = jnp.exp(sink - m)
    inv = 1.0 / (jnp.sum(pp, axis=1, keepdims=True) + jnp.sum(pc, axis=1, keepdims=True) + ps)
    return pp * inv, pc * inv, ps * inv


def attn_fwd(q, kv, sinks_pad):
    L = q.shape[0]
    nb = L // WINDOW

    def body(q_ref, kc_ref, kp_ref, s_ref, o_ref):
        first = pl.program_id(0) == 0
        for h in range(ATT_HEADS):
            k = h // _REP
            ks = slice(HEAD_DIM * k, HEAD_DIM * (k + 1))
            vs = slice(_V_OFF + HEAD_DIM * k, _V_OFF + HEAD_DIM * (k + 1))
            hs = slice(HEAD_DIM * h, HEAD_DIM * (h + 1))
            pp, pc, _ = _attn_probs(q_ref[:, hs], kp_ref[:, ks], kc_ref[:, ks], s_ref[:, h:h + 1], first)
            o_ref[:, hs] = (_dot_nn(pp, kp_ref[:, vs]) + _dot_nn(pc, kc_ref[:, vs])).astype(BF)

    qspec = pl.BlockSpec((WINDOW, D_MODEL), lambda i: (i, 0))
    return _call(body, "attn_fwd", (nb,),
                 [qspec, pl.BlockSpec((WINDOW, KV_DIM), lambda i: (i, 0)),
                  pl.BlockSpec((WINDOW, KV_DIM), lambda i: (jnp.maximum(i - 1, 0), 0)),
                  pl.BlockSpec((1, 128), lambda i: (0, 0))],
                 qspec, _sds((L, D_MODEL), BF))(q, kv, kv, sinks_pad)


def attn_bwd(q, kv, do, sinks_pad):
    L = q.shape[0]
    nb = L // WINDOW

    def body(q_ref, kc_ref, kp_ref, do_ref, s_ref, dq_ref, dc_ref, dp_ref, acc_ref):
        first = pl.program_id(0) == 0

        @pl.when(first)
        def _():
            acc_ref[...] = jnp.zeros_like(acc_ref)

        lane = lax.broadcasted_iota(jnp.int32, (1, 128), 1)
        dsink = jnp.zeros((1, 128), F32)
        for k in range(KV_HEADS):
            ks = slice(HEAD_DIM * k, HEAD_DIM * (k + 1))
            vs = slice(_V_OFF + HEAD_DIM * k, _V_OFF + HEAD_DIM * (k + 1))
            kp, kc, vp, vc = kp_ref[:, ks], kc_ref[:, ks], kp_ref[:, vs], kc_ref[:, vs]
            dkp = jnp.zeros((WINDOW, HEAD_DIM), F32)
            dkc = jnp.zeros((WINDOW, HEAD_DIM), F32)
            dvp = jnp.zeros((WINDOW, HEAD_DIM), F32)
            dvc = jnp.zeros((WINDOW, HEAD_DIM), F32)
            for r in range(_REP):
                h = k * _REP + r
                hs = slice(HEAD_DIM * h, HEAD_DIM * (h + 1))
                qh = q_ref[:, hs]
                doh = do_ref[:, hs]
                pp, pc, ps = _attn_probs(qh, kp, kc, s_ref[:, h:h + 1], first)
                dpp = _dot_nt(doh, vp)
                dpc = _dot_nt(doh, vc)
                delta = jnp.sum(pp * dpp, axis=1, keepdims=True) + jnp.sum(pc * dpc, axis=1, keepdims=True)
                dsp = pp * (dpp - delta) * _SCALE
                dsc = pc * (dpc - delta) * _SCALE
                dq_ref[:, hs] = _dot_nn(dsp, kp) + _dot_nn(dsc, kc)
                dkp = dkp + _dot_tn(dsp, qh)
                dkc = dkc + _dot_tn(dsc, qh)
                dvp = dvp + _dot_tn(pp, doh)
                dvc = dvc + _dot_tn(pc, doh)
                dsink = dsink + jnp.where(lane == h, -jnp.sum(ps * delta, axis=0, keepdims=True), 0.0)
            dp_ref[:, ks] = dkp
            dc_ref[:, ks] = dkc
            dp_ref[:, vs] = dvp
            dc_ref[:, vs] = dvc
        acc_ref[0:1, :] += jnp.sum(dq_ref[...], axis=0, keepdims=True)
        acc_ref[1:2, 0:128] += dsink

    qspec = pl.BlockSpec((WINDOW, D_MODEL), lambda i: (i, 0))
    kspec = pl.BlockSpec((WINDOW, KV_DIM), lambda i: (i, 0))
    return _call(body, "attn_bwd", (nb,),
                 [qspec, kspec, pl.BlockSpec((WINDOW, KV_DIM), lambda i: (jnp.maximum(i - 1, 0), 0)), qspec,
                  pl.BlockSpec((1, 128), lambda i: (0, 0))],
                 (qspec, kspec, kspec, pl.BlockSpec((8, D_MODEL), lambda i: (0, 0))),
                 (_sds((L, D_MODEL), F32), _sds((L, KV_DIM), F32), _sds((L, KV_DIM), F32), _sds((8, D_MODEL), F32)),
                 )(q, kv, kv, do, sinks_pad)


def kv_grad_combine(parts):
    L = parts[0][0].shape[0]
    nb = L // WINDOW
    n = len(parts)

    def body(*refs):
        i = pl.program_id(0)
        o_ref, acc_ref = refs[2 * n], refs[2 * n + 1]

        @pl.when(i == 0)
        def _():
            acc_ref[...] = jnp.zeros_like(acc_ref)

        tot = refs[0][...]
        nxt = refs[1][...]
        for a in range(1, n):
            tot = tot + refs[2 * a][...]
            nxt = nxt + refs[2 * a + 1][...]
        tot = tot + jnp.where(i < nb - 1, nxt, 0.0)
        o_ref[...] = tot
        acc_ref[0:1, :] += jnp.sum(tot, axis=0, keepdims=True)

    cur = pl.BlockSpec((WINDOW, KV_DIM), lambda i: (i, 0))
    nxt = pl.BlockSpec((WINDOW, KV_DIM), lambda i: (jnp.minimum(i + 1, nb - 1), 0))
    args = [t for p in parts for t in p]
    return _call(body, "kv_grad_combine", (nb,), [cur, nxt] * n,
                 (cur, pl.BlockSpec((8, KV_DIM), lambda i: (0, 0))),
                 (_sds((L, KV_DIM), F32), _sds((8, KV_DIM), F32)))(*args)


def mod_fwd(c_all, w, b, name):
    n, _, C = w.shape

    def body(c_ref, w_ref, b_ref, o_ref, ca_ref):
        cv = c_ref[...]
        ca = cv * _sigmoid(cv)
        ca_ref[...] = ca
        o_ref[...] = _dot(ca, w_ref[...], ((1,), (0,))) + b_ref[...]

    return _call(body, name, (n,),
                 [pl.BlockSpec((N_DEV, D_MODEL), lambda i: (0, 0)),
                  pl.BlockSpec((None, D_MODEL, C), lambda i: (i, 0, 0)),
                  pl.BlockSpec((None, 1, C), lambda i: (i, 0, 0))],
                 (pl.BlockSpec((None, N_DEV, C), lambda i: (i, 0, 0)), pl.BlockSpec((N_DEV, D_MODEL), lambda i: (0, 0))),
                 (_sds((n, N_DEV, C), F32), _sds((N_DEV, D_MODEL), F32)))(c_all, w, b)


def mod_wgrad(c_act_t, dmod, name):
    n, _, C = dmod.shape
    tr = 256

    def body(ct_ref, d_ref, o_ref):
        acc = ct_ref[:, 0:1] * d_ref[0:1, :]
        for bidx in range(1, N_DEV):
            acc = acc + ct_ref[:, bidx:bidx + 1] * d_ref[bidx:bidx + 1, :]
        o_ref[...] = acc

    return _call(body, name, (n, D_MODEL // tr),
                 [pl.BlockSpec((tr, N_DEV), lambda i, j: (j, 0)),
                  pl.BlockSpec((None, N_DEV, C), lambda i, j: (i, 0, 0))],
                 pl.BlockSpec((None, tr, C), lambda i, j: (i, j, 0)), _sds((n, D_MODEL, C), F32))(c_act_t, dmod)


def _my_pos():
    return lax.axis_index("x"), lax.axis_index("y"), lax.axis_index("c")


def small_all_gather(v):
    m_per, n = v.shape

    def body(x_ref, out_ref, send_sems, recv_sems, local_sem):
        x, y, c = _my_pos()
        me, sibling = (x, y, c), (x, y, 1 - c)
        chips = [(1 - x, y), (x, 1 - y), (1 - x, 1 - y)]

        def rows(px, py, pc):
            return out_ref.at[pl.ds((4 * px + 2 * py + pc) * m_per, m_per), :]

        def copy(k, block, to, src=None):
            return pltpu.make_async_remote_copy(
                src_ref=rows(*block) if src is None else src, dst_ref=rows(*block),
                send_sem=send_sems.at[k], recv_sem=recv_sems.at[k], device_id=to, device_id_type=MESH)

        mine = pltpu.make_async_copy(x_ref, rows(*me), local_sem)
        mine.start()
        first = [copy(0, me, sibling, src=x_ref)]
        first += [copy(1 + j, me, (*chip, c), src=x_ref) for j, chip in enumerate(chips)]
        for cp in first:
            cp.start()
        passed = [copy(4 + j, (*chip, c), sibling) for j, chip in enumerate(chips)]
        for j, chip in enumerate(chips):
            copy(1 + j, (*chip, c), me).wait_recv()
            passed[j].start()
        copy(0, sibling, me).wait_recv()
        for j, chip in enumerate(chips):
            copy(4 + j, (*chip, 1 - c), me).wait_recv()
        for cp in first + passed:
            cp.wait_send()
        mine.wait()

    return pl.pallas_call(
        body, name="small_all_gather",
        out_shape=_sds((N_DEV * m_per, n), v.dtype),
        in_specs=[pl.BlockSpec(memory_space=pltpu.VMEM)],
        out_specs=pl.BlockSpec(memory_space=pltpu.VMEM),
        scratch_shapes=[pltpu.SemaphoreType.DMA((7,)), pltpu.SemaphoreType.DMA((7,)), pltpu.SemaphoreType.DMA],
        compiler_params=pltpu.CompilerParams(vmem_limit_bytes=VMEM_LIMIT),
    )(v)


def big_all_gather(arrs):
    n = len(arrs)

    def body(*refs):
        ins, outs = refs[:n], refs[n:2 * n]
        send_sems, recv_sems, local_sems = refs[2 * n], refs[2 * n + 1], refs[2 * n + 2]
        x, y, c = _my_pos()
        me, sibling = (x, y, c), (x, y, 1 - c)
        chips = [(1 - x, y), (x, 1 - y), (1 - x, 1 - y)]

        def slot(a, px, py, pc):
            return outs[a].at[4 * px + 2 * py + pc]

        def copy(a, k, block, to, src=None):
            return pltpu.make_async_remote_copy(
                src_ref=slot(a, *block) if src is None else src, dst_ref=slot(a, *block),
                send_sem=send_sems.at[7 * a + k], recv_sem=recv_sems.at[7 * a + k], device_id=to, device_id_type=MESH)

        mine = [pltpu.make_async_copy(ins[a], slot(a, *me), local_sems.at[a]) for a in range(n)]
        for cp in mine:
            cp.start()
        first = []
        for a in range(n):
            first.append(copy(a, 0, me, sibling, src=ins[a]))
            first += [copy(a, 1 + j, me, (*chip, c), src=ins[a]) for j, chip in enumerate(chips)]
        for cp in first:
            cp.start()
        passed = []
        for a in range(n):
            for j, chip in enumerate(chips):
                copy(a, 1 + j, (*chip, c), me).wait_recv()
                fwd = copy(a, 4 + j, (*chip, c), sibling)
                fwd.start()
                passed.append(fwd)
        for a in range(n):
            copy(a, 0, sibling, me).wait_recv()
            for j, chip in enumerate(chips):
                copy(a, 4 + j, (*chip, 1 - c), me).wait_recv()
        for cp in first + passed:
            cp.wait_send()
        for cp in mine:
            cp.wait()

    hbm = pl.BlockSpec(memory_space=pltpu.HBM)
    return pl.pallas_call(
        body, name="big_all_gather",
        out_shape=[_sds((N_DEV,) + a.shape, a.dtype) for a in arrs],
        in_specs=[hbm] * n, out_specs=[hbm] * n,
        scratch_shapes=[pltpu.SemaphoreType.DMA((7 * n,)), pltpu.SemaphoreType.DMA((7 * n,)),
                        pltpu.SemaphoreType.DMA((n,))],
    )(*arrs)


_FLIPS =[(fx, fy, fc) for fx in (0, 1) for fy in (0, 1) for fc in (0, 1)][1:]
_HBM = pl.BlockSpec(memory_space=pltpu.HBM)
_SEM = pl.BlockSpec(memory_space=pltpu.SEMAPHORE)
_EFFECT = pltpu.SideEffectType.DATAFLOW_SIDE_EFFECTING


def _flip(x, y, c, f):
    return (1 - x if f[0] else x), (1 - y if f[1] else y), (1 - c if f[2] else c)


def _xfer_copies(srcs, lands, send_sems, recv_sems, scatter):
    x, y, c = _my_pos()
    me = 4 * x + 2 * y + c
    copies = []
    for a in range(len(srcs)):
        for k, f in enumerate(_FLIPS):
            px, py, pc = _flip(x, y, c, f)
            src = srcs[a].at[4 * px + 2 * py + pc] if scatter else srcs[a]
            copies.append(pltpu.make_async_remote_copy(
                src_ref=src, dst_ref=lands[a].at[me], send_sem=send_sems.at[7 * a + k],
                recv_sem=recv_sems.at[7 * a + k], device_id=(px, py, pc), device_id_type=MESH))
    return copies


def xfer_start(arrs, scatter, after, name):
    n = len(arrs)
    land_shapes = [a.shape if scatter else (N_DEV,) + a.shape for a in arrs]

    def body(*refs):
        srcs, lands = refs[:n], refs[n:2 * n]
        send_sems, recv_sems = refs[2 * n + 1], refs[2 * n + 2]
        token = refs[-1]
        for cp in _xfer_copies(srcs, lands, send_sems, recv_sems, scatter):
            cp.start()
        token[...] = jnp.zeros_like(token)

    out = pl.pallas_call(
        body, name=name,
        out_shape=(pltpu.SemaphoreType.DMA((7 * n,)), pltpu.SemaphoreType.DMA((7 * n,)),
                   *[pltpu.HBM(a.shape, a.dtype) for a in arrs],
                   *[pltpu.HBM(s, a.dtype) for s, a in zip(land_shapes, arrs)],
                   _sds((8, 128), F32)),
        in_specs=[_HBM] * (2 * n) + [pl.BlockSpec(memory_space=pl.ANY)],
        out_specs=(_SEM, _SEM, *([_HBM] * (2 * n)), pl.BlockSpec(memory_space=pltpu.VMEM)),
        input_output_aliases={i: 2 + i for i in range(2 * n)},
        compiler_params=pltpu.CompilerParams(has_side_effects=_EFFECT),
    )(*[pltpu.with_memory_space_constraint(a, pltpu.HBM) for a in arrs],
      *[pltpu.with_memory_space_constraint(lax.empty(s, a.dtype), pltpu.HBM) for s, a in zip(land_shapes, arrs)],
      after)
    return (out[0], out[1], list(out[2:2 + n]), list(out[2 + n:2 + 2 * n]), scatter), out[-1]


def xfer_wait(handle, after, name):
    send_sems, recv_sems, srcs, lands, scatter = handle
    n = len(srcs)

    def body(*refs):
        srcs_r, lands_r = refs[:n], refs[n:2 * n]
        ssem, rsem = refs[2 * n], refs[2 * n + 1]
        for cp in _xfer_copies(srcs_r, lands_r, ssem, rsem, scatter):
            cp.wait_send()
            cp.wait_recv()

    out = pl.pallas_call(
        body, name=name,
        out_shape=(*[pltpu.HBM(a.shape, a.dtype) for a in srcs], *[pltpu.HBM(a.shape, a.dtype) for a in lands]),
        in_specs=[_HBM] * (2 * n) + [_SEM, _SEM, pl.BlockSpec(memory_space=pl.ANY)],
        out_specs=tuple([_HBM] * (2 * n)),
        input_output_aliases={i: i for i in range(2 * n)},
        compiler_params=pltpu.CompilerParams(has_side_effects=_EFFECT),
    )(*srcs, *lands, send_sems, recv_sems, after)
    return list(out[:n]), list(out[n:])


def adamw(parts, w, m, v, name):
    R, C = w.shape
    n_parts = parts.shape[0]
    row_bytes = 2 * (n_parts * C * parts.dtype.itemsize + 7 * C * 4)
    tr = R
    for cand in (512, 256, 128, 64, 32, 16):
        if R % cand == 0 and R > cand and cand * row_bytes <= ADAMW_VMEM_BUDGET:
            tr = cand
            break
    c1 = 1.0 / (1.0 - ADAM_B1 ** ADAM_STEP)
    c2 = 1.0 / (1.0 - ADAM_B2 ** ADAM_STEP)

    def body(p_ref, w_ref, m_ref, v_ref, g_ref, d_ref, nm_ref, nv_ref):
        g = p_ref[0].astype(F32)
        for k in range(1, n_parts):
            g = g + p_ref[k].astype(F32)
        nm = ADAM_B1 * m_ref[...] + (1.0 - ADAM_B1) * g
        nv = ADAM_B2 * v_ref[...] + (1.0 - ADAM_B2) * (g * g)
        g_ref[...] = g
        nm_ref[...] = nm
        nv_ref[...] = nv
        d_ref[...] = -ADAM_LR * ((nm * c1) / (jnp.sqrt(nv * c2) + ADAM_EPS) + ADAM_WD * w_ref[...])

    blk = pl.BlockSpec((tr, C), lambda i: (i, 0))
    return _call(body, name, (R // tr,), [pl.BlockSpec((n_parts, tr, C), lambda i: (0, i, 0)), blk, blk, blk],
                 (blk, blk, blk, blk), tuple(_sds((R, C), F32) for _ in range(4)))(parts, w, m, v)


def _ffn_fwd(x, nw, sh, sc, g, w_gu, w_dn):
    h = norm_mod(x, nw, sh, sc, "ffn_norm")
    gu = matmul(h, w_gu, "nn", BF, "ffn_gu")
    a = swiglu(gu)
    y, xn = matmul(a, w_dn, "nn", BF, "ffn_down", res=x, gate=g, coef=0.5)
    return xn, (x, h, gu, a, y)


def _ffn_bwd(dxo, saved, nw, sc, g, w_gu, w_dn):
    x, h, gu, a, y = saved
    dy, acc1 = resid_gate_bwd(dxo, y, g, 0.5, "ffn_gate_bwd")
    d_wdn = matmul(a, dy, "tn", BF, "ffn_down_wgrad")
    da = matmul(dy, w_dn, "nt", BF, "ffn_down_dgrad")
    dgu = swiglu_bwd(da, gu)
    d_wgu = matmul(h, dgu, "tn", BF, "ffn_gu_wgrad")
    dh = matmul(dgu, w_gu, "nt", F32, "ffn_gu_dgrad")
    dx, acc2 = norm_mod_bwd(x, dh, dxo, nw, sc, "ffn_norm_bwd")
    return dx, d_wgu, d_wdn, (acc2[0], acc2[1], acc1[0]), acc2[2]


def _group_layout(a):
    L = a.shape[0]
    return a[:, :SSM_HEADS].reshape(L, SSM_GROUPS, _HPG).transpose(1, 0, 2)


def _ungroup_layout(a):
    L = a.shape[1]
    return jnp.pad(a.transpose(1, 0, 2).reshape(L, SSM_HEADS), ((0, 0), (0, 128 - SSM_HEADS)))


def _pad_row(vec, n=128):
    return jnp.pad(vec.reshape(1, -1), ((0, 0), (0, n - vec.shape[-1])))


def _mamba_fwd(x, nw, sh, sc, g, p):
    h = norm_mod(x, nw, sh, sc, "mix_norm")
    zx = matmul(h, p["w_in"], "nn", F32, "ssm_in")
    xc = conv_fwd(zx, p["conv_w"], p["conv_b"])
    dt, acs = dt_prep(zx, p["dt_bias"], p["a_log"])
    dt_g, acs_g = _group_layout(dt), _group_layout(acs)
    acs_t = acs_g.transpose(0, 2, 1)
    y, pst = ssd_fwd(xc, dt_g, acs_g, acs_t, p["d_exp"])
    yn = gate_norm(y, zx, p["norm_w"])
    yo, xn = matmul(yn, p["w_out"], "nn", BF, "ssm_out", res=x, gate=g, coef=1.0)
    return xn, (x, h, zx, xc, dt, dt_g, acs_g, acs_t, y, pst, yn, yo)


def _mamba_bwd(dxo, saved, nw, sc, g, p):
    x, h, zx, xc, dt, dt_g, acs_g, acs_t, y, pst, yn, yo = saved
    dyo, acc1 = resid_gate_bwd(dxo, yo, g, 1.0, "mix_gate_bwd")
    d_wout = matmul(yn, dyo, "tn", BF, "ssm_out_wgrad")
    dyn = matmul(dyo, p["w_out"], "nt", F32, "ssm_out_dgrad")
    dy, dz, accn = gate_norm_bwd(dyn, y, zx, p["norm_w"])
    dxs, dB, dC, ddt_g, da_g, dd = ssd_bwd(dy, xc, dt_g, acs_g, acs_t, pst, p["d_exp"])
    dxc = jnp.concatenate([dxs, dB, dC], axis=1)
    ds = conv_bwd_act(dxc, zx, p["conv_w"], p["conv_b"])
    du, accc = conv_bwd(ds, zx, p["conv_w"])
    draw, accdt = dt_bwd(_ungroup_layout(ddt_g), _ungroup_layout(da_g), dt, zx, p["dt_bias"], p["a_log"])
    dzx = jnp.concatenate([dz, du, draw], axis=1)
    d_win = matmul(h, dzx, "tn", BF, "ssm_in_wgrad")[:, :IN_PROJ]
    dh = matmul(dzx, p["w_in"], "nt", F32, "ssm_in_dgrad")
    dx, acc2 = norm_mod_bwd(x, dh, dxo, nw, sc, "mix_norm_bwd")
    small = dict(conv_w=accc[:CONV_WIDTH], conv_b=accc[CONV_WIDTH], dt_bias=accdt[0, :SSM_HEADS],
                 a_log=accdt[1, :SSM_HEADS], d=dd.reshape(SSM_HEADS, SSM_HEADDIM).sum(-1), norm_w=accn[0])
    return dx, d_win, d_wout, (acc2[0], acc2[1], acc1[0]), acc2[2], small


def _attn_layer_fwd(x, nw, sh, sc, g, p, kv):
    h = norm_mod(x, nw, sh, sc, "mix_norm")
    q = matmul(h, p["w_q"], "nn", F32, "attn_q", bias=p["b_q"])
    o = attn_fwd(q, kv, p["sinks"])
    yo, xn = matmul(o, p["w_o"], "nn", BF, "attn_o", bias=p["b_o"], res=x, gate=g, coef=1.0)
    return xn, (x, h, q, o, yo)


def _attn_layer_bwd(dxo, saved, nw, sc, g, p, kv):
    x, h, q, o, yo = saved
    dyo, acc1 = resid_gate_bwd(dxo, yo, g, 1.0, "mix_gate_bwd")
    d_wo = matmul(o, dyo, "tn", BF, "attn_o_wgrad")
    do = matmul(dyo, p["w_o"], "nt", F32, "attn_o_dgrad")
    dq, dkv_c, dkv_p, acca = attn_bwd(q, kv, do, p["sinks"])
    d_wq = matmul(h, dq, "tn", BF, "attn_q_wgrad")
    dh = matmul(dq, p["w_q"], "nt", F32, "attn_q_dgrad")
    dx, acc2 = norm_mod_bwd(x, dh, dxo, nw, sc, "mix_norm_bwd")
    small = dict(b_q=acca[0], sinks=acca[1, :ATT_HEADS], b_o=acc1[1])
    return dx, d_wq, d_wo, (acc2[0], acc2[1], acc1[0]), acc2[2], small, (dkv_c, dkv_p)


def _pack_rows(pieces):
    rows, spans, off = [], [], 0
    for a in pieces:
        flat = a.reshape(-1).astype(F32)
        n = -(-flat.shape[0] // D_MODEL)
        rows.append(jnp.pad(flat, (0, n * D_MODEL - flat.shape[0])).reshape(n, D_MODEL))
        spans.append((off, a.shape))
        off += n
    pad = -off % 8
    if pad:
        rows.append(jnp.zeros((pad, D_MODEL), F32))
    return jnp.concatenate(rows, axis=0), spans, off + pad


def _unpack_rows(g, spans):
    out = []
    for off, shape in spans:
        size = 1
        for s in shape:
            size *= s
        n = -(-size // D_MODEL)
        out.append(g[:, off:off + n].reshape(N_DEV, n * D_MODEL)[:, :size].reshape((N_DEV,) + tuple(shape)))
    return out


def _unshard_last(g):
    nd = g.ndim
    perm = tuple(range(1, nd - 1)) + (0, nd - 1)
    t = g.transpose(perm)
    return t.reshape(t.shape[:-2] + (N_DEV * g.shape[-1],))


def _shard_last(a, me):
    s = a.shape[-1] // N_DEV
    return lax.dynamic_slice_in_dim(a, me * s, s, axis=a.ndim - 1)


def kernel(x, c, ffn_norm_w, ffn_w_gu, ffn_w_down, mod_w, mod_b, mix_norm_w, ssm_w_in, ssm_conv_w, ssm_conv_b, ssm_dt_bias, ssm_a_log, ssm_d, ssm_norm_w, ssm_w_out, kv_norm_w, kv_mod_w, kv_mod_b, w_kv, b_kv, attn_w_q, attn_b_q, attn_sinks, attn_w_o, attn_b_o, final_norm_w, loss_target, m_ffn_norm_w, m_ffn_w_gu, m_ffn_w_down, m_mod_w, m_mod_b, m_mix_norm_w, m_ssm_w_in, m_ssm_conv_w, m_ssm_conv_b, m_ssm_dt_bias, m_ssm_a_log, m_ssm_d, m_ssm_norm_w, m_ssm_w_out, m_kv_norm_w, m_kv_mod_w, m_kv_mod_b, m_w_kv, m_b_kv, m_attn_w_q, m_attn_b_q, m_attn_sinks, m_attn_w_o, m_attn_b_o, m_final_norm_w, v_ffn_norm_w, v_ffn_w_gu, v_ffn_w_down, v_mod_w, v_mod_b, v_mix_norm_w, v_ssm_w_in, v_ssm_conv_w, v_ssm_conv_b, v_ssm_dt_bias, v_ssm_a_log, v_ssm_d, v_ssm_norm_w, v_ssm_w_out, v_kv_norm_w, v_kv_mod_w, v_kv_mod_b, v_w_kv, v_b_kv, v_attn_w_q, v_attn_b_q, v_attn_sinks, v_attn_w_o, v_attn_b_o, v_final_norm_w):
    D = D_MODEL
    me = 4 * lax.axis_index("x") + 2 * lax.axis_index("y") + lax.axis_index("c")
    xs = x[0]
    target = loss_target[0]
    mod_cols = mod_w.shape[-1]
    kvm_cols = kv_mod_w.shape[-1]

    packed, spans, _ = _pack_rows([c, ffn_norm_w, ssm_conv_w, ssm_conv_b, ssm_norm_w])
    nrow = packed.shape[0]
    g1 = small_all_gather(packed).reshape(N_DEV, nrow, D)
    c_all, fnw_g, cw_g, cb_g, snw_g = _unpack_rows(g1, spans)
    c_all = c_all.reshape(N_DEV, D)
    ffn_nw = _unshard_last(fnw_g)
    conv_w = _unshard_last(cw_g)
    conv_b = _unshard_last(cb_g)
    ssm_nw = _unshard_last(snw_g)

    mod_b_loc = lax.dynamic_slice_in_dim(mod_b, me * mod_cols, mod_cols, axis=1).reshape(DEPTH, 1, mod_cols)
    kvb_loc = lax.dynamic_slice_in_dim(kv_mod_b, me * kvm_cols, kvm_cols, axis=0).reshape(1, 1, kvm_cols)
    modp, c_act = mod_fwd(c_all, mod_w, mod_b_loc, "mod_fwd")
    kvmp, _ = mod_fwd(c_all, kv_mod_w.reshape(1, D, kvm_cols), kvb_loc, "kv_mod_fwd")
    packed2, spans2, _ = _pack_rows([modp, kvmp])
    nrow2 = packed2.shape[0]
    g2 = small_all_gather(packed2).reshape(N_DEV, nrow2, D)
    modp_g, kvmp_g = _unpack_rows(g2, spans2)
    mod_all = modp_g.transpose(1, 2, 0, 3).reshape(DEPTH, N_DEV, N_MOD * D)
    kvm_all = kvmp_g.transpose(1, 2, 0, 3).reshape(N_DEV, 2 * D)
    mod_me = lax.dynamic_index_in_dim(mod_all, me, axis=1, keepdims=False).reshape(DEPTH, N_MOD, 1, D)
    kvm_me = lax.dynamic_index_in_dim(kvm_all, me, axis=0, keepdims=False).reshape(2, 1, D)

    r_dn = ffn_w_down.shape[2]
    r_mix = ssm_w_out.shape[1]
    r_at = attn_w_q.shape[1]

    def layer_pack(k):
        if k < N_A:
            mix = [ssm_w_out[k].astype(BF)]
        else:
            mix = [attn_w_q[k - N_A].astype(BF), attn_w_o[k - N_A].astype(BF)]
        arrs = [ffn_w_gu[k].astype(BF).reshape(2 * D, -1),
                jnp.concatenate([ffn_w_down[k].astype(BF).reshape(-1, D)] + mix, axis=0)]
        if k < N_A:
            arrs.append(ssm_w_in[k].astype(BF))
        if k == N_A:
            arrs.append(w_kv.astype(BF))
        return arrs

    def finish(handle, after, name):
        srcs, lands = xfer_wait(handle, after, name)
        if handle[4]:
            srcs = [lax.dynamic_index_in_dim(s, me, 0, keepdims=False) for s in srcs]
        return [lax.dynamic_update_index_in_dim(l, s, me, 0) for l, s in zip(lands, srcs)]

    packs = [layer_pack(k) for k in range(DEPTH)]
    gathered = [None] * DEPTH
    gathered[0] = big_all_gather(packs[0])
    pending, tok_next = xfer_start(packs[1], False, gathered[0][1], "gather_start_1")

    def w_gu_full(i, j):
        return gathered[i][0].reshape(N_DEV, 2, D, -1)[:, j].transpose(1, 0, 2).reshape(D, 2 * D_FF)

    def w_dn_full(i, j):
        return gathered[i][1][:, j * r_dn:(j + 1) * r_dn].reshape(D_FF, D)

    def mix_rows(i, lo, r):
        return gathered[i][1][:, 2 * r_dn + lo:2 * r_dn + lo + r].reshape(N_DEV * r, D)

    def mamba_params(j):
        w_in = jnp.pad(gathered[j][2].transpose(1, 0, 2).reshape(D, IN_PROJ), ((0, 0), (0, IN_PROJ_PAD - IN_PROJ)))
        return dict(w_in=w_in, w_out=mix_rows(j, 0, r_mix), conv_w=conv_w[j], conv_b=conv_b[j].reshape(1, -1),
                    dt_bias=_pad_row(ssm_dt_bias[j]), a_log=_pad_row(ssm_a_log[j]),
                    d_exp=jnp.repeat(ssm_d[j], SSM_HEADDIM).reshape(SSM_GROUPS, 1, _GW),
                    norm_w=ssm_nw[j].reshape(1, -1))

    def attn_params(j):
        return dict(w_q=mix_rows(N_A + j, 0, r_at), w_o=mix_rows(N_A + j, r_at, r_at),
                    b_q=attn_b_q[j].reshape(1, -1), b_o=attn_b_o[j].reshape(1, -1), sinks=_pad_row(attn_sinks[j]))

    saved = []
    kv = None
    kv_saved = None
    w_kv_full = None
    xcur = xs
    for i in range(DEPTH):
        if i >= 1:
            gathered[i] = finish(pending, xcur, "gather_wait_%d" % i)
            if i + 1 < DEPTH:
                pending, tok_next = xfer_start(packs[i + 1], False, xcur, "gather_start_%d" % (i + 1))
        md = mod_me[i]
        if i + 1 < DEPTH:
            md = md + tok_next[0, 0]
        if i == N_A:
            w_kv_full = gathered[N_A][2].reshape(D, KV_DIM)
            h_kv = norm_mod(xcur, kv_norm_w.reshape(1, D), kvm_me[0], kvm_me[1], "kv_norm")
            kv = matmul(h_kv, w_kv_full, "nn", F32, "kv_proj", bias=b_kv.reshape(1, -1))
            kv_saved = (xcur, h_kv)
        x1, s1 = _ffn_fwd(xcur, ffn_nw[i, 0].reshape(1, D), md[0], md[1], md[2], w_gu_full(i, 0), w_dn_full(i, 0))
        if i < N_A:
            pm = mamba_params(i)
            x2, s2 = _mamba_fwd(x1, mix_norm_w[i].reshape(1, D), md[3], md[4], md[5], pm)
        else:
            pm = attn_params(i - N_A)
            x2, s2 = _attn_layer_fwd(x1, mix_norm_w[i].reshape(1, D), md[3], md[4], md[5], pm, kv)
        x3, s3 = _ffn_fwd(x2, ffn_nw[i, 1].reshape(1, D), md[6], md[7], md[8], w_gu_full(i, 1), w_dn_full(i, 1))
        saved.append((s1, s2, s3, pm))
        xcur = x3

    dx, accf = final_loss(xcur, final_norm_w.reshape(1, D), target)
    d_gu = [[None, None] for _ in range(DEPTH)]
    d_dn = [[None, None] for _ in range(DEPTH)]
    d_mod = [None] * DEPTH
    d_ffn_nw = [[None, None] for _ in range(DEPTH)]
    d_mix_nw = [None] * DEPTH
    d_in, d_out, d_q, d_o = [None] * N_A, [None] * N_A, [None] * N_A, [None] * N_A
    sm_m, sm_a = [None] * N_A, [None] * N_A
    kv_parts = [None] * N_A
    d_kv_w = d_kvm = d_kv_nw = d_bkv = None
    exch = [None] * DEPTH
    tok_prev = None
    for i in reversed(range(DEPTH)):
        md = mod_me[i]
        if tok_prev is not None:
            md = md + tok_prev[0, 0]
        s1, s2, s3, pm = saved[i]
        dx, d_gu[i][1], d_dn[i][1], m2, d_ffn_nw[i][1] = _ffn_bwd(
            dx, s3, ffn_nw[i, 1].reshape(1, D), md[7], md[8], w_gu_full(i, 1), w_dn_full(i, 1))
        if i < N_A:
            dx, d_in[i], d_out[i], mm_, d_mix_nw[i], sm_m[i] = _mamba_bwd(
                dx, s2, mix_norm_w[i].reshape(1, D), md[4], md[5], pm)
        else:
            j = i - N_A
            dx, d_q[j], d_o[j], mm_, d_mix_nw[i], sm_a[j], kv_parts[j] = _attn_layer_bwd(
                dx, s2, mix_norm_w[i].reshape(1, D), md[4], md[5], pm, kv)
        dx, d_gu[i][0], d_dn[i][0], m1, d_ffn_nw[i][0] = _ffn_bwd(
            dx, s1, ffn_nw[i, 0].reshape(1, D), md[1], md[2], w_gu_full(i, 0), w_dn_full(i, 0))
        d_mod[i] = jnp.concatenate(list(m1) + list(mm_) + list(m2), axis=0)
        if i == N_A:
            x_kv, h_kv = kv_saved
            dkv, acck = kv_grad_combine(kv_parts)
            d_bkv = acck[0]
            d_kv_w = matmul(h_kv, dkv, "tn", BF, "kv_wgrad")
            dh_kv = matmul(dkv, w_kv_full, "nt", F32, "kv_dgrad")
            dx, acc_kv = norm_mod_bwd(x_kv, dh_kv, dx, kv_norm_w.reshape(1, D), kvm_me[1], "kv_norm_bwd")
            d_kvm = jnp.concatenate([acc_kv[0], acc_kv[1]], axis=0)
            d_kv_nw = acc_kv[2]
        mixg = [d_out[i]] if i < N_A else [d_q[i - N_A], d_o[i - N_A]]
        e = [jnp.stack(d_gu[i], 0).reshape(2 * D, N_DEV, -1).transpose(1, 0, 2),
             jnp.concatenate([t.reshape(N_DEV, -1, D) for t in d_dn[i] + mixg], axis=1)]
        if i < N_A:
            e.append(d_in[i].reshape(D, N_DEV, -1).transpose(1, 0, 2))
        if i == N_A:
            e.append(d_kv_w.reshape(N_DEV, -1, KV_DIM))
        exch[i], tok_prev = xfer_start(e, True, dx, "exch_start_%d" % i)
    grad_x = dx.reshape(x.shape)

    recv = [None] * DEPTH
    for i in reversed(range(DEPTH)):
        recv[i] = finish(exch[i], dx, "exch_wait_%d" % i)
    r_gu = jnp.concatenate([recv[i][0] for i in range(DEPTH)], axis=1)
    r_dn_all = jnp.concatenate([recv[i][1][:, :2 * r_dn] for i in range(DEPTH)], axis=1)
    r_out_all = jnp.concatenate([recv[i][1][:, 2 * r_dn:] for i in range(N_A)], axis=1)
    r_q_all = jnp.concatenate([recv[N_A + j][1][:, 2 * r_dn:2 * r_dn + r_at] for j in range(N_A)], axis=1)
    r_o_all = jnp.concatenate([recv[N_A + j][1][:, 2 * r_dn + r_at:] for j in range(N_A)], axis=1)
    r_in = jnp.concatenate([recv[i][2] for i in range(N_A)], axis=1)
    r_kv = recv[N_A][2]

    small_list = [
        jnp.stack(d_mod, 0), d_kvm,
        jnp.stack([jnp.stack(r, 0) for r in d_ffn_nw], 0),
        jnp.stack(d_mix_nw, 0),
        jnp.stack([s["conv_w"] for s in sm_m], 0), jnp.stack([s["conv_b"] for s in sm_m], 0),
        jnp.stack([s["dt_bias"] for s in sm_m], 0), jnp.stack([s["a_log"] for s in sm_m], 0),
        jnp.stack([s["d"] for s in sm_m], 0), jnp.stack([s["norm_w"] for s in sm_m], 0),
        d_kv_nw, d_bkv,
        jnp.stack([s["b_q"] for s in sm_a], 0), jnp.stack([s["sinks"] for s in sm_a], 0),
        jnp.stack([s["b_o"] for s in sm_a], 0), accf[0], accf[1],
    ]
    packed3, spans3, _ = _pack_rows(small_list)
    nrow3 = packed3.shape[0]
    g3 = small_all_gather(packed3).reshape(N_DEV, nrow3, D)
    (p_mod, p_kvm, p_fnw, p_mnw, p_cw, p_cb, p_dtb, p_al, p_d, p_snw, p_kvnw, p_bkv, p_bq, p_sk, p_bo, p_fin,
     p_loss) = _unpack_rows(g3, spans3)

    loss = 0.5 / D * jnp.sum(p_loss)

    c_act_t = c_act.T
    dmod_loc = _shard_last(p_mod, me).transpose(1, 0, 2)
    dkvm_loc = _shard_last(p_kvm, me).reshape(1, N_DEV, kvm_cols)
    gp_mod_w = mod_wgrad(c_act_t, dmod_loc, "mod_wgrad")
    gp_kvm_w = mod_wgrad(c_act_t, dkvm_loc, "kv_mod_wgrad")[0]

    def as_parts_single(a):
        return a[None]

    def upd(name, parts, w, m, v):
        shp = w.shape
        c_last = shp[-1]
        out = adamw(parts.reshape(parts.shape[0], -1, c_last), w.reshape(-1, c_last), m.reshape(-1, c_last),
                    v.reshape(-1, c_last), "adamw_" + name)
        return tuple(o.reshape(shp) for o in out)

    res = {}
    res["ffn_norm_w"] = upd("ffn_norm_w", _shard_last(p_fnw, me), ffn_norm_w, m_ffn_norm_w, v_ffn_norm_w)
    res["ffn_w_gu"] = upd("ffn_w_gu", r_gu, ffn_w_gu, m_ffn_w_gu, v_ffn_w_gu)
    res["ffn_w_down"] = upd("ffn_w_down", r_dn_all, ffn_w_down, m_ffn_w_down, v_ffn_w_down)
    res["mod_w"] = upd("mod_w", as_parts_single(gp_mod_w), mod_w, m_mod_w, v_mod_w)
    res["mod_b"] = upd("mod_b", p_mod, mod_b, m_mod_b, v_mod_b)
    res["mix_norm_w"] = upd("mix_norm_w", p_mnw, mix_norm_w, m_mix_norm_w, v_mix_norm_w)
    res["ssm_w_in"] = upd("ssm_w_in", r_in, ssm_w_in, m_ssm_w_in, v_ssm_w_in)
    res["ssm_conv_w"] = upd("ssm_conv_w", _shard_last(p_cw, me), ssm_conv_w, m_ssm_conv_w, v_ssm_conv_w)
    res["ssm_conv_b"] = upd("ssm_conv_b", _shard_last(p_cb, me), ssm_conv_b, m_ssm_conv_b, v_ssm_conv_b)
    res["ssm_dt_bias"] = upd("ssm_dt_bias", p_dtb, ssm_dt_bias, m_ssm_dt_bias, v_ssm_dt_bias)
    res["ssm_a_log"] = upd("ssm_a_log", p_al, ssm_a_log, m_ssm_a_log, v_ssm_a_log)
    res["ssm_d"] = upd("ssm_d", p_d, ssm_d, m_ssm_d, v_ssm_d)
    res["ssm_norm_w"] = upd("ssm_norm_w", _shard_last(p_snw, me), ssm_norm_w, m_ssm_norm_w, v_ssm_norm_w)
    res["ssm_w_out"] = upd("ssm_w_out", r_out_all, ssm_w_out, m_ssm_w_out, v_ssm_w_out)
    res["kv_norm_w"] = upd("kv_norm_w", p_kvnw.reshape(N_DEV, 1, D), kv_norm_w.reshape(1, D),
                           m_kv_norm_w.reshape(1, D), v_kv_norm_w.reshape(1, D))
    res["kv_mod_w"] = upd("kv_mod_w", as_parts_single(gp_kvm_w), kv_mod_w, m_kv_mod_w, v_kv_mod_w)
    res["kv_mod_b"] = upd("kv_mod_b", p_kvm.reshape(N_DEV, 1, 2 * D), kv_mod_b.reshape(1, -1),
                          m_kv_mod_b.reshape(1, -1), v_kv_mod_b.reshape(1, -1))
    res["w_kv"] = upd("w_kv", r_kv, w_kv, m_w_kv, v_w_kv)
    res["b_kv"] = upd("b_kv", p_bkv.reshape(N_DEV, 1, KV_DIM), b_kv.reshape(1, -1), m_b_kv.reshape(1, -1),
                      v_b_kv.reshape(1, -1))
    res["attn_w_q"] = upd("attn_w_q", r_q_all, attn_w_q, m_attn_w_q, v_attn_w_q)
    res["attn_b_q"] = upd("attn_b_q", p_bq, attn_b_q, m_attn_b_q, v_attn_b_q)
    res["attn_sinks"] = upd("attn_sinks", p_sk, attn_sinks, m_attn_sinks, v_attn_sinks)
    res["attn_w_o"] = upd("attn_w_o", r_o_all, attn_w_o, m_attn_w_o, v_attn_w_o)
    res["attn_b_o"] = upd("attn_b_o", p_bo, attn_b_o, m_attn_b_o, v_attn_b_o)
    res["final_norm_w"] = upd("final_norm_w", p_fin.reshape(N_DEV, 1, D), final_norm_w.reshape(1, D),
                              m_final_norm_w.reshape(1, D), v_final_norm_w.reshape(1, D))

    names = ["ffn_norm_w", "ffn_w_gu", "ffn_w_down", "mod_w", "mod_b", "mix_norm_w", "ssm_w_in", "ssm_conv_w",
             "ssm_conv_b", "ssm_dt_bias", "ssm_a_log", "ssm_d", "ssm_norm_w", "ssm_w_out", "kv_norm_w", "kv_mod_w",
             "kv_mod_b", "w_kv", "b_kv", "attn_w_q", "attn_b_q", "attn_sinks", "attn_w_o", "attn_b_o", "final_norm_w"]
    vec_shapes = {"kv_norm_w": (D,), "kv_mod_b": (2 * D,), "b_kv": (KV_DIM,), "final_norm_w": (D,)}
    outs = [loss, grad_x]
    for k in range(4):
        for nme in names:
            t = res[nme][k]
            if nme in vec_shapes:
                t = t.reshape(vec_shapes[nme])
            outs.append(t)
    return tuple(outs)
```

```python
import functools

import jax
import jax.numpy as jnp
from jax import lax
from jax.experimental import pallas as pl
from jax.experimental.pallas import tpu as pltpu

F32 = jnp.float32
BF = jnp.bfloat16
MESH = pl.DeviceIdType.MESH

N_DEV = 8
D_MODEL = 1024
DEPTH = 4
N_A = 2
EPS = 1e-5
N_MOD = 9
D_FF = 2816
D_INNER = 2048
SSM_HEADDIM = 64
SSM_HEADS = 32
SSM_GROUPS = 8
SSM_STATE = 128
CONV_WIDTH = 4
CHUNK = 128
CONV_DIM = D_INNER + 2 * SSM_GROUPS * SSM_STATE
IN_PROJ = D_INNER + CONV_DIM + SSM_HEADS
IN_PROJ_PAD = D_INNER + CONV_DIM + 128
ATT_HEADS = 16
KV_HEADS = 4
HEAD_DIM = 64
WINDOW = 128
KV_DIM = 2 * KV_HEADS * HEAD_DIM

ADAM_LR = 0.001
ADAM_B1 = 0.9
ADAM_B2 = 0.999
ADAM_EPS = 1e-08
ADAM_WD = 0.01
ADAM_STEP = 10

VMEM_LIMIT = 48 * 2 ** 20
ADAMW_VMEM_BUDGET = 24 * 2 ** 20
NEG = -1e30


def _call(body, name, grid, in_specs, out_specs, out_shape, scratch=()):
    return pl.pallas_call(
        body, name=name, grid=grid, in_specs=in_specs, out_specs=out_specs, out_shape=out_shape,
        scratch_shapes=list(scratch),
        compiler_params=pltpu.CompilerParams(vmem_limit_bytes=VMEM_LIMIT))


def _tile(n, cap):
    t = (cap // 128) * 128
    while t >= 128:
        if n % t == 0:
            return t
        t -= 128
    return n


def _sds(shape, dtype):
    return jax.ShapeDtypeStruct(shape, dtype)


def _sigmoid(v):
    return 1.0 / (1.0 + jnp.exp(-v))


def _dot(a, b, dims):
    return lax.dot_general(a, b, (dims, ((), ())), preferred_element_type=F32)


def _dot_nn(a, b):
    return _dot(a.astype(BF), b.astype(BF), ((1,), (0,)))


def _dot_nt(a, b):
    return _dot(a.astype(BF), b.astype(BF), ((1,), (1,)))


def _dot_tn(a, b):
    return _dot(a.astype(BF), b.astype(BF), ((0,), (0,)))


def matmul(a, b, mode, out_dtype, name, bias=None, res=None, gate=None, coef=1.0):
    if mode == "nn":
        (M, K), (_, N) = a.shape, b.shape
    elif mode == "nt":
        (M, K), (N, _) = a.shape, b.shape
    else:
        (K, M), (_, N) = a.shape, b.shape
    cap_n = 512 if K > 4096 else 1024
    tm = _tile(M, 512)
    tn = _tile(N, cap_n)
    if mode == "nn":
        a_spec = pl.BlockSpec((tm, K), lambda i, j: (i, 0))
        b_spec = pl.BlockSpec((K, tn), lambda i, j: (0, j))
        fn = _dot_nn
    elif mode == "nt":
        a_spec = pl.BlockSpec((tm, K), lambda i, j: (i, 0))
        b_spec = pl.BlockSpec((tn, K), lambda i, j: (j, 0))
        fn = _dot_nt
    else:
        a_spec = pl.BlockSpec((K, tm), lambda i, j: (0, i))
        b_spec = pl.BlockSpec((K, tn), lambda i, j: (0, j))
        fn = _dot_tn
    has_bias, has_res = bias is not None, res is not None
    o_spec = pl.BlockSpec((tm, tn), lambda i, j: (i, j))
    v_spec = pl.BlockSpec((1, tn), lambda i, j: (0, j))
    in_specs, args = [a_spec, b_spec], [a, b]
    if has_bias:
        in_specs.append(v_spec)
        args.append(bias)
    if has_res:
        in_specs += [o_spec, v_spec]
        args += [res, gate]

    def body(*refs):
        a_ref, b_ref = refs[0], refs[1]
        k = 2
        y = fn(a_ref[...], b_ref[...])
        if has_bias:
            y = y + refs[k][...]
            k += 1
        if has_res:
            res_ref, gate_ref = refs[k], refs[k + 1]
            refs[k + 2][...] = y.astype(out_dtype)
            refs[k + 3][...] = res_ref[...] + coef * gate_ref[...] * y
        else:
            refs[k][...] = y.astype(out_dtype)

    if has_res:
        out_shape = (_sds((M, N), out_dtype), _sds((M, N), F32))
        out_specs = (o_spec, o_spec)
    else:
        out_shape = _sds((M, N), out_dtype)
        out_specs = o_spec
    return _call(body, name, (M // tm, N // tn), in_specs, out_specs, out_shape)(*args)


def norm_mod(x, nw, sh, sc, name):
    L, D = x.shape
    tm = _tile(L, 512)

    def body(x_ref, nw_ref, sh_ref, sc_ref, h_ref):
        xf = x_ref[...]
        r = lax.rsqrt(jnp.mean(xf * xf, axis=-1, keepdims=True) + EPS)
        n = xf * r * nw_ref[...]
        h_ref[...] = (n * (1.0 + sc_ref[...]) + sh_ref[...]).astype(BF)

    row = pl.BlockSpec((tm, D), lambda i: (i, 0))
    vec = pl.BlockSpec((1, D), lambda i: (0, 0))
    return _call(body, name, (L // tm,), [row, vec, vec, vec], row, _sds((L, D), BF))(x, nw, sh, sc)


def norm_mod_bwd(x, dh, dres, nw, sc, name):
    L, D = x.shape
    tm = _tile(L, 512)

    def body(x_ref, dh_ref, dres_ref, nw_ref, sc_ref, dx_ref, acc_ref):
        @pl.when(pl.program_id(0) == 0)
        def _():
            acc_ref[...] = jnp.zeros_like(acc_ref)

        xf = x_ref[...]
        dhf = dh_ref[...].astype(F32)
        r = lax.rsqrt(jnp.mean(xf * xf, axis=-1, keepdims=True) + EPS)
        xhat = xf * r
        nwv = nw_ref[...]
        dn = dhf * (1.0 + sc_ref[...])
        dxhat = dn * nwv
        proj = jnp.mean(dxhat * xhat, axis=-1, keepdims=True)
        dx_ref[...] = dres_ref[...] + r * (dxhat - xhat * proj)
        acc_ref[0:1, :] += jnp.sum(dhf, axis=0, keepdims=True)
        acc_ref[1:2, :] += jnp.sum(dhf * xhat * nwv, axis=0, keepdims=True)
        acc_ref[2:3, :] += jnp.sum(dn * xhat, axis=0, keepdims=True)

    row = pl.BlockSpec((tm, D), lambda i: (i, 0))
    vec = pl.BlockSpec((1, D), lambda i: (0, 0))
    acc = pl.BlockSpec((8, D), lambda i: (0, 0))
    return _call(body, name, (L // tm,), [row, row, row, vec, vec], (row, acc),
                 (_sds((L, D), F32), _sds((8, D), F32)))(x, dh, dres, nw, sc)


def final_loss(x, nw, target):
    L, D = x.shape
    tm = _tile(L, 512)

    def body(x_ref, nw_ref, t_ref, dx_ref, acc_ref):
        @pl.when(pl.program_id(0) == 0)
        def _():
            acc_ref[...] = jnp.zeros_like(acc_ref)

        xf = x_ref[...]
        r = lax.rsqrt(jnp.mean(xf * xf, axis=-1, keepdims=True) + EPS)
        xhat = xf * r
        nwv = nw_ref[...]
        err = xhat * nwv - t_ref[...]
        dy = err * (1.0 / D)
        dxhat = dy * nwv
        proj = jnp.mean(dxhat * xhat, axis=-1, keepdims=True)
        dx_ref[...] = r * (dxhat - xhat * proj)
        acc_ref[0:1, :] += jnp.sum(dy * xhat, axis=0, keepdims=True)
        acc_ref[1:2, :] += jnp.sum(err * err, axis=0, keepdims=True)

    row = pl.BlockSpec((tm, D), lambda i: (i, 0))
    vec = pl.BlockSpec((1, D), lambda i: (0, 0))
    acc = pl.BlockSpec((8, D), lambda i: (0, 0))
    return _call(body, "final_loss", (L // tm,), [row, vec, row], (row, acc),
                 (_sds((L, D), F32), _sds((8, D), F32)))(x, nw, target)


def resid_gate_bwd(dxo, y, gate, coef, name):
    L, D = dxo.shape
    tm = _tile(L, 512)

    def body(dxo_ref, y_ref, g_ref, dy_ref, acc_ref):
        @pl.when(pl.program_id(0) == 0)
        def _():
            acc_ref[...] = jnp.zeros_like(acc_ref)

        d = dxo_ref[...]
        dy = coef * g_ref[...] * d
        dy_ref[...] = dy.astype(BF)
        acc_ref[0:1, :] += coef * jnp.sum(d * y_ref[...].astype(F32), axis=0, keepdims=True)
        acc_ref[1:2, :] += jnp.sum(dy, axis=0, keepdims=True)

    row = pl.BlockSpec((tm, D), lambda i: (i, 0))
    vec = pl.BlockSpec((1, D), lambda i: (0, 0))
    acc = pl.BlockSpec((8, D), lambda i: (0, 0))
    return _call(body, name, (L // tm,), [row, row, vec], (row, acc),
                 (_sds((L, D), BF), _sds((8, D), F32)))(dxo, y, gate)


_NS = N_DEV // 2


def ffn_up(h, w_all, j):
    L, D = h.shape
    S = w_all.shape[2]
    tm = _tile(L, 512)

    def body(h_ref, wg_ref, wu_ref, gu_ref, a_ref):
        hv = h_ref[...]
        g = _dot_nn(hv, wg_ref[...])
        u = _dot_nn(hv, wu_ref[...])
        gu_ref[0] = g.astype(BF)
        gu_ref[1] = u.astype(BF)
        a_ref[...] = (g * _sigmoid(g) * u).astype(BF)

    return _call(body, "ffn_up", (L // tm, _NS),
                 [pl.BlockSpec((tm, D), lambda i, k: (i, 0)),
                  pl.BlockSpec((None, D, S), lambda i, k: (k, j, 0)),
                  pl.BlockSpec((None, D, S), lambda i, k: (k + _NS, j, 0))],
                 (pl.BlockSpec((2, None, tm, S), lambda i, k: (0, k, i, 0)),
                  pl.BlockSpec((None, tm, S), lambda i, k: (k, i, 0))),
                 (_sds((2, _NS, L, S), BF), _sds((_NS, L, S), BF)))(h, w_all, w_all)


def ffn_down(a, wd, x, gate):
    _, L, S = a.shape
    D = wd.shape[2]
    tm = _tile(L, 512)

    def body(a_ref, w_ref, x_ref, g_ref, y_ref, xn_ref):
        y = _dot_nn(a_ref[0], w_ref[0])
        for k in range(1, _NS):
            y = y + _dot_nn(a_ref[k], w_ref[k])
        y_ref[...] = y.astype(BF)
        xn_ref[...] = x_ref[...] + 0.5 * g_ref[...] * y

    row = pl.BlockSpec((tm, D), lambda i: (i, 0))
    return _call(body, "ffn_down", (L // tm,),
                 [pl.BlockSpec((_NS, tm, S), lambda i: (0, i, 0)), pl.BlockSpec((_NS, S, D), lambda i: (0, 0, 0)),
                  row, pl.BlockSpec((1, D), lambda i: (0, 0))],
                 (row, row), (_sds((L, D), BF), _sds((L, D), F32)))(a, wd, x, gate)


def ffn_down_dgrad(dy, wd, gu):
    L, D = dy.shape
    S = wd.shape[1]
    tm = _tile(L, 512)

    def body(dy_ref, w_ref, gu_ref, o_ref):
        da = _dot_nt(dy_ref[...], w_ref[...])
        g = gu_ref[0].astype(F32)
        u = gu_ref[1].astype(F32)
        s = _sigmoid(g)
        o_ref[0] = (da * u * s * (1.0 + g * (1.0 - s))).astype(BF)
        o_ref[1] = (da * g * s).astype(BF)

    slab = pl.BlockSpec((2, None, tm, S), lambda i, k: (0, k, i, 0))
    return _call(body, "ffn_down_dgrad", (L // tm, _NS),
                 [pl.BlockSpec((tm, D), lambda i, k: (i, 0)), pl.BlockSpec((None, S, D), lambda i, k: (k, 0, 0)), slab],
                 slab, _sds((2, _NS, L, S), BF))(dy, wd, gu)


def ffn_down_wgrad(a, dy):
    _, L, S = a.shape
    D = dy.shape[1]

    def body(a_ref, dy_ref, o_ref):
        o_ref[...] = _dot_tn(a_ref[...], dy_ref[...]).astype(BF)

    return _call(body, "ffn_down_wgrad", (_NS,),
                 [pl.BlockSpec((None, L, S), lambda k: (k, 0, 0)), pl.BlockSpec((L, D), lambda k: (0, 0))],
                 pl.BlockSpec((None, S, D), lambda k: (k, 0, 0)), _sds((_NS, S, D), BF))(a, dy)


def ffn_up_wgrad(h, dgu, j, prev):
    L, D = h.shape
    S = dgu.shape[2]
    tm = _tile(D, 512)
    nblk = D // tm

    def body(h_ref, d_ref, *rest):
        rest[-1][...] = _dot_tn(h_ref[...], d_ref[...]).astype(BF)

    in_specs = [pl.BlockSpec((L, tm), lambda i, k: (0, i)), pl.BlockSpec((None, L, S), lambda i, k: (k, 0, 0))]
    args = [h, dgu]
    aliases = {}
    if prev is not None:
        in_specs.append(pl.BlockSpec(memory_space=pl.ANY))
        args.append(prev)
        aliases = {2: 0}
    return pl.pallas_call(
        body, name="ffn_up_wgrad", grid=(nblk, N_DEV), in_specs=in_specs,
        out_specs=pl.BlockSpec((None, tm, S), lambda i, k: (k, i + j * nblk, 0)),
        out_shape=_sds((N_DEV, 2 * D, S), BF), input_output_aliases=aliases,
        compiler_params=pltpu.CompilerParams(vmem_limit_bytes=VMEM_LIMIT))(*args)


def ffn_up_dgrad(dgu, w_all, j):
    _, L, S = dgu.shape
    D = w_all.shape[1] // 2
    tm, tn = _tile(L, 512), _tile(D, 512)
    nj = D // tn

    def body(d_ref, w_ref, o_ref):
        acc = _dot_nt(d_ref[0], w_ref[0])
        for k in range(1, N_DEV):
            acc = acc + _dot_nt(d_ref[k], w_ref[k])
        o_ref[...] = acc

    return _call(body, "ffn_up_dgrad", (L // tm, nj),
                 [pl.BlockSpec((N_DEV, tm, S), lambda i, n: (0, i, 0)),
                  pl.BlockSpec((N_DEV, tn, S), lambda i, n: (0, n + j * nj, 0))],
                 pl.BlockSpec((tm, tn), lambda i, n: (i, n)), _sds((L, D), F32))(dgu, w_all)


def _shift_rows(cur, other, k, down):
    n = cur.shape[0]
    rows = lax.broadcasted_iota(jnp.int32, cur.shape, 0)
    if down:
        return jnp.where(rows < k, pltpu.roll(other, k, 0), pltpu.roll(cur, k, 0))
    return jnp.where(rows >= n - k, pltpu.roll(other, n - k, 0), pltpu.roll(cur, n - k, 0))


def _conv_pre(cur, prev, w_ref, b_ref):
    s = cur * w_ref[CONV_WIDTH - 1:CONV_WIDTH, :] + b_ref[...]
    for k in range(1, CONV_WIDTH):
        s = s + _shift_rows(cur, prev, k, True) * w_ref[CONV_WIDTH - 1 - k:CONV_WIDTH - k, :]
    return s


_XBC_COL0 = D_INNER // 512


def conv_fwd(zx, w, b):
    L = zx.shape[0]
    tm, tc = _tile(L, 256), 512

    def body(cur_ref, prev_ref, w_ref, b_ref, o_ref):
        cur = cur_ref[...]
        prev = jnp.where(pl.program_id(1) > 0, prev_ref[...], 0.0)
        s = _conv_pre(cur, prev, w_ref, b_ref)
        o_ref[...] = s * _sigmoid(s)

    return _call(body, "conv_fwd", (CONV_DIM // tc, L // tm),
                 [pl.BlockSpec((tm, tc), lambda j, i: (i, _XBC_COL0 + j)),
                  pl.BlockSpec((tm, tc), lambda j, i: (jnp.maximum(i - 1, 0), _XBC_COL0 + j)),
                  pl.BlockSpec((CONV_WIDTH, tc), lambda j, i: (0, j)),
                  pl.BlockSpec((1, tc), lambda j, i: (0, j))],
                 pl.BlockSpec((tm, tc), lambda j, i: (i, j)), _sds((L, CONV_DIM), F32))(zx, zx, w, b)


def conv_bwd_act(dxc, zx, w, b):
    L = zx.shape[0]
    tm, tc = _tile(L, 256), 512

    def body(d_ref, cur_ref, prev_ref, w_ref, b_ref, o_ref):
        cur = cur_ref[...]
        prev = jnp.where(pl.program_id(1) > 0, prev_ref[...], 0.0)
        s = _conv_pre(cur, prev, w_ref, b_ref)
        sg = _sigmoid(s)
        o_ref[...] = d_ref[...] * sg * (1.0 + s * (1.0 - sg))

    return _call(body, "conv_bwd_act", (CONV_DIM // tc, L // tm),
                 [pl.BlockSpec((tm, tc), lambda j, i: (i, j)),
                  pl.BlockSpec((tm, tc), lambda j, i: (i, _XBC_COL0 + j)),
                  pl.BlockSpec((tm, tc), lambda j, i: (jnp.maximum(i - 1, 0), _XBC_COL0 + j)),
                  pl.BlockSpec((CONV_WIDTH, tc), lambda j, i: (0, j)),
                  pl.BlockSpec((1, tc), lambda j, i: (0, j))],
                 pl.BlockSpec((tm, tc), lambda j, i: (i, j)), _sds((L, CONV_DIM), F32))(dxc, zx, zx, w, b)


def conv_bwd(ds, zx, w):
    L = zx.shape[0]
    tm, tc = _tile(L, 256), 512
    nblk = L // tm

    def body(ds_ref, dsn_ref, cur_ref, prev_ref, w_ref, du_ref, acc_ref):
        i = pl.program_id(1)

        @pl.when(i == 0)
        def _():
            acc_ref[...] = jnp.zeros_like(acc_ref)

        ds_c = ds_ref[...]
        ds_n = jnp.where(i < nblk - 1, dsn_ref[...], 0.0)
        cur = cur_ref[...]
        prev = jnp.where(i > 0, prev_ref[...], 0.0)
        du = ds_c * w_ref[CONV_WIDTH - 1:CONV_WIDTH, :]
        acc_ref[CONV_WIDTH - 1:CONV_WIDTH, :] += jnp.sum(ds_c * cur, axis=0, keepdims=True)
        for k in range(1, CONV_WIDTH):
            du = du + _shift_rows(ds_c, ds_n, k, False) * w_ref[CONV_WIDTH - 1 - k:CONV_WIDTH - k, :]
            acc_ref[CONV_WIDTH - 1 - k:CONV_WIDTH - k, :] += jnp.sum(
                ds_c * _shift_rows(cur, prev, k, True), axis=0, keepdims=True)
        acc_ref[CONV_WIDTH:CONV_WIDTH + 1, :] += jnp.sum(ds_c, axis=0, keepdims=True)
        du_ref[...] = du.astype(BF)

    return _call(body, "conv_bwd", (CONV_DIM // tc, nblk),
                 [pl.BlockSpec((tm, tc), lambda j, i: (i, j)),
                  pl.BlockSpec((tm, tc), lambda j, i: (jnp.minimum(i + 1, nblk - 1), j)),
                  pl.BlockSpec((tm, tc), lambda j, i: (i, _XBC_COL0 + j)),
                  pl.BlockSpec((tm, tc), lambda j, i: (jnp.maximum(i - 1, 0), _XBC_COL0 + j)),
                  pl.BlockSpec((CONV_WIDTH, tc), lambda j, i: (0, j))],
                 (pl.BlockSpec((tm, tc), lambda j, i: (i, j)), pl.BlockSpec((8, tc), lambda j, i: (0, j))),
                 (_sds((L, CONV_DIM), BF), _sds((8, CONV_DIM), F32)))(ds, ds, zx, zx, w)


_DT_COL = (D_INNER + CONV_DIM) // 128


def dt_prep(zx, bias_pad, alog_pad):
    L = zx.shape[0]

    def body(raw_ref, b_ref, al_ref, dt_ref, acs_ref):
        v = raw_ref[...] + b_ref[...]
        dt = jnp.maximum(v, 0.0) + jnp.log(1.0 + jnp.exp(-jnp.abs(v)))
        dt_ref[...] = dt
        acs = dt * (-jnp.exp(al_ref[...]))
        rows = lax.broadcasted_iota(jnp.int32, acs.shape, 0)
        s = 1
        while s < CHUNK:
            acs = acs + jnp.where(rows >= s, pltpu.roll(acs, s, 0), 0.0)
            s *= 2
        acs_ref[...] = acs

    blk = pl.BlockSpec((CHUNK, 128), lambda i: (i, 0))
    vec = pl.BlockSpec((1, 128), lambda i: (0, 0))
    return _call(body, "dt_prep", (L // CHUNK,),
                 [pl.BlockSpec((CHUNK, 128), lambda i: (i, _DT_COL)), vec, vec], (blk, blk),
                 (_sds((L, 128), F32), _sds((L, 128), F32)))(zx, bias_pad, alog_pad)


def dt_bwd(ddt, da, dt, zx, bias_pad, alog_pad):
    L = zx.shape[0]
    tm = _tile(L, 512)

    def body(ddt_ref, da_ref, dt_ref, raw_ref, b_ref, al_ref, o_ref, acc_ref):
        @pl.when(pl.program_id(0) == 0)
        def _():
            acc_ref[...] = jnp.zeros_like(acc_ref)

        A = -jnp.exp(al_ref[...])
        dav = da_ref[...]
        dd = ddt_ref[...] + dav * A
        draw = dd * _sigmoid(raw_ref[...] + b_ref[...])
        o_ref[...] = draw.astype(BF)
        acc_ref[0:1, :] += jnp.sum(draw, axis=0, keepdims=True)
        acc_ref[1:2, :] += jnp.sum(dav * dt_ref[...], axis=0, keepdims=True) * A

    blk = pl.BlockSpec((tm, 128), lambda i: (i, 0))
    vec = pl.BlockSpec((1, 128), lambda i: (0, 0))
    return _call(body, "dt_bwd", (L // tm,),
                 [blk, blk, blk, pl.BlockSpec((tm, 128), lambda i: (i, _DT_COL)), vec, vec],
                 (blk, pl.BlockSpec((8, 128), lambda i: (0, 0))),
                 (_sds((L, 128), BF), _sds((8, 128), F32)))(ddt, da, dt, zx, bias_pad, alog_pad)


_HPG = SSM_HEADS // SSM_GROUPS
_GW = _HPG * SSM_HEADDIM
_B_COL0 = D_INNER // SSM_STATE
_C_COL0 = (D_INNER + SSM_GROUPS * SSM_STATE) // SSM_STATE


def _ssd_head(x, dtc, ac, ar, r, causal):
    xh = x[:, SSM_HEADDIM * r:SSM_HEADDIM * (r + 1)]
    acol = ac[:, r:r + 1]
    arow = ar[r:r + 1, :]
    alast = ar[r:r + 1, CHUNK - 1:CHUNK]
    lm = jnp.exp(jnp.where(causal, acol - arow, NEG))
    return xh, xh * dtc[:, r:r + 1], acol, alast, lm


def ssd_fwd(xc, dt_g, acs_g, acsT_g, d_exp):
    L = xc.shape[0]
    nc = L // CHUNK

    def body(x_ref, b_ref, c_ref, dt_ref, ac_ref, ar_ref, d_ref, y_ref, pst_ref, st_ref):
        @pl.when(pl.program_id(1) == 0)
        def _():
            st_ref[...] = jnp.zeros_like(st_ref)

        x, Bm, Cm = x_ref[...], b_ref[...], c_ref[...]
        dtc, ac, ar = dt_ref[...], ac_ref[...], ar_ref[...]
        causal = lax.broadcasted_iota(jnp.int32, (CHUNK, CHUNK), 0) >= lax.broadcasted_iota(jnp.int32, (CHUNK, CHUNK), 1)
        CB = _dot_nt(Cm, Bm)
        for r in range(_HPG):
            xh, xd, acol, alast, lm = _ssd_head(x, dtc, ac, ar, r, causal)
            P = st_ref[r]
            y = _dot_nn(CB * lm, xd) + jnp.exp(acol) * _dot_nt(Cm, P)
            y_ref[:, SSM_HEADDIM * r:SSM_HEADDIM * (r + 1)] = y + d_ref[:, SSM_HEADDIM * r:SSM_HEADDIM * (r + 1)] * xh
            pst_ref[r] = P
            st_ref[r] = P * jnp.exp(alast) + _dot_tn(xd * jnp.exp(alast - acol), Bm)

    return _call(
        body, "ssd_fwd", (SSM_GROUPS, nc),
        [pl.BlockSpec((CHUNK, _GW), lambda g, c: (c, g)),
         pl.BlockSpec((CHUNK, SSM_STATE), lambda g, c: (c, _B_COL0 + g)),
         pl.BlockSpec((CHUNK, SSM_STATE), lambda g, c: (c, _C_COL0 + g)),
         pl.BlockSpec((None, CHUNK, _HPG), lambda g, c: (g, c, 0)),
         pl.BlockSpec((None, CHUNK, _HPG), lambda g, c: (g, c, 0)),
         pl.BlockSpec((None, _HPG, CHUNK), lambda g, c: (g, 0, c)),
         pl.BlockSpec((None, 1, _GW), lambda g, c: (g, 0, 0))],
        (pl.BlockSpec((CHUNK, _GW), lambda g, c: (c, g)),
         pl.BlockSpec((None, None, _HPG, SSM_HEADDIM, SSM_STATE), lambda g, c: (c, g, 0, 0, 0))),
        (_sds((L, D_INNER), F32), _sds((nc, SSM_GROUPS, _HPG, SSM_HEADDIM, SSM_STATE), F32)),
        scratch=[pltpu.VMEM((_HPG, SSM_HEADDIM, SSM_STATE), F32)],
    )(xc, xc, xc, dt_g, acs_g, acsT_g, d_exp)


def ssd_bwd(dy, xc, dt_g, acs_g, acsT_g, pst, d_exp):
    L = xc.shape[0]
    nc = L // CHUNK

    def body(dy_ref, x_ref, b_ref, c_ref, dt_ref, ac_ref, ar_ref, pst_ref, d_ref,
             dx_ref, db_ref, dc_ref, ddt_ref, da_ref, dd_ref, dp_ref):
        @pl.when(pl.program_id(1) == 0)
        def _():
            dp_ref[...] = jnp.zeros_like(dp_ref)
            dd_ref[...] = jnp.zeros_like(dd_ref)

        dyv, x, Bm, Cm = dy_ref[...], x_ref[...], b_ref[...], c_ref[...]
        dtc, ac, ar = dt_ref[...], ac_ref[...], ar_ref[...]
        ri = lax.broadcasted_iota(jnp.int32, (CHUNK, CHUNK), 0)
        ci = lax.broadcasted_iota(jnp.int32, (CHUNK, CHUNK), 1)
        causal = ri >= ci
        lane4 = lax.broadcasted_iota(jnp.int32, (CHUNK, _HPG), 1)
        CB = _dot_nt(Cm, Bm)
        dB = jnp.zeros((CHUNK, SSM_STATE), F32)
        dC = jnp.zeros((CHUNK, SSM_STATE), F32)
        dCB = jnp.zeros((CHUNK, CHUNK), F32)
        ddt_blk = jnp.zeros((CHUNK, _HPG), F32)
        da_blk = jnp.zeros((CHUNK, _HPG), F32)
        for r in range(_HPG):
            sl = slice(SSM_HEADDIM * r, SSM_HEADDIM * (r + 1))
            xh, xd, acol, alast, lm = _ssd_head(x, dtc, ac, ar, r, causal)
            dyh = dyv[:, sl]
            P = pst_ref[r]
            dPn = dp_ref[r]
            eA = jnp.exp(acol)
            cd = jnp.exp(alast)
            dte = jnp.exp(alast - acol)
            G = CB * lm
            Z = _dot_nt(Cm, P)
            dZ = eA * dyh
            dC = dC + _dot_nn(dZ, P)
            dp_ref[r] = dPn * cd + _dot_tn(dZ, Cm)
            dA_col = jnp.sum(dZ * Z, axis=1, keepdims=True)
            BdS = _dot_nt(Bm, dPn)
            dxd = dte * BdS
            dB = dB + dte * _dot_nn(xd, dPn)
            t = jnp.sum(xd * BdS, axis=1, keepdims=True) * dte
            dA_col = dA_col - t
            dA_last = jnp.sum(t, axis=0, keepdims=True) + jnp.sum(
                jnp.sum(dPn * P, axis=1, keepdims=True), axis=0, keepdims=True) * cd
            dG = _dot_nt(dyh, xd)
            dxd = dxd + _dot_tn(G, dyh)
            dCB = dCB + dG * lm
            W = dG * G
            dA_col = dA_col + jnp.sum(W, axis=1, keepdims=True)
            dA_row = jnp.sum(jnp.where(ri == ci, dA_col, 0.0), axis=0, keepdims=True) - jnp.sum(W, axis=0, keepdims=True)
            da_col = jnp.sum(jnp.where(ci >= ri, dA_row, 0.0), axis=1, keepdims=True) + dA_last
            da_blk = jnp.where(lane4 == r, da_col, da_blk)
            ddt_blk = jnp.where(lane4 == r, jnp.sum(dxd * xh, axis=1, keepdims=True), ddt_blk)
            dx_ref[:, sl] = dxd * dtc[:, r:r + 1] + d_ref[:, sl] * dyh
        dc_ref[...] = dC + _dot_nn(dCB, Bm)
        db_ref[...] = dB + _dot_tn(dCB, Cm)
        ddt_ref[...] = ddt_blk
        da_ref[...] = da_blk
        dd_ref[...] += jnp.sum(dyv * x, axis=0, keepdims=True)

    rc = lambda g, c: (nc - 1 - c, g)
    small = pl.BlockSpec((None, CHUNK, _HPG), lambda g, c: (g, nc - 1 - c, 0))
    return _call(
        body, "ssd_bwd", (SSM_GROUPS, nc),
        [pl.BlockSpec((CHUNK, _GW), rc),
         pl.BlockSpec((CHUNK, _GW), rc),
         pl.BlockSpec((CHUNK, SSM_STATE), lambda g, c: (nc - 1 - c, _B_COL0 + g)),
         pl.BlockSpec((CHUNK, SSM_STATE), lambda g, c: (nc - 1 - c, _C_COL0 + g)),
         small, small,
         pl.BlockSpec((None, _HPG, CHUNK), lambda g, c: (g, 0, nc - 1 - c)),
         pl.BlockSpec((None, None, _HPG, SSM_HEADDIM, SSM_STATE), lambda g, c: (nc - 1 - c, g, 0, 0, 0)),
         pl.BlockSpec((None, 1, _GW), lambda g, c: (g, 0, 0))],
        (pl.BlockSpec((CHUNK, _GW), rc),
         pl.BlockSpec((CHUNK, SSM_STATE), rc),
         pl.BlockSpec((CHUNK, SSM_STATE), rc),
         small, small,
         pl.BlockSpec((None, 1, _GW), lambda g, c: (g, 0, 0))),
        (_sds((L, D_INNER), F32), _sds((L, SSM_GROUPS * SSM_STATE), F32), _sds((L, SSM_GROUPS * SSM_STATE), F32),
         _sds((SSM_GROUPS, L, _HPG), F32), _sds((SSM_GROUPS, L, _HPG), F32), _sds((SSM_GROUPS, 1, _GW), F32)),
        scratch=[pltpu.VMEM((_HPG, SSM_HEADDIM, SSM_STATE), F32)],
    )(dy, xc, xc, xc, dt_g, acs_g, acsT_g, pst, d_exp)


_NGW = D_INNER // SSM_GROUPS


def gate_norm(y, zx, nw):
    L = y.shape[0]
    tm = _tile(L, 256)

    def body(y_ref, z_ref, nw_ref, o_ref):
        for g in range(SSM_GROUPS):
            sl = slice(_NGW * g, _NGW * (g + 1))
            z = z_ref[:, sl]
            y2 = y_ref[:, sl] * (z * _sigmoid(z))
            r = lax.rsqrt(jnp.mean(y2 * y2, axis=-1, keepdims=True) + EPS)
            o_ref[:, sl] = (y2 * r * nw_ref[:, sl]).astype(BF)

    row = pl.BlockSpec((tm, D_INNER), lambda i: (i, 0))
    return _call(body, "gate_norm", (L // tm,), [row, row, pl.BlockSpec((1, D_INNER), lambda i: (0, 0))],
                 row, _sds((L, D_INNER), BF))(y, zx, nw)


def gate_norm_bwd(dyn, y, zx, nw):
    L = y.shape[0]
    tm = _tile(L, 256)

    def body(d_ref, y_ref, z_ref, nw_ref, dy_ref, dz_ref, acc_ref):
        @pl.when(pl.program_id(0) == 0)
        def _():
            acc_ref[...] = jnp.zeros_like(acc_ref)

        for g in range(SSM_GROUPS):
            sl = slice(_NGW * g, _NGW * (g + 1))
            z = z_ref[:, sl]
            yv = y_ref[:, sl]
            sg = _sigmoid(z)
            sz = z * sg
            y2 = yv * sz
            r = lax.rsqrt(jnp.mean(y2 * y2, axis=-1, keepdims=True) + EPS)
            yh = y2 * r
            d = d_ref[:, sl]
            dn = d * nw_ref[:, sl]
            dy2 = r * (dn - yh * jnp.mean(dn * yh, axis=-1, keepdims=True))
            dy_ref[:, sl] = dy2 * sz
            dz_ref[:, sl] = (dy2 * yv * sg * (1.0 + z * (1.0 - sg))).astype(BF)
            acc_ref[0:1, sl] += jnp.sum(d * yh, axis=0, keepdims=True)

    row = pl.BlockSpec((tm, D_INNER), lambda i: (i, 0))
    return _call(body, "gate_norm_bwd", (L // tm,), [row, row, row, pl.BlockSpec((1, D_INNER), lambda i: (0, 0))],
                 (row, row, pl.BlockSpec((8, D_INNER), lambda i: (0, 0))),
                 (_sds((L, D_INNER), F32), _sds((L, D_INNER), BF), _sds((8, D_INNER), F32)))(dyn, y, zx, nw)


_SCALE = HEAD_DIM ** -0.5
_REP = ATT_HEADS // KV_HEADS
_V_OFF = KV_HEADS * HEAD_DIM


def _attn_probs(qh, kp, kc, sink, first):
    rows = lax.broadcasted_iota(jnp.int32, (WINDOW, WINDOW), 0)
    cols = lax.broadcasted_iota(jnp.int32, (WINDOW, WINDOW), 1)
    sp = jnp.where(jnp.logical_and(cols > rows, jnp.logical_not(first)), _dot_nt(qh, kp) * _SCALE, NEG)
    sc = jnp.where(cols <= rows, _dot_nt(qh, kc) * _SCALE, NEG)
    m = jnp.maximum(jnp.maximum(jnp.max(sp, axis=1, keepdims=True), jnp.max(sc, axis=1, keepdims=True)), sink)
    pp = jnp.exp(sp - m)
    pc = jnp.exp(sc - m)
    ps = jnp.exp(sink - m)
    inv = 1.0 / (jnp.sum(pp, axis=1, keepdims=True) + jnp.sum(pc, axis=1, keepdims=True) + ps)
    return pp * inv, pc * inv, ps * inv


def attn_fwd(q, kv, sinks_pad):
    L = q.shape[0]
    nb = L // WINDOW

    def body(q_ref, kc_ref, kp_ref, s_ref, o_ref):
        first = pl.program_id(0) == 0
        for h in range(ATT_HEADS):
            k = h // _REP
            ks = slice(HEAD_DIM * k, HEAD_DIM * (k + 1))
            vs = slice(_V_OFF + HEAD_DIM * k, _V_OFF + HEAD_DIM * (k + 1))
            hs = slice(HEAD_DIM * h, HEAD_DIM * (h + 1))
            pp, pc, _ = _attn_probs(q_ref[:, hs], kp_ref[:, ks], kc_ref[:, ks], s_ref[:, h:h + 1], first)
            o_ref[:, hs] = (_dot_nn(pp, kp_ref[:, vs]) + _dot_nn(pc, kc_ref[:, vs])).astype(BF)

    qspec = pl.BlockSpec((WINDOW, D_MODEL), lambda i: (i, 0))
    return _call(body, "attn_fwd", (nb,),
                 [qspec, pl.BlockSpec((WINDOW, KV_DIM), lambda i: (i, 0)),
                  pl.BlockSpec((WINDOW, KV_DIM), lambda i: (jnp.maximum(i - 1, 0), 0)),
                  pl.BlockSpec((1, 128), lambda i: (0, 0))],
                 qspec, _sds((L, D_MODEL), BF))(q, kv, kv, sinks_pad)


def attn_bwd(q, kv, do, sinks_pad):
    L = q.shape[0]
    nb = L // WINDOW

    def body(q_ref, kc_ref, kp_ref, do_ref, s_ref, dq_ref, dc_ref, dp_ref, acc_ref):
        first = pl.program_id(0) == 0

        @pl.when(first)
        def _():
            acc_ref[...] = jnp.zeros_like(acc_ref)

        lane = lax.broadcasted_iota(jnp.int32, (1, 128), 1)
        dsink = jnp.zeros((1, 128), F32)
        for k in range(KV_HEADS):
            ks = slice(HEAD_DIM * k, HEAD_DIM * (k + 1))
            vs = slice(_V_OFF + HEAD_DIM * k, _V_OFF + HEAD_DIM * (k + 1))
            kp, kc, vp, vc = kp_ref[:, ks], kc_ref[:, ks], kp_ref[:, vs], kc_ref[:, vs]
            dkp = jnp.zeros((WINDOW, HEAD_DIM), F32)
            dkc = jnp.zeros((WINDOW, HEAD_DIM), F32)
            dvp = jnp.zeros((WINDOW, HEAD_DIM), F32)
            dvc = jnp.zeros((WINDOW, HEAD_DIM), F32)
            for r in range(_REP):
                h = k * _REP + r
                hs = slice(HEAD_DIM * h, HEAD_DIM * (h + 1))
                qh = q_ref[:, hs]
                doh = do_ref[:, hs]
                pp, pc, ps = _attn_probs(qh, kp, kc, s_ref[:, h:h + 1], first)
                dpp = _dot_nt(doh, vp)
                dpc = _dot_nt(doh, vc)
                delta = jnp.sum(pp * dpp, axis=1, keepdims=True) + jnp.sum(pc * dpc, axis=1, keepdims=True)
                dsp = pp * (dpp - delta) * _SCALE
                dsc = pc * (dpc - delta) * _SCALE
                dq_ref[:, hs] = _dot_nn(dsp, kp) + _dot_nn(dsc, kc)
                dkp = dkp + _dot_tn(dsp, qh)
                dkc = dkc + _dot_tn(dsc, qh)
                dvp = dvp + _dot_tn(pp, doh)
                dvc = dvc + _dot_tn(pc, doh)
                dsink = dsink + jnp.where(lane == h, -jnp.sum(ps * delta, axis=0, keepdims=True), 0.0)
            dp_ref[:, ks] = dkp
            dc_ref[:, ks] = dkc
            dp_ref[:, vs] = dvp
            dc_ref[:, vs] = dvc
        acc_ref[0:1, :] += jnp.sum(dq_ref[...], axis=0, keepdims=True)
        acc_ref[1:2, 0:128] += dsink

    qspec = pl.BlockSpec((WINDOW, D_MODEL), lambda i: (i, 0))
    kspec = pl.BlockSpec((WINDOW, KV_DIM), lambda i: (i, 0))
    return _call(body, "attn_bwd", (nb,),
                 [qspec, kspec, pl.BlockSpec((WINDOW, KV_DIM), lambda i: (jnp.maximum(i - 1, 0), 0)), qspec,
                  pl.BlockSpec((1, 128), lambda i: (0, 0))],
                 (qspec, kspec, kspec, pl.BlockSpec((8, D_MODEL), lambda i: (0, 0))),
                 (_sds((L, D_MODEL), F32), _sds((L, KV_DIM), F32), _sds((L, KV_DIM), F32), _sds((8, D_MODEL), F32)),
                 )(q, kv, kv, do, sinks_pad)


def kv_grad_combine(parts):
    L = parts[0][0].shape[0]
    nb = L // WINDOW
    n = len(parts)

    def body(*refs):
        i = pl.program_id(0)
        o_ref, acc_ref = refs[2 * n], refs[2 * n + 1]

        @pl.when(i == 0)
        def _():
            acc_ref[...] = jnp.zeros_like(acc_ref)

        tot = refs[0][...]
        nxt = refs[1][...]
        for a in range(1, n):
            tot = tot + refs[2 * a][...]
            nxt = nxt + refs[2 * a + 1][...]
        tot = tot + jnp.where(i < nb - 1, nxt, 0.0)
        o_ref[...] = tot
        acc_ref[0:1, :] += jnp.sum(tot, axis=0, keepdims=True)

    cur = pl.BlockSpec((WINDOW, KV_DIM), lambda i: (i, 0))
    nxt = pl.BlockSpec((WINDOW, KV_DIM), lambda i: (jnp.minimum(i + 1, nb - 1), 0))
    args = [t for p in parts for t in p]
    return _call(body, "kv_grad_combine", (nb,), [cur, nxt] * n,
                 (cur, pl.BlockSpec((8, KV_DIM), lambda i: (0, 0))),
                 (_sds((L, KV_DIM), F32), _sds((8, KV_DIM), F32)))(*args)


def mod_fwd(c_all, w, b, name):
    n, _, C = w.shape

    def body(c_ref, w_ref, b_ref, o_ref, ca_ref):
        cv = c_ref[...]
        ca = cv * _sigmoid(cv)
        ca_ref[...] = ca
        o_ref[...] = _dot(ca, w_ref[...], ((1,), (0,))) + b_ref[...]

    return _call(body, name, (n,),
                 [pl.BlockSpec((N_DEV, D_MODEL), lambda i: (0, 0)),
                  pl.BlockSpec((None, D_MODEL, C), lambda i: (i, 0, 0)),
                  pl.BlockSpec((None, 1, C), lambda i: (i, 0, 0))],
                 (pl.BlockSpec((None, N_DEV, C), lambda i: (i, 0, 0)), pl.BlockSpec((N_DEV, D_MODEL), lambda i: (0, 0))),
                 (_sds((n, N_DEV, C), F32), _sds((N_DEV, D_MODEL), F32)))(c_all, w, b)


def mod_wgrad(c_act_t, dmod, name):
    n, _, C = dmod.shape
    tr = 256

    def body(ct_ref, d_ref, o_ref):
        acc = ct_ref[:, 0:1] * d_ref[0:1, :]
        for bidx in range(1, N_DEV):
            acc = acc + ct_ref[:, bidx:bidx + 1] * d_ref[bidx:bidx + 1, :]
        o_ref[...] = acc

    return _call(body, name, (n, D_MODEL // tr),
                 [pl.BlockSpec((tr, N_DEV), lambda i, j: (j, 0)),
                  pl.BlockSpec((None, N_DEV, C), lambda i, j: (i, 0, 0))],
                 pl.BlockSpec((None, tr, C), lambda i, j: (i, j, 0)), _sds((n, D_MODEL, C), F32))(c_act_t, dmod)


def _my_pos():
    return lax.axis_index("x"), lax.axis_index("y"), lax.axis_index("c")


def small_all_gather(v):
    m_per, n = v.shape

    def body(x_ref, out_ref, send_sems, recv_sems, local_sem):
        x, y, c = _my_pos()
        me, sibling = (x, y, c), (x, y, 1 - c)
        chips = [(1 - x, y), (x, 1 - y), (1 - x, 1 - y)]

        def rows(px, py, pc):
            return out_ref.at[pl.ds((4 * px + 2 * py + pc) * m_per, m_per), :]

        def copy(k, block, to, src=None):
            return pltpu.make_async_remote_copy(
                src_ref=rows(*block) if src is None else src, dst_ref=rows(*block),
                send_sem=send_sems.at[k], recv_sem=recv_sems.at[k], device_id=to, device_id_type=MESH)

        mine = pltpu.make_async_copy(x_ref, rows(*me), local_sem)
        mine.start()
        first = [copy(0, me, sibling, src=x_ref)]
        first += [copy(1 + j, me, (*chip, c), src=x_ref) for j, chip in enumerate(chips)]
        for cp in first:
            cp.start()
        passed = [copy(4 + j, (*chip, c), sibling) for j, chip in enumerate(chips)]
        for j, chip in enumerate(chips):
            copy(1 + j, (*chip, c), me).wait_recv()
            passed[j].start()
        copy(0, sibling, me).wait_recv()
        for j, chip in enumerate(chips):
            copy(4 + j, (*chip, 1 - c), me).wait_recv()
        for cp in first + passed:
            cp.wait_send()
        mine.wait()

    return pl.pallas_call(
        body, name="small_all_gather",
        out_shape=_sds((N_DEV * m_per, n), v.dtype),
        in_specs=[pl.BlockSpec(memory_space=pltpu.VMEM)],
        out_specs=pl.BlockSpec(memory_space=pltpu.VMEM),
        scratch_shapes=[pltpu.SemaphoreType.DMA((7,)), pltpu.SemaphoreType.DMA((7,)), pltpu.SemaphoreType.DMA],
        compiler_params=pltpu.CompilerParams(vmem_limit_bytes=VMEM_LIMIT),
    )(v)


def big_all_gather(arrs):
    n = len(arrs)

    def body(*refs):
        ins, outs = refs[:n], refs[n:2 * n]
        send_sems, recv_sems, local_sems = refs[2 * n], refs[2 * n + 1], refs[2 * n + 2]
        x, y, c = _my_pos()
        me, sibling = (x, y, c), (x, y, 1 - c)
        chips = [(1 - x, y), (x, 1 - y), (1 - x, 1 - y)]

        def slot(a, px, py, pc):
            return outs[a].at[4 * px + 2 * py + pc]

        def copy(a, k, block, to, src=None):
            return pltpu.make_async_remote_copy(
                src_ref=slot(a, *block) if src is None else src, dst_ref=slot(a, *block),
                send_sem=send_sems.at[7 * a + k], recv_sem=recv_sems.at[7 * a + k], device_id=to, device_id_type=MESH)

        mine = [pltpu.make_async_copy(ins[a], slot(a, *me), local_sems.at[a]) for a in range(n)]
        for cp in mine:
            cp.start()
        first = []
        for a in range(n):
            first.append(copy(a, 0, me, sibling, src=ins[a]))
            first += [copy(a, 1 + j, me, (*chip, c), src=ins[a]) for j, chip in enumerate(chips)]
        for cp in first:
            cp.start()
        passed = []
        for a in range(n):
            for j, chip in enumerate(chips):
                copy(a, 1 + j, (*chip, c), me).wait_recv()
                fwd = copy(a, 4 + j, (*chip, c), sibling)
                fwd.start()
                passed.append(fwd)
        for a in range(n):
            copy(a, 0, sibling, me).wait_recv()
            for j, chip in enumerate(chips):
                copy(a, 4 + j, (*chip, 1 - c), me).wait_recv()
        for cp in first + passed:
            cp.wait_send()
        for cp in mine:
            cp.wait()

    hbm = pl.BlockSpec(memory_space=pltpu.HBM)
    return pl.pallas_call(
        body, name="big_all_gather",
        out_shape=[_sds((N_DEV,) + a.shape, a.dtype) for a in arrs],
        in_specs=[hbm] * n, out_specs=[hbm] * n,
        scratch_shapes=[pltpu.SemaphoreType.DMA((7 * n,)), pltpu.SemaphoreType.DMA((7 * n,)),
                        pltpu.SemaphoreType.DMA((n,))],
    )(*arrs)


_FLIPS =[(fx, fy, fc) for fx in (0, 1) for fy in (0, 1) for fc in (0, 1)][1:]
_HBM = pl.BlockSpec(memory_space=pltpu.HBM)
_SEM = pl.BlockSpec(memory_space=pltpu.SEMAPHORE)
_EFFECT = pltpu.SideEffectType.DATAFLOW_SIDE_EFFECTING


def _flip(x, y, c, f):
    return (1 - x if f[0] else x), (1 - y if f[1] else y), (1 - c if f[2] else c)


def _xfer_copies(srcs, lands, send_sems, recv_sems, scatter):
    x, y, c = _my_pos()
    me = 4 * x + 2 * y + c
    copies = []
    for a in range(len(srcs)):
        for k, f in enumerate(_FLIPS):
            px, py, pc = _flip(x, y, c, f)
            src = srcs[a].at[4 * px + 2 * py + pc] if scatter else srcs[a]
            copies.append(pltpu.make_async_remote_copy(
                src_ref=src, dst_ref=lands[a].at[me], send_sem=send_sems.at[7 * a + k],
                recv_sem=recv_sems.at[7 * a + k], device_id=(px, py, pc), device_id_type=MESH))
    return copies


def _own_copies(srcs, lands, local_sems, scatter):
    x, y, c = _my_pos()
    me = 4 * x + 2 * y + c
    return [pltpu.make_async_copy(srcs[a].at[me] if scatter else srcs[a], lands[a].at[me], local_sems.at[a])
            for a in range(len(srcs))]


def xfer_start(arrs, scatter, after, name):
    n = len(arrs)
    land_shapes = [a.shape if scatter else (N_DEV,) + a.shape for a in arrs]

    def body(*refs):
        srcs, lands = refs[:n], refs[n:2 * n]
        send_sems, recv_sems, local_sems = refs[2 * n + 1], refs[2 * n + 2], refs[2 * n + 3]
        token = refs[-1]
        for cp in _xfer_copies(srcs, lands, send_sems, recv_sems, scatter):
            cp.start()
        for cp in _own_copies(srcs, lands, local_sems, scatter):
            cp.start()
        token[...] = jnp.zeros_like(token)

    out = pl.pallas_call(
        body, name=name,
        out_shape=(pltpu.SemaphoreType.DMA((7 * n,)), pltpu.SemaphoreType.DMA((7 * n,)),
                   pltpu.SemaphoreType.DMA((n,)),
                   *[pltpu.HBM(a.shape, a.dtype) for a in arrs],
                   *[pltpu.HBM(s, a.dtype) for s, a in zip(land_shapes, arrs)],
                   _sds((8, 128), F32)),
        in_specs=[_HBM] * (2 * n) + [pl.BlockSpec(memory_space=pl.ANY)],
        out_specs=(_SEM, _SEM, _SEM, *([_HBM] * (2 * n)), pl.BlockSpec(memory_space=pltpu.VMEM)),
        input_output_aliases={i: 3 + i for i in range(2 * n)},
        compiler_params=pltpu.CompilerParams(has_side_effects=_EFFECT),
    )(*[pltpu.with_memory_space_constraint(a, pltpu.HBM) for a in arrs],
      *[pltpu.with_memory_space_constraint(lax.empty(s, a.dtype), pltpu.HBM) for s, a in zip(land_shapes, arrs)],
      after)
    return (out[0], out[1], out[2], list(out[3:3 + n]), list(out[3 + n:3 + 2 * n]), scatter), out[-1]


def xfer_wait(handle, after, name):
    send_sems, recv_sems, local_sems, srcs, lands, scatter = handle
    n = len(srcs)

    def body(*refs):
        srcs_r, lands_r = refs[:n], refs[n:2 * n]
        ssem, rsem, lsem = refs[2 * n], refs[2 * n + 1], refs[2 * n + 2]
        for cp in _xfer_copies(srcs_r, lands_r, ssem, rsem, scatter):
            cp.wait_send()
            cp.wait_recv()
        for cp in _own_copies(srcs_r, lands_r, lsem, scatter):
            cp.wait()

    out = pl.pallas_call(
        body, name=name,
        out_shape=(*[pltpu.HBM(a.shape, a.dtype) for a in srcs], *[pltpu.HBM(a.shape, a.dtype) for a in lands]),
        in_specs=[_HBM] * (2 * n) + [_SEM, _SEM, _SEM, pl.BlockSpec(memory_space=pl.ANY)],
        out_specs=tuple([_HBM] * (2 * n)),
        input_output_aliases={i: i for i in range(2 * n)},
        compiler_params=pltpu.CompilerParams(has_side_effects=_EFFECT),
    )(*srcs, *lands, send_sems, recv_sems, local_sems, after)
    return list(out[n:])


def adamw(parts, w, m, v, name):
    R, C = w.shape
    n_parts = parts.shape[0]
    row_bytes = 2 * (n_parts * C * parts.dtype.itemsize + 7 * C * 4)
    tr = R
    for cand in (512, 256, 128, 64, 32, 16):
        if R % cand == 0 and R > cand and cand * row_bytes <= ADAMW_VMEM_BUDGET:
            tr = cand
            break
    c1 = 1.0 / (1.0 - ADAM_B1 ** ADAM_STEP)
    c2 = 1.0 / (1.0 - ADAM_B2 ** ADAM_STEP)

    def body(p_ref, w_ref, m_ref, v_ref, g_ref, d_ref, nm_ref, nv_ref):
        g = p_ref[0].astype(F32)
        for k in range(1, n_parts):
            g = g + p_ref[k].astype(F32)
        nm = ADAM_B1 * m_ref[...] + (1.0 - ADAM_B1) * g
        nv = ADAM_B2 * v_ref[...] + (1.0 - ADAM_B2) * (g * g)
        g_ref[...] = g
        nm_ref[...] = nm
        nv_ref[...] = nv
        d_ref[...] = -ADAM_LR * ((nm * c1) / (jnp.sqrt(nv * c2) + ADAM_EPS) + ADAM_WD * w_ref[...])

    blk = pl.BlockSpec((tr, C), lambda i: (i, 0))
    return _call(body, name, (R // tr,), [pl.BlockSpec((n_parts, tr, C), lambda i: (0, i, 0)), blk, blk, blk],
                 (blk, blk, blk, blk), tuple(_sds((R, C), F32) for _ in range(4)))(parts, w, m, v)


def _ffn_fwd(x, nw, sh, sc, g, w_all, j, w_dn):
    h = norm_mod(x, nw, sh, sc, "ffn_norm")
    gu, a = ffn_up(h, w_all, j)
    y, xn = ffn_down(a, w_dn, x, g)
    return xn, (x, h, gu, a, y)


def _ffn_bwd(dxo, saved, nw, sc, g, w_all, j, w_dn, d_wgu_prev):
    x, h, gu, a, y = saved
    dy, acc1 = resid_gate_bwd(dxo, y, g, 0.5, "ffn_gate_bwd")
    d_wdn = ffn_down_wgrad(a, dy)
    dgu = ffn_down_dgrad(dy, w_dn, gu).reshape((N_DEV,) + a.shape[1:])
    d_wgu = ffn_up_wgrad(h, dgu, j, d_wgu_prev)
    dh = ffn_up_dgrad(dgu, w_all, j)
    dx, acc2 = norm_mod_bwd(x, dh, dxo, nw, sc, "ffn_norm_bwd")
    return dx, d_wgu, d_wdn, (acc2[0], acc2[1], acc1[0]), acc2[2]


def _group_layout(a):
    L = a.shape[0]
    return a[:, :SSM_HEADS].reshape(L, SSM_GROUPS, _HPG).transpose(1, 0, 2)


def _ungroup_layout(a):
    L = a.shape[1]
    return jnp.pad(a.transpose(1, 0, 2).reshape(L, SSM_HEADS), ((0, 0), (0, 128 - SSM_HEADS)))


def _pad_row(vec, n=128):
    return jnp.pad(vec.reshape(1, -1), ((0, 0), (0, n - vec.shape[-1])))


def _mamba_fwd(x, nw, sh, sc, g, p):
    h = norm_mod(x, nw, sh, sc, "mix_norm")
    zx = matmul(h, p["w_in"], "nn", F32, "ssm_in")
    xc = conv_fwd(zx, p["conv_w"], p["conv_b"])
    dt, acs = dt_prep(zx, p["dt_bias"], p["a_log"])
    dt_g, acs_g = _group_layout(dt), _group_layout(acs)
    acs_t = acs_g.transpose(0, 2, 1)
    y, pst = ssd_fwd(xc, dt_g, acs_g, acs_t, p["d_exp"])
    yn = gate_norm(y, zx, p["norm_w"])
    yo, xn = matmul(yn, p["w_out"], "nn", BF, "ssm_out", res=x, gate=g, coef=1.0)
    return xn, (x, h, zx, xc, dt, dt_g, acs_g, acs_t, y, pst, yn, yo)


def _mamba_bwd(dxo, saved, nw, sc, g, p):
    x, h, zx, xc, dt, dt_g, acs_g, acs_t, y, pst, yn, yo = saved
    dyo, acc1 = resid_gate_bwd(dxo, yo, g, 1.0, "mix_gate_bwd")
    d_wout = matmul(yn, dyo, "tn", BF, "ssm_out_wgrad")
    dyn = matmul(dyo, p["w_out"], "nt", F32, "ssm_out_dgrad")
    dy, dz, accn = gate_norm_bwd(dyn, y, zx, p["norm_w"])
    dxs, dB, dC, ddt_g, da_g, dd = ssd_bwd(dy, xc, dt_g, acs_g, acs_t, pst, p["d_exp"])
    dxc = jnp.concatenate([dxs, dB, dC], axis=1)
    ds = conv_bwd_act(dxc, zx, p["conv_w"], p["conv_b"])
    du, accc = conv_bwd(ds, zx, p["conv_w"])
    draw, accdt = dt_bwd(_ungroup_layout(ddt_g), _ungroup_layout(da_g), dt, zx, p["dt_bias"], p["a_log"])
    dzx = jnp.concatenate([dz, du, draw], axis=1)
    d_win = matmul(h, dzx, "tn", BF, "ssm_in_wgrad")[:, :IN_PROJ]
    dh = matmul(dzx, p["w_in"], "nt", F32, "ssm_in_dgrad")
    dx, acc2 = norm_mod_bwd(x, dh, dxo, nw, sc, "mix_norm_bwd")
    small = dict(conv_w=accc[:CONV_WIDTH], conv_b=accc[CONV_WIDTH], dt_bias=accdt[0, :SSM_HEADS],
                 a_log=accdt[1, :SSM_HEADS], d=dd.reshape(SSM_HEADS, SSM_HEADDIM).sum(-1), norm_w=accn[0])
    return dx, d_win, d_wout, (acc2[0], acc2[1], acc1[0]), acc2[2], small


def _attn_layer_fwd(x, nw, sh, sc, g, p, kv):
    h = norm_mod(x, nw, sh, sc, "mix_norm")
    q = matmul(h, p["w_q"], "nn", F32, "attn_q", bias=p["b_q"])
    o = attn_fwd(q, kv, p["sinks"])
    yo, xn = matmul(o, p["w_o"], "nn", BF, "attn_o", bias=p["b_o"], res=x, gate=g, coef=1.0)
    return xn, (x, h, q, o, yo)


def _attn_layer_bwd(dxo, saved, nw, sc, g, p, kv):
    x, h, q, o, yo = saved
    dyo, acc1 = resid_gate_bwd(dxo, yo, g, 1.0, "mix_gate_bwd")
    d_wo = matmul(o, dyo, "tn", BF, "attn_o_wgrad")
    do = matmul(dyo, p["w_o"], "nt", F32, "attn_o_dgrad")
    dq, dkv_c, dkv_p, acca = attn_bwd(q, kv, do, p["sinks"])
    d_wq = matmul(h, dq, "tn", BF, "attn_q_wgrad")
    dh = matmul(dq, p["w_q"], "nt", F32, "attn_q_dgrad")
    dx, acc2 = norm_mod_bwd(x, dh, dxo, nw, sc, "mix_norm_bwd")
    small = dict(b_q=acca[0], sinks=acca[1, :ATT_HEADS], b_o=acc1[1])
    return dx, d_wq, d_wo, (acc2[0], acc2[1], acc1[0]), acc2[2], small, (dkv_c, dkv_p)


def _pack_rows(pieces):
    rows, spans, off = [], [], 0
    for a in pieces:
        flat = a.reshape(-1).astype(F32)
        n = -(-flat.shape[0] // D_MODEL)
        rows.append(jnp.pad(flat, (0, n * D_MODEL - flat.shape[0])).reshape(n, D_MODEL))
        spans.append((off, a.shape))
        off += n
    pad = -off % 8
    if pad:
        rows.append(jnp.zeros((pad, D_MODEL), F32))
    return jnp.concatenate(rows, axis=0), spans, off + pad


def _unpack_rows(g, spans):
    out = []
    for off, shape in spans:
        size = 1
        for s in shape:
            size *= s
        n = -(-size // D_MODEL)
        out.append(g[:, off:off + n].reshape(N_DEV, n * D_MODEL)[:, :size].reshape((N_DEV,) + tuple(shape)))
    return out


def _unshard_last(g):
    nd = g.ndim
    perm = tuple(range(1, nd - 1)) + (0, nd - 1)
    t = g.transpose(perm)
    return t.reshape(t.shape[:-2] + (N_DEV * g.shape[-1],))


def _shard_last(a, me):
    s = a.shape[-1] // N_DEV
    return lax.dynamic_slice_in_dim(a, me * s, s, axis=a.ndim - 1)


def kernel(x, c, ffn_norm_w, ffn_w_gu, ffn_w_down, mod_w, mod_b, mix_norm_w, ssm_w_in, ssm_conv_w, ssm_conv_b, ssm_dt_bias, ssm_a_log, ssm_d, ssm_norm_w, ssm_w_out, kv_norm_w, kv_mod_w, kv_mod_b, w_kv, b_kv, attn_w_q, attn_b_q, attn_sinks, attn_w_o, attn_b_o, final_norm_w, loss_target, m_ffn_norm_w, m_ffn_w_gu, m_ffn_w_down, m_mod_w, m_mod_b, m_mix_norm_w, m_ssm_w_in, m_ssm_conv_w, m_ssm_conv_b, m_ssm_dt_bias, m_ssm_a_log, m_ssm_d, m_ssm_norm_w, m_ssm_w_out, m_kv_norm_w, m_kv_mod_w, m_kv_mod_b, m_w_kv, m_b_kv, m_attn_w_q, m_attn_b_q, m_attn_sinks, m_attn_w_o, m_attn_b_o, m_final_norm_w, v_ffn_norm_w, v_ffn_w_gu, v_ffn_w_down, v_mod_w, v_mod_b, v_mix_norm_w, v_ssm_w_in, v_ssm_conv_w, v_ssm_conv_b, v_ssm_dt_bias, v_ssm_a_log, v_ssm_d, v_ssm_norm_w, v_ssm_w_out, v_kv_norm_w, v_kv_mod_w, v_kv_mod_b, v_w_kv, v_b_kv, v_attn_w_q, v_attn_b_q, v_attn_sinks, v_attn_w_o, v_attn_b_o, v_final_norm_w):
    D = D_MODEL
    me = 4 * lax.axis_index("x") + 2 * lax.axis_index("y") + lax.axis_index("c")
    xs = x[0]
    target = loss_target[0]
    mod_cols = mod_w.shape[-1]
    kvm_cols = kv_mod_w.shape[-1]

    packed, spans, _ = _pack_rows([c, ffn_norm_w, ssm_conv_w, ssm_conv_b, ssm_norm_w])
    nrow = packed.shape[0]
    g1 = small_all_gather(packed).reshape(N_DEV, nrow, D)
    c_all, fnw_g, cw_g, cb_g, snw_g = _unpack_rows(g1, spans)
    c_all = c_all.reshape(N_DEV, D)
    ffn_nw = _unshard_last(fnw_g)
    conv_w = _unshard_last(cw_g)
    conv_b = _unshard_last(cb_g)
    ssm_nw = _unshard_last(snw_g)

    mod_b_loc = lax.dynamic_slice_in_dim(mod_b, me * mod_cols, mod_cols, axis=1).reshape(DEPTH, 1, mod_cols)
    kvb_loc = lax.dynamic_slice_in_dim(kv_mod_b, me * kvm_cols, kvm_cols, axis=0).reshape(1, 1, kvm_cols)
    modp, c_act = mod_fwd(c_all, mod_w, mod_b_loc, "mod_fwd")
    kvmp, _ = mod_fwd(c_all, kv_mod_w.reshape(1, D, kvm_cols), kvb_loc, "kv_mod_fwd")
    packed2, spans2, _ = _pack_rows([modp, kvmp])
    nrow2 = packed2.shape[0]
    g2 = small_all_gather(packed2).reshape(N_DEV, nrow2, D)
    modp_g, kvmp_g = _unpack_rows(g2, spans2)
    mod_all = modp_g.transpose(1, 2, 0, 3).reshape(DEPTH, N_DEV, N_MOD * D)
    kvm_all = kvmp_g.transpose(1, 2, 0, 3).reshape(N_DEV, 2 * D)
    mod_me = lax.dynamic_index_in_dim(mod_all, me, axis=1, keepdims=False).reshape(DEPTH, N_MOD, 1, D)
    kvm_me = lax.dynamic_index_in_dim(kvm_all, me, axis=0, keepdims=False).reshape(2, 1, D)

    r_dn = ffn_w_down.shape[2]
    r_mix = ssm_w_out.shape[1]
    r_at = attn_w_q.shape[1]

    def layer_pack(k):
        if k < N_A:
            mix = [ssm_w_out[k].astype(BF)]
        else:
            mix = [attn_w_q[k - N_A].astype(BF), attn_w_o[k - N_A].astype(BF)]
        arrs = [ffn_w_gu[k].astype(BF).reshape(2 * D, -1),
                jnp.concatenate([ffn_w_down[k].astype(BF).reshape(-1, D)] + mix, axis=0)]
        if k < N_A:
            arrs.append(ssm_w_in[k].astype(BF))
        if k == N_A:
            arrs.append(w_kv.astype(BF))
        return arrs

    packs = [layer_pack(k) for k in range(DEPTH)]
    gathered = [None] * DEPTH
    gathered[0] = big_all_gather(packs[0])
    pending, tok_next = xfer_start(packs[1], False, gathered[0][1], "gather_start_1")

    def w_dn_full(i, j):
        return gathered[i][1][:, j * r_dn:(j + 1) * r_dn].reshape(_NS, D_FF // _NS, D)

    def mix_rows(i, lo, r):
        return gathered[i][1][:, 2 * r_dn + lo:2 * r_dn + lo + r].reshape(N_DEV * r, D)

    def mamba_params(j):
        w_in = jnp.pad(gathered[j][2].transpose(1, 0, 2).reshape(D, IN_PROJ), ((0, 0), (0, IN_PROJ_PAD - IN_PROJ)))
        return dict(w_in=w_in, w_out=mix_rows(j, 0, r_mix), conv_w=conv_w[j], conv_b=conv_b[j].reshape(1, -1),
                    dt_bias=_pad_row(ssm_dt_bias[j]), a_log=_pad_row(ssm_a_log[j]),
                    d_exp=jnp.repeat(ssm_d[j], SSM_HEADDIM).reshape(SSM_GROUPS, 1, _GW),
                    norm_w=ssm_nw[j].reshape(1, -1))

    def attn_params(j):
        return dict(w_q=mix_rows(N_A + j, 0, r_at), w_o=mix_rows(N_A + j, r_at, r_at),
                    b_q=attn_b_q[j].reshape(1, -1), b_o=attn_b_o[j].reshape(1, -1), sinks=_pad_row(attn_sinks[j]))

    saved = []
    kv = None
    kv_saved = None
    w_kv_full = None
    xcur = xs
    for i in range(DEPTH):
        if i >= 1:
            gathered[i] = xfer_wait(pending, xcur, "gather_wait_%d" % i)
            if i + 1 < DEPTH:
                pending, tok_next = xfer_start(packs[i + 1], False, xcur, "gather_start_%d" % (i + 1))
        md = mod_me[i]
        if i + 1 < DEPTH:
            md = md + tok_next[0, 0]
        if i == N_A:
            w_kv_full = gathered[N_A][2].reshape(D, KV_DIM)
            h_kv = norm_mod(xcur, kv_norm_w.reshape(1, D), kvm_me[0], kvm_me[1], "kv_norm")
            kv = matmul(h_kv, w_kv_full, "nn", F32, "kv_proj", bias=b_kv.reshape(1, -1))
            kv_saved = (xcur, h_kv)
        x1, s1 = _ffn_fwd(xcur, ffn_nw[i, 0].reshape(1, D), md[0], md[1], md[2], gathered[i][0], 0, w_dn_full(i, 0))
        if i < N_A:
            pm = mamba_params(i)
            x2, s2 = _mamba_fwd(x1, mix_norm_w[i].reshape(1, D), md[3], md[4], md[5], pm)
        else:
            pm = attn_params(i - N_A)
            x2, s2 = _attn_layer_fwd(x1, mix_norm_w[i].reshape(1, D), md[3], md[4], md[5], pm, kv)
        x3, s3 = _ffn_fwd(x2, ffn_nw[i, 1].reshape(1, D), md[6], md[7], md[8], gathered[i][0], 1, w_dn_full(i, 1))
        saved.append((s1, s2, s3, pm))
        xcur = x3

    dx, accf = final_loss(xcur, final_norm_w.reshape(1, D), target)
    d_gu = [[None, None] for _ in range(DEPTH)]
    d_dn = [[None, None] for _ in range(DEPTH)]
    d_mod = [None] * DEPTH
    d_ffn_nw = [[None, None] for _ in range(DEPTH)]
    d_mix_nw = [None] * DEPTH
    d_in, d_out, d_q, d_o = [None] * N_A, [None] * N_A, [None] * N_A, [None] * N_A
    sm_m, sm_a = [None] * N_A, [None] * N_A
    kv_parts = [None] * N_A
    d_kv_w = d_kvm = d_kv_nw = d_bkv = None
    exch = [None] * DEPTH
    tok_prev = None
    for i in reversed(range(DEPTH)):
        md = mod_me[i]
        if tok_prev is not None:
            md = md + tok_prev[0, 0]
        s1, s2, s3, pm = saved[i]
        dx, d_gu_half, d_dn[i][1], m2, d_ffn_nw[i][1] = _ffn_bwd(
            dx, s3, ffn_nw[i, 1].reshape(1, D), md[7], md[8], gathered[i][0], 1, w_dn_full(i, 1), None)
        if i < N_A:
            dx, d_in[i], d_out[i], mm_, d_mix_nw[i], sm_m[i] = _mamba_bwd(
                dx, s2, mix_norm_w[i].reshape(1, D), md[4], md[5], pm)
        else:
            j = i - N_A
            dx, d_q[j], d_o[j], mm_, d_mix_nw[i], sm_a[j], kv_parts[j] = _attn_layer_bwd(
                dx, s2, mix_norm_w[i].reshape(1, D), md[4], md[5], pm, kv)
        dx, d_gu[i], d_dn[i][0], m1, d_ffn_nw[i][0] = _ffn_bwd(
            dx, s1, ffn_nw[i, 0].reshape(1, D), md[1], md[2], gathered[i][0], 0, w_dn_full(i, 0), d_gu_half)
        d_mod[i] = jnp.concatenate(list(m1) + list(mm_) + list(m2), axis=0)
        if i == N_A:
            x_kv, h_kv = kv_saved
            dkv, acck = kv_grad_combine(kv_parts)
            d_bkv = acck[0]
            d_kv_w = matmul(h_kv, dkv, "tn", BF, "kv_wgrad")
            dh_kv = matmul(dkv, w_kv_full, "nt", F32, "kv_dgrad")
            dx, acc_kv = norm_mod_bwd(x_kv, dh_kv, dx, kv_norm_w.reshape(1, D), kvm_me[1], "kv_norm_bwd")
            d_kvm = jnp.concatenate([acc_kv[0], acc_kv[1]], axis=0)
            d_kv_nw = acc_kv[2]
        mixg = [d_out[i]] if i < N_A else [d_q[i - N_A], d_o[i - N_A]]
        e = [d_gu[i],
             jnp.concatenate([t.reshape(N_DEV, -1, D) for t in d_dn[i] + mixg], axis=1)]
        if i < N_A:
            e.append(d_in[i].reshape(D, N_DEV, -1).transpose(1, 0, 2))
        if i == N_A:
            e.append(d_kv_w.reshape(N_DEV, -1, KV_DIM))
        exch[i], tok_prev = xfer_start(e, True, dx, "exch_start_%d" % i)
    grad_x = dx.reshape(x.shape)

    recv = [None] * DEPTH
    for i in reversed(range(DEPTH)):
        recv[i] = xfer_wait(exch[i], dx, "exch_wait_%d" % i)
    r_gu = jnp.concatenate([recv[i][0] for i in range(DEPTH)], axis=1)
    r_dn_all = jnp.concatenate([recv[i][1][:, :2 * r_dn] for i in range(DEPTH)], axis=1)
    r_out_all = jnp.concatenate([recv[i][1][:, 2 * r_dn:] for i in range(N_A)], axis=1)
    r_q_all = jnp.concatenate([recv[N_A + j][1][:, 2 * r_dn:2 * r_dn + r_at] for j in range(N_A)], axis=1)
    r_o_all = jnp.concatenate([recv[N_A + j][1][:, 2 * r_dn + r_at:] for j in range(N_A)], axis=1)
    r_in = jnp.concatenate([recv[i][2] for i in range(N_A)], axis=1)
    r_kv = recv[N_A][2]

    small_list = [
        jnp.stack(d_mod, 0), d_kvm,
        jnp.stack([jnp.stack(r, 0) for r in d_ffn_nw], 0),
        jnp.stack(d_mix_nw, 0),
        jnp.stack([s["conv_w"] for s in sm_m], 0), jnp.stack([s["conv_b"] for s in sm_m], 0),
        jnp.stack([s["dt_bias"] for s in sm_m], 0), jnp.stack([s["a_log"] for s in sm_m], 0),
        jnp.stack([s["d"] for s in sm_m], 0), jnp.stack([s["norm_w"] for s in sm_m], 0),
        d_kv_nw, d_bkv,
        jnp.stack([s["b_q"] for s in sm_a], 0), jnp.stack([s["sinks"] for s in sm_a], 0),
        jnp.stack([s["b_o"] for s in sm_a], 0), accf[0], accf[1],
    ]
    packed3, spans3, _ = _pack_rows(small_list)
    nrow3 = packed3.shape[0]
    g3 = small_all_gather(packed3).reshape(N_DEV, nrow3, D)
    (p_mod, p_kvm, p_fnw, p_mnw, p_cw, p_cb, p_dtb, p_al, p_d, p_snw, p_kvnw, p_bkv, p_bq, p_sk, p_bo, p_fin,
     p_loss) = _unpack_rows(g3, spans3)

    loss = 0.5 / D * jnp.sum(p_loss)

    c_act_t = c_act.T
    dmod_loc = _shard_last(p_mod, me).transpose(1, 0, 2)
    dkvm_loc = _shard_last(p_kvm, me).reshape(1, N_DEV, kvm_cols)
    gp_mod_w = mod_wgrad(c_act_t, dmod_loc, "mod_wgrad")
    gp_kvm_w = mod_wgrad(c_act_t, dkvm_loc, "kv_mod_wgrad")[0]

    def as_parts_single(a):
        return a[None]

    def upd(name, parts, w, m, v):
        shp = w.shape
        c_last = shp[-1]
        out = adamw(parts.reshape(parts.shape[0], -1, c_last), w.reshape(-1, c_last), m.reshape(-1, c_last),
                    v.reshape(-1, c_last), "adamw_" + name)
        return tuple(o.reshape(shp) for o in out)

    res = {}
    res["ffn_norm_w"] = upd("ffn_norm_w", _shard_last(p_fnw, me), ffn_norm_w, m_ffn_norm_w, v_ffn_norm_w)
    res["ffn_w_gu"] = upd("ffn_w_gu", r_gu, ffn_w_gu, m_ffn_w_gu, v_ffn_w_gu)
    res["ffn_w_down"] = upd("ffn_w_down", r_dn_all, ffn_w_down, m_ffn_w_down, v_ffn_w_down)
    res["mod_w"] = upd("mod_w", as_parts_single(gp_mod_w), mod_w, m_mod_w, v_mod_w)
    res["mod_b"] = upd("mod_b", p_mod, mod_b, m_mod_b, v_mod_b)
    res["mix_norm_w"] = upd("mix_norm_w", p_mnw, mix_norm_w, m_mix_norm_w, v_mix_norm_w)
    res["ssm_w_in"] = upd("ssm_w_in", r_in, ssm_w_in, m_ssm_w_in, v_ssm_w_in)
    res["ssm_conv_w"] = upd("ssm_conv_w", _shard_last(p_cw, me), ssm_conv_w, m_ssm_conv_w, v_ssm_conv_w)
    res["ssm_conv_b"] = upd("ssm_conv_b", _shard_last(p_cb, me), ssm_conv_b, m_ssm_conv_b, v_ssm_conv_b)
    res["ssm_dt_bias"] = upd("ssm_dt_bias", p_dtb, ssm_dt_bias, m_ssm_dt_bias, v_ssm_dt_bias)
    res["ssm_a_log"] = upd("ssm_a_log", p_al, ssm_a_log, m_ssm_a_log, v_ssm_a_log)
    res["ssm_d"] = upd("ssm_d", p_d, ssm_d, m_ssm_d, v_ssm_d)
    res["ssm_norm_w"] = upd("ssm_norm_w", _shard_last(p_snw, me), ssm_norm_w, m_ssm_norm_w, v_ssm_norm_w)
    res["ssm_w_out"] = upd("ssm_w_out", r_out_all, ssm_w_out, m_ssm_w_out, v_ssm_w_out)
    res["kv_norm_w"] = upd("kv_norm_w", p_kvnw.reshape(N_DEV, 1, D), kv_norm_w.reshape(1, D),
                           m_kv_norm_w.reshape(1, D), v_kv_norm_w.reshape(1, D))
    res["kv_mod_w"] = upd("kv_mod_w", as_parts_single(gp_kvm_w), kv_mod_w, m_kv_mod_w, v_kv_mod_w)
    res["kv_mod_b"] = upd("kv_mod_b", p_kvm.reshape(N_DEV, 1, 2 * D), kv_mod_b.reshape(1, -1),
                          m_kv_mod_b.reshape(1, -1), v_kv_mod_b.reshape(1, -1))
    res["w_kv"] = upd("w_kv", r_kv, w_kv, m_w_kv, v_w_kv)
    res["b_kv"] = upd("b_kv", p_bkv.reshape(N_DEV, 1, KV_DIM), b_kv.reshape(1, -1), m_b_kv.reshape(1, -1),
                      v_b_kv.reshape(1, -1))
    res["attn_w_q"] = upd("attn_w_q", r_q_all, attn_w_q, m_attn_w_q, v_attn_w_q)
    res["attn_b_q"] = upd("attn_b_q", p_bq, attn_b_q, m_attn_b_q, v_attn_b_q)
    res["attn_sinks"] = upd("attn_sinks", p_sk, attn_sinks, m_attn_sinks, v_attn_sinks)
    res["attn_w_o"] = upd("attn_w_o", r_o_all, attn_w_o, m_attn_w_o, v_attn_w_o)
    res["attn_b_o"] = upd("attn_b_o", p_bo, attn_b_o, m_attn_b_o, v_attn_b_o)
    res["final_norm_w"] = upd("final_norm_w", p_fin.reshape(N_DEV, 1, D), final_norm_w.reshape(1, D),
                              m_final_norm_w.reshape(1, D), v_final_norm_w.reshape(1, D))

    names = ["ffn_norm_w", "ffn_w_gu", "ffn_w_down", "mod_w", "mod_b", "mix_norm_w", "ssm_w_in", "ssm_conv_w",
             "ssm_conv_b", "ssm_dt_bias", "ssm_a_log", "ssm_d", "ssm_norm_w", "ssm_w_out", "kv_norm_w", "kv_mod_w",
             "kv_mod_b", "w_kv", "b_kv", "attn_w_q", "attn_b_q", "attn_sinks", "attn_w_o", "attn_b_o", "final_norm_w"]
    vec_shapes = {"kv_norm_w": (D,), "kv_mod_b": (2 * D,), "b_kv": (KV_DIM,), "final_norm_w": (D,)}
    outs = [loss, grad_x]
    for k in range(4):
        for nme in names:
            t = res[nme][k]
            if nme in vec_shapes:
                t = t.reshape(vec_shapes[nme])
            outs.append(t)
    return tuple(outs)
```

```python
import functools

import jax
import jax.numpy as jnp
from jax import lax
from jax.experimental import pallas as pl
from jax.experimental.pallas import tpu as pltpu

F32 = jnp.float32
BF = jnp.bfloat16
MESH = pl.DeviceIdType.MESH

N_DEV = 8
D_MODEL = 1024
DEPTH = 4
N_A = 2
EPS = 1e-5
N_MOD = 9
D_FF = 2816
D_INNER = 2048
SSM_HEADDIM = 64
SSM_HEADS = 32
SSM_GROUPS = 8
SSM_STATE = 128
CONV_WIDTH = 4
CHUNK = 256
CONV_DIM = D_INNER + 2 * SSM_GROUPS * SSM_STATE
IN_PROJ = D_INNER + CONV_DIM + SSM_HEADS
IN_PROJ_PAD = D_INNER + CONV_DIM + 128
ATT_HEADS = 16
KV_HEADS = 4
HEAD_DIM = 64
WINDOW = 128
KV_DIM = 2 * KV_HEADS * HEAD_DIM

ADAM_LR = 0.001
ADAM_B1 = 0.9
ADAM_B2 = 0.999
ADAM_EPS = 1e-08
ADAM_WD = 0.01
ADAM_STEP = 10

VMEM_LIMIT = 48 * 2 ** 20
ADAMW_VMEM_BUDGET = 24 * 2 ** 20
NEG = -1e30


def _call(body, name, grid, in_specs, out_specs, out_shape, scratch=()):
    return pl.pallas_call(
        body, name=name, grid=grid, in_specs=in_specs, out_specs=out_specs, out_shape=out_shape,
        scratch_shapes=list(scratch),
        compiler_params=pltpu.CompilerParams(vmem_limit_bytes=VMEM_LIMIT))


def _tile(n, cap):
    t = (cap // 128) * 128
    while t >= 128:
        if n % t == 0:
            return t
        t -= 128
    return n


def _sds(shape, dtype):
    return jax.ShapeDtypeStruct(shape, dtype)


def _sigmoid(v):
    return 1.0 / (1.0 + jnp.exp(-v))


def _dot(a, b, dims):
    return lax.dot_general(a, b, (dims, ((), ())), preferred_element_type=F32)


def _dot_nn(a, b):
    return _dot(a.astype(BF), b.astype(BF), ((1,), (0,)))


def _dot_nt(a, b):
    return _dot(a.astype(BF), b.astype(BF), ((1,), (1,)))


def _dot_tn(a, b):
    return _dot(a.astype(BF), b.astype(BF), ((0,), (0,)))


def matmul(a, b, mode, out_dtype, name, bias=None, res=None, gate=None, coef=1.0):
    if mode == "nn":
        (M, K), (_, N) = a.shape, b.shape
    elif mode == "nt":
        (M, K), (N, _) = a.shape, b.shape
    else:
        (K, M), (_, N) = a.shape, b.shape
    cap_n = 512 if K > 4096 else 1024
    tm = _tile(M, 1024 if mode == "tn" else 512)
    tn = _tile(N, cap_n)
    if mode == "nn":
        a_spec = pl.BlockSpec((tm, K), lambda i, j: (i, 0))
        b_spec = pl.BlockSpec((K, tn), lambda i, j: (0, j))
        fn = _dot_nn
    elif mode == "nt":
        a_spec = pl.BlockSpec((tm, K), lambda i, j: (i, 0))
        b_spec = pl.BlockSpec((tn, K), lambda i, j: (j, 0))
        fn = _dot_nt
    else:
        a_spec = pl.BlockSpec((K, tm), lambda i, j: (0, i))
        b_spec = pl.BlockSpec((K, tn), lambda i, j: (0, j))
        fn = _dot_tn
    has_bias, has_res = bias is not None, res is not None
    o_spec = pl.BlockSpec((tm, tn), lambda i, j: (i, j))
    v_spec = pl.BlockSpec((1, tn), lambda i, j: (0, j))
    in_specs, args = [a_spec, b_spec], [a, b]
    if has_bias:
        in_specs.append(v_spec)
        args.append(bias)
    if has_res:
        in_specs += [o_spec, v_spec]
        args += [res, gate]

    def body(*refs):
        a_ref, b_ref = refs[0], refs[1]
        k = 2
        y = fn(a_ref[...], b_ref[...])
        if has_bias:
            y = y + refs[k][...]
            k += 1
        if has_res:
            res_ref, gate_ref = refs[k], refs[k + 1]
            refs[k + 2][...] = y.astype(out_dtype)
            refs[k + 3][...] = res_ref[...] + coef * gate_ref[...] * y
        else:
            refs[k][...] = y.astype(out_dtype)

    if has_res:
        out_shape = (_sds((M, N), out_dtype), _sds((M, N), F32))
        out_specs = (o_spec, o_spec)
    else:
        out_shape = _sds((M, N), out_dtype)
        out_specs = o_spec
    return _call(body, name, (M // tm, N // tn), in_specs, out_specs, out_shape)(*args)


def norm_mod(x, nw, sh, sc, name):
    L, D = x.shape
    tm = _tile(L, 512)

    def body(x_ref, nw_ref, sh_ref, sc_ref, h_ref):
        xf = x_ref[...]
        r = lax.rsqrt(jnp.mean(xf * xf, axis=-1, keepdims=True) + EPS)
        n = xf * r * nw_ref[...]
        h_ref[...] = (n * (1.0 + sc_ref[...]) + sh_ref[...]).astype(BF)

    row = pl.BlockSpec((tm, D), lambda i: (i, 0))
    vec = pl.BlockSpec((1, D), lambda i: (0, 0))
    return _call(body, name, (L // tm,), [row, vec, vec, vec], row, _sds((L, D), BF))(x, nw, sh, sc)


def norm_mod_bwd(x, dh, dres, nw, sc, name):
    L, D = x.shape
    tm = _tile(L, 512)

    def body(x_ref, dh_ref, dres_ref, nw_ref, sc_ref, dx_ref, acc_ref):
        @pl.when(pl.program_id(0) == 0)
        def _():
            acc_ref[...] = jnp.zeros_like(acc_ref)

        xf = x_ref[...]
        dhf = dh_ref[...].astype(F32)
        r = lax.rsqrt(jnp.mean(xf * xf, axis=-1, keepdims=True) + EPS)
        xhat = xf * r
        nwv = nw_ref[...]
        dn = dhf * (1.0 + sc_ref[...])
        dxhat = dn * nwv
        proj = jnp.mean(dxhat * xhat, axis=-1, keepdims=True)
        dx_ref[...] = dres_ref[...] + r * (dxhat - xhat * proj)
        acc_ref[0:1, :] += jnp.sum(dhf, axis=0, keepdims=True)
        acc_ref[1:2, :] += jnp.sum(dhf * xhat * nwv, axis=0, keepdims=True)
        acc_ref[2:3, :] += jnp.sum(dn * xhat, axis=0, keepdims=True)

    row = pl.BlockSpec((tm, D), lambda i: (i, 0))
    vec = pl.BlockSpec((1, D), lambda i: (0, 0))
    acc = pl.BlockSpec((8, D), lambda i: (0, 0))
    return _call(body, name, (L // tm,), [row, row, row, vec, vec], (row, acc),
                 (_sds((L, D), F32), _sds((8, D), F32)))(x, dh, dres, nw, sc)


def final_loss(x, nw, target):
    L, D = x.shape
    tm = _tile(L, 512)

    def body(x_ref, nw_ref, t_ref, dx_ref, acc_ref):
        @pl.when(pl.program_id(0) == 0)
        def _():
            acc_ref[...] = jnp.zeros_like(acc_ref)

        xf = x_ref[...]
        r = lax.rsqrt(jnp.mean(xf * xf, axis=-1, keepdims=True) + EPS)
        xhat = xf * r
        nwv = nw_ref[...]
        err = xhat * nwv - t_ref[...]
        dy = err * (1.0 / D)
        dxhat = dy * nwv
        proj = jnp.mean(dxhat * xhat, axis=-1, keepdims=True)
        dx_ref[...] = r * (dxhat - xhat * proj)
        acc_ref[0:1, :] += jnp.sum(dy * xhat, axis=0, keepdims=True)
        acc_ref[1:2, :] += jnp.sum(err * err, axis=0, keepdims=True)

    row = pl.BlockSpec((tm, D), lambda i: (i, 0))
    vec = pl.BlockSpec((1, D), lambda i: (0, 0))
    acc = pl.BlockSpec((8, D), lambda i: (0, 0))
    return _call(body, "final_loss", (L // tm,), [row, vec, row], (row, acc),
                 (_sds((L, D), F32), _sds((8, D), F32)))(x, nw, target)


def resid_gate_bwd(dxo, y, gate, coef, name):
    L, D = dxo.shape
    tm = _tile(L, 512)

    def body(dxo_ref, y_ref, g_ref, dy_ref, acc_ref):
        @pl.when(pl.program_id(0) == 0)
        def _():
            acc_ref[...] = jnp.zeros_like(acc_ref)

        d = dxo_ref[...]
        dy = coef * g_ref[...] * d
        dy_ref[...] = dy.astype(BF)
        acc_ref[0:1, :] += coef * jnp.sum(d * y_ref[...].astype(F32), axis=0, keepdims=True)
        acc_ref[1:2, :] += jnp.sum(dy, axis=0, keepdims=True)

    row = pl.BlockSpec((tm, D), lambda i: (i, 0))
    vec = pl.BlockSpec((1, D), lambda i: (0, 0))
    acc = pl.BlockSpec((8, D), lambda i: (0, 0))
    return _call(body, name, (L // tm,), [row, row, vec], (row, acc),
                 (_sds((L, D), BF), _sds((8, D), F32)))(dxo, y, gate)


def ffn_up(h, wt):
    L, D = h.shape
    F = wt.shape[0] // 2
    tm, tn = _tile(L, 512), _tile(F, 256)
    nj = F // tn

    def body(h_ref, wg_ref, wu_ref, g_ref, u_ref, a_ref):
        hv = h_ref[...]
        g = _dot_nt(hv, wg_ref[...])
        u = _dot_nt(hv, wu_ref[...])
        g_ref[...] = g.astype(BF)
        u_ref[...] = u.astype(BF)
        a_ref[...] = (g * _sigmoid(g) * u).astype(BF)

    o = pl.BlockSpec((tm, tn), lambda i, n: (i, n))
    return _call(body, "ffn_up", (L // tm, nj),
                 [pl.BlockSpec((tm, D), lambda i, n: (i, 0)),
                  pl.BlockSpec((tn, D), lambda i, n: (n, 0)),
                  pl.BlockSpec((tn, D), lambda i, n: (n + nj, 0))],
                 (o, o, o), tuple(_sds((L, F), BF) for _ in range(3)))(h, wt, wt)


def ffn_down_dgrad(dy, wd, g, u):
    L, D = dy.shape
    F = wd.shape[0]
    tm, tn = _tile(L, 512), _tile(F, 256)

    def body(dy_ref, w_ref, g_ref, u_ref, dg_ref, du_ref):
        da = _dot_nt(dy_ref[...], w_ref[...])
        gv = g_ref[...].astype(F32)
        uv = u_ref[...].astype(F32)
        s = _sigmoid(gv)
        dg_ref[...] = (da * uv * s * (1.0 + gv * (1.0 - s))).astype(BF)
        du_ref[...] = (da * gv * s).astype(BF)

    o = pl.BlockSpec((tm, tn), lambda i, n: (i, n))
    return _call(body, "ffn_down_dgrad", (L // tm, F // tn),
                 [pl.BlockSpec((tm, D), lambda i, n: (i, 0)), pl.BlockSpec((tn, D), lambda i, n: (n, 0)), o, o],
                 (o, o), (_sds((L, F), BF), _sds((L, F), BF)))(dy, wd, g, u)


def ffn_up_wgrad(dg, du, h):
    L, F = dg.shape
    D = h.shape[1]
    tm = _tile(F, 256)
    nblk = F // tm

    def half(d, off, prev):
        def body(d_ref, h_ref, *rest):
            rest[-1][...] = _dot_tn(d_ref[...], h_ref[...]).astype(BF)

        in_specs = [pl.BlockSpec((L, tm), lambda i: (0, i)), pl.BlockSpec((L, D), lambda i: (0, 0))]
        args = [d, h]
        aliases = {}
        if prev is not None:
            in_specs.append(pl.BlockSpec(memory_space=pl.ANY))
            args.append(prev)
            aliases = {2: 0}
        return pl.pallas_call(
            body, name="ffn_up_wgrad", grid=(nblk,), in_specs=in_specs,
            out_specs=pl.BlockSpec((tm, D), lambda i: (i + off * nblk, 0)),
            out_shape=_sds((2 * F, D), BF), input_output_aliases=aliases,
            compiler_params=pltpu.CompilerParams(vmem_limit_bytes=VMEM_LIMIT))(*args)

    return half(du, 1, half(dg, 0, None))


def ffn_up_dgrad(dg, du, wt):
    L, F = dg.shape
    D = wt.shape[1]
    tm, tn = _tile(L, 512), _tile(D, 512)

    def body(dg_ref, du_ref, wg_ref, wu_ref, o_ref):
        o_ref[...] = _dot_nn(dg_ref[...], wg_ref[...]) + _dot_nn(du_ref[...], wu_ref[...])

    a = pl.BlockSpec((tm, F), lambda i, n: (i, 0))
    return _call(body, "ffn_up_dgrad", (L // tm, D // tn),
                 [a, a, pl.BlockSpec((F, tn), lambda i, n: (0, n)), pl.BlockSpec((F, tn), lambda i, n: (1, n))],
                 pl.BlockSpec((tm, tn), lambda i, n: (i, n)), _sds((L, D), F32))(dg, du, wt, wt)


def _shift_rows(cur, other, k, down):
    n = cur.shape[0]
    rows = lax.broadcasted_iota(jnp.int32, cur.shape, 0)
    if down:
        return jnp.where(rows < k, pltpu.roll(other, k, 0), pltpu.roll(cur, k, 0))
    return jnp.where(rows >= n - k, pltpu.roll(other, n - k, 0), pltpu.roll(cur, n - k, 0))


def _conv_pre(cur, prev, w_ref, b_ref):
    s = cur * w_ref[CONV_WIDTH - 1:CONV_WIDTH, :] + b_ref[...]
    for k in range(1, CONV_WIDTH):
        s = s + _shift_rows(cur, prev, k, True) * w_ref[CONV_WIDTH - 1 - k:CONV_WIDTH - k, :]
    return s


_XBC_COL0 = D_INNER // 512


def conv_fwd(zx, w, b):
    L = zx.shape[0]
    tm, tc = _tile(L, 256), 512

    def body(cur_ref, prev_ref, w_ref, b_ref, o_ref):
        cur = cur_ref[...]
        prev = jnp.where(pl.program_id(1) > 0, prev_ref[...], 0.0)
        s = _conv_pre(cur, prev, w_ref, b_ref)
        o_ref[...] = s * _sigmoid(s)

    return _call(body, "conv_fwd", (CONV_DIM // tc, L // tm),
                 [pl.BlockSpec((tm, tc), lambda j, i: (i, _XBC_COL0 + j)),
                  pl.BlockSpec((tm, tc), lambda j, i: (jnp.maximum(i - 1, 0), _XBC_COL0 + j)),
                  pl.BlockSpec((CONV_WIDTH, tc), lambda j, i: (0, j)),
                  pl.BlockSpec((1, tc), lambda j, i: (0, j))],
                 pl.BlockSpec((tm, tc), lambda j, i: (i, j)), _sds((L, CONV_DIM), F32))(zx, zx, w, b)


def conv_bwd_act(dxc, zx, w, b):
    L = zx.shape[0]
    tm, tc = _tile(L, 256), 512

    def body(d_ref, cur_ref, prev_ref, w_ref, b_ref, o_ref):
        cur = cur_ref[...]
        prev = jnp.where(pl.program_id(1) > 0, prev_ref[...], 0.0)
        s = _conv_pre(cur, prev, w_ref, b_ref)
        sg = _sigmoid(s)
        o_ref[...] = d_ref[...] * sg * (1.0 + s * (1.0 - sg))

    return _call(body, "conv_bwd_act", (CONV_DIM // tc, L // tm),
                 [pl.BlockSpec((tm, tc), lambda j, i: (i, j)),
                  pl.BlockSpec((tm, tc), lambda j, i: (i, _XBC_COL0 + j)),
                  pl.BlockSpec((tm, tc), lambda j, i: (jnp.maximum(i - 1, 0), _XBC_COL0 + j)),
                  pl.BlockSpec((CONV_WIDTH, tc), lambda j, i: (0, j)),
                  pl.BlockSpec((1, tc), lambda j, i: (0, j))],
                 pl.BlockSpec((tm, tc), lambda j, i: (i, j)), _sds((L, CONV_DIM), F32))(dxc, zx, zx, w, b)


def conv_bwd(ds, zx, w):
    L = zx.shape[0]
    tm, tc = _tile(L, 256), 512
    nblk = L // tm

    def body(ds_ref, dsn_ref, cur_ref, prev_ref, w_ref, du_ref, acc_ref):
        i = pl.program_id(1)

        @pl.when(i == 0)
        def _():
            acc_ref[...] = jnp.zeros_like(acc_ref)

        ds_c = ds_ref[...]
        ds_n = jnp.where(i < nblk - 1, dsn_ref[...], 0.0)
        cur = cur_ref[...]
        prev = jnp.where(i > 0, prev_ref[...], 0.0)
        du = ds_c * w_ref[CONV_WIDTH - 1:CONV_WIDTH, :]
        acc_ref[CONV_WIDTH - 1:CONV_WIDTH, :] += jnp.sum(ds_c * cur, axis=0, keepdims=True)
        for k in range(1, CONV_WIDTH):
            du = du + _shift_rows(ds_c, ds_n, k, False) * w_ref[CONV_WIDTH - 1 - k:CONV_WIDTH - k, :]
            acc_ref[CONV_WIDTH - 1 - k:CONV_WIDTH - k, :] += jnp.sum(
                ds_c * _shift_rows(cur, prev, k, True), axis=0, keepdims=True)
        acc_ref[CONV_WIDTH:CONV_WIDTH + 1, :] += jnp.sum(ds_c, axis=0, keepdims=True)
        du_ref[...] = du.astype(BF)

    return _call(body, "conv_bwd", (CONV_DIM // tc, nblk),
                 [pl.BlockSpec((tm, tc), lambda j, i: (i, j)),
                  pl.BlockSpec((tm, tc), lambda j, i: (jnp.minimum(i + 1, nblk - 1), j)),
                  pl.BlockSpec((tm, tc), lambda j, i: (i, _XBC_COL0 + j)),
                  pl.BlockSpec((tm, tc), lambda j, i: (jnp.maximum(i - 1, 0), _XBC_COL0 + j)),
                  pl.BlockSpec((CONV_WIDTH, tc), lambda j, i: (0, j))],
                 (pl.BlockSpec((tm, tc), lambda j, i: (i, j)), pl.BlockSpec((8, tc), lambda j, i: (0, j))),
                 (_sds((L, CONV_DIM), BF), _sds((8, CONV_DIM), F32)))(ds, ds, zx, zx, w)


_DT_COL = (D_INNER + CONV_DIM) // 128


def dt_prep(zx, bias_pad, alog_pad):
    L = zx.shape[0]

    def body(raw_ref, b_ref, al_ref, dt_ref, acs_ref):
        v = raw_ref[...] + b_ref[...]
        dt = jnp.maximum(v, 0.0) + jnp.log(1.0 + jnp.exp(-jnp.abs(v)))
        dt_ref[...] = dt
        acs = dt * (-jnp.exp(al_ref[...]))
        rows = lax.broadcasted_iota(jnp.int32, acs.shape, 0)
        s = 1
        while s < CHUNK:
            acs = acs + jnp.where(rows >= s, pltpu.roll(acs, s, 0), 0.0)
            s *= 2
        acs_ref[...] = acs

    blk = pl.BlockSpec((CHUNK, 128), lambda i: (i, 0))
    vec = pl.BlockSpec((1, 128), lambda i: (0, 0))
    return _call(body, "dt_prep", (L // CHUNK,),
                 [pl.BlockSpec((CHUNK, 128), lambda i: (i, _DT_COL)), vec, vec], (blk, blk),
                 (_sds((L, 128), F32), _sds((L, 128), F32)))(zx, bias_pad, alog_pad)


def dt_bwd(ddt, da, dt, zx, bias_pad, alog_pad):
    L = zx.shape[0]
    tm = _tile(L, 512)

    def body(ddt_ref, da_ref, dt_ref, raw_ref, b_ref, al_ref, o_ref, acc_ref):
        @pl.when(pl.program_id(0) == 0)
        def _():
            acc_ref[...] = jnp.zeros_like(acc_ref)

        A = -jnp.exp(al_ref[...])
        dav = da_ref[...]
        dd = ddt_ref[...] + dav * A
        draw = dd * _sigmoid(raw_ref[...] + b_ref[...])
        o_ref[...] = draw.astype(BF)
        acc_ref[0:1, :] += jnp.sum(draw, axis=0, keepdims=True)
        acc_ref[1:2, :] += jnp.sum(dav * dt_ref[...], axis=0, keepdims=True) * A

    blk = pl.BlockSpec((tm, 128), lambda i: (i, 0))
    vec = pl.BlockSpec((1, 128), lambda i: (0, 0))
    return _call(body, "dt_bwd", (L // tm,),
                 [blk, blk, blk, pl.BlockSpec((tm, 128), lambda i: (i, _DT_COL)), vec, vec],
                 (blk, pl.BlockSpec((8, 128), lambda i: (0, 0))),
                 (_sds((L, 128), BF), _sds((8, 128), F32)))(ddt, da, dt, zx, bias_pad, alog_pad)


_HPG = SSM_HEADS // SSM_GROUPS
_GW = _HPG * SSM_HEADDIM
_B_COL0 = D_INNER // SSM_STATE
_C_COL0 = (D_INNER + SSM_GROUPS * SSM_STATE) // SSM_STATE


def _ssd_head(x, dtc, ac, ar, r, causal):
    xh = x[:, SSM_HEADDIM * r:SSM_HEADDIM * (r + 1)]
    acol = ac[:, r:r + 1]
    arow = ar[r:r + 1, :]
    alast = ar[r:r + 1, CHUNK - 1:CHUNK]
    lm = jnp.exp(jnp.where(causal, acol - arow, NEG))
    return xh, xh * dtc[:, r:r + 1], acol, alast, lm


def ssd_fwd(xc, dt_g, acs_g, acsT_g, d_exp):
    L = xc.shape[0]
    nc = L // CHUNK

    def body(x_ref, b_ref, c_ref, dt_ref, ac_ref, ar_ref, d_ref, y_ref, pst_ref, st_ref):
        @pl.when(pl.program_id(1) == 0)
        def _():
            st_ref[...] = jnp.zeros_like(st_ref)

        x, Bm, Cm = x_ref[...], b_ref[...], c_ref[...]
        dtc, ac, ar = dt_ref[...], ac_ref[...], ar_ref[...]
        causal = lax.broadcasted_iota(jnp.int32, (CHUNK, CHUNK), 0) >= lax.broadcasted_iota(jnp.int32, (CHUNK, CHUNK), 1)
        CB = _dot_nt(Cm, Bm)
        for r in range(_HPG):
            xh, xd, acol, alast, lm = _ssd_head(x, dtc, ac, ar, r, causal)
            P = st_ref[r]
            y = _dot_nn(CB * lm, xd) + jnp.exp(acol) * _dot_nt(Cm, P)
            y_ref[:, SSM_HEADDIM * r:SSM_HEADDIM * (r + 1)] = y + d_ref[:, SSM_HEADDIM * r:SSM_HEADDIM * (r + 1)] * xh
            pst_ref[r] = P
            st_ref[r] = P * jnp.exp(alast) + _dot_tn(xd * jnp.exp(alast - acol), Bm)

    return _call(
        body, "ssd_fwd", (SSM_GROUPS, nc),
        [pl.BlockSpec((CHUNK, _GW), lambda g, c: (c, g)),
         pl.BlockSpec((CHUNK, SSM_STATE), lambda g, c: (c, _B_COL0 + g)),
         pl.BlockSpec((CHUNK, SSM_STATE), lambda g, c: (c, _C_COL0 + g)),
         pl.BlockSpec((None, CHUNK, _HPG), lambda g, c: (g, c, 0)),
         pl.BlockSpec((None, CHUNK, _HPG), lambda g, c: (g, c, 0)),
         pl.BlockSpec((None, _HPG, CHUNK), lambda g, c: (g, 0, c)),
         pl.BlockSpec((None, 1, _GW), lambda g, c: (g, 0, 0))],
        (pl.BlockSpec((CHUNK, _GW), lambda g, c: (c, g)),
         pl.BlockSpec((None, None, _HPG, SSM_HEADDIM, SSM_STATE), lambda g, c: (c, g, 0, 0, 0))),
        (_sds((L, D_INNER), F32), _sds((nc, SSM_GROUPS, _HPG, SSM_HEADDIM, SSM_STATE), F32)),
        scratch=[pltpu.VMEM((_HPG, SSM_HEADDIM, SSM_STATE), F32)],
    )(xc, xc, xc, dt_g, acs_g, acsT_g, d_exp)


def ssd_bwd(dy, xc, dt_g, acs_g, acsT_g, pst, d_exp):
    L = xc.shape[0]
    nc = L // CHUNK

    def body(dy_ref, x_ref, b_ref, c_ref, dt_ref, ac_ref, ar_ref, pst_ref, d_ref,
             dx_ref, db_ref, dc_ref, ddt_ref, da_ref, dd_ref, dp_ref):
        @pl.when(pl.program_id(1) == 0)
        def _():
            dp_ref[...] = jnp.zeros_like(dp_ref)
            dd_ref[...] = jnp.zeros_like(dd_ref)

        dyv, x, Bm, Cm = dy_ref[...], x_ref[...], b_ref[...], c_ref[...]
        dtc, ac, ar = dt_ref[...], ac_ref[...], ar_ref[...]
        ri = lax.broadcasted_iota(jnp.int32, (CHUNK, CHUNK), 0)
        ci = lax.broadcasted_iota(jnp.int32, (CHUNK, CHUNK), 1)
        causal = ri >= ci
        lane4 = lax.broadcasted_iota(jnp.int32, (CHUNK, _HPG), 1)
        CB = _dot_nt(Cm, Bm)
        dB = jnp.zeros((CHUNK, SSM_STATE), F32)
        dC = jnp.zeros((CHUNK, SSM_STATE), F32)
        dCB = jnp.zeros((CHUNK, CHUNK), F32)
        ddt_blk = jnp.zeros((CHUNK, _HPG), F32)
        da_blk = jnp.zeros((CHUNK, _HPG), F32)
        for r in range(_HPG):
            sl = slice(SSM_HEADDIM * r, SSM_HEADDIM * (r + 1))
            xh, xd, acol, alast, lm = _ssd_head(x, dtc, ac, ar, r, causal)
            dyh = dyv[:, sl]
            P = pst_ref[r]
            dPn = dp_ref[r]
            eA = jnp.exp(acol)
            cd = jnp.exp(alast)
            dte = jnp.exp(alast - acol)
            G = CB * lm
            Z = _dot_nt(Cm, P)
            dZ = eA * dyh
            dC = dC + _dot_nn(dZ, P)
            dp_ref[r] = dPn * cd + _dot_tn(dZ, Cm)
            dA_col = jnp.sum(dZ * Z, axis=1, keepdims=True)
            BdS = _dot_nt(Bm, dPn)
            dxd = dte * BdS
            dB = dB + dte * _dot_nn(xd, dPn)
            t = jnp.sum(xd * BdS, axis=1, keepdims=True) * dte
            dA_col = dA_col - t
            dA_last = jnp.sum(t, axis=0, keepdims=True) + jnp.sum(
                jnp.sum(dPn * P, axis=1, keepdims=True), axis=0, keepdims=True) * cd
            dG = _dot_nt(dyh, xd)
            dxd = dxd + _dot_tn(G, dyh)
            dCB = dCB + dG * lm
            W = dG * G
            dA_col = dA_col + jnp.sum(W, axis=1, keepdims=True)
            dA_row = jnp.sum(jnp.where(ri == ci, dA_col, 0.0), axis=0, keepdims=True) - jnp.sum(W, axis=0, keepdims=True)
            da_col = jnp.sum(jnp.where(ci >= ri, dA_row, 0.0), axis=1, keepdims=True) + dA_last
            da_blk = jnp.where(lane4 == r, da_col, da_blk)
            ddt_blk = jnp.where(lane4 == r, jnp.sum(dxd * xh, axis=1, keepdims=True), ddt_blk)
            dx_ref[:, sl] = dxd * dtc[:, r:r + 1] + d_ref[:, sl] * dyh
        dc_ref[...] = dC + _dot_nn(dCB, Bm)
        db_ref[...] = dB + _dot_tn(dCB, Cm)
        ddt_ref[...] = ddt_blk
        da_ref[...] = da_blk
        dd_ref[...] += jnp.sum(dyv * x, axis=0, keepdims=True)

    rc = lambda g, c: (nc - 1 - c, g)
    small = pl.BlockSpec((None, CHUNK, _HPG), lambda g, c: (g, nc - 1 - c, 0))
    return _call(
        body, "ssd_bwd", (SSM_GROUPS, nc),
        [pl.BlockSpec((CHUNK, _GW), rc),
         pl.BlockSpec((CHUNK, _GW), rc),
         pl.BlockSpec((CHUNK, SSM_STATE), lambda g, c: (nc - 1 - c, _B_COL0 + g)),
         pl.BlockSpec((CHUNK, SSM_STATE), lambda g, c: (nc - 1 - c, _C_COL0 + g)),
         small, small,
         pl.BlockSpec((None, _HPG, CHUNK), lambda g, c: (g, 0, nc - 1 - c)),
         pl.BlockSpec((None, None, _HPG, SSM_HEADDIM, SSM_STATE), lambda g, c: (nc - 1 - c, g, 0, 0, 0)),
         pl.BlockSpec((None, 1, _GW), lambda g, c: (g, 0, 0))],
        (pl.BlockSpec((CHUNK, _GW), rc),
         pl.BlockSpec((CHUNK, SSM_STATE), rc),
         pl.BlockSpec((CHUNK, SSM_STATE), rc),
         small, small,
         pl.BlockSpec((None, 1, _GW), lambda g, c: (g, 0, 0))),
        (_sds((L, D_INNER), F32), _sds((L, SSM_GROUPS * SSM_STATE), F32), _sds((L, SSM_GROUPS * SSM_STATE), F32),
         _sds((SSM_GROUPS, L, _HPG), F32), _sds((SSM_GROUPS, L, _HPG), F32), _sds((SSM_GROUPS, 1, _GW), F32)),
        scratch=[pltpu.VMEM((_HPG, SSM_HEADDIM, SSM_STATE), F32)],
    )(dy, xc, xc, xc, dt_g, acs_g, acsT_g, pst, d_exp)


_NGW = D_INNER // SSM_GROUPS


def gate_norm(y, zx, nw):
    L = y.shape[0]
    tm = _tile(L, 256)

    def body(y_ref, z_ref, nw_ref, o_ref):
        for g in range(SSM_GROUPS):
            sl = slice(_NGW * g, _NGW * (g + 1))
            z = z_ref[:, sl]
            y2 = y_ref[:, sl] * (z * _sigmoid(z))
            r = lax.rsqrt(jnp.mean(y2 * y2, axis=-1, keepdims=True) + EPS)
            o_ref[:, sl] = (y2 * r * nw_ref[:, sl]).astype(BF)

    row = pl.BlockSpec((tm, D_INNER), lambda i: (i, 0))
    return _call(body, "gate_norm", (L // tm,), [row, row, pl.BlockSpec((1, D_INNER), lambda i: (0, 0))],
                 row, _sds((L, D_INNER), BF))(y, zx, nw)


def gate_norm_bwd(dyn, y, zx, nw):
    L = y.shape[0]
    tm = _tile(L, 256)

    def body(d_ref, y_ref, z_ref, nw_ref, dy_ref, dz_ref, acc_ref):
        @pl.when(pl.program_id(0) == 0)
        def _():
            acc_ref[...] = jnp.zeros_like(acc_ref)

        for g in range(SSM_GROUPS):
            sl = slice(_NGW * g, _NGW * (g + 1))
            z = z_ref[:, sl]
            yv = y_ref[:, sl]
            sg = _sigmoid(z)
            sz = z * sg
            y2 = yv * sz
            r = lax.rsqrt(jnp.mean(y2 * y2, axis=-1, keepdims=True) + EPS)
            yh = y2 * r
            d = d_ref[:, sl]
            dn = d * nw_ref[:, sl]
            dy2 = r * (dn - yh * jnp.mean(dn * yh, axis=-1, keepdims=True))
            dy_ref[:, sl] = dy2 * sz
            dz_ref[:, sl] = (dy2 * yv * sg * (1.0 + z * (1.0 - sg))).astype(BF)
            acc_ref[0:1, sl] += jnp.sum(d * yh, axis=0, keepdims=True)

    row = pl.BlockSpec((tm, D_INNER), lambda i: (i, 0))
    return _call(body, "gate_norm_bwd", (L // tm,), [row, row, row, pl.BlockSpec((1, D_INNER), lambda i: (0, 0))],
                 (row, row, pl.BlockSpec((8, D_INNER), lambda i: (0, 0))),
                 (_sds((L, D_INNER), F32), _sds((L, D_INNER), BF), _sds((8, D_INNER), F32)))(dyn, y, zx, nw)


_SCALE = HEAD_DIM ** -0.5
_REP = ATT_HEADS // KV_HEADS
_V_OFF = KV_HEADS * HEAD_DIM


def _attn_probs(qh, kp, kc, sink, first):
    rows = lax.broadcasted_iota(jnp.int32, (WINDOW, WINDOW), 0)
    cols = lax.broadcasted_iota(jnp.int32, (WINDOW, WINDOW), 1)
    sp = jnp.where(jnp.logical_and(cols > rows, jnp.logical_not(first)), _dot_nt(qh, kp) * _SCALE, NEG)
    sc = jnp.where(cols <= rows, _dot_nt(qh, kc) * _SCALE, NEG)
    m = jnp.maximum(jnp.maximum(jnp.max(sp, axis=1, keepdims=True), jnp.max(sc, axis=1, keepdims=True)), sink)
    pp = jnp.exp(sp - m)
    pc = jnp.exp(sc - m)
    ps = jnp.exp(sink - m)
    inv = 1.0 / (jnp.sum(pp, axis=1, keepdims=True) + jnp.sum(pc, axis=1, keepdims=True) + ps)
    return pp * inv, pc * inv, ps * inv


def attn_fwd(q, kv, sinks_pad):
    L = q.shape[0]
    nb = L // WINDOW

    def body(q_ref, kc_ref, kp_ref, s_ref, o_ref):
        first = pl.program_id(0) == 0
        for h in range(ATT_HEADS):
            k = h // _REP
            ks = slice(HEAD_DIM * k, HEAD_DIM * (k + 1))
            vs = slice(_V_OFF + HEAD_DIM * k, _V_OFF + HEAD_DIM * (k + 1))
            hs = slice(HEAD_DIM * h, HEAD_DIM * (h + 1))
            pp, pc, _ = _attn_probs(q_ref[:, hs], kp_ref[:, ks], kc_ref[:, ks], s_ref[:, h:h + 1], first)
            o_ref[:, hs] = (_dot_nn(pp, kp_ref[:, vs]) + _dot_nn(pc, kc_ref[:, vs])).astype(BF)

    qspec = pl.BlockSpec((WINDOW, D_MODEL), lambda i: (i, 0))
    return _call(body, "attn_fwd", (nb,),
                 [qspec, pl.BlockSpec((WINDOW, KV_DIM), lambda i: (i, 0)),
                  pl.BlockSpec((WINDOW, KV_DIM), lambda i: (jnp.maximum(i - 1, 0), 0)),
                  pl.BlockSpec((1, 128), lambda i: (0, 0))],
                 qspec, _sds((L, D_MODEL), BF))(q, kv, kv, sinks_pad)


def attn_bwd(q, kv, do, sinks_pad):
    L = q.shape[0]
    nb = L // WINDOW

    def body(q_ref, kc_ref, kp_ref, do_ref, s_ref, dq_ref, dc_ref, dp_ref, acc_ref):
        first = pl.program_id(0) == 0

        @pl.when(first)
        def _():
            acc_ref[...] = jnp.zeros_like(acc_ref)

        lane = lax.broadcasted_iota(jnp.int32, (1, 128), 1)
        dsink = jnp.zeros((1, 128), F32)
        for k in range(KV_HEADS):
            ks = slice(HEAD_DIM * k, HEAD_DIM * (k + 1))
            vs = slice(_V_OFF + HEAD_DIM * k, _V_OFF + HEAD_DIM * (k + 1))
            kp, kc, vp, vc = kp_ref[:, ks], kc_ref[:, ks], kp_ref[:, vs], kc_ref[:, vs]
            dkp = jnp.zeros((WINDOW, HEAD_DIM), F32)
            dkc = jnp.zeros((WINDOW, HEAD_DIM), F32)
            dvp = jnp.zeros((WINDOW, HEAD_DIM), F32)
            dvc = jnp.zeros((WINDOW, HEAD_DIM), F32)
            for r in range(_REP):
                h = k * _REP + r
                hs = slice(HEAD_DIM * h, HEAD_DIM * (h + 1))
                qh = q_ref[:, hs]
                doh = do_ref[:, hs]
                pp, pc, ps = _attn_probs(qh, kp, kc, s_ref[:, h:h + 1], first)
                dpp = _dot_nt(doh, vp)
                dpc = _dot_nt(doh, vc)
                delta = jnp.sum(pp * dpp, axis=1, keepdims=True) + jnp.sum(pc * dpc, axis=1, keepdims=True)
                dsp = pp * (dpp - delta) * _SCALE
                dsc = pc * (dpc - delta) * _SCALE
                dq_ref[:, hs] = _dot_nn(dsp, kp) + _dot_nn(dsc, kc)
                dkp = dkp + _dot_tn(dsp, qh)
                dkc = dkc + _dot_tn(dsc, qh)
                dvp = dvp + _dot_tn(pp, doh)
                dvc = dvc + _dot_tn(pc, doh)
                dsink = dsink + jnp.where(lane == h, -jnp.sum(ps * delta, axis=0, keepdims=True), 0.0)
            dp_ref[:, ks] = dkp
            dc_ref[:, ks] = dkc
            dp_ref[:, vs] = dvp
            dc_ref[:, vs] = dvc
        acc_ref[0:1, :] += jnp.sum(dq_ref[...], axis=0, keepdims=True)
        acc_ref[1:2, 0:128] += dsink

    qspec = pl.BlockSpec((WINDOW, D_MODEL), lambda i: (i, 0))
    kspec = pl.BlockSpec((WINDOW, KV_DIM), lambda i: (i, 0))
    return _call(body, "attn_bwd", (nb,),
                 [qspec, kspec, pl.BlockSpec((WINDOW, KV_DIM), lambda i: (jnp.maximum(i - 1, 0), 0)), qspec,
                  pl.BlockSpec((1, 128), lambda i: (0, 0))],
                 (qspec, kspec, kspec, pl.BlockSpec((8, D_MODEL), lambda i: (0, 0))),
                 (_sds((L, D_MODEL), F32), _sds((L, KV_DIM), F32), _sds((L, KV_DIM), F32), _sds((8, D_MODEL), F32)),
                 )(q, kv, kv, do, sinks_pad)


def kv_grad_combine(parts):
    L = parts[0][0].shape[0]
    nb = L // WINDOW
    n = len(parts)

    def body(*refs):
        i = pl.program_id(0)
        o_ref, acc_ref = refs[2 * n], refs[2 * n + 1]

        @pl.when(i == 0)
        def _():
            acc_ref[...] = jnp.zeros_like(acc_ref)

        tot = refs[0][...]
        nxt = refs[1][...]
        for a in range(1, n):
            tot = tot + refs[2 * a][...]
            nxt = nxt + refs[2 * a + 1][...]
        tot = tot + jnp.where(i < nb - 1, nxt, 0.0)
        o_ref[...] = tot
        acc_ref[0:1, :] += jnp.sum(tot, axis=0, keepdims=True)

    cur = pl.BlockSpec((WINDOW, KV_DIM), lambda i: (i, 0))
    nxt = pl.BlockSpec((WINDOW, KV_DIM), lambda i: (jnp.minimum(i + 1, nb - 1), 0))
    args = [t for p in parts for t in p]
    return _call(body, "kv_grad_combine", (nb,), [cur, nxt] * n,
                 (cur, pl.BlockSpec((8, KV_DIM), lambda i: (0, 0))),
                 (_sds((L, KV_DIM), F32), _sds((8, KV_DIM), F32)))(*args)


def mod_fwd(c_all, w, b, name):
    n, _, C = w.shape

    def body(c_ref, w_ref, b_ref, o_ref, ca_ref):
        cv = c_ref[...]
        ca = cv * _sigmoid(cv)
        ca_ref[...] = ca
        o_ref[...] = _dot(ca, w_ref[...], ((1,), (0,))) + b_ref[...]

    return _call(body, name, (n,),
                 [pl.BlockSpec((N_DEV, D_MODEL), lambda i: (0, 0)),
                  pl.BlockSpec((None, D_MODEL, C), lambda i: (i, 0, 0)),
                  pl.BlockSpec((None, 1, C), lambda i: (i, 0, 0))],
                 (pl.BlockSpec((None, N_DEV, C), lambda i: (i, 0, 0)), pl.BlockSpec((N_DEV, D_MODEL), lambda i: (0, 0))),
                 (_sds((n, N_DEV, C), F32), _sds((N_DEV, D_MODEL), F32)))(c_all, w, b)


def mod_wgrad(c_act_t, dmod, name):
    n, _, C = dmod.shape
    tr = 256

    def body(ct_ref, d_ref, o_ref):
        acc = ct_ref[:, 0:1] * d_ref[0:1, :]
        for bidx in range(1, N_DEV):
            acc = acc + ct_ref[:, bidx:bidx + 1] * d_ref[bidx:bidx + 1, :]
        o_ref[...] = acc

    return _call(body, name, (n, D_MODEL // tr),
                 [pl.BlockSpec((tr, N_DEV), lambda i, j: (j, 0)),
                  pl.BlockSpec((None, N_DEV, C), lambda i, j: (i, 0, 0))],
                 pl.BlockSpec((None, tr, C), lambda i, j: (i, j, 0)), _sds((n, D_MODEL, C), F32))(c_act_t, dmod)


def _my_pos():
    return lax.axis_index("x"), lax.axis_index("y"), lax.axis_index("c")


def small_all_gather(v):
    m_per, n = v.shape

    def body(x_ref, out_ref, send_sems, recv_sems, local_sem):
        x, y, c = _my_pos()
        me, sibling = (x, y, c), (x, y, 1 - c)
        chips = [(1 - x, y), (x, 1 - y), (1 - x, 1 - y)]

        def rows(px, py, pc):
            return out_ref.at[pl.ds((4 * px + 2 * py + pc) * m_per, m_per), :]

        def copy(k, block, to, src=None):
            return pltpu.make_async_remote_copy(
                src_ref=rows(*block) if src is None else src, dst_ref=rows(*block),
                send_sem=send_sems.at[k], recv_sem=recv_sems.at[k], device_id=to, device_id_type=MESH)

        mine = pltpu.make_async_copy(x_ref, rows(*me), local_sem)
        mine.start()
        first = [copy(0, me, sibling, src=x_ref)]
        first += [copy(1 + j, me, (*chip, c), src=x_ref) for j, chip in enumerate(chips)]
        for cp in first:
            cp.start()
        passed = [copy(4 + j, (*chip, c), sibling) for j, chip in enumerate(chips)]
        for j, chip in enumerate(chips):
            copy(1 + j, (*chip, c), me).wait_recv()
            passed[j].start()
        copy(0, sibling, me).wait_recv()
        for j, chip in enumerate(chips):
            copy(4 + j, (*chip, 1 - c), me).wait_recv()
        for cp in first + passed:
            cp.wait_send()
        mine.wait()

    return pl.pallas_call(
        body, name="small_all_gather",
        out_shape=_sds((N_DEV * m_per, n), v.dtype),
        in_specs=[pl.BlockSpec(memory_space=pltpu.VMEM)],
        out_specs=pl.BlockSpec(memory_space=pltpu.VMEM),
        scratch_shapes=[pltpu.SemaphoreType.DMA((7,)), pltpu.SemaphoreType.DMA((7,)), pltpu.SemaphoreType.DMA],
        compiler_params=pltpu.CompilerParams(vmem_limit_bytes=VMEM_LIMIT),
    )(v)


def big_all_gather(arrs):
    n = len(arrs)

    def body(*refs):
        ins, outs = refs[:n], refs[n:2 * n]
        send_sems, recv_sems, local_sems = refs[2 * n], refs[2 * n + 1], refs[2 * n + 2]
        x, y, c = _my_pos()
        me, sibling = (x, y, c), (x, y, 1 - c)
        chips = [(1 - x, y), (x, 1 - y), (1 - x, 1 - y)]

        def slot(a, px, py, pc):
            return outs[a].at[4 * px + 2 * py + pc]

        def copy(a, k, block, to, src=None):
            return pltpu.make_async_remote_copy(
                src_ref=slot(a, *block) if src is None else src, dst_ref=slot(a, *block),
                send_sem=send_sems.at[7 * a + k], recv_sem=recv_sems.at[7 * a + k], device_id=to, device_id_type=MESH)

        mine = [pltpu.make_async_copy(ins[a], slot(a, *me), local_sems.at[a]) for a in range(n)]
        for cp in mine:
            cp.start()
        first = []
        for a in range(n):
            first.append(copy(a, 0, me, sibling, src=ins[a]))
            first += [copy(a, 1 + j, me, (*chip, c), src=ins[a]) for j, chip in enumerate(chips)]
        for cp in first:
            cp.start()
        passed = []
        for a in range(n):
            for j, chip in enumerate(chips):
                copy(a, 1 + j, (*chip, c), me).wait_recv()
                fwd = copy(a, 4 + j, (*chip, c), sibling)
                fwd.start()
                passed.append(fwd)
        for a in range(n):
            copy(a, 0, sibling, me).wait_recv()
            for j, chip in enumerate(chips):
                copy(a, 4 + j, (*chip, 1 - c), me).wait_recv()
        for cp in first + passed:
            cp.wait_send()
        for cp in mine:
            cp.wait()

    hbm = pl.BlockSpec(memory_space=pltpu.HBM)
    return pl.pallas_call(
        body, name="big_all_gather",
        out_shape=[_sds((N_DEV,) + a.shape, a.dtype) for a in arrs],
        in_specs=[hbm] * n, out_specs=[hbm] * n,
        scratch_shapes=[pltpu.SemaphoreType.DMA((7 * n,)), pltpu.SemaphoreType.DMA((7 * n,)),
                        pltpu.SemaphoreType.DMA((n,))],
    )(*arrs)


_FLIPS =[(fx, fy, fc) for fx in (0, 1) for fy in (0, 1) for fc in (0, 1)][1:]
_HBM = pl.BlockSpec(memory_space=pltpu.HBM)
_SEM = pl.BlockSpec(memory_space=pltpu.SEMAPHORE)
_EFFECT = pltpu.SideEffectType.DATAFLOW_SIDE_EFFECTING


def _flip(x, y, c, f):
    return (1 - x if f[0] else x), (1 - y if f[1] else y), (1 - c if f[2] else c)


def _xfer_copies(srcs, lands, send_sems, recv_sems, scatter):
    x, y, c = _my_pos()
    me = 4 * x + 2 * y + c
    copies = []
    for a in range(len(srcs)):
        for k, f in enumerate(_FLIPS):
            px, py, pc = _flip(x, y, c, f)
            src = srcs[a].at[4 * px + 2 * py + pc] if scatter else srcs[a]
            copies.append(pltpu.make_async_remote_copy(
                src_ref=src, dst_ref=lands[a].at[me], send_sem=send_sems.at[7 * a + k],
                recv_sem=recv_sems.at[7 * a + k], device_id=(px, py, pc), device_id_type=MESH))
    return copies


def _own_copies(srcs, lands, local_sems, scatter):
    x, y, c = _my_pos()
    me = 4 * x + 2 * y + c
    return [pltpu.make_async_copy(srcs[a].at[me] if scatter else srcs[a], lands[a].at[me], local_sems.at[a])
            for a in range(len(srcs))]


def xfer_start(arrs, scatter, after, name):
    n = len(arrs)
    land_shapes = [a.shape if scatter else (N_DEV,) + a.shape for a in arrs]

    def body(*refs):
        srcs, lands = refs[:n], refs[n:2 * n]
        send_sems, recv_sems, local_sems = refs[2 * n + 1], refs[2 * n + 2], refs[2 * n + 3]
        token = refs[-1]
        for cp in _xfer_copies(srcs, lands, send_sems, recv_sems, scatter):
            cp.start()
        for cp in _own_copies(srcs, lands, local_sems, scatter):
            cp.start()
        token[...] = jnp.zeros_like(token)

    out = pl.pallas_call(
        body, name=name,
        out_shape=(pltpu.SemaphoreType.DMA((7 * n,)), pltpu.SemaphoreType.DMA((7 * n,)),
                   pltpu.SemaphoreType.DMA((n,)),
                   *[pltpu.HBM(a.shape, a.dtype) for a in arrs],
                   *[pltpu.HBM(s, a.dtype) for s, a in zip(land_shapes, arrs)],
                   _sds((8, 128), F32)),
        in_specs=[_HBM] * (2 * n) + [pl.BlockSpec(memory_space=pl.ANY)],
        out_specs=(_SEM, _SEM, _SEM, *([_HBM] * (2 * n)), pl.BlockSpec(memory_space=pltpu.VMEM)),
        input_output_aliases={i: 3 + i for i in range(2 * n)},
        compiler_params=pltpu.CompilerParams(has_side_effects=_EFFECT),
    )(*[pltpu.with_memory_space_constraint(a, pltpu.HBM) for a in arrs],
      *[pltpu.with_memory_space_constraint(lax.empty(s, a.dtype), pltpu.HBM) for s, a in zip(land_shapes, arrs)],
      after)
    return (out[0], out[1], out[2], list(out[3:3 + n]), list(out[3 + n:3 + 2 * n]), scatter), out[-1]


def xfer_wait(handle, after, name):
    send_sems, recv_sems, local_sems, srcs, lands, scatter = handle
    n = len(srcs)

    def body(*refs):
        srcs_r, lands_r = refs[:n], refs[n:2 * n]
        ssem, rsem, lsem = refs[2 * n], refs[2 * n + 1], refs[2 * n + 2]
        for cp in _xfer_copies(srcs_r, lands_r, ssem, rsem, scatter):
            cp.wait_send()
            cp.wait_recv()
        for cp in _own_copies(srcs_r, lands_r, lsem, scatter):
            cp.wait()

    out = pl.pallas_call(
        body, name=name,
        out_shape=(*[pltpu.HBM(a.shape, a.dtype) for a in srcs], *[pltpu.HBM(a.shape, a.dtype) for a in lands]),
        in_specs=[_HBM] * (2 * n) + [_SEM, _SEM, _SEM, pl.BlockSpec(memory_space=pl.ANY)],
        out_specs=tuple([_HBM] * (2 * n)),
        input_output_aliases={i: i for i in range(2 * n)},
        compiler_params=pltpu.CompilerParams(has_side_effects=_EFFECT),
    )(*srcs, *lands, send_sems, recv_sems, local_sems, after)
    return list(out[n:])


def adamw(parts, w, m, v, name, row0=0, prev=None):
    r_tot, C = w.shape
    n_parts, R = parts.shape[0], parts.shape[1]
    row_bytes = 2 * (n_parts * C * parts.dtype.itemsize + 7 * C * 4)
    tr = R
    for cand in (512, 352, 256, 176, 128, 64):
        if R % cand == 0 and row0 % cand == 0 and R > cand and cand * row_bytes <= ADAMW_VMEM_BUDGET:
            tr = cand
            break
    tc = C
    if tr == R and R * row_bytes > ADAMW_VMEM_BUDGET:
        assert row0 == 0 and R == r_tot
        tc = next(t for t in (512, 256, 128) if C % t == 0 and R * row_bytes * t // C <= ADAMW_VMEM_BUDGET)
    assert row0 % tr == 0 and (tr % 8 == 0 or (tr == r_tot and row0 == 0))
    blk0 = row0 // tr
    c1 = 1.0 / (1.0 - ADAM_B1 ** ADAM_STEP)
    c2 = 1.0 / (1.0 - ADAM_B2 ** ADAM_STEP)

    def body(p_ref, w_ref, m_ref, v_ref, *rest):
        g_ref, d_ref, nm_ref, nv_ref = rest[-4:]
        g = p_ref[0].astype(F32)
        for k in range(1, n_parts):
            g = g + p_ref[k].astype(F32)
        nm = ADAM_B1 * m_ref[...] + (1.0 - ADAM_B1) * g
        nv = ADAM_B2 * v_ref[...] + (1.0 - ADAM_B2) * (g * g)
        g_ref[...] = g
        nm_ref[...] = nm
        nv_ref[...] = nv
        d_ref[...] = -ADAM_LR * ((nm * c1) / (jnp.sqrt(nv * c2) + ADAM_EPS) + ADAM_WD * w_ref[...])

    if tc == C:
        grid = (R // tr,)
        blk = pl.BlockSpec((tr, C), lambda i: (i + blk0, 0))
        p_spec = pl.BlockSpec((n_parts, tr, C), lambda i: (0, i, 0))
    else:
        grid = (C // tc,)
        blk = pl.BlockSpec((R, tc), lambda i: (0, i))
        p_spec = pl.BlockSpec((n_parts, R, tc), lambda i: (0, 0, i))
    in_specs = [p_spec, blk, blk, blk]
    args = [parts, w, m, v]
    aliases = {}
    if prev is not None:
        in_specs += [pl.BlockSpec(memory_space=pl.ANY)] * 4
        args += list(prev)
        aliases = {4 + k: k for k in range(4)}
    return pl.pallas_call(
        body, name=name, grid=grid, in_specs=in_specs, out_specs=(blk, blk, blk, blk),
        out_shape=tuple(_sds((r_tot, C), F32) for _ in range(4)), input_output_aliases=aliases,
        compiler_params=pltpu.CompilerParams(vmem_limit_bytes=VMEM_LIMIT))(*args)


def _ffn_fwd(x, nw, sh, sc, g, wt_gu, w_dn):
    h = norm_mod(x, nw, sh, sc, "ffn_norm")
    gp, up, a = ffn_up(h, wt_gu)
    y, xn = matmul(a, w_dn, "nn", BF, "ffn_down", res=x, gate=g, coef=0.5)
    return xn, (x, h, gp, up, a, y)


def _ffn_bwd(dxo, saved, nw, sc, g, wt_gu, w_dn):
    x, h, gp, up, a, y = saved
    dy, acc1 = resid_gate_bwd(dxo, y, g, 0.5, "ffn_gate_bwd")
    d_wdn = matmul(a, dy, "tn", BF, "ffn_down_wgrad")
    dg, du = ffn_down_dgrad(dy, w_dn, gp, up)
    d_wt = ffn_up_wgrad(dg, du, h)
    dh = ffn_up_dgrad(dg, du, wt_gu)
    dx, acc2 = norm_mod_bwd(x, dh, dxo, nw, sc, "ffn_norm_bwd")
    return dx, d_wt, d_wdn, (acc2[0], acc2[1], acc1[0]), acc2[2]


def _group_layout(a):
    L = a.shape[0]
    return a[:, :SSM_HEADS].reshape(L, SSM_GROUPS, _HPG).transpose(1, 0, 2)


def _ungroup_layout(a):
    L = a.shape[1]
    return jnp.pad(a.transpose(1, 0, 2).reshape(L, SSM_HEADS), ((0, 0), (0, 128 - SSM_HEADS)))


def _pad_row(vec, n=128):
    return jnp.pad(vec.reshape(1, -1), ((0, 0), (0, n - vec.shape[-1])))


def _mamba_fwd(x, nw, sh, sc, g, p):
    h = norm_mod(x, nw, sh, sc, "mix_norm")
    zx = matmul(h, p["w_in_t"], "nt", F32, "ssm_in")
    xc = conv_fwd(zx, p["conv_w"], p["conv_b"])
    dt, acs = dt_prep(zx, p["dt_bias"], p["a_log"])
    dt_g, acs_g = _group_layout(dt), _group_layout(acs)
    acs_t = acs_g.transpose(0, 2, 1)
    y, pst = ssd_fwd(xc, dt_g, acs_g, acs_t, p["d_exp"])
    yn = gate_norm(y, zx, p["norm_w"])
    yo, xn = matmul(yn, p["w_out"], "nn", BF, "ssm_out", res=x, gate=g, coef=1.0)
    return xn, (x, h, zx, xc, dt, dt_g, acs_g, acs_t, y, pst, yn, yo)


def _mamba_bwd(dxo, saved, nw, sc, g, p):
    x, h, zx, xc, dt, dt_g, acs_g, acs_t, y, pst, yn, yo = saved
    dyo, acc1 = resid_gate_bwd(dxo, yo, g, 1.0, "mix_gate_bwd")
    d_wout = matmul(yn, dyo, "tn", BF, "ssm_out_wgrad")
    dyn = matmul(dyo, p["w_out"], "nt", F32, "ssm_out_dgrad")
    dy, dz, accn = gate_norm_bwd(dyn, y, zx, p["norm_w"])
    dxs, dB, dC, ddt_g, da_g, dd = ssd_bwd(dy, xc, dt_g, acs_g, acs_t, pst, p["d_exp"])
    dxc = jnp.concatenate([dxs, dB, dC], axis=1)
    ds = conv_bwd_act(dxc, zx, p["conv_w"], p["conv_b"])
    du, accc = conv_bwd(ds, zx, p["conv_w"])
    draw, accdt = dt_bwd(_ungroup_layout(ddt_g), _ungroup_layout(da_g), dt, zx, p["dt_bias"], p["a_log"])
    dzx = jnp.concatenate([dz, du, draw], axis=1)
    d_win = matmul(dzx, h, "tn", BF, "ssm_in_wgrad")[:IN_PROJ]
    dh = matmul(dzx, p["w_in_t"], "nn", F32, "ssm_in_dgrad")
    dx, acc2 = norm_mod_bwd(x, dh, dxo, nw, sc, "mix_norm_bwd")
    small = dict(conv_w=accc[:CONV_WIDTH], conv_b=accc[CONV_WIDTH], dt_bias=accdt[0, :SSM_HEADS],
                 a_log=accdt[1, :SSM_HEADS], d=dd.reshape(SSM_HEADS, SSM_HEADDIM).sum(-1), norm_w=accn[0])
    return dx, d_win, d_wout, (acc2[0], acc2[1], acc1[0]), acc2[2], small


def _attn_layer_fwd(x, nw, sh, sc, g, p, kv):
    h = norm_mod(x, nw, sh, sc, "mix_norm")
    q = matmul(h, p["w_q"], "nn", F32, "attn_q", bias=p["b_q"])
    o = attn_fwd(q, kv, p["sinks"])
    yo, xn = matmul(o, p["w_o"], "nn", BF, "attn_o", bias=p["b_o"], res=x, gate=g, coef=1.0)
    return xn, (x, h, q, o, yo)


def _attn_layer_bwd(dxo, saved, nw, sc, g, p, kv):
    x, h, q, o, yo = saved
    dyo, acc1 = resid_gate_bwd(dxo, yo, g, 1.0, "mix_gate_bwd")
    d_wo = matmul(o, dyo, "tn", BF, "attn_o_wgrad")
    do = matmul(dyo, p["w_o"], "nt", F32, "attn_o_dgrad")
    dq, dkv_c, dkv_p, acca = attn_bwd(q, kv, do, p["sinks"])
    d_wq = matmul(h, dq, "tn", BF, "attn_q_wgrad")
    dh = matmul(dq, p["w_q"], "nt", F32, "attn_q_dgrad")
    dx, acc2 = norm_mod_bwd(x, dh, dxo, nw, sc, "mix_norm_bwd")
    small = dict(b_q=acca[0], sinks=acca[1, :ATT_HEADS], b_o=acc1[1])
    return dx, d_wq, d_wo, (acc2[0], acc2[1], acc1[0]), acc2[2], small, (dkv_c, dkv_p)


def _pack_rows(pieces):
    rows, spans, off = [], [], 0
    for a in pieces:
        flat = a.reshape(-1).astype(F32)
        n = -(-flat.shape[0] // D_MODEL)
        rows.append(jnp.pad(flat, (0, n * D_MODEL - flat.shape[0])).reshape(n, D_MODEL))
        spans.append((off, a.shape))
        off += n
    pad = -off % 8
    if pad:
        rows.append(jnp.zeros((pad, D_MODEL), F32))
    return jnp.concatenate(rows, axis=0), spans, off + pad


def _unpack_rows(g, spans):
    out = []
    for off, shape in spans:
        size = 1
        for s in shape:
            size *= s
        n = -(-size // D_MODEL)
        out.append(g[:, off:off + n].reshape(N_DEV, n * D_MODEL)[:, :size].reshape((N_DEV,) + tuple(shape)))
    return out


def _unshard_last(g):
    nd = g.ndim
    perm = tuple(range(1, nd - 1)) + (0, nd - 1)
    t = g.transpose(perm)
    return t.reshape(t.shape[:-2] + (N_DEV * g.shape[-1],))


def _shard_last(a, me):
    s = a.shape[-1] // N_DEV
    return lax.dynamic_slice_in_dim(a, me * s, s, axis=a.ndim - 1)


def kernel(x, c, ffn_norm_w, ffn_w_gu, ffn_w_down, mod_w, mod_b, mix_norm_w, ssm_w_in, ssm_conv_w, ssm_conv_b, ssm_dt_bias, ssm_a_log, ssm_d, ssm_norm_w, ssm_w_out, kv_norm_w, kv_mod_w, kv_mod_b, w_kv, b_kv, attn_w_q, attn_b_q, attn_sinks, attn_w_o, attn_b_o, final_norm_w, loss_target, m_ffn_norm_w, m_ffn_w_gu, m_ffn_w_down, m_mod_w, m_mod_b, m_mix_norm_w, m_ssm_w_in, m_ssm_conv_w, m_ssm_conv_b, m_ssm_dt_bias, m_ssm_a_log, m_ssm_d, m_ssm_norm_w, m_ssm_w_out, m_kv_norm_w, m_kv_mod_w, m_kv_mod_b, m_w_kv, m_b_kv, m_attn_w_q, m_attn_b_q, m_attn_sinks, m_attn_w_o, m_attn_b_o, m_final_norm_w, v_ffn_norm_w, v_ffn_w_gu, v_ffn_w_down, v_mod_w, v_mod_b, v_mix_norm_w, v_ssm_w_in, v_ssm_conv_w, v_ssm_conv_b, v_ssm_dt_bias, v_ssm_a_log, v_ssm_d, v_ssm_norm_w, v_ssm_w_out, v_kv_norm_w, v_kv_mod_w, v_kv_mod_b, v_w_kv, v_b_kv, v_attn_w_q, v_attn_b_q, v_attn_sinks, v_attn_w_o, v_attn_b_o, v_final_norm_w):
    D = D_MODEL
    me = 4 * lax.axis_index("x") + 2 * lax.axis_index("y") + lax.axis_index("c")
    xs = x[0]
    target = loss_target[0]
    mod_cols = mod_w.shape[-1]
    kvm_cols = kv_mod_w.shape[-1]

    packed, spans, _ = _pack_rows([c, ffn_norm_w, ssm_conv_w, ssm_conv_b, ssm_norm_w])
    nrow = packed.shape[0]
    g1 = small_all_gather(packed).reshape(N_DEV, nrow, D)
    c_all, fnw_g, cw_g, cb_g, snw_g = _unpack_rows(g1, spans)
    c_all = c_all.reshape(N_DEV, D)
    ffn_nw = _unshard_last(fnw_g)
    conv_w = _unshard_last(cw_g)
    conv_b = _unshard_last(cb_g)
    ssm_nw = _unshard_last(snw_g)

    mod_b_loc = lax.dynamic_slice_in_dim(mod_b, me * mod_cols, mod_cols, axis=1).reshape(DEPTH, 1, mod_cols)
    kvb_loc = lax.dynamic_slice_in_dim(kv_mod_b, me * kvm_cols, kvm_cols, axis=0).reshape(1, 1, kvm_cols)
    modp, c_act = mod_fwd(c_all, mod_w, mod_b_loc, "mod_fwd")
    kvmp, _ = mod_fwd(c_all, kv_mod_w.reshape(1, D, kvm_cols), kvb_loc, "kv_mod_fwd")
    packed2, spans2, _ = _pack_rows([modp, kvmp])
    nrow2 = packed2.shape[0]
    g2 = small_all_gather(packed2).reshape(N_DEV, nrow2, D)
    modp_g, kvmp_g = _unpack_rows(g2, spans2)
    mod_all = modp_g.transpose(1, 2, 0, 3).reshape(DEPTH, N_DEV, N_MOD * D)
    kvm_all = kvmp_g.transpose(1, 2, 0, 3).reshape(N_DEV, 2 * D)
    mod_me = lax.dynamic_index_in_dim(mod_all, me, axis=1, keepdims=False).reshape(DEPTH, N_MOD, 1, D)
    kvm_me = lax.dynamic_index_in_dim(kvm_all, me, axis=0, keepdims=False).reshape(2, 1, D)

    gu_t = jnp.swapaxes(ffn_w_gu, 2, 3)
    win_t = jnp.transpose(ssm_w_in, (2, 0, 1))
    S = gu_t.shape[2]
    s_in = win_t.shape[0]
    r_dn = ffn_w_down.shape[2]
    r_mix = ssm_w_out.shape[1]
    r_at = attn_w_q.shape[1]

    def layer_pack(k):
        if k < N_A:
            mix = [ssm_w_out[k].astype(BF)]
        else:
            mix = [attn_w_q[k - N_A].astype(BF), attn_w_o[k - N_A].astype(BF)]
        arrs = [gu_t[k, 0].astype(BF), gu_t[k, 1].astype(BF),
                jnp.concatenate([ffn_w_down[k].astype(BF).reshape(-1, D)] + mix, axis=0)]
        if k < N_A:
            arrs.append(win_t[:, k].astype(BF))
        if k == N_A:
            arrs.append(w_kv.astype(BF))
        return arrs

    packs = [layer_pack(k) for k in range(DEPTH)]
    gathered = [None] * DEPTH
    gathered[0] = big_all_gather(packs[0])
    pending, tok_next = xfer_start(packs[1], False, gathered[0][2], "gather_start_1")

    def wt_gu_full(i, j):
        return gathered[i][j].reshape(N_DEV * S, D)

    def w_dn_full(i, j):
        return gathered[i][2][:, j * r_dn:(j + 1) * r_dn].reshape(D_FF, D)

    def mix_rows(i, lo, r):
        return gathered[i][2][:, 2 * r_dn + lo:2 * r_dn + lo + r].reshape(N_DEV * r, D)

    def mamba_params(j):
        w_in_t = jnp.pad(gathered[j][3].reshape(IN_PROJ, D), ((0, IN_PROJ_PAD - IN_PROJ), (0, 0)))
        return dict(w_in_t=w_in_t, w_out=mix_rows(j, 0, r_mix), conv_w=conv_w[j], conv_b=conv_b[j].reshape(1, -1),
                    dt_bias=_pad_row(ssm_dt_bias[j]), a_log=_pad_row(ssm_a_log[j]),
                    d_exp=jnp.repeat(ssm_d[j], SSM_HEADDIM).reshape(SSM_GROUPS, 1, _GW),
                    norm_w=ssm_nw[j].reshape(1, -1))

    def attn_params(j):
        return dict(w_q=mix_rows(N_A + j, 0, r_at), w_o=mix_rows(N_A + j, r_at, r_at),
                    b_q=attn_b_q[j].reshape(1, -1), b_o=attn_b_o[j].reshape(1, -1), sinks=_pad_row(attn_sinks[j]))

    saved = []
    kv = None
    kv_saved = None
    w_kv_full = None
    xcur = xs
    for i in range(DEPTH):
        if i >= 1:
            gathered[i] = xfer_wait(pending, xcur, "gather_wait_%d" % i)
            if i + 1 < DEPTH:
                pending, tok_next = xfer_start(packs[i + 1], False, xcur, "gather_start_%d" % (i + 1))
        md = mod_me[i]
        if i + 1 < DEPTH:
            md = md + tok_next[0, 0]
        if i == N_A:
            w_kv_full = gathered[N_A][3].reshape(D, KV_DIM)
            h_kv = norm_mod(xcur, kv_norm_w.reshape(1, D), kvm_me[0], kvm_me[1], "kv_norm")
            kv = matmul(h_kv, w_kv_full, "nn", F32, "kv_proj", bias=b_kv.reshape(1, -1))
            kv_saved = (xcur, h_kv)
        x1, s1 = _ffn_fwd(xcur, ffn_nw[i, 0].reshape(1, D), md[0], md[1], md[2], wt_gu_full(i, 0), w_dn_full(i, 0))
        if i < N_A:
            pm = mamba_params(i)
            x2, s2 = _mamba_fwd(x1, mix_norm_w[i].reshape(1, D), md[3], md[4], md[5], pm)
        else:
            pm = attn_params(i - N_A)
            x2, s2 = _attn_layer_fwd(x1, mix_norm_w[i].reshape(1, D), md[3], md[4], md[5], pm, kv)
        x3, s3 = _ffn_fwd(x2, ffn_nw[i, 1].reshape(1, D), md[6], md[7], md[8], wt_gu_full(i, 1), w_dn_full(i, 1))
        saved.append((s1, s2, s3, pm))
        xcur = x3

    dx, accf = final_loss(xcur, final_norm_w.reshape(1, D), target)
    d_mod = [None] * DEPTH
    d_ffn_nw = [[None, None] for _ in range(DEPTH)]
    d_mix_nw = [None] * DEPTH
    sm_m, sm_a = [None] * N_A, [None] * N_A
    kv_parts = [None] * N_A
    d_kvm = d_kv_nw = d_bkv = None
    exchanges = []
    tok = None

    def send(arrs, tag):
        handle, t = xfer_start(arrs, True, dx, "exch_start_%s" % tag)
        exchanges.append((handle, tag))
        return t

    def ffn_slabs(d_wt, d_wdn):
        return jnp.concatenate([d_wt.reshape(N_DEV, S, D), d_wdn.reshape(N_DEV, r_dn, D)], axis=1)

    for i in reversed(range(DEPTH)):
        md = mod_me[i]
        s1, s2, s3, pm = saved[i]
        g2 = md[8] if tok is None else md[8] + tok[0, 0]
        dx, d_wt, d_wdn, m2, d_ffn_nw[i][1] = _ffn_bwd(
            dx, s3, ffn_nw[i, 1].reshape(1, D), md[7], g2, wt_gu_full(i, 1), w_dn_full(i, 1))
        tok = send([ffn_slabs(d_wt, d_wdn)], "f%d1" % i)
        gm = md[5] + tok[0, 0]
        if i < N_A:
            dx, d_in, d_out, mm_, d_mix_nw[i], sm_m[i] = _mamba_bwd(dx, s2, mix_norm_w[i].reshape(1, D), md[4], gm, pm)
            tok = send([d_in.reshape(N_DEV, s_in, D), d_out.reshape(N_DEV, r_mix, D)], "m%d" % i)
        else:
            j = i - N_A
            dx, d_q, d_o, mm_, d_mix_nw[i], sm_a[j], kv_parts[j] = _attn_layer_bwd(
                dx, s2, mix_norm_w[i].reshape(1, D), md[4], gm, pm, kv)
            tok = send([jnp.concatenate([d_q.reshape(N_DEV, r_at, D), d_o.reshape(N_DEV, r_at, D)], axis=1)], "m%d" % i)
        g1 = md[2] + tok[0, 0]
        dx, d_wt, d_wdn, m1, d_ffn_nw[i][0] = _ffn_bwd(
            dx, s1, ffn_nw[i, 0].reshape(1, D), md[1], g1, wt_gu_full(i, 0), w_dn_full(i, 0))
        d_mod[i] = jnp.concatenate(list(m1) + list(mm_) + list(m2), axis=0)
        last = [ffn_slabs(d_wt, d_wdn)]
        if i == N_A:
            x_kv, h_kv = kv_saved
            dkv, acck = kv_grad_combine(kv_parts)
            d_bkv = acck[0]
            d_kv_w = matmul(h_kv, dkv, "tn", BF, "kv_wgrad")
            dh_kv = matmul(dkv, w_kv_full, "nt", F32, "kv_dgrad")
            dx, acc_kv = norm_mod_bwd(x_kv, dh_kv, dx, kv_norm_w.reshape(1, D), kvm_me[1], "kv_norm_bwd")
            d_kvm = jnp.concatenate([acc_kv[0], acc_kv[1]], axis=0)
            d_kv_nw = acc_kv[2]
            last.append(d_kv_w.reshape(N_DEV, -1, KV_DIM))
        tok = send(last, "f%d0" % i)
    grad_x = dx.reshape(x.shape)

    small_list = [
        jnp.stack(d_mod, 0), d_kvm,
        jnp.stack([jnp.stack(r, 0) for r in d_ffn_nw], 0),
        jnp.stack(d_mix_nw, 0),
        jnp.stack([s["conv_w"] for s in sm_m], 0), jnp.stack([s["conv_b"] for s in sm_m], 0),
        jnp.stack([s["dt_bias"] for s in sm_m], 0), jnp.stack([s["a_log"] for s in sm_m], 0),
        jnp.stack([s["d"] for s in sm_m], 0), jnp.stack([s["norm_w"] for s in sm_m], 0),
        d_kv_nw, d_bkv,
        jnp.stack([s["b_q"] for s in sm_a], 0), jnp.stack([s["sinks"] for s in sm_a], 0),
        jnp.stack([s["b_o"] for s in sm_a], 0), accf[0], accf[1],
    ]
    packed3, spans3, _ = _pack_rows(small_list)
    nrow3 = packed3.shape[0]
    g3 = small_all_gather(packed3).reshape(N_DEV, nrow3, D)
    (p_mod, p_kvm, p_fnw, p_mnw, p_cw, p_cb, p_dtb, p_al, p_d, p_snw, p_kvnw, p_bkv, p_bq, p_sk, p_bo, p_fin,
     p_loss) = _unpack_rows(g3, spans3)

    loss = 0.5 / D * jnp.sum(p_loss)

    c_act_t = c_act.T
    dmod_loc = _shard_last(p_mod, me).transpose(1, 0, 2)
    dkvm_loc = _shard_last(p_kvm, me).reshape(1, N_DEV, kvm_cols)
    gp_mod_w = mod_wgrad(c_act_t, dmod_loc, "mod_wgrad")
    gp_kvm_w = mod_wgrad(c_act_t, dkvm_loc, "kv_mod_wgrad")[0]

    def as_parts_single(a):
        return a[None]

    def upd(name, parts, w, m, v):
        shp = w.shape
        c_last = shp[-1]
        out = adamw(parts.reshape(parts.shape[0], -1, c_last), w.reshape(-1, c_last), m.reshape(-1, c_last),
                    v.reshape(-1, c_last), "adamw_" + name)
        return tuple(o.reshape(shp) for o in out)

    views = {
        "ffn_w_gu": [jnp.swapaxes(t, 2, 3).reshape(-1, D) for t in (ffn_w_gu, m_ffn_w_gu, v_ffn_w_gu)],
        "ffn_w_down": [t.reshape(-1, D) for t in (ffn_w_down, m_ffn_w_down, v_ffn_w_down)],
        "ssm_w_out": [t.reshape(-1, D) for t in (ssm_w_out, m_ssm_w_out, v_ssm_w_out)],
        "attn_w_q": [t.reshape(-1, D) for t in (attn_w_q, m_attn_w_q, v_attn_w_q)],
        "attn_w_o": [t.reshape(-1, D) for t in (attn_w_o, m_attn_w_o, v_attn_w_o)],
    }
    filled = {k: None for k in views}

    def upd_rows(name, parts, row0):
        w, m, v = views[name]
        filled[name] = adamw(parts, w, m, v, "adamw_" + name, row0=row0, prev=filled[name])
        return filled[name][3]

    chain = dx
    r_in_parts = [None] * N_A
    r_kv = None
    for handle, tag in exchanges:
        got = xfer_wait(handle, chain, "exch_wait_%s" % tag)
        i = int(tag[1])
        if tag[0] == "f":
            jf = int(tag[2])
            upd_rows("ffn_w_gu", got[0][:, :S], (2 * i + jf) * S)
            chain = upd_rows("ffn_w_down", got[0][:, S:], (2 * i + jf) * r_dn)
            if len(got) > 1:
                r_kv = got[1]
        elif i < N_A:
            r_in_parts[i] = got[0]
            chain = upd_rows("ssm_w_out", got[1], i * r_mix)
        else:
            upd_rows("attn_w_q", got[0][:, :r_at], (i - N_A) * r_at)
            chain = upd_rows("attn_w_o", got[0][:, r_at:], (i - N_A) * r_at)

    res = {}
    res["ffn_w_gu"] = tuple(jnp.swapaxes(t.reshape(gu_t.shape), 2, 3) for t in filled["ffn_w_gu"])
    res["ffn_w_down"] = tuple(t.reshape(ffn_w_down.shape) for t in filled["ffn_w_down"])
    res["ssm_w_out"] = tuple(t.reshape(ssm_w_out.shape) for t in filled["ssm_w_out"])
    res["attn_w_q"] = tuple(t.reshape(attn_w_q.shape) for t in filled["attn_w_q"])
    res["attn_w_o"] = tuple(t.reshape(attn_w_o.shape) for t in filled["attn_w_o"])
    win_out = adamw(jnp.stack(r_in_parts, axis=2).reshape(N_DEV, s_in * N_A, D),
                    *[jnp.transpose(t, (2, 0, 1)).reshape(-1, D) for t in (ssm_w_in, m_ssm_w_in, v_ssm_w_in)],
                    "adamw_ssm_w_in")
    res["ssm_w_in"] = tuple(jnp.transpose(t.reshape(win_t.shape), (1, 2, 0)) for t in win_out)
    res["w_kv"] = upd("w_kv", r_kv, w_kv, m_w_kv, v_w_kv)

    res["ffn_norm_w"] = upd("ffn_norm_w", _shard_last(p_fnw, me), ffn_norm_w, m_ffn_norm_w, v_ffn_norm_w)
    res["mod_w"] = upd("mod_w", as_parts_single(gp_mod_w), mod_w, m_mod_w, v_mod_w)
    res["mod_b"] = upd("mod_b", p_mod, mod_b, m_mod_b, v_mod_b)
    res["mix_norm_w"] = upd("mix_norm_w", p_mnw, mix_norm_w, m_mix_norm_w, v_mix_norm_w)
    res["ssm_conv_w"] = upd("ssm_conv_w", _shard_last(p_cw, me), ssm_conv_w, m_ssm_conv_w, v_ssm_conv_w)
    res["ssm_conv_b"] = upd("ssm_conv_b", _shard_last(p_cb, me), ssm_conv_b, m_ssm_conv_b, v_ssm_conv_b)
    res["ssm_dt_bias"] = upd("ssm_dt_bias", p_dtb, ssm_dt_bias, m_ssm_dt_bias, v_ssm_dt_bias)
    res["ssm_a_log"] = upd("ssm_a_log", p_al, ssm_a_log, m_ssm_a_log, v_ssm_a_log)
    res["ssm_d"] = upd("ssm_d", p_d, ssm_d, m_ssm_d, v_ssm_d)
    res["ssm_norm_w"] = upd("ssm_norm_w", _shard_last(p_snw, me), ssm_norm_w, m_ssm_norm_w, v_ssm_norm_w)
    res["kv_norm_w"] = upd("kv_norm_w", p_kvnw.reshape(N_DEV, 1, D), kv_norm_w.reshape(1, D),
                           m_kv_norm_w.reshape(1, D), v_kv_norm_w.reshape(1, D))
    res["kv_mod_w"] = upd("kv_mod_w", as_parts_single(gp_kvm_w), kv_mod_w, m_kv_mod_w, v_kv_mod_w)
    res["kv_mod_b"] = upd("kv_mod_b", p_kvm.reshape(N_DEV, 1, 2 * D), kv_mod_b.reshape(1, -1),
                          m_kv_mod_b.reshape(1, -1), v_kv_mod_b.reshape(1, -1))
    res["b_kv"] = upd("b_kv", p_bkv.reshape(N_DEV, 1, KV_DIM), b_kv.reshape(1, -1), m_b_kv.reshape(1, -1),
                      v_b_kv.reshape(1, -1))
    res["attn_b_q"] = upd("attn_b_q", p_bq, attn_b_q, m_attn_b_q, v_attn_b_q)
    res["attn_sinks"] = upd("attn_sinks", p_sk, attn_sinks, m_attn_sinks, v_attn_sinks)
    res["attn_b_o"] = upd("attn_b_o", p_bo, attn_b_o, m_attn_b_o, v_attn_b_o)
    res["final_norm_w"] = upd("final_norm_w", p_fin.reshape(N_DEV, 1, D), final_norm_w.reshape(1, D),
                              m_final_norm_w.reshape(1, D), v_final_norm_w.reshape(1, D))

    names = ["ffn_norm_w", "ffn_w_gu", "ffn_w_down", "mod_w", "mod_b", "mix_norm_w", "ssm_w_in", "ssm_conv_w",
             "ssm_conv_b", "ssm_dt_bias", "ssm_a_log", "ssm_d", "ssm_norm_w", "ssm_w_out", "kv_norm_w", "kv_mod_w",
             "kv_mod_b", "w_kv", "b_kv", "attn_w_q", "attn_b_q", "attn_sinks", "attn_w_o", "attn_b_o", "final_norm_w"]
    vec_shapes = {"kv_norm_w": (D,), "kv_mod_b": (2 * D,), "b_kv": (KV_DIM,), "final_norm_w": (D,)}
    outs = [loss, grad_x]
    for k in range(4):
        for nme in names:
            t = res[nme][k]
            if nme in vec_shapes:
                t = t.reshape(vec_shapes[nme])
            outs.append(t)
    return tuple(outs)
```

```python
import functools

import jax
import jax.numpy as jnp
from jax import lax
from jax.experimental import pallas as pl
from jax.experimental.pallas import tpu as pltpu

F32 = jnp.float32
BF = jnp.bfloat16
MESH = pl.DeviceIdType.MESH

N_DEV = 8
D_MODEL = 1024
DEPTH = 4
N_A = 2
EPS = 1e-5
N_MOD = 9
D_FF = 2816
D_INNER = 2048
SSM_HEADDIM = 64
SSM_HEADS = 32
SSM_GROUPS = 8
SSM_STATE = 128
CONV_WIDTH = 4
CHUNK = 256
CONV_DIM = D_INNER + 2 * SSM_GROUPS * SSM_STATE
IN_PROJ = D_INNER + CONV_DIM + SSM_HEADS
IN_PROJ_PAD = D_INNER + CONV_DIM + 128
ATT_HEADS = 16
KV_HEADS = 4
HEAD_DIM = 64
WINDOW = 128
KV_DIM = 2 * KV_HEADS * HEAD_DIM

ADAM_LR = 0.001
ADAM_B1 = 0.9
ADAM_B2 = 0.999
ADAM_EPS = 1e-08
ADAM_WD = 0.01
ADAM_STEP = 10

VMEM_LIMIT = 48 * 2 ** 20
ADAMW_VMEM_BUDGET = 24 * 2 ** 20
NEG = -1e30


def _call(body, name, grid, in_specs, out_specs, out_shape, scratch=()):
    return pl.pallas_call(
        body, name=name, grid=grid, in_specs=in_specs, out_specs=out_specs, out_shape=out_shape,
        scratch_shapes=list(scratch),
        compiler_params=pltpu.CompilerParams(vmem_limit_bytes=VMEM_LIMIT))


def _tile(n, cap):
    t = (cap // 128) * 128
    while t >= 128:
        if n % t == 0:
            return t
        t -= 128
    return n


def _sds(shape, dtype):
    return jax.ShapeDtypeStruct(shape, dtype)


def _sigmoid(v):
    return 1.0 / (1.0 + jnp.exp(-v))


def _dot(a, b, dims):
    return lax.dot_general(a, b, (dims, ((), ())), preferred_element_type=F32)


def _dot_nn(a, b):
    return _dot(a.astype(BF), b.astype(BF), ((1,), (0,)))


def _dot_nt(a, b):
    return _dot(a.astype(BF), b.astype(BF), ((1,), (1,)))


def _dot_tn(a, b):
    return _dot(a.astype(BF), b.astype(BF), ((0,), (0,)))


def matmul(a, b, mode, out_dtype, name, bias=None, res=None, gate=None, coef=1.0):
    if mode == "nn":
        (M, K), (_, N) = a.shape, b.shape
    elif mode == "nt":
        (M, K), (N, _) = a.shape, b.shape
    else:
        (K, M), (_, N) = a.shape, b.shape
    cap_n = 512 if K > 4096 else 1024
    tm = _tile(M, 1024 if (mode == "tn" or K <= D_FF) else 512)
    tn = _tile(N, cap_n)
    if mode != "tn" and tm * tn > 1024 * 896:
        tn = _tile(N, 512)
    if mode == "nn":
        a_spec = pl.BlockSpec((tm, K), lambda i, j: (i, 0))
        b_spec = pl.BlockSpec((K, tn), lambda i, j: (0, j))
        fn = _dot_nn
    elif mode == "nt":
        a_spec = pl.BlockSpec((tm, K), lambda i, j: (i, 0))
        b_spec = pl.BlockSpec((tn, K), lambda i, j: (j, 0))
        fn = _dot_nt
    else:
        a_spec = pl.BlockSpec((K, tm), lambda i, j: (0, i))
        b_spec = pl.BlockSpec((K, tn), lambda i, j: (0, j))
        fn = _dot_tn
    has_bias, has_res = bias is not None, res is not None
    o_spec = pl.BlockSpec((tm, tn), lambda i, j: (i, j))
    v_spec = pl.BlockSpec((1, tn), lambda i, j: (0, j))
    in_specs, args = [a_spec, b_spec], [a, b]
    if has_bias:
        in_specs.append(v_spec)
        args.append(bias)
    if has_res:
        in_specs += [o_spec, v_spec]
        args += [res, gate]

    def body(*refs):
        a_ref, b_ref = refs[0], refs[1]
        k = 2
        y = fn(a_ref[...], b_ref[...])
        if has_bias:
            y = y + refs[k][...]
            k += 1
        if has_res:
            res_ref, gate_ref = refs[k], refs[k + 1]
            refs[k + 2][...] = y.astype(out_dtype)
            refs[k + 3][...] = res_ref[...] + coef * gate_ref[...] * y
        else:
            refs[k][...] = y.astype(out_dtype)

    if has_res:
        out_shape = (_sds((M, N), out_dtype), _sds((M, N), F32))
        out_specs = (o_spec, o_spec)
    else:
        out_shape = _sds((M, N), out_dtype)
        out_specs = o_spec
    return _call(body, name, (M // tm, N // tn), in_specs, out_specs, out_shape)(*args)


def norm_mod(x, nw, sh, sc, name):
    L, D = x.shape
    tm = _tile(L, 512)

    def body(x_ref, nw_ref, sh_ref, sc_ref, h_ref):
        xf = x_ref[...]
        r = lax.rsqrt(jnp.mean(xf * xf, axis=-1, keepdims=True) + EPS)
        n = xf * r * nw_ref[...]
        h_ref[...] = (n * (1.0 + sc_ref[...]) + sh_ref[...]).astype(BF)

    row = pl.BlockSpec((tm, D), lambda i: (i, 0))
    vec = pl.BlockSpec((1, D), lambda i: (0, 0))
    return _call(body, name, (L // tm,), [row, vec, vec, vec], row, _sds((L, D), BF))(x, nw, sh, sc)


def norm_mod_bwd(x, dh, dres, nw, sc, name):
    L, D = x.shape
    tm = _tile(L, 512)

    def body(x_ref, dh_ref, dres_ref, nw_ref, sc_ref, dx_ref, acc_ref):
        @pl.when(pl.program_id(0) == 0)
        def _():
            acc_ref[...] = jnp.zeros_like(acc_ref)

        xf = x_ref[...]
        dhf = dh_ref[...].astype(F32)
        r = lax.rsqrt(jnp.mean(xf * xf, axis=-1, keepdims=True) + EPS)
        xhat = xf * r
        nwv = nw_ref[...]
        dn = dhf * (1.0 + sc_ref[...])
        dxhat = dn * nwv
        proj = jnp.mean(dxhat * xhat, axis=-1, keepdims=True)
        dx_ref[...] = dres_ref[...] + r * (dxhat - xhat * proj)
        acc_ref[0:1, :] += jnp.sum(dhf, axis=0, keepdims=True)
        acc_ref[1:2, :] += jnp.sum(dhf * xhat * nwv, axis=0, keepdims=True)
        acc_ref[2:3, :] += jnp.sum(dn * xhat, axis=0, keepdims=True)

    row = pl.BlockSpec((tm, D), lambda i: (i, 0))
    vec = pl.BlockSpec((1, D), lambda i: (0, 0))
    acc = pl.BlockSpec((8, D), lambda i: (0, 0))
    return _call(body, name, (L // tm,), [row, row, row, vec, vec], (row, acc),
                 (_sds((L, D), F32), _sds((8, D), F32)))(x, dh, dres, nw, sc)


def final_loss(x, nw, target):
    L, D = x.shape
    tm = _tile(L, 512)

    def body(x_ref, nw_ref, t_ref, dx_ref, acc_ref):
        @pl.when(pl.program_id(0) == 0)
        def _():
            acc_ref[...] = jnp.zeros_like(acc_ref)

        xf = x_ref[...]
        r = lax.rsqrt(jnp.mean(xf * xf, axis=-1, keepdims=True) + EPS)
        xhat = xf * r
        nwv = nw_ref[...]
        err = xhat * nwv - t_ref[...]
        dy = err * (1.0 / D)
        dxhat = dy * nwv
        proj = jnp.mean(dxhat * xhat, axis=-1, keepdims=True)
        dx_ref[...] = r * (dxhat - xhat * proj)
        acc_ref[0:1, :] += jnp.sum(dy * xhat, axis=0, keepdims=True)
        acc_ref[1:2, :] += jnp.sum(err * err, axis=0, keepdims=True)

    row = pl.BlockSpec((tm, D), lambda i: (i, 0))
    vec = pl.BlockSpec((1, D), lambda i: (0, 0))
    acc = pl.BlockSpec((8, D), lambda i: (0, 0))
    return _call(body, "final_loss", (L // tm,), [row, vec, row], (row, acc),
                 (_sds((L, D), F32), _sds((8, D), F32)))(x, nw, target)


def resid_gate_bwd(dxo, y, gate, coef, name):
    L, D = dxo.shape
    tm = _tile(L, 512)

    def body(dxo_ref, y_ref, g_ref, dy_ref, acc_ref):
        @pl.when(pl.program_id(0) == 0)
        def _():
            acc_ref[...] = jnp.zeros_like(acc_ref)

        d = dxo_ref[...]
        dy = coef * g_ref[...] * d
        dy_ref[...] = dy.astype(BF)
        acc_ref[0:1, :] += coef * jnp.sum(d * y_ref[...].astype(F32), axis=0, keepdims=True)
        acc_ref[1:2, :] += jnp.sum(dy, axis=0, keepdims=True)

    row = pl.BlockSpec((tm, D), lambda i: (i, 0))
    vec = pl.BlockSpec((1, D), lambda i: (0, 0))
    acc = pl.BlockSpec((8, D), lambda i: (0, 0))
    return _call(body, name, (L // tm,), [row, row, vec], (row, acc),
                 (_sds((L, D), BF), _sds((8, D), F32)))(dxo, y, gate)


def ffn_up(h, wt):
    L, D = h.shape
    F = wt.shape[0] // 2
    tm, tn = _tile(L, 2048), _tile(F, 256)
    nj = F // tn

    def body(h_ref, wg_ref, wu_ref, g_ref, u_ref, a_ref):
        hv = h_ref[...]
        g = _dot_nt(hv, wg_ref[...])
        u = _dot_nt(hv, wu_ref[...])
        g_ref[...] = g.astype(BF)
        u_ref[...] = u.astype(BF)
        a_ref[...] = (g * _sigmoid(g) * u).astype(BF)

    o = pl.BlockSpec((tm, tn), lambda i, n: (i, n))
    return _call(body, "ffn_up", (L // tm, nj),
                 [pl.BlockSpec((tm, D), lambda i, n: (i, 0)),
                  pl.BlockSpec((tn, D), lambda i, n: (n, 0)),
                  pl.BlockSpec((tn, D), lambda i, n: (n + nj, 0))],
                 (o, o, o), tuple(_sds((L, F), BF) for _ in range(3)))(h, wt, wt)


def ffn_down_dgrad(dy, wd, g, u):
    L, D = dy.shape
    F = wd.shape[0]
    tm, tn = _tile(L, 2048), _tile(F, 256)

    def body(dy_ref, w_ref, g_ref, u_ref, dg_ref, du_ref):
        da = _dot_nt(dy_ref[...], w_ref[...])
        gv = g_ref[...].astype(F32)
        uv = u_ref[...].astype(F32)
        s = _sigmoid(gv)
        dg_ref[...] = (da * uv * s * (1.0 + gv * (1.0 - s))).astype(BF)
        du_ref[...] = (da * gv * s).astype(BF)

    o = pl.BlockSpec((tm, tn), lambda i, n: (i, n))
    return _call(body, "ffn_down_dgrad", (L // tm, F // tn),
                 [pl.BlockSpec((tm, D), lambda i, n: (i, 0)), pl.BlockSpec((tn, D), lambda i, n: (n, 0)), o, o],
                 (o, o), (_sds((L, F), BF), _sds((L, F), BF)))(dy, wd, g, u)


def ffn_up_wgrad(dg, du, h):
    L, F = dg.shape
    D = h.shape[1]
    tm = _tile(F, 256)
    nblk = F // tm

    def half(d, off, prev):
        def body(d_ref, h_ref, *rest):
            rest[-1][...] = _dot_tn(d_ref[...], h_ref[...]).astype(BF)

        in_specs = [pl.BlockSpec((L, tm), lambda i: (0, i)), pl.BlockSpec((L, D), lambda i: (0, 0))]
        args = [d, h]
        aliases = {}
        if prev is not None:
            in_specs.append(pl.BlockSpec(memory_space=pl.ANY))
            args.append(prev)
            aliases = {2: 0}
        return pl.pallas_call(
            body, name="ffn_up_wgrad", grid=(nblk,), in_specs=in_specs,
            out_specs=pl.BlockSpec((tm, D), lambda i: (i + off * nblk, 0)),
            out_shape=_sds((2 * F, D), BF), input_output_aliases=aliases,
            compiler_params=pltpu.CompilerParams(vmem_limit_bytes=VMEM_LIMIT))(*args)

    return half(du, 1, half(dg, 0, None))


def ffn_up_dgrad(dg, du, wt):
    L, F = dg.shape
    D = wt.shape[1]
    tm, tn = _tile(L, 1024), _tile(D, 512)

    def body(dg_ref, du_ref, wg_ref, wu_ref, o_ref):
        o_ref[...] = _dot_nn(dg_ref[...], wg_ref[...]) + _dot_nn(du_ref[...], wu_ref[...])

    a = pl.BlockSpec((tm, F), lambda i, n: (i, 0))
    return _call(body, "ffn_up_dgrad", (L // tm, D // tn),
                 [a, a, pl.BlockSpec((F, tn), lambda i, n: (0, n)), pl.BlockSpec((F, tn), lambda i, n: (1, n))],
                 pl.BlockSpec((tm, tn), lambda i, n: (i, n)), _sds((L, D), F32))(dg, du, wt, wt)


def _shift_rows(cur, other, k, down):
    n = cur.shape[0]
    rows = lax.broadcasted_iota(jnp.int32, cur.shape, 0)
    if down:
        return jnp.where(rows < k, pltpu.roll(other, k, 0), pltpu.roll(cur, k, 0))
    return jnp.where(rows >= n - k, pltpu.roll(other, n - k, 0), pltpu.roll(cur, n - k, 0))


def _conv_pre(cur, prev, w_ref, b_ref):
    s = cur * w_ref[CONV_WIDTH - 1:CONV_WIDTH, :] + b_ref[...]
    for k in range(1, CONV_WIDTH):
        s = s + _shift_rows(cur, prev, k, True) * w_ref[CONV_WIDTH - 1 - k:CONV_WIDTH - k, :]
    return s


_XBC_COL0 = D_INNER // 512


def conv_fwd(zx, w, b):
    L = zx.shape[0]
    tm, tc = _tile(L, 256), 512

    def body(cur_ref, prev_ref, w_ref, b_ref, o_ref):
        cur = cur_ref[...]
        prev = jnp.where(pl.program_id(1) > 0, prev_ref[...], 0.0)
        s = _conv_pre(cur, prev, w_ref, b_ref)
        o_ref[...] = s * _sigmoid(s)

    return _call(body, "conv_fwd", (CONV_DIM // tc, L // tm),
                 [pl.BlockSpec((tm, tc), lambda j, i: (i, _XBC_COL0 + j)),
                  pl.BlockSpec((tm, tc), lambda j, i: (jnp.maximum(i - 1, 0), _XBC_COL0 + j)),
                  pl.BlockSpec((CONV_WIDTH, tc), lambda j, i: (0, j)),
                  pl.BlockSpec((1, tc), lambda j, i: (0, j))],
                 pl.BlockSpec((tm, tc), lambda j, i: (i, j)), _sds((L, CONV_DIM), F32))(zx, zx, w, b)


def conv_bwd_act(dxc, zx, w, b):
    L = zx.shape[0]
    tm, tc = _tile(L, 256), 512

    def body(d_ref, cur_ref, prev_ref, w_ref, b_ref, o_ref):
        cur = cur_ref[...]
        prev = jnp.where(pl.program_id(1) > 0, prev_ref[...], 0.0)
        s = _conv_pre(cur, prev, w_ref, b_ref)
        sg = _sigmoid(s)
        o_ref[...] = d_ref[...] * sg * (1.0 + s * (1.0 - sg))

    return _call(body, "conv_bwd_act", (CONV_DIM // tc, L // tm),
                 [pl.BlockSpec((tm, tc), lambda j, i: (i, j)),
                  pl.BlockSpec((tm, tc), lambda j, i: (i, _XBC_COL0 + j)),
                  pl.BlockSpec((tm, tc), lambda j, i: (jnp.maximum(i - 1, 0), _XBC_COL0 + j)),
                  pl.BlockSpec((CONV_WIDTH, tc), lambda j, i: (0, j)),
                  pl.BlockSpec((1, tc), lambda j, i: (0, j))],
                 pl.BlockSpec((tm, tc), lambda j, i: (i, j)), _sds((L, CONV_DIM), F32))(dxc, zx, zx, w, b)


def conv_bwd(ds, zx, w):
    L = zx.shape[0]
    tm, tc = _tile(L, 256), 512
    nblk = L // tm

    def body(ds_ref, dsn_ref, cur_ref, prev_ref, w_ref, du_ref, acc_ref):
        i = pl.program_id(1)

        @pl.when(i == 0)
        def _():
            acc_ref[...] = jnp.zeros_like(acc_ref)

        ds_c = ds_ref[...]
        ds_n = jnp.where(i < nblk - 1, dsn_ref[...], 0.0)
        cur = cur_ref[...]
        prev = jnp.where(i > 0, prev_ref[...], 0.0)
        du = ds_c * w_ref[CONV_WIDTH - 1:CONV_WIDTH, :]
        acc_ref[CONV_WIDTH - 1:CONV_WIDTH, :] += jnp.sum(ds_c * cur, axis=0, keepdims=True)
        for k in range(1, CONV_WIDTH):
            du = du + _shift_rows(ds_c, ds_n, k, False) * w_ref[CONV_WIDTH - 1 - k:CONV_WIDTH - k, :]
            acc_ref[CONV_WIDTH - 1 - k:CONV_WIDTH - k, :] += jnp.sum(
                ds_c * _shift_rows(cur, prev, k, True), axis=0, keepdims=True)
        acc_ref[CONV_WIDTH:CONV_WIDTH + 1, :] += jnp.sum(ds_c, axis=0, keepdims=True)
        du_ref[...] = du.astype(BF)

    return _call(body, "conv_bwd", (CONV_DIM // tc, nblk),
                 [pl.BlockSpec((tm, tc), lambda j, i: (i, j)),
                  pl.BlockSpec((tm, tc), lambda j, i: (jnp.minimum(i + 1, nblk - 1), j)),
                  pl.BlockSpec((tm, tc), lambda j, i: (i, _XBC_COL0 + j)),
                  pl.BlockSpec((tm, tc), lambda j, i: (jnp.maximum(i - 1, 0), _XBC_COL0 + j)),
                  pl.BlockSpec((CONV_WIDTH, tc), lambda j, i: (0, j))],
                 (pl.BlockSpec((tm, tc), lambda j, i: (i, j)), pl.BlockSpec((8, tc), lambda j, i: (0, j))),
                 (_sds((L, CONV_DIM), BF), _sds((8, CONV_DIM), F32)))(ds, ds, zx, zx, w)


_DT_COL = (D_INNER + CONV_DIM) // 128


def dt_prep(zx, bias_pad, alog_pad):
    L = zx.shape[0]

    def body(raw_ref, b_ref, al_ref, dt_ref, acs_ref):
        v = raw_ref[...] + b_ref[...]
        dt = jnp.maximum(v, 0.0) + jnp.log(1.0 + jnp.exp(-jnp.abs(v)))
        dt_ref[...] = dt
        acs = dt * (-jnp.exp(al_ref[...]))
        rows = lax.broadcasted_iota(jnp.int32, acs.shape, 0)
        s = 1
        while s < CHUNK:
            acs = acs + jnp.where(rows >= s, pltpu.roll(acs, s, 0), 0.0)
            s *= 2
        acs_ref[...] = acs

    blk = pl.BlockSpec((CHUNK, 128), lambda i: (i, 0))
    vec = pl.BlockSpec((1, 128), lambda i: (0, 0))
    return _call(body, "dt_prep", (L // CHUNK,),
                 [pl.BlockSpec((CHUNK, 128), lambda i: (i, _DT_COL)), vec, vec], (blk, blk),
                 (_sds((L, 128), F32), _sds((L, 128), F32)))(zx, bias_pad, alog_pad)


def dt_bwd(ddt, da, dt, zx, bias_pad, alog_pad):
    L = zx.shape[0]
    tm = _tile(L, 512)

    def body(ddt_ref, da_ref, dt_ref, raw_ref, b_ref, al_ref, o_ref, acc_ref):
        @pl.when(pl.program_id(0) == 0)
        def _():
            acc_ref[...] = jnp.zeros_like(acc_ref)

        A = -jnp.exp(al_ref[...])
        dav = da_ref[...]
        dd = ddt_ref[...] + dav * A
        draw = dd * _sigmoid(raw_ref[...] + b_ref[...])
        o_ref[...] = draw.astype(BF)
        acc_ref[0:1, :] += jnp.sum(draw, axis=0, keepdims=True)
        acc_ref[1:2, :] += jnp.sum(dav * dt_ref[...], axis=0, keepdims=True) * A

    blk = pl.BlockSpec((tm, 128), lambda i: (i, 0))
    vec = pl.BlockSpec((1, 128), lambda i: (0, 0))
    return _call(body, "dt_bwd", (L // tm,),
                 [blk, blk, blk, pl.BlockSpec((tm, 128), lambda i: (i, _DT_COL)), vec, vec],
                 (blk, pl.BlockSpec((8, 128), lambda i: (0, 0))),
                 (_sds((L, 128), BF), _sds((8, 128), F32)))(ddt, da, dt, zx, bias_pad, alog_pad)


_HPG = SSM_HEADS // SSM_GROUPS
_GW = _HPG * SSM_HEADDIM
_B_COL0 = D_INNER // SSM_STATE
_C_COL0 = (D_INNER + SSM_GROUPS * SSM_STATE) // SSM_STATE


def _ssd_head(x, dtc, ac, ar, r, causal):
    xh = x[:, SSM_HEADDIM * r:SSM_HEADDIM * (r + 1)]
    acol = ac[:, r:r + 1]
    arow = ar[r:r + 1, :]
    alast = ar[r:r + 1, CHUNK - 1:CHUNK]
    lm = jnp.exp(jnp.where(causal, acol - arow, NEG))
    return xh, xh * dtc[:, r:r + 1], acol, alast, lm


def ssd_fwd(xc, dt_g, acs_g, acsT_g, d_exp):
    L = xc.shape[0]
    nc = L // CHUNK

    def body(x_ref, b_ref, c_ref, dt_ref, ac_ref, ar_ref, d_ref, y_ref, pst_ref, st_ref):
        @pl.when(pl.program_id(1) == 0)
        def _():
            st_ref[...] = jnp.zeros_like(st_ref)

        x, Bm, Cm = x_ref[...], b_ref[...], c_ref[...]
        dtc, ac, ar = dt_ref[...], ac_ref[...], ar_ref[...]
        causal = lax.broadcasted_iota(jnp.int32, (CHUNK, CHUNK), 0) >= lax.broadcasted_iota(jnp.int32, (CHUNK, CHUNK), 1)
        CB = _dot_nt(Cm, Bm)
        for r in range(_HPG):
            xh, xd, acol, alast, lm = _ssd_head(x, dtc, ac, ar, r, causal)
            P = st_ref[r]
            y = _dot_nn(CB * lm, xd) + jnp.exp(acol) * _dot_nt(Cm, P)
            y_ref[:, SSM_HEADDIM * r:SSM_HEADDIM * (r + 1)] = y + d_ref[:, SSM_HEADDIM * r:SSM_HEADDIM * (r + 1)] * xh
            pst_ref[r] = P
            st_ref[r] = P * jnp.exp(alast) + _dot_tn(xd * jnp.exp(alast - acol), Bm)

    return _call(
        body, "ssd_fwd", (SSM_GROUPS, nc),
        [pl.BlockSpec((CHUNK, _GW), lambda g, c: (c, g)),
         pl.BlockSpec((CHUNK, SSM_STATE), lambda g, c: (c, _B_COL0 + g)),
         pl.BlockSpec((CHUNK, SSM_STATE), lambda g, c: (c, _C_COL0 + g)),
         pl.BlockSpec((None, CHUNK, _HPG), lambda g, c: (g, c, 0)),
         pl.BlockSpec((None, CHUNK, _HPG), lambda g, c: (g, c, 0)),
         pl.BlockSpec((None, _HPG, CHUNK), lambda g, c: (g, 0, c)),
         pl.BlockSpec((None, 1, _GW), lambda g, c: (g, 0, 0))],
        (pl.BlockSpec((CHUNK, _GW), lambda g, c: (c, g)),
         pl.BlockSpec((None, None, _HPG, SSM_HEADDIM, SSM_STATE), lambda g, c: (c, g, 0, 0, 0))),
        (_sds((L, D_INNER), F32), _sds((nc, SSM_GROUPS, _HPG, SSM_HEADDIM, SSM_STATE), F32)),
        scratch=[pltpu.VMEM((_HPG, SSM_HEADDIM, SSM_STATE), F32)],
    )(xc, xc, xc, dt_g, acs_g, acsT_g, d_exp)


def ssd_bwd(dy, xc, dt_g, acs_g, acsT_g, pst, d_exp):
    L = xc.shape[0]
    nc = L // CHUNK

    def body(dy_ref, x_ref, b_ref, c_ref, dt_ref, ac_ref, ar_ref, pst_ref, d_ref,
             dx_ref, db_ref, dc_ref, ddt_ref, da_ref, dd_ref, dp_ref):
        @pl.when(pl.program_id(1) == 0)
        def _():
            dp_ref[...] = jnp.zeros_like(dp_ref)
            dd_ref[...] = jnp.zeros_like(dd_ref)

        dyv, x, Bm, Cm = dy_ref[...], x_ref[...], b_ref[...], c_ref[...]
        dtc, ac, ar = dt_ref[...], ac_ref[...], ar_ref[...]
        ri = lax.broadcasted_iota(jnp.int32, (CHUNK, CHUNK), 0)
        ci = lax.broadcasted_iota(jnp.int32, (CHUNK, CHUNK), 1)
        causal = ri >= ci
        lane4 = lax.broadcasted_iota(jnp.int32, (CHUNK, _HPG), 1)
        CB = _dot_nt(Cm, Bm)
        dB = jnp.zeros((CHUNK, SSM_STATE), F32)
        dC = jnp.zeros((CHUNK, SSM_STATE), F32)
        dCB = jnp.zeros((CHUNK, CHUNK), F32)
        ddt_blk = jnp.zeros((CHUNK, _HPG), F32)
        da_blk = jnp.zeros((CHUNK, _HPG), F32)
        for r in range(_HPG):
            sl = slice(SSM_HEADDIM * r, SSM_HEADDIM * (r + 1))
            xh, xd, acol, alast, lm = _ssd_head(x, dtc, ac, ar, r, causal)
            dyh = dyv[:, sl]
            P = pst_ref[r]
            dPn = dp_ref[r]
            eA = jnp.exp(acol)
            cd = jnp.exp(alast)
            dte = jnp.exp(alast - acol)
            G = CB * lm
            Z = _dot_nt(Cm, P)
            dZ = eA * dyh
            dC = dC + _dot_nn(dZ, P)
            dp_ref[r] = dPn * cd + _dot_tn(dZ, Cm)
            dA_col = jnp.sum(dZ * Z, axis=1, keepdims=True)
            BdS = _dot_nt(Bm, dPn)
            dxd = dte * BdS
            dB = dB + dte * _dot_nn(xd, dPn)
            t = jnp.sum(xd * BdS, axis=1, keepdims=True) * dte
            dA_col = dA_col - t
            dA_last = jnp.sum(t, axis=0, keepdims=True) + jnp.sum(
                jnp.sum(dPn * P, axis=1, keepdims=True), axis=0, keepdims=True) * cd
            dG = _dot_nt(dyh, xd)
            dxd = dxd + _dot_tn(G, dyh)
            dCB = dCB + dG * lm
            W = dG * G
            dA_col = dA_col + jnp.sum(W, axis=1, keepdims=True)
            dA_row = jnp.sum(jnp.where(ri == ci, dA_col, 0.0), axis=0, keepdims=True) - jnp.sum(W, axis=0, keepdims=True)
            da_col = jnp.sum(jnp.where(ci >= ri, dA_row, 0.0), axis=1, keepdims=True) + dA_last
            da_blk = jnp.where(lane4 == r, da_col, da_blk)
            ddt_blk = jnp.where(lane4 == r, jnp.sum(dxd * xh, axis=1, keepdims=True), ddt_blk)
            dx_ref[:, sl] = dxd * dtc[:, r:r + 1] + d_ref[:, sl] * dyh
        dc_ref[...] = dC + _dot_nn(dCB, Bm)
        db_ref[...] = dB + _dot_tn(dCB, Cm)
        ddt_ref[...] = ddt_blk
        da_ref[...] = da_blk
        dd_ref[...] += jnp.sum(dyv * x, axis=0, keepdims=True)

    rc = lambda g, c: (nc - 1 - c, g)
    small = pl.BlockSpec((None, CHUNK, _HPG), lambda g, c: (g, nc - 1 - c, 0))
    return _call(
        body, "ssd_bwd", (SSM_GROUPS, nc),
        [pl.BlockSpec((CHUNK, _GW), rc),
         pl.BlockSpec((CHUNK, _GW), rc),
         pl.BlockSpec((CHUNK, SSM_STATE), lambda g, c: (nc - 1 - c, _B_COL0 + g)),
         pl.BlockSpec((CHUNK, SSM_STATE), lambda g, c: (nc - 1 - c, _C_COL0 + g)),
         small, small,
         pl.BlockSpec((None, _HPG, CHUNK), lambda g, c: (g, 0, nc - 1 - c)),
         pl.BlockSpec((None, None, _HPG, SSM_HEADDIM, SSM_STATE), lambda g, c: (nc - 1 - c, g, 0, 0, 0)),
         pl.BlockSpec((None, 1, _GW), lambda g, c: (g, 0, 0))],
        (pl.BlockSpec((CHUNK, _GW), rc),
         pl.BlockSpec((CHUNK, SSM_STATE), rc),
         pl.BlockSpec((CHUNK, SSM_STATE), rc),
         small, small,
         pl.BlockSpec((None, 1, _GW), lambda g, c: (g, 0, 0))),
        (_sds((L, D_INNER), F32), _sds((L, SSM_GROUPS * SSM_STATE), F32), _sds((L, SSM_GROUPS * SSM_STATE), F32),
         _sds((SSM_GROUPS, L, _HPG), F32), _sds((SSM_GROUPS, L, _HPG), F32), _sds((SSM_GROUPS, 1, _GW), F32)),
        scratch=[pltpu.VMEM((_HPG, SSM_HEADDIM, SSM_STATE), F32)],
    )(dy, xc, xc, xc, dt_g, acs_g, acsT_g, pst, d_exp)


_NGW = D_INNER // SSM_GROUPS


def gate_norm(y, zx, nw):
    L = y.shape[0]
    tm = _tile(L, 256)

    def body(y_ref, z_ref, nw_ref, o_ref):
        for g in range(SSM_GROUPS):
            sl = slice(_NGW * g, _NGW * (g + 1))
            z = z_ref[:, sl]
            y2 = y_ref[:, sl] * (z * _sigmoid(z))
            r = lax.rsqrt(jnp.mean(y2 * y2, axis=-1, keepdims=True) + EPS)
            o_ref[:, sl] = (y2 * r * nw_ref[:, sl]).astype(BF)

    row = pl.BlockSpec((tm, D_INNER), lambda i: (i, 0))
    return _call(body, "gate_norm", (L // tm,), [row, row, pl.BlockSpec((1, D_INNER), lambda i: (0, 0))],
                 row, _sds((L, D_INNER), BF))(y, zx, nw)


def gate_norm_bwd(dyn, y, zx, nw):
    L = y.shape[0]
    tm = _tile(L, 256)

    def body(d_ref, y_ref, z_ref, nw_ref, dy_ref, dz_ref, acc_ref):
        @pl.when(pl.program_id(0) == 0)
        def _():
            acc_ref[...] = jnp.zeros_like(acc_ref)

        for g in range(SSM_GROUPS):
            sl = slice(_NGW * g, _NGW * (g + 1))
            z = z_ref[:, sl]
            yv = y_ref[:, sl]
            sg = _sigmoid(z)
            sz = z * sg
            y2 = yv * sz
            r = lax.rsqrt(jnp.mean(y2 * y2, axis=-1, keepdims=True) + EPS)
            yh = y2 * r
            d = d_ref[:, sl]
            dn = d * nw_ref[:, sl]
            dy2 = r * (dn - yh * jnp.mean(dn * yh, axis=-1, keepdims=True))
            dy_ref[:, sl] = dy2 * sz
            dz_ref[:, sl] = (dy2 * yv * sg * (1.0 + z * (1.0 - sg))).astype(BF)
            acc_ref[0:1, sl] += jnp.sum(d * yh, axis=0, keepdims=True)

    row = pl.BlockSpec((tm, D_INNER), lambda i: (i, 0))
    return _call(body, "gate_norm_bwd", (L // tm,), [row, row, row, pl.BlockSpec((1, D_INNER), lambda i: (0, 0))],
                 (row, row, pl.BlockSpec((8, D_INNER), lambda i: (0, 0))),
                 (_sds((L, D_INNER), F32), _sds((L, D_INNER), BF), _sds((8, D_INNER), F32)))(dyn, y, zx, nw)


_SCALE = HEAD_DIM ** -0.5
_REP = ATT_HEADS // KV_HEADS
_V_OFF = KV_HEADS * HEAD_DIM


def _stack_heads(ref, k):
    return jnp.concatenate([ref[:, HEAD_DIM * (k * _REP + r):HEAD_DIM * (k * _REP + r + 1)] for r in range(_REP)],
                           axis=0)


def _stack_sinks(s_ref, k):
    return jnp.concatenate([jnp.broadcast_to(s_ref[:, k * _REP + r:k * _REP + r + 1], (WINDOW, 1))
                            for r in range(_REP)], axis=0)


def _attn_probs(q4, kp, kc, sink, first):
    shape = (_REP * WINDOW, WINDOW)
    rows = jnp.bitwise_and(lax.broadcasted_iota(jnp.int32, shape, 0), WINDOW - 1)
    cols = lax.broadcasted_iota(jnp.int32, shape, 1)
    sp = jnp.where(jnp.logical_and(cols > rows, jnp.logical_not(first)), _dot_nt(q4, kp) * _SCALE, NEG)
    sc = jnp.where(cols <= rows, _dot_nt(q4, kc) * _SCALE, NEG)
    m = jnp.maximum(jnp.maximum(jnp.max(sp, axis=1, keepdims=True), jnp.max(sc, axis=1, keepdims=True)), sink)
    pp = jnp.exp(sp - m)
    pc = jnp.exp(sc - m)
    ps = jnp.exp(sink - m)
    inv = 1.0 / (jnp.sum(pp, axis=1, keepdims=True) + jnp.sum(pc, axis=1, keepdims=True) + ps)
    return pp * inv, pc * inv, ps * inv


def attn_fwd(q, kv, sinks_pad):
    L = q.shape[0]
    nb = L // WINDOW

    def body(q_ref, kc_ref, kp_ref, s_ref, o_ref):
        first = pl.program_id(0) == 0
        for k in range(KV_HEADS):
            ks = slice(HEAD_DIM * k, HEAD_DIM * (k + 1))
            vs = slice(_V_OFF + HEAD_DIM * k, _V_OFF + HEAD_DIM * (k + 1))
            pp, pc, _ = _attn_probs(_stack_heads(q_ref, k), kp_ref[:, ks], kc_ref[:, ks], _stack_sinks(s_ref, k), first)
            o4 = _dot_nn(pp, kp_ref[:, vs]) + _dot_nn(pc, kc_ref[:, vs])
            for r in range(_REP):
                h = k * _REP + r
                o_ref[:, HEAD_DIM * h:HEAD_DIM * (h + 1)] = o4[WINDOW * r:WINDOW * (r + 1)].astype(BF)

    qspec = pl.BlockSpec((WINDOW, D_MODEL), lambda i: (i, 0))
    return _call(body, "attn_fwd", (nb,),
                 [qspec, pl.BlockSpec((WINDOW, KV_DIM), lambda i: (i, 0)),
                  pl.BlockSpec((WINDOW, KV_DIM), lambda i: (jnp.maximum(i - 1, 0), 0)),
                  pl.BlockSpec((1, 128), lambda i: (0, 0))],
                 qspec, _sds((L, D_MODEL), BF))(q, kv, kv, sinks_pad)


def attn_bwd(q, kv, do, sinks_pad):
    L = q.shape[0]
    nb = L // WINDOW

    def body(q_ref, kc_ref, kp_ref, do_ref, s_ref, dq_ref, dc_ref, dp_ref, acc_ref):
        first = pl.program_id(0) == 0

        @pl.when(first)
        def _():
            acc_ref[...] = jnp.zeros_like(acc_ref)

        lane = lax.broadcasted_iota(jnp.int32, (1, 128), 1)
        dsink = jnp.zeros((1, 128), F32)
        for k in range(KV_HEADS):
            ks = slice(HEAD_DIM * k, HEAD_DIM * (k + 1))
            vs = slice(_V_OFF + HEAD_DIM * k, _V_OFF + HEAD_DIM * (k + 1))
            kp, kc, vp, vc = kp_ref[:, ks], kc_ref[:, ks], kp_ref[:, vs], kc_ref[:, vs]
            q4 = _stack_heads(q_ref, k)
            do4 = _stack_heads(do_ref, k)
            pp, pc, ps = _attn_probs(q4, kp, kc, _stack_sinks(s_ref, k), first)
            dpp = _dot_nt(do4, vp)
            dpc = _dot_nt(do4, vc)
            delta = jnp.sum(pp * dpp, axis=1, keepdims=True) + jnp.sum(pc * dpc, axis=1, keepdims=True)
            dsp = pp * (dpp - delta) * _SCALE
            dsc = pc * (dpc - delta) * _SCALE
            dq4 = _dot_nn(dsp, kp) + _dot_nn(dsc, kc)
            psd = ps * delta
            for r in range(_REP):
                h = k * _REP + r
                rs = slice(WINDOW * r, WINDOW * (r + 1))
                dq_ref[:, HEAD_DIM * h:HEAD_DIM * (h + 1)] = dq4[rs]
                dsink = dsink + jnp.where(lane == h, -jnp.sum(psd[rs], axis=0, keepdims=True), 0.0)
            dp_ref[:, ks] = _dot_tn(dsp, q4)
            dc_ref[:, ks] = _dot_tn(dsc, q4)
            dp_ref[:, vs] = _dot_tn(pp, do4)
            dc_ref[:, vs] = _dot_tn(pc, do4)
        acc_ref[0:1, :] += jnp.sum(dq_ref[...], axis=0, keepdims=True)
        acc_ref[1:2, 0:128] += dsink

    qspec = pl.BlockSpec((WINDOW, D_MODEL), lambda i: (i, 0))
    kspec = pl.BlockSpec((WINDOW, KV_DIM), lambda i: (i, 0))
    return _call(body, "attn_bwd", (nb,),
                 [qspec, kspec, pl.BlockSpec((WINDOW, KV_DIM), lambda i: (jnp.maximum(i - 1, 0), 0)), qspec,
                  pl.BlockSpec((1, 128), lambda i: (0, 0))],
                 (qspec, kspec, kspec, pl.BlockSpec((8, D_MODEL), lambda i: (0, 0))),
                 (_sds((L, D_MODEL), F32), _sds((L, KV_DIM), F32), _sds((L, KV_DIM), F32), _sds((8, D_MODEL), F32)),
                 )(q, kv, kv, do, sinks_pad)


def kv_grad_combine(parts):
    L = parts[0][0].shape[0]
    nb = L // WINDOW
    n = len(parts)

    def body(*refs):
        i = pl.program_id(0)
        o_ref, acc_ref = refs[2 * n], refs[2 * n + 1]

        @pl.when(i == 0)
        def _():
            acc_ref[...] = jnp.zeros_like(acc_ref)

        tot = refs[0][...]
        nxt = refs[1][...]
        for a in range(1, n):
            tot = tot + refs[2 * a][...]
            nxt = nxt + refs[2 * a + 1][...]
        tot = tot + jnp.where(i < nb - 1, nxt, 0.0)
        o_ref[...] = tot
        acc_ref[0:1, :] += jnp.sum(tot, axis=0, keepdims=True)

    cur = pl.BlockSpec((WINDOW, KV_DIM), lambda i: (i, 0))
    nxt = pl.BlockSpec((WINDOW, KV_DIM), lambda i: (jnp.minimum(i + 1, nb - 1), 0))
    args = [t for p in parts for t in p]
    return _call(body, "kv_grad_combine", (nb,), [cur, nxt] * n,
                 (cur, pl.BlockSpec((8, KV_DIM), lambda i: (0, 0))),
                 (_sds((L, KV_DIM), F32), _sds((8, KV_DIM), F32)))(*args)


def mod_fwd(c_all, w, b, name):
    n, _, C = w.shape

    def body(c_ref, w_ref, b_ref, o_ref, ca_ref):
        cv = c_ref[...]
        ca = cv * _sigmoid(cv)
        ca_ref[...] = ca
        o_ref[...] = _dot(ca, w_ref[...], ((1,), (0,))) + b_ref[...]

    return _call(body, name, (n,),
                 [pl.BlockSpec((N_DEV, D_MODEL), lambda i: (0, 0)),
                  pl.BlockSpec((None, D_MODEL, C), lambda i: (i, 0, 0)),
                  pl.BlockSpec((None, 1, C), lambda i: (i, 0, 0))],
                 (pl.BlockSpec((None, N_DEV, C), lambda i: (i, 0, 0)), pl.BlockSpec((N_DEV, D_MODEL), lambda i: (0, 0))),
                 (_sds((n, N_DEV, C), F32), _sds((N_DEV, D_MODEL), F32)))(c_all, w, b)


def mod_wgrad(c_act_t, dmod, name):
    n, _, C = dmod.shape
    tr = 256

    def body(ct_ref, d_ref, o_ref):
        acc = ct_ref[:, 0:1] * d_ref[0:1, :]
        for bidx in range(1, N_DEV):
            acc = acc + ct_ref[:, bidx:bidx + 1] * d_ref[bidx:bidx + 1, :]
        o_ref[...] = acc

    return _call(body, name, (n, D_MODEL // tr),
                 [pl.BlockSpec((tr, N_DEV), lambda i, j: (j, 0)),
                  pl.BlockSpec((None, N_DEV, C), lambda i, j: (i, 0, 0))],
                 pl.BlockSpec((None, tr, C), lambda i, j: (i, j, 0)), _sds((n, D_MODEL, C), F32))(c_act_t, dmod)


def _my_pos():
    return lax.axis_index("x"), lax.axis_index("y"), lax.axis_index("c")


def small_all_gather(v):
    m_per, n = v.shape

    def body(x_ref, out_ref, send_sems, recv_sems, local_sem):
        x, y, c = _my_pos()
        me, sibling = (x, y, c), (x, y, 1 - c)
        chips = [(1 - x, y), (x, 1 - y), (1 - x, 1 - y)]

        def rows(px, py, pc):
            return out_ref.at[pl.ds((4 * px + 2 * py + pc) * m_per, m_per), :]

        def copy(k, block, to, src=None):
            return pltpu.make_async_remote_copy(
                src_ref=rows(*block) if src is None else src, dst_ref=rows(*block),
                send_sem=send_sems.at[k], recv_sem=recv_sems.at[k], device_id=to, device_id_type=MESH)

        mine = pltpu.make_async_copy(x_ref, rows(*me), local_sem)
        mine.start()
        first = [copy(0, me, sibling, src=x_ref)]
        first += [copy(1 + j, me, (*chip, c), src=x_ref) for j, chip in enumerate(chips)]
        for cp in first:
            cp.start()
        passed = [copy(4 + j, (*chip, c), sibling) for j, chip in enumerate(chips)]
        for j, chip in enumerate(chips):
            copy(1 + j, (*chip, c), me).wait_recv()
            passed[j].start()
        copy(0, sibling, me).wait_recv()
        for j, chip in enumerate(chips):
            copy(4 + j, (*chip, 1 - c), me).wait_recv()
        for cp in first + passed:
            cp.wait_send()
        mine.wait()

    return pl.pallas_call(
        body, name="small_all_gather",
        out_shape=_sds((N_DEV * m_per, n), v.dtype),
        in_specs=[pl.BlockSpec(memory_space=pltpu.VMEM)],
        out_specs=pl.BlockSpec(memory_space=pltpu.VMEM),
        scratch_shapes=[pltpu.SemaphoreType.DMA((7,)), pltpu.SemaphoreType.DMA((7,)), pltpu.SemaphoreType.DMA],
        compiler_params=pltpu.CompilerParams(vmem_limit_bytes=VMEM_LIMIT),
    )(v)


def big_all_gather(arrs):
    n = len(arrs)

    def body(*refs):
        ins, outs = refs[:n], refs[n:2 * n]
        send_sems, recv_sems, local_sems = refs[2 * n], refs[2 * n + 1], refs[2 * n + 2]
        x, y, c = _my_pos()
        me, sibling = (x, y, c), (x, y, 1 - c)
        chips = [(1 - x, y), (x, 1 - y), (1 - x, 1 - y)]

        def slot(a, px, py, pc):
            return outs[a].at[4 * px + 2 * py + pc]

        def copy(a, k, block, to, src=None):
            return pltpu.make_async_remote_copy(
                src_ref=slot(a, *block) if src is None else src, dst_ref=slot(a, *block),
                send_sem=send_sems.at[7 * a + k], recv_sem=recv_sems.at[7 * a + k], device_id=to, device_id_type=MESH)

        mine = [pltpu.make_async_copy(ins[a], slot(a, *me), local_sems.at[a]) for a in range(n)]
        for cp in mine:
            cp.start()
        first = []
        for a in range(n):
            first.append(copy(a, 0, me, sibling, src=ins[a]))
            first += [copy(a, 1 + j, me, (*chip, c), src=ins[a]) for j, chip in enumerate(chips)]
        for cp in first:
            cp.start()
        passed = []
        for a in range(n):
            for j, chip in enumerate(chips):
                copy(a, 1 + j, (*chip, c), me).wait_recv()
                fwd = copy(a, 4 + j, (*chip, c), sibling)
                fwd.start()
                passed.append(fwd)
        for a in range(n):
            copy(a, 0, sibling, me).wait_recv()
            for j, chip in enumerate(chips):
                copy(a, 4 + j, (*chip, 1 - c), me).wait_recv()
        for cp in first + passed:
            cp.wait_send()
        for cp in mine:
            cp.wait()

    hbm = pl.BlockSpec(memory_space=pltpu.HBM)
    return pl.pallas_call(
        body, name="big_all_gather",
        out_shape=[_sds((N_DEV,) + a.shape, a.dtype) for a in arrs],
        in_specs=[hbm] * n, out_specs=[hbm] * n,
        scratch_shapes=[pltpu.SemaphoreType.DMA((7 * n,)), pltpu.SemaphoreType.DMA((7 * n,)),
                        pltpu.SemaphoreType.DMA((n,))],
    )(*arrs)


_FLIPS =[(fx, fy, fc) for fx in (0, 1) for fy in (0, 1) for fc in (0, 1)][1:]
_HBM = pl.BlockSpec(memory_space=pltpu.HBM)
_SEM = pl.BlockSpec(memory_space=pltpu.SEMAPHORE)
_EFFECT = pltpu.SideEffectType.DATAFLOW_SIDE_EFFECTING


def _flip(x, y, c, f):
    return (1 - x if f[0] else x), (1 - y if f[1] else y), (1 - c if f[2] else c)


def _xfer_copies(srcs, lands, send_sems, recv_sems, scatter):
    x, y, c = _my_pos()
    me = 4 * x + 2 * y + c
    copies = []
    for a in range(len(srcs)):
        for k, f in enumerate(_FLIPS):
            px, py, pc = _flip(x, y, c, f)
            src = srcs[a].at[4 * px + 2 * py + pc] if scatter else srcs[a]
            copies.append(pltpu.make_async_remote_copy(
                src_ref=src, dst_ref=lands[a].at[me], send_sem=send_sems.at[7 * a + k],
                recv_sem=recv_sems.at[7 * a + k], device_id=(px, py, pc), device_id_type=MESH))
    return copies


def _own_copies(srcs, lands, local_sems, scatter):
    x, y, c = _my_pos()
    me = 4 * x + 2 * y + c
    return [pltpu.make_async_copy(srcs[a].at[me] if scatter else srcs[a], lands[a].at[me], local_sems.at[a])
            for a in range(len(srcs))]


def xfer_start(arrs, scatter, after, name):
    n = len(arrs)
    land_shapes = [a.shape if scatter else (N_DEV,) + a.shape for a in arrs]

    def body(*refs):
        srcs, lands = refs[:n], refs[n:2 * n]
        send_sems, recv_sems, local_sems = refs[2 * n + 1], refs[2 * n + 2], refs[2 * n + 3]
        token = refs[-1]
        for cp in _xfer_copies(srcs, lands, send_sems, recv_sems, scatter):
            cp.start()
        for cp in _own_copies(srcs, lands, local_sems, scatter):
            cp.start()
        token[...] = jnp.zeros_like(token)

    out = pl.pallas_call(
        body, name=name,
        out_shape=(pltpu.SemaphoreType.DMA((7 * n,)), pltpu.SemaphoreType.DMA((7 * n,)),
                   pltpu.SemaphoreType.DMA((n,)),
                   *[pltpu.HBM(a.shape, a.dtype) for a in arrs],
                   *[pltpu.HBM(s, a.dtype) for s, a in zip(land_shapes, arrs)],
                   _sds((8, 128), F32)),
        in_specs=[_HBM] * (2 * n) + [pl.BlockSpec(memory_space=pl.ANY)],
        out_specs=(_SEM, _SEM, _SEM, *([_HBM] * (2 * n)), pl.BlockSpec(memory_space=pltpu.VMEM)),
        input_output_aliases={i: 3 + i for i in range(2 * n)},
        compiler_params=pltpu.CompilerParams(has_side_effects=_EFFECT),
    )(*[pltpu.with_memory_space_constraint(a, pltpu.HBM) for a in arrs],
      *[pltpu.with_memory_space_constraint(lax.empty(s, a.dtype), pltpu.HBM) for s, a in zip(land_shapes, arrs)],
      after)
    return (out[0], out[1], out[2], list(out[3:3 + n]), list(out[3 + n:3 + 2 * n]), scatter), out[-1]


def xfer_wait(handle, after, name):
    send_sems, recv_sems, local_sems, srcs, lands, scatter = handle
    n = len(srcs)

    def body(*refs):
        srcs_r, lands_r = refs[:n], refs[n:2 * n]
        ssem, rsem, lsem = refs[2 * n], refs[2 * n + 1], refs[2 * n + 2]
        for cp in _xfer_copies(srcs_r, lands_r, ssem, rsem, scatter):
            cp.wait_send()
            cp.wait_recv()
        for cp in _own_copies(srcs_r, lands_r, lsem, scatter):
            cp.wait()

    out = pl.pallas_call(
        body, name=name,
        out_shape=(*[pltpu.HBM(a.shape, a.dtype) for a in srcs], *[pltpu.HBM(a.shape, a.dtype) for a in lands]),
        in_specs=[_HBM] * (2 * n) + [_SEM, _SEM, _SEM, pl.BlockSpec(memory_space=pl.ANY)],
        out_specs=tuple([_HBM] * (2 * n)),
        input_output_aliases={i: i for i in range(2 * n)},
        compiler_params=pltpu.CompilerParams(has_side_effects=_EFFECT),
    )(*srcs, *lands, send_sems, recv_sems, local_sems, after)
    return list(out[n:])


def adamw(parts, w, m, v, name, row0=0, prev=None):
    r_tot, C = w.shape
    n_parts, R = parts.shape[0], parts.shape[1]
    row_bytes = 2 * (n_parts * C * parts.dtype.itemsize + 7 * C * 4)
    tr = R
    for cand in (512, 352, 256, 176, 128, 64):
        if R % cand == 0 and row0 % cand == 0 and R > cand and cand * row_bytes <= ADAMW_VMEM_BUDGET:
            tr = cand
            break
    tc = C
    if tr == R and R * row_bytes > ADAMW_VMEM_BUDGET:
        assert row0 == 0 and R == r_tot
        tc = next(t for t in (512, 256, 128) if C % t == 0 and R * row_bytes * t // C <= ADAMW_VMEM_BUDGET)
    assert row0 % tr == 0 and (tr % 8 == 0 or (tr == r_tot and row0 == 0))
    blk0 = row0 // tr
    c1 = 1.0 / (1.0 - ADAM_B1 ** ADAM_STEP)
    c2 = 1.0 / (1.0 - ADAM_B2 ** ADAM_STEP)

    def body(p_ref, w_ref, m_ref, v_ref, *rest):
        g_ref, d_ref, nm_ref, nv_ref = rest[-4:]
        g = p_ref[0].astype(F32)
        for k in range(1, n_parts):
            g = g + p_ref[k].astype(F32)
        nm = ADAM_B1 * m_ref[...] + (1.0 - ADAM_B1) * g
        nv = ADAM_B2 * v_ref[...] + (1.0 - ADAM_B2) * (g * g)
        g_ref[...] = g
        nm_ref[...] = nm
        nv_ref[...] = nv
        d_ref[...] = -ADAM_LR * ((nm * c1) / (jnp.sqrt(nv * c2) + ADAM_EPS) + ADAM_WD * w_ref[...])

    if tc == C:
        grid = (R // tr,)
        blk = pl.BlockSpec((tr, C), lambda i: (i + blk0, 0))
        p_spec = pl.BlockSpec((n_parts, tr, C), lambda i: (0, i, 0))
    else:
        grid = (C // tc,)
        blk = pl.BlockSpec((R, tc), lambda i: (0, i))
        p_spec = pl.BlockSpec((n_parts, R, tc), lambda i: (0, 0, i))
    in_specs = [p_spec, blk, blk, blk]
    args = [parts, w, m, v]
    aliases = {}
    if prev is not None:
        in_specs += [pl.BlockSpec(memory_space=pl.ANY)] * 4
        args += list(prev)
        aliases = {4 + k: k for k in range(4)}
    return pl.pallas_call(
        body, name=name, grid=grid, in_specs=in_specs, out_specs=(blk, blk, blk, blk),
        out_shape=tuple(_sds((r_tot, C), F32) for _ in range(4)), input_output_aliases=aliases,
        compiler_params=pltpu.CompilerParams(vmem_limit_bytes=VMEM_LIMIT))(*args)


def _ffn_fwd(x, nw, sh, sc, g, wt_gu, w_dn):
    h = norm_mod(x, nw, sh, sc, "ffn_norm")
    gp, up, a = ffn_up(h, wt_gu)
    y, xn = matmul(a, w_dn, "nn", BF, "ffn_down", res=x, gate=g, coef=0.5)
    return xn, (x, h, gp, up, a, y)


def _ffn_bwd(dxo, saved, nw, sc, g, wt_gu, w_dn):
    x, h, gp, up, a, y = saved
    dy, acc1 = resid_gate_bwd(dxo, y, g, 0.5, "ffn_gate_bwd")
    d_wdn = matmul(a, dy, "tn", BF, "ffn_down_wgrad")
    dg, du = ffn_down_dgrad(dy, w_dn, gp, up)
    d_wt = ffn_up_wgrad(dg, du, h)
    dh = ffn_up_dgrad(dg, du, wt_gu)
    dx, acc2 = norm_mod_bwd(x, dh, dxo, nw, sc, "ffn_norm_bwd")
    return dx, d_wt, d_wdn, (acc2[0], acc2[1], acc1[0]), acc2[2]


def _group_layout(a):
    L = a.shape[0]
    return a[:, :SSM_HEADS].reshape(L, SSM_GROUPS, _HPG).transpose(1, 0, 2)


def _ungroup_layout(a):
    L = a.shape[1]
    return jnp.pad(a.transpose(1, 0, 2).reshape(L, SSM_HEADS), ((0, 0), (0, 128 - SSM_HEADS)))


def _pad_row(vec, n=128):
    return jnp.pad(vec.reshape(1, -1), ((0, 0), (0, n - vec.shape[-1])))


def _mamba_fwd(x, nw, sh, sc, g, p):
    h = norm_mod(x, nw, sh, sc, "mix_norm")
    zx = matmul(h, p["w_in_t"], "nt", F32, "ssm_in")
    xc = conv_fwd(zx, p["conv_w"], p["conv_b"])
    dt, acs = dt_prep(zx, p["dt_bias"], p["a_log"])
    dt_g, acs_g = _group_layout(dt), _group_layout(acs)
    acs_t = acs_g.transpose(0, 2, 1)
    y, pst = ssd_fwd(xc, dt_g, acs_g, acs_t, p["d_exp"])
    yn = gate_norm(y, zx, p["norm_w"])
    yo, xn = matmul(yn, p["w_out"], "nn", BF, "ssm_out", res=x, gate=g, coef=1.0)
    return xn, (x, h, zx, xc, dt, dt_g, acs_g, acs_t, y, pst, yn, yo)


def _mamba_bwd(dxo, saved, nw, sc, g, p):
    x, h, zx, xc, dt, dt_g, acs_g, acs_t, y, pst, yn, yo = saved
    dyo, acc1 = resid_gate_bwd(dxo, yo, g, 1.0, "mix_gate_bwd")
    d_wout = matmul(yn, dyo, "tn", BF, "ssm_out_wgrad")
    dyn = matmul(dyo, p["w_out"], "nt", F32, "ssm_out_dgrad")
    dy, dz, accn = gate_norm_bwd(dyn, y, zx, p["norm_w"])
    dxs, dB, dC, ddt_g, da_g, dd = ssd_bwd(dy, xc, dt_g, acs_g, acs_t, pst, p["d_exp"])
    dxc = jnp.concatenate([dxs, dB, dC], axis=1)
    ds = conv_bwd_act(dxc, zx, p["conv_w"], p["conv_b"])
    du, accc = conv_bwd(ds, zx, p["conv_w"])
    draw, accdt = dt_bwd(_ungroup_layout(ddt_g), _ungroup_layout(da_g), dt, zx, p["dt_bias"], p["a_log"])
    dzx = jnp.concatenate([dz, du, draw], axis=1)
    d_win = matmul(dzx, h, "tn", BF, "ssm_in_wgrad")[:IN_PROJ]
    dh = matmul(dzx, p["w_in_t"], "nn", F32, "ssm_in_dgrad")
    dx, acc2 = norm_mod_bwd(x, dh, dxo, nw, sc, "mix_norm_bwd")
    small = dict(conv_w=accc[:CONV_WIDTH], conv_b=accc[CONV_WIDTH], dt_bias=accdt[0, :SSM_HEADS],
                 a_log=accdt[1, :SSM_HEADS], d=dd.reshape(SSM_HEADS, SSM_HEADDIM).sum(-1), norm_w=accn[0])
    return dx, d_win, d_wout, (acc2[0], acc2[1], acc1[0]), acc2[2], small


def _attn_layer_fwd(x, nw, sh, sc, g, p, kv):
    h = norm_mod(x, nw, sh, sc, "mix_norm")
    q = matmul(h, p["w_q"], "nn", F32, "attn_q", bias=p["b_q"])
    o = attn_fwd(q, kv, p["sinks"])
    yo, xn = matmul(o, p["w_o"], "nn", BF, "attn_o", bias=p["b_o"], res=x, gate=g, coef=1.0)
    return xn, (x, h, q, o, yo)


def _attn_layer_bwd(dxo, saved, nw, sc, g, p, kv):
    x, h, q, o, yo = saved
    dyo, acc1 = resid_gate_bwd(dxo, yo, g, 1.0, "mix_gate_bwd")
    d_wo = matmul(o, dyo, "tn", BF, "attn_o_wgrad")
    do = matmul(dyo, p["w_o"], "nt", F32, "attn_o_dgrad")
    dq, dkv_c, dkv_p, acca = attn_bwd(q, kv, do, p["sinks"])
    d_wq = matmul(h, dq, "tn", BF, "attn_q_wgrad")
    dh = matmul(dq, p["w_q"], "nt", F32, "attn_q_dgrad")
    dx, acc2 = norm_mod_bwd(x, dh, dxo, nw, sc, "mix_norm_bwd")
    small = dict(b_q=acca[0], sinks=acca[1, :ATT_HEADS], b_o=acc1[1])
    return dx, d_wq, d_wo, (acc2[0], acc2[1], acc1[0]), acc2[2], small, (dkv_c, dkv_p)


def _pack_rows(pieces):
    rows, spans, off = [], [], 0
    for a in pieces:
        flat = a.reshape(-1).astype(F32)
        n = -(-flat.shape[0] // D_MODEL)
        rows.append(jnp.pad(flat, (0, n * D_MODEL - flat.shape[0])).reshape(n, D_MODEL))
        spans.append((off, a.shape))
        off += n
    pad = -off % 8
    if pad:
        rows.append(jnp.zeros((pad, D_MODEL), F32))
    return jnp.concatenate(rows, axis=0), spans, off + pad


def _unpack_rows(g, spans):
    out = []
    for off, shape in spans:
        size = 1
        for s in shape:
            size *= s
        n = -(-size // D_MODEL)
        out.append(g[:, off:off + n].reshape(N_DEV, n * D_MODEL)[:, :size].reshape((N_DEV,) + tuple(shape)))
    return out


def _unshard_last(g):
    nd = g.ndim
    perm = tuple(range(1, nd - 1)) + (0, nd - 1)
    t = g.transpose(perm)
    return t.reshape(t.shape[:-2] + (N_DEV * g.shape[-1],))


def _shard_last(a, me):
    s = a.shape[-1] // N_DEV
    return lax.dynamic_slice_in_dim(a, me * s, s, axis=a.ndim - 1)


def kernel(x, c, ffn_norm_w, ffn_w_gu, ffn_w_down, mod_w, mod_b, mix_norm_w, ssm_w_in, ssm_conv_w, ssm_conv_b, ssm_dt_bias, ssm_a_log, ssm_d, ssm_norm_w, ssm_w_out, kv_norm_w, kv_mod_w, kv_mod_b, w_kv, b_kv, attn_w_q, attn_b_q, attn_sinks, attn_w_o, attn_b_o, final_norm_w, loss_target, m_ffn_norm_w, m_ffn_w_gu, m_ffn_w_down, m_mod_w, m_mod_b, m_mix_norm_w, m_ssm_w_in, m_ssm_conv_w, m_ssm_conv_b, m_ssm_dt_bias, m_ssm_a_log, m_ssm_d, m_ssm_norm_w, m_ssm_w_out, m_kv_norm_w, m_kv_mod_w, m_kv_mod_b, m_w_kv, m_b_kv, m_attn_w_q, m_attn_b_q, m_attn_sinks, m_attn_w_o, m_attn_b_o, m_final_norm_w, v_ffn_norm_w, v_ffn_w_gu, v_ffn_w_down, v_mod_w, v_mod_b, v_mix_norm_w, v_ssm_w_in, v_ssm_conv_w, v_ssm_conv_b, v_ssm_dt_bias, v_ssm_a_log, v_ssm_d, v_ssm_norm_w, v_ssm_w_out, v_kv_norm_w, v_kv_mod_w, v_kv_mod_b, v_w_kv, v_b_kv, v_attn_w_q, v_attn_b_q, v_attn_sinks, v_attn_w_o, v_attn_b_o, v_final_norm_w):
    D = D_MODEL
    me = 4 * lax.axis_index("x") + 2 * lax.axis_index("y") + lax.axis_index("c")
    xs = x[0]
    target = loss_target[0]
    mod_cols = mod_w.shape[-1]
    kvm_cols = kv_mod_w.shape[-1]

    packed, spans, _ = _pack_rows([c, ffn_norm_w, ssm_conv_w, ssm_conv_b, ssm_norm_w])
    nrow = packed.shape[0]
    g1 = small_all_gather(packed).reshape(N_DEV, nrow, D)
    c_all, fnw_g, cw_g, cb_g, snw_g = _unpack_rows(g1, spans)
    c_all = c_all.reshape(N_DEV, D)
    ffn_nw = _unshard_last(fnw_g)
    conv_w = _unshard_last(cw_g)
    conv_b = _unshard_last(cb_g)
    ssm_nw = _unshard_last(snw_g)

    mod_b_loc = lax.dynamic_slice_in_dim(mod_b, me * mod_cols, mod_cols, axis=1).reshape(DEPTH, 1, mod_cols)
    kvb_loc = lax.dynamic_slice_in_dim(kv_mod_b, me * kvm_cols, kvm_cols, axis=0).reshape(1, 1, kvm_cols)
    modp, c_act = mod_fwd(c_all, mod_w, mod_b_loc, "mod_fwd")
    kvmp, _ = mod_fwd(c_all, kv_mod_w.reshape(1, D, kvm_cols), kvb_loc, "kv_mod_fwd")
    packed2, spans2, _ = _pack_rows([modp, kvmp])
    nrow2 = packed2.shape[0]
    g2 = small_all_gather(packed2).reshape(N_DEV, nrow2, D)
    modp_g, kvmp_g = _unpack_rows(g2, spans2)
    mod_all = modp_g.transpose(1, 2, 0, 3).reshape(DEPTH, N_DEV, N_MOD * D)
    kvm_all = kvmp_g.transpose(1, 2, 0, 3).reshape(N_DEV, 2 * D)
    mod_me = lax.dynamic_index_in_dim(mod_all, me, axis=1, keepdims=False).reshape(DEPTH, N_MOD, 1, D)
    kvm_me = lax.dynamic_index_in_dim(kvm_all, me, axis=0, keepdims=False).reshape(2, 1, D)

    gu_t = jnp.swapaxes(ffn_w_gu, 2, 3)
    win_t = jnp.transpose(ssm_w_in, (2, 0, 1))
    S = gu_t.shape[2]
    s_in = win_t.shape[0]
    r_dn = ffn_w_down.shape[2]
    r_mix = ssm_w_out.shape[1]
    r_at = attn_w_q.shape[1]

    def layer_pack(k):
        arrs = [gu_t[k, 0].astype(BF), gu_t[k, 1].astype(BF), ffn_w_down[k, 0].astype(BF), ffn_w_down[k, 1].astype(BF)]
        if k < N_A:
            arrs += [ssm_w_out[k].astype(BF), win_t[:, k].astype(BF)]
        else:
            arrs += [attn_w_q[k - N_A].astype(BF), attn_w_o[k - N_A].astype(BF)]
        if k == N_A:
            arrs.append(w_kv.astype(BF))
        return arrs

    packs = [layer_pack(k) for k in range(DEPTH)]
    gathered = [None] * DEPTH
    gathered[0] = big_all_gather(packs[0])
    pending, tok_next = xfer_start(packs[1], False, gathered[0][2], "gather_start_1")

    def wt_gu_full(i, j):
        return gathered[i][j].reshape(N_DEV * S, D)

    def w_dn_full(i, j):
        return gathered[i][2 + j].reshape(D_FF, D)

    def mix_rows(i, a):
        return gathered[i][4 + a].reshape(-1, D)

    def mamba_params(j):
        w_in_t = jnp.pad(mix_rows(j, 1), ((0, IN_PROJ_PAD - IN_PROJ), (0, 0)))
        return dict(w_in_t=w_in_t, w_out=mix_rows(j, 0), conv_w=conv_w[j], conv_b=conv_b[j].reshape(1, -1),
                    dt_bias=_pad_row(ssm_dt_bias[j]), a_log=_pad_row(ssm_a_log[j]),
                    d_exp=jnp.repeat(ssm_d[j], SSM_HEADDIM).reshape(SSM_GROUPS, 1, _GW),
                    norm_w=ssm_nw[j].reshape(1, -1))

    def attn_params(j):
        return dict(w_q=mix_rows(N_A + j, 0), w_o=mix_rows(N_A + j, 1),
                    b_q=attn_b_q[j].reshape(1, -1), b_o=attn_b_o[j].reshape(1, -1), sinks=_pad_row(attn_sinks[j]))

    saved = []
    kv = None
    kv_saved = None
    w_kv_full = None
    xcur = xs
    for i in range(DEPTH):
        if i >= 1:
            gathered[i] = xfer_wait(pending, xcur, "gather_wait_%d" % i)
            if i + 1 < DEPTH:
                pending, tok_next = xfer_start(packs[i + 1], False, xcur, "gather_start_%d" % (i + 1))
        md = mod_me[i]
        if i + 1 < DEPTH:
            md = md + tok_next[0, 0]
        if i == N_A:
            w_kv_full = gathered[N_A][6].reshape(D, KV_DIM)
            h_kv = norm_mod(xcur, kv_norm_w.reshape(1, D), kvm_me[0], kvm_me[1], "kv_norm")
            kv = matmul(h_kv, w_kv_full, "nn", F32, "kv_proj", bias=b_kv.reshape(1, -1))
            kv_saved = (xcur, h_kv)
        x1, s1 = _ffn_fwd(xcur, ffn_nw[i, 0].reshape(1, D), md[0], md[1], md[2], wt_gu_full(i, 0), w_dn_full(i, 0))
        if i < N_A:
            pm = mamba_params(i)
            x2, s2 = _mamba_fwd(x1, mix_norm_w[i].reshape(1, D), md[3], md[4], md[5], pm)
        else:
            pm = attn_params(i - N_A)
            x2, s2 = _attn_layer_fwd(x1, mix_norm_w[i].reshape(1, D), md[3], md[4], md[5], pm, kv)
        x3, s3 = _ffn_fwd(x2, ffn_nw[i, 1].reshape(1, D), md[6], md[7], md[8], wt_gu_full(i, 1), w_dn_full(i, 1))
        saved.append((s1, s2, s3, pm))
        xcur = x3

    dx, accf = final_loss(xcur, final_norm_w.reshape(1, D), target)
    d_mod = [None] * DEPTH
    d_ffn_nw = [[None, None] for _ in range(DEPTH)]
    d_mix_nw = [None] * DEPTH
    sm_m, sm_a = [None] * N_A, [None] * N_A
    kv_parts = [None] * N_A
    d_kvm = d_kv_nw = d_bkv = None
    exchanges = []
    tok = None

    def send(arrs, tag):
        handle, t = xfer_start(arrs, True, dx, "exch_start_%s" % tag)
        exchanges.append((handle, tag))
        return t

    def ffn_slabs(d_wt, d_wdn):
        return [d_wt.reshape(N_DEV, S, D), d_wdn.reshape(N_DEV, r_dn, D)]

    for i in reversed(range(DEPTH)):
        md = mod_me[i]
        s1, s2, s3, pm = saved[i]
        g2 = md[8] if tok is None else md[8] + tok[0, 0]
        dx, d_wt, d_wdn, m2, d_ffn_nw[i][1] = _ffn_bwd(
            dx, s3, ffn_nw[i, 1].reshape(1, D), md[7], g2, wt_gu_full(i, 1), w_dn_full(i, 1))
        tok = send(ffn_slabs(d_wt, d_wdn), "f%d1" % i)
        gm = md[5] + tok[0, 0]
        if i < N_A:
            dx, d_in, d_out, mm_, d_mix_nw[i], sm_m[i] = _mamba_bwd(dx, s2, mix_norm_w[i].reshape(1, D), md[4], gm, pm)
            tok = send([d_in.reshape(N_DEV, s_in, D), d_out.reshape(N_DEV, r_mix, D)], "m%d" % i)
        else:
            j = i - N_A
            dx, d_q, d_o, mm_, d_mix_nw[i], sm_a[j], kv_parts[j] = _attn_layer_bwd(
                dx, s2, mix_norm_w[i].reshape(1, D), md[4], gm, pm, kv)
            tok = send([d_q.reshape(N_DEV, r_at, D), d_o.reshape(N_DEV, r_at, D)], "m%d" % i)
        g1 = md[2] + tok[0, 0]
        dx, d_wt, d_wdn, m1, d_ffn_nw[i][0] = _ffn_bwd(
            dx, s1, ffn_nw[i, 0].reshape(1, D), md[1], g1, wt_gu_full(i, 0), w_dn_full(i, 0))
        d_mod[i] = jnp.concatenate(list(m1) + list(mm_) + list(m2), axis=0)
        last = ffn_slabs(d_wt, d_wdn)
        if i == N_A:
            x_kv, h_kv = kv_saved
            dkv, acck = kv_grad_combine(kv_parts)
            d_bkv = acck[0]
            d_kv_w = matmul(h_kv, dkv, "tn", BF, "kv_wgrad")
            dh_kv = matmul(dkv, w_kv_full, "nt", F32, "kv_dgrad")
            dx, acc_kv = norm_mod_bwd(x_kv, dh_kv, dx, kv_norm_w.reshape(1, D), kvm_me[1], "kv_norm_bwd")
            d_kvm = jnp.concatenate([acc_kv[0], acc_kv[1]], axis=0)
            d_kv_nw = acc_kv[2]
            last.append(d_kv_w.reshape(N_DEV, -1, KV_DIM))
        tok = send(last, "f%d0" % i)
    grad_x = dx.reshape(x.shape)

    small_list = [
        jnp.stack(d_mod, 0), d_kvm,
        jnp.stack([jnp.stack(r, 0) for r in d_ffn_nw], 0),
        jnp.stack(d_mix_nw, 0),
        jnp.stack([s["conv_w"] for s in sm_m], 0), jnp.stack([s["conv_b"] for s in sm_m], 0),
        jnp.stack([s["dt_bias"] for s in sm_m], 0), jnp.stack([s["a_log"] for s in sm_m], 0),
        jnp.stack([s["d"] for s in sm_m], 0), jnp.stack([s["norm_w"] for s in sm_m], 0),
        d_kv_nw, d_bkv,
        jnp.stack([s["b_q"] for s in sm_a], 0), jnp.stack([s["sinks"] for s in sm_a], 0),
        jnp.stack([s["b_o"] for s in sm_a], 0), accf[0], accf[1],
    ]
    packed3, spans3, _ = _pack_rows(small_list)
    nrow3 = packed3.shape[0]
    g3 = small_all_gather(packed3).reshape(N_DEV, nrow3, D)
    (p_mod, p_kvm, p_fnw, p_mnw, p_cw, p_cb, p_dtb, p_al, p_d, p_snw, p_kvnw, p_bkv, p_bq, p_sk, p_bo, p_fin,
     p_loss) = _unpack_rows(g3, spans3)

    loss = 0.5 / D * jnp.sum(p_loss)

    c_act_t = c_act.T
    dmod_loc = _shard_last(p_mod, me).transpose(1, 0, 2)
    dkvm_loc = _shard_last(p_kvm, me).reshape(1, N_DEV, kvm_cols)
    gp_mod_w = mod_wgrad(c_act_t, dmod_loc, "mod_wgrad")
    gp_kvm_w = mod_wgrad(c_act_t, dkvm_loc, "kv_mod_wgrad")[0]

    def as_parts_single(a):
        return a[None]

    def upd(name, parts, w, m, v):
        shp = w.shape
        c_last = shp[-1]
        out = adamw(parts.reshape(parts.shape[0], -1, c_last), w.reshape(-1, c_last), m.reshape(-1, c_last),
                    v.reshape(-1, c_last), "adamw_" + name)
        return tuple(o.reshape(shp) for o in out)

    views = {
        "ffn_w_gu": [jnp.swapaxes(t, 2, 3).reshape(-1, D) for t in (ffn_w_gu, m_ffn_w_gu, v_ffn_w_gu)],
        "ffn_w_down": [t.reshape(-1, D) for t in (ffn_w_down, m_ffn_w_down, v_ffn_w_down)],
        "ssm_w_out": [t.reshape(-1, D) for t in (ssm_w_out, m_ssm_w_out, v_ssm_w_out)],
        "attn_w_q": [t.reshape(-1, D) for t in (attn_w_q, m_attn_w_q, v_attn_w_q)],
        "attn_w_o": [t.reshape(-1, D) for t in (attn_w_o, m_attn_w_o, v_attn_w_o)],
    }
    filled = {k: None for k in views}

    def upd_rows(name, parts, row0):
        w, m, v = views[name]
        filled[name] = adamw(parts, w, m, v, "adamw_" + name, row0=row0, prev=filled[name])
        return filled[name][3]

    chain = dx
    r_in_parts = [None] * N_A
    r_kv = None
    for handle, tag in exchanges:
        got = xfer_wait(handle, chain, "exch_wait_%s" % tag)
        i = int(tag[1])
        if tag[0] == "f":
            jf = int(tag[2])
            upd_rows("ffn_w_gu", got[0], (2 * i + jf) * S)
            chain = upd_rows("ffn_w_down", got[1], (2 * i + jf) * r_dn)
            if len(got) > 2:
                r_kv = got[2]
        elif i < N_A:
            r_in_parts[i] = got[0]
            chain = upd_rows("ssm_w_out", got[1], i * r_mix)
        else:
            upd_rows("attn_w_q", got[0], (i - N_A) * r_at)
            chain = upd_rows("attn_w_o", got[1], (i - N_A) * r_at)

    res = {}
    res["ffn_w_gu"] = tuple(jnp.swapaxes(t.reshape(gu_t.shape), 2, 3) for t in filled["ffn_w_gu"])
    res["ffn_w_down"] = tuple(t.reshape(ffn_w_down.shape) for t in filled["ffn_w_down"])
    res["ssm_w_out"] = tuple(t.reshape(ssm_w_out.shape) for t in filled["ssm_w_out"])
    res["attn_w_q"] = tuple(t.reshape(attn_w_q.shape) for t in filled["attn_w_q"])
    res["attn_w_o"] = tuple(t.reshape(attn_w_o.shape) for t in filled["attn_w_o"])
    win_out = adamw(jnp.stack(r_in_parts, axis=2).reshape(N_DEV, s_in * N_A, D),
                    *[jnp.transpose(t, (2, 0, 1)).reshape(-1, D) for t in (ssm_w_in, m_ssm_w_in, v_ssm_w_in)],
                    "adamw_ssm_w_in")
    res["ssm_w_in"] = tuple(jnp.transpose(t.reshape(win_t.shape), (1, 2, 0)) for t in win_out)
    res["w_kv"] = upd("w_kv", r_kv, w_kv, m_w_kv, v_w_kv)

    res["ffn_norm_w"] = upd("ffn_norm_w", _shard_last(p_fnw, me), ffn_norm_w, m_ffn_norm_w, v_ffn_norm_w)
    res["mod_w"] = upd("mod_w", as_parts_single(gp_mod_w), mod_w, m_mod_w, v_mod_w)
    res["mod_b"] = upd("mod_b", p_mod, mod_b, m_mod_b, v_mod_b)
    res["mix_norm_w"] = upd("mix_norm_w", p_mnw, mix_norm_w, m_mix_norm_w, v_mix_norm_w)
    res["ssm_conv_w"] = upd("ssm_conv_w", _shard_last(p_cw, me), ssm_conv_w, m_ssm_conv_w, v_ssm_conv_w)
    res["ssm_conv_b"] = upd("ssm_conv_b", _shard_last(p_cb, me), ssm_conv_b, m_ssm_conv_b, v_ssm_conv_b)
    res["ssm_dt_bias"] = upd("ssm_dt_bias", p_dtb, ssm_dt_bias, m_ssm_dt_bias, v_ssm_dt_bias)
    res["ssm_a_log"] = upd("ssm_a_log", p_al, ssm_a_log, m_ssm_a_log, v_ssm_a_log)
    res["ssm_d"] = upd("ssm_d", p_d, ssm_d, m_ssm_d, v_ssm_d)
    res["ssm_norm_w"] = upd("ssm_norm_w", _shard_last(p_snw, me), ssm_norm_w, m_ssm_norm_w, v_ssm_norm_w)
    res["kv_norm_w"] = upd("kv_norm_w", p_kvnw.reshape(N_DEV, 1, D), kv_norm_w.reshape(1, D),
                           m_kv_norm_w.reshape(1, D), v_kv_norm_w.reshape(1, D))
    res["kv_mod_w"] = upd("kv_mod_w", as_parts_single(gp_kvm_w), kv_mod_w, m_kv_mod_w, v_kv_mod_w)
    res["kv_mod_b"] = upd("kv_mod_b", p_kvm.reshape(N_DEV, 1, 2 * D), kv_mod_b.reshape(1, -1),
                          m_kv_mod_b.reshape(1, -1), v_kv_mod_b.reshape(1, -1))
    res["b_kv"] = upd("b_kv", p_bkv.reshape(N_DEV, 1, KV_DIM), b_kv.reshape(1, -1), m_b_kv.reshape(1, -1),
                      v_b_kv.reshape(1, -1))
    res["attn_b_q"] = upd("attn_b_q", p_bq, attn_b_q, m_attn_b_q, v_attn_b_q)
    res["attn_sinks"] = upd("attn_sinks", p_sk, attn_sinks, m_attn_sinks, v_attn_sinks)
    res["attn_b_o"] = upd("attn_b_o", p_bo, attn_b_o, m_attn_b_o, v_attn_b_o)
    res["final_norm_w"] = upd("final_norm_w", p_fin.reshape(N_DEV, 1, D), final_norm_w.reshape(1, D),
                              m_final_norm_w.reshape(1, D), v_final_norm_w.reshape(1, D))

    names = ["ffn_norm_w", "ffn_w_gu", "ffn_w_down", "mod_w", "mod_b", "mix_norm_w", "ssm_w_in", "ssm_conv_w",
             "ssm_conv_b", "ssm_dt_bias", "ssm_a_log", "ssm_d", "ssm_norm_w", "ssm_w_out", "kv_norm_w", "kv_mod_w",
             "kv_mod_b", "w_kv", "b_kv", "attn_w_q", "attn_b_q", "attn_sinks", "attn_w_o", "attn_b_o", "final_norm_w"]
    vec_shapes = {"kv_norm_w": (D,), "kv_mod_b": (2 * D,), "b_kv": (KV_DIM,), "final_norm_w": (D,)}
    outs = [loss, grad_x]
    for k in range(4):
        for nme in names:
            t = res[nme][k]
            if nme in vec_shapes:
                t = t.reshape(vec_shapes[nme])
            outs.append(t)
    return tuple(outs)
```

```python
import functools

import jax
import jax.numpy as jnp
from jax import lax
from jax.experimental import pallas as pl
from jax.experimental.pallas import tpu as pltpu

F32 = jnp.float32
BF = jnp.bfloat16
MESH = pl.DeviceIdType.MESH

N_DEV = 8
D_MODEL = 1024
DEPTH = 4
N_A = 2
EPS = 1e-5
N_MOD = 9
D_FF = 2816
D_INNER = 2048
SSM_HEADDIM = 64
SSM_HEADS = 32
SSM_GROUPS = 8
SSM_STATE = 128
CONV_WIDTH = 4
CHUNK = 512
CONV_DIM = D_INNER + 2 * SSM_GROUPS * SSM_STATE
IN_PROJ = D_INNER + CONV_DIM + SSM_HEADS
IN_PROJ_PAD = D_INNER + CONV_DIM + 128
ATT_HEADS = 16
KV_HEADS = 4
HEAD_DIM = 64
WINDOW = 128
KV_DIM = 2 * KV_HEADS * HEAD_DIM

ADAM_LR = 0.001
ADAM_B1 = 0.9
ADAM_B2 = 0.999
ADAM_EPS = 1e-08
ADAM_WD = 0.01
ADAM_STEP = 10

VMEM_LIMIT = 48 * 2 ** 20
ADAMW_VMEM_BUDGET = 24 * 2 ** 20
NEG = -1e30


def _call(body, name, grid, in_specs, out_specs, out_shape, scratch=()):
    return pl.pallas_call(
        body, name=name, grid=grid, in_specs=in_specs, out_specs=out_specs, out_shape=out_shape,
        scratch_shapes=list(scratch),
        compiler_params=pltpu.CompilerParams(vmem_limit_bytes=VMEM_LIMIT))


def _tile(n, cap):
    t = (cap // 128) * 128
    while t >= 128:
        if n % t == 0:
            return t
        t -= 128
    return n


def _sds(shape, dtype):
    return jax.ShapeDtypeStruct(shape, dtype)


def _sigmoid(v):
    return 1.0 / (1.0 + jnp.exp(-v))


def _dot(a, b, dims):
    return lax.dot_general(a, b, (dims, ((), ())), preferred_element_type=F32)


def _dot_nn(a, b):
    return _dot(a.astype(BF), b.astype(BF), ((1,), (0,)))


def _dot_nt(a, b):
    return _dot(a.astype(BF), b.astype(BF), ((1,), (1,)))


def _dot_tn(a, b):
    return _dot(a.astype(BF), b.astype(BF), ((0,), (0,)))


def matmul(a, b, mode, out_dtype, name, bias=None, res=None, gate=None, coef=1.0):
    if mode == "nn":
        (M, K), (_, N) = a.shape, b.shape
    elif mode == "nt":
        (M, K), (N, _) = a.shape, b.shape
    else:
        (K, M), (_, N) = a.shape, b.shape
    cap_n = 512 if K > 4096 else 1024
    tm = _tile(M, 1024 if (mode == "tn" or K <= D_FF) else 512)
    tn = _tile(N, cap_n)
    if mode != "tn" and tm * tn > 1024 * 896:
        tn = _tile(N, 512)
    if mode == "nn":
        a_spec = pl.BlockSpec((tm, K), lambda i, j: (i, 0))
        b_spec = pl.BlockSpec((K, tn), lambda i, j: (0, j))
        fn = _dot_nn
    elif mode == "nt":
        a_spec = pl.BlockSpec((tm, K), lambda i, j: (i, 0))
        b_spec = pl.BlockSpec((tn, K), lambda i, j: (j, 0))
        fn = _dot_nt
    else:
        a_spec = pl.BlockSpec((K, tm), lambda i, j: (0, i))
        b_spec = pl.BlockSpec((K, tn), lambda i, j: (0, j))
        fn = _dot_tn
    has_bias, has_res = bias is not None, res is not None
    o_spec = pl.BlockSpec((tm, tn), lambda i, j: (i, j))
    v_spec = pl.BlockSpec((1, tn), lambda i, j: (0, j))
    in_specs, args = [a_spec, b_spec], [a, b]
    if has_bias:
        in_specs.append(v_spec)
        args.append(bias)
    if has_res:
        in_specs += [o_spec, v_spec]
        args += [res, gate]

    def body(*refs):
        a_ref, b_ref = refs[0], refs[1]
        k = 2
        y = fn(a_ref[...], b_ref[...])
        if has_bias:
            y = y + refs[k][...]
            k += 1
        if has_res:
            res_ref, gate_ref = refs[k], refs[k + 1]
            refs[k + 2][...] = y.astype(out_dtype)
            refs[k + 3][...] = res_ref[...] + coef * gate_ref[...] * y
        else:
            refs[k][...] = y.astype(out_dtype)

    if has_res:
        out_shape = (_sds((M, N), out_dtype), _sds((M, N), F32))
        out_specs = (o_spec, o_spec)
    else:
        out_shape = _sds((M, N), out_dtype)
        out_specs = o_spec
    return _call(body, name, (M // tm, N // tn), in_specs, out_specs, out_shape)(*args)


def norm_mod(x, nw, sh, sc, name):
    L, D = x.shape
    tm = _tile(L, 512)

    def body(x_ref, nw_ref, sh_ref, sc_ref, h_ref):
        xf = x_ref[...]
        r = lax.rsqrt(jnp.mean(xf * xf, axis=-1, keepdims=True) + EPS)
        n = xf * r * nw_ref[...]
        h_ref[...] = (n * (1.0 + sc_ref[...]) + sh_ref[...]).astype(BF)

    row = pl.BlockSpec((tm, D), lambda i: (i, 0))
    vec = pl.BlockSpec((1, D), lambda i: (0, 0))
    return _call(body, name, (L // tm,), [row, vec, vec, vec], row, _sds((L, D), BF))(x, nw, sh, sc)


def norm_mod_bwd(x, dh, dres, nw, sc, name):
    L, D = x.shape
    tm = _tile(L, 512)

    def body(x_ref, dh_ref, dres_ref, nw_ref, sc_ref, dx_ref, acc_ref):
        @pl.when(pl.program_id(0) == 0)
        def _():
            acc_ref[...] = jnp.zeros_like(acc_ref)

        xf = x_ref[...]
        dhf = dh_ref[...].astype(F32)
        r = lax.rsqrt(jnp.mean(xf * xf, axis=-1, keepdims=True) + EPS)
        xhat = xf * r
        nwv = nw_ref[...]
        dn = dhf * (1.0 + sc_ref[...])
        dxhat = dn * nwv
        proj = jnp.mean(dxhat * xhat, axis=-1, keepdims=True)
        dx_ref[...] = dres_ref[...] + r * (dxhat - xhat * proj)
        acc_ref[0:1, :] += jnp.sum(dhf, axis=0, keepdims=True)
        acc_ref[1:2, :] += jnp.sum(dhf * xhat * nwv, axis=0, keepdims=True)
        acc_ref[2:3, :] += jnp.sum(dn * xhat, axis=0, keepdims=True)

    row = pl.BlockSpec((tm, D), lambda i: (i, 0))
    vec = pl.BlockSpec((1, D), lambda i: (0, 0))
    acc = pl.BlockSpec((8, D), lambda i: (0, 0))
    return _call(body, name, (L // tm,), [row, row, row, vec, vec], (row, acc),
                 (_sds((L, D), F32), _sds((8, D), F32)))(x, dh, dres, nw, sc)


def final_loss(x, nw, target):
    L, D = x.shape
    tm = _tile(L, 512)

    def body(x_ref, nw_ref, t_ref, dx_ref, acc_ref):
        @pl.when(pl.program_id(0) == 0)
        def _():
            acc_ref[...] = jnp.zeros_like(acc_ref)

        xf = x_ref[...]
        r = lax.rsqrt(jnp.mean(xf * xf, axis=-1, keepdims=True) + EPS)
        xhat = xf * r
        nwv = nw_ref[...]
        err = xhat * nwv - t_ref[...]
        dy = err * (1.0 / D)
        dxhat = dy * nwv
        proj = jnp.mean(dxhat * xhat, axis=-1, keepdims=True)
        dx_ref[...] = r * (dxhat - xhat * proj)
        acc_ref[0:1, :] += jnp.sum(dy * xhat, axis=0, keepdims=True)
        acc_ref[1:2, :] += jnp.sum(err * err, axis=0, keepdims=True)

    row = pl.BlockSpec((tm, D), lambda i: (i, 0))
    vec = pl.BlockSpec((1, D), lambda i: (0, 0))
    acc = pl.BlockSpec((8, D), lambda i: (0, 0))
    return _call(body, "final_loss", (L // tm,), [row, vec, row], (row, acc),
                 (_sds((L, D), F32), _sds((8, D), F32)))(x, nw, target)


def resid_gate_bwd(dxo, y, gate, coef, name):
    L, D = dxo.shape
    tm = _tile(L, 512)

    def body(dxo_ref, y_ref, g_ref, dy_ref, acc_ref):
        @pl.when(pl.program_id(0) == 0)
        def _():
            acc_ref[...] = jnp.zeros_like(acc_ref)

        d = dxo_ref[...]
        dy = coef * g_ref[...] * d
        dy_ref[...] = dy.astype(BF)
        acc_ref[0:1, :] += coef * jnp.sum(d * y_ref[...].astype(F32), axis=0, keepdims=True)
        acc_ref[1:2, :] += jnp.sum(dy, axis=0, keepdims=True)

    row = pl.BlockSpec((tm, D), lambda i: (i, 0))
    vec = pl.BlockSpec((1, D), lambda i: (0, 0))
    acc = pl.BlockSpec((8, D), lambda i: (0, 0))
    return _call(body, name, (L // tm,), [row, row, vec], (row, acc),
                 (_sds((L, D), BF), _sds((8, D), F32)))(dxo, y, gate)


def ffn_up(h, wt):
    L, D = h.shape
    F = wt.shape[0] // 2
    tm, tn = _tile(L, 2048), _tile(F, 256)
    nj = F // tn

    def body(h_ref, wg_ref, wu_ref, g_ref, u_ref, a_ref):
        hv = h_ref[...]
        g = _dot_nt(hv, wg_ref[...])
        u = _dot_nt(hv, wu_ref[...])
        g_ref[...] = g.astype(BF)
        u_ref[...] = u.astype(BF)
        a_ref[...] = (g * _sigmoid(g) * u).astype(BF)

    o = pl.BlockSpec((tm, tn), lambda i, n: (i, n))
    return _call(body, "ffn_up", (L // tm, nj),
                 [pl.BlockSpec((tm, D), lambda i, n: (i, 0)),
                  pl.BlockSpec((tn, D), lambda i, n: (n, 0)),
                  pl.BlockSpec((tn, D), lambda i, n: (n + nj, 0))],
                 (o, o, o), tuple(_sds((L, F), BF) for _ in range(3)))(h, wt, wt)


def ffn_down_dgrad(dy, wd, g, u):
    L, D = dy.shape
    F = wd.shape[0]
    tm, tn = _tile(L, 2048), _tile(F, 256)

    def body(dy_ref, w_ref, g_ref, u_ref, dg_ref, du_ref):
        da = _dot_nt(dy_ref[...], w_ref[...])
        gv = g_ref[...].astype(F32)
        uv = u_ref[...].astype(F32)
        s = _sigmoid(gv)
        dg_ref[...] = (da * uv * s * (1.0 + gv * (1.0 - s))).astype(BF)
        du_ref[...] = (da * gv * s).astype(BF)

    o = pl.BlockSpec((tm, tn), lambda i, n: (i, n))
    return _call(body, "ffn_down_dgrad", (L // tm, F // tn),
                 [pl.BlockSpec((tm, D), lambda i, n: (i, 0)), pl.BlockSpec((tn, D), lambda i, n: (n, 0)), o, o],
                 (o, o), (_sds((L, F), BF), _sds((L, F), BF)))(dy, wd, g, u)


def ffn_up_wgrad(dg, du, h):
    L, F = dg.shape
    D = h.shape[1]
    tm = _tile(F, 256)
    nblk = F // tm

    def half(d, off, prev):
        def body(d_ref, h_ref, *rest):
            rest[-1][...] = _dot_tn(d_ref[...], h_ref[...]).astype(BF)

        in_specs = [pl.BlockSpec((L, tm), lambda i: (0, i)), pl.BlockSpec((L, D), lambda i: (0, 0))]
        args = [d, h]
        aliases = {}
        if prev is not None:
            in_specs.append(pl.BlockSpec(memory_space=pl.ANY))
            args.append(prev)
            aliases = {2: 0}
        return pl.pallas_call(
            body, name="ffn_up_wgrad", grid=(nblk,), in_specs=in_specs,
            out_specs=pl.BlockSpec((tm, D), lambda i: (i + off * nblk, 0)),
            out_shape=_sds((2 * F, D), BF), input_output_aliases=aliases,
            compiler_params=pltpu.CompilerParams(vmem_limit_bytes=VMEM_LIMIT))(*args)

    return half(du, 1, half(dg, 0, None))


def ffn_up_dgrad(dg, du, wt):
    L, F = dg.shape
    D = wt.shape[1]
    tm, tn = _tile(L, 1024), _tile(D, 512)

    def body(dg_ref, du_ref, wg_ref, wu_ref, o_ref):
        o_ref[...] = _dot_nn(dg_ref[...], wg_ref[...]) + _dot_nn(du_ref[...], wu_ref[...])

    a = pl.BlockSpec((tm, F), lambda i, n: (i, 0))
    return _call(body, "ffn_up_dgrad", (L // tm, D // tn),
                 [a, a, pl.BlockSpec((F, tn), lambda i, n: (0, n)), pl.BlockSpec((F, tn), lambda i, n: (1, n))],
                 pl.BlockSpec((tm, tn), lambda i, n: (i, n)), _sds((L, D), F32))(dg, du, wt, wt)


def _shift_rows(cur, other, k, down):
    n = cur.shape[0]
    rows = lax.broadcasted_iota(jnp.int32, cur.shape, 0)
    if down:
        return jnp.where(rows < k, pltpu.roll(other, k, 0), pltpu.roll(cur, k, 0))
    return jnp.where(rows >= n - k, pltpu.roll(other, n - k, 0), pltpu.roll(cur, n - k, 0))


def _conv_pre(cur, prev, w_ref, b_ref):
    s = cur * w_ref[CONV_WIDTH - 1:CONV_WIDTH, :] + b_ref[...]
    for k in range(1, CONV_WIDTH):
        s = s + _shift_rows(cur, prev, k, True) * w_ref[CONV_WIDTH - 1 - k:CONV_WIDTH - k, :]
    return s


_XBC_COL0 = D_INNER // 512


def conv_fwd(zx, w, b):
    L = zx.shape[0]
    tm, tc = _tile(L, 256), 512

    def body(cur_ref, prev_ref, w_ref, b_ref, o_ref):
        cur = cur_ref[...]
        prev = jnp.where(pl.program_id(1) > 0, prev_ref[...], 0.0)
        s = _conv_pre(cur, prev, w_ref, b_ref)
        o_ref[...] = s * _sigmoid(s)

    return _call(body, "conv_fwd", (CONV_DIM // tc, L // tm),
                 [pl.BlockSpec((tm, tc), lambda j, i: (i, _XBC_COL0 + j)),
                  pl.BlockSpec((tm, tc), lambda j, i: (jnp.maximum(i - 1, 0), _XBC_COL0 + j)),
                  pl.BlockSpec((CONV_WIDTH, tc), lambda j, i: (0, j)),
                  pl.BlockSpec((1, tc), lambda j, i: (0, j))],
                 pl.BlockSpec((tm, tc), lambda j, i: (i, j)), _sds((L, CONV_DIM), F32))(zx, zx, w, b)


def conv_bwd_act(dxc, zx, w, b):
    L = zx.shape[0]
    tm, tc = _tile(L, 256), 512

    def body(d_ref, cur_ref, prev_ref, w_ref, b_ref, o_ref):
        cur = cur_ref[...]
        prev = jnp.where(pl.program_id(1) > 0, prev_ref[...], 0.0)
        s = _conv_pre(cur, prev, w_ref, b_ref)
        sg = _sigmoid(s)
        o_ref[...] = d_ref[...] * sg * (1.0 + s * (1.0 - sg))

    return _call(body, "conv_bwd_act", (CONV_DIM // tc, L // tm),
                 [pl.BlockSpec((tm, tc), lambda j, i: (i, j)),
                  pl.BlockSpec((tm, tc), lambda j, i: (i, _XBC_COL0 + j)),
                  pl.BlockSpec((tm, tc), lambda j, i: (jnp.maximum(i - 1, 0), _XBC_COL0 + j)),
                  pl.BlockSpec((CONV_WIDTH, tc), lambda j, i: (0, j)),
                  pl.BlockSpec((1, tc), lambda j, i: (0, j))],
                 pl.BlockSpec((tm, tc), lambda j, i: (i, j)), _sds((L, CONV_DIM), F32))(dxc, zx, zx, w, b)


def conv_bwd(ds, zx, w):
    L = zx.shape[0]
    tm, tc = _tile(L, 256), 512
    nblk = L // tm

    def body(ds_ref, dsn_ref, cur_ref, prev_ref, w_ref, du_ref, acc_ref):
        i = pl.program_id(1)

        @pl.when(i == 0)
        def _():
            acc_ref[...] = jnp.zeros_like(acc_ref)

        ds_c = ds_ref[...]
        ds_n = jnp.where(i < nblk - 1, dsn_ref[...], 0.0)
        cur = cur_ref[...]
        prev = jnp.where(i > 0, prev_ref[...], 0.0)
        du = ds_c * w_ref[CONV_WIDTH - 1:CONV_WIDTH, :]
        acc_ref[CONV_WIDTH - 1:CONV_WIDTH, :] += jnp.sum(ds_c * cur, axis=0, keepdims=True)
        for k in range(1, CONV_WIDTH):
            du = du + _shift_rows(ds_c, ds_n, k, False) * w_ref[CONV_WIDTH - 1 - k:CONV_WIDTH - k, :]
            acc_ref[CONV_WIDTH - 1 - k:CONV_WIDTH - k, :] += jnp.sum(
                ds_c * _shift_rows(cur, prev, k, True), axis=0, keepdims=True)
        acc_ref[CONV_WIDTH:CONV_WIDTH + 1, :] += jnp.sum(ds_c, axis=0, keepdims=True)
        du_ref[...] = du.astype(BF)

    return _call(body, "conv_bwd", (CONV_DIM // tc, nblk),
                 [pl.BlockSpec((tm, tc), lambda j, i: (i, j)),
                  pl.BlockSpec((tm, tc), lambda j, i: (jnp.minimum(i + 1, nblk - 1), j)),
                  pl.BlockSpec((tm, tc), lambda j, i: (i, _XBC_COL0 + j)),
                  pl.BlockSpec((tm, tc), lambda j, i: (jnp.maximum(i - 1, 0), _XBC_COL0 + j)),
                  pl.BlockSpec((CONV_WIDTH, tc), lambda j, i: (0, j))],
                 (pl.BlockSpec((tm, tc), lambda j, i: (i, j)), pl.BlockSpec((8, tc), lambda j, i: (0, j))),
                 (_sds((L, CONV_DIM), BF), _sds((8, CONV_DIM), F32)))(ds, ds, zx, zx, w)


_DT_COL = (D_INNER + CONV_DIM) // 128


def dt_prep(zx, bias_pad, alog_pad):
    L = zx.shape[0]

    def body(raw_ref, b_ref, al_ref, dt_ref, acs_ref):
        v = raw_ref[...] + b_ref[...]
        dt = jnp.maximum(v, 0.0) + jnp.log(1.0 + jnp.exp(-jnp.abs(v)))
        dt_ref[...] = dt
        acs = dt * (-jnp.exp(al_ref[...]))
        rows = lax.broadcasted_iota(jnp.int32, acs.shape, 0)
        s = 1
        while s < CHUNK:
            acs = acs + jnp.where(rows >= s, pltpu.roll(acs, s, 0), 0.0)
            s *= 2
        acs_ref[...] = acs

    blk = pl.BlockSpec((CHUNK, 128), lambda i: (i, 0))
    vec = pl.BlockSpec((1, 128), lambda i: (0, 0))
    return _call(body, "dt_prep", (L // CHUNK,),
                 [pl.BlockSpec((CHUNK, 128), lambda i: (i, _DT_COL)), vec, vec], (blk, blk),
                 (_sds((L, 128), F32), _sds((L, 128), F32)))(zx, bias_pad, alog_pad)


def dt_bwd(ddt, da, dt, zx, bias_pad, alog_pad):
    L = zx.shape[0]
    tm = _tile(L, 512)

    def body(ddt_ref, da_ref, dt_ref, raw_ref, b_ref, al_ref, o_ref, acc_ref):
        @pl.when(pl.program_id(0) == 0)
        def _():
            acc_ref[...] = jnp.zeros_like(acc_ref)

        A = -jnp.exp(al_ref[...])
        dav = da_ref[...]
        dd = ddt_ref[...] + dav * A
        draw = dd * _sigmoid(raw_ref[...] + b_ref[...])
        o_ref[...] = draw.astype(BF)
        acc_ref[0:1, :] += jnp.sum(draw, axis=0, keepdims=True)
        acc_ref[1:2, :] += jnp.sum(dav * dt_ref[...], axis=0, keepdims=True) * A

    blk = pl.BlockSpec((tm, 128), lambda i: (i, 0))
    vec = pl.BlockSpec((1, 128), lambda i: (0, 0))
    return _call(body, "dt_bwd", (L // tm,),
                 [blk, blk, blk, pl.BlockSpec((tm, 128), lambda i: (i, _DT_COL)), vec, vec],
                 (blk, pl.BlockSpec((8, 128), lambda i: (0, 0))),
                 (_sds((L, 128), BF), _sds((8, 128), F32)))(ddt, da, dt, zx, bias_pad, alog_pad)


_HPG = SSM_HEADS // SSM_GROUPS
_GW = _HPG * SSM_HEADDIM
_B_COL0 = D_INNER // SSM_STATE
_C_COL0 = (D_INNER + SSM_GROUPS * SSM_STATE) // SSM_STATE


def _ssd_head(x, dtc, ac, ar, r, causal):
    xh = x[:, SSM_HEADDIM * r:SSM_HEADDIM * (r + 1)]
    acol = ac[:, r:r + 1]
    arow = ar[r:r + 1, :]
    alast = ar[r:r + 1, CHUNK - 1:CHUNK]
    lm = jnp.exp(jnp.where(causal, acol - arow, NEG))
    return xh, xh * dtc[:, r:r + 1], acol, alast, lm


def ssd_fwd(xc, dt_g, acs_g, acsT_g, d_exp):
    L = xc.shape[0]
    nc = L // CHUNK

    def body(x_ref, b_ref, c_ref, dt_ref, ac_ref, ar_ref, d_ref, y_ref, pst_ref, st_ref):
        @pl.when(pl.program_id(1) == 0)
        def _():
            st_ref[...] = jnp.zeros_like(st_ref)

        x, Bm, Cm = x_ref[...], b_ref[...], c_ref[...]
        dtc, ac, ar = dt_ref[...], ac_ref[...], ar_ref[...]
        causal = lax.broadcasted_iota(jnp.int32, (CHUNK, CHUNK), 0) >= lax.broadcasted_iota(jnp.int32, (CHUNK, CHUNK), 1)
        CB = _dot_nt(Cm, Bm)
        for r in range(_HPG):
            xh, xd, acol, alast, lm = _ssd_head(x, dtc, ac, ar, r, causal)
            P = st_ref[r]
            y = _dot_nn(CB * lm, xd) + jnp.exp(acol) * _dot_nt(Cm, P)
            y_ref[:, SSM_HEADDIM * r:SSM_HEADDIM * (r + 1)] = y + d_ref[:, SSM_HEADDIM * r:SSM_HEADDIM * (r + 1)] * xh
            pst_ref[r] = P
            st_ref[r] = P * jnp.exp(alast) + _dot_tn(xd * jnp.exp(alast - acol), Bm)

    return _call(
        body, "ssd_fwd", (SSM_GROUPS, nc),
        [pl.BlockSpec((CHUNK, _GW), lambda g, c: (c, g)),
         pl.BlockSpec((CHUNK, SSM_STATE), lambda g, c: (c, _B_COL0 + g)),
         pl.BlockSpec((CHUNK, SSM_STATE), lambda g, c: (c, _C_COL0 + g)),
         pl.BlockSpec((None, CHUNK, _HPG), lambda g, c: (g, c, 0)),
         pl.BlockSpec((None, CHUNK, _HPG), lambda g, c: (g, c, 0)),
         pl.BlockSpec((None, _HPG, CHUNK), lambda g, c: (g, 0, c)),
         pl.BlockSpec((None, 1, _GW), lambda g, c: (g, 0, 0))],
        (pl.BlockSpec((CHUNK, _GW), lambda g, c: (c, g)),
         pl.BlockSpec((None, None, _HPG, SSM_HEADDIM, SSM_STATE), lambda g, c: (c, g, 0, 0, 0))),
        (_sds((L, D_INNER), F32), _sds((nc, SSM_GROUPS, _HPG, SSM_HEADDIM, SSM_STATE), F32)),
        scratch=[pltpu.VMEM((_HPG, SSM_HEADDIM, SSM_STATE), F32)],
    )(xc, xc, xc, dt_g, acs_g, acsT_g, d_exp)


def ssd_bwd(dy, xc, dt_g, acs_g, acsT_g, pst, d_exp):
    L = xc.shape[0]
    nc = L // CHUNK

    def body(dy_ref, x_ref, b_ref, c_ref, dt_ref, ac_ref, ar_ref, pst_ref, d_ref,
             dx_ref, db_ref, dc_ref, ddt_ref, da_ref, dd_ref, dp_ref):
        @pl.when(pl.program_id(1) == 0)
        def _():
            dp_ref[...] = jnp.zeros_like(dp_ref)
            dd_ref[...] = jnp.zeros_like(dd_ref)

        dyv, x, Bm, Cm = dy_ref[...], x_ref[...], b_ref[...], c_ref[...]
        dtc, ac, ar = dt_ref[...], ac_ref[...], ar_ref[...]
        ri = lax.broadcasted_iota(jnp.int32, (CHUNK, CHUNK), 0)
        ci = lax.broadcasted_iota(jnp.int32, (CHUNK, CHUNK), 1)
        causal = ri >= ci
        lane4 = lax.broadcasted_iota(jnp.int32, (CHUNK, _HPG), 1)
        CB = _dot_nt(Cm, Bm)
        dB = jnp.zeros((CHUNK, SSM_STATE), F32)
        dC = jnp.zeros((CHUNK, SSM_STATE), F32)
        dCB = jnp.zeros((CHUNK, CHUNK), F32)
        ddt_blk = jnp.zeros((CHUNK, _HPG), F32)
        da_blk = jnp.zeros((CHUNK, _HPG), F32)
        for r in range(_HPG):
            sl = slice(SSM_HEADDIM * r, SSM_HEADDIM * (r + 1))
            xh, xd, acol, alast, lm = _ssd_head(x, dtc, ac, ar, r, causal)
            dyh = dyv[:, sl]
            P = pst_ref[r]
            dPn = dp_ref[r]
            eA = jnp.exp(acol)
            cd = jnp.exp(alast)
            dte = jnp.exp(alast - acol)
            G = CB * lm
            Z = _dot_nt(Cm, P)
            dZ = eA * dyh
            dC = dC + _dot_nn(dZ, P)
            dp_ref[r] = dPn * cd + _dot_tn(dZ, Cm)
            dA_col = jnp.sum(dZ * Z, axis=1, keepdims=True)
            BdS = _dot_nt(Bm, dPn)
            dxd = dte * BdS
            dB = dB + dte * _dot_nn(xd, dPn)
            t = jnp.sum(xd * BdS, axis=1, keepdims=True) * dte
            dA_col = dA_col - t
            dA_last = jnp.sum(t, axis=0, keepdims=True) + jnp.sum(
                jnp.sum(dPn * P, axis=1, keepdims=True), axis=0, keepdims=True) * cd
            dG = _dot_nt(dyh, xd)
            dxd = dxd + _dot_tn(G, dyh)
            dCB = dCB + dG * lm
            W = dG * G
            dA_col = dA_col + jnp.sum(W, axis=1, keepdims=True)
            dA_row = jnp.sum(jnp.where(ri == ci, dA_col, 0.0), axis=0, keepdims=True) - jnp.sum(W, axis=0, keepdims=True)
            da_col = jnp.sum(jnp.where(ci >= ri, dA_row, 0.0), axis=1, keepdims=True) + dA_last
            da_blk = jnp.where(lane4 == r, da_col, da_blk)
            ddt_blk = jnp.where(lane4 == r, jnp.sum(dxd * xh, axis=1, keepdims=True), ddt_blk)
            dx_ref[:, sl] = dxd * dtc[:, r:r + 1] + d_ref[:, sl] * dyh
        dc_ref[...] = dC + _dot_nn(dCB, Bm)
        db_ref[...] = dB + _dot_tn(dCB, Cm)
        ddt_ref[...] = ddt_blk
        da_ref[...] = da_blk
        dd_ref[...] += jnp.sum(dyv * x, axis=0, keepdims=True)

    rc = lambda g, c: (nc - 1 - c, g)
    small = pl.BlockSpec((None, CHUNK, _HPG), lambda g, c: (g, nc - 1 - c, 0))
    return _call(
        body, "ssd_bwd", (SSM_GROUPS, nc),
        [pl.BlockSpec((CHUNK, _GW), rc),
         pl.BlockSpec((CHUNK, _GW), rc),
         pl.BlockSpec((CHUNK, SSM_STATE), lambda g, c: (nc - 1 - c, _B_COL0 + g)),
         pl.BlockSpec((CHUNK, SSM_STATE), lambda g, c: (nc - 1 - c, _C_COL0 + g)),
         small, small,
         pl.BlockSpec((None, _HPG, CHUNK), lambda g, c: (g, 0, nc - 1 - c)),
         pl.BlockSpec((None, None, _HPG, SSM_HEADDIM, SSM_STATE), lambda g, c: (nc - 1 - c, g, 0, 0, 0)),
         pl.BlockSpec((None, 1, _GW), lambda g, c: (g, 0, 0))],
        (pl.BlockSpec((CHUNK, _GW), rc),
         pl.BlockSpec((CHUNK, SSM_STATE), rc),
         pl.BlockSpec((CHUNK, SSM_STATE), rc),
         small, small,
         pl.BlockSpec((None, 1, _GW), lambda g, c: (g, 0, 0))),
        (_sds((L, D_INNER), F32), _sds((L, SSM_GROUPS * SSM_STATE), F32), _sds((L, SSM_GROUPS * SSM_STATE), F32),
         _sds((SSM_GROUPS, L, _HPG), F32), _sds((SSM_GROUPS, L, _HPG), F32), _sds((SSM_GROUPS, 1, _GW), F32)),
        scratch=[pltpu.VMEM((_HPG, SSM_HEADDIM, SSM_STATE), F32)],
    )(dy, xc, xc, xc, dt_g, acs_g, acsT_g, pst, d_exp)


_NGW = D_INNER // SSM_GROUPS


def gate_norm(y, zx, nw):
    L = y.shape[0]
    tm = _tile(L, 256)

    def body(y_ref, z_ref, nw_ref, o_ref):
        for g in range(SSM_GROUPS):
            sl = slice(_NGW * g, _NGW * (g + 1))
            z = z_ref[:, sl]
            y2 = y_ref[:, sl] * (z * _sigmoid(z))
            r = lax.rsqrt(jnp.mean(y2 * y2, axis=-1, keepdims=True) + EPS)
            o_ref[:, sl] = (y2 * r * nw_ref[:, sl]).astype(BF)

    row = pl.BlockSpec((tm, D_INNER), lambda i: (i, 0))
    return _call(body, "gate_norm", (L // tm,), [row, row, pl.BlockSpec((1, D_INNER), lambda i: (0, 0))],
                 row, _sds((L, D_INNER), BF))(y, zx, nw)


def gate_norm_bwd(dyn, y, zx, nw):
    L = y.shape[0]
    tm = _tile(L, 256)

    def body(d_ref, y_ref, z_ref, nw_ref, dy_ref, dz_ref, acc_ref):
        @pl.when(pl.program_id(0) == 0)
        def _():
            acc_ref[...] = jnp.zeros_like(acc_ref)

        for g in range(SSM_GROUPS):
            sl = slice(_NGW * g, _NGW * (g + 1))
            z = z_ref[:, sl]
            yv = y_ref[:, sl]
            sg = _sigmoid(z)
            sz = z * sg
            y2 = yv * sz
            r = lax.rsqrt(jnp.mean(y2 * y2, axis=-1, keepdims=True) + EPS)
            yh = y2 * r
            d = d_ref[:, sl]
            dn = d * nw_ref[:, sl]
            dy2 = r * (dn - yh * jnp.mean(dn * yh, axis=-1, keepdims=True))
            dy_ref[:, sl] = dy2 * sz
            dz_ref[:, sl] = (dy2 * yv * sg * (1.0 + z * (1.0 - sg))).astype(BF)
            acc_ref[0:1, sl] += jnp.sum(d * yh, axis=0, keepdims=True)

    row = pl.BlockSpec((tm, D_INNER), lambda i: (i, 0))
    return _call(body, "gate_norm_bwd", (L // tm,), [row, row, row, pl.BlockSpec((1, D_INNER), lambda i: (0, 0))],
                 (row, row, pl.BlockSpec((8, D_INNER), lambda i: (0, 0))),
                 (_sds((L, D_INNER), F32), _sds((L, D_INNER), BF), _sds((8, D_INNER), F32)))(dyn, y, zx, nw)


_SCALE = HEAD_DIM ** -0.5
_REP = ATT_HEADS // KV_HEADS
_V_OFF = KV_HEADS * HEAD_DIM


def _stack_heads(ref, k):
    return jnp.concatenate([ref[:, HEAD_DIM * (k * _REP + r):HEAD_DIM * (k * _REP + r + 1)] for r in range(_REP)],
                           axis=0)


def _stack_sinks(s_ref, k):
    return jnp.concatenate([jnp.broadcast_to(s_ref[:, k * _REP + r:k * _REP + r + 1], (WINDOW, 1))
                            for r in range(_REP)], axis=0)


def _attn_probs(q4, kp, kc, sink, first):
    shape = (_REP * WINDOW, WINDOW)
    rows = jnp.bitwise_and(lax.broadcasted_iota(jnp.int32, shape, 0), WINDOW - 1)
    cols = lax.broadcasted_iota(jnp.int32, shape, 1)
    sp = jnp.where(jnp.logical_and(cols > rows, jnp.logical_not(first)), _dot_nt(q4, kp) * _SCALE, NEG)
    sc = jnp.where(cols <= rows, _dot_nt(q4, kc) * _SCALE, NEG)
    m = jnp.maximum(jnp.maximum(jnp.max(sp, axis=1, keepdims=True), jnp.max(sc, axis=1, keepdims=True)), sink)
    pp = jnp.exp(sp - m)
    pc = jnp.exp(sc - m)
    ps = jnp.exp(sink - m)
    inv = 1.0 / (jnp.sum(pp, axis=1, keepdims=True) + jnp.sum(pc, axis=1, keepdims=True) + ps)
    return pp * inv, pc * inv, ps * inv


def attn_fwd(q, kv, sinks_pad):
    L = q.shape[0]
    nb = L // WINDOW

    def body(q_ref, kc_ref, kp_ref, s_ref, o_ref):
        first = pl.program_id(0) == 0
        for k in range(KV_HEADS):
            ks = slice(HEAD_DIM * k, HEAD_DIM * (k + 1))
            vs = slice(_V_OFF + HEAD_DIM * k, _V_OFF + HEAD_DIM * (k + 1))
            pp, pc, _ = _attn_probs(_stack_heads(q_ref, k), kp_ref[:, ks], kc_ref[:, ks], _stack_sinks(s_ref, k), first)
            o4 = _dot_nn(pp, kp_ref[:, vs]) + _dot_nn(pc, kc_ref[:, vs])
            for r in range(_REP):
                h = k * _REP + r
                o_ref[:, HEAD_DIM * h:HEAD_DIM * (h + 1)] = o4[WINDOW * r:WINDOW * (r + 1)].astype(BF)

    qspec = pl.BlockSpec((WINDOW, D_MODEL), lambda i: (i, 0))
    return _call(body, "attn_fwd", (nb,),
                 [qspec, pl.BlockSpec((WINDOW, KV_DIM), lambda i: (i, 0)),
                  pl.BlockSpec((WINDOW, KV_DIM), lambda i: (jnp.maximum(i - 1, 0), 0)),
                  pl.BlockSpec((1, 128), lambda i: (0, 0))],
                 qspec, _sds((L, D_MODEL), BF))(q, kv, kv, sinks_pad)


def attn_bwd(q, kv, do, sinks_pad):
    L = q.shape[0]
    nb = L // WINDOW

    def body(q_ref, kc_ref, kp_ref, do_ref, s_ref, dq_ref, dc_ref, dp_ref, acc_ref):
        first = pl.program_id(0) == 0

        @pl.when(first)
        def _():
            acc_ref[...] = jnp.zeros_like(acc_ref)

        lane = lax.broadcasted_iota(jnp.int32, (1, 128), 1)
        dsink = jnp.zeros((1, 128), F32)
        for k in range(KV_HEADS):
            ks = slice(HEAD_DIM * k, HEAD_DIM * (k + 1))
            vs = slice(_V_OFF + HEAD_DIM * k, _V_OFF + HEAD_DIM * (k + 1))
            kp, kc, vp, vc = kp_ref[:, ks], kc_ref[:, ks], kp_ref[:, vs], kc_ref[:, vs]
            q4 = _stack_heads(q_ref, k)
            do4 = _stack_heads(do_ref, k)
            pp, pc, ps = _attn_probs(q4, kp, kc, _stack_sinks(s_ref, k), first)
            dpp = _dot_nt(do4, vp)
            dpc = _dot_nt(do4, vc)
            delta = jnp.sum(pp * dpp, axis=1, keepdims=True) + jnp.sum(pc * dpc, axis=1, keepdims=True)
            dsp = pp * (dpp - delta) * _SCALE
            dsc = pc * (dpc - delta) * _SCALE
            dq4 = _dot_nn(dsp, kp) + _dot_nn(dsc, kc)
            psd = ps * delta
            for r in range(_REP):
                h = k * _REP + r
                rs = slice(WINDOW * r, WINDOW * (r + 1))
                dq_ref[:, HEAD_DIM * h:HEAD_DIM * (h + 1)] = dq4[rs]
                dsink = dsink + jnp.where(lane == h, -jnp.sum(psd[rs], axis=0, keepdims=True), 0.0)
            dp_ref[:, ks] = _dot_tn(dsp, q4)
            dc_ref[:, ks] = _dot_tn(dsc, q4)
            dp_ref[:, vs] = _dot_tn(pp, do4)
            dc_ref[:, vs] = _dot_tn(pc, do4)
        acc_ref[0:1, :] += jnp.sum(dq_ref[...], axis=0, keepdims=True)
        acc_ref[1:2, 0:128] += dsink

    qspec = pl.BlockSpec((WINDOW, D_MODEL), lambda i: (i, 0))
    kspec = pl.BlockSpec((WINDOW, KV_DIM), lambda i: (i, 0))
    return _call(body, "attn_bwd", (nb,),
                 [qspec, kspec, pl.BlockSpec((WINDOW, KV_DIM), lambda i: (jnp.maximum(i - 1, 0), 0)), qspec,
                  pl.BlockSpec((1, 128), lambda i: (0, 0))],
                 (qspec, kspec, kspec, pl.BlockSpec((8, D_MODEL), lambda i: (0, 0))),
                 (_sds((L, D_MODEL), F32), _sds((L, KV_DIM), F32), _sds((L, KV_DIM), F32), _sds((8, D_MODEL), F32)),
                 )(q, kv, kv, do, sinks_pad)


def kv_grad_combine(parts):
    L = parts[0][0].shape[0]
    nb = L // WINDOW
    n = len(parts)

    def body(*refs):
        i = pl.program_id(0)
        o_ref, acc_ref = refs[2 * n], refs[2 * n + 1]

        @pl.when(i == 0)
        def _():
            acc_ref[...] = jnp.zeros_like(acc_ref)

        tot = refs[0][...]
        nxt = refs[1][...]
        for a in range(1, n):
            tot = tot + refs[2 * a][...]
            nxt = nxt + refs[2 * a + 1][...]
        tot = tot + jnp.where(i < nb - 1, nxt, 0.0)
        o_ref[...] = tot
        acc_ref[0:1, :] += jnp.sum(tot, axis=0, keepdims=True)

    cur = pl.BlockSpec((WINDOW, KV_DIM), lambda i: (i, 0))
    nxt = pl.BlockSpec((WINDOW, KV_DIM), lambda i: (jnp.minimum(i + 1, nb - 1), 0))
    args = [t for p in parts for t in p]
    return _call(body, "kv_grad_combine", (nb,), [cur, nxt] * n,
                 (cur, pl.BlockSpec((8, KV_DIM), lambda i: (0, 0))),
                 (_sds((L, KV_DIM), F32), _sds((8, KV_DIM), F32)))(*args)


def mod_fwd(c_all, w, b, name):
    n, _, C = w.shape

    def body(c_ref, w_ref, b_ref, o_ref, ca_ref):
        cv = c_ref[...]
        ca = cv * _sigmoid(cv)
        ca_ref[...] = ca
        o_ref[...] = _dot(ca, w_ref[...], ((1,), (0,))) + b_ref[...]

    return _call(body, name, (n,),
                 [pl.BlockSpec((N_DEV, D_MODEL), lambda i: (0, 0)),
                  pl.BlockSpec((None, D_MODEL, C), lambda i: (i, 0, 0)),
                  pl.BlockSpec((None, 1, C), lambda i: (i, 0, 0))],
                 (pl.BlockSpec((None, N_DEV, C), lambda i: (i, 0, 0)), pl.BlockSpec((N_DEV, D_MODEL), lambda i: (0, 0))),
                 (_sds((n, N_DEV, C), F32), _sds((N_DEV, D_MODEL), F32)))(c_all, w, b)


def mod_wgrad(c_act_t, dmod, name):
    n, _, C = dmod.shape
    tr = 256

    def body(ct_ref, d_ref, o_ref):
        acc = ct_ref[:, 0:1] * d_ref[0:1, :]
        for bidx in range(1, N_DEV):
            acc = acc + ct_ref[:, bidx:bidx + 1] * d_ref[bidx:bidx + 1, :]
        o_ref[...] = acc

    return _call(body, name, (n, D_MODEL // tr),
                 [pl.BlockSpec((tr, N_DEV), lambda i, j: (j, 0)),
                  pl.BlockSpec((None, N_DEV, C), lambda i, j: (i, 0, 0))],
                 pl.BlockSpec((None, tr, C), lambda i, j: (i, j, 0)), _sds((n, D_MODEL, C), F32))(c_act_t, dmod)


def _my_pos():
    return lax.axis_index("x"), lax.axis_index("y"), lax.axis_index("c")


def small_all_gather(v):
    m_per, n = v.shape

    def body(x_ref, out_ref, send_sems, recv_sems, local_sem):
        x, y, c = _my_pos()
        me, sibling = (x, y, c), (x, y, 1 - c)
        chips = [(1 - x, y), (x, 1 - y), (1 - x, 1 - y)]

        def rows(px, py, pc):
            return out_ref.at[pl.ds((4 * px + 2 * py + pc) * m_per, m_per), :]

        def copy(k, block, to, src=None):
            return pltpu.make_async_remote_copy(
                src_ref=rows(*block) if src is None else src, dst_ref=rows(*block),
                send_sem=send_sems.at[k], recv_sem=recv_sems.at[k], device_id=to, device_id_type=MESH)

        mine = pltpu.make_async_copy(x_ref, rows(*me), local_sem)
        mine.start()
        first = [copy(0, me, sibling, src=x_ref)]
        first += [copy(1 + j, me, (*chip, c), src=x_ref) for j, chip in enumerate(chips)]
        for cp in first:
            cp.start()
        passed = [copy(4 + j, (*chip, c), sibling) for j, chip in enumerate(chips)]
        for j, chip in enumerate(chips):
            copy(1 + j, (*chip, c), me).wait_recv()
            passed[j].start()
        copy(0, sibling, me).wait_recv()
        for j, chip in enumerate(chips):
            copy(4 + j, (*chip, 1 - c), me).wait_recv()
        for cp in first + passed:
            cp.wait_send()
        mine.wait()

    return pl.pallas_call(
        body, name="small_all_gather",
        out_shape=_sds((N_DEV * m_per, n), v.dtype),
        in_specs=[pl.BlockSpec(memory_space=pltpu.VMEM)],
        out_specs=pl.BlockSpec(memory_space=pltpu.VMEM),
        scratch_shapes=[pltpu.SemaphoreType.DMA((7,)), pltpu.SemaphoreType.DMA((7,)), pltpu.SemaphoreType.DMA],
        compiler_params=pltpu.CompilerParams(vmem_limit_bytes=VMEM_LIMIT),
    )(v)


def big_all_gather(arrs):
    n = len(arrs)

    def body(*refs):
        ins, outs = refs[:n], refs[n:2 * n]
        send_sems, recv_sems, local_sems = refs[2 * n], refs[2 * n + 1], refs[2 * n + 2]
        x, y, c = _my_pos()
        me, sibling = (x, y, c), (x, y, 1 - c)
        chips = [(1 - x, y), (x, 1 - y), (1 - x, 1 - y)]

        def slot(a, px, py, pc):
            return outs[a].at[4 * px + 2 * py + pc]

        def copy(a, k, block, to, src=None):
            return pltpu.make_async_remote_copy(
                src_ref=slot(a, *block) if src is None else src, dst_ref=slot(a, *block),
                send_sem=send_sems.at[7 * a + k], recv_sem=recv_sems.at[7 * a + k], device_id=to, device_id_type=MESH)

        mine = [pltpu.make_async_copy(ins[a], slot(a, *me), local_sems.at[a]) for a in range(n)]
        for cp in mine:
            cp.start()
        first = []
        for a in range(n):
            first.append(copy(a, 0, me, sibling, src=ins[a]))
            first += [copy(a, 1 + j, me, (*chip, c), src=ins[a]) for j, chip in enumerate(chips)]
        for cp in first:
            cp.start()
        passed = []
        for a in range(n):
            for j, chip in enumerate(chips):
                copy(a, 1 + j, (*chip, c), me).wait_recv()
                fwd = copy(a, 4 + j, (*chip, c), sibling)
                fwd.start()
                passed.append(fwd)
        for a in range(n):
            copy(a, 0, sibling, me).wait_recv()
            for j, chip in enumerate(chips):
                copy(a, 4 + j, (*chip, 1 - c), me).wait_recv()
        for cp in first + passed:
            cp.wait_send()
        for cp in mine:
            cp.wait()

    hbm = pl.BlockSpec(memory_space=pltpu.HBM)
    return pl.pallas_call(
        body, name="big_all_gather",
        out_shape=[_sds((N_DEV,) + a.shape, a.dtype) for a in arrs],
        in_specs=[hbm] * n, out_specs=[hbm] * n,
        scratch_shapes=[pltpu.SemaphoreType.DMA((7 * n,)), pltpu.SemaphoreType.DMA((7 * n,)),
                        pltpu.SemaphoreType.DMA((n,))],
    )(*arrs)


_FLIPS =[(fx, fy, fc) for fx in (0, 1) for fy in (0, 1) for fc in (0, 1)][1:]
_HBM = pl.BlockSpec(memory_space=pltpu.HBM)
_SEM = pl.BlockSpec(memory_space=pltpu.SEMAPHORE)
_EFFECT = pltpu.SideEffectType.DATAFLOW_SIDE_EFFECTING


def _flip(x, y, c, f):
    return (1 - x if f[0] else x), (1 - y if f[1] else y), (1 - c if f[2] else c)


def _xfer_copies(srcs, lands, send_sems, recv_sems, scatter):
    x, y, c = _my_pos()
    me = 4 * x + 2 * y + c
    copies = []
    for a in range(len(srcs)):
        for k, f in enumerate(_FLIPS):
            px, py, pc = _flip(x, y, c, f)
            src = srcs[a].at[4 * px + 2 * py + pc] if scatter else srcs[a]
            copies.append(pltpu.make_async_remote_copy(
                src_ref=src, dst_ref=lands[a].at[me], send_sem=send_sems.at[7 * a + k],
                recv_sem=recv_sems.at[7 * a + k], device_id=(px, py, pc), device_id_type=MESH))
    return copies


def _own_copies(srcs, lands, local_sems, scatter):
    x, y, c = _my_pos()
    me = 4 * x + 2 * y + c
    return [pltpu.make_async_copy(srcs[a].at[me] if scatter else srcs[a], lands[a].at[me], local_sems.at[a])
            for a in range(len(srcs))]


def xfer_start(arrs, scatter, after, name):
    n = len(arrs)
    land_shapes = [a.shape if scatter else (N_DEV,) + a.shape for a in arrs]

    def body(*refs):
        srcs, lands = refs[:n], refs[n:2 * n]
        send_sems, recv_sems, local_sems = refs[2 * n + 1], refs[2 * n + 2], refs[2 * n + 3]
        token = refs[-1]
        for cp in _xfer_copies(srcs, lands, send_sems, recv_sems, scatter):
            cp.start()
        for cp in _own_copies(srcs, lands, local_sems, scatter):
            cp.start()
        token[...] = jnp.zeros_like(token)

    out = pl.pallas_call(
        body, name=name,
        out_shape=(pltpu.SemaphoreType.DMA((7 * n,)), pltpu.SemaphoreType.DMA((7 * n,)),
                   pltpu.SemaphoreType.DMA((n,)),
                   *[pltpu.HBM(a.shape, a.dtype) for a in arrs],
                   *[pltpu.HBM(s, a.dtype) for s, a in zip(land_shapes, arrs)],
                   _sds((8, 128), F32)),
        in_specs=[_HBM] * (2 * n) + [pl.BlockSpec(memory_space=pl.ANY)],
        out_specs=(_SEM, _SEM, _SEM, *([_HBM] * (2 * n)), pl.BlockSpec(memory_space=pltpu.VMEM)),
        input_output_aliases={i: 3 + i for i in range(2 * n)},
        compiler_params=pltpu.CompilerParams(has_side_effects=_EFFECT),
    )(*[pltpu.with_memory_space_constraint(a, pltpu.HBM) for a in arrs],
      *[pltpu.with_memory_space_constraint(lax.empty(s, a.dtype), pltpu.HBM) for s, a in zip(land_shapes, arrs)],
      after)
    return (out[0], out[1], out[2], list(out[3:3 + n]), list(out[3 + n:3 + 2 * n]), scatter), out[-1]


def xfer_wait(handle, after, name):
    send_sems, recv_sems, local_sems, srcs, lands, scatter = handle
    n = len(srcs)

    def body(*refs):
        srcs_r, lands_r = refs[:n], refs[n:2 * n]
        ssem, rsem, lsem = refs[2 * n], refs[2 * n + 1], refs[2 * n + 2]
        for cp in _xfer_copies(srcs_r, lands_r, ssem, rsem, scatter):
            cp.wait_send()
            cp.wait_recv()
        for cp in _own_copies(srcs_r, lands_r, lsem, scatter):
            cp.wait()

    out = pl.pallas_call(
        body, name=name,
        out_shape=(*[pltpu.HBM(a.shape, a.dtype) for a in srcs], *[pltpu.HBM(a.shape, a.dtype) for a in lands]),
        in_specs=[_HBM] * (2 * n) + [_SEM, _SEM, _SEM, pl.BlockSpec(memory_space=pl.ANY)],
        out_specs=tuple([_HBM] * (2 * n)),
        input_output_aliases={i: i for i in range(2 * n)},
        compiler_params=pltpu.CompilerParams(has_side_effects=_EFFECT),
    )(*srcs, *lands, send_sems, recv_sems, local_sems, after)
    return list(out[n:])


def adamw(parts, w, m, v, name, row0=0, prev=None):
    r_tot, C = w.shape
    n_parts, R = parts.shape[0], parts.shape[1]
    row_bytes = 2 * (n_parts * C * parts.dtype.itemsize + 7 * C * 4)
    tr = R
    for cand in (512, 352, 256, 176, 128, 64):
        if R % cand == 0 and row0 % cand == 0 and R > cand and cand * row_bytes <= ADAMW_VMEM_BUDGET:
            tr = cand
            break
    tc = C
    if tr == R and R * row_bytes > ADAMW_VMEM_BUDGET:
        assert row0 == 0 and R == r_tot
        tc = next(t for t in (512, 256, 128) if C % t == 0 and R * row_bytes * t // C <= ADAMW_VMEM_BUDGET)
    assert row0 % tr == 0 and (tr % 8 == 0 or (tr == r_tot and row0 == 0))
    blk0 = row0 // tr
    c1 = 1.0 / (1.0 - ADAM_B1 ** ADAM_STEP)
    c2 = 1.0 / (1.0 - ADAM_B2 ** ADAM_STEP)

    def body(p_ref, w_ref, m_ref, v_ref, *rest):
        g_ref, d_ref, nm_ref, nv_ref = rest[-4:]
        g = p_ref[0].astype(F32)
        for k in range(1, n_parts):
            g = g + p_ref[k].astype(F32)
        nm = ADAM_B1 * m_ref[...] + (1.0 - ADAM_B1) * g
        nv = ADAM_B2 * v_ref[...] + (1.0 - ADAM_B2) * (g * g)
        g_ref[...] = g
        nm_ref[...] = nm
        nv_ref[...] = nv
        d_ref[...] = -ADAM_LR * ((nm * c1) / (jnp.sqrt(nv * c2) + ADAM_EPS) + ADAM_WD * w_ref[...])

    if tc == C:
        grid = (R // tr,)
        blk = pl.BlockSpec((tr, C), lambda i: (i + blk0, 0))
        p_spec = pl.BlockSpec((n_parts, tr, C), lambda i: (0, i, 0))
    else:
        grid = (C // tc,)
        blk = pl.BlockSpec((R, tc), lambda i: (0, i))
        p_spec = pl.BlockSpec((n_parts, R, tc), lambda i: (0, 0, i))
    in_specs = [p_spec, blk, blk, blk]
    args = [parts, w, m, v]
    aliases = {}
    if prev is not None:
        in_specs += [pl.BlockSpec(memory_space=pl.ANY)] * 4
        args += list(prev)
        aliases = {4 + k: k for k in range(4)}
    return pl.pallas_call(
        body, name=name, grid=grid, in_specs=in_specs, out_specs=(blk, blk, blk, blk),
        out_shape=tuple(_sds((r_tot, C), F32) for _ in range(4)), input_output_aliases=aliases,
        compiler_params=pltpu.CompilerParams(vmem_limit_bytes=VMEM_LIMIT))(*args)


def _ffn_fwd(x, nw, sh, sc, g, wt_gu, w_dn):
    h = norm_mod(x, nw, sh, sc, "ffn_norm")
    gp, up, a = ffn_up(h, wt_gu)
    y, xn = matmul(a, w_dn, "nn", BF, "ffn_down", res=x, gate=g, coef=0.5)
    return xn, (x, h, gp, up, a, y)


def _ffn_bwd(dxo, saved, nw, sc, g, wt_gu, w_dn):
    x, h, gp, up, a, y = saved
    dy, acc1 = resid_gate_bwd(dxo, y, g, 0.5, "ffn_gate_bwd")
    d_wdn = matmul(a, dy, "tn", BF, "ffn_down_wgrad")
    dg, du = ffn_down_dgrad(dy, w_dn, gp, up)
    d_wt = ffn_up_wgrad(dg, du, h)
    dh = ffn_up_dgrad(dg, du, wt_gu)
    dx, acc2 = norm_mod_bwd(x, dh, dxo, nw, sc, "ffn_norm_bwd")
    return dx, d_wt, d_wdn, (acc2[0], acc2[1], acc1[0]), acc2[2]


def _group_layout(a):
    L = a.shape[0]
    return a[:, :SSM_HEADS].reshape(L, SSM_GROUPS, _HPG).transpose(1, 0, 2)


def _ungroup_layout(a):
    L = a.shape[1]
    return jnp.pad(a.transpose(1, 0, 2).reshape(L, SSM_HEADS), ((0, 0), (0, 128 - SSM_HEADS)))


def _pad_row(vec, n=128):
    return jnp.pad(vec.reshape(1, -1), ((0, 0), (0, n - vec.shape[-1])))


def _mamba_fwd(x, nw, sh, sc, g, p):
    h = norm_mod(x, nw, sh, sc, "mix_norm")
    zx = matmul(h, p["w_in_t"], "nt", F32, "ssm_in")
    xc = conv_fwd(zx, p["conv_w"], p["conv_b"])
    dt, acs = dt_prep(zx, p["dt_bias"], p["a_log"])
    dt_g, acs_g = _group_layout(dt), _group_layout(acs)
    acs_t = acs_g.transpose(0, 2, 1)
    y, pst = ssd_fwd(xc, dt_g, acs_g, acs_t, p["d_exp"])
    yn = gate_norm(y, zx, p["norm_w"])
    yo, xn = matmul(yn, p["w_out"], "nn", BF, "ssm_out", res=x, gate=g, coef=1.0)
    return xn, (x, h, zx, xc, dt, dt_g, acs_g, acs_t, y, pst, yn, yo)


def _mamba_bwd(dxo, saved, nw, sc, g, p):
    x, h, zx, xc, dt, dt_g, acs_g, acs_t, y, pst, yn, yo = saved
    dyo, acc1 = resid_gate_bwd(dxo, yo, g, 1.0, "mix_gate_bwd")
    d_wout = matmul(yn, dyo, "tn", BF, "ssm_out_wgrad")
    dyn = matmul(dyo, p["w_out"], "nt", F32, "ssm_out_dgrad")
    dy, dz, accn = gate_norm_bwd(dyn, y, zx, p["norm_w"])
    dxs, dB, dC, ddt_g, da_g, dd = ssd_bwd(dy, xc, dt_g, acs_g, acs_t, pst, p["d_exp"])
    dxc = jnp.concatenate([dxs, dB, dC], axis=1)
    ds = conv_bwd_act(dxc, zx, p["conv_w"], p["conv_b"])
    du, accc = conv_bwd(ds, zx, p["conv_w"])
    draw, accdt = dt_bwd(_ungroup_layout(ddt_g), _ungroup_layout(da_g), dt, zx, p["dt_bias"], p["a_log"])
    dzx = jnp.concatenate([dz, du, draw], axis=1)
    d_win = matmul(dzx, h, "tn", BF, "ssm_in_wgrad")[:IN_PROJ]
    dh = matmul(dzx, p["w_in_t"], "nn", F32, "ssm_in_dgrad")
    dx, acc2 = norm_mod_bwd(x, dh, dxo, nw, sc, "mix_norm_bwd")
    small = dict(conv_w=accc[:CONV_WIDTH], conv_b=accc[CONV_WIDTH], dt_bias=accdt[0, :SSM_HEADS],
                 a_log=accdt[1, :SSM_HEADS], d=dd.reshape(SSM_HEADS, SSM_HEADDIM).sum(-1), norm_w=accn[0])
    return dx, d_win, d_wout, (acc2[0], acc2[1], acc1[0]), acc2[2], small


def _attn_layer_fwd(x, nw, sh, sc, g, p, kv):
    h = norm_mod(x, nw, sh, sc, "mix_norm")
    q = matmul(h, p["w_q"], "nn", F32, "attn_q", bias=p["b_q"])
    o = attn_fwd(q, kv, p["sinks"])
    yo, xn = matmul(o, p["w_o"], "nn", BF, "attn_o", bias=p["b_o"], res=x, gate=g, coef=1.0)
    return xn, (x, h, q, o, yo)


def _attn_layer_bwd(dxo, saved, nw, sc, g, p, kv):
    x, h, q, o, yo = saved
    dyo, acc1 = resid_gate_bwd(dxo, yo, g, 1.0, "mix_gate_bwd")
    d_wo = matmul(o, dyo, "tn", BF, "attn_o_wgrad")
    do = matmul(dyo, p["w_o"], "nt", F32, "attn_o_dgrad")
    dq, dkv_c, dkv_p, acca = attn_bwd(q, kv, do, p["sinks"])
    d_wq = matmul(h, dq, "tn", BF, "attn_q_wgrad")
    dh = matmul(dq, p["w_q"], "nt", F32, "attn_q_dgrad")
    dx, acc2 = norm_mod_bwd(x, dh, dxo, nw, sc, "mix_norm_bwd")
    small = dict(b_q=acca[0], sinks=acca[1, :ATT_HEADS], b_o=acc1[1])
    return dx, d_wq, d_wo, (acc2[0], acc2[1], acc1[0]), acc2[2], small, (dkv_c, dkv_p)


def _pack_rows(pieces):
    rows, spans, off = [], [], 0
    for a in pieces:
        flat = a.reshape(-1).astype(F32)
        n = -(-flat.shape[0] // D_MODEL)
        rows.append(jnp.pad(flat, (0, n * D_MODEL - flat.shape[0])).reshape(n, D_MODEL))
        spans.append((off, a.shape))
        off += n
    pad = -off % 8
    if pad:
        rows.append(jnp.zeros((pad, D_MODEL), F32))
    return jnp.concatenate(rows, axis=0), spans, off + pad


def _unpack_rows(g, spans):
    out = []
    for off, shape in spans:
        size = 1
        for s in shape:
            size *= s
        n = -(-size // D_MODEL)
        out.append(g[:, off:off + n].reshape(N_DEV, n * D_MODEL)[:, :size].reshape((N_DEV,) + tuple(shape)))
    return out


def _unshard_last(g):
    nd = g.ndim
    perm = tuple(range(1, nd - 1)) + (0, nd - 1)
    t = g.transpose(perm)
    return t.reshape(t.shape[:-2] + (N_DEV * g.shape[-1],))


def _shard_last(a, me):
    s = a.shape[-1] // N_DEV
    return lax.dynamic_slice_in_dim(a, me * s, s, axis=a.ndim - 1)


def kernel(x, c, ffn_norm_w, ffn_w_gu, ffn_w_down, mod_w, mod_b, mix_norm_w, ssm_w_in, ssm_conv_w, ssm_conv_b, ssm_dt_bias, ssm_a_log, ssm_d, ssm_norm_w, ssm_w_out, kv_norm_w, kv_mod_w, kv_mod_b, w_kv, b_kv, attn_w_q, attn_b_q, attn_sinks, attn_w_o, attn_b_o, final_norm_w, loss_target, m_ffn_norm_w, m_ffn_w_gu, m_ffn_w_down, m_mod_w, m_mod_b, m_mix_norm_w, m_ssm_w_in, m_ssm_conv_w, m_ssm_conv_b, m_ssm_dt_bias, m_ssm_a_log, m_ssm_d, m_ssm_norm_w, m_ssm_w_out, m_kv_norm_w, m_kv_mod_w, m_kv_mod_b, m_w_kv, m_b_kv, m_attn_w_q, m_attn_b_q, m_attn_sinks, m_attn_w_o, m_attn_b_o, m_final_norm_w, v_ffn_norm_w, v_ffn_w_gu, v_ffn_w_down, v_mod_w, v_mod_b, v_mix_norm_w, v_ssm_w_in, v_ssm_conv_w, v_ssm_conv_b, v_ssm_dt_bias, v_ssm_a_log, v_ssm_d, v_ssm_norm_w, v_ssm_w_out, v_kv_norm_w, v_kv_mod_w, v_kv_mod_b, v_w_kv, v_b_kv, v_attn_w_q, v_attn_b_q, v_attn_sinks, v_attn_w_o, v_attn_b_o, v_final_norm_w):
    D = D_MODEL
    me = 4 * lax.axis_index("x") + 2 * lax.axis_index("y") + lax.axis_index("c")
    xs = x[0]
    target = loss_target[0]
    mod_cols = mod_w.shape[-1]
    kvm_cols = kv_mod_w.shape[-1]

    packed, spans, _ = _pack_rows([c, ffn_norm_w, ssm_conv_w, ssm_conv_b, ssm_norm_w])
    nrow = packed.shape[0]
    g1 = small_all_gather(packed).reshape(N_DEV, nrow, D)
    c_all, fnw_g, cw_g, cb_g, snw_g = _unpack_rows(g1, spans)
    c_all = c_all.reshape(N_DEV, D)
    ffn_nw = _unshard_last(fnw_g)
    conv_w = _unshard_last(cw_g)
    conv_b = _unshard_last(cb_g)
    ssm_nw = _unshard_last(snw_g)

    mod_b_loc = lax.dynamic_slice_in_dim(mod_b, me * mod_cols, mod_cols, axis=1).reshape(DEPTH, 1, mod_cols)
    kvb_loc = lax.dynamic_slice_in_dim(kv_mod_b, me * kvm_cols, kvm_cols, axis=0).reshape(1, 1, kvm_cols)
    modp, c_act = mod_fwd(c_all, mod_w, mod_b_loc, "mod_fwd")
    kvmp, _ = mod_fwd(c_all, kv_mod_w.reshape(1, D, kvm_cols), kvb_loc, "kv_mod_fwd")
    packed2, spans2, _ = _pack_rows([modp, kvmp])
    nrow2 = packed2.shape[0]
    g2 = small_all_gather(packed2).reshape(N_DEV, nrow2, D)
    modp_g, kvmp_g = _unpack_rows(g2, spans2)
    mod_all = modp_g.transpose(1, 2, 0, 3).reshape(DEPTH, N_DEV, N_MOD * D)
    kvm_all = kvmp_g.transpose(1, 2, 0, 3).reshape(N_DEV, 2 * D)
    mod_me = lax.dynamic_index_in_dim(mod_all, me, axis=1, keepdims=False).reshape(DEPTH, N_MOD, 1, D)
    kvm_me = lax.dynamic_index_in_dim(kvm_all, me, axis=0, keepdims=False).reshape(2, 1, D)

    gu_t = jnp.swapaxes(ffn_w_gu, 2, 3)
    win_t = jnp.transpose(ssm_w_in, (2, 0, 1))
    S = gu_t.shape[2]
    s_in = win_t.shape[0]
    r_dn = ffn_w_down.shape[2]
    r_mix = ssm_w_out.shape[1]
    r_at = attn_w_q.shape[1]

    def layer_pack(k):
        arrs = [gu_t[k, 0].astype(BF), gu_t[k, 1].astype(BF), ffn_w_down[k, 0].astype(BF), ffn_w_down[k, 1].astype(BF)]
        if k < N_A:
            arrs += [ssm_w_out[k].astype(BF), win_t[:, k].astype(BF)]
        else:
            arrs += [attn_w_q[k - N_A].astype(BF), attn_w_o[k - N_A].astype(BF)]
        if k == N_A:
            arrs.append(w_kv.astype(BF))
        return arrs

    packs = [layer_pack(k) for k in range(DEPTH)]
    gathered = [None] * DEPTH
    first = big_all_gather([packs[0][0], packs[0][2]])
    pend_rest, tok_next = xfer_start([packs[0][1]] + packs[0][3:], False, first[0], "gather_start_0")
    pending = [None] * DEPTH
    for k in range(1, DEPTH):
        pending[k], tok_next = xfer_start(packs[k], False, tok_next, "gather_start_%d" % k)
    gathered[0] = [first[0], None, first[1]]

    def wt_gu_full(i, j):
        return gathered[i][j].reshape(N_DEV * S, D)

    def w_dn_full(i, j):
        return gathered[i][2 + j].reshape(D_FF, D)

    def mix_rows(i, a):
        return gathered[i][4 + a].reshape(-1, D)

    def mamba_params(j):
        w_in_t = jnp.pad(mix_rows(j, 1), ((0, IN_PROJ_PAD - IN_PROJ), (0, 0)))
        return dict(w_in_t=w_in_t, w_out=mix_rows(j, 0), conv_w=conv_w[j], conv_b=conv_b[j].reshape(1, -1),
                    dt_bias=_pad_row(ssm_dt_bias[j]), a_log=_pad_row(ssm_a_log[j]),
                    d_exp=jnp.repeat(ssm_d[j], SSM_HEADDIM).reshape(SSM_GROUPS, 1, _GW),
                    norm_w=ssm_nw[j].reshape(1, -1))

    def attn_params(j):
        return dict(w_q=mix_rows(N_A + j, 0), w_o=mix_rows(N_A + j, 1),
                    b_q=attn_b_q[j].reshape(1, -1), b_o=attn_b_o[j].reshape(1, -1), sinks=_pad_row(attn_sinks[j]))

    saved = []
    kv = None
    kv_saved = None
    w_kv_full = None
    xcur = xs
    for i in range(DEPTH):
        if i >= 1:
            gathered[i] = xfer_wait(pending[i], xcur, "gather_wait_%d" % i)
        md = mod_me[i]
        if i == 0:
            md = md + tok_next[0, 0]
        if i == N_A:
            w_kv_full = gathered[N_A][6].reshape(D, KV_DIM)
            h_kv = norm_mod(xcur, kv_norm_w.reshape(1, D), kvm_me[0], kvm_me[1], "kv_norm")
            kv = matmul(h_kv, w_kv_full, "nn", F32, "kv_proj", bias=b_kv.reshape(1, -1))
            kv_saved = (xcur, h_kv)
        x1, s1 = _ffn_fwd(xcur, ffn_nw[i, 0].reshape(1, D), md[0], md[1], md[2], wt_gu_full(i, 0), w_dn_full(i, 0))
        if i == 0:
            rest = xfer_wait(pend_rest, x1, "gather_wait_0")
            gathered[0] = [first[0], rest[0], first[1]] + rest[1:]
        if i < N_A:
            pm = mamba_params(i)
            x2, s2 = _mamba_fwd(x1, mix_norm_w[i].reshape(1, D), md[3], md[4], md[5], pm)
        else:
            pm = attn_params(i - N_A)
            x2, s2 = _attn_layer_fwd(x1, mix_norm_w[i].reshape(1, D), md[3], md[4], md[5], pm, kv)
        x3, s3 = _ffn_fwd(x2, ffn_nw[i, 1].reshape(1, D), md[6], md[7], md[8], wt_gu_full(i, 1), w_dn_full(i, 1))
        saved.append((s1, s2, s3, pm))
        xcur = x3

    dx, accf = final_loss(xcur, final_norm_w.reshape(1, D), target)
    d_mod = [None] * DEPTH
    d_ffn_nw = [[None, None] for _ in range(DEPTH)]
    d_mix_nw = [None] * DEPTH
    sm_m, sm_a = [None] * N_A, [None] * N_A
    kv_parts = [None] * N_A
    d_kvm = d_kv_nw = d_bkv = None
    exchanges = []
    tok = None

    def send(arrs, tag):
        handle, t = xfer_start(arrs, True, dx, "exch_start_%s" % tag)
        exchanges.append((handle, tag))
        return t

    def ffn_slabs(d_wt, d_wdn):
        return [d_wt.reshape(N_DEV, S, D), d_wdn.reshape(N_DEV, r_dn, D)]

    for i in reversed(range(DEPTH)):
        md = mod_me[i]
        s1, s2, s3, pm = saved[i]
        g2 = md[8] if tok is None else md[8] + tok[0, 0]
        dx, d_wt, d_wdn, m2, d_ffn_nw[i][1] = _ffn_bwd(
            dx, s3, ffn_nw[i, 1].reshape(1, D), md[7], g2, wt_gu_full(i, 1), w_dn_full(i, 1))
        tok = send(ffn_slabs(d_wt, d_wdn), "f%d1" % i)
        gm = md[5] + tok[0, 0]
        if i < N_A:
            dx, d_in, d_out, mm_, d_mix_nw[i], sm_m[i] = _mamba_bwd(dx, s2, mix_norm_w[i].reshape(1, D), md[4], gm, pm)
            tok = send([d_in.reshape(N_DEV, s_in, D), d_out.reshape(N_DEV, r_mix, D)], "m%d" % i)
        else:
            j = i - N_A
            dx, d_q, d_o, mm_, d_mix_nw[i], sm_a[j], kv_parts[j] = _attn_layer_bwd(
                dx, s2, mix_norm_w[i].reshape(1, D), md[4], gm, pm, kv)
            tok = send([d_q.reshape(N_DEV, r_at, D), d_o.reshape(N_DEV, r_at, D)], "m%d" % i)
        g1 = md[2] + tok[0, 0]
        dx, d_wt, d_wdn, m1, d_ffn_nw[i][0] = _ffn_bwd(
            dx, s1, ffn_nw[i, 0].reshape(1, D), md[1], g1, wt_gu_full(i, 0), w_dn_full(i, 0))
        d_mod[i] = jnp.concatenate(list(m1) + list(mm_) + list(m2), axis=0)
        last = ffn_slabs(d_wt, d_wdn)
        if i == N_A:
            x_kv, h_kv = kv_saved
            dkv, acck = kv_grad_combine(kv_parts)
            d_bkv = acck[0]
            d_kv_w = matmul(h_kv, dkv, "tn", BF, "kv_wgrad")
            dh_kv = matmul(dkv, w_kv_full, "nt", F32, "kv_dgrad")
            dx, acc_kv = norm_mod_bwd(x_kv, dh_kv, dx, kv_norm_w.reshape(1, D), kvm_me[1], "kv_norm_bwd")
            d_kvm = jnp.concatenate([acc_kv[0], acc_kv[1]], axis=0)
            d_kv_nw = acc_kv[2]
            last.append(d_kv_w.reshape(N_DEV, -1, KV_DIM))
        tok = send(last, "f%d0" % i)
    grad_x = dx.reshape(x.shape)

    small_list = [
        jnp.stack(d_mod, 0), d_kvm,
        jnp.stack([jnp.stack(r, 0) for r in d_ffn_nw], 0),
        jnp.stack(d_mix_nw, 0),
        jnp.stack([s["conv_w"] for s in sm_m], 0), jnp.stack([s["conv_b"] for s in sm_m], 0),
        jnp.stack([s["dt_bias"] for s in sm_m], 0), jnp.stack([s["a_log"] for s in sm_m], 0),
        jnp.stack([s["d"] for s in sm_m], 0), jnp.stack([s["norm_w"] for s in sm_m], 0),
        d_kv_nw, d_bkv,
        jnp.stack([s["b_q"] for s in sm_a], 0), jnp.stack([s["sinks"] for s in sm_a], 0),
        jnp.stack([s["b_o"] for s in sm_a], 0), accf[0], accf[1],
    ]
    packed3, spans3, _ = _pack_rows(small_list)
    nrow3 = packed3.shape[0]
    g3 = small_all_gather(packed3).reshape(N_DEV, nrow3, D)
    (p_mod, p_kvm, p_fnw, p_mnw, p_cw, p_cb, p_dtb, p_al, p_d, p_snw, p_kvnw, p_bkv, p_bq, p_sk, p_bo, p_fin,
     p_loss) = _unpack_rows(g3, spans3)

    loss = 0.5 / D * jnp.sum(p_loss)

    c_act_t = c_act.T
    dmod_loc = _shard_last(p_mod, me).transpose(1, 0, 2)
    dkvm_loc = _shard_last(p_kvm, me).reshape(1, N_DEV, kvm_cols)
    gp_mod_w = mod_wgrad(c_act_t, dmod_loc, "mod_wgrad")
    gp_kvm_w = mod_wgrad(c_act_t, dkvm_loc, "kv_mod_wgrad")[0]

    def as_parts_single(a):
        return a[None]

    def upd(name, parts, w, m, v):
        shp = w.shape
        c_last = shp[-1]
        out = adamw(parts.reshape(parts.shape[0], -1, c_last), w.reshape(-1, c_last), m.reshape(-1, c_last),
                    v.reshape(-1, c_last), "adamw_" + name)
        return tuple(o.reshape(shp) for o in out)

    views = {
        "ffn_w_gu": [jnp.swapaxes(t, 2, 3).reshape(-1, D) for t in (ffn_w_gu, m_ffn_w_gu, v_ffn_w_gu)],
        "ffn_w_down": [t.reshape(-1, D) for t in (ffn_w_down, m_ffn_w_down, v_ffn_w_down)],
        "ssm_w_out": [t.reshape(-1, D) for t in (ssm_w_out, m_ssm_w_out, v_ssm_w_out)],
        "attn_w_q": [t.reshape(-1, D) for t in (attn_w_q, m_attn_w_q, v_attn_w_q)],
        "attn_w_o": [t.reshape(-1, D) for t in (attn_w_o, m_attn_w_o, v_attn_w_o)],
    }
    filled = {k: None for k in views}

    def upd_rows(name, parts, row0):
        w, m, v = views[name]
        filled[name] = adamw(parts, w, m, v, "adamw_" + name, row0=row0, prev=filled[name])
        return filled[name][3]

    def fence(arrs):
        tot = jnp.zeros((1, 1), F32)
        for a in arrs:
            tot = tot + lax.slice(a, (0,) * a.ndim, (1,) * a.ndim).reshape(1, 1)
        return jnp.broadcast_to(tot, (8, 128))

    res = {}
    res["ffn_norm_w"] = upd("ffn_norm_w", _shard_last(p_fnw, me), ffn_norm_w, m_ffn_norm_w, v_ffn_norm_w)
    res["mod_w"] = upd("mod_w", as_parts_single(gp_mod_w), mod_w, m_mod_w, v_mod_w)
    res["mod_b"] = upd("mod_b", p_mod, mod_b, m_mod_b, v_mod_b)
    res["mix_norm_w"] = upd("mix_norm_w", p_mnw, mix_norm_w, m_mix_norm_w, v_mix_norm_w)
    res["ssm_conv_w"] = upd("ssm_conv_w", _shard_last(p_cw, me), ssm_conv_w, m_ssm_conv_w, v_ssm_conv_w)
    res["ssm_conv_b"] = upd("ssm_conv_b", _shard_last(p_cb, me), ssm_conv_b, m_ssm_conv_b, v_ssm_conv_b)
    res["ssm_dt_bias"] = upd("ssm_dt_bias", p_dtb, ssm_dt_bias, m_ssm_dt_bias, v_ssm_dt_bias)
    res["ssm_a_log"] = upd("ssm_a_log", p_al, ssm_a_log, m_ssm_a_log, v_ssm_a_log)
    res["ssm_d"] = upd("ssm_d", p_d, ssm_d, m_ssm_d, v_ssm_d)
    res["ssm_norm_w"] = upd("ssm_norm_w", _shard_last(p_snw, me), ssm_norm_w, m_ssm_norm_w, v_ssm_norm_w)
    res["kv_norm_w"] = upd("kv_norm_w", p_kvnw.reshape(N_DEV, 1, D), kv_norm_w.reshape(1, D),
                           m_kv_norm_w.reshape(1, D), v_kv_norm_w.reshape(1, D))
    res["kv_mod_w"] = upd("kv_mod_w", as_parts_single(gp_kvm_w), kv_mod_w, m_kv_mod_w, v_kv_mod_w)
    res["kv_mod_b"] = upd("kv_mod_b", p_kvm.reshape(N_DEV, 1, 2 * D), kv_mod_b.reshape(1, -1),
                          m_kv_mod_b.reshape(1, -1), v_kv_mod_b.reshape(1, -1))
    res["b_kv"] = upd("b_kv", p_bkv.reshape(N_DEV, 1, KV_DIM), b_kv.reshape(1, -1), m_b_kv.reshape(1, -1),
                      v_b_kv.reshape(1, -1))
    res["attn_b_q"] = upd("attn_b_q", p_bq, attn_b_q, m_attn_b_q, v_attn_b_q)
    res["attn_sinks"] = upd("attn_sinks", p_sk, attn_sinks, m_attn_sinks, v_attn_sinks)
    res["attn_b_o"] = upd("attn_b_o", p_bo, attn_b_o, m_attn_b_o, v_attn_b_o)
    res["final_norm_w"] = upd("final_norm_w", p_fin.reshape(N_DEV, 1, D), final_norm_w.reshape(1, D),
                              m_final_norm_w.reshape(1, D), v_final_norm_w.reshape(1, D))

    chain = fence([dx] + [t[3] for t in res.values()])
    r_in_parts = [None] * N_A
    r_kv = None
    for handle, tag in exchanges:
        got = xfer_wait(handle, chain, "exch_wait_%s" % tag)
        i = int(tag[1])
        if tag[0] == "f":
            jf = int(tag[2])
            done = [upd_rows("ffn_w_gu", got[0], (2 * i + jf) * S), upd_rows("ffn_w_down", got[1], (2 * i + jf) * r_dn)]
            if len(got) > 2:
                res["w_kv"] = upd("w_kv", got[2], w_kv, m_w_kv, v_w_kv)
                done.append(res["w_kv"][3])
        elif i < N_A:
            r_in_parts[i] = got[0]
            done = [upd_rows("ssm_w_out", got[1], i * r_mix)]
            if i == 0:
                win_out = adamw(jnp.stack(r_in_parts, axis=2).reshape(N_DEV, s_in * N_A, D),
                                *[jnp.transpose(t, (2, 0, 1)).reshape(-1, D) for t in (ssm_w_in, m_ssm_w_in, v_ssm_w_in)],
                                "adamw_ssm_w_in")
                res["ssm_w_in"] = tuple(jnp.transpose(t.reshape(win_t.shape), (1, 2, 0)) for t in win_out)
                done.append(win_out[3])
        else:
            done = [upd_rows("attn_w_q", got[0], (i - N_A) * r_at), upd_rows("attn_w_o", got[1], (i - N_A) * r_at)]
        chain = fence(done)

    res["ffn_w_gu"] = tuple(jnp.swapaxes(t.reshape(gu_t.shape), 2, 3) for t in filled["ffn_w_gu"])
    res["ffn_w_down"] = tuple(t.reshape(ffn_w_down.shape) for t in filled["ffn_w_down"])
    res["ssm_w_out"] = tuple(t.reshape(ssm_w_out.shape) for t in filled["ssm_w_out"])
    res["attn_w_q"] = tuple(t.reshape(attn_w_q.shape) for t in filled["attn_w_q"])
    res["attn_w_o"] = tuple(t.reshape(attn_w_o.shape) for t in filled["attn_w_o"])

    names = ["ffn_norm_w", "ffn_w_gu", "ffn_w_down", "mod_w", "mod_b", "mix_norm_w", "ssm_w_in", "ssm_conv_w",
             "ssm_conv_b", "ssm_dt_bias", "ssm_a_log", "ssm_d", "ssm_norm_w", "ssm_w_out", "kv_norm_w", "kv_mod_w",
             "kv_mod_b", "w_kv", "b_kv", "attn_w_q", "attn_b_q", "attn_sinks", "attn_w_o", "attn_b_o", "final_norm_w"]
    vec_shapes = {"kv_norm_w": (D,), "kv_mod_b": (2 * D,), "b_kv": (KV_DIM,), "final_norm_w": (D,)}
    outs = [loss, grad_x]
    for k in range(4):
        for nme in names:
            t = res[nme][k]
            if nme in vec_shapes:
                t = t.reshape(vec_shapes[nme])
            outs.append(t)
    return tuple(outs)
```

```python
import functools

import jax
import jax.numpy as jnp
from jax import lax
from jax.experimental import pallas as pl
from jax.experimental.pallas import tpu as pltpu

F32 = jnp.float32
BF = jnp.bfloat16
MESH = pl.DeviceIdType.MESH

N_DEV = 8
D_MODEL = 1024
DEPTH = 4
N_A = 2
EPS = 1e-5
N_MOD = 9
D_FF = 2816
D_INNER = 2048
SSM_HEADDIM = 64
SSM_HEADS = 32
SSM_GROUPS = 8
SSM_STATE = 128
CONV_WIDTH = 4
CHUNK = 512
CONV_DIM = D_INNER + 2 * SSM_GROUPS * SSM_STATE
IN_PROJ = D_INNER + CONV_DIM + SSM_HEADS
IN_PROJ_PAD = D_INNER + CONV_DIM + 128
ATT_HEADS = 16
KV_HEADS = 4
HEAD_DIM = 64
WINDOW = 128
KV_DIM = 2 * KV_HEADS * HEAD_DIM

ADAM_LR = 0.001
ADAM_B1 = 0.9
ADAM_B2 = 0.999
ADAM_EPS = 1e-08
ADAM_WD = 0.01
ADAM_STEP = 10

VMEM_LIMIT = 48 * 2 ** 20
ADAMW_VMEM_BUDGET = 24 * 2 ** 20
NEG = -1e30


def _call(body, name, grid, in_specs, out_specs, out_shape, scratch=()):
    return pl.pallas_call(
        body, name=name, grid=grid, in_specs=in_specs, out_specs=out_specs, out_shape=out_shape,
        scratch_shapes=list(scratch),
        compiler_params=pltpu.CompilerParams(vmem_limit_bytes=VMEM_LIMIT))


def _tile(n, cap):
    t = (cap // 128) * 128
    while t >= 128:
        if n % t == 0:
            return t
        t -= 128
    return n


def _sds(shape, dtype):
    return jax.ShapeDtypeStruct(shape, dtype)


def _sigmoid(v):
    return 1.0 / (1.0 + jnp.exp(-v))


def _dot(a, b, dims):
    return lax.dot_general(a, b, (dims, ((), ())), preferred_element_type=F32)


def _dot_nn(a, b):
    return _dot(a.astype(BF), b.astype(BF), ((1,), (0,)))


def _dot_nt(a, b):
    return _dot(a.astype(BF), b.astype(BF), ((1,), (1,)))


def _dot_tn(a, b):
    return _dot(a.astype(BF), b.astype(BF), ((0,), (0,)))


def matmul(a, b, mode, out_dtype, name, bias=None, res=None, gate=None, coef=1.0):
    if mode == "nn":
        (M, K), (_, N) = a.shape, b.shape
    elif mode == "nt":
        (M, K), (N, _) = a.shape, b.shape
    else:
        (K, M), (_, N) = a.shape, b.shape
    cap_n = 512 if K > 4096 else 1024
    tm = _tile(M, 1024 if (mode == "tn" or K <= D_FF) else 512)
    tn = _tile(N, cap_n)
    if mode != "tn" and tm * tn > 1024 * 896:
        tn = _tile(N, 512)
    if mode == "nn":
        a_spec = pl.BlockSpec((tm, K), lambda i, j: (i, 0))
        b_spec = pl.BlockSpec((K, tn), lambda i, j: (0, j))
        fn = _dot_nn
    elif mode == "nt":
        a_spec = pl.BlockSpec((tm, K), lambda i, j: (i, 0))
        b_spec = pl.BlockSpec((tn, K), lambda i, j: (j, 0))
        fn = _dot_nt
    else:
        a_spec = pl.BlockSpec((K, tm), lambda i, j: (0, i))
        b_spec = pl.BlockSpec((K, tn), lambda i, j: (0, j))
        fn = _dot_tn
    has_bias, has_res = bias is not None, res is not None
    o_spec = pl.BlockSpec((tm, tn), lambda i, j: (i, j))
    v_spec = pl.BlockSpec((1, tn), lambda i, j: (0, j))
    in_specs, args = [a_spec, b_spec], [a, b]
    if has_bias:
        in_specs.append(v_spec)
        args.append(bias)
    if has_res:
        in_specs += [o_spec, v_spec]
        args += [res, gate]

    def body(*refs):
        a_ref, b_ref = refs[0], refs[1]
        k = 2
        y = fn(a_ref[...], b_ref[...])
        if has_bias:
            y = y + refs[k][...]
            k += 1
        if has_res:
            res_ref, gate_ref = refs[k], refs[k + 1]
            refs[k + 2][...] = y.astype(out_dtype)
            refs[k + 3][...] = res_ref[...] + coef * gate_ref[...] * y
        else:
            refs[k][...] = y.astype(out_dtype)

    if has_res:
        out_shape = (_sds((M, N), out_dtype), _sds((M, N), F32))
        out_specs = (o_spec, o_spec)
    else:
        out_shape = _sds((M, N), out_dtype)
        out_specs = o_spec
    return _call(body, name, (M // tm, N // tn), in_specs, out_specs, out_shape)(*args)


def norm_mod(x, nw, sh, sc, name):
    L, D = x.shape
    tm = _tile(L, 512)

    def body(x_ref, nw_ref, sh_ref, sc_ref, h_ref):
        xf = x_ref[...]
        r = lax.rsqrt(jnp.mean(xf * xf, axis=-1, keepdims=True) + EPS)
        n = xf * r * nw_ref[...]
        h_ref[...] = (n * (1.0 + sc_ref[...]) + sh_ref[...]).astype(BF)

    row = pl.BlockSpec((tm, D), lambda i: (i, 0))
    vec = pl.BlockSpec((1, D), lambda i: (0, 0))
    return _call(body, name, (L // tm,), [row, vec, vec, vec], row, _sds((L, D), BF))(x, nw, sh, sc)


def norm_mod_bwd(x, dh, dres, nw, sc, name):
    L, D = x.shape
    tm = _tile(L, 512)

    def body(x_ref, dh_ref, dres_ref, nw_ref, sc_ref, dx_ref, acc_ref):
        @pl.when(pl.program_id(0) == 0)
        def _():
            acc_ref[...] = jnp.zeros_like(acc_ref)

        xf = x_ref[...]
        dhf = dh_ref[...].astype(F32)
        r = lax.rsqrt(jnp.mean(xf * xf, axis=-1, keepdims=True) + EPS)
        xhat = xf * r
        nwv = nw_ref[...]
        dn = dhf * (1.0 + sc_ref[...])
        dxhat = dn * nwv
        proj = jnp.mean(dxhat * xhat, axis=-1, keepdims=True)
        dx_ref[...] = dres_ref[...] + r * (dxhat - xhat * proj)
        acc_ref[0:1, :] += jnp.sum(dhf, axis=0, keepdims=True)
        acc_ref[1:2, :] += jnp.sum(dhf * xhat * nwv, axis=0, keepdims=True)
        acc_ref[2:3, :] += jnp.sum(dn * xhat, axis=0, keepdims=True)

    row = pl.BlockSpec((tm, D), lambda i: (i, 0))
    vec = pl.BlockSpec((1, D), lambda i: (0, 0))
    acc = pl.BlockSpec((8, D), lambda i: (0, 0))
    return _call(body, name, (L // tm,), [row, row, row, vec, vec], (row, acc),
                 (_sds((L, D), F32), _sds((8, D), F32)))(x, dh, dres, nw, sc)


def final_loss(x, nw, target):
    L, D = x.shape
    tm = _tile(L, 512)

    def body(x_ref, nw_ref, t_ref, dx_ref, acc_ref):
        @pl.when(pl.program_id(0) == 0)
        def _():
            acc_ref[...] = jnp.zeros_like(acc_ref)

        xf = x_ref[...]
        r = lax.rsqrt(jnp.mean(xf * xf, axis=-1, keepdims=True) + EPS)
        xhat = xf * r
        nwv = nw_ref[...]
        err = xhat * nwv - t_ref[...]
        dy = err * (1.0 / D)
        dxhat = dy * nwv
        proj = jnp.mean(dxhat * xhat, axis=-1, keepdims=True)
        dx_ref[...] = r * (dxhat - xhat * proj)
        acc_ref[0:1, :] += jnp.sum(dy * xhat, axis=0, keepdims=True)
        acc_ref[1:2, :] += jnp.sum(err * err, axis=0, keepdims=True)

    row = pl.BlockSpec((tm, D), lambda i: (i, 0))
    vec = pl.BlockSpec((1, D), lambda i: (0, 0))
    acc = pl.BlockSpec((8, D), lambda i: (0, 0))
    return _call(body, "final_loss", (L // tm,), [row, vec, row], (row, acc),
                 (_sds((L, D), F32), _sds((8, D), F32)))(x, nw, target)


def resid_gate_bwd(dxo, y, gate, coef, name):
    L, D = dxo.shape
    tm = _tile(L, 512)

    def body(dxo_ref, y_ref, g_ref, dy_ref, acc_ref):
        @pl.when(pl.program_id(0) == 0)
        def _():
            acc_ref[...] = jnp.zeros_like(acc_ref)

        d = dxo_ref[...]
        dy = coef * g_ref[...] * d
        dy_ref[...] = dy.astype(BF)
        acc_ref[0:1, :] += coef * jnp.sum(d * y_ref[...].astype(F32), axis=0, keepdims=True)
        acc_ref[1:2, :] += jnp.sum(dy, axis=0, keepdims=True)

    row = pl.BlockSpec((tm, D), lambda i: (i, 0))
    vec = pl.BlockSpec((1, D), lambda i: (0, 0))
    acc = pl.BlockSpec((8, D), lambda i: (0, 0))
    return _call(body, name, (L // tm,), [row, row, vec], (row, acc),
                 (_sds((L, D), BF), _sds((8, D), F32)))(dxo, y, gate)


def ffn_up(h, wt):
    L, D = h.shape
    F = wt.shape[0] // 2
    tm, tn = _tile(L, 2048), _tile(F, 256)
    nj = F // tn

    def body(h_ref, wg_ref, wu_ref, g_ref, u_ref, a_ref):
        hv = h_ref[...]
        g = _dot_nt(hv, wg_ref[...])
        u = _dot_nt(hv, wu_ref[...])
        g_ref[...] = g.astype(BF)
        u_ref[...] = u.astype(BF)
        a_ref[...] = (g * _sigmoid(g) * u).astype(BF)

    o = pl.BlockSpec((tm, tn), lambda i, n: (i, n))
    return _call(body, "ffn_up", (L // tm, nj),
                 [pl.BlockSpec((tm, D), lambda i, n: (i, 0)),
                  pl.BlockSpec((tn, D), lambda i, n: (n, 0)),
                  pl.BlockSpec((tn, D), lambda i, n: (n + nj, 0))],
                 (o, o, o), tuple(_sds((L, F), BF) for _ in range(3)))(h, wt, wt)


def ffn_down_dgrad(dy, wd, g, u):
    L, D = dy.shape
    F = wd.shape[0]
    tm, tn = _tile(L, 2048), _tile(F, 256)

    def body(dy_ref, w_ref, g_ref, u_ref, dg_ref, du_ref):
        da = _dot_nt(dy_ref[...], w_ref[...])
        gv = g_ref[...].astype(F32)
        uv = u_ref[...].astype(F32)
        s = _sigmoid(gv)
        dg_ref[...] = (da * uv * s * (1.0 + gv * (1.0 - s))).astype(BF)
        du_ref[...] = (da * gv * s).astype(BF)

    o = pl.BlockSpec((tm, tn), lambda i, n: (i, n))
    return _call(body, "ffn_down_dgrad", (L // tm, F // tn),
                 [pl.BlockSpec((tm, D), lambda i, n: (i, 0)), pl.BlockSpec((tn, D), lambda i, n: (n, 0)), o, o],
                 (o, o), (_sds((L, F), BF), _sds((L, F), BF)))(dy, wd, g, u)


def ffn_up_wgrad(dg, du, h):
    L, F = dg.shape
    D = h.shape[1]
    tm = _tile(F, 256)
    nblk = F // tm

    def half(d, off, prev):
        def body(d_ref, h_ref, *rest):
            rest[-1][...] = _dot_tn(d_ref[...], h_ref[...]).astype(BF)

        in_specs = [pl.BlockSpec((L, tm), lambda i: (0, i)), pl.BlockSpec((L, D), lambda i: (0, 0))]
        args = [d, h]
        aliases = {}
        if prev is not None:
            in_specs.append(pl.BlockSpec(memory_space=pl.ANY))
            args.append(prev)
            aliases = {2: 0}
        return pl.pallas_call(
            body, name="ffn_up_wgrad", grid=(nblk,), in_specs=in_specs,
            out_specs=pl.BlockSpec((tm, D), lambda i: (i + off * nblk, 0)),
            out_shape=_sds((2 * F, D), BF), input_output_aliases=aliases,
            compiler_params=pltpu.CompilerParams(vmem_limit_bytes=VMEM_LIMIT))(*args)

    return half(du, 1, half(dg, 0, None))


def ffn_up_dgrad(dg, du, wt):
    L, F = dg.shape
    D = wt.shape[1]
    tm, tn = _tile(L, 1024), _tile(D, 512)

    def body(dg_ref, du_ref, wg_ref, wu_ref, o_ref):
        o_ref[...] = _dot_nn(dg_ref[...], wg_ref[...]) + _dot_nn(du_ref[...], wu_ref[...])

    a = pl.BlockSpec((tm, F), lambda i, n: (i, 0))
    return _call(body, "ffn_up_dgrad", (L // tm, D // tn),
                 [a, a, pl.BlockSpec((F, tn), lambda i, n: (0, n)), pl.BlockSpec((F, tn), lambda i, n: (1, n))],
                 pl.BlockSpec((tm, tn), lambda i, n: (i, n)), _sds((L, D), F32))(dg, du, wt, wt)


def _shift_rows(cur, other, k, down):
    n = cur.shape[0]
    rows = lax.broadcasted_iota(jnp.int32, cur.shape, 0)
    if down:
        return jnp.where(rows < k, pltpu.roll(other, k, 0), pltpu.roll(cur, k, 0))
    return jnp.where(rows >= n - k, pltpu.roll(other, n - k, 0), pltpu.roll(cur, n - k, 0))


def _conv_pre(cur, prev, w_ref, b_ref):
    s = cur * w_ref[CONV_WIDTH - 1:CONV_WIDTH, :] + b_ref[...]
    for k in range(1, CONV_WIDTH):
        s = s + _shift_rows(cur, prev, k, True) * w_ref[CONV_WIDTH - 1 - k:CONV_WIDTH - k, :]
    return s


_XBC_COL0 = D_INNER // 512


def conv_fwd(zx, w, b):
    L = zx.shape[0]
    tm, tc = _tile(L, 256), 512

    def body(cur_ref, prev_ref, w_ref, b_ref, o_ref):
        cur = cur_ref[...]
        prev = jnp.where(pl.program_id(1) > 0, prev_ref[...], 0.0)
        s = _conv_pre(cur, prev, w_ref, b_ref)
        o_ref[...] = s * _sigmoid(s)

    return _call(body, "conv_fwd", (CONV_DIM // tc, L // tm),
                 [pl.BlockSpec((tm, tc), lambda j, i: (i, _XBC_COL0 + j)),
                  pl.BlockSpec((tm, tc), lambda j, i: (jnp.maximum(i - 1, 0), _XBC_COL0 + j)),
                  pl.BlockSpec((CONV_WIDTH, tc), lambda j, i: (0, j)),
                  pl.BlockSpec((1, tc), lambda j, i: (0, j))],
                 pl.BlockSpec((tm, tc), lambda j, i: (i, j)), _sds((L, CONV_DIM), F32))(zx, zx, w, b)


def conv_bwd_act(dxc, zx, w, b):
    L = zx.shape[0]
    tm, tc = _tile(L, 256), 512

    def body(d_ref, cur_ref, prev_ref, w_ref, b_ref, o_ref):
        cur = cur_ref[...]
        prev = jnp.where(pl.program_id(1) > 0, prev_ref[...], 0.0)
        s = _conv_pre(cur, prev, w_ref, b_ref)
        sg = _sigmoid(s)
        o_ref[...] = d_ref[...] * sg * (1.0 + s * (1.0 - sg))

    return _call(body, "conv_bwd_act", (CONV_DIM // tc, L // tm),
                 [pl.BlockSpec((tm, tc), lambda j, i: (i, j)),
                  pl.BlockSpec((tm, tc), lambda j, i: (i, _XBC_COL0 + j)),
                  pl.BlockSpec((tm, tc), lambda j, i: (jnp.maximum(i - 1, 0), _XBC_COL0 + j)),
                  pl.BlockSpec((CONV_WIDTH, tc), lambda j, i: (0, j)),
                  pl.BlockSpec((1, tc), lambda j, i: (0, j))],
                 pl.BlockSpec((tm, tc), lambda j, i: (i, j)), _sds((L, CONV_DIM), F32))(dxc, zx, zx, w, b)


def conv_bwd(ds, zx, w):
    L = zx.shape[0]
    tm, tc = _tile(L, 256), 512
    nblk = L // tm

    def body(ds_ref, dsn_ref, cur_ref, prev_ref, w_ref, du_ref, acc_ref):
        i = pl.program_id(1)

        @pl.when(i == 0)
        def _():
            acc_ref[...] = jnp.zeros_like(acc_ref)

        ds_c = ds_ref[...]
        ds_n = jnp.where(i < nblk - 1, dsn_ref[...], 0.0)
        cur = cur_ref[...]
        prev = jnp.where(i > 0, prev_ref[...], 0.0)
        du = ds_c * w_ref[CONV_WIDTH - 1:CONV_WIDTH, :]
        acc_ref[CONV_WIDTH - 1:CONV_WIDTH, :] += jnp.sum(ds_c * cur, axis=0, keepdims=True)
        for k in range(1, CONV_WIDTH):
            du = du + _shift_rows(ds_c, ds_n, k, False) * w_ref[CONV_WIDTH - 1 - k:CONV_WIDTH - k, :]
            acc_ref[CONV_WIDTH - 1 - k:CONV_WIDTH - k, :] += jnp.sum(
                ds_c * _shift_rows(cur, prev, k, True), axis=0, keepdims=True)
        acc_ref[CONV_WIDTH:CONV_WIDTH + 1, :] += jnp.sum(ds_c, axis=0, keepdims=True)
        du_ref[...] = du.astype(BF)

    return _call(body, "conv_bwd", (CONV_DIM // tc, nblk),
                 [pl.BlockSpec((tm, tc), lambda j, i: (i, j)),
                  pl.BlockSpec((tm, tc), lambda j, i: (jnp.minimum(i + 1, nblk - 1), j)),
                  pl.BlockSpec((tm, tc), lambda j, i: (i, _XBC_COL0 + j)),
                  pl.BlockSpec((tm, tc), lambda j, i: (jnp.maximum(i - 1, 0), _XBC_COL0 + j)),
                  pl.BlockSpec((CONV_WIDTH, tc), lambda j, i: (0, j))],
                 (pl.BlockSpec((tm, tc), lambda j, i: (i, j)), pl.BlockSpec((8, tc), lambda j, i: (0, j))),
                 (_sds((L, CONV_DIM), BF), _sds((8, CONV_DIM), F32)))(ds, ds, zx, zx, w)


_DT_COL = (D_INNER + CONV_DIM) // 128


def dt_prep(zx, bias_pad, alog_pad):
    L = zx.shape[0]

    def body(raw_ref, b_ref, al_ref, dt_ref, acs_ref):
        v = raw_ref[...] + b_ref[...]
        dt = jnp.maximum(v, 0.0) + jnp.log(1.0 + jnp.exp(-jnp.abs(v)))
        dt_ref[...] = dt
        acs = dt * (-jnp.exp(al_ref[...]))
        rows = lax.broadcasted_iota(jnp.int32, acs.shape, 0)
        s = 1
        while s < CHUNK:
            acs = acs + jnp.where(rows >= s, pltpu.roll(acs, s, 0), 0.0)
            s *= 2
        acs_ref[...] = acs

    blk = pl.BlockSpec((CHUNK, 128), lambda i: (i, 0))
    vec = pl.BlockSpec((1, 128), lambda i: (0, 0))
    return _call(body, "dt_prep", (L // CHUNK,),
                 [pl.BlockSpec((CHUNK, 128), lambda i: (i, _DT_COL)), vec, vec], (blk, blk),
                 (_sds((L, 128), F32), _sds((L, 128), F32)))(zx, bias_pad, alog_pad)


def dt_bwd(ddt, da, dt, zx, bias_pad, alog_pad):
    L = zx.shape[0]
    tm = _tile(L, 512)

    def body(ddt_ref, da_ref, dt_ref, raw_ref, b_ref, al_ref, o_ref, acc_ref):
        @pl.when(pl.program_id(0) == 0)
        def _():
            acc_ref[...] = jnp.zeros_like(acc_ref)

        A = -jnp.exp(al_ref[...])
        dav = da_ref[...]
        dd = ddt_ref[...] + dav * A
        draw = dd * _sigmoid(raw_ref[...] + b_ref[...])
        o_ref[...] = draw.astype(BF)
        acc_ref[0:1, :] += jnp.sum(draw, axis=0, keepdims=True)
        acc_ref[1:2, :] += jnp.sum(dav * dt_ref[...], axis=0, keepdims=True) * A

    blk = pl.BlockSpec((tm, 128), lambda i: (i, 0))
    vec = pl.BlockSpec((1, 128), lambda i: (0, 0))
    return _call(body, "dt_bwd", (L // tm,),
                 [blk, blk, blk, pl.BlockSpec((tm, 128), lambda i: (i, _DT_COL)), vec, vec],
                 (blk, pl.BlockSpec((8, 128), lambda i: (0, 0))),
                 (_sds((L, 128), BF), _sds((8, 128), F32)))(ddt, da, dt, zx, bias_pad, alog_pad)


_HPG = SSM_HEADS // SSM_GROUPS
_GW = _HPG * SSM_HEADDIM
_B_COL0 = D_INNER // SSM_STATE
_C_COL0 = (D_INNER + SSM_GROUPS * SSM_STATE) // SSM_STATE


def _ssd_head(x, dtc, ac, ar, r, causal):
    xh = x[:, SSM_HEADDIM * r:SSM_HEADDIM * (r + 1)]
    acol = ac[:, r:r + 1]
    arow = ar[r:r + 1, :]
    alast = ar[r:r + 1, CHUNK - 1:CHUNK]
    lm = jnp.exp(jnp.where(causal, acol - arow, NEG))
    return xh, xh * dtc[:, r:r + 1], acol, alast, lm


def ssd_fwd(xc, dt_g, acs_g, acsT_g, d_exp):
    L = xc.shape[0]
    nc = L // CHUNK

    def body(x_ref, b_ref, c_ref, dt_ref, ac_ref, ar_ref, d_ref, y_ref, pst_ref, st_ref):
        @pl.when(pl.program_id(1) == 0)
        def _():
            st_ref[...] = jnp.zeros_like(st_ref)

        x, Bm, Cm = x_ref[...], b_ref[...], c_ref[...]
        dtc, ac, ar = dt_ref[...], ac_ref[...], ar_ref[...]
        causal = lax.broadcasted_iota(jnp.int32, (CHUNK, CHUNK), 0) >= lax.broadcasted_iota(jnp.int32, (CHUNK, CHUNK), 1)
        CB = _dot_nt(Cm, Bm)
        for r in range(_HPG):
            xh, xd, acol, alast, lm = _ssd_head(x, dtc, ac, ar, r, causal)
            P = st_ref[r]
            y = _dot_nn(CB * lm, xd) + jnp.exp(acol) * _dot_nt(Cm, P)
            y_ref[:, SSM_HEADDIM * r:SSM_HEADDIM * (r + 1)] = y + d_ref[:, SSM_HEADDIM * r:SSM_HEADDIM * (r + 1)] * xh
            pst_ref[r] = P
            st_ref[r] = P * jnp.exp(alast) + _dot_tn(xd * jnp.exp(alast - acol), Bm)

    return _call(
        body, "ssd_fwd", (SSM_GROUPS, nc),
        [pl.BlockSpec((CHUNK, _GW), lambda g, c: (c, g)),
         pl.BlockSpec((CHUNK, SSM_STATE), lambda g, c: (c, _B_COL0 + g)),
         pl.BlockSpec((CHUNK, SSM_STATE), lambda g, c: (c, _C_COL0 + g)),
         pl.BlockSpec((None, CHUNK, _HPG), lambda g, c: (g, c, 0)),
         pl.BlockSpec((None, CHUNK, _HPG), lambda g, c: (g, c, 0)),
         pl.BlockSpec((None, _HPG, CHUNK), lambda g, c: (g, 0, c)),
         pl.BlockSpec((None, 1, _GW), lambda g, c: (g, 0, 0))],
        (pl.BlockSpec((CHUNK, _GW), lambda g, c: (c, g)),
         pl.BlockSpec((None, None, _HPG, SSM_HEADDIM, SSM_STATE), lambda g, c: (c, g, 0, 0, 0))),
        (_sds((L, D_INNER), F32), _sds((nc, SSM_GROUPS, _HPG, SSM_HEADDIM, SSM_STATE), F32)),
        scratch=[pltpu.VMEM((_HPG, SSM_HEADDIM, SSM_STATE), F32)],
    )(xc, xc, xc, dt_g, acs_g, acsT_g, d_exp)


def ssd_bwd(dy, xc, dt_g, acs_g, acsT_g, pst, d_exp):
    L = xc.shape[0]
    nc = L // CHUNK

    def body(dy_ref, x_ref, b_ref, c_ref, dt_ref, ac_ref, ar_ref, pst_ref, d_ref,
             dx_ref, db_ref, dc_ref, ddt_ref, da_ref, dd_ref, dp_ref):
        @pl.when(pl.program_id(1) == 0)
        def _():
            dp_ref[...] = jnp.zeros_like(dp_ref)
            dd_ref[...] = jnp.zeros_like(dd_ref)

        dyv, x, Bm, Cm = dy_ref[...], x_ref[...], b_ref[...], c_ref[...]
        dtc, ac, ar = dt_ref[...], ac_ref[...], ar_ref[...]
        ri = lax.broadcasted_iota(jnp.int32, (CHUNK, CHUNK), 0)
        ci = lax.broadcasted_iota(jnp.int32, (CHUNK, CHUNK), 1)
        causal = ri >= ci
        lane4 = lax.broadcasted_iota(jnp.int32, (CHUNK, _HPG), 1)
        CB = _dot_nt(Cm, Bm)
        dB = jnp.zeros((CHUNK, SSM_STATE), F32)
        dC = jnp.zeros((CHUNK, SSM_STATE), F32)
        dCB = jnp.zeros((CHUNK, CHUNK), F32)
        ddt_blk = jnp.zeros((CHUNK, _HPG), F32)
        da_blk = jnp.zeros((CHUNK, _HPG), F32)
        for r in range(_HPG):
            sl = slice(SSM_HEADDIM * r, SSM_HEADDIM * (r + 1))
            xh, xd, acol, alast, lm = _ssd_head(x, dtc, ac, ar, r, causal)
            dyh = dyv[:, sl]
            P = pst_ref[r]
            dPn = dp_ref[r]
            eA = jnp.exp(acol)
            cd = jnp.exp(alast)
            dte = jnp.exp(alast - acol)
            G = CB * lm
            Z = _dot_nt(Cm, P)
            dZ = eA * dyh
            dC = dC + _dot_nn(dZ, P)
            dp_ref[r] = dPn * cd + _dot_tn(dZ, Cm)
            dA_col = jnp.sum(dZ * Z, axis=1, keepdims=True)
            BdS = _dot_nt(Bm, dPn)
            dxd = dte * BdS
            dB = dB + dte * _dot_nn(xd, dPn)
            t = jnp.sum(xd * BdS, axis=1, keepdims=True) * dte
            dA_col = dA_col - t
            dA_last = jnp.sum(t, axis=0, keepdims=True) + jnp.sum(
                jnp.sum(dPn * P, axis=1, keepdims=True), axis=0, keepdims=True) * cd
            dG = _dot_nt(dyh, xd)
            dxd = dxd + _dot_tn(G, dyh)
            dCB = dCB + dG * lm
            W = dG * G
            dA_col = dA_col + jnp.sum(W, axis=1, keepdims=True)
            dA_row = jnp.sum(jnp.where(ri == ci, dA_col, 0.0), axis=0, keepdims=True) - jnp.sum(W, axis=0, keepdims=True)
            da_col = jnp.sum(jnp.where(ci >= ri, dA_row, 0.0), axis=1, keepdims=True) + dA_last
            da_blk = jnp.where(lane4 == r, da_col, da_blk)
            ddt_blk = jnp.where(lane4 == r, jnp.sum(dxd * xh, axis=1, keepdims=True), ddt_blk)
            dx_ref[:, sl] = dxd * dtc[:, r:r + 1] + d_ref[:, sl] * dyh
        dc_ref[...] = dC + _dot_nn(dCB, Bm)
        db_ref[...] = dB + _dot_tn(dCB, Cm)
        ddt_ref[...] = ddt_blk
        da_ref[...] = da_blk
        dd_ref[...] += jnp.sum(dyv * x, axis=0, keepdims=True)

    rc = lambda g, c: (nc - 1 - c, g)
    small = pl.BlockSpec((None, CHUNK, _HPG), lambda g, c: (g, nc - 1 - c, 0))
    return _call(
        body, "ssd_bwd", (SSM_GROUPS, nc),
        [pl.BlockSpec((CHUNK, _GW), rc),
         pl.BlockSpec((CHUNK, _GW), rc),
         pl.BlockSpec((CHUNK, SSM_STATE), lambda g, c: (nc - 1 - c, _B_COL0 + g)),
         pl.BlockSpec((CHUNK, SSM_STATE), lambda g, c: (nc - 1 - c, _C_COL0 + g)),
         small, small,
         pl.BlockSpec((None, _HPG, CHUNK), lambda g, c: (g, 0, nc - 1 - c)),
         pl.BlockSpec((None, None, _HPG, SSM_HEADDIM, SSM_STATE), lambda g, c: (nc - 1 - c, g, 0, 0, 0)),
         pl.BlockSpec((None, 1, _GW), lambda g, c: (g, 0, 0))],
        (pl.BlockSpec((CHUNK, _GW), rc),
         pl.BlockSpec((CHUNK, SSM_STATE), rc),
         pl.BlockSpec((CHUNK, SSM_STATE), rc),
         small, small,
         pl.BlockSpec((None, 1, _GW), lambda g, c: (g, 0, 0))),
        (_sds((L, D_INNER), F32), _sds((L, SSM_GROUPS * SSM_STATE), F32), _sds((L, SSM_GROUPS * SSM_STATE), F32),
         _sds((SSM_GROUPS, L, _HPG), F32), _sds((SSM_GROUPS, L, _HPG), F32), _sds((SSM_GROUPS, 1, _GW), F32)),
        scratch=[pltpu.VMEM((_HPG, SSM_HEADDIM, SSM_STATE), F32)],
    )(dy, xc, xc, xc, dt_g, acs_g, acsT_g, pst, d_exp)


_NGW = D_INNER // SSM_GROUPS


def gate_norm(y, zx, nw):
    L = y.shape[0]
    tm = _tile(L, 256)

    def body(y_ref, z_ref, nw_ref, o_ref):
        for g in range(SSM_GROUPS):
            sl = slice(_NGW * g, _NGW * (g + 1))
            z = z_ref[:, sl]
            y2 = y_ref[:, sl] * (z * _sigmoid(z))
            r = lax.rsqrt(jnp.mean(y2 * y2, axis=-1, keepdims=True) + EPS)
            o_ref[:, sl] = (y2 * r * nw_ref[:, sl]).astype(BF)

    row = pl.BlockSpec((tm, D_INNER), lambda i: (i, 0))
    return _call(body, "gate_norm", (L // tm,), [row, row, pl.BlockSpec((1, D_INNER), lambda i: (0, 0))],
                 row, _sds((L, D_INNER), BF))(y, zx, nw)


def gate_norm_bwd(dyn, y, zx, nw):
    L = y.shape[0]
    tm = _tile(L, 256)

    def body(d_ref, y_ref, z_ref, nw_ref, dy_ref, dz_ref, acc_ref):
        @pl.when(pl.program_id(0) == 0)
        def _():
            acc_ref[...] = jnp.zeros_like(acc_ref)

        for g in range(SSM_GROUPS):
            sl = slice(_NGW * g, _NGW * (g + 1))
            z = z_ref[:, sl]
            yv = y_ref[:, sl]
            sg = _sigmoid(z)
            sz = z * sg
            y2 = yv * sz
            r = lax.rsqrt(jnp.mean(y2 * y2, axis=-1, keepdims=True) + EPS)
            yh = y2 * r
            d = d_ref[:, sl]
            dn = d * nw_ref[:, sl]
            dy2 = r * (dn - yh * jnp.mean(dn * yh, axis=-1, keepdims=True))
            dy_ref[:, sl] = dy2 * sz
            dz_ref[:, sl] = (dy2 * yv * sg * (1.0 + z * (1.0 - sg))).astype(BF)
            acc_ref[0:1, sl] += jnp.sum(d * yh, axis=0, keepdims=True)

    row = pl.BlockSpec((tm, D_INNER), lambda i: (i, 0))
    return _call(body, "gate_norm_bwd", (L // tm,), [row, row, row, pl.BlockSpec((1, D_INNER), lambda i: (0, 0))],
                 (row, row, pl.BlockSpec((8, D_INNER), lambda i: (0, 0))),
                 (_sds((L, D_INNER), F32), _sds((L, D_INNER), BF), _sds((8, D_INNER), F32)))(dyn, y, zx, nw)


_SCALE = HEAD_DIM ** -0.5
_REP = ATT_HEADS // KV_HEADS
_V_OFF = KV_HEADS * HEAD_DIM


def _stack_heads(ref, k):
    return jnp.concatenate([ref[:, HEAD_DIM * (k * _REP + r):HEAD_DIM * (k * _REP + r + 1)] for r in range(_REP)],
                           axis=0)


def _stack_sinks(s_ref, k):
    return jnp.concatenate([jnp.broadcast_to(s_ref[:, k * _REP + r:k * _REP + r + 1], (WINDOW, 1))
                            for r in range(_REP)], axis=0)


def _attn_probs(q4, kp, kc, sink, first):
    shape = (_REP * WINDOW, WINDOW)
    rows = jnp.bitwise_and(lax.broadcasted_iota(jnp.int32, shape, 0), WINDOW - 1)
    cols = lax.broadcasted_iota(jnp.int32, shape, 1)
    sp = jnp.where(jnp.logical_and(cols > rows, jnp.logical_not(first)), _dot_nt(q4, kp) * _SCALE, NEG)
    sc = jnp.where(cols <= rows, _dot_nt(q4, kc) * _SCALE, NEG)
    m = jnp.maximum(jnp.maximum(jnp.max(sp, axis=1, keepdims=True), jnp.max(sc, axis=1, keepdims=True)), sink)
    pp = jnp.exp(sp - m)
    pc = jnp.exp(sc - m)
    ps = jnp.exp(sink - m)
    inv = 1.0 / (jnp.sum(pp, axis=1, keepdims=True) + jnp.sum(pc, axis=1, keepdims=True) + ps)
    return pp * inv, pc * inv, ps * inv


def attn_fwd(q, kv, sinks_pad):
    L = q.shape[0]
    nb = L // WINDOW

    def body(q_ref, kc_ref, kp_ref, s_ref, o_ref):
        first = pl.program_id(0) == 0
        for k in range(KV_HEADS):
            ks = slice(HEAD_DIM * k, HEAD_DIM * (k + 1))
            vs = slice(_V_OFF + HEAD_DIM * k, _V_OFF + HEAD_DIM * (k + 1))
            pp, pc, _ = _attn_probs(_stack_heads(q_ref, k), kp_ref[:, ks], kc_ref[:, ks], _stack_sinks(s_ref, k), first)
            o4 = _dot_nn(pp, kp_ref[:, vs]) + _dot_nn(pc, kc_ref[:, vs])
            for r in range(_REP):
                h = k * _REP + r
                o_ref[:, HEAD_DIM * h:HEAD_DIM * (h + 1)] = o4[WINDOW * r:WINDOW * (r + 1)].astype(BF)

    qspec = pl.BlockSpec((WINDOW, D_MODEL), lambda i: (i, 0))
    return _call(body, "attn_fwd", (nb,),
                 [qspec, pl.BlockSpec((WINDOW, KV_DIM), lambda i: (i, 0)),
                  pl.BlockSpec((WINDOW, KV_DIM), lambda i: (jnp.maximum(i - 1, 0), 0)),
                  pl.BlockSpec((1, 128), lambda i: (0, 0))],
                 qspec, _sds((L, D_MODEL), BF))(q, kv, kv, sinks_pad)


def attn_bwd(q, kv, do, sinks_pad):
    L = q.shape[0]
    nb = L // WINDOW

    def body(q_ref, kc_ref, kp_ref, do_ref, s_ref, dq_ref, dc_ref, dp_ref, acc_ref):
        first = pl.program_id(0) == 0

        @pl.when(first)
        def _():
            acc_ref[...] = jnp.zeros_like(acc_ref)

        lane = lax.broadcasted_iota(jnp.int32, (1, 128), 1)
        dsink = jnp.zeros((1, 128), F32)
        for k in range(KV_HEADS):
            ks = slice(HEAD_DIM * k, HEAD_DIM * (k + 1))
            vs = slice(_V_OFF + HEAD_DIM * k, _V_OFF + HEAD_DIM * (k + 1))
            kp, kc, vp, vc = kp_ref[:, ks], kc_ref[:, ks], kp_ref[:, vs], kc_ref[:, vs]
            q4 = _stack_heads(q_ref, k)
            do4 = _stack_heads(do_ref, k)
            pp, pc, ps = _attn_probs(q4, kp, kc, _stack_sinks(s_ref, k), first)
            dpp = _dot_nt(do4, vp)
            dpc = _dot_nt(do4, vc)
            delta = jnp.sum(pp * dpp, axis=1, keepdims=True) + jnp.sum(pc * dpc, axis=1, keepdims=True)
            dsp = pp * (dpp - delta) * _SCALE
            dsc = pc * (dpc - delta) * _SCALE
            dq4 = _dot_nn(dsp, kp) + _dot_nn(dsc, kc)
            psd = ps * delta
            for r in range(_REP):
                h = k * _REP + r
                rs = slice(WINDOW * r, WINDOW * (r + 1))
                dq_ref[:, HEAD_DIM * h:HEAD_DIM * (h + 1)] = dq4[rs]
                dsink = dsink + jnp.where(lane == h, -jnp.sum(psd[rs], axis=0, keepdims=True), 0.0)
            dp_ref[:, ks] = _dot_tn(dsp, q4)
            dc_ref[:, ks] = _dot_tn(dsc, q4)
            dp_ref[:, vs] = _dot_tn(pp, do4)
            dc_ref[:, vs] = _dot_tn(pc, do4)
        acc_ref[0:1, :] += jnp.sum(dq_ref[...], axis=0, keepdims=True)
        acc_ref[1:2, 0:128] += dsink

    qspec = pl.BlockSpec((WINDOW, D_MODEL), lambda i: (i, 0))
    kspec = pl.BlockSpec((WINDOW, KV_DIM), lambda i: (i, 0))
    return _call(body, "attn_bwd", (nb,),
                 [qspec, kspec, pl.BlockSpec((WINDOW, KV_DIM), lambda i: (jnp.maximum(i - 1, 0), 0)), qspec,
                  pl.BlockSpec((1, 128), lambda i: (0, 0))],
                 (qspec, kspec, kspec, pl.BlockSpec((8, D_MODEL), lambda i: (0, 0))),
                 (_sds((L, D_MODEL), F32), _sds((L, KV_DIM), F32), _sds((L, KV_DIM), F32), _sds((8, D_MODEL), F32)),
                 )(q, kv, kv, do, sinks_pad)


def kv_grad_combine(parts):
    L = parts[0][0].shape[0]
    nb = L // WINDOW
    n = len(parts)

    def body(*refs):
        i = pl.program_id(0)
        o_ref, acc_ref = refs[2 * n], refs[2 * n + 1]

        @pl.when(i == 0)
        def _():
            acc_ref[...] = jnp.zeros_like(acc_ref)

        tot = refs[0][...]
        nxt = refs[1][...]
        for a in range(1, n):
            tot = tot + refs[2 * a][...]
            nxt = nxt + refs[2 * a + 1][...]
        tot = tot + jnp.where(i < nb - 1, nxt, 0.0)
        o_ref[...] = tot
        acc_ref[0:1, :] += jnp.sum(tot, axis=0, keepdims=True)

    cur = pl.BlockSpec((WINDOW, KV_DIM), lambda i: (i, 0))
    nxt = pl.BlockSpec((WINDOW, KV_DIM), lambda i: (jnp.minimum(i + 1, nb - 1), 0))
    args = [t for p in parts for t in p]
    return _call(body, "kv_grad_combine", (nb,), [cur, nxt] * n,
                 (cur, pl.BlockSpec((8, KV_DIM), lambda i: (0, 0))),
                 (_sds((L, KV_DIM), F32), _sds((8, KV_DIM), F32)))(*args)


def mod_fwd(c_all, w, b, name):
    n, _, C = w.shape

    def body(c_ref, w_ref, b_ref, o_ref, ca_ref):
        cv = c_ref[...]
        ca = cv * _sigmoid(cv)
        ca_ref[...] = ca
        o_ref[...] = _dot(ca, w_ref[...], ((1,), (0,))) + b_ref[...]

    return _call(body, name, (n,),
                 [pl.BlockSpec((N_DEV, D_MODEL), lambda i: (0, 0)),
                  pl.BlockSpec((None, D_MODEL, C), lambda i: (i, 0, 0)),
                  pl.BlockSpec((None, 1, C), lambda i: (i, 0, 0))],
                 (pl.BlockSpec((None, N_DEV, C), lambda i: (i, 0, 0)), pl.BlockSpec((N_DEV, D_MODEL), lambda i: (0, 0))),
                 (_sds((n, N_DEV, C), F32), _sds((N_DEV, D_MODEL), F32)))(c_all, w, b)


def mod_wgrad(c_act_t, dmod, name):
    n, _, C = dmod.shape
    tr = 256

    def body(ct_ref, d_ref, o_ref):
        acc = ct_ref[:, 0:1] * d_ref[0:1, :]
        for bidx in range(1, N_DEV):
            acc = acc + ct_ref[:, bidx:bidx + 1] * d_ref[bidx:bidx + 1, :]
        o_ref[...] = acc

    return _call(body, name, (n, D_MODEL // tr),
                 [pl.BlockSpec((tr, N_DEV), lambda i, j: (j, 0)),
                  pl.BlockSpec((None, N_DEV, C), lambda i, j: (i, 0, 0))],
                 pl.BlockSpec((None, tr, C), lambda i, j: (i, j, 0)), _sds((n, D_MODEL, C), F32))(c_act_t, dmod)


def _my_pos():
    return lax.axis_index("x"), lax.axis_index("y"), lax.axis_index("c")


def small_all_gather(v):
    m_per, n = v.shape

    def body(x_ref, out_ref, send_sems, recv_sems, local_sem):
        x, y, c = _my_pos()
        me, sibling = (x, y, c), (x, y, 1 - c)
        chips = [(1 - x, y), (x, 1 - y), (1 - x, 1 - y)]

        def rows(px, py, pc):
            return out_ref.at[pl.ds((4 * px + 2 * py + pc) * m_per, m_per), :]

        def copy(k, block, to, src=None):
            return pltpu.make_async_remote_copy(
                src_ref=rows(*block) if src is None else src, dst_ref=rows(*block),
                send_sem=send_sems.at[k], recv_sem=recv_sems.at[k], device_id=to, device_id_type=MESH)

        mine = pltpu.make_async_copy(x_ref, rows(*me), local_sem)
        mine.start()
        first = [copy(0, me, sibling, src=x_ref)]
        first += [copy(1 + j, me, (*chip, c), src=x_ref) for j, chip in enumerate(chips)]
        for cp in first:
            cp.start()
        passed = [copy(4 + j, (*chip, c), sibling) for j, chip in enumerate(chips)]
        for j, chip in enumerate(chips):
            copy(1 + j, (*chip, c), me).wait_recv()
            passed[j].start()
        copy(0, sibling, me).wait_recv()
        for j, chip in enumerate(chips):
            copy(4 + j, (*chip, 1 - c), me).wait_recv()
        for cp in first + passed:
            cp.wait_send()
        mine.wait()

    return pl.pallas_call(
        body, name="small_all_gather",
        out_shape=_sds((N_DEV * m_per, n), v.dtype),
        in_specs=[pl.BlockSpec(memory_space=pltpu.VMEM)],
        out_specs=pl.BlockSpec(memory_space=pltpu.VMEM),
        scratch_shapes=[pltpu.SemaphoreType.DMA((7,)), pltpu.SemaphoreType.DMA((7,)), pltpu.SemaphoreType.DMA],
        compiler_params=pltpu.CompilerParams(vmem_limit_bytes=VMEM_LIMIT),
    )(v)


def big_all_gather(arrs):
    n = len(arrs)

    def body(*refs):
        ins, outs = refs[:n], refs[n:2 * n]
        send_sems, recv_sems, local_sems = refs[2 * n], refs[2 * n + 1], refs[2 * n + 2]
        x, y, c = _my_pos()
        me, sibling = (x, y, c), (x, y, 1 - c)
        chips = [(1 - x, y), (x, 1 - y), (1 - x, 1 - y)]

        def slot(a, px, py, pc):
            return outs[a].at[4 * px + 2 * py + pc]

        def copy(a, k, block, to, src=None):
            return pltpu.make_async_remote_copy(
                src_ref=slot(a, *block) if src is None else src, dst_ref=slot(a, *block),
                send_sem=send_sems.at[7 * a + k], recv_sem=recv_sems.at[7 * a + k], device_id=to, device_id_type=MESH)

        mine = [pltpu.make_async_copy(ins[a], slot(a, *me), local_sems.at[a]) for a in range(n)]
        for cp in mine:
            cp.start()
        first = []
        for a in range(n):
            first.append(copy(a, 0, me, sibling, src=ins[a]))
            first += [copy(a, 1 + j, me, (*chip, c), src=ins[a]) for j, chip in enumerate(chips)]
        for cp in first:
            cp.start()
        passed = []
        for a in range(n):
            for j, chip in enumerate(chips):
                copy(a, 1 + j, (*chip, c), me).wait_recv()
                fwd = copy(a, 4 + j, (*chip, c), sibling)
                fwd.start()
                passed.append(fwd)
        for a in range(n):
            copy(a, 0, sibling, me).wait_recv()
            for j, chip in enumerate(chips):
                copy(a, 4 + j, (*chip, 1 - c), me).wait_recv()
        for cp in first + passed:
            cp.wait_send()
        for cp in mine:
            cp.wait()

    hbm = pl.BlockSpec(memory_space=pltpu.HBM)
    return pl.pallas_call(
        body, name="big_all_gather",
        out_shape=[_sds((N_DEV,) + a.shape, a.dtype) for a in arrs],
        in_specs=[hbm] * n, out_specs=[hbm] * n,
        scratch_shapes=[pltpu.SemaphoreType.DMA((7 * n,)), pltpu.SemaphoreType.DMA((7 * n,)),
                        pltpu.SemaphoreType.DMA((n,))],
    )(*arrs)


_FLIPS =[(fx, fy, fc) for fx in (0, 1) for fy in (0, 1) for fc in (0, 1)][1:]
_HBM = pl.BlockSpec(memory_space=pltpu.HBM)
_SEM = pl.BlockSpec(memory_space=pltpu.SEMAPHORE)
_EFFECT = pltpu.SideEffectType.DATAFLOW_SIDE_EFFECTING


def _flip(x, y, c, f):
    return (1 - x if f[0] else x), (1 - y if f[1] else y), (1 - c if f[2] else c)


def _xfer_copies(srcs, lands, send_sems, recv_sems, scatter):
    x, y, c = _my_pos()
    me = 4 * x + 2 * y + c
    copies = []
    for a in range(len(srcs)):
        for k, f in enumerate(_FLIPS):
            px, py, pc = _flip(x, y, c, f)
            src = srcs[a].at[4 * px + 2 * py + pc] if scatter else srcs[a]
            copies.append(pltpu.make_async_remote_copy(
                src_ref=src, dst_ref=lands[a].at[me], send_sem=send_sems.at[7 * a + k],
                recv_sem=recv_sems.at[7 * a + k], device_id=(px, py, pc), device_id_type=MESH))
    return copies


def _own_copies(srcs, lands, local_sems, scatter):
    x, y, c = _my_pos()
    me = 4 * x + 2 * y + c
    return [pltpu.make_async_copy(srcs[a].at[me] if scatter else srcs[a], lands[a].at[me], local_sems.at[a])
            for a in range(len(srcs))]


def xfer_start(arrs, scatter, after, name):
    n = len(arrs)
    land_shapes = [a.shape if scatter else (N_DEV,) + a.shape for a in arrs]

    def body(*refs):
        srcs, lands = refs[:n], refs[n:2 * n]
        send_sems, recv_sems, local_sems = refs[2 * n + 1], refs[2 * n + 2], refs[2 * n + 3]
        token = refs[-1]
        for cp in _xfer_copies(srcs, lands, send_sems, recv_sems, scatter):
            cp.start()
        for cp in _own_copies(srcs, lands, local_sems, scatter):
            cp.start()
        token[...] = jnp.zeros_like(token)

    out = pl.pallas_call(
        body, name=name,
        out_shape=(pltpu.SemaphoreType.DMA((7 * n,)), pltpu.SemaphoreType.DMA((7 * n,)),
                   pltpu.SemaphoreType.DMA((n,)),
                   *[pltpu.HBM(a.shape, a.dtype) for a in arrs],
                   *[pltpu.HBM(s, a.dtype) for s, a in zip(land_shapes, arrs)],
                   _sds((8, 128), F32)),
        in_specs=[_HBM] * (2 * n) + [pl.BlockSpec(memory_space=pl.ANY)],
        out_specs=(_SEM, _SEM, _SEM, *([_HBM] * (2 * n)), pl.BlockSpec(memory_space=pltpu.VMEM)),
        input_output_aliases={i: 3 + i for i in range(2 * n)},
        compiler_params=pltpu.CompilerParams(has_side_effects=_EFFECT),
    )(*[pltpu.with_memory_space_constraint(a, pltpu.HBM) for a in arrs],
      *[pltpu.with_memory_space_constraint(lax.empty(s, a.dtype), pltpu.HBM) for s, a in zip(land_shapes, arrs)],
      after)
    return (out[0], out[1], out[2], list(out[3:3 + n]), list(out[3 + n:3 + 2 * n]), scatter), out[-1]


def xfer_wait(handle, after, name):
    send_sems, recv_sems, local_sems, srcs, lands, scatter = handle
    n = len(srcs)

    def body(*refs):
        srcs_r, lands_r = refs[:n], refs[n:2 * n]
        ssem, rsem, lsem = refs[2 * n], refs[2 * n + 1], refs[2 * n + 2]
        for cp in _xfer_copies(srcs_r, lands_r, ssem, rsem, scatter):
            cp.wait_send()
            cp.wait_recv()
        for cp in _own_copies(srcs_r, lands_r, lsem, scatter):
            cp.wait()

    out = pl.pallas_call(
        body, name=name,
        out_shape=(*[pltpu.HBM(a.shape, a.dtype) for a in srcs], *[pltpu.HBM(a.shape, a.dtype) for a in lands]),
        in_specs=[_HBM] * (2 * n) + [_SEM, _SEM, _SEM, pl.BlockSpec(memory_space=pl.ANY)],
        out_specs=tuple([_HBM] * (2 * n)),
        input_output_aliases={i: i for i in range(2 * n)},
        compiler_params=pltpu.CompilerParams(has_side_effects=_EFFECT),
    )(*srcs, *lands, send_sems, recv_sems, local_sems, after)
    return list(out[n:])


def adamw(parts, w, m, v, name, row0=0, prev=None):
    r_tot, C = w.shape
    n_parts, R = parts.shape[0], parts.shape[1]
    row_bytes = 2 * (n_parts * C * parts.dtype.itemsize + 7 * C * 4)
    tr = R
    for cand in (512, 352, 256, 176, 128, 64):
        if R % cand == 0 and row0 % cand == 0 and R > cand and cand * row_bytes <= ADAMW_VMEM_BUDGET:
            tr = cand
            break
    tc = C
    if tr == R and R * row_bytes > ADAMW_VMEM_BUDGET:
        assert row0 == 0 and R == r_tot
        tc = next(t for t in (512, 256, 128) if C % t == 0 and R * row_bytes * t // C <= ADAMW_VMEM_BUDGET)
    assert row0 % tr == 0 and (tr % 8 == 0 or (tr == r_tot and row0 == 0))
    blk0 = row0 // tr
    c1 = 1.0 / (1.0 - ADAM_B1 ** ADAM_STEP)
    c2 = 1.0 / (1.0 - ADAM_B2 ** ADAM_STEP)

    def body(p_ref, w_ref, m_ref, v_ref, *rest):
        g_ref, d_ref, nm_ref, nv_ref = rest[-4:]
        g = p_ref[0].astype(F32)
        for k in range(1, n_parts):
            g = g + p_ref[k].astype(F32)
        nm = ADAM_B1 * m_ref[...] + (1.0 - ADAM_B1) * g
        nv = ADAM_B2 * v_ref[...] + (1.0 - ADAM_B2) * (g * g)
        g_ref[...] = g
        nm_ref[...] = nm
        nv_ref[...] = nv
        d_ref[...] = -ADAM_LR * ((nm * c1) / (jnp.sqrt(nv * c2) + ADAM_EPS) + ADAM_WD * w_ref[...])

    if tc == C:
        grid = (R // tr,)
        blk = pl.BlockSpec((tr, C), lambda i: (i + blk0, 0))
        p_spec = pl.BlockSpec((n_parts, tr, C), lambda i: (0, i, 0))
    else:
        grid = (C // tc,)
        blk = pl.BlockSpec((R, tc), lambda i: (0, i))
        p_spec = pl.BlockSpec((n_parts, R, tc), lambda i: (0, 0, i))
    in_specs = [p_spec, blk, blk, blk]
    args = [parts, w, m, v]
    aliases = {}
    if prev is not None:
        in_specs += [pl.BlockSpec(memory_space=pl.ANY)] * 4
        args += list(prev)
        aliases = {4 + k: k for k in range(4)}
    return pl.pallas_call(
        body, name=name, grid=grid, in_specs=in_specs, out_specs=(blk, blk, blk, blk),
        out_shape=tuple(_sds((r_tot, C), F32) for _ in range(4)), input_output_aliases=aliases,
        compiler_params=pltpu.CompilerParams(vmem_limit_bytes=VMEM_LIMIT))(*args)


def _ffn_fwd(x, nw, sh, sc, g, wt_gu, w_dn):
    h = norm_mod(x, nw, sh, sc, "ffn_norm")
    gp, up, a = ffn_up(h, wt_gu)
    y, xn = matmul(a, w_dn, "nn", BF, "ffn_down", res=x, gate=g, coef=0.5)
    return xn, (x, h, gp, up, a, y)


def _ffn_bwd(dxo, saved, nw, sc, g, wt_gu, w_dn):
    x, h, gp, up, a, y = saved
    dy, acc1 = resid_gate_bwd(dxo, y, g, 0.5, "ffn_gate_bwd")
    d_wdn = matmul(a, dy, "tn", BF, "ffn_down_wgrad")
    dg, du = ffn_down_dgrad(dy, w_dn, gp, up)
    d_wt = ffn_up_wgrad(dg, du, h)
    dh = ffn_up_dgrad(dg, du, wt_gu)
    dx, acc2 = norm_mod_bwd(x, dh, dxo, nw, sc, "ffn_norm_bwd")
    return dx, d_wt, d_wdn, (acc2[0], acc2[1], acc1[0]), acc2[2]


def _group_layout(a):
    L = a.shape[0]
    return a[:, :SSM_HEADS].reshape(L, SSM_GROUPS, _HPG).transpose(1, 0, 2)


def _ungroup_layout(a):
    L = a.shape[1]
    return jnp.pad(a.transpose(1, 0, 2).reshape(L, SSM_HEADS), ((0, 0), (0, 128 - SSM_HEADS)))


def _pad_row(vec, n=128):
    return jnp.pad(vec.reshape(1, -1), ((0, 0), (0, n - vec.shape[-1])))


def _mamba_fwd(x, nw, sh, sc, g, p):
    h = norm_mod(x, nw, sh, sc, "mix_norm")
    zx = matmul(h, p["w_in_t"], "nt", F32, "ssm_in")
    xc = conv_fwd(zx, p["conv_w"], p["conv_b"])
    dt, acs = dt_prep(zx, p["dt_bias"], p["a_log"])
    dt_g, acs_g = _group_layout(dt), _group_layout(acs)
    acs_t = acs_g.transpose(0, 2, 1)
    y, pst = ssd_fwd(xc, dt_g, acs_g, acs_t, p["d_exp"])
    yn = gate_norm(y, zx, p["norm_w"])
    yo, xn = matmul(yn, p["w_out"], "nn", BF, "ssm_out", res=x, gate=g, coef=1.0)
    return xn, (x, h, zx, xc, dt, dt_g, acs_g, acs_t, y, pst, yn, yo)


def _mamba_bwd(dxo, saved, nw, sc, g, p):
    x, h, zx, xc, dt, dt_g, acs_g, acs_t, y, pst, yn, yo = saved
    dyo, acc1 = resid_gate_bwd(dxo, yo, g, 1.0, "mix_gate_bwd")
    d_wout = matmul(yn, dyo, "tn", BF, "ssm_out_wgrad")
    dyn = matmul(dyo, p["w_out"], "nt", F32, "ssm_out_dgrad")
    dy, dz, accn = gate_norm_bwd(dyn, y, zx, p["norm_w"])
    dxs, dB, dC, ddt_g, da_g, dd = ssd_bwd(dy, xc, dt_g, acs_g, acs_t, pst, p["d_exp"])
    dxc = jnp.concatenate([dxs, dB, dC], axis=1)
    ds = conv_bwd_act(dxc, zx, p["conv_w"], p["conv_b"])
    du, accc = conv_bwd(ds, zx, p["conv_w"])
    draw, accdt = dt_bwd(_ungroup_layout(ddt_g), _ungroup_layout(da_g), dt, zx, p["dt_bias"], p["a_log"])
    dzx = jnp.concatenate([dz, du, draw], axis=1)
    d_win = matmul(dzx, h, "tn", BF, "ssm_in_wgrad")[:IN_PROJ]
    dh = matmul(dzx, p["w_in_t"], "nn", F32, "ssm_in_dgrad")
    dx, acc2 = norm_mod_bwd(x, dh, dxo, nw, sc, "mix_norm_bwd")
    small = dict(conv_w=accc[:CONV_WIDTH], conv_b=accc[CONV_WIDTH], dt_bias=accdt[0, :SSM_HEADS],
                 a_log=accdt[1, :SSM_HEADS], d=dd.reshape(SSM_HEADS, SSM_HEADDIM).sum(-1), norm_w=accn[0])
    return dx, d_win, d_wout, (acc2[0], acc2[1], acc1[0]), acc2[2], small


def _attn_layer_fwd(x, nw, sh, sc, g, p, kv):
    h = norm_mod(x, nw, sh, sc, "mix_norm")
    q = matmul(h, p["w_q"], "nn", F32, "attn_q", bias=p["b_q"])
    o = attn_fwd(q, kv, p["sinks"])
    yo, xn = matmul(o, p["w_o"], "nn", BF, "attn_o", bias=p["b_o"], res=x, gate=g, coef=1.0)
    return xn, (x, h, q, o, yo)


def _attn_layer_bwd(dxo, saved, nw, sc, g, p, kv):
    x, h, q, o, yo = saved
    dyo, acc1 = resid_gate_bwd(dxo, yo, g, 1.0, "mix_gate_bwd")
    d_wo = matmul(o, dyo, "tn", BF, "attn_o_wgrad")
    do = matmul(dyo, p["w_o"], "nt", F32, "attn_o_dgrad")
    dq, dkv_c, dkv_p, acca = attn_bwd(q, kv, do, p["sinks"])
    d_wq = matmul(h, dq, "tn", BF, "attn_q_wgrad")
    dh = matmul(dq, p["w_q"], "nt", F32, "attn_q_dgrad")
    dx, acc2 = norm_mod_bwd(x, dh, dxo, nw, sc, "mix_norm_bwd")
    small = dict(b_q=acca[0], sinks=acca[1, :ATT_HEADS], b_o=acc1[1])
    return dx, d_wq, d_wo, (acc2[0], acc2[1], acc1[0]), acc2[2], small, (dkv_c, dkv_p)


def _pack_rows(pieces):
    rows, spans, off = [], [], 0
    for a in pieces:
        flat = a.reshape(-1).astype(F32)
        n = -(-flat.shape[0] // D_MODEL)
        rows.append(jnp.pad(flat, (0, n * D_MODEL - flat.shape[0])).reshape(n, D_MODEL))
        spans.append((off, a.shape))
        off += n
    pad = -off % 8
    if pad:
        rows.append(jnp.zeros((pad, D_MODEL), F32))
    return jnp.concatenate(rows, axis=0), spans, off + pad


def _unpack_rows(g, spans):
    out = []
    for off, shape in spans:
        size = 1
        for s in shape:
            size *= s
        n = -(-size // D_MODEL)
        out.append(g[:, off:off + n].reshape(N_DEV, n * D_MODEL)[:, :size].reshape((N_DEV,) + tuple(shape)))
    return out


def _unshard_last(g):
    nd = g.ndim
    perm = tuple(range(1, nd - 1)) + (0, nd - 1)
    t = g.transpose(perm)
    return t.reshape(t.shape[:-2] + (N_DEV * g.shape[-1],))


def _shard_last(a, me):
    s = a.shape[-1] // N_DEV
    return lax.dynamic_slice_in_dim(a, me * s, s, axis=a.ndim - 1)


def kernel(x, c, ffn_norm_w, ffn_w_gu, ffn_w_down, mod_w, mod_b, mix_norm_w, ssm_w_in, ssm_conv_w, ssm_conv_b, ssm_dt_bias, ssm_a_log, ssm_d, ssm_norm_w, ssm_w_out, kv_norm_w, kv_mod_w, kv_mod_b, w_kv, b_kv, attn_w_q, attn_b_q, attn_sinks, attn_w_o, attn_b_o, final_norm_w, loss_target, m_ffn_norm_w, m_ffn_w_gu, m_ffn_w_down, m_mod_w, m_mod_b, m_mix_norm_w, m_ssm_w_in, m_ssm_conv_w, m_ssm_conv_b, m_ssm_dt_bias, m_ssm_a_log, m_ssm_d, m_ssm_norm_w, m_ssm_w_out, m_kv_norm_w, m_kv_mod_w, m_kv_mod_b, m_w_kv, m_b_kv, m_attn_w_q, m_attn_b_q, m_attn_sinks, m_attn_w_o, m_attn_b_o, m_final_norm_w, v_ffn_norm_w, v_ffn_w_gu, v_ffn_w_down, v_mod_w, v_mod_b, v_mix_norm_w, v_ssm_w_in, v_ssm_conv_w, v_ssm_conv_b, v_ssm_dt_bias, v_ssm_a_log, v_ssm_d, v_ssm_norm_w, v_ssm_w_out, v_kv_norm_w, v_kv_mod_w, v_kv_mod_b, v_w_kv, v_b_kv, v_attn_w_q, v_attn_b_q, v_attn_sinks, v_attn_w_o, v_attn_b_o, v_final_norm_w):
    D = D_MODEL
    me = 4 * lax.axis_index("x") + 2 * lax.axis_index("y") + lax.axis_index("c")
    xs = x[0]
    target = loss_target[0]
    mod_cols = mod_w.shape[-1]
    kvm_cols = kv_mod_w.shape[-1]

    packed, spans, _ = _pack_rows([c, ffn_norm_w, ssm_conv_w, ssm_conv_b, ssm_norm_w])
    nrow = packed.shape[0]
    g1 = small_all_gather(packed).reshape(N_DEV, nrow, D)
    c_all, fnw_g, cw_g, cb_g, snw_g = _unpack_rows(g1, spans)
    c_all = c_all.reshape(N_DEV, D)
    ffn_nw = _unshard_last(fnw_g)
    conv_w = _unshard_last(cw_g)
    conv_b = _unshard_last(cb_g)
    ssm_nw = _unshard_last(snw_g)

    mod_b_loc = lax.dynamic_slice_in_dim(mod_b, me * mod_cols, mod_cols, axis=1).reshape(DEPTH, 1, mod_cols)
    kvb_loc = lax.dynamic_slice_in_dim(kv_mod_b, me * kvm_cols, kvm_cols, axis=0).reshape(1, 1, kvm_cols)
    modp, c_act = mod_fwd(c_all, mod_w, mod_b_loc, "mod_fwd")
    kvmp, _ = mod_fwd(c_all, kv_mod_w.reshape(1, D, kvm_cols), kvb_loc, "kv_mod_fwd")
    packed2, spans2, _ = _pack_rows([modp, kvmp])
    nrow2 = packed2.shape[0]
    g2 = small_all_gather(packed2).reshape(N_DEV, nrow2, D)
    modp_g, kvmp_g = _unpack_rows(g2, spans2)
    mod_all = modp_g.transpose(1, 2, 0, 3).reshape(DEPTH, N_DEV, N_MOD * D)
    kvm_all = kvmp_g.transpose(1, 2, 0, 3).reshape(N_DEV, 2 * D)
    mod_me = lax.dynamic_index_in_dim(mod_all, me, axis=1, keepdims=False).reshape(DEPTH, N_MOD, 1, D)
    kvm_me = lax.dynamic_index_in_dim(kvm_all, me, axis=0, keepdims=False).reshape(2, 1, D)

    gu_t = jnp.swapaxes(ffn_w_gu, 2, 3)
    win_t = jnp.transpose(ssm_w_in, (2, 0, 1))
    S = gu_t.shape[2]
    s_in = win_t.shape[0]
    r_dn = ffn_w_down.shape[2]
    r_mix = ssm_w_out.shape[1]
    r_at = attn_w_q.shape[1]

    def layer_pack(k):
        arrs = [gu_t[k, 0].astype(BF), gu_t[k, 1].astype(BF), ffn_w_down[k, 0].astype(BF), ffn_w_down[k, 1].astype(BF)]
        if k < N_A:
            arrs += [ssm_w_out[k].astype(BF), win_t[:, k].astype(BF)]
        else:
            arrs += [attn_w_q[k - N_A].astype(BF), attn_w_o[k - N_A].astype(BF)]
        if k == N_A:
            arrs.append(w_kv.astype(BF))
        return arrs

    packs = [layer_pack(k) for k in range(DEPTH)]
    gathered = [None] * DEPTH
    first = big_all_gather([packs[0][0], packs[0][2]])
    pend_rest, tok_next = xfer_start([packs[0][1]] + packs[0][3:], False, g2, "gather_start_0")
    pending = [None] * DEPTH
    for k in range(1, DEPTH):
        pending[k], tok_next = xfer_start(packs[k], False, tok_next, "gather_start_%d" % k)
    gathered[0] = [first[0], None, first[1]]

    def wt_gu_full(i, j):
        return gathered[i][j].reshape(N_DEV * S, D)

    def w_dn_full(i, j):
        return gathered[i][2 + j].reshape(D_FF, D)

    def mix_rows(i, a):
        return gathered[i][4 + a].reshape(-1, D)

    def mamba_params(j):
        w_in_t = jnp.pad(mix_rows(j, 1), ((0, IN_PROJ_PAD - IN_PROJ), (0, 0)))
        return dict(w_in_t=w_in_t, w_out=mix_rows(j, 0), conv_w=conv_w[j], conv_b=conv_b[j].reshape(1, -1),
                    dt_bias=_pad_row(ssm_dt_bias[j]), a_log=_pad_row(ssm_a_log[j]),
                    d_exp=jnp.repeat(ssm_d[j], SSM_HEADDIM).reshape(SSM_GROUPS, 1, _GW),
                    norm_w=ssm_nw[j].reshape(1, -1))

    def attn_params(j):
        return dict(w_q=mix_rows(N_A + j, 0), w_o=mix_rows(N_A + j, 1),
                    b_q=attn_b_q[j].reshape(1, -1), b_o=attn_b_o[j].reshape(1, -1), sinks=_pad_row(attn_sinks[j]))

    saved = []
    kv = None
    kv_saved = None
    w_kv_full = None
    xcur = xs
    for i in range(DEPTH):
        if i >= 1:
            gathered[i] = xfer_wait(pending[i], xcur, "gather_wait_%d" % i)
        md = mod_me[i]
        if i == 0:
            md = md + tok_next[0, 0]
        if i == N_A:
            w_kv_full = gathered[N_A][6].reshape(D, KV_DIM)
            h_kv = norm_mod(xcur, kv_norm_w.reshape(1, D), kvm_me[0], kvm_me[1], "kv_norm")
            kv = matmul(h_kv, w_kv_full, "nn", F32, "kv_proj", bias=b_kv.reshape(1, -1))
            kv_saved = (xcur, h_kv)
        x1, s1 = _ffn_fwd(xcur, ffn_nw[i, 0].reshape(1, D), md[0], md[1], md[2], wt_gu_full(i, 0), w_dn_full(i, 0))
        if i == 0:
            rest = xfer_wait(pend_rest, x1, "gather_wait_0")
            gathered[0] = [first[0], rest[0], first[1]] + rest[1:]
        if i < N_A:
            pm = mamba_params(i)
            x2, s2 = _mamba_fwd(x1, mix_norm_w[i].reshape(1, D), md[3], md[4], md[5], pm)
        else:
            pm = attn_params(i - N_A)
            x2, s2 = _attn_layer_fwd(x1, mix_norm_w[i].reshape(1, D), md[3], md[4], md[5], pm, kv)
        x3, s3 = _ffn_fwd(x2, ffn_nw[i, 1].reshape(1, D), md[6], md[7], md[8], wt_gu_full(i, 1), w_dn_full(i, 1))
        saved.append((s1, s2, s3, pm))
        xcur = x3

    dx, accf = final_loss(xcur, final_norm_w.reshape(1, D), target)
    d_mod = [None] * DEPTH
    d_ffn_nw = [[None, None] for _ in range(DEPTH)]
    d_mix_nw = [None] * DEPTH
    sm_m, sm_a = [None] * N_A, [None] * N_A
    kv_parts = [None] * N_A
    d_kvm = d_kv_nw = d_bkv = None
    exchanges = []
    tok = None

    def send(arrs, tag, after=None):
        handle, t = xfer_start(arrs, True, dx if after is None else after, "exch_start_%s" % tag)
        exchanges.append((handle, tag))
        return t

    def ffn_slabs(d_wt, d_wdn):
        return [d_wt.reshape(N_DEV, S, D), d_wdn.reshape(N_DEV, r_dn, D)]

    for i in reversed(range(DEPTH)):
        md = mod_me[i]
        s1, s2, s3, pm = saved[i]
        g2 = md[8] if tok is None else md[8] + tok[0, 0]
        dx, d_wt, d_wdn, m2, d_ffn_nw[i][1] = _ffn_bwd(
            dx, s3, ffn_nw[i, 1].reshape(1, D), md[7], g2, wt_gu_full(i, 1), w_dn_full(i, 1))
        tok = send(ffn_slabs(d_wt, d_wdn), "f%d1" % i)
        gm = md[5] + tok[0, 0]
        if i < N_A:
            dx, d_in, d_out, mm_, d_mix_nw[i], sm_m[i] = _mamba_bwd(dx, s2, mix_norm_w[i].reshape(1, D), md[4], gm, pm)
            tok = send([d_in.reshape(N_DEV, s_in, D), d_out.reshape(N_DEV, r_mix, D)], "m%d" % i)
        else:
            j = i - N_A
            dx, d_q, d_o, mm_, d_mix_nw[i], sm_a[j], kv_parts[j] = _attn_layer_bwd(
                dx, s2, mix_norm_w[i].reshape(1, D), md[4], gm, pm, kv)
            tok = send([d_q.reshape(N_DEV, r_at, D), d_o.reshape(N_DEV, r_at, D)], "m%d" % i)
        g1 = md[2] + tok[0, 0]
        dx, d_wt, d_wdn, m1, d_ffn_nw[i][0] = _ffn_bwd(
            dx, s1, ffn_nw[i, 0].reshape(1, D), md[1], g1, wt_gu_full(i, 0), w_dn_full(i, 0))
        d_mod[i] = jnp.concatenate(list(m1) + list(mm_) + list(m2), axis=0)
        last = ffn_slabs(d_wt, d_wdn)
        if i == N_A:
            x_kv, h_kv = kv_saved
            dkv, acck = kv_grad_combine(kv_parts)
            d_bkv = acck[0]
            d_kv_w = matmul(h_kv, dkv, "tn", BF, "kv_wgrad")
            dh_kv = matmul(dkv, w_kv_full, "nt", F32, "kv_dgrad")
            dx, acc_kv = norm_mod_bwd(x_kv, dh_kv, dx, kv_norm_w.reshape(1, D), kvm_me[1], "kv_norm_bwd")
            d_kvm = jnp.concatenate([acc_kv[0], acc_kv[1]], axis=0)
            d_kv_nw = acc_kv[2]
            last.append(d_kv_w.reshape(N_DEV, -1, KV_DIM))
        if i > 0:
            tok = send(last, "f%d0" % i)
    grad_x = dx.reshape(x.shape)

    small_list = [
        jnp.stack(d_mod, 0), d_kvm,
        jnp.stack([jnp.stack(r, 0) for r in d_ffn_nw], 0),
        jnp.stack(d_mix_nw, 0),
        jnp.stack([s["conv_w"] for s in sm_m], 0), jnp.stack([s["conv_b"] for s in sm_m], 0),
        jnp.stack([s["dt_bias"] for s in sm_m], 0), jnp.stack([s["a_log"] for s in sm_m], 0),
        jnp.stack([s["d"] for s in sm_m], 0), jnp.stack([s["norm_w"] for s in sm_m], 0),
        d_kv_nw, d_bkv,
        jnp.stack([s["b_q"] for s in sm_a], 0), jnp.stack([s["sinks"] for s in sm_a], 0),
        jnp.stack([s["b_o"] for s in sm_a], 0), accf[0], accf[1],
    ]
    packed3, spans3, _ = _pack_rows(small_list)
    nrow3 = packed3.shape[0]
    g3 = small_all_gather(packed3).reshape(N_DEV, nrow3, D)
    send(last, "f00", after=g3)
    (p_mod, p_kvm, p_fnw, p_mnw, p_cw, p_cb, p_dtb, p_al, p_d, p_snw, p_kvnw, p_bkv, p_bq, p_sk, p_bo, p_fin,
     p_loss) = _unpack_rows(g3, spans3)

    loss = 0.5 / D * jnp.sum(p_loss)

    c_act_t = c_act.T
    dmod_loc = _shard_last(p_mod, me).transpose(1, 0, 2)
    dkvm_loc = _shard_last(p_kvm, me).reshape(1, N_DEV, kvm_cols)
    gp_mod_w = mod_wgrad(c_act_t, dmod_loc, "mod_wgrad")
    gp_kvm_w = mod_wgrad(c_act_t, dkvm_loc, "kv_mod_wgrad")[0]

    def as_parts_single(a):
        return a[None]

    def upd(name, parts, w, m, v):
        shp = w.shape
        c_last = shp[-1]
        out = adamw(parts.reshape(parts.shape[0], -1, c_last), w.reshape(-1, c_last), m.reshape(-1, c_last),
                    v.reshape(-1, c_last), "adamw_" + name)
        return tuple(o.reshape(shp) for o in out)

    views = {
        "ffn_w_gu": [jnp.swapaxes(t, 2, 3).reshape(-1, D) for t in (ffn_w_gu, m_ffn_w_gu, v_ffn_w_gu)],
        "ffn_w_down": [t.reshape(-1, D) for t in (ffn_w_down, m_ffn_w_down, v_ffn_w_down)],
        "ssm_w_out": [t.reshape(-1, D) for t in (ssm_w_out, m_ssm_w_out, v_ssm_w_out)],
        "attn_w_q": [t.reshape(-1, D) for t in (attn_w_q, m_attn_w_q, v_attn_w_q)],
        "attn_w_o": [t.reshape(-1, D) for t in (attn_w_o, m_attn_w_o, v_attn_w_o)],
    }
    filled = {k: None for k in views}

    def upd_rows(name, parts, row0):
        w, m, v = views[name]
        filled[name] = adamw(parts, w, m, v, "adamw_" + name, row0=row0, prev=filled[name])
        return filled[name][3]

    def fence(arrs):
        tot = jnp.zeros((1, 1), F32)
        for a in arrs:
            tot = tot + lax.slice(a, (0,) * a.ndim, (1,) * a.ndim).reshape(1, 1)
        return jnp.broadcast_to(tot, (8, 128))

    res = {}
    res["ffn_norm_w"] = upd("ffn_norm_w", _shard_last(p_fnw, me), ffn_norm_w, m_ffn_norm_w, v_ffn_norm_w)
    res["mod_w"] = upd("mod_w", as_parts_single(gp_mod_w), mod_w, m_mod_w, v_mod_w)
    res["mod_b"] = upd("mod_b", p_mod, mod_b, m_mod_b, v_mod_b)
    res["mix_norm_w"] = upd("mix_norm_w", p_mnw, mix_norm_w, m_mix_norm_w, v_mix_norm_w)
    res["ssm_conv_w"] = upd("ssm_conv_w", _shard_last(p_cw, me), ssm_conv_w, m_ssm_conv_w, v_ssm_conv_w)
    res["ssm_conv_b"] = upd("ssm_conv_b", _shard_last(p_cb, me), ssm_conv_b, m_ssm_conv_b, v_ssm_conv_b)
    res["ssm_dt_bias"] = upd("ssm_dt_bias", p_dtb, ssm_dt_bias, m_ssm_dt_bias, v_ssm_dt_bias)
    res["ssm_a_log"] = upd("ssm_a_log", p_al, ssm_a_log, m_ssm_a_log, v_ssm_a_log)
    res["ssm_d"] = upd("ssm_d", p_d, ssm_d, m_ssm_d, v_ssm_d)
    res["ssm_norm_w"] = upd("ssm_norm_w", _shard_last(p_snw, me), ssm_norm_w, m_ssm_norm_w, v_ssm_norm_w)
    res["kv_norm_w"] = upd("kv_norm_w", p_kvnw.reshape(N_DEV, 1, D), kv_norm_w.reshape(1, D),
                           m_kv_norm_w.reshape(1, D), v_kv_norm_w.reshape(1, D))
    res["kv_mod_w"] = upd("kv_mod_w", as_parts_single(gp_kvm_w), kv_mod_w, m_kv_mod_w, v_kv_mod_w)
    res["kv_mod_b"] = upd("kv_mod_b", p_kvm.reshape(N_DEV, 1, 2 * D), kv_mod_b.reshape(1, -1),
                          m_kv_mod_b.reshape(1, -1), v_kv_mod_b.reshape(1, -1))
    res["b_kv"] = upd("b_kv", p_bkv.reshape(N_DEV, 1, KV_DIM), b_kv.reshape(1, -1), m_b_kv.reshape(1, -1),
                      v_b_kv.reshape(1, -1))
    res["attn_b_q"] = upd("attn_b_q", p_bq, attn_b_q, m_attn_b_q, v_attn_b_q)
    res["attn_sinks"] = upd("attn_sinks", p_sk, attn_sinks, m_attn_sinks, v_attn_sinks)
    res["attn_b_o"] = upd("attn_b_o", p_bo, attn_b_o, m_attn_b_o, v_attn_b_o)
    res["final_norm_w"] = upd("final_norm_w", p_fin.reshape(N_DEV, 1, D), final_norm_w.reshape(1, D),
                              m_final_norm_w.reshape(1, D), v_final_norm_w.reshape(1, D))

    chain = fence([dx] + [t[3] for t in res.values()])
    r_in_parts = [None] * N_A
    r_kv = None
    for handle, tag in exchanges:
        got = xfer_wait(handle, chain, "exch_wait_%s" % tag)
        i = int(tag[1])
        if tag[0] == "f":
            jf = int(tag[2])
            done = [upd_rows("ffn_w_gu", got[0], (2 * i + jf) * S), upd_rows("ffn_w_down", got[1], (2 * i + jf) * r_dn)]
            if len(got) > 2:
                res["w_kv"] = upd("w_kv", got[2], w_kv, m_w_kv, v_w_kv)
                done.append(res["w_kv"][3])
        elif i < N_A:
            r_in_parts[i] = got[0]
            done = [upd_rows("ssm_w_out", got[1], i * r_mix)]
            if i == 0:
                win_out = adamw(jnp.stack(r_in_parts, axis=2).reshape(N_DEV, s_in * N_A, D),
                                *[jnp.transpose(t, (2, 0, 1)).reshape(-1, D) for t in (ssm_w_in, m_ssm_w_in, v_ssm_w_in)],
                                "adamw_ssm_w_in")
                res["ssm_w_in"] = tuple(jnp.transpose(t.reshape(win_t.shape), (1, 2, 0)) for t in win_out)
                done.append(win_out[3])
        else:
            done = [upd_rows("attn_w_q", got[0], (i - N_A) * r_at), upd_rows("attn_w_o", got[1], (i - N_A) * r_at)]
        chain = fence(done)

    res["ffn_w_gu"] = tuple(jnp.swapaxes(t.reshape(gu_t.shape), 2, 3) for t in filled["ffn_w_gu"])
    res["ffn_w_down"] = tuple(t.reshape(ffn_w_down.shape) for t in filled["ffn_w_down"])
    res["ssm_w_out"] = tuple(t.reshape(ssm_w_out.shape) for t in filled["ssm_w_out"])
    res["attn_w_q"] = tuple(t.reshape(attn_w_q.shape) for t in filled["attn_w_q"])
    res["attn_w_o"] = tuple(t.reshape(attn_w_o.shape) for t in filled["attn_w_o"])

    names = ["ffn_norm_w", "ffn_w_gu", "ffn_w_down", "mod_w", "mod_b", "mix_norm_w", "ssm_w_in", "ssm_conv_w",
             "ssm_conv_b", "ssm_dt_bias", "ssm_a_log", "ssm_d", "ssm_norm_w", "ssm_w_out", "kv_norm_w", "kv_mod_w",
             "kv_mod_b", "w_kv", "b_kv", "attn_w_q", "attn_b_q", "attn_sinks", "attn_w_o", "attn_b_o", "final_norm_w"]
    vec_shapes = {"kv_norm_w": (D,), "kv_mod_b": (2 * D,), "b_kv": (KV_DIM,), "final_norm_w": (D,)}
    outs = [loss, grad_x]
    for k in range(4):
        for nme in names:
            t = res[nme][k]
            if nme in vec_shapes:
                t = t.reshape(vec_shapes[nme])
            outs.append(t)
    return tuple(outs)
```

```python
import functools

import jax
import jax.numpy as jnp
from jax import lax
from jax.experimental import pallas as pl
from jax.experimental.pallas import tpu as pltpu

F32 = jnp.float32
BF = jnp.bfloat16
MESH = pl.DeviceIdType.MESH

N_DEV = 8
D_MODEL = 1024
DEPTH = 4
N_A = 2
EPS = 1e-5
N_MOD = 9
D_FF = 2816
D_INNER = 2048
SSM_HEADDIM = 64
SSM_HEADS = 32
SSM_GROUPS = 8
SSM_STATE = 128
CONV_WIDTH = 4
CHUNK = 512
CONV_DIM = D_INNER + 2 * SSM_GROUPS * SSM_STATE
IN_PROJ = D_INNER + CONV_DIM + SSM_HEADS
IN_PROJ_PAD = D_INNER + CONV_DIM + 128
ATT_HEADS = 16
KV_HEADS = 4
HEAD_DIM = 64
WINDOW = 128
KV_DIM = 2 * KV_HEADS * HEAD_DIM

ADAM_LR = 0.001
ADAM_B1 = 0.9
ADAM_B2 = 0.999
ADAM_EPS = 1e-08
ADAM_WD = 0.01
ADAM_STEP = 10

VMEM_LIMIT = 48 * 2 ** 20
ADAMW_VMEM_BUDGET = 24 * 2 ** 20
NEG = -1e30


def _call(body, name, grid, in_specs, out_specs, out_shape, scratch=()):
    return pl.pallas_call(
        body, name=name, grid=grid, in_specs=in_specs, out_specs=out_specs, out_shape=out_shape,
        scratch_shapes=list(scratch),
        compiler_params=pltpu.CompilerParams(vmem_limit_bytes=VMEM_LIMIT))


def _tile(n, cap):
    t = (cap // 128) * 128
    while t >= 128:
        if n % t == 0:
            return t
        t -= 128
    return n


def _sds(shape, dtype):
    return jax.ShapeDtypeStruct(shape, dtype)


def _sigmoid(v):
    return 1.0 / (1.0 + jnp.exp(-v))


def _dot(a, b, dims):
    return lax.dot_general(a, b, (dims, ((), ())), preferred_element_type=F32)


def _dot_nn(a, b):
    return _dot(a.astype(BF), b.astype(BF), ((1,), (0,)))


def _dot_nt(a, b):
    return _dot(a.astype(BF), b.astype(BF), ((1,), (1,)))


def _dot_tn(a, b):
    return _dot(a.astype(BF), b.astype(BF), ((0,), (0,)))


def matmul(a, b, mode, out_dtype, name, bias=None, res=None, gate=None, coef=1.0):
    if mode == "nn":
        (M, K), (_, N) = a.shape, b.shape
    elif mode == "nt":
        (M, K), (N, _) = a.shape, b.shape
    else:
        (K, M), (_, N) = a.shape, b.shape
    cap_n = 512 if K > 4096 else 1024
    tm = _tile(M, 1024 if (mode == "tn" or K <= D_FF) else 512)
    tn = _tile(N, cap_n)
    if mode != "tn" and tm * tn > 1024 * 896:
        tn = _tile(N, 512)
    if mode == "nn":
        a_spec = pl.BlockSpec((tm, K), lambda i, j: (i, 0))
        b_spec = pl.BlockSpec((K, tn), lambda i, j: (0, j))
        fn = _dot_nn
    elif mode == "nt":
        a_spec = pl.BlockSpec((tm, K), lambda i, j: (i, 0))
        b_spec = pl.BlockSpec((tn, K), lambda i, j: (j, 0))
        fn = _dot_nt
    else:
        a_spec = pl.BlockSpec((K, tm), lambda i, j: (0, i))
        b_spec = pl.BlockSpec((K, tn), lambda i, j: (0, j))
        fn = _dot_tn
    has_bias, has_res = bias is not None, res is not None
    o_spec = pl.BlockSpec((tm, tn), lambda i, j: (i, j))
    v_spec = pl.BlockSpec((1, tn), lambda i, j: (0, j))
    in_specs, args = [a_spec, b_spec], [a, b]
    if has_bias:
        in_specs.append(v_spec)
        args.append(bias)
    if has_res:
        in_specs += [o_spec, v_spec]
        args += [res, gate]

    def body(*refs):
        a_ref, b_ref = refs[0], refs[1]
        k = 2
        y = fn(a_ref[...], b_ref[...])
        if has_bias:
            y = y + refs[k][...]
            k += 1
        if has_res:
            res_ref, gate_ref = refs[k], refs[k + 1]
            refs[k + 2][...] = y.astype(out_dtype)
            refs[k + 3][...] = res_ref[...] + coef * gate_ref[...] * y
        else:
            refs[k][...] = y.astype(out_dtype)

    if has_res:
        out_shape = (_sds((M, N), out_dtype), _sds((M, N), F32))
        out_specs = (o_spec, o_spec)
    else:
        out_shape = _sds((M, N), out_dtype)
        out_specs = o_spec
    return _call(body, name, (M // tm, N // tn), in_specs, out_specs, out_shape)(*args)


def norm_mod(x, nw, sh, sc, name):
    L, D = x.shape
    tm = _tile(L, 512)

    def body(x_ref, nw_ref, sh_ref, sc_ref, h_ref):
        xf = x_ref[...]
        r = lax.rsqrt(jnp.mean(xf * xf, axis=-1, keepdims=True) + EPS)
        n = xf * r * nw_ref[...]
        h_ref[...] = (n * (1.0 + sc_ref[...]) + sh_ref[...]).astype(BF)

    row = pl.BlockSpec((tm, D), lambda i: (i, 0))
    vec = pl.BlockSpec((1, D), lambda i: (0, 0))
    return _call(body, name, (L // tm,), [row, vec, vec, vec], row, _sds((L, D), BF))(x, nw, sh, sc)


def norm_mod_bwd(x, dh, dres, nw, sc, name):
    L, D = x.shape
    tm = _tile(L, 512)

    def body(x_ref, dh_ref, dres_ref, nw_ref, sc_ref, dx_ref, acc_ref):
        @pl.when(pl.program_id(0) == 0)
        def _():
            acc_ref[...] = jnp.zeros_like(acc_ref)

        xf = x_ref[...]
        dhf = dh_ref[...].astype(F32)
        r = lax.rsqrt(jnp.mean(xf * xf, axis=-1, keepdims=True) + EPS)
        xhat = xf * r
        nwv = nw_ref[...]
        dn = dhf * (1.0 + sc_ref[...])
        dxhat = dn * nwv
        proj = jnp.mean(dxhat * xhat, axis=-1, keepdims=True)
        dx_ref[...] = dres_ref[...] + r * (dxhat - xhat * proj)
        acc_ref[0:1, :] += jnp.sum(dhf, axis=0, keepdims=True)
        acc_ref[1:2, :] += jnp.sum(dhf * xhat * nwv, axis=0, keepdims=True)
        acc_ref[2:3, :] += jnp.sum(dn * xhat, axis=0, keepdims=True)

    row = pl.BlockSpec((tm, D), lambda i: (i, 0))
    vec = pl.BlockSpec((1, D), lambda i: (0, 0))
    acc = pl.BlockSpec((8, D), lambda i: (0, 0))
    return _call(body, name, (L // tm,), [row, row, row, vec, vec], (row, acc),
                 (_sds((L, D), F32), _sds((8, D), F32)))(x, dh, dres, nw, sc)


def final_loss(x, nw, target):
    L, D = x.shape
    tm = _tile(L, 512)

    def body(x_ref, nw_ref, t_ref, dx_ref, acc_ref):
        @pl.when(pl.program_id(0) == 0)
        def _():
            acc_ref[...] = jnp.zeros_like(acc_ref)

        xf = x_ref[...]
        r = lax.rsqrt(jnp.mean(xf * xf, axis=-1, keepdims=True) + EPS)
        xhat = xf * r
        nwv = nw_ref[...]
        err = xhat * nwv - t_ref[...]
        dy = err * (1.0 / D)
        dxhat = dy * nwv
        proj = jnp.mean(dxhat * xhat, axis=-1, keepdims=True)
        dx_ref[...] = r * (dxhat - xhat * proj)
        acc_ref[0:1, :] += jnp.sum(dy * xhat, axis=0, keepdims=True)
        acc_ref[1:2, :] += jnp.sum(err * err, axis=0, keepdims=True)

    row = pl.BlockSpec((tm, D), lambda i: (i, 0))
    vec = pl.BlockSpec((1, D), lambda i: (0, 0))
    acc = pl.BlockSpec((8, D), lambda i: (0, 0))
    return _call(body, "final_loss", (L // tm,), [row, vec, row], (row, acc),
                 (_sds((L, D), F32), _sds((8, D), F32)))(x, nw, target)


def resid_gate_bwd(dxo, y, gate, coef, name):
    L, D = dxo.shape
    tm = _tile(L, 512)

    def body(dxo_ref, y_ref, g_ref, dy_ref, acc_ref):
        @pl.when(pl.program_id(0) == 0)
        def _():
            acc_ref[...] = jnp.zeros_like(acc_ref)

        d = dxo_ref[...]
        dy = coef * g_ref[...] * d
        dy_ref[...] = dy.astype(BF)
        acc_ref[0:1, :] += coef * jnp.sum(d * y_ref[...].astype(F32), axis=0, keepdims=True)
        acc_ref[1:2, :] += jnp.sum(dy, axis=0, keepdims=True)

    row = pl.BlockSpec((tm, D), lambda i: (i, 0))
    vec = pl.BlockSpec((1, D), lambda i: (0, 0))
    acc = pl.BlockSpec((8, D), lambda i: (0, 0))
    return _call(body, name, (L // tm,), [row, row, vec], (row, acc),
                 (_sds((L, D), BF), _sds((8, D), F32)))(dxo, y, gate)


def ffn_up(h, wt):
    L, D = h.shape
    F = wt.shape[0] // 2
    tm, tn = _tile(L, 2048), _tile(F, 256)
    nj = F // tn

    def body(h_ref, wg_ref, wu_ref, g_ref, u_ref, a_ref):
        hv = h_ref[...]
        g = _dot_nt(hv, wg_ref[...])
        u = _dot_nt(hv, wu_ref[...])
        g_ref[...] = g.astype(BF)
        u_ref[...] = u.astype(BF)
        a_ref[...] = (g * _sigmoid(g) * u).astype(BF)

    o = pl.BlockSpec((tm, tn), lambda i, n: (i, n))
    return _call(body, "ffn_up", (L // tm, nj),
                 [pl.BlockSpec((tm, D), lambda i, n: (i, 0)),
                  pl.BlockSpec((tn, D), lambda i, n: (n, 0)),
                  pl.BlockSpec((tn, D), lambda i, n: (n + nj, 0))],
                 (o, o, o), tuple(_sds((L, F), BF) for _ in range(3)))(h, wt, wt)


def ffn_down_dgrad(dy, wd, g, u):
    L, D = dy.shape
    F = wd.shape[0]
    tm, tn = _tile(L, 2048), _tile(F, 256)

    def body(dy_ref, w_ref, g_ref, u_ref, dg_ref, du_ref):
        da = _dot_nt(dy_ref[...], w_ref[...])
        gv = g_ref[...].astype(F32)
        uv = u_ref[...].astype(F32)
        s = _sigmoid(gv)
        dg_ref[...] = (da * uv * s * (1.0 + gv * (1.0 - s))).astype(BF)
        du_ref[...] = (da * gv * s).astype(BF)

    o = pl.BlockSpec((tm, tn), lambda i, n: (i, n))
    return _call(body, "ffn_down_dgrad", (L // tm, F // tn),
                 [pl.BlockSpec((tm, D), lambda i, n: (i, 0)), pl.BlockSpec((tn, D), lambda i, n: (n, 0)), o, o],
                 (o, o), (_sds((L, F), BF), _sds((L, F), BF)))(dy, wd, g, u)


def ffn_up_wgrad(dg, du, h):
    L, F = dg.shape
    D = h.shape[1]
    tm = _tile(F, 256)
    nblk = F // tm

    def half(d, off, prev):
        def body(d_ref, h_ref, *rest):
            rest[-1][...] = _dot_tn(d_ref[...], h_ref[...]).astype(BF)

        in_specs = [pl.BlockSpec((L, tm), lambda i: (0, i)), pl.BlockSpec((L, D), lambda i: (0, 0))]
        args = [d, h]
        aliases = {}
        if prev is not None:
            in_specs.append(pl.BlockSpec(memory_space=pl.ANY))
            args.append(prev)
            aliases = {2: 0}
        return pl.pallas_call(
            body, name="ffn_up_wgrad", grid=(nblk,), in_specs=in_specs,
            out_specs=pl.BlockSpec((tm, D), lambda i: (i + off * nblk, 0)),
            out_shape=_sds((2 * F, D), BF), input_output_aliases=aliases,
            compiler_params=pltpu.CompilerParams(vmem_limit_bytes=VMEM_LIMIT))(*args)

    return half(du, 1, half(dg, 0, None))


def ffn_up_dgrad(dg, du, wt):
    L, F = dg.shape
    D = wt.shape[1]
    tm, tn = _tile(L, 1024), _tile(D, 512)

    def body(dg_ref, du_ref, wg_ref, wu_ref, o_ref):
        o_ref[...] = _dot_nn(dg_ref[...], wg_ref[...]) + _dot_nn(du_ref[...], wu_ref[...])

    a = pl.BlockSpec((tm, F), lambda i, n: (i, 0))
    return _call(body, "ffn_up_dgrad", (L // tm, D // tn),
                 [a, a, pl.BlockSpec((F, tn), lambda i, n: (0, n)), pl.BlockSpec((F, tn), lambda i, n: (1, n))],
                 pl.BlockSpec((tm, tn), lambda i, n: (i, n)), _sds((L, D), F32))(dg, du, wt, wt)


def _shift_rows(cur, other, k, down):
    n = cur.shape[0]
    rows = lax.broadcasted_iota(jnp.int32, cur.shape, 0)
    if down:
        return jnp.where(rows < k, pltpu.roll(other, k, 0), pltpu.roll(cur, k, 0))
    return jnp.where(rows >= n - k, pltpu.roll(other, n - k, 0), pltpu.roll(cur, n - k, 0))


def _conv_pre(cur, prev, w_ref, b_ref):
    s = cur * w_ref[CONV_WIDTH - 1:CONV_WIDTH, :] + b_ref[...]
    for k in range(1, CONV_WIDTH):
        s = s + _shift_rows(cur, prev, k, True) * w_ref[CONV_WIDTH - 1 - k:CONV_WIDTH - k, :]
    return s


_XBC_COL0 = D_INNER // 512


def conv_fwd(zx, w, b):
    L = zx.shape[0]
    tm, tc = _tile(L, 256), 512

    def body(cur_ref, prev_ref, w_ref, b_ref, o_ref):
        cur = cur_ref[...]
        prev = jnp.where(pl.program_id(1) > 0, prev_ref[...], 0.0)
        s = _conv_pre(cur, prev, w_ref, b_ref)
        o_ref[...] = s * _sigmoid(s)

    return _call(body, "conv_fwd", (CONV_DIM // tc, L // tm),
                 [pl.BlockSpec((tm, tc), lambda j, i: (i, _XBC_COL0 + j)),
                  pl.BlockSpec((tm, tc), lambda j, i: (jnp.maximum(i - 1, 0), _XBC_COL0 + j)),
                  pl.BlockSpec((CONV_WIDTH, tc), lambda j, i: (0, j)),
                  pl.BlockSpec((1, tc), lambda j, i: (0, j))],
                 pl.BlockSpec((tm, tc), lambda j, i: (i, j)), _sds((L, CONV_DIM), F32))(zx, zx, w, b)


def conv_bwd_act(dxc, zx, w, b):
    L = zx.shape[0]
    tm, tc = _tile(L, 256), 512

    def body(d_ref, cur_ref, prev_ref, w_ref, b_ref, o_ref):
        cur = cur_ref[...]
        prev = jnp.where(pl.program_id(1) > 0, prev_ref[...], 0.0)
        s = _conv_pre(cur, prev, w_ref, b_ref)
        sg = _sigmoid(s)
        o_ref[...] = d_ref[...] * sg * (1.0 + s * (1.0 - sg))

    return _call(body, "conv_bwd_act", (CONV_DIM // tc, L // tm),
                 [pl.BlockSpec((tm, tc), lambda j, i: (i, j)),
                  pl.BlockSpec((tm, tc), lambda j, i: (i, _XBC_COL0 + j)),
                  pl.BlockSpec((tm, tc), lambda j, i: (jnp.maximum(i - 1, 0), _XBC_COL0 + j)),
                  pl.BlockSpec((CONV_WIDTH, tc), lambda j, i: (0, j)),
                  pl.BlockSpec((1, tc), lambda j, i: (0, j))],
                 pl.BlockSpec((tm, tc), lambda j, i: (i, j)), _sds((L, CONV_DIM), F32))(dxc, zx, zx, w, b)


def conv_bwd(ds, zx, w):
    L = zx.shape[0]
    tm, tc = _tile(L, 256), 512
    nblk = L // tm

    def body(ds_ref, dsn_ref, cur_ref, prev_ref, w_ref, du_ref, acc_ref):
        i = pl.program_id(1)

        @pl.when(i == 0)
        def _():
            acc_ref[...] = jnp.zeros_like(acc_ref)

        ds_c = ds_ref[...]
        ds_n = jnp.where(i < nblk - 1, dsn_ref[...], 0.0)
        cur = cur_ref[...]
        prev = jnp.where(i > 0, prev_ref[...], 0.0)
        du = ds_c * w_ref[CONV_WIDTH - 1:CONV_WIDTH, :]
        acc_ref[CONV_WIDTH - 1:CONV_WIDTH, :] += jnp.sum(ds_c * cur, axis=0, keepdims=True)
        for k in range(1, CONV_WIDTH):
            du = du + _shift_rows(ds_c, ds_n, k, False) * w_ref[CONV_WIDTH - 1 - k:CONV_WIDTH - k, :]
            acc_ref[CONV_WIDTH - 1 - k:CONV_WIDTH - k, :] += jnp.sum(
                ds_c * _shift_rows(cur, prev, k, True), axis=0, keepdims=True)
        acc_ref[CONV_WIDTH:CONV_WIDTH + 1, :] += jnp.sum(ds_c, axis=0, keepdims=True)
        du_ref[...] = du.astype(BF)

    return _call(body, "conv_bwd", (CONV_DIM // tc, nblk),
                 [pl.BlockSpec((tm, tc), lambda j, i: (i, j)),
                  pl.BlockSpec((tm, tc), lambda j, i: (jnp.minimum(i + 1, nblk - 1), j)),
                  pl.BlockSpec((tm, tc), lambda j, i: (i, _XBC_COL0 + j)),
                  pl.BlockSpec((tm, tc), lambda j, i: (jnp.maximum(i - 1, 0), _XBC_COL0 + j)),
                  pl.BlockSpec((CONV_WIDTH, tc), lambda j, i: (0, j))],
                 (pl.BlockSpec((tm, tc), lambda j, i: (i, j)), pl.BlockSpec((8, tc), lambda j, i: (0, j))),
                 (_sds((L, CONV_DIM), BF), _sds((8, CONV_DIM), F32)))(ds, ds, zx, zx, w)


_DT_COL = (D_INNER + CONV_DIM) // 128


def dt_prep(zx, bias_pad, alog_pad):
    L = zx.shape[0]

    def body(raw_ref, b_ref, al_ref, dt_ref, acs_ref):
        v = raw_ref[...] + b_ref[...]
        dt = jnp.maximum(v, 0.0) + jnp.log(1.0 + jnp.exp(-jnp.abs(v)))
        dt_ref[...] = dt
        acs = dt * (-jnp.exp(al_ref[...]))
        rows = lax.broadcasted_iota(jnp.int32, acs.shape, 0)
        s = 1
        while s < CHUNK:
            acs = acs + jnp.where(rows >= s, pltpu.roll(acs, s, 0), 0.0)
            s *= 2
        acs_ref[...] = acs

    blk = pl.BlockSpec((CHUNK, 128), lambda i: (i, 0))
    vec = pl.BlockSpec((1, 128), lambda i: (0, 0))
    return _call(body, "dt_prep", (L // CHUNK,),
                 [pl.BlockSpec((CHUNK, 128), lambda i: (i, _DT_COL)), vec, vec], (blk, blk),
                 (_sds((L, 128), F32), _sds((L, 128), F32)))(zx, bias_pad, alog_pad)


def dt_bwd(ddt, da, dt, zx, bias_pad, alog_pad):
    L = zx.shape[0]
    tm = _tile(L, 512)

    def body(ddt_ref, da_ref, dt_ref, raw_ref, b_ref, al_ref, o_ref, acc_ref):
        @pl.when(pl.program_id(0) == 0)
        def _():
            acc_ref[...] = jnp.zeros_like(acc_ref)

        A = -jnp.exp(al_ref[...])
        dav = da_ref[...]
        dd = ddt_ref[...] + dav * A
        draw = dd * _sigmoid(raw_ref[...] + b_ref[...])
        o_ref[...] = draw.astype(BF)
        acc_ref[0:1, :] += jnp.sum(draw, axis=0, keepdims=True)
        acc_ref[1:2, :] += jnp.sum(dav * dt_ref[...], axis=0, keepdims=True) * A

    blk = pl.BlockSpec((tm, 128), lambda i: (i, 0))
    vec = pl.BlockSpec((1, 128), lambda i: (0, 0))
    return _call(body, "dt_bwd", (L // tm,),
                 [blk, blk, blk, pl.BlockSpec((tm, 128), lambda i: (i, _DT_COL)), vec, vec],
                 (blk, pl.BlockSpec((8, 128), lambda i: (0, 0))),
                 (_sds((L, 128), BF), _sds((8, 128), F32)))(ddt, da, dt, zx, bias_pad, alog_pad)


_HPG = SSM_HEADS // SSM_GROUPS
_GW = _HPG * SSM_HEADDIM
_B_COL0 = D_INNER // SSM_STATE
_C_COL0 = (D_INNER + SSM_GROUPS * SSM_STATE) // SSM_STATE


def _ssd_head(x, dtc, ac, ar, r, causal):
    xh = x[:, SSM_HEADDIM * r:SSM_HEADDIM * (r + 1)]
    acol = ac[:, r:r + 1]
    arow = ar[r:r + 1, :]
    alast = ar[r:r + 1, CHUNK - 1:CHUNK]
    lm = jnp.exp(jnp.where(causal, acol - arow, NEG))
    return xh, xh * dtc[:, r:r + 1], acol, alast, lm


def ssd_fwd(xc, dt_g, acs_g, acsT_g, d_exp):
    L = xc.shape[0]
    nc = L // CHUNK

    def body(x_ref, b_ref, c_ref, dt_ref, ac_ref, ar_ref, d_ref, y_ref, pst_ref, st_ref):
        @pl.when(pl.program_id(1) == 0)
        def _():
            st_ref[...] = jnp.zeros_like(st_ref)

        x, Bm, Cm = x_ref[...], b_ref[...], c_ref[...]
        dtc, ac, ar = dt_ref[...], ac_ref[...], ar_ref[...]
        causal = lax.broadcasted_iota(jnp.int32, (CHUNK, CHUNK), 0) >= lax.broadcasted_iota(jnp.int32, (CHUNK, CHUNK), 1)
        CB = _dot_nt(Cm, Bm)
        for r in range(_HPG):
            xh, xd, acol, alast, lm = _ssd_head(x, dtc, ac, ar, r, causal)
            P = st_ref[r]
            y = _dot_nn(CB * lm, xd) + jnp.exp(acol) * _dot_nt(Cm, P)
            y_ref[:, SSM_HEADDIM * r:SSM_HEADDIM * (r + 1)] = y + d_ref[:, SSM_HEADDIM * r:SSM_HEADDIM * (r + 1)] * xh
            pst_ref[r] = P
            st_ref[r] = P * jnp.exp(alast) + _dot_tn(xd * jnp.exp(alast - acol), Bm)

    return _call(
        body, "ssd_fwd", (SSM_GROUPS, nc),
        [pl.BlockSpec((CHUNK, _GW), lambda g, c: (c, g)),
         pl.BlockSpec((CHUNK, SSM_STATE), lambda g, c: (c, _B_COL0 + g)),
         pl.BlockSpec((CHUNK, SSM_STATE), lambda g, c: (c, _C_COL0 + g)),
         pl.BlockSpec((None, CHUNK, _HPG), lambda g, c: (g, c, 0)),
         pl.BlockSpec((None, CHUNK, _HPG), lambda g, c: (g, c, 0)),
         pl.BlockSpec((None, _HPG, CHUNK), lambda g, c: (g, 0, c)),
         pl.BlockSpec((None, 1, _GW), lambda g, c: (g, 0, 0))],
        (pl.BlockSpec((CHUNK, _GW), lambda g, c: (c, g)),
         pl.BlockSpec((None, None, _HPG, SSM_HEADDIM, SSM_STATE), lambda g, c: (c, g, 0, 0, 0))),
        (_sds((L, D_INNER), F32), _sds((nc, SSM_GROUPS, _HPG, SSM_HEADDIM, SSM_STATE), F32)),
        scratch=[pltpu.VMEM((_HPG, SSM_HEADDIM, SSM_STATE), F32)],
    )(xc, xc, xc, dt_g, acs_g, acsT_g, d_exp)


def ssd_bwd(dy, xc, dt_g, acs_g, acsT_g, pst, d_exp):
    L = xc.shape[0]
    nc = L // CHUNK

    def body(dy_ref, x_ref, b_ref, c_ref, dt_ref, ac_ref, ar_ref, pst_ref, d_ref,
             dx_ref, db_ref, dc_ref, ddt_ref, da_ref, dd_ref, dp_ref):
        @pl.when(pl.program_id(1) == 0)
        def _():
            dp_ref[...] = jnp.zeros_like(dp_ref)
            dd_ref[...] = jnp.zeros_like(dd_ref)

        dyv, x, Bm, Cm = dy_ref[...], x_ref[...], b_ref[...], c_ref[...]
        dtc, ac, ar = dt_ref[...], ac_ref[...], ar_ref[...]
        ri = lax.broadcasted_iota(jnp.int32, (CHUNK, CHUNK), 0)
        ci = lax.broadcasted_iota(jnp.int32, (CHUNK, CHUNK), 1)
        causal = ri >= ci
        lane4 = lax.broadcasted_iota(jnp.int32, (CHUNK, _HPG), 1)
        CB = _dot_nt(Cm, Bm)
        dB = jnp.zeros((CHUNK, SSM_STATE), F32)
        dC = jnp.zeros((CHUNK, SSM_STATE), F32)
        dCB = jnp.zeros((CHUNK, CHUNK), F32)
        ddt_blk = jnp.zeros((CHUNK, _HPG), F32)
        da_blk = jnp.zeros((CHUNK, _HPG), F32)
        for r in range(_HPG):
            sl = slice(SSM_HEADDIM * r, SSM_HEADDIM * (r + 1))
            xh, xd, acol, alast, lm = _ssd_head(x, dtc, ac, ar, r, causal)
            dyh = dyv[:, sl]
            P = pst_ref[r]
            dPn = dp_ref[r]
            eA = jnp.exp(acol)
            cd = jnp.exp(alast)
            dte = jnp.exp(alast - acol)
            G = CB * lm
            Z = _dot_nt(Cm, P)
            dZ = eA * dyh
            dC = dC + _dot_nn(dZ, P)
            dp_ref[r] = dPn * cd + _dot_tn(dZ, Cm)
            dA_col = jnp.sum(dZ * Z, axis=1, keepdims=True)
            BdS = _dot_nt(Bm, dPn)
            dxd = dte * BdS
            dB = dB + dte * _dot_nn(xd, dPn)
            t = jnp.sum(xd * BdS, axis=1, keepdims=True) * dte
            dA_col = dA_col - t
            dA_last = jnp.sum(t, axis=0, keepdims=True) + jnp.sum(
                jnp.sum(dPn * P, axis=1, keepdims=True), axis=0, keepdims=True) * cd
            dG = _dot_nt(dyh, xd)
            dxd = dxd + _dot_tn(G, dyh)
            dCB = dCB + dG * lm
            W = dG * G
            dA_col = dA_col + jnp.sum(W, axis=1, keepdims=True)
            dA_row = jnp.sum(jnp.where(ri == ci, dA_col, 0.0), axis=0, keepdims=True) - jnp.sum(W, axis=0, keepdims=True)
            da_col = jnp.sum(jnp.where(ci >= ri, dA_row, 0.0), axis=1, keepdims=True) + dA_last
            da_blk = jnp.where(lane4 == r, da_col, da_blk)
            ddt_blk = jnp.where(lane4 == r, jnp.sum(dxd * xh, axis=1, keepdims=True), ddt_blk)
            dx_ref[:, sl] = dxd * dtc[:, r:r + 1] + d_ref[:, sl] * dyh
        dc_ref[...] = dC + _dot_nn(dCB, Bm)
        db_ref[...] = dB + _dot_tn(dCB, Cm)
        ddt_ref[...] = ddt_blk
        da_ref[...] = da_blk
        dd_ref[...] += jnp.sum(dyv * x, axis=0, keepdims=True)

    rc = lambda g, c: (nc - 1 - c, g)
    small = pl.BlockSpec((None, CHUNK, _HPG), lambda g, c: (g, nc - 1 - c, 0))
    return _call(
        body, "ssd_bwd", (SSM_GROUPS, nc),
        [pl.BlockSpec((CHUNK, _GW), rc),
         pl.BlockSpec((CHUNK, _GW), rc),
         pl.BlockSpec((CHUNK, SSM_STATE), lambda g, c: (nc - 1 - c, _B_COL0 + g)),
         pl.BlockSpec((CHUNK, SSM_STATE), lambda g, c: (nc - 1 - c, _C_COL0 + g)),
         small, small,
         pl.BlockSpec((None, _HPG, CHUNK), lambda g, c: (g, 0, nc - 1 - c)),
         pl.BlockSpec((None, None, _HPG, SSM_HEADDIM, SSM_STATE), lambda g, c: (nc - 1 - c, g, 0, 0, 0)),
         pl.BlockSpec((None, 1, _GW), lambda g, c: (g, 0, 0))],
        (pl.BlockSpec((CHUNK, _GW), rc),
         pl.BlockSpec((CHUNK, SSM_STATE), rc),
         pl.BlockSpec((CHUNK, SSM_STATE), rc),
         small, small,
         pl.BlockSpec((None, 1, _GW), lambda g, c: (g, 0, 0))),
        (_sds((L, D_INNER), F32), _sds((L, SSM_GROUPS * SSM_STATE), F32), _sds((L, SSM_GROUPS * SSM_STATE), F32),
         _sds((SSM_GROUPS, L, _HPG), F32), _sds((SSM_GROUPS, L, _HPG), F32), _sds((SSM_GROUPS, 1, _GW), F32)),
        scratch=[pltpu.VMEM((_HPG, SSM_HEADDIM, SSM_STATE), F32)],
    )(dy, xc, xc, xc, dt_g, acs_g, acsT_g, pst, d_exp)


_NGW = D_INNER // SSM_GROUPS


def gate_norm(y, zx, nw):
    L = y.shape[0]
    tm = _tile(L, 256)

    def body(y_ref, z_ref, nw_ref, o_ref):
        for g in range(SSM_GROUPS):
            sl = slice(_NGW * g, _NGW * (g + 1))
            z = z_ref[:, sl]
            y2 = y_ref[:, sl] * (z * _sigmoid(z))
            r = lax.rsqrt(jnp.mean(y2 * y2, axis=-1, keepdims=True) + EPS)
            o_ref[:, sl] = (y2 * r * nw_ref[:, sl]).astype(BF)

    row = pl.BlockSpec((tm, D_INNER), lambda i: (i, 0))
    return _call(body, "gate_norm", (L // tm,), [row, row, pl.BlockSpec((1, D_INNER), lambda i: (0, 0))],
                 row, _sds((L, D_INNER), BF))(y, zx, nw)


def gate_norm_bwd(dyn, y, zx, nw):
    L = y.shape[0]
    tm = _tile(L, 256)

    def body(d_ref, y_ref, z_ref, nw_ref, dy_ref, dz_ref, acc_ref):
        @pl.when(pl.program_id(0) == 0)
        def _():
            acc_ref[...] = jnp.zeros_like(acc_ref)

        for g in range(SSM_GROUPS):
            sl = slice(_NGW * g, _NGW * (g + 1))
            z = z_ref[:, sl]
            yv = y_ref[:, sl]
            sg = _sigmoid(z)
            sz = z * sg
            y2 = yv * sz
            r = lax.rsqrt(jnp.mean(y2 * y2, axis=-1, keepdims=True) + EPS)
            yh = y2 * r
            d = d_ref[:, sl]
            dn = d * nw_ref[:, sl]
            dy2 = r * (dn - yh * jnp.mean(dn * yh, axis=-1, keepdims=True))
            dy_ref[:, sl] = dy2 * sz
            dz_ref[:, sl] = (dy2 * yv * sg * (1.0 + z * (1.0 - sg))).astype(BF)
            acc_ref[0:1, sl] += jnp.sum(d * yh, axis=0, keepdims=True)

    row = pl.BlockSpec((tm, D_INNER), lambda i: (i, 0))
    return _call(body, "gate_norm_bwd", (L // tm,), [row, row, row, pl.BlockSpec((1, D_INNER), lambda i: (0, 0))],
                 (row, row, pl.BlockSpec((8, D_INNER), lambda i: (0, 0))),
                 (_sds((L, D_INNER), F32), _sds((L, D_INNER), BF), _sds((8, D_INNER), F32)))(dyn, y, zx, nw)


_SCALE = HEAD_DIM ** -0.5
_REP = ATT_HEADS // KV_HEADS
_V_OFF = KV_HEADS * HEAD_DIM


def _stack_heads(ref, k):
    return jnp.concatenate([ref[:, HEAD_DIM * (k * _REP + r):HEAD_DIM * (k * _REP + r + 1)] for r in range(_REP)],
                           axis=0)


def _stack_sinks(s_ref, k):
    return jnp.concatenate([jnp.broadcast_to(s_ref[:, k * _REP + r:k * _REP + r + 1], (WINDOW, 1))
                            for r in range(_REP)], axis=0)


def _attn_probs(q4, kp, kc, sink, first):
    shape = (_REP * WINDOW, WINDOW)
    rows = jnp.bitwise_and(lax.broadcasted_iota(jnp.int32, shape, 0), WINDOW - 1)
    cols = lax.broadcasted_iota(jnp.int32, shape, 1)
    sp = jnp.where(jnp.logical_and(cols > rows, jnp.logical_not(first)), _dot_nt(q4, kp) * _SCALE, NEG)
    sc = jnp.where(cols <= rows, _dot_nt(q4, kc) * _SCALE, NEG)
    m = jnp.maximum(jnp.maximum(jnp.max(sp, axis=1, keepdims=True), jnp.max(sc, axis=1, keepdims=True)), sink)
    pp = jnp.exp(sp - m)
    pc = jnp.exp(sc - m)
    ps = jnp.exp(sink - m)
    inv = 1.0 / (jnp.sum(pp, axis=1, keepdims=True) + jnp.sum(pc, axis=1, keepdims=True) + ps)
    return pp * inv, pc * inv, ps * inv


def attn_fwd(q, kv, sinks_pad):
    L = q.shape[0]
    nb = L // WINDOW

    def body(q_ref, kc_ref, kp_ref, s_ref, o_ref):
        first = pl.program_id(0) == 0
        for k in range(KV_HEADS):
            ks = slice(HEAD_DIM * k, HEAD_DIM * (k + 1))
            vs = slice(_V_OFF + HEAD_DIM * k, _V_OFF + HEAD_DIM * (k + 1))
            pp, pc, _ = _attn_probs(_stack_heads(q_ref, k), kp_ref[:, ks], kc_ref[:, ks], _stack_sinks(s_ref, k), first)
            o4 = _dot_nn(pp, kp_ref[:, vs]) + _dot_nn(pc, kc_ref[:, vs])
            for r in range(_REP):
                h = k * _REP + r
                o_ref[:, HEAD_DIM * h:HEAD_DIM * (h + 1)] = o4[WINDOW * r:WINDOW * (r + 1)].astype(BF)

    qspec = pl.BlockSpec((WINDOW, D_MODEL), lambda i: (i, 0))
    return _call(body, "attn_fwd", (nb,),
                 [qspec, pl.BlockSpec((WINDOW, KV_DIM), lambda i: (i, 0)),
                  pl.BlockSpec((WINDOW, KV_DIM), lambda i: (jnp.maximum(i - 1, 0), 0)),
                  pl.BlockSpec((1, 128), lambda i: (0, 0))],
                 qspec, _sds((L, D_MODEL), BF))(q, kv, kv, sinks_pad)


def attn_bwd(q, kv, do, sinks_pad):
    L = q.shape[0]
    nb = L // WINDOW

    def body(q_ref, kc_ref, kp_ref, do_ref, s_ref, dq_ref, dc_ref, dp_ref, acc_ref):
        first = pl.program_id(0) == 0

        @pl.when(first)
        def _():
            acc_ref[...] = jnp.zeros_like(acc_ref)

        lane = lax.broadcasted_iota(jnp.int32, (1, 128), 1)
        dsink = jnp.zeros((1, 128), F32)
        for k in range(KV_HEADS):
            ks = slice(HEAD_DIM * k, HEAD_DIM * (k + 1))
            vs = slice(_V_OFF + HEAD_DIM * k, _V_OFF + HEAD_DIM * (k + 1))
            kp, kc, vp, vc = kp_ref[:, ks], kc_ref[:, ks], kp_ref[:, vs], kc_ref[:, vs]
            q4 = _stack_heads(q_ref, k)
            do4 = _stack_heads(do_ref, k)
            pp, pc, ps = _attn_probs(q4, kp, kc, _stack_sinks(s_ref, k), first)
            dpp = _dot_nt(do4, vp)
            dpc = _dot_nt(do4, vc)
            delta = jnp.sum(pp * dpp, axis=1, keepdims=True) + jnp.sum(pc * dpc, axis=1, keepdims=True)
            dsp = pp * (dpp - delta) * _SCALE
            dsc = pc * (dpc - delta) * _SCALE
            dq4 = _dot_nn(dsp, kp) + _dot_nn(dsc, kc)
            psd = ps * delta
            for r in range(_REP):
                h = k * _REP + r
                rs = slice(WINDOW * r, WINDOW * (r + 1))
                dq_ref[:, HEAD_DIM * h:HEAD_DIM * (h + 1)] = dq4[rs]
                dsink = dsink + jnp.where(lane == h, -jnp.sum(psd[rs], axis=0, keepdims=True), 0.0)
            dp_ref[:, ks] = _dot_tn(dsp, q4)
            dc_ref[:, ks] = _dot_tn(dsc, q4)
            dp_ref[:, vs] = _dot_tn(pp, do4)
            dc_ref[:, vs] = _dot_tn(pc, do4)
        acc_ref[0:1, :] += jnp.sum(dq_ref[...], axis=0, keepdims=True)
        acc_ref[1:2, 0:128] += dsink

    qspec = pl.BlockSpec((WINDOW, D_MODEL), lambda i: (i, 0))
    kspec = pl.BlockSpec((WINDOW, KV_DIM), lambda i: (i, 0))
    return _call(body, "attn_bwd", (nb,),
                 [qspec, kspec, pl.BlockSpec((WINDOW, KV_DIM), lambda i: (jnp.maximum(i - 1, 0), 0)), qspec,
                  pl.BlockSpec((1, 128), lambda i: (0, 0))],
                 (qspec, kspec, kspec, pl.BlockSpec((8, D_MODEL), lambda i: (0, 0))),
                 (_sds((L, D_MODEL), F32), _sds((L, KV_DIM), F32), _sds((L, KV_DIM), F32), _sds((8, D_MODEL), F32)),
                 )(q, kv, kv, do, sinks_pad)


def kv_grad_combine(parts):
    L = parts[0][0].shape[0]
    nb = L // WINDOW
    n = len(parts)

    def body(*refs):
        i = pl.program_id(0)
        o_ref, acc_ref = refs[2 * n], refs[2 * n + 1]

        @pl.when(i == 0)
        def _():
            acc_ref[...] = jnp.zeros_like(acc_ref)

        tot = refs[0][...]
        nxt = refs[1][...]
        for a in range(1, n):
            tot = tot + refs[2 * a][...]
            nxt = nxt + refs[2 * a + 1][...]
        tot = tot + jnp.where(i < nb - 1, nxt, 0.0)
        o_ref[...] = tot
        acc_ref[0:1, :] += jnp.sum(tot, axis=0, keepdims=True)

    cur = pl.BlockSpec((WINDOW, KV_DIM), lambda i: (i, 0))
    nxt = pl.BlockSpec((WINDOW, KV_DIM), lambda i: (jnp.minimum(i + 1, nb - 1), 0))
    args = [t for p in parts for t in p]
    return _call(body, "kv_grad_combine", (nb,), [cur, nxt] * n,
                 (cur, pl.BlockSpec((8, KV_DIM), lambda i: (0, 0))),
                 (_sds((L, KV_DIM), F32), _sds((8, KV_DIM), F32)))(*args)


def mod_fwd(c_all, w, b, name):
    n, _, C = w.shape

    def body(c_ref, w_ref, b_ref, o_ref, ca_ref):
        cv = c_ref[...]
        ca = cv * _sigmoid(cv)
        ca_ref[...] = ca
        o_ref[...] = _dot(ca, w_ref[...], ((1,), (0,))) + b_ref[...]

    return _call(body, name, (n,),
                 [pl.BlockSpec((N_DEV, D_MODEL), lambda i: (0, 0)),
                  pl.BlockSpec((None, D_MODEL, C), lambda i: (i, 0, 0)),
                  pl.BlockSpec((None, 1, C), lambda i: (i, 0, 0))],
                 (pl.BlockSpec((None, N_DEV, C), lambda i: (i, 0, 0)), pl.BlockSpec((N_DEV, D_MODEL), lambda i: (0, 0))),
                 (_sds((n, N_DEV, C), F32), _sds((N_DEV, D_MODEL), F32)))(c_all, w, b)


def mod_wgrad(c_act_t, dmod, name):
    n, _, C = dmod.shape
    tr = 256

    def body(ct_ref, d_ref, o_ref):
        acc = ct_ref[:, 0:1] * d_ref[0:1, :]
        for bidx in range(1, N_DEV):
            acc = acc + ct_ref[:, bidx:bidx + 1] * d_ref[bidx:bidx + 1, :]
        o_ref[...] = acc

    return _call(body, name, (n, D_MODEL // tr),
                 [pl.BlockSpec((tr, N_DEV), lambda i, j: (j, 0)),
                  pl.BlockSpec((None, N_DEV, C), lambda i, j: (i, 0, 0))],
                 pl.BlockSpec((None, tr, C), lambda i, j: (i, j, 0)), _sds((n, D_MODEL, C), F32))(c_act_t, dmod)


def _my_pos():
    return lax.axis_index("x"), lax.axis_index("y"), lax.axis_index("c")


def small_all_gather(v):
    m_per, n = v.shape

    def body(x_ref, out_ref, send_sems, recv_sems, local_sem):
        x, y, c = _my_pos()
        me, sibling = (x, y, c), (x, y, 1 - c)
        chips = [(1 - x, y), (x, 1 - y), (1 - x, 1 - y)]

        def rows(px, py, pc):
            return out_ref.at[pl.ds((4 * px + 2 * py + pc) * m_per, m_per), :]

        def copy(k, block, to, src=None):
            return pltpu.make_async_remote_copy(
                src_ref=rows(*block) if src is None else src, dst_ref=rows(*block),
                send_sem=send_sems.at[k], recv_sem=recv_sems.at[k], device_id=to, device_id_type=MESH)

        mine = pltpu.make_async_copy(x_ref, rows(*me), local_sem)
        mine.start()
        first = [copy(0, me, sibling, src=x_ref)]
        first += [copy(1 + j, me, (*chip, c), src=x_ref) for j, chip in enumerate(chips)]
        for cp in first:
            cp.start()
        passed = [copy(4 + j, (*chip, c), sibling) for j, chip in enumerate(chips)]
        for j, chip in enumerate(chips):
            copy(1 + j, (*chip, c), me).wait_recv()
            passed[j].start()
        copy(0, sibling, me).wait_recv()
        for j, chip in enumerate(chips):
            copy(4 + j, (*chip, 1 - c), me).wait_recv()
        for cp in first + passed:
            cp.wait_send()
        mine.wait()

    return pl.pallas_call(
        body, name="small_all_gather",
        out_shape=_sds((N_DEV * m_per, n), v.dtype),
        in_specs=[pl.BlockSpec(memory_space=pltpu.VMEM)],
        out_specs=pl.BlockSpec(memory_space=pltpu.VMEM),
        scratch_shapes=[pltpu.SemaphoreType.DMA((7,)), pltpu.SemaphoreType.DMA((7,)), pltpu.SemaphoreType.DMA],
        compiler_params=pltpu.CompilerParams(vmem_limit_bytes=VMEM_LIMIT),
    )(v)


def big_all_gather(arrs):
    n = len(arrs)

    def body(*refs):
        ins, outs = refs[:n], refs[n:2 * n]
        send_sems, recv_sems, local_sems = refs[2 * n], refs[2 * n + 1], refs[2 * n + 2]
        x, y, c = _my_pos()
        me, sibling = (x, y, c), (x, y, 1 - c)
        chips = [(1 - x, y), (x, 1 - y), (1 - x, 1 - y)]

        def slot(a, px, py, pc):
            return outs[a].at[4 * px + 2 * py + pc]

        def copy(a, k, block, to, src=None):
            return pltpu.make_async_remote_copy(
                src_ref=slot(a, *block) if src is None else src, dst_ref=slot(a, *block),
                send_sem=send_sems.at[7 * a + k], recv_sem=recv_sems.at[7 * a + k], device_id=to, device_id_type=MESH)

        mine = [pltpu.make_async_copy(ins[a], slot(a, *me), local_sems.at[a]) for a in range(n)]
        for cp in mine:
            cp.start()
        first = []
        for a in range(n):
            first.append(copy(a, 0, me, sibling, src=ins[a]))
            first += [copy(a, 1 + j, me, (*chip, c), src=ins[a]) for j, chip in enumerate(chips)]
        for cp in first:
            cp.start()
        passed = []
        for a in range(n):
            for j, chip in enumerate(chips):
                copy(a, 1 + j, (*chip, c), me).wait_recv()
                fwd = copy(a, 4 + j, (*chip, c), sibling)
                fwd.start()
                passed.append(fwd)
        for a in range(n):
            copy(a, 0, sibling, me).wait_recv()
            for j, chip in enumerate(chips):
                copy(a, 4 + j, (*chip, 1 - c), me).wait_recv()
        for cp in first + passed:
            cp.wait_send()
        for cp in mine:
            cp.wait()

    hbm = pl.BlockSpec(memory_space=pltpu.HBM)
    return pl.pallas_call(
        body, name="big_all_gather",
        out_shape=[_sds((N_DEV,) + a.shape, a.dtype) for a in arrs],
        in_specs=[hbm] * n, out_specs=[hbm] * n,
        scratch_shapes=[pltpu.SemaphoreType.DMA((7 * n,)), pltpu.SemaphoreType.DMA((7 * n,)),
                        pltpu.SemaphoreType.DMA((n,))],
    )(*arrs)


_FLIPS =[(fx, fy, fc) for fx in (0, 1) for fy in (0, 1) for fc in (0, 1)][1:]
_HBM = pl.BlockSpec(memory_space=pltpu.HBM)
_SEM = pl.BlockSpec(memory_space=pltpu.SEMAPHORE)
_EFFECT = pltpu.SideEffectType.DATAFLOW_SIDE_EFFECTING


def _flip(x, y, c, f):
    return (1 - x if f[0] else x), (1 - y if f[1] else y), (1 - c if f[2] else c)


def _xfer_copies(srcs, lands, send_sems, recv_sems, scatter):
    x, y, c = _my_pos()
    me = 4 * x + 2 * y + c
    copies = []
    for a in range(len(srcs)):
        for k, f in enumerate(_FLIPS):
            px, py, pc = _flip(x, y, c, f)
            src = srcs[a].at[4 * px + 2 * py + pc] if scatter else srcs[a]
            copies.append(pltpu.make_async_remote_copy(
                src_ref=src, dst_ref=lands[a].at[me], send_sem=send_sems.at[7 * a + k],
                recv_sem=recv_sems.at[7 * a + k], device_id=(px, py, pc), device_id_type=MESH))
    return copies


def _own_copies(srcs, lands, local_sems, scatter):
    x, y, c = _my_pos()
    me = 4 * x + 2 * y + c
    return [pltpu.make_async_copy(srcs[a].at[me] if scatter else srcs[a], lands[a].at[me], local_sems.at[a])
            for a in range(len(srcs))]


def xfer_start(arrs, scatter, after, name):
    n = len(arrs)
    land_shapes = [a.shape if scatter else (N_DEV,) + a.shape for a in arrs]

    def body(*refs):
        srcs, lands = refs[:n], refs[n:2 * n]
        send_sems, recv_sems, local_sems = refs[2 * n + 1], refs[2 * n + 2], refs[2 * n + 3]
        token = refs[-1]
        for cp in _xfer_copies(srcs, lands, send_sems, recv_sems, scatter):
            cp.start()
        for cp in _own_copies(srcs, lands, local_sems, scatter):
            cp.start()
        token[...] = jnp.zeros_like(token)

    out = pl.pallas_call(
        body, name=name,
        out_shape=(pltpu.SemaphoreType.DMA((7 * n,)), pltpu.SemaphoreType.DMA((7 * n,)),
                   pltpu.SemaphoreType.DMA((n,)),
                   *[pltpu.HBM(a.shape, a.dtype) for a in arrs],
                   *[pltpu.HBM(s, a.dtype) for s, a in zip(land_shapes, arrs)],
                   _sds((8, 128), F32)),
        in_specs=[_HBM] * (2 * n) + [pl.BlockSpec(memory_space=pl.ANY)],
        out_specs=(_SEM, _SEM, _SEM, *([_HBM] * (2 * n)), pl.BlockSpec(memory_space=pltpu.VMEM)),
        input_output_aliases={i: 3 + i for i in range(2 * n)},
        compiler_params=pltpu.CompilerParams(has_side_effects=_EFFECT),
    )(*[pltpu.with_memory_space_constraint(a, pltpu.HBM) for a in arrs],
      *[pltpu.with_memory_space_constraint(lax.empty(s, a.dtype), pltpu.HBM) for s, a in zip(land_shapes, arrs)],
      after)
    return (out[0], out[1], out[2], list(out[3:3 + n]), list(out[3 + n:3 + 2 * n]), scatter), out[-1]


def xfer_wait(handle, after, name):
    send_sems, recv_sems, local_sems, srcs, lands, scatter = handle
    n = len(srcs)

    def body(*refs):
        srcs_r, lands_r = refs[:n], refs[n:2 * n]
        ssem, rsem, lsem = refs[2 * n], refs[2 * n + 1], refs[2 * n + 2]
        for cp in _xfer_copies(srcs_r, lands_r, ssem, rsem, scatter):
            cp.wait_send()
            cp.wait_recv()
        for cp in _own_copies(srcs_r, lands_r, lsem, scatter):
            cp.wait()

    out = pl.pallas_call(
        body, name=name,
        out_shape=(*[pltpu.HBM(a.shape, a.dtype) for a in srcs], *[pltpu.HBM(a.shape, a.dtype) for a in lands]),
        in_specs=[_HBM] * (2 * n) + [_SEM, _SEM, _SEM, pl.BlockSpec(memory_space=pl.ANY)],
        out_specs=tuple([_HBM] * (2 * n)),
        input_output_aliases={i: i for i in range(2 * n)},
        compiler_params=pltpu.CompilerParams(has_side_effects=_EFFECT),
    )(*srcs, *lands, send_sems, recv_sems, local_sems, after)
    return list(out[n:])


def adamw(parts, w, m, v, name, row0=0, prev=None):
    r_tot, C = w.shape
    n_parts, R = parts.shape[0], parts.shape[1]
    row_bytes = 2 * (n_parts * C * parts.dtype.itemsize + 7 * C * 4)
    tr = R
    for cand in (512, 352, 256, 176, 128, 64):
        if R % cand == 0 and row0 % cand == 0 and R > cand and cand * row_bytes <= ADAMW_VMEM_BUDGET:
            tr = cand
            break
    tc = C
    if tr == R and R * row_bytes > ADAMW_VMEM_BUDGET:
        assert row0 == 0 and R == r_tot
        tc = next(t for t in (512, 256, 128) if C % t == 0 and R * row_bytes * t // C <= ADAMW_VMEM_BUDGET)
    assert row0 % tr == 0 and (tr % 8 == 0 or (tr == r_tot and row0 == 0))
    blk0 = row0 // tr
    c1 = 1.0 / (1.0 - ADAM_B1 ** ADAM_STEP)
    c2 = 1.0 / (1.0 - ADAM_B2 ** ADAM_STEP)

    def body(p_ref, w_ref, m_ref, v_ref, *rest):
        g_ref, d_ref, nm_ref, nv_ref = rest[-4:]
        g = p_ref[0].astype(F32)
        for k in range(1, n_parts):
            g = g + p_ref[k].astype(F32)
        nm = ADAM_B1 * m_ref[...] + (1.0 - ADAM_B1) * g
        nv = ADAM_B2 * v_ref[...] + (1.0 - ADAM_B2) * (g * g)
        g_ref[...] = g
        nm_ref[...] = nm
        nv_ref[...] = nv
        d_ref[...] = -ADAM_LR * ((nm * c1) / (jnp.sqrt(nv * c2) + ADAM_EPS) + ADAM_WD * w_ref[...])

    if tc == C:
        grid = (R // tr,)
        blk = pl.BlockSpec((tr, C), lambda i: (i + blk0, 0))
        p_spec = pl.BlockSpec((n_parts, tr, C), lambda i: (0, i, 0))
    else:
        grid = (C // tc,)
        blk = pl.BlockSpec((R, tc), lambda i: (0, i))
        p_spec = pl.BlockSpec((n_parts, R, tc), lambda i: (0, 0, i))
    in_specs = [p_spec, blk, blk, blk]
    args = [parts, w, m, v]
    aliases = {}
    if prev is not None:
        in_specs += [pl.BlockSpec(memory_space=pl.ANY)] * 4
        args += list(prev)
        aliases = {4 + k: k for k in range(4)}
    return pl.pallas_call(
        body, name=name, grid=grid, in_specs=in_specs, out_specs=(blk, blk, blk, blk),
        out_shape=tuple(_sds((r_tot, C), F32) for _ in range(4)), input_output_aliases=aliases,
        compiler_params=pltpu.CompilerParams(vmem_limit_bytes=VMEM_LIMIT))(*args)


def _ffn_fwd(x, nw, sh, sc, g, wt_gu, w_dn):
    h = norm_mod(x, nw, sh, sc, "ffn_norm")
    gp, up, a = ffn_up(h, wt_gu)
    y, xn = matmul(a, w_dn, "nn", BF, "ffn_down", res=x, gate=g, coef=0.5)
    return xn, (x, h, gp, up, a, y)


def _ffn_bwd(dxo, saved, nw, sc, g, wt_gu, w_dn):
    x, h, gp, up, a, y = saved
    dy, acc1 = resid_gate_bwd(dxo, y, g, 0.5, "ffn_gate_bwd")
    d_wdn = matmul(a, dy, "tn", BF, "ffn_down_wgrad")
    dg, du = ffn_down_dgrad(dy, w_dn, gp, up)
    d_wt = ffn_up_wgrad(dg, du, h)
    dh = ffn_up_dgrad(dg, du, wt_gu)
    dx, acc2 = norm_mod_bwd(x, dh, dxo, nw, sc, "ffn_norm_bwd")
    return dx, d_wt, d_wdn, (acc2[0], acc2[1], acc1[0]), acc2[2]


def _group_layout(a):
    L = a.shape[0]
    return a[:, :SSM_HEADS].reshape(L, SSM_GROUPS, _HPG).transpose(1, 0, 2)


def _ungroup_layout(a):
    L = a.shape[1]
    return jnp.pad(a.transpose(1, 0, 2).reshape(L, SSM_HEADS), ((0, 0), (0, 128 - SSM_HEADS)))


def _pad_row(vec, n=128):
    return jnp.pad(vec.reshape(1, -1), ((0, 0), (0, n - vec.shape[-1])))


def _mamba_fwd(x, nw, sh, sc, g, p):
    h = norm_mod(x, nw, sh, sc, "mix_norm")
    zx = matmul(h, p["w_in_t"], "nt", F32, "ssm_in")
    xc = conv_fwd(zx, p["conv_w"], p["conv_b"])
    dt, acs = dt_prep(zx, p["dt_bias"], p["a_log"])
    dt_g, acs_g = _group_layout(dt), _group_layout(acs)
    acs_t = acs_g.transpose(0, 2, 1)
    y, pst = ssd_fwd(xc, dt_g, acs_g, acs_t, p["d_exp"])
    yn = gate_norm(y, zx, p["norm_w"])
    yo, xn = matmul(yn, p["w_out"], "nn", BF, "ssm_out", res=x, gate=g, coef=1.0)
    return xn, (x, h, zx, xc, dt, dt_g, acs_g, acs_t, y, pst, yn, yo)


def _mamba_bwd(dxo, saved, nw, sc, g, p):
    x, h, zx, xc, dt, dt_g, acs_g, acs_t, y, pst, yn, yo = saved
    dyo, acc1 = resid_gate_bwd(dxo, yo, g, 1.0, "mix_gate_bwd")
    d_wout = matmul(yn, dyo, "tn", BF, "ssm_out_wgrad")
    dyn = matmul(dyo, p["w_out"], "nt", F32, "ssm_out_dgrad")
    dy, dz, accn = gate_norm_bwd(dyn, y, zx, p["norm_w"])
    dxs, dB, dC, ddt_g, da_g, dd = ssd_bwd(dy, xc, dt_g, acs_g, acs_t, pst, p["d_exp"])
    dxc = jnp.concatenate([dxs, dB, dC], axis=1)
    ds = conv_bwd_act(dxc, zx, p["conv_w"], p["conv_b"])
    du, accc = conv_bwd(ds, zx, p["conv_w"])
    draw, accdt = dt_bwd(_ungroup_layout(ddt_g), _ungroup_layout(da_g), dt, zx, p["dt_bias"], p["a_log"])
    dzx = jnp.concatenate([dz, du, draw], axis=1)
    d_win = matmul(dzx, h, "tn", BF, "ssm_in_wgrad")[:IN_PROJ]
    dh = matmul(dzx, p["w_in_t"], "nn", F32, "ssm_in_dgrad")
    dx, acc2 = norm_mod_bwd(x, dh, dxo, nw, sc, "mix_norm_bwd")
    small = dict(conv_w=accc[:CONV_WIDTH], conv_b=accc[CONV_WIDTH], dt_bias=accdt[0, :SSM_HEADS],
                 a_log=accdt[1, :SSM_HEADS], d=dd.reshape(SSM_HEADS, SSM_HEADDIM).sum(-1), norm_w=accn[0])
    return dx, d_win, d_wout, (acc2[0], acc2[1], acc1[0]), acc2[2], small


def _attn_layer_fwd(x, nw, sh, sc, g, p, kv):
    h = norm_mod(x, nw, sh, sc, "mix_norm")
    q = matmul(h, p["w_q"], "nn", F32, "attn_q", bias=p["b_q"])
    o = attn_fwd(q, kv, p["sinks"])
    yo, xn = matmul(o, p["w_o"], "nn", BF, "attn_o", bias=p["b_o"], res=x, gate=g, coef=1.0)
    return xn, (x, h, q, o, yo)


def _attn_layer_bwd(dxo, saved, nw, sc, g, p, kv):
    x, h, q, o, yo = saved
    dyo, acc1 = resid_gate_bwd(dxo, yo, g, 1.0, "mix_gate_bwd")
    d_wo = matmul(o, dyo, "tn", BF, "attn_o_wgrad")
    do = matmul(dyo, p["w_o"], "nt", F32, "attn_o_dgrad")
    dq, dkv_c, dkv_p, acca = attn_bwd(q, kv, do, p["sinks"])
    d_wq = matmul(h, dq, "tn", BF, "attn_q_wgrad")
    dh = matmul(dq, p["w_q"], "nt", F32, "attn_q_dgrad")
    dx, acc2 = norm_mod_bwd(x, dh, dxo, nw, sc, "mix_norm_bwd")
    small = dict(b_q=acca[0], sinks=acca[1, :ATT_HEADS], b_o=acc1[1])
    return dx, d_wq, d_wo, (acc2[0], acc2[1], acc1[0]), acc2[2], small, (dkv_c, dkv_p)


def _pack_rows(pieces):
    rows, spans, off = [], [], 0
    for a in pieces:
        flat = a.reshape(-1).astype(F32)
        n = -(-flat.shape[0] // D_MODEL)
        rows.append(jnp.pad(flat, (0, n * D_MODEL - flat.shape[0])).reshape(n, D_MODEL))
        spans.append((off, a.shape))
        off += n
    pad = -off % 8
    if pad:
        rows.append(jnp.zeros((pad, D_MODEL), F32))
    return jnp.concatenate(rows, axis=0), spans, off + pad


def _unpack_rows(g, spans):
    out = []
    for off, shape in spans:
        size = 1
        for s in shape:
            size *= s
        n = -(-size // D_MODEL)
        out.append(g[:, off:off + n].reshape(N_DEV, n * D_MODEL)[:, :size].reshape((N_DEV,) + tuple(shape)))
    return out


def _unshard_last(g):
    nd = g.ndim
    perm = tuple(range(1, nd - 1)) + (0, nd - 1)
    t = g.transpose(perm)
    return t.reshape(t.shape[:-2] + (N_DEV * g.shape[-1],))


def _shard_last(a, me):
    s = a.shape[-1] // N_DEV
    return lax.dynamic_slice_in_dim(a, me * s, s, axis=a.ndim - 1)


def kernel(x, c, ffn_norm_w, ffn_w_gu, ffn_w_down, mod_w, mod_b, mix_norm_w, ssm_w_in, ssm_conv_w, ssm_conv_b, ssm_dt_bias, ssm_a_log, ssm_d, ssm_norm_w, ssm_w_out, kv_norm_w, kv_mod_w, kv_mod_b, w_kv, b_kv, attn_w_q, attn_b_q, attn_sinks, attn_w_o, attn_b_o, final_norm_w, loss_target, m_ffn_norm_w, m_ffn_w_gu, m_ffn_w_down, m_mod_w, m_mod_b, m_mix_norm_w, m_ssm_w_in, m_ssm_conv_w, m_ssm_conv_b, m_ssm_dt_bias, m_ssm_a_log, m_ssm_d, m_ssm_norm_w, m_ssm_w_out, m_kv_norm_w, m_kv_mod_w, m_kv_mod_b, m_w_kv, m_b_kv, m_attn_w_q, m_attn_b_q, m_attn_sinks, m_attn_w_o, m_attn_b_o, m_final_norm_w, v_ffn_norm_w, v_ffn_w_gu, v_ffn_w_down, v_mod_w, v_mod_b, v_mix_norm_w, v_ssm_w_in, v_ssm_conv_w, v_ssm_conv_b, v_ssm_dt_bias, v_ssm_a_log, v_ssm_d, v_ssm_norm_w, v_ssm_w_out, v_kv_norm_w, v_kv_mod_w, v_kv_mod_b, v_w_kv, v_b_kv, v_attn_w_q, v_attn_b_q, v_attn_sinks, v_attn_w_o, v_attn_b_o, v_final_norm_w):
    D = D_MODEL
    me = 4 * lax.axis_index("x") + 2 * lax.axis_index("y") + lax.axis_index("c")
    xs = x[0]
    target = loss_target[0]
    mod_cols = mod_w.shape[-1]
    kvm_cols = kv_mod_w.shape[-1]

    def fence(arrs):
        tot = jnp.zeros((1, 1), F32)
        for a in arrs:
            tot = tot + lax.slice(a, (0,) * a.ndim, (1,) * a.ndim).reshape(1, 1).astype(F32)
        return jnp.broadcast_to(tot, (8, 128))

    packed, spans, _ = _pack_rows([c, ffn_norm_w, ssm_conv_w, ssm_conv_b, ssm_norm_w])
    nrow = packed.shape[0]
    g1 = small_all_gather(packed).reshape(N_DEV, nrow, D)
    c_all, fnw_g, cw_g, cb_g, snw_g = _unpack_rows(g1, spans)
    c_all = c_all.reshape(N_DEV, D)
    ffn_nw = _unshard_last(fnw_g)
    conv_w = _unshard_last(cw_g)
    conv_b = _unshard_last(cb_g)
    ssm_nw = _unshard_last(snw_g)

    mod_b_loc = lax.dynamic_slice_in_dim(mod_b, me * mod_cols, mod_cols, axis=1).reshape(DEPTH, 1, mod_cols)
    kvb_loc = lax.dynamic_slice_in_dim(kv_mod_b, me * kvm_cols, kvm_cols, axis=0).reshape(1, 1, kvm_cols)
    modp, c_act = mod_fwd(c_all, mod_w, mod_b_loc, "mod_fwd")
    kvmp, _ = mod_fwd(c_all, kv_mod_w.reshape(1, D, kvm_cols), kvb_loc, "kv_mod_fwd")
    packed2, spans2, _ = _pack_rows([modp, kvmp])
    nrow2 = packed2.shape[0]
    g2 = small_all_gather(packed2).reshape(N_DEV, nrow2, D)
    modp_g, kvmp_g = _unpack_rows(g2, spans2)
    mod_all = modp_g.transpose(1, 2, 0, 3).reshape(DEPTH, N_DEV, N_MOD * D)
    kvm_all = kvmp_g.transpose(1, 2, 0, 3).reshape(N_DEV, 2 * D)
    mod_me = lax.dynamic_index_in_dim(mod_all, me, axis=1, keepdims=False).reshape(DEPTH, N_MOD, 1, D)
    kvm_me = lax.dynamic_index_in_dim(kvm_all, me, axis=0, keepdims=False).reshape(2, 1, D)

    gu_t = jnp.swapaxes(ffn_w_gu, 2, 3)
    win_t = jnp.transpose(ssm_w_in, (2, 0, 1))
    S = gu_t.shape[2]
    s_in = win_t.shape[0]
    r_dn = ffn_w_down.shape[2]
    r_mix = ssm_w_out.shape[1]
    r_at = attn_w_q.shape[1]

    def layer_pack(k):
        arrs = [gu_t[k, 0].astype(BF), gu_t[k, 1].astype(BF), ffn_w_down[k, 0].astype(BF), ffn_w_down[k, 1].astype(BF)]
        if k < N_A:
            arrs += [ssm_w_out[k].astype(BF), win_t[:, k].astype(BF)]
        else:
            arrs += [attn_w_q[k - N_A].astype(BF), attn_w_o[k - N_A].astype(BF)]
        if k == N_A:
            arrs.append(w_kv.astype(BF))
        return arrs

    packs = [layer_pack(k) for k in range(DEPTH)]
    gathered = [None] * DEPTH
    first = big_all_gather([packs[0][0], packs[0][2]])
    pend_rest, tok_next = xfer_start([packs[0][1]] + packs[0][3:], False, fence([g2, first[0]]), "gather_start_0")
    pending = [None] * DEPTH
    for k in range(1, DEPTH):
        pending[k], tok_next = xfer_start(packs[k], False, tok_next, "gather_start_%d" % k)
    gathered[0] = [first[0], None, first[1]]

    def wt_gu_full(i, j):
        return gathered[i][j].reshape(N_DEV * S, D)

    def w_dn_full(i, j):
        return gathered[i][2 + j].reshape(D_FF, D)

    def mix_rows(i, a):
        return gathered[i][4 + a].reshape(-1, D)

    def mamba_params(j):
        w_in_t = jnp.pad(mix_rows(j, 1), ((0, IN_PROJ_PAD - IN_PROJ), (0, 0)))
        return dict(w_in_t=w_in_t, w_out=mix_rows(j, 0), conv_w=conv_w[j], conv_b=conv_b[j].reshape(1, -1),
                    dt_bias=_pad_row(ssm_dt_bias[j]), a_log=_pad_row(ssm_a_log[j]),
                    d_exp=jnp.repeat(ssm_d[j], SSM_HEADDIM).reshape(SSM_GROUPS, 1, _GW),
                    norm_w=ssm_nw[j].reshape(1, -1))

    def attn_params(j):
        return dict(w_q=mix_rows(N_A + j, 0), w_o=mix_rows(N_A + j, 1),
                    b_q=attn_b_q[j].reshape(1, -1), b_o=attn_b_o[j].reshape(1, -1), sinks=_pad_row(attn_sinks[j]))

    saved = []
    kv = None
    kv_saved = None
    w_kv_full = None
    xcur = xs
    for i in range(DEPTH):
        if i >= 1:
            gathered[i] = xfer_wait(pending[i], xcur, "gather_wait_%d" % i)
        md = mod_me[i]
        if i == 0:
            md = md + tok_next[0, 0]
        if i == N_A:
            w_kv_full = gathered[N_A][6].reshape(D, KV_DIM)
            h_kv = norm_mod(xcur, kv_norm_w.reshape(1, D), kvm_me[0], kvm_me[1], "kv_norm")
            kv = matmul(h_kv, w_kv_full, "nn", F32, "kv_proj", bias=b_kv.reshape(1, -1))
            kv_saved = (xcur, h_kv)
        x1, s1 = _ffn_fwd(xcur, ffn_nw[i, 0].reshape(1, D), md[0], md[1], md[2], wt_gu_full(i, 0), w_dn_full(i, 0))
        if i == 0:
            rest = xfer_wait(pend_rest, x1, "gather_wait_0")
            gathered[0] = [first[0], rest[0], first[1]] + rest[1:]
        if i < N_A:
            pm = mamba_params(i)
            x2, s2 = _mamba_fwd(x1, mix_norm_w[i].reshape(1, D), md[3], md[4], md[5], pm)
        else:
            pm = attn_params(i - N_A)
            x2, s2 = _attn_layer_fwd(x1, mix_norm_w[i].reshape(1, D), md[3], md[4], md[5], pm, kv)
        x3, s3 = _ffn_fwd(x2, ffn_nw[i, 1].reshape(1, D), md[6], md[7], md[8], wt_gu_full(i, 1), w_dn_full(i, 1))
        saved.append((s1, s2, s3, pm))
        xcur = x3

    dx, accf = final_loss(xcur, final_norm_w.reshape(1, D), target)
    d_mod = [None] * DEPTH
    d_ffn_nw = [[None, None] for _ in range(DEPTH)]
    d_mix_nw = [None] * DEPTH
    sm_m, sm_a = [None] * N_A, [None] * N_A
    kv_parts = [None] * N_A
    d_kvm = d_kv_nw = d_bkv = None
    exchanges = []
    tok = None

    def send(arrs, tag, after=None):
        handle, t = xfer_start(arrs, True, dx if after is None else after, "exch_start_%s" % tag)
        exchanges.append((handle, tag))
        return t

    def ffn_slabs(d_wt, d_wdn):
        return [d_wt.reshape(N_DEV, S, D), d_wdn.reshape(N_DEV, r_dn, D)]

    for i in reversed(range(DEPTH)):
        md = mod_me[i]
        s1, s2, s3, pm = saved[i]
        g2 = md[8] if tok is None else md[8] + tok[0, 0]
        dx, d_wt, d_wdn, m2, d_ffn_nw[i][1] = _ffn_bwd(
            dx, s3, ffn_nw[i, 1].reshape(1, D), md[7], g2, wt_gu_full(i, 1), w_dn_full(i, 1))
        tok = send(ffn_slabs(d_wt, d_wdn), "f%d1" % i)
        gm = md[5] + tok[0, 0]
        if i < N_A:
            dx, d_in, d_out, mm_, d_mix_nw[i], sm_m[i] = _mamba_bwd(dx, s2, mix_norm_w[i].reshape(1, D), md[4], gm, pm)
            tok = send([d_in.reshape(N_DEV, s_in, D), d_out.reshape(N_DEV, r_mix, D)], "m%d" % i)
        else:
            j = i - N_A
            dx, d_q, d_o, mm_, d_mix_nw[i], sm_a[j], kv_parts[j] = _attn_layer_bwd(
                dx, s2, mix_norm_w[i].reshape(1, D), md[4], gm, pm, kv)
            tok = send([d_q.reshape(N_DEV, r_at, D), d_o.reshape(N_DEV, r_at, D)], "m%d" % i)
        g1 = md[2] + tok[0, 0]
        dx, d_wt, d_wdn, m1, d_ffn_nw[i][0] = _ffn_bwd(
            dx, s1, ffn_nw[i, 0].reshape(1, D), md[1], g1, wt_gu_full(i, 0), w_dn_full(i, 0))
        d_mod[i] = jnp.concatenate(list(m1) + list(mm_) + list(m2), axis=0)
        last = ffn_slabs(d_wt, d_wdn)
        if i == N_A:
            x_kv, h_kv = kv_saved
            dkv, acck = kv_grad_combine(kv_parts)
            d_bkv = acck[0]
            d_kv_w = matmul(h_kv, dkv, "tn", BF, "kv_wgrad")
            dh_kv = matmul(dkv, w_kv_full, "nt", F32, "kv_dgrad")
            dx, acc_kv = norm_mod_bwd(x_kv, dh_kv, dx, kv_norm_w.reshape(1, D), kvm_me[1], "kv_norm_bwd")
            d_kvm = jnp.concatenate([acc_kv[0], acc_kv[1]], axis=0)
            d_kv_nw = acc_kv[2]
            last.append(d_kv_w.reshape(N_DEV, -1, KV_DIM))
        if i > 0:
            tok = send(last, "f%d0" % i)
    grad_x = dx.reshape(x.shape)

    small_list = [
        jnp.stack(d_mod, 0), d_kvm,
        jnp.stack([jnp.stack(r, 0) for r in d_ffn_nw], 0),
        jnp.stack(d_mix_nw, 0),
        jnp.stack([s["conv_w"] for s in sm_m], 0), jnp.stack([s["conv_b"] for s in sm_m], 0),
        jnp.stack([s["dt_bias"] for s in sm_m], 0), jnp.stack([s["a_log"] for s in sm_m], 0),
        jnp.stack([s["d"] for s in sm_m], 0), jnp.stack([s["norm_w"] for s in sm_m], 0),
        d_kv_nw, d_bkv,
        jnp.stack([s["b_q"] for s in sm_a], 0), jnp.stack([s["sinks"] for s in sm_a], 0),
        jnp.stack([s["b_o"] for s in sm_a], 0), accf[0], accf[1],
    ]
    packed3, spans3, _ = _pack_rows(small_list)
    nrow3 = packed3.shape[0]
    g3 = small_all_gather(packed3).reshape(N_DEV, nrow3, D)
    tok_last = send(last, "f00", after=g3)
    (p_mod, p_kvm, p_fnw, p_mnw, p_cw, p_cb, p_dtb, p_al, p_d, p_snw, p_kvnw, p_bkv, p_bq, p_sk, p_bo, p_fin,
     p_loss) = _unpack_rows(g3, spans3)

    loss = 0.5 / D * jnp.sum(p_loss)

    c_act_t = c_act.T
    dmod_loc = _shard_last(p_mod, me).transpose(1, 0, 2)
    dkvm_loc = _shard_last(p_kvm, me).reshape(1, N_DEV, kvm_cols) + tok_last[0, 0]
    gp_mod_w = mod_wgrad(c_act_t, dmod_loc, "mod_wgrad")
    gp_kvm_w = mod_wgrad(c_act_t, dkvm_loc, "kv_mod_wgrad")[0]

    def as_parts_single(a):
        return a[None]

    def upd(name, parts, w, m, v):
        shp = w.shape
        c_last = shp[-1]
        out = adamw(parts.reshape(parts.shape[0], -1, c_last), w.reshape(-1, c_last), m.reshape(-1, c_last),
                    v.reshape(-1, c_last), "adamw_" + name)
        return tuple(o.reshape(shp) for o in out)

    views = {
        "ffn_w_gu": [jnp.swapaxes(t, 2, 3).reshape(-1, D) for t in (ffn_w_gu, m_ffn_w_gu, v_ffn_w_gu)],
        "ffn_w_down": [t.reshape(-1, D) for t in (ffn_w_down, m_ffn_w_down, v_ffn_w_down)],
        "ssm_w_out": [t.reshape(-1, D) for t in (ssm_w_out, m_ssm_w_out, v_ssm_w_out)],
        "attn_w_q": [t.reshape(-1, D) for t in (attn_w_q, m_attn_w_q, v_attn_w_q)],
        "attn_w_o": [t.reshape(-1, D) for t in (attn_w_o, m_attn_w_o, v_attn_w_o)],
    }
    filled = {k: None for k in views}

    def upd_rows(name, parts, row0):
        w, m, v = views[name]
        filled[name] = adamw(parts, w, m, v, "adamw_" + name, row0=row0, prev=filled[name])
        return filled[name][3]

    res = {}
    res["ffn_norm_w"] = upd("ffn_norm_w", _shard_last(p_fnw, me), ffn_norm_w, m_ffn_norm_w, v_ffn_norm_w)
    res["mod_w"] = upd("mod_w", as_parts_single(gp_mod_w), mod_w, m_mod_w, v_mod_w)
    res["mod_b"] = upd("mod_b", p_mod, mod_b, m_mod_b, v_mod_b)
    res["mix_norm_w"] = upd("mix_norm_w", p_mnw, mix_norm_w, m_mix_norm_w, v_mix_norm_w)
    res["ssm_conv_w"] = upd("ssm_conv_w", _shard_last(p_cw, me), ssm_conv_w, m_ssm_conv_w, v_ssm_conv_w)
    res["ssm_conv_b"] = upd("ssm_conv_b", _shard_last(p_cb, me), ssm_conv_b, m_ssm_conv_b, v_ssm_conv_b)
    res["ssm_dt_bias"] = upd("ssm_dt_bias", p_dtb, ssm_dt_bias, m_ssm_dt_bias, v_ssm_dt_bias)
    res["ssm_a_log"] = upd("ssm_a_log", p_al, ssm_a_log, m_ssm_a_log, v_ssm_a_log)
    res["ssm_d"] = upd("ssm_d", p_d, ssm_d, m_ssm_d, v_ssm_d)
    res["ssm_norm_w"] = upd("ssm_norm_w", _shard_last(p_snw, me), ssm_norm_w, m_ssm_norm_w, v_ssm_norm_w)
    res["kv_norm_w"] = upd("kv_norm_w", p_kvnw.reshape(N_DEV, 1, D), kv_norm_w.reshape(1, D),
                           m_kv_norm_w.reshape(1, D), v_kv_norm_w.reshape(1, D))
    res["kv_mod_w"] = upd("kv_mod_w", as_parts_single(gp_kvm_w), kv_mod_w, m_kv_mod_w, v_kv_mod_w)
    res["kv_mod_b"] = upd("kv_mod_b", p_kvm.reshape(N_DEV, 1, 2 * D), kv_mod_b.reshape(1, -1),
                          m_kv_mod_b.reshape(1, -1), v_kv_mod_b.reshape(1, -1))
    res["b_kv"] = upd("b_kv", p_bkv.reshape(N_DEV, 1, KV_DIM), b_kv.reshape(1, -1), m_b_kv.reshape(1, -1),
                      v_b_kv.reshape(1, -1))
    res["attn_b_q"] = upd("attn_b_q", p_bq, attn_b_q, m_attn_b_q, v_attn_b_q)
    res["attn_sinks"] = upd("attn_sinks", p_sk, attn_sinks, m_attn_sinks, v_attn_sinks)
    res["attn_b_o"] = upd("attn_b_o", p_bo, attn_b_o, m_attn_b_o, v_attn_b_o)
    res["final_norm_w"] = upd("final_norm_w", p_fin.reshape(N_DEV, 1, D), final_norm_w.reshape(1, D),
                              m_final_norm_w.reshape(1, D), v_final_norm_w.reshape(1, D))

    chain = fence([dx, tok_last] + [t[3] for t in res.values()])
    r_in_parts = [None] * N_A
    r_kv = None
    for handle, tag in exchanges:
        got = xfer_wait(handle, chain, "exch_wait_%s" % tag)
        i = int(tag[1])
        if tag[0] == "f":
            jf = int(tag[2])
            done = [upd_rows("ffn_w_gu", got[0], (2 * i + jf) * S), upd_rows("ffn_w_down", got[1], (2 * i + jf) * r_dn)]
            if len(got) > 2:
                res["w_kv"] = upd("w_kv", got[2], w_kv, m_w_kv, v_w_kv)
                done.append(res["w_kv"][3])
        elif i < N_A:
            r_in_parts[i] = got[0]
            done = [upd_rows("ssm_w_out", got[1], i * r_mix)]
            if i == 0:
                win_out = adamw(jnp.stack(r_in_parts, axis=2).reshape(N_DEV, s_in * N_A, D),
                                *[jnp.transpose(t, (2, 0, 1)).reshape(-1, D) for t in (ssm_w_in, m_ssm_w_in, v_ssm_w_in)],
                                "adamw_ssm_w_in")
                res["ssm_w_in"] = tuple(jnp.transpose(t.reshape(win_t.shape), (1, 2, 0)) for t in win_out)
                done.append(win_out[3])
        else:
            done = [upd_rows("attn_w_q", got[0], (i - N_A) * r_at), upd_rows("attn_w_o", got[1], (i - N_A) * r_at)]
        chain = fence(done)

    res["ffn_w_gu"] = tuple(jnp.swapaxes(t.reshape(gu_t.shape), 2, 3) for t in filled["ffn_w_gu"])
    res["ffn_w_down"] = tuple(t.reshape(ffn_w_down.shape) for t in filled["ffn_w_down"])
    res["ssm_w_out"] = tuple(t.reshape(ssm_w_out.shape) for t in filled["ssm_w_out"])
    res["attn_w_q"] = tuple(t.reshape(attn_w_q.shape) for t in filled["attn_w_q"])
    res["attn_w_o"] = tuple(t.reshape(attn_w_o.shape) for t in filled["attn_w_o"])

    names = ["ffn_norm_w", "ffn_w_gu", "ffn_w_down", "mod_w", "mod_b", "mix_norm_w", "ssm_w_in", "ssm_conv_w",
             "ssm_conv_b", "ssm_dt_bias", "ssm_a_log", "ssm_d", "ssm_norm_w", "ssm_w_out", "kv_norm_w", "kv_mod_w",
             "kv_mod_b", "w_kv", "b_kv", "attn_w_q", "attn_b_q", "attn_sinks", "attn_w_o", "attn_b_o", "final_norm_w"]
    vec_shapes = {"kv_norm_w": (D,), "kv_mod_b": (2 * D,), "b_kv": (KV_DIM,), "final_norm_w": (D,)}
    outs = [loss, grad_x]
    for k in range(4):
        for nme in names:
            t = res[nme][k]
            if nme in vec_shapes:
                t = t.reshape(vec_shapes[nme])
            outs.append(t)
    return tuple(outs)
```

```python
import functools

import jax
import jax.numpy as jnp
from jax import lax
from jax.experimental import pallas as pl
from jax.experimental.pallas import tpu as pltpu

F32 = jnp.float32
BF = jnp.bfloat16
MESH = pl.DeviceIdType.MESH

N_DEV = 8
D_MODEL = 1024
DEPTH = 4
N_A = 2
EPS = 1e-5
N_MOD = 9
D_FF = 2816
D_INNER = 2048
SSM_HEADDIM = 64
SSM_HEADS = 32
SSM_GROUPS = 8
SSM_STATE = 128
CONV_WIDTH = 4
CHUNK = 512
CONV_DIM = D_INNER + 2 * SSM_GROUPS * SSM_STATE
IN_PROJ = D_INNER + CONV_DIM + SSM_HEADS
IN_PROJ_PAD = D_INNER + CONV_DIM + 128
ATT_HEADS = 16
KV_HEADS = 4
HEAD_DIM = 64
WINDOW = 128
KV_DIM = 2 * KV_HEADS * HEAD_DIM

ADAM_LR = 0.001
ADAM_B1 = 0.9
ADAM_B2 = 0.999
ADAM_EPS = 1e-08
ADAM_WD = 0.01
ADAM_STEP = 10

VMEM_LIMIT = 48 * 2 ** 20
ADAMW_VMEM_BUDGET = 24 * 2 ** 20
NEG = -1e30


def _call(body, name, grid, in_specs, out_specs, out_shape, scratch=()):
    return pl.pallas_call(
        body, name=name, grid=grid, in_specs=in_specs, out_specs=out_specs, out_shape=out_shape,
        scratch_shapes=list(scratch),
        compiler_params=pltpu.CompilerParams(vmem_limit_bytes=VMEM_LIMIT))


def _tile(n, cap):
    t = (cap // 128) * 128
    while t >= 128:
        if n % t == 0:
            return t
        t -= 128
    return n


def _sds(shape, dtype):
    return jax.ShapeDtypeStruct(shape, dtype)


def _sigmoid(v):
    return 1.0 / (1.0 + jnp.exp(-v))


def _dot(a, b, dims):
    return lax.dot_general(a, b, (dims, ((), ())), preferred_element_type=F32)


def _dot_nn(a, b):
    return _dot(a.astype(BF), b.astype(BF), ((1,), (0,)))


def _dot_nt(a, b):
    return _dot(a.astype(BF), b.astype(BF), ((1,), (1,)))


def _dot_tn(a, b):
    return _dot(a.astype(BF), b.astype(BF), ((0,), (0,)))


def matmul(a, b, mode, out_dtype, name, bias=None, res=None, gate=None, coef=1.0):
    if mode == "nn":
        (M, K), (_, N) = a.shape, b.shape
    elif mode == "nt":
        (M, K), (N, _) = a.shape, b.shape
    else:
        (K, M), (_, N) = a.shape, b.shape
    cap_n = 512 if K > 4096 else 1024
    tm = _tile(M, 1024 if (mode == "tn" or K <= D_FF) else 512)
    tn = _tile(N, cap_n)
    if mode != "tn" and tm * tn > 1024 * 896:
        tn = _tile(N, 512)
    if mode == "nn":
        a_spec = pl.BlockSpec((tm, K), lambda i, j: (i, 0))
        b_spec = pl.BlockSpec((K, tn), lambda i, j: (0, j))
        fn = _dot_nn
    elif mode == "nt":
        a_spec = pl.BlockSpec((tm, K), lambda i, j: (i, 0))
        b_spec = pl.BlockSpec((tn, K), lambda i, j: (j, 0))
        fn = _dot_nt
    else:
        a_spec = pl.BlockSpec((K, tm), lambda i, j: (0, i))
        b_spec = pl.BlockSpec((K, tn), lambda i, j: (0, j))
        fn = _dot_tn
    has_bias, has_res = bias is not None, res is not None
    o_spec = pl.BlockSpec((tm, tn), lambda i, j: (i, j))
    v_spec = pl.BlockSpec((1, tn), lambda i, j: (0, j))
    in_specs, args = [a_spec, b_spec], [a, b]
    if has_bias:
        in_specs.append(v_spec)
        args.append(bias)
    if has_res:
        in_specs += [o_spec, v_spec]
        args += [res, gate]

    def body(*refs):
        a_ref, b_ref = refs[0], refs[1]
        k = 2
        y = fn(a_ref[...], b_ref[...])
        if has_bias:
            y = y + refs[k][...]
            k += 1
        if has_res:
            res_ref, gate_ref = refs[k], refs[k + 1]
            refs[k + 2][...] = y.astype(out_dtype)
            refs[k + 3][...] = res_ref[...] + coef * gate_ref[...] * y
        else:
            refs[k][...] = y.astype(out_dtype)

    if has_res:
        out_shape = (_sds((M, N), out_dtype), _sds((M, N), F32))
        out_specs = (o_spec, o_spec)
    else:
        out_shape = _sds((M, N), out_dtype)
        out_specs = o_spec
    return _call(body, name, (M // tm, N // tn), in_specs, out_specs, out_shape)(*args)


def norm_mod(x, nw, sh, sc, name):
    L, D = x.shape
    tm = _tile(L, 512)

    def body(x_ref, nw_ref, sh_ref, sc_ref, h_ref):
        xf = x_ref[...]
        r = lax.rsqrt(jnp.mean(xf * xf, axis=-1, keepdims=True) + EPS)
        n = xf * r * nw_ref[...]
        h_ref[...] = (n * (1.0 + sc_ref[...]) + sh_ref[...]).astype(BF)

    row = pl.BlockSpec((tm, D), lambda i: (i, 0))
    vec = pl.BlockSpec((1, D), lambda i: (0, 0))
    return _call(body, name, (L // tm,), [row, vec, vec, vec], row, _sds((L, D), BF))(x, nw, sh, sc)


def norm_mod_bwd(x, dh, dres, nw, sc, name):
    L, D = x.shape
    tm = _tile(L, 512)

    def body(x_ref, dh_ref, dres_ref, nw_ref, sc_ref, dx_ref, acc_ref):
        @pl.when(pl.program_id(0) == 0)
        def _():
            acc_ref[...] = jnp.zeros_like(acc_ref)

        xf = x_ref[...]
        dhf = dh_ref[...].astype(F32)
        r = lax.rsqrt(jnp.mean(xf * xf, axis=-1, keepdims=True) + EPS)
        xhat = xf * r
        nwv = nw_ref[...]
        dn = dhf * (1.0 + sc_ref[...])
        dxhat = dn * nwv
        proj = jnp.mean(dxhat * xhat, axis=-1, keepdims=True)
        dx_ref[...] = dres_ref[...] + r * (dxhat - xhat * proj)
        acc_ref[0:1, :] += jnp.sum(dhf, axis=0, keepdims=True)
        acc_ref[1:2, :] += jnp.sum(dhf * xhat * nwv, axis=0, keepdims=True)
        acc_ref[2:3, :] += jnp.sum(dn * xhat, axis=0, keepdims=True)

    row = pl.BlockSpec((tm, D), lambda i: (i, 0))
    vec = pl.BlockSpec((1, D), lambda i: (0, 0))
    acc = pl.BlockSpec((8, D), lambda i: (0, 0))
    return _call(body, name, (L // tm,), [row, row, row, vec, vec], (row, acc),
                 (_sds((L, D), F32), _sds((8, D), F32)))(x, dh, dres, nw, sc)


def final_loss(x, nw, target):
    L, D = x.shape
    tm = _tile(L, 512)

    def body(x_ref, nw_ref, t_ref, dx_ref, acc_ref):
        @pl.when(pl.program_id(0) == 0)
        def _():
            acc_ref[...] = jnp.zeros_like(acc_ref)

        xf = x_ref[...]
        r = lax.rsqrt(jnp.mean(xf * xf, axis=-1, keepdims=True) + EPS)
        xhat = xf * r
        nwv = nw_ref[...]
        err = xhat * nwv - t_ref[...]
        dy = err * (1.0 / D)
        dxhat = dy * nwv
        proj = jnp.mean(dxhat * xhat, axis=-1, keepdims=True)
        dx_ref[...] = r * (dxhat - xhat * proj)
        acc_ref[0:1, :] += jnp.sum(dy * xhat, axis=0, keepdims=True)
        acc_ref[1:2, :] += jnp.sum(err * err, axis=0, keepdims=True)

    row = pl.BlockSpec((tm, D), lambda i: (i, 0))
    vec = pl.BlockSpec((1, D), lambda i: (0, 0))
    acc = pl.BlockSpec((8, D), lambda i: (0, 0))
    return _call(body, "final_loss", (L // tm,), [row, vec, row], (row, acc),
                 (_sds((L, D), F32), _sds((8, D), F32)))(x, nw, target)


def resid_gate_bwd(dxo, y, gate, coef, name):
    L, D = dxo.shape
    tm = _tile(L, 512)

    def body(dxo_ref, y_ref, g_ref, dy_ref, acc_ref):
        @pl.when(pl.program_id(0) == 0)
        def _():
            acc_ref[...] = jnp.zeros_like(acc_ref)

        d = dxo_ref[...]
        dy = coef * g_ref[...] * d
        dy_ref[...] = dy.astype(BF)
        acc_ref[0:1, :] += coef * jnp.sum(d * y_ref[...].astype(F32), axis=0, keepdims=True)
        acc_ref[1:2, :] += jnp.sum(dy, axis=0, keepdims=True)

    row = pl.BlockSpec((tm, D), lambda i: (i, 0))
    vec = pl.BlockSpec((1, D), lambda i: (0, 0))
    acc = pl.BlockSpec((8, D), lambda i: (0, 0))
    return _call(body, name, (L // tm,), [row, row, vec], (row, acc),
                 (_sds((L, D), BF), _sds((8, D), F32)))(dxo, y, gate)


def ffn_up(h, wt):
    L, D = h.shape
    F = wt.shape[0] // 2
    tm, tn = _tile(L, 2048), _tile(F, 256)
    nj = F // tn

    def body(h_ref, wg_ref, wu_ref, g_ref, u_ref, a_ref):
        hv = h_ref[...]
        g = _dot_nt(hv, wg_ref[...])
        u = _dot_nt(hv, wu_ref[...])
        g_ref[...] = g.astype(BF)
        u_ref[...] = u.astype(BF)
        a_ref[...] = (g * _sigmoid(g) * u).astype(BF)

    o = pl.BlockSpec((tm, tn), lambda i, n: (i, n))
    return _call(body, "ffn_up", (L // tm, nj),
                 [pl.BlockSpec((tm, D), lambda i, n: (i, 0)),
                  pl.BlockSpec((tn, D), lambda i, n: (n, 0)),
                  pl.BlockSpec((tn, D), lambda i, n: (n + nj, 0))],
                 (o, o, o), tuple(_sds((L, F), BF) for _ in range(3)))(h, wt, wt)


def ffn_down_dgrad(dy, wd, g, u):
    L, D = dy.shape
    F = wd.shape[0]
    tm, tn = _tile(L, 2048), _tile(F, 256)

    def body(dy_ref, w_ref, g_ref, u_ref, dg_ref, du_ref):
        da = _dot_nt(dy_ref[...], w_ref[...])
        gv = g_ref[...].astype(F32)
        uv = u_ref[...].astype(F32)
        s = _sigmoid(gv)
        dg_ref[...] = (da * uv * s * (1.0 + gv * (1.0 - s))).astype(BF)
        du_ref[...] = (da * gv * s).astype(BF)

    o = pl.BlockSpec((tm, tn), lambda i, n: (i, n))
    return _call(body, "ffn_down_dgrad", (L // tm, F // tn),
                 [pl.BlockSpec((tm, D), lambda i, n: (i, 0)), pl.BlockSpec((tn, D), lambda i, n: (n, 0)), o, o],
                 (o, o), (_sds((L, F), BF), _sds((L, F), BF)))(dy, wd, g, u)


def ffn_up_wgrad(dg, du, h):
    L, F = dg.shape
    D = h.shape[1]
    tm = _tile(F, 256)
    nblk = F // tm

    def half(d, off, prev):
        def body(d_ref, h_ref, *rest):
            rest[-1][...] = _dot_tn(d_ref[...], h_ref[...]).astype(BF)

        in_specs = [pl.BlockSpec((L, tm), lambda i: (0, i)), pl.BlockSpec((L, D), lambda i: (0, 0))]
        args = [d, h]
        aliases = {}
        if prev is not None:
            in_specs.append(pl.BlockSpec(memory_space=pl.ANY))
            args.append(prev)
            aliases = {2: 0}
        return pl.pallas_call(
            body, name="ffn_up_wgrad", grid=(nblk,), in_specs=in_specs,
            out_specs=pl.BlockSpec((tm, D), lambda i: (i + off * nblk, 0)),
            out_shape=_sds((2 * F, D), BF), input_output_aliases=aliases,
            compiler_params=pltpu.CompilerParams(vmem_limit_bytes=VMEM_LIMIT))(*args)

    return half(du, 1, half(dg, 0, None))


def ffn_up_dgrad(dg, du, wt):
    L, F = dg.shape
    D = wt.shape[1]
    tm, tn = _tile(L, 1024), _tile(D, 512)

    def body(dg_ref, du_ref, wg_ref, wu_ref, o_ref):
        o_ref[...] = _dot_nn(dg_ref[...], wg_ref[...]) + _dot_nn(du_ref[...], wu_ref[...])

    a = pl.BlockSpec((tm, F), lambda i, n: (i, 0))
    return _call(body, "ffn_up_dgrad", (L // tm, D // tn),
                 [a, a, pl.BlockSpec((F, tn), lambda i, n: (0, n)), pl.BlockSpec((F, tn), lambda i, n: (1, n))],
                 pl.BlockSpec((tm, tn), lambda i, n: (i, n)), _sds((L, D), F32))(dg, du, wt, wt)


def _shift_rows(cur, other, k, down):
    n = cur.shape[0]
    rows = lax.broadcasted_iota(jnp.int32, cur.shape, 0)
    if down:
        return jnp.where(rows < k, pltpu.roll(other, k, 0), pltpu.roll(cur, k, 0))
    return jnp.where(rows >= n - k, pltpu.roll(other, n - k, 0), pltpu.roll(cur, n - k, 0))


def _conv_pre(cur, prev, w_ref, b_ref):
    s = cur * w_ref[CONV_WIDTH - 1:CONV_WIDTH, :] + b_ref[...]
    for k in range(1, CONV_WIDTH):
        s = s + _shift_rows(cur, prev, k, True) * w_ref[CONV_WIDTH - 1 - k:CONV_WIDTH - k, :]
    return s


_XBC_COL0 = D_INNER // 512


def conv_fwd(zx, w, b):
    L = zx.shape[0]
    tm, tc = _tile(L, 256), 512

    def body(cur_ref, prev_ref, w_ref, b_ref, o_ref):
        cur = cur_ref[...]
        prev = jnp.where(pl.program_id(1) > 0, prev_ref[...], 0.0)
        s = _conv_pre(cur, prev, w_ref, b_ref)
        o_ref[...] = s * _sigmoid(s)

    return _call(body, "conv_fwd", (CONV_DIM // tc, L // tm),
                 [pl.BlockSpec((tm, tc), lambda j, i: (i, _XBC_COL0 + j)),
                  pl.BlockSpec((tm, tc), lambda j, i: (jnp.maximum(i - 1, 0), _XBC_COL0 + j)),
                  pl.BlockSpec((CONV_WIDTH, tc), lambda j, i: (0, j)),
                  pl.BlockSpec((1, tc), lambda j, i: (0, j))],
                 pl.BlockSpec((tm, tc), lambda j, i: (i, j)), _sds((L, CONV_DIM), F32))(zx, zx, w, b)


def conv_bwd_act(dxc, zx, w, b):
    L = zx.shape[0]
    tm, tc = _tile(L, 256), 512

    def body(d_ref, cur_ref, prev_ref, w_ref, b_ref, o_ref):
        cur = cur_ref[...]
        prev = jnp.where(pl.program_id(1) > 0, prev_ref[...], 0.0)
        s = _conv_pre(cur, prev, w_ref, b_ref)
        sg = _sigmoid(s)
        o_ref[...] = d_ref[...] * sg * (1.0 + s * (1.0 - sg))

    return _call(body, "conv_bwd_act", (CONV_DIM // tc, L // tm),
                 [pl.BlockSpec((tm, tc), lambda j, i: (i, j)),
                  pl.BlockSpec((tm, tc), lambda j, i: (i, _XBC_COL0 + j)),
                  pl.BlockSpec((tm, tc), lambda j, i: (jnp.maximum(i - 1, 0), _XBC_COL0 + j)),
                  pl.BlockSpec((CONV_WIDTH, tc), lambda j, i: (0, j)),
                  pl.BlockSpec((1, tc), lambda j, i: (0, j))],
                 pl.BlockSpec((tm, tc), lambda j, i: (i, j)), _sds((L, CONV_DIM), F32))(dxc, zx, zx, w, b)


def conv_bwd(ds, zx, w):
    L = zx.shape[0]
    tm, tc = _tile(L, 256), 512
    nblk = L // tm

    def body(ds_ref, dsn_ref, cur_ref, prev_ref, w_ref, du_ref, acc_ref):
        i = pl.program_id(1)

        @pl.when(i == 0)
        def _():
            acc_ref[...] = jnp.zeros_like(acc_ref)

        ds_c = ds_ref[...]
        ds_n = jnp.where(i < nblk - 1, dsn_ref[...], 0.0)
        cur = cur_ref[...]
        prev = jnp.where(i > 0, prev_ref[...], 0.0)
        du = ds_c * w_ref[CONV_WIDTH - 1:CONV_WIDTH, :]
        acc_ref[CONV_WIDTH - 1:CONV_WIDTH, :] += jnp.sum(ds_c * cur, axis=0, keepdims=True)
        for k in range(1, CONV_WIDTH):
            du = du + _shift_rows(ds_c, ds_n, k, False) * w_ref[CONV_WIDTH - 1 - k:CONV_WIDTH - k, :]
            acc_ref[CONV_WIDTH - 1 - k:CONV_WIDTH - k, :] += jnp.sum(
                ds_c * _shift_rows(cur, prev, k, True), axis=0, keepdims=True)
        acc_ref[CONV_WIDTH:CONV_WIDTH + 1, :] += jnp.sum(ds_c, axis=0, keepdims=True)
        du_ref[...] = du.astype(BF)

    return _call(body, "conv_bwd", (CONV_DIM // tc, nblk),
                 [pl.BlockSpec((tm, tc), lambda j, i: (i, j)),
                  pl.BlockSpec((tm, tc), lambda j, i: (jnp.minimum(i + 1, nblk - 1), j)),
                  pl.BlockSpec((tm, tc), lambda j, i: (i, _XBC_COL0 + j)),
                  pl.BlockSpec((tm, tc), lambda j, i: (jnp.maximum(i - 1, 0), _XBC_COL0 + j)),
                  pl.BlockSpec((CONV_WIDTH, tc), lambda j, i: (0, j))],
                 (pl.BlockSpec((tm, tc), lambda j, i: (i, j)), pl.BlockSpec((8, tc), lambda j, i: (0, j))),
                 (_sds((L, CONV_DIM), BF), _sds((8, CONV_DIM), F32)))(ds, ds, zx, zx, w)


_DT_COL = (D_INNER + CONV_DIM) // 128


def dt_prep(zx, bias_pad, alog_pad):
    L = zx.shape[0]

    def body(raw_ref, b_ref, al_ref, dt_ref, acs_ref):
        v = raw_ref[...] + b_ref[...]
        dt = jnp.maximum(v, 0.0) + jnp.log(1.0 + jnp.exp(-jnp.abs(v)))
        dt_ref[...] = dt
        acs = dt * (-jnp.exp(al_ref[...]))
        rows = lax.broadcasted_iota(jnp.int32, acs.shape, 0)
        s = 1
        while s < CHUNK:
            acs = acs + jnp.where(rows >= s, pltpu.roll(acs, s, 0), 0.0)
            s *= 2
        acs_ref[...] = acs

    blk = pl.BlockSpec((CHUNK, 128), lambda i: (i, 0))
    vec = pl.BlockSpec((1, 128), lambda i: (0, 0))
    return _call(body, "dt_prep", (L // CHUNK,),
                 [pl.BlockSpec((CHUNK, 128), lambda i: (i, _DT_COL)), vec, vec], (blk, blk),
                 (_sds((L, 128), F32), _sds((L, 128), F32)))(zx, bias_pad, alog_pad)


def dt_bwd(ddt, da, dt, zx, bias_pad, alog_pad):
    L = zx.shape[0]
    tm = _tile(L, 512)

    def body(ddt_ref, da_ref, dt_ref, raw_ref, b_ref, al_ref, o_ref, acc_ref):
        @pl.when(pl.program_id(0) == 0)
        def _():
            acc_ref[...] = jnp.zeros_like(acc_ref)

        A = -jnp.exp(al_ref[...])
        dav = da_ref[...]
        dd = ddt_ref[...] + dav * A
        draw = dd * _sigmoid(raw_ref[...] + b_ref[...])
        o_ref[...] = draw.astype(BF)
        acc_ref[0:1, :] += jnp.sum(draw, axis=0, keepdims=True)
        acc_ref[1:2, :] += jnp.sum(dav * dt_ref[...], axis=0, keepdims=True) * A

    blk = pl.BlockSpec((tm, 128), lambda i: (i, 0))
    vec = pl.BlockSpec((1, 128), lambda i: (0, 0))
    return _call(body, "dt_bwd", (L // tm,),
                 [blk, blk, blk, pl.BlockSpec((tm, 128), lambda i: (i, _DT_COL)), vec, vec],
                 (blk, pl.BlockSpec((8, 128), lambda i: (0, 0))),
                 (_sds((L, 128), BF), _sds((8, 128), F32)))(ddt, da, dt, zx, bias_pad, alog_pad)


_HPG = SSM_HEADS // SSM_GROUPS
_GW = _HPG * SSM_HEADDIM
_B_COL0 = D_INNER // SSM_STATE
_C_COL0 = (D_INNER + SSM_GROUPS * SSM_STATE) // SSM_STATE


def _ssd_head(x, dtc, ac, ar, r, causal):
    xh = x[:, SSM_HEADDIM * r:SSM_HEADDIM * (r + 1)]
    acol = ac[:, r:r + 1]
    arow = ar[r:r + 1, :]
    alast = ar[r:r + 1, CHUNK - 1:CHUNK]
    lm = jnp.exp(jnp.where(causal, acol - arow, NEG))
    return xh, xh * dtc[:, r:r + 1], acol, alast, lm


def ssd_fwd(xc, dt_g, acs_g, acsT_g, d_exp):
    L = xc.shape[0]
    nc = L // CHUNK

    def body(x_ref, b_ref, c_ref, dt_ref, ac_ref, ar_ref, d_ref, y_ref, pst_ref, st_ref):
        @pl.when(pl.program_id(1) == 0)
        def _():
            st_ref[...] = jnp.zeros_like(st_ref)

        x, Bm, Cm = x_ref[...], b_ref[...], c_ref[...]
        dtc, ac, ar = dt_ref[...], ac_ref[...], ar_ref[...]
        causal = lax.broadcasted_iota(jnp.int32, (CHUNK, CHUNK), 0) >= lax.broadcasted_iota(jnp.int32, (CHUNK, CHUNK), 1)
        CB = _dot_nt(Cm, Bm)
        for r in range(_HPG):
            xh, xd, acol, alast, lm = _ssd_head(x, dtc, ac, ar, r, causal)
            P = st_ref[r]
            y = _dot_nn(CB * lm, xd) + jnp.exp(acol) * _dot_nt(Cm, P)
            y_ref[:, SSM_HEADDIM * r:SSM_HEADDIM * (r + 1)] = y + d_ref[:, SSM_HEADDIM * r:SSM_HEADDIM * (r + 1)] * xh
            pst_ref[r] = P
            st_ref[r] = P * jnp.exp(alast) + _dot_tn(xd * jnp.exp(alast - acol), Bm)

    return _call(
        body, "ssd_fwd", (SSM_GROUPS, nc),
        [pl.BlockSpec((CHUNK, _GW), lambda g, c: (c, g)),
         pl.BlockSpec((CHUNK, SSM_STATE), lambda g, c: (c, _B_COL0 + g)),
         pl.BlockSpec((CHUNK, SSM_STATE), lambda g, c: (c, _C_COL0 + g)),
         pl.BlockSpec((None, CHUNK, _HPG), lambda g, c: (g, c, 0)),
         pl.BlockSpec((None, CHUNK, _HPG), lambda g, c: (g, c, 0)),
         pl.BlockSpec((None, _HPG, CHUNK), lambda g, c: (g, 0, c)),
         pl.BlockSpec((None, 1, _GW), lambda g, c: (g, 0, 0))],
        (pl.BlockSpec((CHUNK, _GW), lambda g, c: (c, g)),
         pl.BlockSpec((None, None, _HPG, SSM_HEADDIM, SSM_STATE), lambda g, c: (c, g, 0, 0, 0))),
        (_sds((L, D_INNER), F32), _sds((nc, SSM_GROUPS, _HPG, SSM_HEADDIM, SSM_STATE), F32)),
        scratch=[pltpu.VMEM((_HPG, SSM_HEADDIM, SSM_STATE), F32)],
    )(xc, xc, xc, dt_g, acs_g, acsT_g, d_exp)


def ssd_bwd(dy, xc, dt_g, acs_g, acsT_g, pst, d_exp):
    L = xc.shape[0]
    nc = L // CHUNK

    def body(dy_ref, x_ref, b_ref, c_ref, dt_ref, ac_ref, ar_ref, pst_ref, d_ref,
             dx_ref, db_ref, dc_ref, ddt_ref, da_ref, dd_ref, dp_ref):
        @pl.when(pl.program_id(1) == 0)
        def _():
            dp_ref[...] = jnp.zeros_like(dp_ref)
            dd_ref[...] = jnp.zeros_like(dd_ref)

        dyv, x, Bm, Cm = dy_ref[...], x_ref[...], b_ref[...], c_ref[...]
        dtc, ac, ar = dt_ref[...], ac_ref[...], ar_ref[...]
        ri = lax.broadcasted_iota(jnp.int32, (CHUNK, CHUNK), 0)
        ci = lax.broadcasted_iota(jnp.int32, (CHUNK, CHUNK), 1)
        causal = ri >= ci
        lane4 = lax.broadcasted_iota(jnp.int32, (CHUNK, _HPG), 1)
        CB = _dot_nt(Cm, Bm)
        dB = jnp.zeros((CHUNK, SSM_STATE), F32)
        dC = jnp.zeros((CHUNK, SSM_STATE), F32)
        dCB = jnp.zeros((CHUNK, CHUNK), F32)
        ddt_blk = jnp.zeros((CHUNK, _HPG), F32)
        da_blk = jnp.zeros((CHUNK, _HPG), F32)
        for r in range(_HPG):
            sl = slice(SSM_HEADDIM * r, SSM_HEADDIM * (r + 1))
            xh, xd, acol, alast, lm = _ssd_head(x, dtc, ac, ar, r, causal)
            dyh = dyv[:, sl]
            P = pst_ref[r]
            dPn = dp_ref[r]
            eA = jnp.exp(acol)
            cd = jnp.exp(alast)
            dte = jnp.exp(alast - acol)
            G = CB * lm
            Z = _dot_nt(Cm, P)
            dZ = eA * dyh
            dC = dC + _dot_nn(dZ, P)
            dp_ref[r] = dPn * cd + _dot_tn(dZ, Cm)
            dA_col = jnp.sum(dZ * Z, axis=1, keepdims=True)
            BdS = _dot_nt(Bm, dPn)
            dxd = dte * BdS
            dB = dB + dte * _dot_nn(xd, dPn)
            t = jnp.sum(xd * BdS, axis=1, keepdims=True) * dte
            dA_col = dA_col - t
            dA_last = jnp.sum(t, axis=0, keepdims=True) + jnp.sum(
                jnp.sum(dPn * P, axis=1, keepdims=True), axis=0, keepdims=True) * cd
            dG = _dot_nt(dyh, xd)
            dxd = dxd + _dot_tn(G, dyh)
            dCB = dCB + dG * lm
            W = dG * G
            dA_col = dA_col + jnp.sum(W, axis=1, keepdims=True)
            dA_row = jnp.sum(jnp.where(ri == ci, dA_col, 0.0), axis=0, keepdims=True) - jnp.sum(W, axis=0, keepdims=True)
            da_col = jnp.sum(jnp.where(ci >= ri, dA_row, 0.0), axis=1, keepdims=True) + dA_last
            da_blk = jnp.where(lane4 == r, da_col, da_blk)
            ddt_blk = jnp.where(lane4 == r, jnp.sum(dxd * xh, axis=1, keepdims=True), ddt_blk)
            dx_ref[:, sl] = dxd * dtc[:, r:r + 1] + d_ref[:, sl] * dyh
        dc_ref[...] = dC + _dot_nn(dCB, Bm)
        db_ref[...] = dB + _dot_tn(dCB, Cm)
        ddt_ref[...] = ddt_blk
        da_ref[...] = da_blk
        dd_ref[...] += jnp.sum(dyv * x, axis=0, keepdims=True)

    rc = lambda g, c: (nc - 1 - c, g)
    small = pl.BlockSpec((None, CHUNK, _HPG), lambda g, c: (g, nc - 1 - c, 0))
    return _call(
        body, "ssd_bwd", (SSM_GROUPS, nc),
        [pl.BlockSpec((CHUNK, _GW), rc),
         pl.BlockSpec((CHUNK, _GW), rc),
         pl.BlockSpec((CHUNK, SSM_STATE), lambda g, c: (nc - 1 - c, _B_COL0 + g)),
         pl.BlockSpec((CHUNK, SSM_STATE), lambda g, c: (nc - 1 - c, _C_COL0 + g)),
         small, small,
         pl.BlockSpec((None, _HPG, CHUNK), lambda g, c: (g, 0, nc - 1 - c)),
         pl.BlockSpec((None, None, _HPG, SSM_HEADDIM, SSM_STATE), lambda g, c: (nc - 1 - c, g, 0, 0, 0)),
         pl.BlockSpec((None, 1, _GW), lambda g, c: (g, 0, 0))],
        (pl.BlockSpec((CHUNK, _GW), rc),
         pl.BlockSpec((CHUNK, SSM_STATE), rc),
         pl.BlockSpec((CHUNK, SSM_STATE), rc),
         small, small,
         pl.BlockSpec((None, 1, _GW), lambda g, c: (g, 0, 0))),
        (_sds((L, D_INNER), F32), _sds((L, SSM_GROUPS * SSM_STATE), F32), _sds((L, SSM_GROUPS * SSM_STATE), F32),
         _sds((SSM_GROUPS, L, _HPG), F32), _sds((SSM_GROUPS, L, _HPG), F32), _sds((SSM_GROUPS, 1, _GW), F32)),
        scratch=[pltpu.VMEM((_HPG, SSM_HEADDIM, SSM_STATE), F32)],
    )(dy, xc, xc, xc, dt_g, acs_g, acsT_g, pst, d_exp)


_NGW = D_INNER // SSM_GROUPS


def gate_norm(y, zx, nw):
    L = y.shape[0]
    tm = _tile(L, 256)

    def body(y_ref, z_ref, nw_ref, o_ref):
        for g in range(SSM_GROUPS):
            sl = slice(_NGW * g, _NGW * (g + 1))
            z = z_ref[:, sl]
            y2 = y_ref[:, sl] * (z * _sigmoid(z))
            r = lax.rsqrt(jnp.mean(y2 * y2, axis=-1, keepdims=True) + EPS)
            o_ref[:, sl] = (y2 * r * nw_ref[:, sl]).astype(BF)

    row = pl.BlockSpec((tm, D_INNER), lambda i: (i, 0))
    return _call(body, "gate_norm", (L // tm,), [row, row, pl.BlockSpec((1, D_INNER), lambda i: (0, 0))],
                 row, _sds((L, D_INNER), BF))(y, zx, nw)


def gate_norm_bwd(dyn, y, zx, nw):
    L = y.shape[0]
    tm = _tile(L, 256)

    def body(d_ref, y_ref, z_ref, nw_ref, dy_ref, dz_ref, acc_ref):
        @pl.when(pl.program_id(0) == 0)
        def _():
            acc_ref[...] = jnp.zeros_like(acc_ref)

        for g in range(SSM_GROUPS):
            sl = slice(_NGW * g, _NGW * (g + 1))
            z = z_ref[:, sl]
            yv = y_ref[:, sl]
            sg = _sigmoid(z)
            sz = z * sg
            y2 = yv * sz
            r = lax.rsqrt(jnp.mean(y2 * y2, axis=-1, keepdims=True) + EPS)
            yh = y2 * r
            d = d_ref[:, sl]
            dn = d * nw_ref[:, sl]
            dy2 = r * (dn - yh * jnp.mean(dn * yh, axis=-1, keepdims=True))
            dy_ref[:, sl] = dy2 * sz
            dz_ref[:, sl] = (dy2 * yv * sg * (1.0 + z * (1.0 - sg))).astype(BF)
            acc_ref[0:1, sl] += jnp.sum(d * yh, axis=0, keepdims=True)

    row = pl.BlockSpec((tm, D_INNER), lambda i: (i, 0))
    return _call(body, "gate_norm_bwd", (L // tm,), [row, row, row, pl.BlockSpec((1, D_INNER), lambda i: (0, 0))],
                 (row, row, pl.BlockSpec((8, D_INNER), lambda i: (0, 0))),
                 (_sds((L, D_INNER), F32), _sds((L, D_INNER), BF), _sds((8, D_INNER), F32)))(dyn, y, zx, nw)


_SCALE = HEAD_DIM ** -0.5
_REP = ATT_HEADS // KV_HEADS
_V_OFF = KV_HEADS * HEAD_DIM


def _stack_heads(ref, k):
    return jnp.concatenate([ref[:, HEAD_DIM * (k * _REP + r):HEAD_DIM * (k * _REP + r + 1)] for r in range(_REP)],
                           axis=0)


def _stack_sinks(s_ref, k):
    return jnp.concatenate([jnp.broadcast_to(s_ref[:, k * _REP + r:k * _REP + r + 1], (WINDOW, 1))
                            for r in range(_REP)], axis=0)


def _attn_probs(q4, kp, kc, sink, first):
    shape = (_REP * WINDOW, WINDOW)
    rows = jnp.bitwise_and(lax.broadcasted_iota(jnp.int32, shape, 0), WINDOW - 1)
    cols = lax.broadcasted_iota(jnp.int32, shape, 1)
    sp = jnp.where(jnp.logical_and(cols > rows, jnp.logical_not(first)), _dot_nt(q4, kp) * _SCALE, NEG)
    sc = jnp.where(cols <= rows, _dot_nt(q4, kc) * _SCALE, NEG)
    m = jnp.maximum(jnp.maximum(jnp.max(sp, axis=1, keepdims=True), jnp.max(sc, axis=1, keepdims=True)), sink)
    pp = jnp.exp(sp - m)
    pc = jnp.exp(sc - m)
    ps = jnp.exp(sink - m)
    inv = 1.0 / (jnp.sum(pp, axis=1, keepdims=True) + jnp.sum(pc, axis=1, keepdims=True) + ps)
    return pp * inv, pc * inv, ps * inv


def attn_fwd(q, kv, sinks_pad):
    L = q.shape[0]
    nb = L // WINDOW

    def body(q_ref, kc_ref, kp_ref, s_ref, o_ref):
        first = pl.program_id(0) == 0
        for k in range(KV_HEADS):
            ks = slice(HEAD_DIM * k, HEAD_DIM * (k + 1))
            vs = slice(_V_OFF + HEAD_DIM * k, _V_OFF + HEAD_DIM * (k + 1))
            pp, pc, _ = _attn_probs(_stack_heads(q_ref, k), kp_ref[:, ks], kc_ref[:, ks], _stack_sinks(s_ref, k), first)
            o4 = _dot_nn(pp, kp_ref[:, vs]) + _dot_nn(pc, kc_ref[:, vs])
            for r in range(_REP):
                h = k * _REP + r
                o_ref[:, HEAD_DIM * h:HEAD_DIM * (h + 1)] = o4[WINDOW * r:WINDOW * (r + 1)].astype(BF)

    qspec = pl.BlockSpec((WINDOW, D_MODEL), lambda i: (i, 0))
    return _call(body, "attn_fwd", (nb,),
                 [qspec, pl.BlockSpec((WINDOW, KV_DIM), lambda i: (i, 0)),
                  pl.BlockSpec((WINDOW, KV_DIM), lambda i: (jnp.maximum(i - 1, 0), 0)),
                  pl.BlockSpec((1, 128), lambda i: (0, 0))],
                 qspec, _sds((L, D_MODEL), BF))(q, kv, kv, sinks_pad)


def attn_bwd(q, kv, do, sinks_pad):
    L = q.shape[0]
    nb = L // WINDOW

    def body(q_ref, kc_ref, kp_ref, do_ref, s_ref, dq_ref, dc_ref, dp_ref, acc_ref):
        first = pl.program_id(0) == 0

        @pl.when(first)
        def _():
            acc_ref[...] = jnp.zeros_like(acc_ref)

        lane = lax.broadcasted_iota(jnp.int32, (1, 128), 1)
        dsink = jnp.zeros((1, 128), F32)
        for k in range(KV_HEADS):
            ks = slice(HEAD_DIM * k, HEAD_DIM * (k + 1))
            vs = slice(_V_OFF + HEAD_DIM * k, _V_OFF + HEAD_DIM * (k + 1))
            kp, kc, vp, vc = kp_ref[:, ks], kc_ref[:, ks], kp_ref[:, vs], kc_ref[:, vs]
            q4 = _stack_heads(q_ref, k)
            do4 = _stack_heads(do_ref, k)
            pp, pc, ps = _attn_probs(q4, kp, kc, _stack_sinks(s_ref, k), first)
            dpp = _dot_nt(do4, vp)
            dpc = _dot_nt(do4, vc)
            delta = jnp.sum(pp * dpp, axis=1, keepdims=True) + jnp.sum(pc * dpc, axis=1, keepdims=True)
            dsp = pp * (dpp - delta) * _SCALE
            dsc = pc * (dpc - delta) * _SCALE
            dq4 = _dot_nn(dsp, kp) + _dot_nn(dsc, kc)
            psd = ps * delta
            for r in range(_REP):
                h = k * _REP + r
                rs = slice(WINDOW * r, WINDOW * (r + 1))
                dq_ref[:, HEAD_DIM * h:HEAD_DIM * (h + 1)] = dq4[rs]
                dsink = dsink + jnp.where(lane == h, -jnp.sum(psd[rs], axis=0, keepdims=True), 0.0)
            dp_ref[:, ks] = _dot_tn(dsp, q4)
            dc_ref[:, ks] = _dot_tn(dsc, q4)
            dp_ref[:, vs] = _dot_tn(pp, do4)
            dc_ref[:, vs] = _dot_tn(pc, do4)
        acc_ref[0:1, :] += jnp.sum(dq_ref[...], axis=0, keepdims=True)
        acc_ref[1:2, 0:128] += dsink

    qspec = pl.BlockSpec((WINDOW, D_MODEL), lambda i: (i, 0))
    kspec = pl.BlockSpec((WINDOW, KV_DIM), lambda i: (i, 0))
    return _call(body, "attn_bwd", (nb,),
                 [qspec, kspec, pl.BlockSpec((WINDOW, KV_DIM), lambda i: (jnp.maximum(i - 1, 0), 0)), qspec,
                  pl.BlockSpec((1, 128), lambda i: (0, 0))],
                 (qspec, kspec, kspec, pl.BlockSpec((8, D_MODEL), lambda i: (0, 0))),
                 (_sds((L, D_MODEL), F32), _sds((L, KV_DIM), F32), _sds((L, KV_DIM), F32), _sds((8, D_MODEL), F32)),
                 )(q, kv, kv, do, sinks_pad)


def kv_grad_combine(parts):
    L = parts[0][0].shape[0]
    nb = L // WINDOW
    n = len(parts)

    def body(*refs):
        i = pl.program_id(0)
        o_ref, acc_ref = refs[2 * n], refs[2 * n + 1]

        @pl.when(i == 0)
        def _():
            acc_ref[...] = jnp.zeros_like(acc_ref)

        tot = refs[0][...]
        nxt = refs[1][...]
        for a in range(1, n):
            tot = tot + refs[2 * a][...]
            nxt = nxt + refs[2 * a + 1][...]
        tot = tot + jnp.where(i < nb - 1, nxt, 0.0)
        o_ref[...] = tot
        acc_ref[0:1, :] += jnp.sum(tot, axis=0, keepdims=True)

    cur = pl.BlockSpec((WINDOW, KV_DIM), lambda i: (i, 0))
    nxt = pl.BlockSpec((WINDOW, KV_DIM), lambda i: (jnp.minimum(i + 1, nb - 1), 0))
    args = [t for p in parts for t in p]
    return _call(body, "kv_grad_combine", (nb,), [cur, nxt] * n,
                 (cur, pl.BlockSpec((8, KV_DIM), lambda i: (0, 0))),
                 (_sds((L, KV_DIM), F32), _sds((8, KV_DIM), F32)))(*args)


def mod_fwd(c_all, w, b, name):
    n, _, C = w.shape

    def body(c_ref, w_ref, b_ref, o_ref, ca_ref):
        cv = c_ref[...]
        ca = cv * _sigmoid(cv)
        ca_ref[...] = ca
        o_ref[...] = _dot(ca, w_ref[...], ((1,), (0,))) + b_ref[...]

    return _call(body, name, (n,),
                 [pl.BlockSpec((N_DEV, D_MODEL), lambda i: (0, 0)),
                  pl.BlockSpec((None, D_MODEL, C), lambda i: (i, 0, 0)),
                  pl.BlockSpec((None, 1, C), lambda i: (i, 0, 0))],
                 (pl.BlockSpec((None, N_DEV, C), lambda i: (i, 0, 0)), pl.BlockSpec((N_DEV, D_MODEL), lambda i: (0, 0))),
                 (_sds((n, N_DEV, C), F32), _sds((N_DEV, D_MODEL), F32)))(c_all, w, b)


def mod_wgrad(c_act_t, dmod, name):
    n, _, C = dmod.shape
    tr = 256

    def body(ct_ref, d_ref, o_ref):
        acc = ct_ref[:, 0:1] * d_ref[0:1, :]
        for bidx in range(1, N_DEV):
            acc = acc + ct_ref[:, bidx:bidx + 1] * d_ref[bidx:bidx + 1, :]
        o_ref[...] = acc

    return _call(body, name, (n, D_MODEL // tr),
                 [pl.BlockSpec((tr, N_DEV), lambda i, j: (j, 0)),
                  pl.BlockSpec((None, N_DEV, C), lambda i, j: (i, 0, 0))],
                 pl.BlockSpec((None, tr, C), lambda i, j: (i, j, 0)), _sds((n, D_MODEL, C), F32))(c_act_t, dmod)


def _my_pos():
    return lax.axis_index("x"), lax.axis_index("y"), lax.axis_index("c")


def small_all_gather(v):
    m_per, n = v.shape

    def body(x_ref, out_ref, send_sems, recv_sems, local_sem):
        x, y, c = _my_pos()
        me, sibling = (x, y, c), (x, y, 1 - c)
        chips = [(1 - x, y), (x, 1 - y), (1 - x, 1 - y)]

        def rows(px, py, pc):
            return out_ref.at[pl.ds((4 * px + 2 * py + pc) * m_per, m_per), :]

        def copy(k, block, to, src=None):
            return pltpu.make_async_remote_copy(
                src_ref=rows(*block) if src is None else src, dst_ref=rows(*block),
                send_sem=send_sems.at[k], recv_sem=recv_sems.at[k], device_id=to, device_id_type=MESH)

        mine = pltpu.make_async_copy(x_ref, rows(*me), local_sem)
        mine.start()
        first = [copy(0, me, sibling, src=x_ref)]
        first += [copy(1 + j, me, (*chip, c), src=x_ref) for j, chip in enumerate(chips)]
        for cp in first:
            cp.start()
        passed = [copy(4 + j, (*chip, c), sibling) for j, chip in enumerate(chips)]
        for j, chip in enumerate(chips):
            copy(1 + j, (*chip, c), me).wait_recv()
            passed[j].start()
        copy(0, sibling, me).wait_recv()
        for j, chip in enumerate(chips):
            copy(4 + j, (*chip, 1 - c), me).wait_recv()
        for cp in first + passed:
            cp.wait_send()
        mine.wait()

    return pl.pallas_call(
        body, name="small_all_gather",
        out_shape=_sds((N_DEV * m_per, n), v.dtype),
        in_specs=[pl.BlockSpec(memory_space=pltpu.VMEM)],
        out_specs=pl.BlockSpec(memory_space=pltpu.VMEM),
        scratch_shapes=[pltpu.SemaphoreType.DMA((7,)), pltpu.SemaphoreType.DMA((7,)), pltpu.SemaphoreType.DMA],
        compiler_params=pltpu.CompilerParams(vmem_limit_bytes=VMEM_LIMIT),
    )(v)


def big_all_gather(arrs):
    n = len(arrs)

    def body(*refs):
        ins, outs = refs[:n], refs[n:2 * n]
        send_sems, recv_sems, local_sems = refs[2 * n], refs[2 * n + 1], refs[2 * n + 2]
        x, y, c = _my_pos()
        me, sibling = (x, y, c), (x, y, 1 - c)
        chips = [(1 - x, y), (x, 1 - y), (1 - x, 1 - y)]

        def slot(a, px, py, pc):
            return outs[a].at[4 * px + 2 * py + pc]

        def copy(a, k, block, to, src=None):
            return pltpu.make_async_remote_copy(
                src_ref=slot(a, *block) if src is None else src, dst_ref=slot(a, *block),
                send_sem=send_sems.at[7 * a + k], recv_sem=recv_sems.at[7 * a + k], device_id=to, device_id_type=MESH)

        mine = [pltpu.make_async_copy(ins[a], slot(a, *me), local_sems.at[a]) for a in range(n)]
        for cp in mine:
            cp.start()
        first = []
        for a in range(n):
            first.append(copy(a, 0, me, sibling, src=ins[a]))
            first += [copy(a, 1 + j, me, (*chip, c), src=ins[a]) for j, chip in enumerate(chips)]
        for cp in first:
            cp.start()
        passed = []
        for a in range(n):
            for j, chip in enumerate(chips):
                copy(a, 1 + j, (*chip, c), me).wait_recv()
                fwd = copy(a, 4 + j, (*chip, c), sibling)
                fwd.start()
                passed.append(fwd)
        for a in range(n):
            copy(a, 0, sibling, me).wait_recv()
            for j, chip in enumerate(chips):
                copy(a, 4 + j, (*chip, 1 - c), me).wait_recv()
        for cp in first + passed:
            cp.wait_send()
        for cp in mine:
            cp.wait()

    hbm = pl.BlockSpec(memory_space=pltpu.HBM)
    return pl.pallas_call(
        body, name="big_all_gather",
        out_shape=[_sds((N_DEV,) + a.shape, a.dtype) for a in arrs],
        in_specs=[hbm] * n, out_specs=[hbm] * n,
        scratch_shapes=[pltpu.SemaphoreType.DMA((7 * n,)), pltpu.SemaphoreType.DMA((7 * n,)),
                        pltpu.SemaphoreType.DMA((n,))],
    )(*arrs)


_FLIPS =[(fx, fy, fc) for fx in (0, 1) for fy in (0, 1) for fc in (0, 1)][1:]
_HBM = pl.BlockSpec(memory_space=pltpu.HBM)
_SEM = pl.BlockSpec(memory_space=pltpu.SEMAPHORE)
_EFFECT = pltpu.SideEffectType.DATAFLOW_SIDE_EFFECTING


def _flip(x, y, c, f):
    return (1 - x if f[0] else x), (1 - y if f[1] else y), (1 - c if f[2] else c)


def _xfer_copies(srcs, lands, send_sems, recv_sems, scatter):
    x, y, c = _my_pos()
    me = 4 * x + 2 * y + c
    copies = []
    for a in range(len(srcs)):
        for k, f in enumerate(_FLIPS):
            px, py, pc = _flip(x, y, c, f)
            src = srcs[a].at[4 * px + 2 * py + pc] if scatter else srcs[a]
            copies.append(pltpu.make_async_remote_copy(
                src_ref=src, dst_ref=lands[a].at[me], send_sem=send_sems.at[7 * a + k],
                recv_sem=recv_sems.at[7 * a + k], device_id=(px, py, pc), device_id_type=MESH))
    return copies


def _own_copies(srcs, lands, local_sems, scatter):
    x, y, c = _my_pos()
    me = 4 * x + 2 * y + c
    return [pltpu.make_async_copy(srcs[a].at[me] if scatter else srcs[a], lands[a].at[me], local_sems.at[a])
            for a in range(len(srcs))]


def xfer_start(arrs, scatter, after, name):
    n = len(arrs)
    land_shapes = [a.shape if scatter else (N_DEV,) + a.shape for a in arrs]

    def body(*refs):
        srcs, lands = refs[:n], refs[n:2 * n]
        send_sems, recv_sems, local_sems = refs[2 * n + 1], refs[2 * n + 2], refs[2 * n + 3]
        token = refs[-1]
        for cp in _xfer_copies(srcs, lands, send_sems, recv_sems, scatter):
            cp.start()
        for cp in _own_copies(srcs, lands, local_sems, scatter):
            cp.start()
        token[...] = jnp.zeros_like(token)

    out = pl.pallas_call(
        body, name=name,
        out_shape=(pltpu.SemaphoreType.DMA((7 * n,)), pltpu.SemaphoreType.DMA((7 * n,)),
                   pltpu.SemaphoreType.DMA((n,)),
                   *[pltpu.HBM(a.shape, a.dtype) for a in arrs],
                   *[pltpu.HBM(s, a.dtype) for s, a in zip(land_shapes, arrs)],
                   _sds((8, 128), F32)),
        in_specs=[_HBM] * (2 * n) + [pl.BlockSpec(memory_space=pl.ANY)],
        out_specs=(_SEM, _SEM, _SEM, *([_HBM] * (2 * n)), pl.BlockSpec(memory_space=pltpu.VMEM)),
        input_output_aliases={i: 3 + i for i in range(2 * n)},
        compiler_params=pltpu.CompilerParams(has_side_effects=_EFFECT),
    )(*[pltpu.with_memory_space_constraint(a, pltpu.HBM) for a in arrs],
      *[pltpu.with_memory_space_constraint(lax.empty(s, a.dtype), pltpu.HBM) for s, a in zip(land_shapes, arrs)],
      after)
    return (out[0], out[1], out[2], list(out[3:3 + n]), list(out[3 + n:3 + 2 * n]), scatter), out[-1]


def xfer_wait(handle, after, name):
    send_sems, recv_sems, local_sems, srcs, lands, scatter = handle
    n = len(srcs)

    def body(*refs):
        srcs_r, lands_r = refs[:n], refs[n:2 * n]
        ssem, rsem, lsem = refs[2 * n], refs[2 * n + 1], refs[2 * n + 2]
        for cp in _xfer_copies(srcs_r, lands_r, ssem, rsem, scatter):
            cp.wait_send()
            cp.wait_recv()
        for cp in _own_copies(srcs_r, lands_r, lsem, scatter):
            cp.wait()

    out = pl.pallas_call(
        body, name=name,
        out_shape=(*[pltpu.HBM(a.shape, a.dtype) for a in srcs], *[pltpu.HBM(a.shape, a.dtype) for a in lands]),
        in_specs=[_HBM] * (2 * n) + [_SEM, _SEM, _SEM, pl.BlockSpec(memory_space=pl.ANY)],
        out_specs=tuple([_HBM] * (2 * n)),
        input_output_aliases={i: i for i in range(2 * n)},
        compiler_params=pltpu.CompilerParams(has_side_effects=_EFFECT),
    )(*srcs, *lands, send_sems, recv_sems, local_sems, after)
    return list(out[n:])


def _g2_first(srcs, lands, send_sems, recv_sems):
    x, y, c = _my_pos()
    me = 4 * x + 2 * y + c
    peers = [(x, y, 1 - c), (1 - x, y, c), (x, 1 - y, c), (1 - x, 1 - y, c)]
    return [[pltpu.make_async_remote_copy(
        src_ref=srcs[a], dst_ref=lands[a].at[me], send_sem=send_sems.at[4 * a + k], recv_sem=recv_sems.at[4 * a + k],
        device_id=p, device_id_type=MESH) for k, p in enumerate(peers)] for a in range(len(srcs))]


def _g2_relay(lands, send_sems, recv_sems):
    x, y, c = _my_pos()
    chips = [(1 - x, y), (x, 1 - y), (1 - x, 1 - y)]
    out = []
    for a in range(len(lands)):
        row = []
        for j, (px, py) in enumerate(chips):
            slab = lands[a].at[4 * px + 2 * py + c]
            row.append(pltpu.make_async_remote_copy(
                src_ref=slab, dst_ref=slab, send_sem=send_sems.at[3 * a + j], recv_sem=recv_sems.at[3 * a + j],
                device_id=(x, y, 1 - c), device_id_type=MESH))
        out.append(row)
    return out


def gather2_start(arrs, after, name):
    n = len(arrs)

    def body(*refs):
        srcs, lands = refs[:n], refs[n:2 * n]
        send_sems, recv_sems, local_sems = refs[2 * n + 1], refs[2 * n + 2], refs[2 * n + 3]
        for row in _g2_first(srcs, lands, send_sems, recv_sems):
            for cp in row:
                cp.start()
        for cp in _own_copies(srcs, lands, local_sems, False):
            cp.start()
        refs[-1][...] = jnp.zeros_like(refs[-1])

    lands0 = [lax.empty((N_DEV,) + a.shape, a.dtype) for a in arrs]
    out = pl.pallas_call(
        body, name=name,
        out_shape=(pltpu.SemaphoreType.DMA((4 * n,)), pltpu.SemaphoreType.DMA((4 * n,)), pltpu.SemaphoreType.DMA((n,)),
                   *[pltpu.HBM(a.shape, a.dtype) for a in arrs], *[pltpu.HBM(l.shape, l.dtype) for l in lands0],
                   _sds((8, 128), F32)),
        in_specs=[_HBM] * (2 * n) + [pl.BlockSpec(memory_space=pl.ANY)],
        out_specs=(_SEM, _SEM, _SEM, *([_HBM] * (2 * n)), pl.BlockSpec(memory_space=pltpu.VMEM)),
        input_output_aliases={i: 3 + i for i in range(2 * n)},
        compiler_params=pltpu.CompilerParams(has_side_effects=_EFFECT),
    )(*[pltpu.with_memory_space_constraint(a, pltpu.HBM) for a in arrs],
      *[pltpu.with_memory_space_constraint(l, pltpu.HBM) for l in lands0], after)
    return dict(send1=out[0], recv1=out[1], local=out[2], srcs=list(out[3:3 + n]), lands=list(out[3 + n:3 + 2 * n])), out[-1]


def gather2_relay(handle, after, name):
    n = len(handle["lands"])

    def body(*refs):
        lands = refs[:n]
        send1, recv1 = refs[n], refs[n + 1]
        send2, recv2 = refs[n + 3], refs[n + 4]
        firsts = _g2_first([l.at[0] for l in lands], lands, send1, recv1)
        relays = _g2_relay(lands, send2, recv2)
        for a in range(n):
            for j in range(3):
                firsts[a][1 + j].wait_recv()
                relays[a][j].start()
        refs[-1][...] = jnp.zeros_like(refs[-1])

    out = pl.pallas_call(
        body, name=name,
        out_shape=(pltpu.SemaphoreType.DMA((3 * n,)), pltpu.SemaphoreType.DMA((3 * n,)),
                   *[pltpu.HBM(l.shape, l.dtype) for l in handle["lands"]], _sds((8, 128), F32)),
        in_specs=[_HBM] * n + [_SEM, _SEM, pl.BlockSpec(memory_space=pl.ANY)],
        out_specs=(_SEM, _SEM, *([_HBM] * n), pl.BlockSpec(memory_space=pltpu.VMEM)),
        input_output_aliases={i: 2 + i for i in range(n)},
        compiler_params=pltpu.CompilerParams(has_side_effects=_EFFECT),
    )(*handle["lands"], handle["send1"], handle["recv1"], after)
    new = dict(handle)
    new.update(send2=out[0], recv2=out[1], lands=list(out[2:2 + n]))
    return new, out[-1]


def gather2_wait(handle, after, name):
    n = len(handle["lands"])

    def body(*refs):
        srcs, lands = refs[:n], refs[n:2 * n]
        send1, recv1, local, send2, recv2 = refs[2 * n:2 * n + 5]
        for a, row in enumerate(_g2_first(srcs, lands, send1, recv1)):
            row[0].wait_recv()
            for cp in row:
                cp.wait_send()
        for row in _g2_relay(lands, send2, recv2):
            for cp in row:
                cp.wait_send()
                cp.wait_recv()
        for cp in _own_copies(srcs, lands, local, False):
            cp.wait()

    out = pl.pallas_call(
        body, name=name,
        out_shape=(*[pltpu.HBM(a.shape, a.dtype) for a in handle["srcs"]],
                   *[pltpu.HBM(l.shape, l.dtype) for l in handle["lands"]]),
        in_specs=[_HBM] * (2 * n) + [_SEM] * 5 + [pl.BlockSpec(memory_space=pl.ANY)],
        out_specs=tuple([_HBM] * (2 * n)),
        input_output_aliases={i: i for i in range(2 * n)},
        compiler_params=pltpu.CompilerParams(has_side_effects=_EFFECT),
    )(*handle["srcs"], *handle["lands"], handle["send1"], handle["recv1"], handle["local"], handle["send2"],
      handle["recv2"], after)
    return list(out[n:])


def adamw(parts, w, m, v, name, row0=0, prev=None):
    r_tot, C = w.shape
    n_parts, R = parts.shape[0], parts.shape[1]
    row_bytes = 2 * (n_parts * C * parts.dtype.itemsize + 7 * C * 4)
    tr = R
    for cand in (512, 352, 256, 176, 128, 64):
        if R % cand == 0 and row0 % cand == 0 and R > cand and cand * row_bytes <= ADAMW_VMEM_BUDGET:
            tr = cand
            break
    tc = C
    if tr == R and R * row_bytes > ADAMW_VMEM_BUDGET:
        assert row0 == 0 and R == r_tot
        tc = next(t for t in (512, 256, 128) if C % t == 0 and R * row_bytes * t // C <= ADAMW_VMEM_BUDGET)
    assert row0 % tr == 0 and (tr % 8 == 0 or (tr == r_tot and row0 == 0))
    blk0 = row0 // tr
    c1 = 1.0 / (1.0 - ADAM_B1 ** ADAM_STEP)
    c2 = 1.0 / (1.0 - ADAM_B2 ** ADAM_STEP)

    def body(p_ref, w_ref, m_ref, v_ref, *rest):
        g_ref, d_ref, nm_ref, nv_ref = rest[-4:]
        g = p_ref[0].astype(F32)
        for k in range(1, n_parts):
            g = g + p_ref[k].astype(F32)
        nm = ADAM_B1 * m_ref[...] + (1.0 - ADAM_B1) * g
        nv = ADAM_B2 * v_ref[...] + (1.0 - ADAM_B2) * (g * g)
        g_ref[...] = g
        nm_ref[...] = nm
        nv_ref[...] = nv
        d_ref[...] = -ADAM_LR * ((nm * c1) / (jnp.sqrt(nv * c2) + ADAM_EPS) + ADAM_WD * w_ref[...])

    if tc == C:
        grid = (R // tr,)
        blk = pl.BlockSpec((tr, C), lambda i: (i + blk0, 0))
        p_spec = pl.BlockSpec((n_parts, tr, C), lambda i: (0, i, 0))
    else:
        grid = (C // tc,)
        blk = pl.BlockSpec((R, tc), lambda i: (0, i))
        p_spec = pl.BlockSpec((n_parts, R, tc), lambda i: (0, 0, i))
    in_specs = [p_spec, blk, blk, blk]
    args = [parts, w, m, v]
    aliases = {}
    if prev is not None:
        in_specs += [pl.BlockSpec(memory_space=pl.ANY)] * 4
        args += list(prev)
        aliases = {4 + k: k for k in range(4)}
    return pl.pallas_call(
        body, name=name, grid=grid, in_specs=in_specs, out_specs=(blk, blk, blk, blk),
        out_shape=tuple(_sds((r_tot, C), F32) for _ in range(4)), input_output_aliases=aliases,
        compiler_params=pltpu.CompilerParams(vmem_limit_bytes=VMEM_LIMIT))(*args)


def _ffn_fwd(x, nw, sh, sc, g, wt_gu, w_dn):
    h = norm_mod(x, nw, sh, sc, "ffn_norm")
    gp, up, a = ffn_up(h, wt_gu)
    y, xn = matmul(a, w_dn, "nn", BF, "ffn_down", res=x, gate=g, coef=0.5)
    return xn, (x, h, gp, up, a, y)


def _ffn_bwd(dxo, saved, nw, sc, g, wt_gu, w_dn):
    x, h, gp, up, a, y = saved
    dy, acc1 = resid_gate_bwd(dxo, y, g, 0.5, "ffn_gate_bwd")
    d_wdn = matmul(a, dy, "tn", BF, "ffn_down_wgrad")
    dg, du = ffn_down_dgrad(dy, w_dn, gp, up)
    d_wt = ffn_up_wgrad(dg, du, h)
    dh = ffn_up_dgrad(dg, du, wt_gu)
    dx, acc2 = norm_mod_bwd(x, dh, dxo, nw, sc, "ffn_norm_bwd")
    return dx, d_wt, d_wdn, (acc2[0], acc2[1], acc1[0]), acc2[2]


def _group_layout(a):
    L = a.shape[0]
    return a[:, :SSM_HEADS].reshape(L, SSM_GROUPS, _HPG).transpose(1, 0, 2)


def _ungroup_layout(a):
    L = a.shape[1]
    return jnp.pad(a.transpose(1, 0, 2).reshape(L, SSM_HEADS), ((0, 0), (0, 128 - SSM_HEADS)))


def _pad_row(vec, n=128):
    return jnp.pad(vec.reshape(1, -1), ((0, 0), (0, n - vec.shape[-1])))


def _mamba_fwd(x, nw, sh, sc, g, p):
    h = norm_mod(x, nw, sh, sc, "mix_norm")
    zx = matmul(h, p["w_in_t"], "nt", F32, "ssm_in")
    xc = conv_fwd(zx, p["conv_w"], p["conv_b"])
    dt, acs = dt_prep(zx, p["dt_bias"], p["a_log"])
    dt_g, acs_g = _group_layout(dt), _group_layout(acs)
    acs_t = acs_g.transpose(0, 2, 1)
    y, pst = ssd_fwd(xc, dt_g, acs_g, acs_t, p["d_exp"])
    yn = gate_norm(y, zx, p["norm_w"])
    yo, xn = matmul(yn, p["w_out"], "nn", BF, "ssm_out", res=x, gate=g, coef=1.0)
    return xn, (x, h, zx, xc, dt, dt_g, acs_g, acs_t, y, pst, yn, yo)


def _mamba_bwd(dxo, saved, nw, sc, g, p):
    x, h, zx, xc, dt, dt_g, acs_g, acs_t, y, pst, yn, yo = saved
    dyo, acc1 = resid_gate_bwd(dxo, yo, g, 1.0, "mix_gate_bwd")
    d_wout = matmul(yn, dyo, "tn", BF, "ssm_out_wgrad")
    dyn = matmul(dyo, p["w_out"], "nt", F32, "ssm_out_dgrad")
    dy, dz, accn = gate_norm_bwd(dyn, y, zx, p["norm_w"])
    dxs, dB, dC, ddt_g, da_g, dd = ssd_bwd(dy, xc, dt_g, acs_g, acs_t, pst, p["d_exp"])
    dxc = jnp.concatenate([dxs, dB, dC], axis=1)
    ds = conv_bwd_act(dxc, zx, p["conv_w"], p["conv_b"])
    du, accc = conv_bwd(ds, zx, p["conv_w"])
    draw, accdt = dt_bwd(_ungroup_layout(ddt_g), _ungroup_layout(da_g), dt, zx, p["dt_bias"], p["a_log"])
    dzx = jnp.concatenate([dz, du, draw], axis=1)
    d_win = matmul(dzx, h, "tn", BF, "ssm_in_wgrad")[:IN_PROJ]
    dh = matmul(dzx, p["w_in_t"], "nn", F32, "ssm_in_dgrad")
    dx, acc2 = norm_mod_bwd(x, dh, dxo, nw, sc, "mix_norm_bwd")
    small = dict(conv_w=accc[:CONV_WIDTH], conv_b=accc[CONV_WIDTH], dt_bias=accdt[0, :SSM_HEADS],
                 a_log=accdt[1, :SSM_HEADS], d=dd.reshape(SSM_HEADS, SSM_HEADDIM).sum(-1), norm_w=accn[0])
    return dx, d_win, d_wout, (acc2[0], acc2[1], acc1[0]), acc2[2], small


def _attn_layer_fwd(x, nw, sh, sc, g, p, kv):
    h = norm_mod(x, nw, sh, sc, "mix_norm")
    q = matmul(h, p["w_q"], "nn", F32, "attn_q", bias=p["b_q"])
    o = attn_fwd(q, kv, p["sinks"])
    yo, xn = matmul(o, p["w_o"], "nn", BF, "attn_o", bias=p["b_o"], res=x, gate=g, coef=1.0)
    return xn, (x, h, q, o, yo)


def _attn_layer_bwd(dxo, saved, nw, sc, g, p, kv):
    x, h, q, o, yo = saved
    dyo, acc1 = resid_gate_bwd(dxo, yo, g, 1.0, "mix_gate_bwd")
    d_wo = matmul(o, dyo, "tn", BF, "attn_o_wgrad")
    do = matmul(dyo, p["w_o"], "nt", F32, "attn_o_dgrad")
    dq, dkv_c, dkv_p, acca = attn_bwd(q, kv, do, p["sinks"])
    d_wq = matmul(h, dq, "tn", BF, "attn_q_wgrad")
    dh = matmul(dq, p["w_q"], "nt", F32, "attn_q_dgrad")
    dx, acc2 = norm_mod_bwd(x, dh, dxo, nw, sc, "mix_norm_bwd")
    small = dict(b_q=acca[0], sinks=acca[1, :ATT_HEADS], b_o=acc1[1])
    return dx, d_wq, d_wo, (acc2[0], acc2[1], acc1[0]), acc2[2], small, (dkv_c, dkv_p)


def _pack_rows(pieces):
    rows, spans, off = [], [], 0
    for a in pieces:
        flat = a.reshape(-1).astype(F32)
        n = -(-flat.shape[0] // D_MODEL)
        rows.append(jnp.pad(flat, (0, n * D_MODEL - flat.shape[0])).reshape(n, D_MODEL))
        spans.append((off, a.shape))
        off += n
    pad = -off % 8
    if pad:
        rows.append(jnp.zeros((pad, D_MODEL), F32))
    return jnp.concatenate(rows, axis=0), spans, off + pad


def _unpack_rows(g, spans):
    out = []
    for off, shape in spans:
        size = 1
        for s in shape:
            size *= s
        n = -(-size // D_MODEL)
        out.append(g[:, off:off + n].reshape(N_DEV, n * D_MODEL)[:, :size].reshape((N_DEV,) + tuple(shape)))
    return out


def _unshard_last(g):
    nd = g.ndim
    perm = tuple(range(1, nd - 1)) + (0, nd - 1)
    t = g.transpose(perm)
    return t.reshape(t.shape[:-2] + (N_DEV * g.shape[-1],))


def _shard_last(a, me):
    s = a.shape[-1] // N_DEV
    return lax.dynamic_slice_in_dim(a, me * s, s, axis=a.ndim - 1)


def kernel(x, c, ffn_norm_w, ffn_w_gu, ffn_w_down, mod_w, mod_b, mix_norm_w, ssm_w_in, ssm_conv_w, ssm_conv_b, ssm_dt_bias, ssm_a_log, ssm_d, ssm_norm_w, ssm_w_out, kv_norm_w, kv_mod_w, kv_mod_b, w_kv, b_kv, attn_w_q, attn_b_q, attn_sinks, attn_w_o, attn_b_o, final_norm_w, loss_target, m_ffn_norm_w, m_ffn_w_gu, m_ffn_w_down, m_mod_w, m_mod_b, m_mix_norm_w, m_ssm_w_in, m_ssm_conv_w, m_ssm_conv_b, m_ssm_dt_bias, m_ssm_a_log, m_ssm_d, m_ssm_norm_w, m_ssm_w_out, m_kv_norm_w, m_kv_mod_w, m_kv_mod_b, m_w_kv, m_b_kv, m_attn_w_q, m_attn_b_q, m_attn_sinks, m_attn_w_o, m_attn_b_o, m_final_norm_w, v_ffn_norm_w, v_ffn_w_gu, v_ffn_w_down, v_mod_w, v_mod_b, v_mix_norm_w, v_ssm_w_in, v_ssm_conv_w, v_ssm_conv_b, v_ssm_dt_bias, v_ssm_a_log, v_ssm_d, v_ssm_norm_w, v_ssm_w_out, v_kv_norm_w, v_kv_mod_w, v_kv_mod_b, v_w_kv, v_b_kv, v_attn_w_q, v_attn_b_q, v_attn_sinks, v_attn_w_o, v_attn_b_o, v_final_norm_w):
    D = D_MODEL
    me = 4 * lax.axis_index("x") + 2 * lax.axis_index("y") + lax.axis_index("c")
    xs = x[0]
    target = loss_target[0]
    mod_cols = mod_w.shape[-1]
    kvm_cols = kv_mod_w.shape[-1]

    def fence(arrs):
        tot = jnp.zeros((1, 1), F32)
        for a in arrs:
            tot = tot + lax.slice(a, (0,) * a.ndim, (1,) * a.ndim).reshape(1, 1).astype(F32)
        return jnp.broadcast_to(tot, (8, 128))

    packed, spans, _ = _pack_rows([c, ffn_norm_w, ssm_conv_w, ssm_conv_b, ssm_norm_w])
    nrow = packed.shape[0]
    g1 = small_all_gather(packed).reshape(N_DEV, nrow, D)
    c_all, fnw_g, cw_g, cb_g, snw_g = _unpack_rows(g1, spans)
    c_all = c_all.reshape(N_DEV, D)
    ffn_nw = _unshard_last(fnw_g)
    conv_w = _unshard_last(cw_g)
    conv_b = _unshard_last(cb_g)
    ssm_nw = _unshard_last(snw_g)

    mod_b_loc = lax.dynamic_slice_in_dim(mod_b, me * mod_cols, mod_cols, axis=1).reshape(DEPTH, 1, mod_cols)
    kvb_loc = lax.dynamic_slice_in_dim(kv_mod_b, me * kvm_cols, kvm_cols, axis=0).reshape(1, 1, kvm_cols)
    modp, c_act = mod_fwd(c_all, mod_w, mod_b_loc, "mod_fwd")
    kvmp, _ = mod_fwd(c_all, kv_mod_w.reshape(1, D, kvm_cols), kvb_loc, "kv_mod_fwd")
    packed2, spans2, _ = _pack_rows([modp, kvmp])
    nrow2 = packed2.shape[0]
    g2 = small_all_gather(packed2).reshape(N_DEV, nrow2, D)
    modp_g, kvmp_g = _unpack_rows(g2, spans2)
    mod_all = modp_g.transpose(1, 2, 0, 3).reshape(DEPTH, N_DEV, N_MOD * D)
    kvm_all = kvmp_g.transpose(1, 2, 0, 3).reshape(N_DEV, 2 * D)
    mod_me = lax.dynamic_index_in_dim(mod_all, me, axis=1, keepdims=False).reshape(DEPTH, N_MOD, 1, D)
    kvm_me = lax.dynamic_index_in_dim(kvm_all, me, axis=0, keepdims=False).reshape(2, 1, D)

    gu_t = jnp.swapaxes(ffn_w_gu, 2, 3)
    win_t = jnp.transpose(ssm_w_in, (2, 0, 1))
    S = gu_t.shape[2]
    s_in = win_t.shape[0]
    r_dn = ffn_w_down.shape[2]
    r_mix = ssm_w_out.shape[1]
    r_at = attn_w_q.shape[1]

    def layer_pack(k):
        arrs = [gu_t[k, 0].astype(BF), gu_t[k, 1].astype(BF), ffn_w_down[k, 0].astype(BF), ffn_w_down[k, 1].astype(BF)]
        if k < N_A:
            arrs += [ssm_w_out[k].astype(BF), win_t[:, k].astype(BF)]
        else:
            arrs += [attn_w_q[k - N_A].astype(BF), attn_w_o[k - N_A].astype(BF)]
        if k == N_A:
            arrs.append(w_kv.astype(BF))
        return arrs

    packs = [layer_pack(k) for k in range(DEPTH)]
    gathered = [None] * DEPTH
    first = big_all_gather([packs[0][0], packs[0][2], packs[0][4], packs[0][5]])
    pending = [None] * DEPTH
    pending[0], tok_next = gather2_start([packs[0][1], packs[0][3]], fence([g2, first[0]]), "gather_start_0")
    for k in range(1, DEPTH):
        pending[k], tok_next = gather2_start(packs[k], tok_next, "gather_start_%d" % k)
    gathered[0] = [first[0], None, first[1], None, first[2], first[3]]

    def wt_gu_full(i, j):
        return gathered[i][j].reshape(N_DEV * S, D)

    def w_dn_full(i, j):
        return gathered[i][2 + j].reshape(D_FF, D)

    def mix_rows(i, a):
        return gathered[i][4 + a].reshape(-1, D)

    def mamba_params(j):
        w_in_t = jnp.pad(mix_rows(j, 1), ((0, IN_PROJ_PAD - IN_PROJ), (0, 0)))
        return dict(w_in_t=w_in_t, w_out=mix_rows(j, 0), conv_w=conv_w[j], conv_b=conv_b[j].reshape(1, -1),
                    dt_bias=_pad_row(ssm_dt_bias[j]), a_log=_pad_row(ssm_a_log[j]),
                    d_exp=jnp.repeat(ssm_d[j], SSM_HEADDIM).reshape(SSM_GROUPS, 1, _GW),
                    norm_w=ssm_nw[j].reshape(1, -1))

    def attn_params(j):
        return dict(w_q=mix_rows(N_A + j, 0), w_o=mix_rows(N_A + j, 1),
                    b_q=attn_b_q[j].reshape(1, -1), b_o=attn_b_o[j].reshape(1, -1), sinks=_pad_row(attn_sinks[j]))

    saved = []
    kv = None
    kv_saved = None
    w_kv_full = None
    xcur = xs
    for i in range(DEPTH):
        if i >= 1:
            gathered[i] = gather2_wait(pending[i], xcur, "gather_wait_%d" % i)
        md = mod_me[i]
        if i == 0:
            md = md + tok_next[0, 0]
        if i == N_A:
            w_kv_full = gathered[N_A][6].reshape(D, KV_DIM)
            h_kv = norm_mod(xcur, kv_norm_w.reshape(1, D), kvm_me[0], kvm_me[1], "kv_norm")
            kv = matmul(h_kv, w_kv_full, "nn", F32, "kv_proj", bias=b_kv.reshape(1, -1))
            kv_saved = (xcur, h_kv)
        x1, s1 = _ffn_fwd(xcur, ffn_nw[i, 0].reshape(1, D), md[0], md[1], md[2], wt_gu_full(i, 0), w_dn_full(i, 0))
        gm = md[5]
        if i == 0:
            pending[0], tok_r = gather2_relay(pending[0], x1, "gather_relay_0")
            gm = gm + tok_r[0, 0]
        if i < N_A:
            pm = mamba_params(i)
            x2, s2 = _mamba_fwd(x1, mix_norm_w[i].reshape(1, D), md[3], md[4], gm, pm)
        else:
            pm = attn_params(i - N_A)
            x2, s2 = _attn_layer_fwd(x1, mix_norm_w[i].reshape(1, D), md[3], md[4], gm, pm, kv)
        g2f = md[8]
        if i + 1 < DEPTH:
            pending[i + 1], tok_r = gather2_relay(pending[i + 1], x2, "gather_relay_%d" % (i + 1))
            g2f = g2f + tok_r[0, 0]
        if i == 0:
            rest = gather2_wait(pending[0], x2, "gather_wait_0")
            gathered[0] = [first[0], rest[0], first[1], rest[1], first[2], first[3]]
        x3, s3 = _ffn_fwd(x2, ffn_nw[i, 1].reshape(1, D), md[6], md[7], g2f, wt_gu_full(i, 1), w_dn_full(i, 1))
        saved.append((s1, s2, s3, pm))
        xcur = x3

    dx, accf = final_loss(xcur, final_norm_w.reshape(1, D), target)
    d_mod = [None] * DEPTH
    d_ffn_nw = [[None, None] for _ in range(DEPTH)]
    d_mix_nw = [None] * DEPTH
    sm_m, sm_a = [None] * N_A, [None] * N_A
    kv_parts = [None] * N_A
    d_kvm = d_kv_nw = d_bkv = None
    exchanges = []
    tok = None

    def send(arrs, tag, after=None):
        handle, t = xfer_start(arrs, True, dx if after is None else after, "exch_start_%s" % tag)
        exchanges.append((handle, tag))
        return t

    def ffn_slabs(d_wt, d_wdn):
        return [d_wt.reshape(N_DEV, S, D), d_wdn.reshape(N_DEV, r_dn, D)]

    for i in reversed(range(DEPTH)):
        md = mod_me[i]
        s1, s2, s3, pm = saved[i]
        g2 = md[8] if tok is None else md[8] + tok[0, 0]
        dx, d_wt, d_wdn, m2, d_ffn_nw[i][1] = _ffn_bwd(
            dx, s3, ffn_nw[i, 1].reshape(1, D), md[7], g2, wt_gu_full(i, 1), w_dn_full(i, 1))
        tok = send(ffn_slabs(d_wt, d_wdn), "f%d1" % i)
        gm = md[5] + tok[0, 0]
        if i < N_A:
            dx, d_in, d_out, mm_, d_mix_nw[i], sm_m[i] = _mamba_bwd(dx, s2, mix_norm_w[i].reshape(1, D), md[4], gm, pm)
            tok = send([d_in.reshape(N_DEV, s_in, D), d_out.reshape(N_DEV, r_mix, D)], "m%d" % i)
        else:
            j = i - N_A
            dx, d_q, d_o, mm_, d_mix_nw[i], sm_a[j], kv_parts[j] = _attn_layer_bwd(
                dx, s2, mix_norm_w[i].reshape(1, D), md[4], gm, pm, kv)
            tok = send([d_q.reshape(N_DEV, r_at, D), d_o.reshape(N_DEV, r_at, D)], "m%d" % i)
        g1 = md[2] + tok[0, 0]
        dx, d_wt, d_wdn, m1, d_ffn_nw[i][0] = _ffn_bwd(
            dx, s1, ffn_nw[i, 0].reshape(1, D), md[1], g1, wt_gu_full(i, 0), w_dn_full(i, 0))
        d_mod[i] = jnp.concatenate(list(m1) + list(mm_) + list(m2), axis=0)
        last = ffn_slabs(d_wt, d_wdn)
        if i == N_A:
            x_kv, h_kv = kv_saved
            dkv, acck = kv_grad_combine(kv_parts)
            d_bkv = acck[0]
            d_kv_w = matmul(h_kv, dkv, "tn", BF, "kv_wgrad")
            dh_kv = matmul(dkv, w_kv_full, "nt", F32, "kv_dgrad")
            dx, acc_kv = norm_mod_bwd(x_kv, dh_kv, dx, kv_norm_w.reshape(1, D), kvm_me[1], "kv_norm_bwd")
            d_kvm = jnp.concatenate([acc_kv[0], acc_kv[1]], axis=0)
            d_kv_nw = acc_kv[2]
            last.append(d_kv_w.reshape(N_DEV, -1, KV_DIM))
        if i > 0:
            tok = send(last, "f%d0" % i)
    grad_x = dx.reshape(x.shape)

    small_list = [
        jnp.stack(d_mod, 0), d_kvm,
        jnp.stack([jnp.stack(r, 0) for r in d_ffn_nw], 0),
        jnp.stack(d_mix_nw, 0),
        jnp.stack([s["conv_w"] for s in sm_m], 0), jnp.stack([s["conv_b"] for s in sm_m], 0),
        jnp.stack([s["dt_bias"] for s in sm_m], 0), jnp.stack([s["a_log"] for s in sm_m], 0),
        jnp.stack([s["d"] for s in sm_m], 0), jnp.stack([s["norm_w"] for s in sm_m], 0),
        d_kv_nw, d_bkv,
        jnp.stack([s["b_q"] for s in sm_a], 0), jnp.stack([s["sinks"] for s in sm_a], 0),
        jnp.stack([s["b_o"] for s in sm_a], 0), accf[0], accf[1],
    ]
    packed3, spans3, _ = _pack_rows(small_list)
    nrow3 = packed3.shape[0]
    g3 = small_all_gather(packed3).reshape(N_DEV, nrow3, D)
    tok_last = send(last, "f00", after=g3)
    (p_mod, p_kvm, p_fnw, p_mnw, p_cw, p_cb, p_dtb, p_al, p_d, p_snw, p_kvnw, p_bkv, p_bq, p_sk, p_bo, p_fin,
     p_loss) = _unpack_rows(g3, spans3)

    loss = 0.5 / D * jnp.sum(p_loss)

    c_act_t = c_act.T
    dmod_loc = _shard_last(p_mod, me).transpose(1, 0, 2)
    dkvm_loc = _shard_last(p_kvm, me).reshape(1, N_DEV, kvm_cols) + tok_last[0, 0]
    gp_mod_w = mod_wgrad(c_act_t, dmod_loc, "mod_wgrad")
    gp_kvm_w = mod_wgrad(c_act_t, dkvm_loc, "kv_mod_wgrad")[0]

    def as_parts_single(a):
        return a[None]

    def upd(name, parts, w, m, v):
        shp = w.shape
        c_last = shp[-1]
        out = adamw(parts.reshape(parts.shape[0], -1, c_last), w.reshape(-1, c_last), m.reshape(-1, c_last),
                    v.reshape(-1, c_last), "adamw_" + name)
        return tuple(o.reshape(shp) for o in out)

    views = {
        "ffn_w_gu": [jnp.swapaxes(t, 2, 3).reshape(-1, D) for t in (ffn_w_gu, m_ffn_w_gu, v_ffn_w_gu)],
        "ffn_w_down": [t.reshape(-1, D) for t in (ffn_w_down, m_ffn_w_down, v_ffn_w_down)],
        "ssm_w_out": [t.reshape(-1, D) for t in (ssm_w_out, m_ssm_w_out, v_ssm_w_out)],
        "attn_w_q": [t.reshape(-1, D) for t in (attn_w_q, m_attn_w_q, v_attn_w_q)],
        "attn_w_o": [t.reshape(-1, D) for t in (attn_w_o, m_attn_w_o, v_attn_w_o)],
    }
    filled = {k: None for k in views}

    def upd_rows(name, parts, row0):
        w, m, v = views[name]
        filled[name] = adamw(parts, w, m, v, "adamw_" + name, row0=row0, prev=filled[name])
        return filled[name][3]

    res = {}
    res["ffn_norm_w"] = upd("ffn_norm_w", _shard_last(p_fnw, me), ffn_norm_w, m_ffn_norm_w, v_ffn_norm_w)
    res["mod_w"] = upd("mod_w", as_parts_single(gp_mod_w), mod_w, m_mod_w, v_mod_w)
    res["mod_b"] = upd("mod_b", p_mod, mod_b, m_mod_b, v_mod_b)
    res["mix_norm_w"] = upd("mix_norm_w", p_mnw, mix_norm_w, m_mix_norm_w, v_mix_norm_w)
    res["ssm_conv_w"] = upd("ssm_conv_w", _shard_last(p_cw, me), ssm_conv_w, m_ssm_conv_w, v_ssm_conv_w)
    res["ssm_conv_b"] = upd("ssm_conv_b", _shard_last(p_cb, me), ssm_conv_b, m_ssm_conv_b, v_ssm_conv_b)
    res["ssm_dt_bias"] = upd("ssm_dt_bias", p_dtb, ssm_dt_bias, m_ssm_dt_bias, v_ssm_dt_bias)
    res["ssm_a_log"] = upd("ssm_a_log", p_al, ssm_a_log, m_ssm_a_log, v_ssm_a_log)
    res["ssm_d"] = upd("ssm_d", p_d, ssm_d, m_ssm_d, v_ssm_d)
    res["ssm_norm_w"] = upd("ssm_norm_w", _shard_last(p_snw, me), ssm_norm_w, m_ssm_norm_w, v_ssm_norm_w)
    res["kv_norm_w"] = upd("kv_norm_w", p_kvnw.reshape(N_DEV, 1, D), kv_norm_w.reshape(1, D),
                           m_kv_norm_w.reshape(1, D), v_kv_norm_w.reshape(1, D))
    res["kv_mod_w"] = upd("kv_mod_w", as_parts_single(gp_kvm_w), kv_mod_w, m_kv_mod_w, v_kv_mod_w)
    res["kv_mod_b"] = upd("kv_mod_b", p_kvm.reshape(N_DEV, 1, 2 * D), kv_mod_b.reshape(1, -1),
                          m_kv_mod_b.reshape(1, -1), v_kv_mod_b.reshape(1, -1))
    res["b_kv"] = upd("b_kv", p_bkv.reshape(N_DEV, 1, KV_DIM), b_kv.reshape(1, -1), m_b_kv.reshape(1, -1),
                      v_b_kv.reshape(1, -1))
    res["attn_b_q"] = upd("attn_b_q", p_bq, attn_b_q, m_attn_b_q, v_attn_b_q)
    res["attn_sinks"] = upd("attn_sinks", p_sk, attn_sinks, m_attn_sinks, v_attn_sinks)
    res["attn_b_o"] = upd("attn_b_o", p_bo, attn_b_o, m_attn_b_o, v_attn_b_o)
    res["final_norm_w"] = upd("final_norm_w", p_fin.reshape(N_DEV, 1, D), final_norm_w.reshape(1, D),
                              m_final_norm_w.reshape(1, D), v_final_norm_w.reshape(1, D))

    chain = fence([dx, tok_last] + [t[3] for t in res.values()])
    r_in_parts = [None] * N_A
    r_kv = None
    for handle, tag in exchanges:
        got = xfer_wait(handle, chain, "exch_wait_%s" % tag)
        i = int(tag[1])
        if tag[0] == "f":
            jf = int(tag[2])
            done = [upd_rows("ffn_w_gu", got[0], (2 * i + jf) * S), upd_rows("ffn_w_down", got[1], (2 * i + jf) * r_dn)]
            if len(got) > 2:
                res["w_kv"] = upd("w_kv", got[2], w_kv, m_w_kv, v_w_kv)
                done.append(res["w_kv"][3])
        elif i < N_A:
            r_in_parts[i] = got[0]
            done = [upd_rows("ssm_w_out", got[1], i * r_mix)]
            if i == 0:
                win_out = adamw(jnp.stack(r_in_parts, axis=2).reshape(N_DEV, s_in * N_A, D),
                                *[jnp.transpose(t, (2, 0, 1)).reshape(-1, D) for t in (ssm_w_in, m_ssm_w_in, v_ssm_w_in)],
                                "adamw_ssm_w_in")
                res["ssm_w_in"] = tuple(jnp.transpose(t.reshape(win_t.shape), (1, 2, 0)) for t in win_out)
                done.append(win_out[3])
        else:
            done = [upd_rows("attn_w_q", got[0], (i - N_A) * r_at), upd_rows("attn_w_o", got[1], (i - N_A) * r_at)]
        chain = fence(done)

    res["ffn_w_gu"] = tuple(jnp.swapaxes(t.reshape(gu_t.shape), 2, 3) for t in filled["ffn_w_gu"])
    res["ffn_w_down"] = tuple(t.reshape(ffn_w_down.shape) for t in filled["ffn_w_down"])
    res["ssm_w_out"] = tuple(t.reshape(ssm_w_out.shape) for t in filled["ssm_w_out"])
    res["attn_w_q"] = tuple(t.reshape(attn_w_q.shape) for t in filled["attn_w_q"])
    res["attn_w_o"] = tuple(t.reshape(attn_w_o.shape) for t in filled["attn_w_o"])

    names = ["ffn_norm_w", "ffn_w_gu", "ffn_w_down", "mod_w", "mod_b", "mix_norm_w", "ssm_w_in", "ssm_conv_w",
             "ssm_conv_b", "ssm_dt_bias", "ssm_a_log", "ssm_d", "ssm_norm_w", "ssm_w_out", "kv_norm_w", "kv_mod_w",
             "kv_mod_b", "w_kv", "b_kv", "attn_w_q", "attn_b_q", "attn_sinks", "attn_w_o", "attn_b_o", "final_norm_w"]
    vec_shapes = {"kv_norm_w": (D,), "kv_mod_b": (2 * D,), "b_kv": (KV_DIM,), "final_norm_w": (D,)}
    outs = [loss, grad_x]
    for k in range(4):
        for nme in names:
            t = res[nme][k]
            if nme in vec_shapes:
                t = t.reshape(vec_shapes[nme])
            outs.append(t)
    return tuple(outs)
```

```python
import functools

import jax
import jax.numpy as jnp
from jax import lax
from jax.experimental import pallas as pl
from jax.experimental.pallas import tpu as pltpu

F32 = jnp.float32
BF = jnp.bfloat16
MESH = pl.DeviceIdType.MESH

N_DEV = 8
D_MODEL = 1024
DEPTH = 4
N_A = 2
EPS = 1e-5
N_MOD = 9
D_FF = 2816
D_INNER = 2048
SSM_HEADDIM = 64
SSM_HEADS = 32
SSM_GROUPS = 8
SSM_STATE = 128
CONV_WIDTH = 4
CHUNK = 512
CONV_DIM = D_INNER + 2 * SSM_GROUPS * SSM_STATE
IN_PROJ = D_INNER + CONV_DIM + SSM_HEADS
IN_PROJ_PAD = D_INNER + CONV_DIM + 128
ATT_HEADS = 16
KV_HEADS = 4
HEAD_DIM = 64
WINDOW = 128
KV_DIM = 2 * KV_HEADS * HEAD_DIM

ADAM_LR = 0.001
ADAM_B1 = 0.9
ADAM_B2 = 0.999
ADAM_EPS = 1e-08
ADAM_WD = 0.01
ADAM_STEP = 10

VMEM_LIMIT = 48 * 2 ** 20
ADAMW_VMEM_BUDGET = 24 * 2 ** 20
NEG = -1e30


def _call(body, name, grid, in_specs, out_specs, out_shape, scratch=()):
    return pl.pallas_call(
        body, name=name, grid=grid, in_specs=in_specs, out_specs=out_specs, out_shape=out_shape,
        scratch_shapes=list(scratch),
        compiler_params=pltpu.CompilerParams(vmem_limit_bytes=VMEM_LIMIT))


def _tile(n, cap):
    t = (cap // 128) * 128
    while t >= 128:
        if n % t == 0:
            return t
        t -= 128
    return n


def _sds(shape, dtype):
    return jax.ShapeDtypeStruct(shape, dtype)


def _sigmoid(v):
    return 1.0 / (1.0 + jnp.exp(-v))


def _dot(a, b, dims):
    return lax.dot_general(a, b, (dims, ((), ())), preferred_element_type=F32)


def _dot_nn(a, b):
    return _dot(a.astype(BF), b.astype(BF), ((1,), (0,)))


def _dot_nt(a, b):
    return _dot(a.astype(BF), b.astype(BF), ((1,), (1,)))


def _dot_tn(a, b):
    return _dot(a.astype(BF), b.astype(BF), ((0,), (0,)))


def matmul(a, b, mode, out_dtype, name, bias=None, res=None, gate=None, coef=1.0):
    if mode == "nn":
        (M, K), (_, N) = a.shape, b.shape
    elif mode == "nt":
        (M, K), (N, _) = a.shape, b.shape
    else:
        (K, M), (_, N) = a.shape, b.shape
    cap_n = 512 if K > 4096 else 1024
    tm = _tile(M, 1024 if (mode == "tn" or K <= D_FF) else 512)
    tn = _tile(N, cap_n)
    if mode != "tn" and tm * tn > 1024 * 896:
        tn = _tile(N, 512)
    if mode == "nn":
        a_spec = pl.BlockSpec((tm, K), lambda i, j: (i, 0))
        b_spec = pl.BlockSpec((K, tn), lambda i, j: (0, j))
        fn = _dot_nn
    elif mode == "nt":
        a_spec = pl.BlockSpec((tm, K), lambda i, j: (i, 0))
        b_spec = pl.BlockSpec((tn, K), lambda i, j: (j, 0))
        fn = _dot_nt
    else:
        a_spec = pl.BlockSpec((K, tm), lambda i, j: (0, i))
        b_spec = pl.BlockSpec((K, tn), lambda i, j: (0, j))
        fn = _dot_tn
    has_bias, has_res = bias is not None, res is not None
    o_spec = pl.BlockSpec((tm, tn), lambda i, j: (i, j))
    v_spec = pl.BlockSpec((1, tn), lambda i, j: (0, j))
    in_specs, args = [a_spec, b_spec], [a, b]
    if has_bias:
        in_specs.append(v_spec)
        args.append(bias)
    if has_res:
        in_specs += [o_spec, v_spec]
        args += [res, gate]

    def body(*refs):
        a_ref, b_ref = refs[0], refs[1]
        k = 2
        y = fn(a_ref[...], b_ref[...])
        if has_bias:
            y = y + refs[k][...]
            k += 1
        if has_res:
            res_ref, gate_ref = refs[k], refs[k + 1]
            refs[k + 2][...] = y.astype(out_dtype)
            refs[k + 3][...] = res_ref[...] + coef * gate_ref[...] * y
        else:
            refs[k][...] = y.astype(out_dtype)

    if has_res:
        out_shape = (_sds((M, N), out_dtype), _sds((M, N), F32))
        out_specs = (o_spec, o_spec)
    else:
        out_shape = _sds((M, N), out_dtype)
        out_specs = o_spec
    return _call(body, name, (M // tm, N // tn), in_specs, out_specs, out_shape)(*args)


def norm_mod(x, nw, sh, sc, name):
    L, D = x.shape
    tm = _tile(L, 512)

    def body(x_ref, nw_ref, sh_ref, sc_ref, h_ref):
        xf = x_ref[...]
        r = lax.rsqrt(jnp.mean(xf * xf, axis=-1, keepdims=True) + EPS)
        n = xf * r * nw_ref[...]
        h_ref[...] = (n * (1.0 + sc_ref[...]) + sh_ref[...]).astype(BF)

    row = pl.BlockSpec((tm, D), lambda i: (i, 0))
    vec = pl.BlockSpec((1, D), lambda i: (0, 0))
    return _call(body, name, (L // tm,), [row, vec, vec, vec], row, _sds((L, D), BF))(x, nw, sh, sc)


def norm_mod_bwd(x, dh, dres, nw, sc, name):
    L, D = x.shape
    tm = _tile(L, 512)

    def body(x_ref, dh_ref, dres_ref, nw_ref, sc_ref, dx_ref, acc_ref):
        @pl.when(pl.program_id(0) == 0)
        def _():
            acc_ref[...] = jnp.zeros_like(acc_ref)

        xf = x_ref[...]
        dhf = dh_ref[...].astype(F32)
        r = lax.rsqrt(jnp.mean(xf * xf, axis=-1, keepdims=True) + EPS)
        xhat = xf * r
        nwv = nw_ref[...]
        dn = dhf * (1.0 + sc_ref[...])
        dxhat = dn * nwv
        proj = jnp.mean(dxhat * xhat, axis=-1, keepdims=True)
        dx_ref[...] = dres_ref[...] + r * (dxhat - xhat * proj)
        acc_ref[0:1, :] += jnp.sum(dhf, axis=0, keepdims=True)
        acc_ref[1:2, :] += jnp.sum(dhf * xhat * nwv, axis=0, keepdims=True)
        acc_ref[2:3, :] += jnp.sum(dn * xhat, axis=0, keepdims=True)

    row = pl.BlockSpec((tm, D), lambda i: (i, 0))
    vec = pl.BlockSpec((1, D), lambda i: (0, 0))
    acc = pl.BlockSpec((8, D), lambda i: (0, 0))
    return _call(body, name, (L // tm,), [row, row, row, vec, vec], (row, acc),
                 (_sds((L, D), F32), _sds((8, D), F32)))(x, dh, dres, nw, sc)


def final_loss(x, nw, target):
    L, D = x.shape
    tm = _tile(L, 512)

    def body(x_ref, nw_ref, t_ref, dx_ref, acc_ref):
        @pl.when(pl.program_id(0) == 0)
        def _():
            acc_ref[...] = jnp.zeros_like(acc_ref)

        xf = x_ref[...]
        r = lax.rsqrt(jnp.mean(xf * xf, axis=-1, keepdims=True) + EPS)
        xhat = xf * r
        nwv = nw_ref[...]
        err = xhat * nwv - t_ref[...]
        dy = err * (1.0 / D)
        dxhat = dy * nwv
        proj = jnp.mean(dxhat * xhat, axis=-1, keepdims=True)
        dx_ref[...] = r * (dxhat - xhat * proj)
        acc_ref[0:1, :] += jnp.sum(dy * xhat, axis=0, keepdims=True)
        acc_ref[1:2, :] += jnp.sum(err * err, axis=0, keepdims=True)

    row = pl.BlockSpec((tm, D), lambda i: (i, 0))
    vec = pl.BlockSpec((1, D), lambda i: (0, 0))
    acc = pl.BlockSpec((8, D), lambda i: (0, 0))
    return _call(body, "final_loss", (L // tm,), [row, vec, row], (row, acc),
                 (_sds((L, D), F32), _sds((8, D), F32)))(x, nw, target)


def resid_gate_bwd(dxo, y, gate, coef, name):
    L, D = dxo.shape
    tm = _tile(L, 512)

    def body(dxo_ref, y_ref, g_ref, dy_ref, acc_ref):
        @pl.when(pl.program_id(0) == 0)
        def _():
            acc_ref[...] = jnp.zeros_like(acc_ref)

        d = dxo_ref[...]
        dy = coef * g_ref[...] * d
        dy_ref[...] = dy.astype(BF)
        acc_ref[0:1, :] += coef * jnp.sum(d * y_ref[...].astype(F32), axis=0, keepdims=True)
        acc_ref[1:2, :] += jnp.sum(dy, axis=0, keepdims=True)

    row = pl.BlockSpec((tm, D), lambda i: (i, 0))
    vec = pl.BlockSpec((1, D), lambda i: (0, 0))
    acc = pl.BlockSpec((8, D), lambda i: (0, 0))
    return _call(body, name, (L // tm,), [row, row, vec], (row, acc),
                 (_sds((L, D), BF), _sds((8, D), F32)))(dxo, y, gate)


def ffn_up(h, wt):
    L, D = h.shape
    F = wt.shape[0] // 2
    tm, tn = _tile(L, 2048), _tile(F, 256)
    nj = F // tn

    def body(h_ref, wg_ref, wu_ref, g_ref, u_ref, a_ref):
        hv = h_ref[...]
        g = _dot_nt(hv, wg_ref[...])
        u = _dot_nt(hv, wu_ref[...])
        g_ref[...] = g.astype(BF)
        u_ref[...] = u.astype(BF)
        a_ref[...] = (g * _sigmoid(g) * u).astype(BF)

    o = pl.BlockSpec((tm, tn), lambda i, n: (i, n))
    return _call(body, "ffn_up", (L // tm, nj),
                 [pl.BlockSpec((tm, D), lambda i, n: (i, 0)),
                  pl.BlockSpec((tn, D), lambda i, n: (n, 0)),
                  pl.BlockSpec((tn, D), lambda i, n: (n + nj, 0))],
                 (o, o, o), tuple(_sds((L, F), BF) for _ in range(3)))(h, wt, wt)


def ffn_down_dgrad(dy, wd, g, u):
    L, D = dy.shape
    F = wd.shape[0]
    tm, tn = _tile(L, 2048), _tile(F, 256)

    def body(dy_ref, w_ref, g_ref, u_ref, dg_ref, du_ref):
        da = _dot_nt(dy_ref[...], w_ref[...])
        gv = g_ref[...].astype(F32)
        uv = u_ref[...].astype(F32)
        s = _sigmoid(gv)
        dg_ref[...] = (da * uv * s * (1.0 + gv * (1.0 - s))).astype(BF)
        du_ref[...] = (da * gv * s).astype(BF)

    o = pl.BlockSpec((tm, tn), lambda i, n: (i, n))
    return _call(body, "ffn_down_dgrad", (L // tm, F // tn),
                 [pl.BlockSpec((tm, D), lambda i, n: (i, 0)), pl.BlockSpec((tn, D), lambda i, n: (n, 0)), o, o],
                 (o, o), (_sds((L, F), BF), _sds((L, F), BF)))(dy, wd, g, u)


def ffn_up_wgrad(dg, du, h):
    L, F = dg.shape
    D = h.shape[1]
    tm = _tile(F, 256)
    nblk = F // tm

    def half(d, off, prev):
        def body(d_ref, h_ref, *rest):
            rest[-1][...] = _dot_tn(d_ref[...], h_ref[...]).astype(BF)

        in_specs = [pl.BlockSpec((L, tm), lambda i: (0, i)), pl.BlockSpec((L, D), lambda i: (0, 0))]
        args = [d, h]
        aliases = {}
        if prev is not None:
            in_specs.append(pl.BlockSpec(memory_space=pl.ANY))
            args.append(prev)
            aliases = {2: 0}
        return pl.pallas_call(
            body, name="ffn_up_wgrad", grid=(nblk,), in_specs=in_specs,
            out_specs=pl.BlockSpec((tm, D), lambda i: (i + off * nblk, 0)),
            out_shape=_sds((2 * F, D), BF), input_output_aliases=aliases,
            compiler_params=pltpu.CompilerParams(vmem_limit_bytes=VMEM_LIMIT))(*args)

    return half(du, 1, half(dg, 0, None))


def ffn_up_dgrad(dg, du, wt):
    L, F = dg.shape
    D = wt.shape[1]
    tm, tn = _tile(L, 1024), _tile(D, 512)

    def body(dg_ref, du_ref, wg_ref, wu_ref, o_ref):
        o_ref[...] = _dot_nn(dg_ref[...], wg_ref[...]) + _dot_nn(du_ref[...], wu_ref[...])

    a = pl.BlockSpec((tm, F), lambda i, n: (i, 0))
    return _call(body, "ffn_up_dgrad", (L // tm, D // tn),
                 [a, a, pl.BlockSpec((F, tn), lambda i, n: (0, n)), pl.BlockSpec((F, tn), lambda i, n: (1, n))],
                 pl.BlockSpec((tm, tn), lambda i, n: (i, n)), _sds((L, D), F32))(dg, du, wt, wt)


_HALO = 8


def _shift_down(cur, prev8, k):
    out = pltpu.roll(cur, k, 0)
    rows8 = lax.broadcasted_iota(jnp.int32, prev8.shape, 0)
    head = jnp.where(rows8 < k, pltpu.roll(prev8, k, 0), out[0:_HALO])
    if cur.shape[0] == _HALO:
        return head
    return jnp.concatenate([head, out[_HALO:]], axis=0)


def _shift_up(cur, next8, k):
    n = cur.shape[0]
    out = pltpu.roll(cur, n - k, 0)
    rows8 = lax.broadcasted_iota(jnp.int32, next8.shape, 0)
    tail = jnp.where(rows8 >= _HALO - k, pltpu.roll(next8, _HALO - k, 0), out[n - _HALO:])
    return jnp.concatenate([out[:n - _HALO], tail], axis=0)


def _conv_pre(cur, prev8, w_ref, b_ref):
    shifted = [_shift_down(cur, prev8, k) for k in range(1, CONV_WIDTH)]
    s = cur * w_ref[CONV_WIDTH - 1:CONV_WIDTH, :] + b_ref[...]
    for k in range(1, CONV_WIDTH):
        s = s + shifted[k - 1] * w_ref[CONV_WIDTH - 1 - k:CONV_WIDTH - k, :]
    return s, shifted


def _silu_grad(s):
    sg = _sigmoid(s)
    return sg * (1.0 + s * (1.0 - sg))


_XBC_COL0 = D_INNER // 512


def conv_fwd(zx, w, b):
    L = zx.shape[0]
    tm, tc = _tile(L, 256), 512
    hb = tm // _HALO

    def body(cur_ref, prev_ref, w_ref, b_ref, o_ref):
        prev8 = jnp.where(pl.program_id(1) > 0, prev_ref[...], 0.0)
        s, _ = _conv_pre(cur_ref[...], prev8, w_ref, b_ref)
        o_ref[...] = s * _sigmoid(s)

    return _call(body, "conv_fwd", (CONV_DIM // tc, L // tm),
                 [pl.BlockSpec((tm, tc), lambda j, i: (i, _XBC_COL0 + j)),
                  pl.BlockSpec((_HALO, tc), lambda j, i: (jnp.maximum(i * hb - 1, 0), _XBC_COL0 + j)),
                  pl.BlockSpec((CONV_WIDTH, tc), lambda j, i: (0, j)),
                  pl.BlockSpec((1, tc), lambda j, i: (0, j))],
                 pl.BlockSpec((tm, tc), lambda j, i: (i, j)), _sds((L, CONV_DIM), F32))(zx, zx, w, b)


def conv_bwd(dxc, zx, w, b):
    L = zx.shape[0]
    tm, tc = _tile(L, 256), 512
    nblk = L // tm
    hb = tm // _HALO

    def body(d_ref, dn_ref, cur_ref, prev_ref, next_ref, w_ref, b_ref, du_ref, acc_ref):
        i = pl.program_id(1)

        @pl.when(i == 0)
        def _():
            acc_ref[...] = jnp.zeros_like(acc_ref)

        cur = cur_ref[...]
        prev8 = jnp.where(i > 0, prev_ref[...], 0.0)
        s, shifted = _conv_pre(cur, prev8, w_ref, b_ref)
        ds_c = d_ref[...] * _silu_grad(s)
        s_n, _ = _conv_pre(next_ref[...], cur[tm - _HALO:], w_ref, b_ref)
        ds_n = jnp.where(i < nblk - 1, dn_ref[...] * _silu_grad(s_n), 0.0)
        du = ds_c * w_ref[CONV_WIDTH - 1:CONV_WIDTH, :]
        acc_ref[CONV_WIDTH - 1:CONV_WIDTH, :] += jnp.sum(ds_c * cur, axis=0, keepdims=True)
        for k in range(1, CONV_WIDTH):
            du = du + _shift_up(ds_c, ds_n, k) * w_ref[CONV_WIDTH - 1 - k:CONV_WIDTH - k, :]
            acc_ref[CONV_WIDTH - 1 - k:CONV_WIDTH - k, :] += jnp.sum(ds_c * shifted[k - 1], axis=0, keepdims=True)
        acc_ref[CONV_WIDTH:CONV_WIDTH + 1, :] += jnp.sum(ds_c, axis=0, keepdims=True)
        du_ref[...] = du.astype(BF)

    nxt = lambda j, i: (jnp.minimum((i + 1) * hb, L // _HALO - 1), j)
    return _call(body, "conv_bwd", (CONV_DIM // tc, nblk),
                 [pl.BlockSpec((tm, tc), lambda j, i: (i, j)),
                  pl.BlockSpec((_HALO, tc), nxt),
                  pl.BlockSpec((tm, tc), lambda j, i: (i, _XBC_COL0 + j)),
                  pl.BlockSpec((_HALO, tc), lambda j, i: (jnp.maximum(i * hb - 1, 0), _XBC_COL0 + j)),
                  pl.BlockSpec((_HALO, tc), lambda j, i: (jnp.minimum((i + 1) * hb, L // _HALO - 1), _XBC_COL0 + j)),
                  pl.BlockSpec((CONV_WIDTH, tc), lambda j, i: (0, j)),
                  pl.BlockSpec((1, tc), lambda j, i: (0, j))],
                 (pl.BlockSpec((tm, tc), lambda j, i: (i, j)), pl.BlockSpec((8, tc), lambda j, i: (0, j))),
                 (_sds((L, CONV_DIM), BF), _sds((8, CONV_DIM), F32)))(dxc, dxc, zx, zx, zx, w, b)


_DT_COL = (D_INNER + CONV_DIM) // 128


def dt_prep(zx, bias_pad, alog_pad):
    L = zx.shape[0]

    def body(raw_ref, b_ref, al_ref, dt_ref, acs_ref):
        v = raw_ref[...] + b_ref[...]
        dt = jnp.maximum(v, 0.0) + jnp.log(1.0 + jnp.exp(-jnp.abs(v)))
        dt_ref[...] = dt
        acs = dt * (-jnp.exp(al_ref[...]))
        rows = lax.broadcasted_iota(jnp.int32, acs.shape, 0)
        s = 1
        while s < CHUNK:
            acs = acs + jnp.where(rows >= s, pltpu.roll(acs, s, 0), 0.0)
            s *= 2
        acs_ref[...] = acs

    blk = pl.BlockSpec((CHUNK, 128), lambda i: (i, 0))
    vec = pl.BlockSpec((1, 128), lambda i: (0, 0))
    return _call(body, "dt_prep", (L // CHUNK,),
                 [pl.BlockSpec((CHUNK, 128), lambda i: (i, _DT_COL)), vec, vec], (blk, blk),
                 (_sds((L, 128), F32), _sds((L, 128), F32)))(zx, bias_pad, alog_pad)


def dt_bwd(ddt, da, dt, zx, bias_pad, alog_pad):
    L = zx.shape[0]
    tm = _tile(L, 512)

    def body(ddt_ref, da_ref, dt_ref, raw_ref, b_ref, al_ref, o_ref, acc_ref):
        @pl.when(pl.program_id(0) == 0)
        def _():
            acc_ref[...] = jnp.zeros_like(acc_ref)

        A = -jnp.exp(al_ref[...])
        dav = da_ref[...]
        dd = ddt_ref[...] + dav * A
        draw = dd * _sigmoid(raw_ref[...] + b_ref[...])
        o_ref[...] = draw.astype(BF)
        acc_ref[0:1, :] += jnp.sum(draw, axis=0, keepdims=True)
        acc_ref[1:2, :] += jnp.sum(dav * dt_ref[...], axis=0, keepdims=True) * A

    blk = pl.BlockSpec((tm, 128), lambda i: (i, 0))
    vec = pl.BlockSpec((1, 128), lambda i: (0, 0))
    return _call(body, "dt_bwd", (L // tm,),
                 [blk, blk, blk, pl.BlockSpec((tm, 128), lambda i: (i, _DT_COL)), vec, vec],
                 (blk, pl.BlockSpec((8, 128), lambda i: (0, 0))),
                 (_sds((L, 128), BF), _sds((8, 128), F32)))(ddt, da, dt, zx, bias_pad, alog_pad)


_HPG = SSM_HEADS // SSM_GROUPS
_GW = _HPG * SSM_HEADDIM
_B_COL0 = D_INNER // SSM_STATE
_C_COL0 = (D_INNER + SSM_GROUPS * SSM_STATE) // SSM_STATE


def _ssd_head(x, dtc, ac, ar, r, causal):
    xh = x[:, SSM_HEADDIM * r:SSM_HEADDIM * (r + 1)]
    acol = ac[:, r:r + 1]
    arow = ar[r:r + 1, :]
    alast = ar[r:r + 1, CHUNK - 1:CHUNK]
    lm = jnp.exp(jnp.where(causal, acol - arow, NEG))
    return xh, xh * dtc[:, r:r + 1], acol, alast, lm


def ssd_fwd(xc, dt_g, acs_g, acsT_g, d_exp):
    L = xc.shape[0]
    nc = L // CHUNK

    def body(x_ref, b_ref, c_ref, dt_ref, ac_ref, ar_ref, d_ref, y_ref, pst_ref, st_ref):
        @pl.when(pl.program_id(1) == 0)
        def _():
            st_ref[...] = jnp.zeros_like(st_ref)

        x, Bm, Cm = x_ref[...], b_ref[...], c_ref[...]
        dtc, ac, ar = dt_ref[...], ac_ref[...], ar_ref[...]
        causal = lax.broadcasted_iota(jnp.int32, (CHUNK, CHUNK), 0) >= lax.broadcasted_iota(jnp.int32, (CHUNK, CHUNK), 1)
        CB = _dot_nt(Cm, Bm)
        for r in range(_HPG):
            xh, xd, acol, alast, lm = _ssd_head(x, dtc, ac, ar, r, causal)
            P = st_ref[r]
            y = _dot_nn(CB * lm, xd) + jnp.exp(acol) * _dot_nt(Cm, P)
            y_ref[:, SSM_HEADDIM * r:SSM_HEADDIM * (r + 1)] = y + d_ref[:, SSM_HEADDIM * r:SSM_HEADDIM * (r + 1)] * xh
            pst_ref[r] = P
            st_ref[r] = P * jnp.exp(alast) + _dot_tn(xd * jnp.exp(alast - acol), Bm)

    return _call(
        body, "ssd_fwd", (SSM_GROUPS, nc),
        [pl.BlockSpec((CHUNK, _GW), lambda g, c: (c, g)),
         pl.BlockSpec((CHUNK, SSM_STATE), lambda g, c: (c, _B_COL0 + g)),
         pl.BlockSpec((CHUNK, SSM_STATE), lambda g, c: (c, _C_COL0 + g)),
         pl.BlockSpec((None, CHUNK, _HPG), lambda g, c: (g, c, 0)),
         pl.BlockSpec((None, CHUNK, _HPG), lambda g, c: (g, c, 0)),
         pl.BlockSpec((None, _HPG, CHUNK), lambda g, c: (g, 0, c)),
         pl.BlockSpec((None, 1, _GW), lambda g, c: (g, 0, 0))],
        (pl.BlockSpec((CHUNK, _GW), lambda g, c: (c, g)),
         pl.BlockSpec((None, None, _HPG, SSM_HEADDIM, SSM_STATE), lambda g, c: (c, g, 0, 0, 0))),
        (_sds((L, D_INNER), F32), _sds((nc, SSM_GROUPS, _HPG, SSM_HEADDIM, SSM_STATE), F32)),
        scratch=[pltpu.VMEM((_HPG, SSM_HEADDIM, SSM_STATE), F32)],
    )(xc, xc, xc, dt_g, acs_g, acsT_g, d_exp)


def ssd_bwd(dy, xc, dt_g, acs_g, acsT_g, pst, d_exp):
    L = xc.shape[0]
    nc = L // CHUNK

    def body(dy_ref, x_ref, b_ref, c_ref, dt_ref, ac_ref, ar_ref, pst_ref, d_ref,
             dx_ref, db_ref, dc_ref, ddt_ref, da_ref, dd_ref, dp_ref):
        @pl.when(pl.program_id(1) == 0)
        def _():
            dp_ref[...] = jnp.zeros_like(dp_ref)
            dd_ref[...] = jnp.zeros_like(dd_ref)

        dyv, x, Bm, Cm = dy_ref[...], x_ref[...], b_ref[...], c_ref[...]
        dtc, ac, ar = dt_ref[...], ac_ref[...], ar_ref[...]
        ri = lax.broadcasted_iota(jnp.int32, (CHUNK, CHUNK), 0)
        ci = lax.broadcasted_iota(jnp.int32, (CHUNK, CHUNK), 1)
        causal = ri >= ci
        lane4 = lax.broadcasted_iota(jnp.int32, (CHUNK, _HPG), 1)
        CB = _dot_nt(Cm, Bm)
        dB = jnp.zeros((CHUNK, SSM_STATE), F32)
        dC = jnp.zeros((CHUNK, SSM_STATE), F32)
        dCB = jnp.zeros((CHUNK, CHUNK), F32)
        ddt_blk = jnp.zeros((CHUNK, _HPG), F32)
        da_blk = jnp.zeros((CHUNK, _HPG), F32)
        for r in range(_HPG):
            sl = slice(SSM_HEADDIM * r, SSM_HEADDIM * (r + 1))
            xh, xd, acol, alast, lm = _ssd_head(x, dtc, ac, ar, r, causal)
            dyh = dyv[:, sl]
            P = pst_ref[r]
            dPn = dp_ref[r]
            eA = jnp.exp(acol)
            cd = jnp.exp(alast)
            dte = jnp.exp(alast - acol)
            G = CB * lm
            Z = _dot_nt(Cm, P)
            dZ = eA * dyh
            dC = dC + _dot_nn(dZ, P)
            dp_ref[r] = dPn * cd + _dot_tn(dZ, Cm)
            dA_col = jnp.sum(dZ * Z, axis=1, keepdims=True)
            BdS = _dot_nt(Bm, dPn)
            dxd = dte * BdS
            dB = dB + dte * _dot_nn(xd, dPn)
            t = jnp.sum(xd * BdS, axis=1, keepdims=True) * dte
            dA_col = dA_col - t
            dA_last = jnp.sum(t, axis=0, keepdims=True) + jnp.sum(
                jnp.sum(dPn * P, axis=1, keepdims=True), axis=0, keepdims=True) * cd
            dG = _dot_nt(dyh, xd)
            dxd = dxd + _dot_tn(G, dyh)
            dCB = dCB + dG * lm
            W = dG * G
            dA_col = dA_col + jnp.sum(W, axis=1, keepdims=True)
            dA_row = jnp.sum(jnp.where(ri == ci, dA_col, 0.0), axis=0, keepdims=True) - jnp.sum(W, axis=0, keepdims=True)
            da_col = jnp.sum(jnp.where(ci >= ri, dA_row, 0.0), axis=1, keepdims=True) + dA_last
            da_blk = jnp.where(lane4 == r, da_col, da_blk)
            ddt_blk = jnp.where(lane4 == r, jnp.sum(dxd * xh, axis=1, keepdims=True), ddt_blk)
            dx_ref[:, sl] = dxd * dtc[:, r:r + 1] + d_ref[:, sl] * dyh
        dc_ref[...] = dC + _dot_nn(dCB, Bm)
        db_ref[...] = dB + _dot_tn(dCB, Cm)
        ddt_ref[...] = ddt_blk
        da_ref[...] = da_blk
        dd_ref[...] += jnp.sum(dyv * x, axis=0, keepdims=True)

    rc = lambda g, c: (nc - 1 - c, g)
    small = pl.BlockSpec((None, CHUNK, _HPG), lambda g, c: (g, nc - 1 - c, 0))
    return _call(
        body, "ssd_bwd", (SSM_GROUPS, nc),
        [pl.BlockSpec((CHUNK, _GW), rc),
         pl.BlockSpec((CHUNK, _GW), rc),
         pl.BlockSpec((CHUNK, SSM_STATE), lambda g, c: (nc - 1 - c, _B_COL0 + g)),
         pl.BlockSpec((CHUNK, SSM_STATE), lambda g, c: (nc - 1 - c, _C_COL0 + g)),
         small, small,
         pl.BlockSpec((None, _HPG, CHUNK), lambda g, c: (g, 0, nc - 1 - c)),
         pl.BlockSpec((None, None, _HPG, SSM_HEADDIM, SSM_STATE), lambda g, c: (nc - 1 - c, g, 0, 0, 0)),
         pl.BlockSpec((None, 1, _GW), lambda g, c: (g, 0, 0))],
        (pl.BlockSpec((CHUNK, _GW), rc),
         pl.BlockSpec((CHUNK, SSM_STATE), rc),
         pl.BlockSpec((CHUNK, SSM_STATE), rc),
         small, small,
         pl.BlockSpec((None, 1, _GW), lambda g, c: (g, 0, 0))),
        (_sds((L, D_INNER), F32), _sds((L, SSM_GROUPS * SSM_STATE), F32), _sds((L, SSM_GROUPS * SSM_STATE), F32),
         _sds((SSM_GROUPS, L, _HPG), F32), _sds((SSM_GROUPS, L, _HPG), F32), _sds((SSM_GROUPS, 1, _GW), F32)),
        scratch=[pltpu.VMEM((_HPG, SSM_HEADDIM, SSM_STATE), F32)],
    )(dy, xc, xc, xc, dt_g, acs_g, acsT_g, pst, d_exp)


_NGW = D_INNER // SSM_GROUPS


def gate_norm(y, zx, nw):
    L = y.shape[0]
    tm = _tile(L, 256)

    def body(y_ref, z_ref, nw_ref, o_ref):
        for g in range(SSM_GROUPS):
            sl = slice(_NGW * g, _NGW * (g + 1))
            z = z_ref[:, sl]
            y2 = y_ref[:, sl] * (z * _sigmoid(z))
            r = lax.rsqrt(jnp.mean(y2 * y2, axis=-1, keepdims=True) + EPS)
            o_ref[:, sl] = (y2 * r * nw_ref[:, sl]).astype(BF)

    row = pl.BlockSpec((tm, D_INNER), lambda i: (i, 0))
    return _call(body, "gate_norm", (L // tm,), [row, row, pl.BlockSpec((1, D_INNER), lambda i: (0, 0))],
                 row, _sds((L, D_INNER), BF))(y, zx, nw)


def gate_norm_bwd(dyn, y, zx, nw):
    L = y.shape[0]
    tm = _tile(L, 256)

    def body(d_ref, y_ref, z_ref, nw_ref, dy_ref, dz_ref, acc_ref):
        @pl.when(pl.program_id(0) == 0)
        def _():
            acc_ref[...] = jnp.zeros_like(acc_ref)

        for g in range(SSM_GROUPS):
            sl = slice(_NGW * g, _NGW * (g + 1))
            z = z_ref[:, sl]
            yv = y_ref[:, sl]
            sg = _sigmoid(z)
            sz = z * sg
            y2 = yv * sz
            r = lax.rsqrt(jnp.mean(y2 * y2, axis=-1, keepdims=True) + EPS)
            yh = y2 * r
            d = d_ref[:, sl]
            dn = d * nw_ref[:, sl]
            dy2 = r * (dn - yh * jnp.mean(dn * yh, axis=-1, keepdims=True))
            dy_ref[:, sl] = dy2 * sz
            dz_ref[:, sl] = (dy2 * yv * sg * (1.0 + z * (1.0 - sg))).astype(BF)
            acc_ref[0:1, sl] += jnp.sum(d * yh, axis=0, keepdims=True)

    row = pl.BlockSpec((tm, D_INNER), lambda i: (i, 0))
    return _call(body, "gate_norm_bwd", (L // tm,), [row, row, row, pl.BlockSpec((1, D_INNER), lambda i: (0, 0))],
                 (row, row, pl.BlockSpec((8, D_INNER), lambda i: (0, 0))),
                 (_sds((L, D_INNER), F32), _sds((L, D_INNER), BF), _sds((8, D_INNER), F32)))(dyn, y, zx, nw)


_SCALE = HEAD_DIM ** -0.5
_REP = ATT_HEADS // KV_HEADS
_V_OFF = KV_HEADS * HEAD_DIM


def _stack_heads(ref, k):
    return jnp.concatenate([ref[:, HEAD_DIM * (k * _REP + r):HEAD_DIM * (k * _REP + r + 1)] for r in range(_REP)],
                           axis=0)


def _stack_sinks(s_ref, k):
    return jnp.concatenate([jnp.broadcast_to(s_ref[:, k * _REP + r:k * _REP + r + 1], (WINDOW, 1))
                            for r in range(_REP)], axis=0)


def _attn_probs(q4, kp, kc, sink, first):
    shape = (_REP * WINDOW, WINDOW)
    rows = jnp.bitwise_and(lax.broadcasted_iota(jnp.int32, shape, 0), WINDOW - 1)
    cols = lax.broadcasted_iota(jnp.int32, shape, 1)
    sp = jnp.where(jnp.logical_and(cols > rows, jnp.logical_not(first)), _dot_nt(q4, kp) * _SCALE, NEG)
    sc = jnp.where(cols <= rows, _dot_nt(q4, kc) * _SCALE, NEG)
    m = jnp.maximum(jnp.maximum(jnp.max(sp, axis=1, keepdims=True), jnp.max(sc, axis=1, keepdims=True)), sink)
    pp = jnp.exp(sp - m)
    pc = jnp.exp(sc - m)
    ps = jnp.exp(sink - m)
    inv = 1.0 / (jnp.sum(pp, axis=1, keepdims=True) + jnp.sum(pc, axis=1, keepdims=True) + ps)
    return pp * inv, pc * inv, ps * inv


def attn_fwd(q, kv, sinks_pad):
    L = q.shape[0]
    nb = L // WINDOW

    def body(q_ref, kc_ref, kp_ref, s_ref, o_ref):
        first = pl.program_id(0) == 0
        for k in range(KV_HEADS):
            ks = slice(HEAD_DIM * k, HEAD_DIM * (k + 1))
            vs = slice(_V_OFF + HEAD_DIM * k, _V_OFF + HEAD_DIM * (k + 1))
            pp, pc, _ = _attn_probs(_stack_heads(q_ref, k), kp_ref[:, ks], kc_ref[:, ks], _stack_sinks(s_ref, k), first)
            o4 = _dot_nn(pp, kp_ref[:, vs]) + _dot_nn(pc, kc_ref[:, vs])
            for r in range(_REP):
                h = k * _REP + r
                o_ref[:, HEAD_DIM * h:HEAD_DIM * (h + 1)] = o4[WINDOW * r:WINDOW * (r + 1)].astype(BF)

    qspec = pl.BlockSpec((WINDOW, D_MODEL), lambda i: (i, 0))
    return _call(body, "attn_fwd", (nb,),
                 [qspec, pl.BlockSpec((WINDOW, KV_DIM), lambda i: (i, 0)),
                  pl.BlockSpec((WINDOW, KV_DIM), lambda i: (jnp.maximum(i - 1, 0), 0)),
                  pl.BlockSpec((1, 128), lambda i: (0, 0))],
                 qspec, _sds((L, D_MODEL), BF))(q, kv, kv, sinks_pad)


def attn_bwd(q, kv, do, sinks_pad):
    L = q.shape[0]
    nb = L // WINDOW

    def body(q_ref, kc_ref, kp_ref, do_ref, s_ref, dq_ref, dc_ref, dp_ref, acc_ref):
        first = pl.program_id(0) == 0

        @pl.when(first)
        def _():
            acc_ref[...] = jnp.zeros_like(acc_ref)

        lane = lax.broadcasted_iota(jnp.int32, (1, 128), 1)
        dsink = jnp.zeros((1, 128), F32)
        for k in range(KV_HEADS):
            ks = slice(HEAD_DIM * k, HEAD_DIM * (k + 1))
            vs = slice(_V_OFF + HEAD_DIM * k, _V_OFF + HEAD_DIM * (k + 1))
            kp, kc, vp, vc = kp_ref[:, ks], kc_ref[:, ks], kp_ref[:, vs], kc_ref[:, vs]
            q4 = _stack_heads(q_ref, k)
            do4 = _stack_heads(do_ref, k)
            pp, pc, ps = _attn_probs(q4, kp, kc, _stack_sinks(s_ref, k), first)
            dpp = _dot_nt(do4, vp)
            dpc = _dot_nt(do4, vc)
            delta = jnp.sum(pp * dpp, axis=1, keepdims=True) + jnp.sum(pc * dpc, axis=1, keepdims=True)
            dsp = pp * (dpp - delta) * _SCALE
            dsc = pc * (dpc - delta) * _SCALE
            dq4 = _dot_nn(dsp, kp) + _dot_nn(dsc, kc)
            psd = ps * delta
            for r in range(_REP):
                h = k * _REP + r
                rs = slice(WINDOW * r, WINDOW * (r + 1))
                dq_ref[:, HEAD_DIM * h:HEAD_DIM * (h + 1)] = dq4[rs]
                dsink = dsink + jnp.where(lane == h, -jnp.sum(psd[rs], axis=0, keepdims=True), 0.0)
            dp_ref[:, ks] = _dot_tn(dsp, q4)
            dc_ref[:, ks] = _dot_tn(dsc, q4)
            dp_ref[:, vs] = _dot_tn(pp, do4)
            dc_ref[:, vs] = _dot_tn(pc, do4)
        acc_ref[0:1, :] += jnp.sum(dq_ref[...], axis=0, keepdims=True)
        acc_ref[1:2, 0:128] += dsink

    qspec = pl.BlockSpec((WINDOW, D_MODEL), lambda i: (i, 0))
    kspec = pl.BlockSpec((WINDOW, KV_DIM), lambda i: (i, 0))
    return _call(body, "attn_bwd", (nb,),
                 [qspec, kspec, pl.BlockSpec((WINDOW, KV_DIM), lambda i: (jnp.maximum(i - 1, 0), 0)), qspec,
                  pl.BlockSpec((1, 128), lambda i: (0, 0))],
                 (qspec, kspec, kspec, pl.BlockSpec((8, D_MODEL), lambda i: (0, 0))),
                 (_sds((L, D_MODEL), F32), _sds((L, KV_DIM), F32), _sds((L, KV_DIM), F32), _sds((8, D_MODEL), F32)),
                 )(q, kv, kv, do, sinks_pad)


def kv_grad_combine(parts):
    L = parts[0][0].shape[0]
    nb = L // WINDOW
    n = len(parts)

    def body(*refs):
        i = pl.program_id(0)
        o_ref, acc_ref = refs[2 * n], refs[2 * n + 1]

        @pl.when(i == 0)
        def _():
            acc_ref[...] = jnp.zeros_like(acc_ref)

        tot = refs[0][...]
        nxt = refs[1][...]
        for a in range(1, n):
            tot = tot + refs[2 * a][...]
            nxt = nxt + refs[2 * a + 1][...]
        tot = tot + jnp.where(i < nb - 1, nxt, 0.0)
        o_ref[...] = tot
        acc_ref[0:1, :] += jnp.sum(tot, axis=0, keepdims=True)

    cur = pl.BlockSpec((WINDOW, KV_DIM), lambda i: (i, 0))
    nxt = pl.BlockSpec((WINDOW, KV_DIM), lambda i: (jnp.minimum(i + 1, nb - 1), 0))
    args = [t for p in parts for t in p]
    return _call(body, "kv_grad_combine", (nb,), [cur, nxt] * n,
                 (cur, pl.BlockSpec((8, KV_DIM), lambda i: (0, 0))),
                 (_sds((L, KV_DIM), F32), _sds((8, KV_DIM), F32)))(*args)


def mod_fwd(c_all, w, b, name):
    n, _, C = w.shape

    def body(c_ref, w_ref, b_ref, o_ref, ca_ref):
        cv = c_ref[...]
        ca = cv * _sigmoid(cv)
        ca_ref[...] = ca
        o_ref[...] = _dot(ca, w_ref[...], ((1,), (0,))) + b_ref[...]

    return _call(body, name, (n,),
                 [pl.BlockSpec((N_DEV, D_MODEL), lambda i: (0, 0)),
                  pl.BlockSpec((None, D_MODEL, C), lambda i: (i, 0, 0)),
                  pl.BlockSpec((None, 1, C), lambda i: (i, 0, 0))],
                 (pl.BlockSpec((None, N_DEV, C), lambda i: (i, 0, 0)), pl.BlockSpec((N_DEV, D_MODEL), lambda i: (0, 0))),
                 (_sds((n, N_DEV, C), F32), _sds((N_DEV, D_MODEL), F32)))(c_all, w, b)


def mod_wgrad(c_act_t, dmod, name):
    n, _, C = dmod.shape
    tr = 256

    def body(ct_ref, d_ref, o_ref):
        acc = ct_ref[:, 0:1] * d_ref[0:1, :]
        for bidx in range(1, N_DEV):
            acc = acc + ct_ref[:, bidx:bidx + 1] * d_ref[bidx:bidx + 1, :]
        o_ref[...] = acc

    return _call(body, name, (n, D_MODEL // tr),
                 [pl.BlockSpec((tr, N_DEV), lambda i, j: (j, 0)),
                  pl.BlockSpec((None, N_DEV, C), lambda i, j: (i, 0, 0))],
                 pl.BlockSpec((None, tr, C), lambda i, j: (i, j, 0)), _sds((n, D_MODEL, C), F32))(c_act_t, dmod)


def _my_pos():
    return lax.axis_index("x"), lax.axis_index("y"), lax.axis_index("c")


def small_all_gather(v):
    m_per, n = v.shape

    def body(x_ref, out_ref, send_sems, recv_sems, local_sem):
        x, y, c = _my_pos()
        me, sibling = (x, y, c), (x, y, 1 - c)
        chips = [(1 - x, y), (x, 1 - y), (1 - x, 1 - y)]

        def rows(px, py, pc):
            return out_ref.at[pl.ds((4 * px + 2 * py + pc) * m_per, m_per), :]

        def copy(k, block, to, src=None):
            return pltpu.make_async_remote_copy(
                src_ref=rows(*block) if src is None else src, dst_ref=rows(*block),
                send_sem=send_sems.at[k], recv_sem=recv_sems.at[k], device_id=to, device_id_type=MESH)

        mine = pltpu.make_async_copy(x_ref, rows(*me), local_sem)
        mine.start()
        first = [copy(0, me, sibling, src=x_ref)]
        first += [copy(1 + j, me, (*chip, c), src=x_ref) for j, chip in enumerate(chips)]
        for cp in first:
            cp.start()
        passed = [copy(4 + j, (*chip, c), sibling) for j, chip in enumerate(chips)]
        for j, chip in enumerate(chips):
            copy(1 + j, (*chip, c), me).wait_recv()
            passed[j].start()
        copy(0, sibling, me).wait_recv()
        for j, chip in enumerate(chips):
            copy(4 + j, (*chip, 1 - c), me).wait_recv()
        for cp in first + passed:
            cp.wait_send()
        mine.wait()

    return pl.pallas_call(
        body, name="small_all_gather",
        out_shape=_sds((N_DEV * m_per, n), v.dtype),
        in_specs=[pl.BlockSpec(memory_space=pltpu.VMEM)],
        out_specs=pl.BlockSpec(memory_space=pltpu.VMEM),
        scratch_shapes=[pltpu.SemaphoreType.DMA((7,)), pltpu.SemaphoreType.DMA((7,)), pltpu.SemaphoreType.DMA],
        compiler_params=pltpu.CompilerParams(vmem_limit_bytes=VMEM_LIMIT),
    )(v)


def big_all_gather(arrs):
    n = len(arrs)

    def body(*refs):
        ins, outs = refs[:n], refs[n:2 * n]
        send_sems, recv_sems, local_sems = refs[2 * n], refs[2 * n + 1], refs[2 * n + 2]
        x, y, c = _my_pos()
        me, sibling = (x, y, c), (x, y, 1 - c)
        chips = [(1 - x, y), (x, 1 - y), (1 - x, 1 - y)]

        def slot(a, px, py, pc):
            return outs[a].at[4 * px + 2 * py + pc]

        def copy(a, k, block, to, src=None):
            return pltpu.make_async_remote_copy(
                src_ref=slot(a, *block) if src is None else src, dst_ref=slot(a, *block),
                send_sem=send_sems.at[7 * a + k], recv_sem=recv_sems.at[7 * a + k], device_id=to, device_id_type=MESH)

        mine = [pltpu.make_async_copy(ins[a], slot(a, *me), local_sems.at[a]) for a in range(n)]
        for cp in mine:
            cp.start()
        first = []
        for a in range(n):
            first.append(copy(a, 0, me, sibling, src=ins[a]))
            first += [copy(a, 1 + j, me, (*chip, c), src=ins[a]) for j, chip in enumerate(chips)]
        for cp in first:
            cp.start()
        passed = []
        for a in range(n):
            for j, chip in enumerate(chips):
                copy(a, 1 + j, (*chip, c), me).wait_recv()
                fwd = copy(a, 4 + j, (*chip, c), sibling)
                fwd.start()
                passed.append(fwd)
        for a in range(n):
            copy(a, 0, sibling, me).wait_recv()
            for j, chip in enumerate(chips):
                copy(a, 4 + j, (*chip, 1 - c), me).wait_recv()
        for cp in first + passed:
            cp.wait_send()
        for cp in mine:
            cp.wait()

    hbm = pl.BlockSpec(memory_space=pltpu.HBM)
    return pl.pallas_call(
        body, name="big_all_gather",
        out_shape=[_sds((N_DEV,) + a.shape, a.dtype) for a in arrs],
        in_specs=[hbm] * n, out_specs=[hbm] * n,
        scratch_shapes=[pltpu.SemaphoreType.DMA((7 * n,)), pltpu.SemaphoreType.DMA((7 * n,)),
                        pltpu.SemaphoreType.DMA((n,))],
    )(*arrs)


_FLIPS =[(fx, fy, fc) for fx in (0, 1) for fy in (0, 1) for fc in (0, 1)][1:]
_HBM = pl.BlockSpec(memory_space=pltpu.HBM)
_SEM = pl.BlockSpec(memory_space=pltpu.SEMAPHORE)
_EFFECT = pltpu.SideEffectType.DATAFLOW_SIDE_EFFECTING


def _flip(x, y, c, f):
    return (1 - x if f[0] else x), (1 - y if f[1] else y), (1 - c if f[2] else c)


def _xfer_copies(srcs, lands, send_sems, recv_sems, scatter):
    x, y, c = _my_pos()
    me = 4 * x + 2 * y + c
    copies = []
    for a in range(len(srcs)):
        for k, f in enumerate(_FLIPS):
            px, py, pc = _flip(x, y, c, f)
            src = srcs[a].at[4 * px + 2 * py + pc] if scatter else srcs[a]
            copies.append(pltpu.make_async_remote_copy(
                src_ref=src, dst_ref=lands[a].at[me], send_sem=send_sems.at[7 * a + k],
                recv_sem=recv_sems.at[7 * a + k], device_id=(px, py, pc), device_id_type=MESH))
    return copies


def _own_copies(srcs, lands, local_sems, scatter):
    x, y, c = _my_pos()
    me = 4 * x + 2 * y + c
    return [pltpu.make_async_copy(srcs[a].at[me] if scatter else srcs[a], lands[a].at[me], local_sems.at[a])
            for a in range(len(srcs))]


def xfer_start(arrs, scatter, after, name):
    n = len(arrs)
    land_shapes = [a.shape if scatter else (N_DEV,) + a.shape for a in arrs]

    def body(*refs):
        srcs, lands = refs[:n], refs[n:2 * n]
        send_sems, recv_sems, local_sems = refs[2 * n + 1], refs[2 * n + 2], refs[2 * n + 3]
        token = refs[-1]
        for cp in _xfer_copies(srcs, lands, send_sems, recv_sems, scatter):
            cp.start()
        for cp in _own_copies(srcs, lands, local_sems, scatter):
            cp.start()
        token[...] = jnp.zeros_like(token)

    out = pl.pallas_call(
        body, name=name,
        out_shape=(pltpu.SemaphoreType.DMA((7 * n,)), pltpu.SemaphoreType.DMA((7 * n,)),
                   pltpu.SemaphoreType.DMA((n,)),
                   *[pltpu.HBM(a.shape, a.dtype) for a in arrs],
                   *[pltpu.HBM(s, a.dtype) for s, a in zip(land_shapes, arrs)],
                   _sds((8, 128), F32)),
        in_specs=[_HBM] * (2 * n) + [pl.BlockSpec(memory_space=pl.ANY)],
        out_specs=(_SEM, _SEM, _SEM, *([_HBM] * (2 * n)), pl.BlockSpec(memory_space=pltpu.VMEM)),
        input_output_aliases={i: 3 + i for i in range(2 * n)},
        compiler_params=pltpu.CompilerParams(has_side_effects=_EFFECT),
    )(*[pltpu.with_memory_space_constraint(a, pltpu.HBM) for a in arrs],
      *[pltpu.with_memory_space_constraint(lax.empty(s, a.dtype), pltpu.HBM) for s, a in zip(land_shapes, arrs)],
      after)
    return (out[0], out[1], out[2], list(out[3:3 + n]), list(out[3 + n:3 + 2 * n]), scatter), out[-1]


def xfer_wait(handle, after, name):
    send_sems, recv_sems, local_sems, srcs, lands, scatter = handle
    n = len(srcs)

    def body(*refs):
        srcs_r, lands_r = refs[:n], refs[n:2 * n]
        ssem, rsem, lsem = refs[2 * n], refs[2 * n + 1], refs[2 * n + 2]
        for cp in _xfer_copies(srcs_r, lands_r, ssem, rsem, scatter):
            cp.wait_send()
            cp.wait_recv()
        for cp in _own_copies(srcs_r, lands_r, lsem, scatter):
            cp.wait()

    out = pl.pallas_call(
        body, name=name,
        out_shape=(*[pltpu.HBM(a.shape, a.dtype) for a in srcs], *[pltpu.HBM(a.shape, a.dtype) for a in lands]),
        in_specs=[_HBM] * (2 * n) + [_SEM, _SEM, _SEM, pl.BlockSpec(memory_space=pl.ANY)],
        out_specs=tuple([_HBM] * (2 * n)),
        input_output_aliases={i: i for i in range(2 * n)},
        compiler_params=pltpu.CompilerParams(has_side_effects=_EFFECT),
    )(*srcs, *lands, send_sems, recv_sems, local_sems, after)
    return list(out[n:])


def _g2_first(srcs, lands, send_sems, recv_sems):
    x, y, c = _my_pos()
    me = 4 * x + 2 * y + c
    peers = [(x, y, 1 - c), (1 - x, y, c), (x, 1 - y, c), (1 - x, 1 - y, c)]
    return [[pltpu.make_async_remote_copy(
        src_ref=srcs[a], dst_ref=lands[a].at[me], send_sem=send_sems.at[4 * a + k], recv_sem=recv_sems.at[4 * a + k],
        device_id=p, device_id_type=MESH) for k, p in enumerate(peers)] for a in range(len(srcs))]


def _g2_relay(lands, send_sems, recv_sems):
    x, y, c = _my_pos()
    chips = [(1 - x, y), (x, 1 - y), (1 - x, 1 - y)]
    out = []
    for a in range(len(lands)):
        row = []
        for j, (px, py) in enumerate(chips):
            slab = lands[a].at[4 * px + 2 * py + c]
            row.append(pltpu.make_async_remote_copy(
                src_ref=slab, dst_ref=slab, send_sem=send_sems.at[3 * a + j], recv_sem=recv_sems.at[3 * a + j],
                device_id=(x, y, 1 - c), device_id_type=MESH))
        out.append(row)
    return out


def gather2_start(arrs, after, name):
    n = len(arrs)

    def body(*refs):
        srcs, lands = refs[:n], refs[n:2 * n]
        send_sems, recv_sems, local_sems = refs[2 * n + 1], refs[2 * n + 2], refs[2 * n + 3]
        for row in _g2_first(srcs, lands, send_sems, recv_sems):
            for cp in row:
                cp.start()
        for cp in _own_copies(srcs, lands, local_sems, False):
            cp.start()
        refs[-1][...] = jnp.zeros_like(refs[-1])

    lands0 = [lax.empty((N_DEV,) + a.shape, a.dtype) for a in arrs]
    out = pl.pallas_call(
        body, name=name,
        out_shape=(pltpu.SemaphoreType.DMA((4 * n,)), pltpu.SemaphoreType.DMA((4 * n,)), pltpu.SemaphoreType.DMA((n,)),
                   *[pltpu.HBM(a.shape, a.dtype) for a in arrs], *[pltpu.HBM(l.shape, l.dtype) for l in lands0],
                   _sds((8, 128), F32)),
        in_specs=[_HBM] * (2 * n) + [pl.BlockSpec(memory_space=pl.ANY)],
        out_specs=(_SEM, _SEM, _SEM, *([_HBM] * (2 * n)), pl.BlockSpec(memory_space=pltpu.VMEM)),
        input_output_aliases={i: 3 + i for i in range(2 * n)},
        compiler_params=pltpu.CompilerParams(has_side_effects=_EFFECT),
    )(*[pltpu.with_memory_space_constraint(a, pltpu.HBM) for a in arrs],
      *[pltpu.with_memory_space_constraint(l, pltpu.HBM) for l in lands0], after)
    return dict(send1=out[0], recv1=out[1], local=out[2], srcs=list(out[3:3 + n]), lands=list(out[3 + n:3 + 2 * n])), out[-1]


def gather2_relay(handle, after, name):
    n = len(handle["lands"])

    def body(*refs):
        lands = refs[:n]
        send1, recv1 = refs[n], refs[n + 1]
        send2, recv2 = refs[n + 3], refs[n + 4]
        firsts = _g2_first([l.at[0] for l in lands], lands, send1, recv1)
        relays = _g2_relay(lands, send2, recv2)
        for a in range(n):
            for j in range(3):
                firsts[a][1 + j].wait_recv()
                relays[a][j].start()
        refs[-1][...] = jnp.zeros_like(refs[-1])

    out = pl.pallas_call(
        body, name=name,
        out_shape=(pltpu.SemaphoreType.DMA((3 * n,)), pltpu.SemaphoreType.DMA((3 * n,)),
                   *[pltpu.HBM(l.shape, l.dtype) for l in handle["lands"]], _sds((8, 128), F32)),
        in_specs=[_HBM] * n + [_SEM, _SEM, pl.BlockSpec(memory_space=pl.ANY)],
        out_specs=(_SEM, _SEM, *([_HBM] * n), pl.BlockSpec(memory_space=pltpu.VMEM)),
        input_output_aliases={i: 2 + i for i in range(n)},
        compiler_params=pltpu.CompilerParams(has_side_effects=_EFFECT),
    )(*handle["lands"], handle["send1"], handle["recv1"], after)
    new = dict(handle)
    new.update(send2=out[0], recv2=out[1], lands=list(out[2:2 + n]))
    return new, out[-1]


def gather2_wait(handle, after, name):
    n = len(handle["lands"])

    def body(*refs):
        srcs, lands = refs[:n], refs[n:2 * n]
        send1, recv1, local, send2, recv2 = refs[2 * n:2 * n + 5]
        for a, row in enumerate(_g2_first(srcs, lands, send1, recv1)):
            row[0].wait_recv()
            for cp in row:
                cp.wait_send()
        for row in _g2_relay(lands, send2, recv2):
            for cp in row:
                cp.wait_send()
                cp.wait_recv()
        for cp in _own_copies(srcs, lands, local, False):
            cp.wait()

    out = pl.pallas_call(
        body, name=name,
        out_shape=(*[pltpu.HBM(a.shape, a.dtype) for a in handle["srcs"]],
                   *[pltpu.HBM(l.shape, l.dtype) for l in handle["lands"]]),
        in_specs=[_HBM] * (2 * n) + [_SEM] * 5 + [pl.BlockSpec(memory_space=pl.ANY)],
        out_specs=tuple([_HBM] * (2 * n)),
        input_output_aliases={i: i for i in range(2 * n)},
        compiler_params=pltpu.CompilerParams(has_side_effects=_EFFECT),
    )(*handle["srcs"], *handle["lands"], handle["send1"], handle["recv1"], handle["local"], handle["send2"],
      handle["recv2"], after)
    return list(out[n:])


def adamw(parts, w, m, v, name, row0=0, prev=None):
    r_tot, C = w.shape
    n_parts, R = parts.shape[0], parts.shape[1]
    row_bytes = 2 * (n_parts * C * parts.dtype.itemsize + 7 * C * 4)
    tr = R
    for cand in (512, 352, 256, 176, 128, 64):
        if R % cand == 0 and row0 % cand == 0 and R > cand and cand * row_bytes <= ADAMW_VMEM_BUDGET:
            tr = cand
            break
    tc = C
    if tr == R and R * row_bytes > ADAMW_VMEM_BUDGET:
        assert row0 == 0 and R == r_tot
        tc = next(t for t in (512, 256, 128) if C % t == 0 and R * row_bytes * t // C <= ADAMW_VMEM_BUDGET)
    assert row0 % tr == 0 and (tr % 8 == 0 or (tr == r_tot and row0 == 0))
    blk0 = row0 // tr
    c1 = 1.0 / (1.0 - ADAM_B1 ** ADAM_STEP)
    c2 = 1.0 / (1.0 - ADAM_B2 ** ADAM_STEP)

    def body(p_ref, w_ref, m_ref, v_ref, *rest):
        g_ref, d_ref, nm_ref, nv_ref = rest[-4:]
        g = p_ref[0].astype(F32)
        for k in range(1, n_parts):
            g = g + p_ref[k].astype(F32)
        nm = ADAM_B1 * m_ref[...] + (1.0 - ADAM_B1) * g
        nv = ADAM_B2 * v_ref[...] + (1.0 - ADAM_B2) * (g * g)
        g_ref[...] = g
        nm_ref[...] = nm
        nv_ref[...] = nv
        d_ref[...] = -ADAM_LR * ((nm * c1) / (jnp.sqrt(nv * c2) + ADAM_EPS) + ADAM_WD * w_ref[...])

    if tc == C:
        grid = (R // tr,)
        blk = pl.BlockSpec((tr, C), lambda i: (i + blk0, 0))
        p_spec = pl.BlockSpec((n_parts, tr, C), lambda i: (0, i, 0))
    else:
        grid = (C // tc,)
        blk = pl.BlockSpec((R, tc), lambda i: (0, i))
        p_spec = pl.BlockSpec((n_parts, R, tc), lambda i: (0, 0, i))
    in_specs = [p_spec, blk, blk, blk]
    args = [parts, w, m, v]
    aliases = {}
    if prev is not None:
        in_specs += [pl.BlockSpec(memory_space=pl.ANY)] * 4
        args += list(prev)
        aliases = {4 + k: k for k in range(4)}
    return pl.pallas_call(
        body, name=name, grid=grid, in_specs=in_specs, out_specs=(blk, blk, blk, blk),
        out_shape=tuple(_sds((r_tot, C), F32) for _ in range(4)), input_output_aliases=aliases,
        compiler_params=pltpu.CompilerParams(vmem_limit_bytes=VMEM_LIMIT))(*args)


def _ffn_fwd(x, nw, sh, sc, g, wt_gu, w_dn):
    h = norm_mod(x, nw, sh, sc, "ffn_norm")
    gp, up, a = ffn_up(h, wt_gu)
    y, xn = matmul(a, w_dn, "nn", BF, "ffn_down", res=x, gate=g, coef=0.5)
    return xn, (x, h, gp, up, a, y)


def _ffn_bwd(dxo, saved, nw, sc, g, wt_gu, w_dn):
    x, h, gp, up, a, y = saved
    dy, acc1 = resid_gate_bwd(dxo, y, g, 0.5, "ffn_gate_bwd")
    d_wdn = matmul(a, dy, "tn", BF, "ffn_down_wgrad")
    dg, du = ffn_down_dgrad(dy, w_dn, gp, up)
    d_wt = ffn_up_wgrad(dg, du, h)
    dh = ffn_up_dgrad(dg, du, wt_gu)
    dx, acc2 = norm_mod_bwd(x, dh, dxo, nw, sc, "ffn_norm_bwd")
    return dx, d_wt, d_wdn, (acc2[0], acc2[1], acc1[0]), acc2[2]


def _group_layout(a):
    L = a.shape[0]
    return a[:, :SSM_HEADS].reshape(L, SSM_GROUPS, _HPG).transpose(1, 0, 2)


def _ungroup_layout(a):
    L = a.shape[1]
    return jnp.pad(a.transpose(1, 0, 2).reshape(L, SSM_HEADS), ((0, 0), (0, 128 - SSM_HEADS)))


def _pad_row(vec, n=128):
    return jnp.pad(vec.reshape(1, -1), ((0, 0), (0, n - vec.shape[-1])))


def _mamba_fwd(x, nw, sh, sc, g, p):
    h = norm_mod(x, nw, sh, sc, "mix_norm")
    zx = matmul(h, p["w_in_t"], "nt", F32, "ssm_in")
    xc = conv_fwd(zx, p["conv_w"], p["conv_b"])
    dt, acs = dt_prep(zx, p["dt_bias"], p["a_log"])
    dt_g, acs_g = _group_layout(dt), _group_layout(acs)
    acs_t = acs_g.transpose(0, 2, 1)
    y, pst = ssd_fwd(xc, dt_g, acs_g, acs_t, p["d_exp"])
    yn = gate_norm(y, zx, p["norm_w"])
    yo, xn = matmul(yn, p["w_out"], "nn", BF, "ssm_out", res=x, gate=g, coef=1.0)
    return xn, (x, h, zx, xc, dt, dt_g, acs_g, acs_t, y, pst, yn, yo)


def _mamba_bwd(dxo, saved, nw, sc, g, p):
    x, h, zx, xc, dt, dt_g, acs_g, acs_t, y, pst, yn, yo = saved
    dyo, acc1 = resid_gate_bwd(dxo, yo, g, 1.0, "mix_gate_bwd")
    d_wout = matmul(yn, dyo, "tn", BF, "ssm_out_wgrad")
    dyn = matmul(dyo, p["w_out"], "nt", F32, "ssm_out_dgrad")
    dy, dz, accn = gate_norm_bwd(dyn, y, zx, p["norm_w"])
    dxs, dB, dC, ddt_g, da_g, dd = ssd_bwd(dy, xc, dt_g, acs_g, acs_t, pst, p["d_exp"])
    dxc = jnp.concatenate([dxs, dB, dC], axis=1)
    du, accc = conv_bwd(dxc, zx, p["conv_w"], p["conv_b"])
    draw, accdt = dt_bwd(_ungroup_layout(ddt_g), _ungroup_layout(da_g), dt, zx, p["dt_bias"], p["a_log"])
    dzx = jnp.concatenate([dz, du, draw], axis=1)
    d_win = matmul(dzx, h, "tn", BF, "ssm_in_wgrad")[:IN_PROJ]
    dh = matmul(dzx, p["w_in_t"], "nn", F32, "ssm_in_dgrad")
    dx, acc2 = norm_mod_bwd(x, dh, dxo, nw, sc, "mix_norm_bwd")
    small = dict(conv_w=accc[:CONV_WIDTH], conv_b=accc[CONV_WIDTH], dt_bias=accdt[0, :SSM_HEADS],
                 a_log=accdt[1, :SSM_HEADS], d=dd.reshape(SSM_HEADS, SSM_HEADDIM).sum(-1), norm_w=accn[0])
    return dx, d_win, d_wout, (acc2[0], acc2[1], acc1[0]), acc2[2], small


def _attn_layer_fwd(x, nw, sh, sc, g, p, kv):
    h = norm_mod(x, nw, sh, sc, "mix_norm")
    q = matmul(h, p["w_q"], "nn", F32, "attn_q", bias=p["b_q"])
    o = attn_fwd(q, kv, p["sinks"])
    yo, xn = matmul(o, p["w_o"], "nn", BF, "attn_o", bias=p["b_o"], res=x, gate=g, coef=1.0)
    return xn, (x, h, q, o, yo)


def _attn_layer_bwd(dxo, saved, nw, sc, g, p, kv):
    x, h, q, o, yo = saved
    dyo, acc1 = resid_gate_bwd(dxo, yo, g, 1.0, "mix_gate_bwd")
    d_wo = matmul(o, dyo, "tn", BF, "attn_o_wgrad")
    do = matmul(dyo, p["w_o"], "nt", F32, "attn_o_dgrad")
    dq, dkv_c, dkv_p, acca = attn_bwd(q, kv, do, p["sinks"])
    d_wq = matmul(h, dq, "tn", BF, "attn_q_wgrad")
    dh = matmul(dq, p["w_q"], "nt", F32, "attn_q_dgrad")
    dx, acc2 = norm_mod_bwd(x, dh, dxo, nw, sc, "mix_norm_bwd")
    small = dict(b_q=acca[0], sinks=acca[1, :ATT_HEADS], b_o=acc1[1])
    return dx, d_wq, d_wo, (acc2[0], acc2[1], acc1[0]), acc2[2], small, (dkv_c, dkv_p)


def _pack_rows(pieces):
    rows, spans, off = [], [], 0
    for a in pieces:
        flat = a.reshape(-1).astype(F32)
        n = -(-flat.shape[0] // D_MODEL)
        rows.append(jnp.pad(flat, (0, n * D_MODEL - flat.shape[0])).reshape(n, D_MODEL))
        spans.append((off, a.shape))
        off += n
    pad = -off % 8
    if pad:
        rows.append(jnp.zeros((pad, D_MODEL), F32))
    return jnp.concatenate(rows, axis=0), spans, off + pad


def _unpack_rows(g, spans):
    out = []
    for off, shape in spans:
        size = 1
        for s in shape:
            size *= s
        n = -(-size // D_MODEL)
        out.append(g[:, off:off + n].reshape(N_DEV, n * D_MODEL)[:, :size].reshape((N_DEV,) + tuple(shape)))
    return out


def _unshard_last(g):
    nd = g.ndim
    perm = tuple(range(1, nd - 1)) + (0, nd - 1)
    t = g.transpose(perm)
    return t.reshape(t.shape[:-2] + (N_DEV * g.shape[-1],))


def _shard_last(a, me):
    s = a.shape[-1] // N_DEV
    return lax.dynamic_slice_in_dim(a, me * s, s, axis=a.ndim - 1)


def kernel(x, c, ffn_norm_w, ffn_w_gu, ffn_w_down, mod_w, mod_b, mix_norm_w, ssm_w_in, ssm_conv_w, ssm_conv_b, ssm_dt_bias, ssm_a_log, ssm_d, ssm_norm_w, ssm_w_out, kv_norm_w, kv_mod_w, kv_mod_b, w_kv, b_kv, attn_w_q, attn_b_q, attn_sinks, attn_w_o, attn_b_o, final_norm_w, loss_target, m_ffn_norm_w, m_ffn_w_gu, m_ffn_w_down, m_mod_w, m_mod_b, m_mix_norm_w, m_ssm_w_in, m_ssm_conv_w, m_ssm_conv_b, m_ssm_dt_bias, m_ssm_a_log, m_ssm_d, m_ssm_norm_w, m_ssm_w_out, m_kv_norm_w, m_kv_mod_w, m_kv_mod_b, m_w_kv, m_b_kv, m_attn_w_q, m_attn_b_q, m_attn_sinks, m_attn_w_o, m_attn_b_o, m_final_norm_w, v_ffn_norm_w, v_ffn_w_gu, v_ffn_w_down, v_mod_w, v_mod_b, v_mix_norm_w, v_ssm_w_in, v_ssm_conv_w, v_ssm_conv_b, v_ssm_dt_bias, v_ssm_a_log, v_ssm_d, v_ssm_norm_w, v_ssm_w_out, v_kv_norm_w, v_kv_mod_w, v_kv_mod_b, v_w_kv, v_b_kv, v_attn_w_q, v_attn_b_q, v_attn_sinks, v_attn_w_o, v_attn_b_o, v_final_norm_w):
    D = D_MODEL
    me = 4 * lax.axis_index("x") + 2 * lax.axis_index("y") + lax.axis_index("c")
    xs = x[0]
    target = loss_target[0]
    mod_cols = mod_w.shape[-1]
    kvm_cols = kv_mod_w.shape[-1]

    def fence(arrs):
        tot = jnp.zeros((1, 1), F32)
        for a in arrs:
            tot = tot + lax.slice(a, (0,) * a.ndim, (1,) * a.ndim).reshape(1, 1).astype(F32)
        return jnp.broadcast_to(tot, (8, 128))

    packed, spans, _ = _pack_rows([c, ffn_norm_w, ssm_conv_w, ssm_conv_b, ssm_norm_w])
    nrow = packed.shape[0]
    g1 = small_all_gather(packed).reshape(N_DEV, nrow, D)
    c_all, fnw_g, cw_g, cb_g, snw_g = _unpack_rows(g1, spans)
    c_all = c_all.reshape(N_DEV, D)
    ffn_nw = _unshard_last(fnw_g)
    conv_w = _unshard_last(cw_g)
    conv_b = _unshard_last(cb_g)
    ssm_nw = _unshard_last(snw_g)

    mod_b_loc = lax.dynamic_slice_in_dim(mod_b, me * mod_cols, mod_cols, axis=1).reshape(DEPTH, 1, mod_cols)
    kvb_loc = lax.dynamic_slice_in_dim(kv_mod_b, me * kvm_cols, kvm_cols, axis=0).reshape(1, 1, kvm_cols)
    modp, c_act = mod_fwd(c_all, mod_w, mod_b_loc, "mod_fwd")
    kvmp, _ = mod_fwd(c_all, kv_mod_w.reshape(1, D, kvm_cols), kvb_loc, "kv_mod_fwd")
    packed2, spans2, _ = _pack_rows([modp, kvmp])
    nrow2 = packed2.shape[0]
    g2 = small_all_gather(packed2).reshape(N_DEV, nrow2, D)
    modp_g, kvmp_g = _unpack_rows(g2, spans2)
    mod_all = modp_g.transpose(1, 2, 0, 3).reshape(DEPTH, N_DEV, N_MOD * D)
    kvm_all = kvmp_g.transpose(1, 2, 0, 3).reshape(N_DEV, 2 * D)
    mod_me = lax.dynamic_index_in_dim(mod_all, me, axis=1, keepdims=False).reshape(DEPTH, N_MOD, 1, D)
    kvm_me = lax.dynamic_index_in_dim(kvm_all, me, axis=0, keepdims=False).reshape(2, 1, D)

    gu_t = jnp.swapaxes(ffn_w_gu, 2, 3)
    win_t = jnp.transpose(ssm_w_in, (2, 0, 1))
    S = gu_t.shape[2]
    s_in = win_t.shape[0]
    r_dn = ffn_w_down.shape[2]
    r_mix = ssm_w_out.shape[1]
    r_at = attn_w_q.shape[1]

    def layer_pack(k):
        arrs = [gu_t[k, 0].astype(BF), gu_t[k, 1].astype(BF), ffn_w_down[k, 0].astype(BF), ffn_w_down[k, 1].astype(BF)]
        if k < N_A:
            arrs += [ssm_w_out[k].astype(BF), win_t[:, k].astype(BF)]
        else:
            arrs += [attn_w_q[k - N_A].astype(BF), attn_w_o[k - N_A].astype(BF)]
        if k == N_A:
            arrs.append(w_kv.astype(BF))
        return arrs

    packs = [layer_pack(k) for k in range(DEPTH)]
    gathered = [None] * DEPTH
    first = big_all_gather([packs[0][0], packs[0][2], packs[0][4], packs[0][5]])
    pending = [None] * DEPTH
    pending[0], tok_next = gather2_start([packs[0][1], packs[0][3]], fence([g2, first[0]]), "gather_start_0")
    for k in range(1, DEPTH):
        pending[k], tok_next = gather2_start(packs[k], tok_next, "gather_start_%d" % k)
    gathered[0] = [first[0], None, first[1], None, first[2], first[3]]

    def wt_gu_full(i, j):
        return gathered[i][j].reshape(N_DEV * S, D)

    def w_dn_full(i, j):
        return gathered[i][2 + j].reshape(D_FF, D)

    def mix_rows(i, a):
        return gathered[i][4 + a].reshape(-1, D)

    def mamba_params(j):
        w_in_t = jnp.pad(mix_rows(j, 1), ((0, IN_PROJ_PAD - IN_PROJ), (0, 0)))
        return dict(w_in_t=w_in_t, w_out=mix_rows(j, 0), conv_w=conv_w[j], conv_b=conv_b[j].reshape(1, -1),
                    dt_bias=_pad_row(ssm_dt_bias[j]), a_log=_pad_row(ssm_a_log[j]),
                    d_exp=jnp.repeat(ssm_d[j], SSM_HEADDIM).reshape(SSM_GROUPS, 1, _GW),
                    norm_w=ssm_nw[j].reshape(1, -1))

    def attn_params(j):
        return dict(w_q=mix_rows(N_A + j, 0), w_o=mix_rows(N_A + j, 1),
                    b_q=attn_b_q[j].reshape(1, -1), b_o=attn_b_o[j].reshape(1, -1), sinks=_pad_row(attn_sinks[j]))

    saved = []
    kv = None
    kv_saved = None
    w_kv_full = None
    xcur = xs
    for i in range(DEPTH):
        if i >= 1:
            gathered[i] = gather2_wait(pending[i], xcur, "gather_wait_%d" % i)
        md = mod_me[i]
        if i == 0:
            md = md + tok_next[0, 0]
        if i == N_A:
            w_kv_full = gathered[N_A][6].reshape(D, KV_DIM)
            h_kv = norm_mod(xcur, kv_norm_w.reshape(1, D), kvm_me[0], kvm_me[1], "kv_norm")
            kv = matmul(h_kv, w_kv_full, "nn", F32, "kv_proj", bias=b_kv.reshape(1, -1))
            kv_saved = (xcur, h_kv)
        x1, s1 = _ffn_fwd(xcur, ffn_nw[i, 0].reshape(1, D), md[0], md[1], md[2], wt_gu_full(i, 0), w_dn_full(i, 0))
        gm = md[5]
        if i == 0:
            pending[0], tok_r = gather2_relay(pending[0], x1, "gather_relay_0")
            gm = gm + tok_r[0, 0]
        if i < N_A:
            pm = mamba_params(i)
            x2, s2 = _mamba_fwd(x1, mix_norm_w[i].reshape(1, D), md[3], md[4], gm, pm)
        else:
            pm = attn_params(i - N_A)
            x2, s2 = _attn_layer_fwd(x1, mix_norm_w[i].reshape(1, D), md[3], md[4], gm, pm, kv)
        g2f = md[8]
        if i + 1 < DEPTH:
            pending[i + 1], tok_r = gather2_relay(pending[i + 1], x2, "gather_relay_%d" % (i + 1))
            g2f = g2f + tok_r[0, 0]
        if i == 0:
            rest = gather2_wait(pending[0], x2, "gather_wait_0")
            gathered[0] = [first[0], rest[0], first[1], rest[1], first[2], first[3]]
        x3, s3 = _ffn_fwd(x2, ffn_nw[i, 1].reshape(1, D), md[6], md[7], g2f, wt_gu_full(i, 1), w_dn_full(i, 1))
        saved.append((s1, s2, s3, pm))
        xcur = x3

    dx, accf = final_loss(xcur, final_norm_w.reshape(1, D), target)
    d_mod = [None] * DEPTH
    d_ffn_nw = [[None, None] for _ in range(DEPTH)]
    d_mix_nw = [None] * DEPTH
    sm_m, sm_a = [None] * N_A, [None] * N_A
    kv_parts = [None] * N_A
    d_kvm = d_kv_nw = d_bkv = None
    exchanges = []
    tok = None

    def send(arrs, tag, after=None):
        handle, t = xfer_start(arrs, True, dx if after is None else after, "exch_start_%s" % tag)
        exchanges.append((handle, tag))
        return t

    def ffn_slabs(d_wt, d_wdn):
        return [d_wt.reshape(N_DEV, S, D), d_wdn.reshape(N_DEV, r_dn, D)]

    for i in reversed(range(DEPTH)):
        md = mod_me[i]
        s1, s2, s3, pm = saved[i]
        g2 = md[8] if tok is None else md[8] + tok[0, 0]
        dx, d_wt, d_wdn, m2, d_ffn_nw[i][1] = _ffn_bwd(
            dx, s3, ffn_nw[i, 1].reshape(1, D), md[7], g2, wt_gu_full(i, 1), w_dn_full(i, 1))
        tok = send(ffn_slabs(d_wt, d_wdn), "f%d1" % i)
        gm = md[5] + tok[0, 0]
        if i < N_A:
            dx, d_in, d_out, mm_, d_mix_nw[i], sm_m[i] = _mamba_bwd(dx, s2, mix_norm_w[i].reshape(1, D), md[4], gm, pm)
            tok = send([d_in.reshape(N_DEV, s_in, D), d_out.reshape(N_DEV, r_mix, D)], "m%d" % i)
        else:
            j = i - N_A
            dx, d_q, d_o, mm_, d_mix_nw[i], sm_a[j], kv_parts[j] = _attn_layer_bwd(
                dx, s2, mix_norm_w[i].reshape(1, D), md[4], gm, pm, kv)
            tok = send([d_q.reshape(N_DEV, r_at, D), d_o.reshape(N_DEV, r_at, D)], "m%d" % i)
        g1 = md[2] + tok[0, 0]
        dx, d_wt, d_wdn, m1, d_ffn_nw[i][0] = _ffn_bwd(
            dx, s1, ffn_nw[i, 0].reshape(1, D), md[1], g1, wt_gu_full(i, 0), w_dn_full(i, 0))
        d_mod[i] = jnp.concatenate(list(m1) + list(mm_) + list(m2), axis=0)
        last = ffn_slabs(d_wt, d_wdn)
        if i == N_A:
            x_kv, h_kv = kv_saved
            dkv, acck = kv_grad_combine(kv_parts)
            d_bkv = acck[0]
            d_kv_w = matmul(h_kv, dkv, "tn", BF, "kv_wgrad")
            dh_kv = matmul(dkv, w_kv_full, "nt", F32, "kv_dgrad")
            dx, acc_kv = norm_mod_bwd(x_kv, dh_kv, dx, kv_norm_w.reshape(1, D), kvm_me[1], "kv_norm_bwd")
            d_kvm = jnp.concatenate([acc_kv[0], acc_kv[1]], axis=0)
            d_kv_nw = acc_kv[2]
            last.append(d_kv_w.reshape(N_DEV, -1, KV_DIM))
        if i > 0:
            tok = send(last, "f%d0" % i)
    grad_x = dx.reshape(x.shape)

    small_list = [
        jnp.stack(d_mod, 0), d_kvm,
        jnp.stack([jnp.stack(r, 0) for r in d_ffn_nw], 0),
        jnp.stack(d_mix_nw, 0),
        jnp.stack([s["conv_w"] for s in sm_m], 0), jnp.stack([s["conv_b"] for s in sm_m], 0),
        jnp.stack([s["dt_bias"] for s in sm_m], 0), jnp.stack([s["a_log"] for s in sm_m], 0),
        jnp.stack([s["d"] for s in sm_m], 0), jnp.stack([s["norm_w"] for s in sm_m], 0),
        d_kv_nw, d_bkv,
        jnp.stack([s["b_q"] for s in sm_a], 0), jnp.stack([s["sinks"] for s in sm_a], 0),
        jnp.stack([s["b_o"] for s in sm_a], 0), accf[0], accf[1],
    ]
    packed3, spans3, _ = _pack_rows(small_list)
    nrow3 = packed3.shape[0]
    g3 = small_all_gather(packed3).reshape(N_DEV, nrow3, D)
    tok_last = send(last, "f00", after=g3)
    (p_mod, p_kvm, p_fnw, p_mnw, p_cw, p_cb, p_dtb, p_al, p_d, p_snw, p_kvnw, p_bkv, p_bq, p_sk, p_bo, p_fin,
     p_loss) = _unpack_rows(g3, spans3)

    loss = 0.5 / D * jnp.sum(p_loss)

    c_act_t = c_act.T
    dmod_loc = _shard_last(p_mod, me).transpose(1, 0, 2)
    dkvm_loc = _shard_last(p_kvm, me).reshape(1, N_DEV, kvm_cols) + tok_last[0, 0]
    gp_mod_w = mod_wgrad(c_act_t, dmod_loc, "mod_wgrad")
    gp_kvm_w = mod_wgrad(c_act_t, dkvm_loc, "kv_mod_wgrad")[0]

    def as_parts_single(a):
        return a[None]

    def upd(name, parts, w, m, v):
        shp = w.shape
        c_last = shp[-1]
        out = adamw(parts.reshape(parts.shape[0], -1, c_last), w.reshape(-1, c_last), m.reshape(-1, c_last),
                    v.reshape(-1, c_last), "adamw_" + name)
        return tuple(o.reshape(shp) for o in out)

    views = {
        "ffn_w_gu": [jnp.swapaxes(t, 2, 3).reshape(-1, D) for t in (ffn_w_gu, m_ffn_w_gu, v_ffn_w_gu)],
        "ffn_w_down": [t.reshape(-1, D) for t in (ffn_w_down, m_ffn_w_down, v_ffn_w_down)],
        "ssm_w_out": [t.reshape(-1, D) for t in (ssm_w_out, m_ssm_w_out, v_ssm_w_out)],
        "attn_w_q": [t.reshape(-1, D) for t in (attn_w_q, m_attn_w_q, v_attn_w_q)],
        "attn_w_o": [t.reshape(-1, D) for t in (attn_w_o, m_attn_w_o, v_attn_w_o)],
    }
    filled = {k: None for k in views}

    def upd_rows(name, parts, row0):
        w, m, v = views[name]
        filled[name] = adamw(parts, w, m, v, "adamw_" + name, row0=row0, prev=filled[name])
        return filled[name][3]

    res = {}
    res["ffn_norm_w"] = upd("ffn_norm_w", _shard_last(p_fnw, me), ffn_norm_w, m_ffn_norm_w, v_ffn_norm_w)
    res["mod_w"] = upd("mod_w", as_parts_single(gp_mod_w), mod_w, m_mod_w, v_mod_w)
    res["mod_b"] = upd("mod_b", p_mod, mod_b, m_mod_b, v_mod_b)
    res["mix_norm_w"] = upd("mix_norm_w", p_mnw, mix_norm_w, m_mix_norm_w, v_mix_norm_w)
    res["ssm_conv_w"] = upd("ssm_conv_w", _shard_last(p_cw, me), ssm_conv_w, m_ssm_conv_w, v_ssm_conv_w)
    res["ssm_conv_b"] = upd("ssm_conv_b", _shard_last(p_cb, me), ssm_conv_b, m_ssm_conv_b, v_ssm_conv_b)
    res["ssm_dt_bias"] = upd("ssm_dt_bias", p_dtb, ssm_dt_bias, m_ssm_dt_bias, v_ssm_dt_bias)
    res["ssm_a_log"] = upd("ssm_a_log", p_al, ssm_a_log, m_ssm_a_log, v_ssm_a_log)
    res["ssm_d"] = upd("ssm_d", p_d, ssm_d, m_ssm_d, v_ssm_d)
    res["ssm_norm_w"] = upd("ssm_norm_w", _shard_last(p_snw, me), ssm_norm_w, m_ssm_norm_w, v_ssm_norm_w)
    res["kv_norm_w"] = upd("kv_norm_w", p_kvnw.reshape(N_DEV, 1, D), kv_norm_w.reshape(1, D),
                           m_kv_norm_w.reshape(1, D), v_kv_norm_w.reshape(1, D))
    res["kv_mod_w"] = upd("kv_mod_w", as_parts_single(gp_kvm_w), kv_mod_w, m_kv_mod_w, v_kv_mod_w)
    res["kv_mod_b"] = upd("kv_mod_b", p_kvm.reshape(N_DEV, 1, 2 * D), kv_mod_b.reshape(1, -1),
                          m_kv_mod_b.reshape(1, -1), v_kv_mod_b.reshape(1, -1))
    res["b_kv"] = upd("b_kv", p_bkv.reshape(N_DEV, 1, KV_DIM), b_kv.reshape(1, -1), m_b_kv.reshape(1, -1),
                      v_b_kv.reshape(1, -1))
    res["attn_b_q"] = upd("attn_b_q", p_bq, attn_b_q, m_attn_b_q, v_attn_b_q)
    res["attn_sinks"] = upd("attn_sinks", p_sk, attn_sinks, m_attn_sinks, v_attn_sinks)
    res["attn_b_o"] = upd("attn_b_o", p_bo, attn_b_o, m_attn_b_o, v_attn_b_o)
    res["final_norm_w"] = upd("final_norm_w", p_fin.reshape(N_DEV, 1, D), final_norm_w.reshape(1, D),
                              m_final_norm_w.reshape(1, D), v_final_norm_w.reshape(1, D))

    chain = fence([dx, tok_last] + [t[3] for t in res.values()])
    r_in_parts = [None] * N_A
    r_kv = None
    for handle, tag in exchanges:
        got = xfer_wait(handle, chain, "exch_wait_%s" % tag)
        i = int(tag[1])
        if tag[0] == "f":
            jf = int(tag[2])
            done = [upd_rows("ffn_w_gu", got[0], (2 * i + jf) * S), upd_rows("ffn_w_down", got[1], (2 * i + jf) * r_dn)]
            if len(got) > 2:
                res["w_kv"] = upd("w_kv", got[2], w_kv, m_w_kv, v_w_kv)
                done.append(res["w_kv"][3])
        elif i < N_A:
            r_in_parts[i] = got[0]
            done = [upd_rows("ssm_w_out", got[1], i * r_mix)]
            if i == 0:
                win_out = adamw(jnp.stack(r_in_parts, axis=2).reshape(N_DEV, s_in * N_A, D),
                                *[jnp.transpose(t, (2, 0, 1)).reshape(-1, D) for t in (ssm_w_in, m_ssm_w_in, v_ssm_w_in)],
                                "adamw_ssm_w_in")
                res["ssm_w_in"] = tuple(jnp.transpose(t.reshape(win_t.shape), (1, 2, 0)) for t in win_out)
                done.append(win_out[3])
        else:
            done = [upd_rows("attn_w_q", got[0], (i - N_A) * r_at), upd_rows("attn_w_o", got[1], (i - N_A) * r_at)]
        chain = fence(done)

    res["ffn_w_gu"] = tuple(jnp.swapaxes(t.reshape(gu_t.shape), 2, 3) for t in filled["ffn_w_gu"])
    res["ffn_w_down"] = tuple(t.reshape(ffn_w_down.shape) for t in filled["ffn_w_down"])
    res["ssm_w_out"] = tuple(t.reshape(ssm_w_out.shape) for t in filled["ssm_w_out"])
    res["attn_w_q"] = tuple(t.reshape(attn_w_q.shape) for t in filled["attn_w_q"])
    res["attn_w_o"] = tuple(t.reshape(attn_w_o.shape) for t in filled["attn_w_o"])

    names = ["ffn_norm_w", "ffn_w_gu", "ffn_w_down", "mod_w", "mod_b", "mix_norm_w", "ssm_w_in", "ssm_conv_w",
             "ssm_conv_b", "ssm_dt_bias", "ssm_a_log", "ssm_d", "ssm_norm_w", "ssm_w_out", "kv_norm_w", "kv_mod_w",
             "kv_mod_b", "w_kv", "b_kv", "attn_w_q", "attn_b_q", "attn_sinks", "attn_w_o", "attn_b_o", "final_norm_w"]
    vec_shapes = {"kv_norm_w": (D,), "kv_mod_b": (2 * D,), "b_kv": (KV_DIM,), "final_norm_w": (D,)}
    outs = [loss, grad_x]
    for k in range(4):
        for nme in names:
            t = res[nme][k]
            if nme in vec_shapes:
                t = t.reshape(vec_shapes[nme])
            outs.append(t)
    return tuple(outs)
```

```python
import functools

import jax
import jax.numpy as jnp
from jax import lax
from jax.experimental import pallas as pl
from jax.experimental.pallas import tpu as pltpu

F32 = jnp.float32
BF = jnp.bfloat16
MESH = pl.DeviceIdType.MESH

N_DEV = 8
D_MODEL = 1024
DEPTH = 4
N_A = 2
EPS = 1e-5
N_MOD = 9
D_FF = 2816
D_INNER = 2048
SSM_HEADDIM = 64
SSM_HEADS = 32
SSM_GROUPS = 8
SSM_STATE = 128
CONV_WIDTH = 4
CHUNK = 512
CONV_DIM = D_INNER + 2 * SSM_GROUPS * SSM_STATE
IN_PROJ = D_INNER + CONV_DIM + SSM_HEADS
IN_PROJ_PAD = D_INNER + CONV_DIM + 128
ATT_HEADS = 16
KV_HEADS = 4
HEAD_DIM = 64
WINDOW = 128
KV_DIM = 2 * KV_HEADS * HEAD_DIM

ADAM_LR = 0.001
ADAM_B1 = 0.9
ADAM_B2 = 0.999
ADAM_EPS = 1e-08
ADAM_WD = 0.01
ADAM_STEP = 10

VMEM_LIMIT = 48 * 2 ** 20
ADAMW_VMEM_BUDGET = 24 * 2 ** 20
NEG = -1e30


def _call(body, name, grid, in_specs, out_specs, out_shape, scratch=()):
    return pl.pallas_call(
        body, name=name, grid=grid, in_specs=in_specs, out_specs=out_specs, out_shape=out_shape,
        scratch_shapes=list(scratch),
        compiler_params=pltpu.CompilerParams(vmem_limit_bytes=VMEM_LIMIT))


def _tile(n, cap):
    t = (cap // 128) * 128
    while t >= 128:
        if n % t == 0:
            return t
        t -= 128
    return n


def _sds(shape, dtype):
    return jax.ShapeDtypeStruct(shape, dtype)


def _sigmoid(v):
    return 1.0 / (1.0 + jnp.exp(-v))


def _dot(a, b, dims):
    return lax.dot_general(a, b, (dims, ((), ())), preferred_element_type=F32)


def _dot_nn(a, b):
    return _dot(a.astype(BF), b.astype(BF), ((1,), (0,)))


def _dot_nt(a, b):
    return _dot(a.astype(BF), b.astype(BF), ((1,), (1,)))


def _dot_tn(a, b):
    return _dot(a.astype(BF), b.astype(BF), ((0,), (0,)))


def matmul(a, b, mode, out_dtype, name, bias=None, res=None, gate=None, coef=1.0):
    if mode == "nn":
        (M, K), (_, N) = a.shape, b.shape
    elif mode == "nt":
        (M, K), (N, _) = a.shape, b.shape
    else:
        (K, M), (_, N) = a.shape, b.shape
    cap_n = 512 if K > 4096 else 1024
    tm = _tile(M, 1024 if (mode == "tn" or K <= D_FF) else 512)
    tn = _tile(N, cap_n)
    if mode != "tn" and tm * tn > 1024 * 896:
        tn = _tile(N, 512)
    if mode == "nn":
        a_spec = pl.BlockSpec((tm, K), lambda i, j: (i, 0))
        b_spec = pl.BlockSpec((K, tn), lambda i, j: (0, j))
        fn = _dot_nn
    elif mode == "nt":
        a_spec = pl.BlockSpec((tm, K), lambda i, j: (i, 0))
        b_spec = pl.BlockSpec((tn, K), lambda i, j: (j, 0))
        fn = _dot_nt
    else:
        a_spec = pl.BlockSpec((K, tm), lambda i, j: (0, i))
        b_spec = pl.BlockSpec((K, tn), lambda i, j: (0, j))
        fn = _dot_tn
    has_bias, has_res = bias is not None, res is not None
    o_spec = pl.BlockSpec((tm, tn), lambda i, j: (i, j))
    v_spec = pl.BlockSpec((1, tn), lambda i, j: (0, j))
    in_specs, args = [a_spec, b_spec], [a, b]
    if has_bias:
        in_specs.append(v_spec)
        args.append(bias)
    if has_res:
        in_specs += [o_spec, v_spec]
        args += [res, gate]

    def body(*refs):
        a_ref, b_ref = refs[0], refs[1]
        k = 2
        y = fn(a_ref[...], b_ref[...])
        if has_bias:
            y = y + refs[k][...]
            k += 1
        if has_res:
            res_ref, gate_ref = refs[k], refs[k + 1]
            refs[k + 2][...] = y.astype(out_dtype)
            refs[k + 3][...] = res_ref[...] + coef * gate_ref[...] * y
        else:
            refs[k][...] = y.astype(out_dtype)

    if has_res:
        out_shape = (_sds((M, N), out_dtype), _sds((M, N), F32))
        out_specs = (o_spec, o_spec)
    else:
        out_shape = _sds((M, N), out_dtype)
        out_specs = o_spec
    return _call(body, name, (M // tm, N // tn), in_specs, out_specs, out_shape)(*args)


def norm_mod(x, nw, sh, sc, name):
    L, D = x.shape
    tm = _tile(L, 512)

    def body(x_ref, nw_ref, sh_ref, sc_ref, h_ref):
        xf = x_ref[...]
        r = lax.rsqrt(jnp.mean(xf * xf, axis=-1, keepdims=True) + EPS)
        n = xf * r * nw_ref[...]
        h_ref[...] = (n * (1.0 + sc_ref[...]) + sh_ref[...]).astype(BF)

    row = pl.BlockSpec((tm, D), lambda i: (i, 0))
    vec = pl.BlockSpec((1, D), lambda i: (0, 0))
    return _call(body, name, (L // tm,), [row, vec, vec, vec], row, _sds((L, D), BF))(x, nw, sh, sc)


def norm_mod_bwd(x, dh, dres, nw, sc, name):
    L, D = x.shape
    tm = _tile(L, 512)

    def body(x_ref, dh_ref, dres_ref, nw_ref, sc_ref, dx_ref, acc_ref):
        @pl.when(pl.program_id(0) == 0)
        def _():
            acc_ref[...] = jnp.zeros_like(acc_ref)

        xf = x_ref[...]
        dhf = dh_ref[...].astype(F32)
        r = lax.rsqrt(jnp.mean(xf * xf, axis=-1, keepdims=True) + EPS)
        xhat = xf * r
        nwv = nw_ref[...]
        dn = dhf * (1.0 + sc_ref[...])
        dxhat = dn * nwv
        proj = jnp.mean(dxhat * xhat, axis=-1, keepdims=True)
        dx_ref[...] = dres_ref[...] + r * (dxhat - xhat * proj)
        acc_ref[0:1, :] += jnp.sum(dhf, axis=0, keepdims=True)
        acc_ref[1:2, :] += jnp.sum(dhf * xhat * nwv, axis=0, keepdims=True)
        acc_ref[2:3, :] += jnp.sum(dn * xhat, axis=0, keepdims=True)

    row = pl.BlockSpec((tm, D), lambda i: (i, 0))
    vec = pl.BlockSpec((1, D), lambda i: (0, 0))
    acc = pl.BlockSpec((8, D), lambda i: (0, 0))
    return _call(body, name, (L // tm,), [row, row, row, vec, vec], (row, acc),
                 (_sds((L, D), F32), _sds((8, D), F32)))(x, dh, dres, nw, sc)


def final_loss(x, nw, target):
    L, D = x.shape
    tm = _tile(L, 512)

    def body(x_ref, nw_ref, t_ref, dx_ref, acc_ref):
        @pl.when(pl.program_id(0) == 0)
        def _():
            acc_ref[...] = jnp.zeros_like(acc_ref)

        xf = x_ref[...]
        r = lax.rsqrt(jnp.mean(xf * xf, axis=-1, keepdims=True) + EPS)
        xhat = xf * r
        nwv = nw_ref[...]
        err = xhat * nwv - t_ref[...]
        dy = err * (1.0 / D)
        dxhat = dy * nwv
        proj = jnp.mean(dxhat * xhat, axis=-1, keepdims=True)
        dx_ref[...] = r * (dxhat - xhat * proj)
        acc_ref[0:1, :] += jnp.sum(dy * xhat, axis=0, keepdims=True)
        acc_ref[1:2, :] += jnp.sum(err * err, axis=0, keepdims=True)

    row = pl.BlockSpec((tm, D), lambda i: (i, 0))
    vec = pl.BlockSpec((1, D), lambda i: (0, 0))
    acc = pl.BlockSpec((8, D), lambda i: (0, 0))
    return _call(body, "final_loss", (L // tm,), [row, vec, row], (row, acc),
                 (_sds((L, D), F32), _sds((8, D), F32)))(x, nw, target)


def resid_gate_bwd(dxo, y, gate, coef, name):
    L, D = dxo.shape
    tm = _tile(L, 512)

    def body(dxo_ref, y_ref, g_ref, dy_ref, acc_ref):
        @pl.when(pl.program_id(0) == 0)
        def _():
            acc_ref[...] = jnp.zeros_like(acc_ref)

        d = dxo_ref[...]
        dy = coef * g_ref[...] * d
        dy_ref[...] = dy.astype(BF)
        acc_ref[0:1, :] += coef * jnp.sum(d * y_ref[...].astype(F32), axis=0, keepdims=True)
        acc_ref[1:2, :] += jnp.sum(dy, axis=0, keepdims=True)

    row = pl.BlockSpec((tm, D), lambda i: (i, 0))
    vec = pl.BlockSpec((1, D), lambda i: (0, 0))
    acc = pl.BlockSpec((8, D), lambda i: (0, 0))
    return _call(body, name, (L // tm,), [row, row, vec], (row, acc),
                 (_sds((L, D), BF), _sds((8, D), F32)))(dxo, y, gate)


def ffn_up(h, wt):
    L, D = h.shape
    F = wt.shape[0] // 2
    tm, tn = _tile(L, 2048), _tile(F, 256)
    nj = F // tn

    def body(h_ref, wg_ref, wu_ref, g_ref, u_ref, a_ref):
        hv = h_ref[...]
        g = _dot_nt(hv, wg_ref[...])
        u = _dot_nt(hv, wu_ref[...])
        g_ref[...] = g.astype(BF)
        u_ref[...] = u.astype(BF)
        a_ref[...] = (g * _sigmoid(g) * u).astype(BF)

    o = pl.BlockSpec((tm, tn), lambda i, n: (i, n))
    return _call(body, "ffn_up", (L // tm, nj),
                 [pl.BlockSpec((tm, D), lambda i, n: (i, 0)),
                  pl.BlockSpec((tn, D), lambda i, n: (n, 0)),
                  pl.BlockSpec((tn, D), lambda i, n: (n + nj, 0))],
                 (o, o, o), tuple(_sds((L, F), BF) for _ in range(3)))(h, wt, wt)


def ffn_down_dgrad(dy, wd, g, u):
    L, D = dy.shape
    F = wd.shape[0]
    tm, tn = _tile(L, 2048), _tile(F, 256)

    def body(dy_ref, w_ref, g_ref, u_ref, dg_ref, du_ref):
        da = _dot_nt(dy_ref[...], w_ref[...])
        gv = g_ref[...].astype(F32)
        uv = u_ref[...].astype(F32)
        s = _sigmoid(gv)
        dg_ref[...] = (da * uv * s * (1.0 + gv * (1.0 - s))).astype(BF)
        du_ref[...] = (da * gv * s).astype(BF)

    o = pl.BlockSpec((tm, tn), lambda i, n: (i, n))
    return _call(body, "ffn_down_dgrad", (L // tm, F // tn),
                 [pl.BlockSpec((tm, D), lambda i, n: (i, 0)), pl.BlockSpec((tn, D), lambda i, n: (n, 0)), o, o],
                 (o, o), (_sds((L, F), BF), _sds((L, F), BF)))(dy, wd, g, u)


def ffn_up_wgrad(dg, du, h):
    L, F = dg.shape
    D = h.shape[1]
    tm = _tile(F, 256)
    nblk = F // tm

    def half(d, off, prev):
        def body(d_ref, h_ref, *rest):
            rest[-1][...] = _dot_tn(d_ref[...], h_ref[...]).astype(BF)

        in_specs = [pl.BlockSpec((L, tm), lambda i: (0, i)), pl.BlockSpec((L, D), lambda i: (0, 0))]
        args = [d, h]
        aliases = {}
        if prev is not None:
            in_specs.append(pl.BlockSpec(memory_space=pl.ANY))
            args.append(prev)
            aliases = {2: 0}
        return pl.pallas_call(
            body, name="ffn_up_wgrad", grid=(nblk,), in_specs=in_specs,
            out_specs=pl.BlockSpec((tm, D), lambda i: (i + off * nblk, 0)),
            out_shape=_sds((2 * F, D), BF), input_output_aliases=aliases,
            compiler_params=pltpu.CompilerParams(vmem_limit_bytes=VMEM_LIMIT))(*args)

    return half(du, 1, half(dg, 0, None))


def ffn_up_dgrad(dg, du, wt):
    L, F = dg.shape
    D = wt.shape[1]
    tm, tn = _tile(L, 1024), _tile(D, 512)

    def body(dg_ref, du_ref, wg_ref, wu_ref, o_ref):
        o_ref[...] = _dot_nn(dg_ref[...], wg_ref[...]) + _dot_nn(du_ref[...], wu_ref[...])

    a = pl.BlockSpec((tm, F), lambda i, n: (i, 0))
    return _call(body, "ffn_up_dgrad", (L // tm, D // tn),
                 [a, a, pl.BlockSpec((F, tn), lambda i, n: (0, n)), pl.BlockSpec((F, tn), lambda i, n: (1, n))],
                 pl.BlockSpec((tm, tn), lambda i, n: (i, n)), _sds((L, D), F32))(dg, du, wt, wt)


_HALO = 8


def _shift_down(cur, prev8, k):
    out = pltpu.roll(cur, k, 0)
    rows8 = lax.broadcasted_iota(jnp.int32, prev8.shape, 0)
    head = jnp.where(rows8 < k, pltpu.roll(prev8, k, 0), out[0:_HALO])
    if cur.shape[0] == _HALO:
        return head
    return jnp.concatenate([head, out[_HALO:]], axis=0)


def _shift_up(cur, next8, k):
    n = cur.shape[0]
    out = pltpu.roll(cur, n - k, 0)
    rows8 = lax.broadcasted_iota(jnp.int32, next8.shape, 0)
    tail = jnp.where(rows8 >= _HALO - k, pltpu.roll(next8, _HALO - k, 0), out[n - _HALO:])
    return jnp.concatenate([out[:n - _HALO], tail], axis=0)


def _conv_pre(cur, prev8, w_ref, b_ref):
    shifted = [_shift_down(cur, prev8, k) for k in range(1, CONV_WIDTH)]
    s = cur * w_ref[CONV_WIDTH - 1:CONV_WIDTH, :] + b_ref[...]
    for k in range(1, CONV_WIDTH):
        s = s + shifted[k - 1] * w_ref[CONV_WIDTH - 1 - k:CONV_WIDTH - k, :]
    return s, shifted


def _silu_grad(s):
    sg = _sigmoid(s)
    return sg * (1.0 + s * (1.0 - sg))


_XBC_COL0 = D_INNER // 512


def conv_fwd(zx, w, b):
    L = zx.shape[0]
    tm, tc = _tile(L, 256), 512
    hb = tm // _HALO

    def body(cur_ref, prev_ref, w_ref, b_ref, o_ref):
        prev8 = jnp.where(pl.program_id(1) > 0, prev_ref[...], 0.0)
        s, _ = _conv_pre(cur_ref[...], prev8, w_ref, b_ref)
        o_ref[...] = s * _sigmoid(s)

    return _call(body, "conv_fwd", (CONV_DIM // tc, L // tm),
                 [pl.BlockSpec((tm, tc), lambda j, i: (i, _XBC_COL0 + j)),
                  pl.BlockSpec((_HALO, tc), lambda j, i: (jnp.maximum(i * hb - 1, 0), _XBC_COL0 + j)),
                  pl.BlockSpec((CONV_WIDTH, tc), lambda j, i: (0, j)),
                  pl.BlockSpec((1, tc), lambda j, i: (0, j))],
                 pl.BlockSpec((tm, tc), lambda j, i: (i, j)), _sds((L, CONV_DIM), F32))(zx, zx, w, b)


def conv_bwd(dxc, zx, w, b):
    L = zx.shape[0]
    tm, tc = _tile(L, 256), 512
    nblk = L // tm
    hb = tm // _HALO

    def body(d_ref, dn_ref, cur_ref, prev_ref, next_ref, w_ref, b_ref, du_ref, acc_ref):
        i = pl.program_id(1)

        @pl.when(i == 0)
        def _():
            acc_ref[...] = jnp.zeros_like(acc_ref)

        cur = cur_ref[...]
        prev8 = jnp.where(i > 0, prev_ref[...], 0.0)
        s, shifted = _conv_pre(cur, prev8, w_ref, b_ref)
        ds_c = d_ref[...] * _silu_grad(s)
        s_n, _ = _conv_pre(next_ref[...], cur[tm - _HALO:], w_ref, b_ref)
        ds_n = jnp.where(i < nblk - 1, dn_ref[...] * _silu_grad(s_n), 0.0)
        du = ds_c * w_ref[CONV_WIDTH - 1:CONV_WIDTH, :]
        acc_ref[CONV_WIDTH - 1:CONV_WIDTH, :] += jnp.sum(ds_c * cur, axis=0, keepdims=True)
        for k in range(1, CONV_WIDTH):
            du = du + _shift_up(ds_c, ds_n, k) * w_ref[CONV_WIDTH - 1 - k:CONV_WIDTH - k, :]
            acc_ref[CONV_WIDTH - 1 - k:CONV_WIDTH - k, :] += jnp.sum(ds_c * shifted[k - 1], axis=0, keepdims=True)
        acc_ref[CONV_WIDTH:CONV_WIDTH + 1, :] += jnp.sum(ds_c, axis=0, keepdims=True)
        du_ref[...] = du.astype(BF)

    nxt = lambda j, i: (jnp.minimum((i + 1) * hb, L // _HALO - 1), j)
    return _call(body, "conv_bwd", (CONV_DIM // tc, nblk),
                 [pl.BlockSpec((tm, tc), lambda j, i: (i, j)),
                  pl.BlockSpec((_HALO, tc), nxt),
                  pl.BlockSpec((tm, tc), lambda j, i: (i, _XBC_COL0 + j)),
                  pl.BlockSpec((_HALO, tc), lambda j, i: (jnp.maximum(i * hb - 1, 0), _XBC_COL0 + j)),
                  pl.BlockSpec((_HALO, tc), lambda j, i: (jnp.minimum((i + 1) * hb, L // _HALO - 1), _XBC_COL0 + j)),
                  pl.BlockSpec((CONV_WIDTH, tc), lambda j, i: (0, j)),
                  pl.BlockSpec((1, tc), lambda j, i: (0, j))],
                 (pl.BlockSpec((tm, tc), lambda j, i: (i, j)), pl.BlockSpec((8, tc), lambda j, i: (0, j))),
                 (_sds((L, CONV_DIM), BF), _sds((8, CONV_DIM), F32)))(dxc, dxc, zx, zx, zx, w, b)


_DT_COL = (D_INNER + CONV_DIM) // 128


def dt_prep(zx, bias_pad, alog_pad):
    L = zx.shape[0]

    def body(raw_ref, b_ref, al_ref, dt_ref, acs_ref):
        v = raw_ref[...] + b_ref[...]
        dt = jnp.maximum(v, 0.0) + jnp.log(1.0 + jnp.exp(-jnp.abs(v)))
        dt_ref[...] = dt
        acs = dt * (-jnp.exp(al_ref[...]))
        rows = lax.broadcasted_iota(jnp.int32, acs.shape, 0)
        s = 1
        while s < CHUNK:
            acs = acs + jnp.where(rows >= s, pltpu.roll(acs, s, 0), 0.0)
            s *= 2
        acs_ref[...] = acs

    blk = pl.BlockSpec((CHUNK, 128), lambda i: (i, 0))
    vec = pl.BlockSpec((1, 128), lambda i: (0, 0))
    return _call(body, "dt_prep", (L // CHUNK,),
                 [pl.BlockSpec((CHUNK, 128), lambda i: (i, _DT_COL)), vec, vec], (blk, blk),
                 (_sds((L, 128), F32), _sds((L, 128), F32)))(zx, bias_pad, alog_pad)


def dt_bwd(ddt, da, dt, zx, bias_pad, alog_pad):
    L = zx.shape[0]
    tm = _tile(L, 512)

    def body(ddt_ref, da_ref, dt_ref, raw_ref, b_ref, al_ref, o_ref, acc_ref):
        @pl.when(pl.program_id(0) == 0)
        def _():
            acc_ref[...] = jnp.zeros_like(acc_ref)

        A = -jnp.exp(al_ref[...])
        dav = da_ref[...]
        dd = ddt_ref[...] + dav * A
        draw = dd * _sigmoid(raw_ref[...] + b_ref[...])
        o_ref[...] = draw.astype(BF)
        acc_ref[0:1, :] += jnp.sum(draw, axis=0, keepdims=True)
        acc_ref[1:2, :] += jnp.sum(dav * dt_ref[...], axis=0, keepdims=True) * A

    blk = pl.BlockSpec((tm, 128), lambda i: (i, 0))
    vec = pl.BlockSpec((1, 128), lambda i: (0, 0))
    return _call(body, "dt_bwd", (L // tm,),
                 [blk, blk, blk, pl.BlockSpec((tm, 128), lambda i: (i, _DT_COL)), vec, vec],
                 (blk, pl.BlockSpec((8, 128), lambda i: (0, 0))),
                 (_sds((L, 128), BF), _sds((8, 128), F32)))(ddt, da, dt, zx, bias_pad, alog_pad)


_HPG = SSM_HEADS // SSM_GROUPS
_GW = _HPG * SSM_HEADDIM
_B_COL0 = D_INNER // SSM_STATE
_C_COL0 = (D_INNER + SSM_GROUPS * SSM_STATE) // SSM_STATE


def _ssd_head(x, dtc, ac, ar, r, causal):
    xh = x[:, SSM_HEADDIM * r:SSM_HEADDIM * (r + 1)]
    acol = ac[:, r:r + 1]
    arow = ar[r:r + 1, :]
    alast = ar[r:r + 1, CHUNK - 1:CHUNK]
    lm = jnp.exp(jnp.where(causal, acol - arow, NEG))
    return xh, xh * dtc[:, r:r + 1], acol, alast, lm


def ssd_fwd(xc, dt_g, acs_g, acsT_g, d_exp):
    L = xc.shape[0]
    nc = L // CHUNK

    def body(x_ref, b_ref, c_ref, dt_ref, ac_ref, ar_ref, d_ref, y_ref, pst_ref, st_ref):
        @pl.when(pl.program_id(1) == 0)
        def _():
            st_ref[...] = jnp.zeros_like(st_ref)

        x, Bm, Cm = x_ref[...], b_ref[...], c_ref[...]
        dtc, ac, ar = dt_ref[...], ac_ref[...], ar_ref[...]
        causal = lax.broadcasted_iota(jnp.int32, (CHUNK, CHUNK), 0) >= lax.broadcasted_iota(jnp.int32, (CHUNK, CHUNK), 1)
        CB = _dot_nt(Cm, Bm)
        for r in range(_HPG):
            xh, xd, acol, alast, lm = _ssd_head(x, dtc, ac, ar, r, causal)
            P = st_ref[r]
            y = _dot_nn(CB * lm, xd) + jnp.exp(acol) * _dot_nt(Cm, P)
            y_ref[:, SSM_HEADDIM * r:SSM_HEADDIM * (r + 1)] = y + d_ref[:, SSM_HEADDIM * r:SSM_HEADDIM * (r + 1)] * xh
            pst_ref[r] = P
            st_ref[r] = P * jnp.exp(alast) + _dot_tn(xd * jnp.exp(alast - acol), Bm)

    return _call(
        body, "ssd_fwd", (SSM_GROUPS, nc),
        [pl.BlockSpec((CHUNK, _GW), lambda g, c: (c, g)),
         pl.BlockSpec((CHUNK, SSM_STATE), lambda g, c: (c, _B_COL0 + g)),
         pl.BlockSpec((CHUNK, SSM_STATE), lambda g, c: (c, _C_COL0 + g)),
         pl.BlockSpec((None, CHUNK, _HPG), lambda g, c: (g, c, 0)),
         pl.BlockSpec((None, CHUNK, _HPG), lambda g, c: (g, c, 0)),
         pl.BlockSpec((None, _HPG, CHUNK), lambda g, c: (g, 0, c)),
         pl.BlockSpec((None, 1, _GW), lambda g, c: (g, 0, 0))],
        (pl.BlockSpec((CHUNK, _GW), lambda g, c: (c, g)),
         pl.BlockSpec((None, None, _HPG, SSM_HEADDIM, SSM_STATE), lambda g, c: (c, g, 0, 0, 0))),
        (_sds((L, D_INNER), F32), _sds((nc, SSM_GROUPS, _HPG, SSM_HEADDIM, SSM_STATE), F32)),
        scratch=[pltpu.VMEM((_HPG, SSM_HEADDIM, SSM_STATE), F32)],
    )(xc, xc, xc, dt_g, acs_g, acsT_g, d_exp)


def ssd_bwd(dy, xc, dt_g, acs_g, acsT_g, pst, d_exp):
    L = xc.shape[0]
    nc = L // CHUNK

    def body(dy_ref, x_ref, b_ref, c_ref, dt_ref, ac_ref, ar_ref, pst_ref, d_ref,
             dx_ref, db_ref, dc_ref, ddt_ref, da_ref, dd_ref, dp_ref):
        @pl.when(pl.program_id(1) == 0)
        def _():
            dp_ref[...] = jnp.zeros_like(dp_ref)
            dd_ref[...] = jnp.zeros_like(dd_ref)

        dyv, x, Bm, Cm = dy_ref[...], x_ref[...], b_ref[...], c_ref[...]
        dtc, ac, ar = dt_ref[...], ac_ref[...], ar_ref[...]
        ri = lax.broadcasted_iota(jnp.int32, (CHUNK, CHUNK), 0)
        ci = lax.broadcasted_iota(jnp.int32, (CHUNK, CHUNK), 1)
        causal = ri >= ci
        lane4 = lax.broadcasted_iota(jnp.int32, (CHUNK, _HPG), 1)
        CB = _dot_nt(Cm, Bm)
        dB = jnp.zeros((CHUNK, SSM_STATE), F32)
        dC = jnp.zeros((CHUNK, SSM_STATE), F32)
        dCB = jnp.zeros((CHUNK, CHUNK), F32)
        ddt_blk = jnp.zeros((CHUNK, _HPG), F32)
        da_blk = jnp.zeros((CHUNK, _HPG), F32)
        for r in range(_HPG):
            sl = slice(SSM_HEADDIM * r, SSM_HEADDIM * (r + 1))
            xh, xd, acol, alast, lm = _ssd_head(x, dtc, ac, ar, r, causal)
            dyh = dyv[:, sl]
            P = pst_ref[r]
            dPn = dp_ref[r]
            eA = jnp.exp(acol)
            cd = jnp.exp(alast)
            dte = jnp.exp(alast - acol)
            G = CB * lm
            dZ = eA * dyh
            dzp = _dot_nn(dZ, P)
            dC = dC + dzp
            dp_ref[r] = dPn * cd + _dot_tn(dZ, Cm)
            dA_col = jnp.sum(dzp * Cm, axis=1, keepdims=True)
            BdS = _dot_nt(Bm, dPn)
            dxd = dte * BdS
            dB = dB + dte * _dot_nn(xd, dPn)
            t = jnp.sum(xd * BdS, axis=1, keepdims=True) * dte
            dA_col = dA_col - t
            dA_last = jnp.sum(t, axis=0, keepdims=True) + jnp.sum(
                jnp.sum(dPn * P, axis=1, keepdims=True), axis=0, keepdims=True) * cd
            dG = _dot_nt(dyh, xd)
            dxd = dxd + _dot_tn(G, dyh)
            dCB = dCB + dG * lm
            W = dG * G
            dA_col = dA_col + jnp.sum(W, axis=1, keepdims=True)
            dA_row = jnp.sum(jnp.where(ri == ci, dA_col, 0.0), axis=0, keepdims=True) - jnp.sum(W, axis=0, keepdims=True)
            da_col = jnp.sum(jnp.where(ci >= ri, dA_row, 0.0), axis=1, keepdims=True) + dA_last
            da_blk = jnp.where(lane4 == r, da_col, da_blk)
            ddt_blk = jnp.where(lane4 == r, jnp.sum(dxd * xh, axis=1, keepdims=True), ddt_blk)
            dx_ref[:, sl] = dxd * dtc[:, r:r + 1] + d_ref[:, sl] * dyh
        dc_ref[...] = dC + _dot_nn(dCB, Bm)
        db_ref[...] = dB + _dot_tn(dCB, Cm)
        ddt_ref[...] = ddt_blk
        da_ref[...] = da_blk
        dd_ref[...] += jnp.sum(dyv * x, axis=0, keepdims=True)

    rc = lambda g, c: (nc - 1 - c, g)
    small = pl.BlockSpec((None, CHUNK, _HPG), lambda g, c: (g, nc - 1 - c, 0))
    return _call(
        body, "ssd_bwd", (SSM_GROUPS, nc),
        [pl.BlockSpec((CHUNK, _GW), rc),
         pl.BlockSpec((CHUNK, _GW), rc),
         pl.BlockSpec((CHUNK, SSM_STATE), lambda g, c: (nc - 1 - c, _B_COL0 + g)),
         pl.BlockSpec((CHUNK, SSM_STATE), lambda g, c: (nc - 1 - c, _C_COL0 + g)),
         small, small,
         pl.BlockSpec((None, _HPG, CHUNK), lambda g, c: (g, 0, nc - 1 - c)),
         pl.BlockSpec((None, None, _HPG, SSM_HEADDIM, SSM_STATE), lambda g, c: (nc - 1 - c, g, 0, 0, 0)),
         pl.BlockSpec((None, 1, _GW), lambda g, c: (g, 0, 0))],
        (pl.BlockSpec((CHUNK, _GW), rc),
         pl.BlockSpec((CHUNK, SSM_STATE), rc),
         pl.BlockSpec((CHUNK, SSM_STATE), rc),
         small, small,
         pl.BlockSpec((None, 1, _GW), lambda g, c: (g, 0, 0))),
        (_sds((L, D_INNER), F32), _sds((L, SSM_GROUPS * SSM_STATE), F32), _sds((L, SSM_GROUPS * SSM_STATE), F32),
         _sds((SSM_GROUPS, L, _HPG), F32), _sds((SSM_GROUPS, L, _HPG), F32), _sds((SSM_GROUPS, 1, _GW), F32)),
        scratch=[pltpu.VMEM((_HPG, SSM_HEADDIM, SSM_STATE), F32)],
    )(dy, xc, xc, xc, dt_g, acs_g, acsT_g, pst, d_exp)


_NGW = D_INNER // SSM_GROUPS


def gate_norm(y, zx, nw):
    L = y.shape[0]
    tm = _tile(L, 256)

    def body(y_ref, z_ref, nw_ref, o_ref):
        for g in range(SSM_GROUPS):
            sl = slice(_NGW * g, _NGW * (g + 1))
            z = z_ref[:, sl]
            y2 = y_ref[:, sl] * (z * _sigmoid(z))
            r = lax.rsqrt(jnp.mean(y2 * y2, axis=-1, keepdims=True) + EPS)
            o_ref[:, sl] = (y2 * r * nw_ref[:, sl]).astype(BF)

    row = pl.BlockSpec((tm, D_INNER), lambda i: (i, 0))
    return _call(body, "gate_norm", (L // tm,), [row, row, pl.BlockSpec((1, D_INNER), lambda i: (0, 0))],
                 row, _sds((L, D_INNER), BF))(y, zx, nw)


def gate_norm_bwd(dyn, y, zx, nw):
    L = y.shape[0]
    tm = _tile(L, 256)

    def body(d_ref, y_ref, z_ref, nw_ref, dy_ref, dz_ref, acc_ref):
        @pl.when(pl.program_id(0) == 0)
        def _():
            acc_ref[...] = jnp.zeros_like(acc_ref)

        for g in range(SSM_GROUPS):
            sl = slice(_NGW * g, _NGW * (g + 1))
            z = z_ref[:, sl]
            yv = y_ref[:, sl]
            sg = _sigmoid(z)
            sz = z * sg
            y2 = yv * sz
            r = lax.rsqrt(jnp.mean(y2 * y2, axis=-1, keepdims=True) + EPS)
            yh = y2 * r
            d = d_ref[:, sl]
            dn = d * nw_ref[:, sl]
            dy2 = r * (dn - yh * jnp.mean(dn * yh, axis=-1, keepdims=True))
            dy_ref[:, sl] = dy2 * sz
            dz_ref[:, sl] = (dy2 * yv * sg * (1.0 + z * (1.0 - sg))).astype(BF)
            acc_ref[0:1, sl] += jnp.sum(d * yh, axis=0, keepdims=True)

    row = pl.BlockSpec((tm, D_INNER), lambda i: (i, 0))
    return _call(body, "gate_norm_bwd", (L // tm,), [row, row, row, pl.BlockSpec((1, D_INNER), lambda i: (0, 0))],
                 (row, row, pl.BlockSpec((8, D_INNER), lambda i: (0, 0))),
                 (_sds((L, D_INNER), F32), _sds((L, D_INNER), BF), _sds((8, D_INNER), F32)))(dyn, y, zx, nw)


_SCALE = HEAD_DIM ** -0.5
_REP = ATT_HEADS // KV_HEADS
_V_OFF = KV_HEADS * HEAD_DIM


def _stack_heads(ref, k):
    return jnp.concatenate([ref[:, HEAD_DIM * (k * _REP + r):HEAD_DIM * (k * _REP + r + 1)] for r in range(_REP)],
                           axis=0)


def _stack_sinks(s_ref, k):
    return jnp.concatenate([jnp.broadcast_to(s_ref[:, k * _REP + r:k * _REP + r + 1], (WINDOW, 1))
                            for r in range(_REP)], axis=0)


def _attn_probs(q4, kp, kc, sink, first):
    shape = (_REP * WINDOW, WINDOW)
    rows = jnp.bitwise_and(lax.broadcasted_iota(jnp.int32, shape, 0), WINDOW - 1)
    cols = lax.broadcasted_iota(jnp.int32, shape, 1)
    sp = jnp.where(jnp.logical_and(cols > rows, jnp.logical_not(first)), _dot_nt(q4, kp) * _SCALE, NEG)
    sc = jnp.where(cols <= rows, _dot_nt(q4, kc) * _SCALE, NEG)
    m = jnp.maximum(jnp.maximum(jnp.max(sp, axis=1, keepdims=True), jnp.max(sc, axis=1, keepdims=True)), sink)
    pp = jnp.exp(sp - m)
    pc = jnp.exp(sc - m)
    ps = jnp.exp(sink - m)
    inv = 1.0 / (jnp.sum(pp, axis=1, keepdims=True) + jnp.sum(pc, axis=1, keepdims=True) + ps)
    return pp * inv, pc * inv, ps * inv


def attn_fwd(q, kv, sinks_pad):
    L = q.shape[0]
    nb = L // WINDOW

    def body(q_ref, kc_ref, kp_ref, s_ref, o_ref):
        first = pl.program_id(0) == 0
        for k in range(KV_HEADS):
            ks = slice(HEAD_DIM * k, HEAD_DIM * (k + 1))
            vs = slice(_V_OFF + HEAD_DIM * k, _V_OFF + HEAD_DIM * (k + 1))
            pp, pc, _ = _attn_probs(_stack_heads(q_ref, k), kp_ref[:, ks], kc_ref[:, ks], _stack_sinks(s_ref, k), first)
            o4 = _dot_nn(pp, kp_ref[:, vs]) + _dot_nn(pc, kc_ref[:, vs])
            for r in range(_REP):
                h = k * _REP + r
                o_ref[:, HEAD_DIM * h:HEAD_DIM * (h + 1)] = o4[WINDOW * r:WINDOW * (r + 1)].astype(BF)

    qspec = pl.BlockSpec((WINDOW, D_MODEL), lambda i: (i, 0))
    return _call(body, "attn_fwd", (nb,),
                 [qspec, pl.BlockSpec((WINDOW, KV_DIM), lambda i: (i, 0)),
                  pl.BlockSpec((WINDOW, KV_DIM), lambda i: (jnp.maximum(i - 1, 0), 0)),
                  pl.BlockSpec((1, 128), lambda i: (0, 0))],
                 qspec, _sds((L, D_MODEL), BF))(q, kv, kv, sinks_pad)


def attn_bwd(q, kv, do, sinks_pad):
    L = q.shape[0]
    nb = L // WINDOW

    def body(q_ref, kc_ref, kp_ref, do_ref, s_ref, dq_ref, dc_ref, dp_ref, acc_ref):
        first = pl.program_id(0) == 0

        @pl.when(first)
        def _():
            acc_ref[...] = jnp.zeros_like(acc_ref)

        lane = lax.broadcasted_iota(jnp.int32, (1, 128), 1)
        dsink = jnp.zeros((1, 128), F32)
        for k in range(KV_HEADS):
            ks = slice(HEAD_DIM * k, HEAD_DIM * (k + 1))
            vs = slice(_V_OFF + HEAD_DIM * k, _V_OFF + HEAD_DIM * (k + 1))
            kp, kc, vp, vc = kp_ref[:, ks], kc_ref[:, ks], kp_ref[:, vs], kc_ref[:, vs]
            q4 = _stack_heads(q_ref, k)
            do4 = _stack_heads(do_ref, k)
            pp, pc, ps = _attn_probs(q4, kp, kc, _stack_sinks(s_ref, k), first)
            dpp = _dot_nt(do4, vp)
            dpc = _dot_nt(do4, vc)
            delta = jnp.sum(pp * dpp, axis=1, keepdims=True) + jnp.sum(pc * dpc, axis=1, keepdims=True)
            dsp = pp * (dpp - delta) * _SCALE
            dsc = pc * (dpc - delta) * _SCALE
            dq4 = _dot_nn(dsp, kp) + _dot_nn(dsc, kc)
            psd = ps * delta
            for r in range(_REP):
                h = k * _REP + r
                rs = slice(WINDOW * r, WINDOW * (r + 1))
                dq_ref[:, HEAD_DIM * h:HEAD_DIM * (h + 1)] = dq4[rs]
                dsink = dsink + jnp.where(lane == h, -jnp.sum(psd[rs], axis=0, keepdims=True), 0.0)
            dp_ref[:, ks] = _dot_tn(dsp, q4)
            dc_ref[:, ks] = _dot_tn(dsc, q4)
            dp_ref[:, vs] = _dot_tn(pp, do4)
            dc_ref[:, vs] = _dot_tn(pc, do4)
        acc_ref[0:1, :] += jnp.sum(dq_ref[...], axis=0, keepdims=True)
        acc_ref[1:2, 0:128] += dsink

    qspec = pl.BlockSpec((WINDOW, D_MODEL), lambda i: (i, 0))
    kspec = pl.BlockSpec((WINDOW, KV_DIM), lambda i: (i, 0))
    return _call(body, "attn_bwd", (nb,),
                 [qspec, kspec, pl.BlockSpec((WINDOW, KV_DIM), lambda i: (jnp.maximum(i - 1, 0), 0)), qspec,
                  pl.BlockSpec((1, 128), lambda i: (0, 0))],
                 (qspec, kspec, kspec, pl.BlockSpec((8, D_MODEL), lambda i: (0, 0))),
                 (_sds((L, D_MODEL), F32), _sds((L, KV_DIM), F32), _sds((L, KV_DIM), F32), _sds((8, D_MODEL), F32)),
                 )(q, kv, kv, do, sinks_pad)


def kv_grad_combine(parts):
    L = parts[0][0].shape[0]
    nb = L // WINDOW
    n = len(parts)

    def body(*refs):
        i = pl.program_id(0)
        o_ref, acc_ref = refs[2 * n], refs[2 * n + 1]

        @pl.when(i == 0)
        def _():
            acc_ref[...] = jnp.zeros_like(acc_ref)

        tot = refs[0][...]
        nxt = refs[1][...]
        for a in range(1, n):
            tot = tot + refs[2 * a][...]
            nxt = nxt + refs[2 * a + 1][...]
        tot = tot + jnp.where(i < nb - 1, nxt, 0.0)
        o_ref[...] = tot
        acc_ref[0:1, :] += jnp.sum(tot, axis=0, keepdims=True)

    cur = pl.BlockSpec((WINDOW, KV_DIM), lambda i: (i, 0))
    nxt = pl.BlockSpec((WINDOW, KV_DIM), lambda i: (jnp.minimum(i + 1, nb - 1), 0))
    args = [t for p in parts for t in p]
    return _call(body, "kv_grad_combine", (nb,), [cur, nxt] * n,
                 (cur, pl.BlockSpec((8, KV_DIM), lambda i: (0, 0))),
                 (_sds((L, KV_DIM), F32), _sds((8, KV_DIM), F32)))(*args)


def mod_fwd(c_all, w, b, name):
    n, _, C = w.shape

    def body(c_ref, w_ref, b_ref, o_ref, ca_ref):
        cv = c_ref[...]
        ca = cv * _sigmoid(cv)
        ca_ref[...] = ca
        o_ref[...] = _dot(ca, w_ref[...], ((1,), (0,))) + b_ref[...]

    return _call(body, name, (n,),
                 [pl.BlockSpec((N_DEV, D_MODEL), lambda i: (0, 0)),
                  pl.BlockSpec((None, D_MODEL, C), lambda i: (i, 0, 0)),
                  pl.BlockSpec((None, 1, C), lambda i: (i, 0, 0))],
                 (pl.BlockSpec((None, N_DEV, C), lambda i: (i, 0, 0)), pl.BlockSpec((N_DEV, D_MODEL), lambda i: (0, 0))),
                 (_sds((n, N_DEV, C), F32), _sds((N_DEV, D_MODEL), F32)))(c_all, w, b)


def mod_wgrad(c_act_t, dmod, name):
    n, _, C = dmod.shape
    tr = 256

    def body(ct_ref, d_ref, o_ref):
        acc = ct_ref[:, 0:1] * d_ref[0:1, :]
        for bidx in range(1, N_DEV):
            acc = acc + ct_ref[:, bidx:bidx + 1] * d_ref[bidx:bidx + 1, :]
        o_ref[...] = acc

    return _call(body, name, (n, D_MODEL // tr),
                 [pl.BlockSpec((tr, N_DEV), lambda i, j: (j, 0)),
                  pl.BlockSpec((None, N_DEV, C), lambda i, j: (i, 0, 0))],
                 pl.BlockSpec((None, tr, C), lambda i, j: (i, j, 0)), _sds((n, D_MODEL, C), F32))(c_act_t, dmod)


def _my_pos():
    return lax.axis_index("x"), lax.axis_index("y"), lax.axis_index("c")


def small_all_gather(v):
    m_per, n = v.shape

    def body(x_ref, out_ref, send_sems, recv_sems, local_sem):
        x, y, c = _my_pos()
        me, sibling = (x, y, c), (x, y, 1 - c)
        chips = [(1 - x, y), (x, 1 - y), (1 - x, 1 - y)]

        def rows(px, py, pc):
            return out_ref.at[pl.ds((4 * px + 2 * py + pc) * m_per, m_per), :]

        def copy(k, block, to, src=None):
            return pltpu.make_async_remote_copy(
                src_ref=rows(*block) if src is None else src, dst_ref=rows(*block),
                send_sem=send_sems.at[k], recv_sem=recv_sems.at[k], device_id=to, device_id_type=MESH)

        mine = pltpu.make_async_copy(x_ref, rows(*me), local_sem)
        mine.start()
        first = [copy(0, me, sibling, src=x_ref)]
        first += [copy(1 + j, me, (*chip, c), src=x_ref) for j, chip in enumerate(chips)]
        for cp in first:
            cp.start()
        passed = [copy(4 + j, (*chip, c), sibling) for j, chip in enumerate(chips)]
        for j, chip in enumerate(chips):
            copy(1 + j, (*chip, c), me).wait_recv()
            passed[j].start()
        copy(0, sibling, me).wait_recv()
        for j, chip in enumerate(chips):
            copy(4 + j, (*chip, 1 - c), me).wait_recv()
        for cp in first + passed:
            cp.wait_send()
        mine.wait()

    return pl.pallas_call(
        body, name="small_all_gather",
        out_shape=_sds((N_DEV * m_per, n), v.dtype),
        in_specs=[pl.BlockSpec(memory_space=pltpu.VMEM)],
        out_specs=pl.BlockSpec(memory_space=pltpu.VMEM),
        scratch_shapes=[pltpu.SemaphoreType.DMA((7,)), pltpu.SemaphoreType.DMA((7,)), pltpu.SemaphoreType.DMA],
        compiler_params=pltpu.CompilerParams(vmem_limit_bytes=VMEM_LIMIT),
    )(v)


def big_all_gather(arrs):
    n = len(arrs)

    def body(*refs):
        ins, outs = refs[:n], refs[n:2 * n]
        send_sems, recv_sems, local_sems = refs[2 * n], refs[2 * n + 1], refs[2 * n + 2]
        x, y, c = _my_pos()
        me, sibling = (x, y, c), (x, y, 1 - c)
        chips = [(1 - x, y), (x, 1 - y), (1 - x, 1 - y)]

        def slot(a, px, py, pc):
            return outs[a].at[4 * px + 2 * py + pc]

        def copy(a, k, block, to, src=None):
            return pltpu.make_async_remote_copy(
                src_ref=slot(a, *block) if src is None else src, dst_ref=slot(a, *block),
                send_sem=send_sems.at[7 * a + k], recv_sem=recv_sems.at[7 * a + k], device_id=to, device_id_type=MESH)

        mine = [pltpu.make_async_copy(ins[a], slot(a, *me), local_sems.at[a]) for a in range(n)]
        for cp in mine:
            cp.start()
        first = []
        for a in range(n):
            first.append(copy(a, 0, me, sibling, src=ins[a]))
            first += [copy(a, 1 + j, me, (*chip, c), src=ins[a]) for j, chip in enumerate(chips)]
        for cp in first:
            cp.start()
        passed = []
        for a in range(n):
            for j, chip in enumerate(chips):
                copy(a, 1 + j, (*chip, c), me).wait_recv()
                fwd = copy(a, 4 + j, (*chip, c), sibling)
                fwd.start()
                passed.append(fwd)
        for a in range(n):
            copy(a, 0, sibling, me).wait_recv()
            for j, chip in enumerate(chips):
                copy(a, 4 + j, (*chip, 1 - c), me).wait_recv()
        for cp in first + passed:
            cp.wait_send()
        for cp in mine:
            cp.wait()

    hbm = pl.BlockSpec(memory_space=pltpu.HBM)
    return pl.pallas_call(
        body, name="big_all_gather",
        out_shape=[_sds((N_DEV,) + a.shape, a.dtype) for a in arrs],
        in_specs=[hbm] * n, out_specs=[hbm] * n,
        scratch_shapes=[pltpu.SemaphoreType.DMA((7 * n,)), pltpu.SemaphoreType.DMA((7 * n,)),
                        pltpu.SemaphoreType.DMA((n,))],
    )(*arrs)


_FLIPS =[(fx, fy, fc) for fx in (0, 1) for fy in (0, 1) for fc in (0, 1)][1:]
_HBM = pl.BlockSpec(memory_space=pltpu.HBM)
_SEM = pl.BlockSpec(memory_space=pltpu.SEMAPHORE)
_EFFECT = pltpu.SideEffectType.DATAFLOW_SIDE_EFFECTING


def _flip(x, y, c, f):
    return (1 - x if f[0] else x), (1 - y if f[1] else y), (1 - c if f[2] else c)


def _xfer_copies(srcs, lands, send_sems, recv_sems, scatter):
    x, y, c = _my_pos()
    me = 4 * x + 2 * y + c
    copies = []
    for a in range(len(srcs)):
        for k, f in enumerate(_FLIPS):
            px, py, pc = _flip(x, y, c, f)
            src = srcs[a].at[4 * px + 2 * py + pc] if scatter else srcs[a]
            copies.append(pltpu.make_async_remote_copy(
                src_ref=src, dst_ref=lands[a].at[me], send_sem=send_sems.at[7 * a + k],
                recv_sem=recv_sems.at[7 * a + k], device_id=(px, py, pc), device_id_type=MESH))
    return copies


def _own_copies(srcs, lands, local_sems, scatter):
    x, y, c = _my_pos()
    me = 4 * x + 2 * y + c
    return [pltpu.make_async_copy(srcs[a].at[me] if scatter else srcs[a], lands[a].at[me], local_sems.at[a])
            for a in range(len(srcs))]


def xfer_start(arrs, scatter, after, name):
    n = len(arrs)
    land_shapes = [a.shape if scatter else (N_DEV,) + a.shape for a in arrs]

    def body(*refs):
        srcs, lands = refs[:n], refs[n:2 * n]
        send_sems, recv_sems, local_sems = refs[2 * n + 1], refs[2 * n + 2], refs[2 * n + 3]
        token = refs[-1]
        for cp in _xfer_copies(srcs, lands, send_sems, recv_sems, scatter):
            cp.start()
        for cp in _own_copies(srcs, lands, local_sems, scatter):
            cp.start()
        token[...] = jnp.zeros_like(token)

    out = pl.pallas_call(
        body, name=name,
        out_shape=(pltpu.SemaphoreType.DMA((7 * n,)), pltpu.SemaphoreType.DMA((7 * n,)),
                   pltpu.SemaphoreType.DMA((n,)),
                   *[pltpu.HBM(a.shape, a.dtype) for a in arrs],
                   *[pltpu.HBM(s, a.dtype) for s, a in zip(land_shapes, arrs)],
                   _sds((8, 128), F32)),
        in_specs=[_HBM] * (2 * n) + [pl.BlockSpec(memory_space=pl.ANY)],
        out_specs=(_SEM, _SEM, _SEM, *([_HBM] * (2 * n)), pl.BlockSpec(memory_space=pltpu.VMEM)),
        input_output_aliases={i: 3 + i for i in range(2 * n)},
        compiler_params=pltpu.CompilerParams(has_side_effects=_EFFECT),
    )(*[pltpu.with_memory_space_constraint(a, pltpu.HBM) for a in arrs],
      *[pltpu.with_memory_space_constraint(lax.empty(s, a.dtype), pltpu.HBM) for s, a in zip(land_shapes, arrs)],
      after)
    return (out[0], out[1], out[2], list(out[3:3 + n]), list(out[3 + n:3 + 2 * n]), scatter), out[-1]


def xfer_wait(handle, after, name):
    send_sems, recv_sems, local_sems, srcs, lands, scatter = handle
    n = len(srcs)

    def body(*refs):
        srcs_r, lands_r = refs[:n], refs[n:2 * n]
        ssem, rsem, lsem = refs[2 * n], refs[2 * n + 1], refs[2 * n + 2]
        for cp in _xfer_copies(srcs_r, lands_r, ssem, rsem, scatter):
            cp.wait_send()
            cp.wait_recv()
        for cp in _own_copies(srcs_r, lands_r, lsem, scatter):
            cp.wait()

    out = pl.pallas_call(
        body, name=name,
        out_shape=(*[pltpu.HBM(a.shape, a.dtype) for a in srcs], *[pltpu.HBM(a.shape, a.dtype) for a in lands]),
        in_specs=[_HBM] * (2 * n) + [_SEM, _SEM, _SEM, pl.BlockSpec(memory_space=pl.ANY)],
        out_specs=tuple([_HBM] * (2 * n)),
        input_output_aliases={i: i for i in range(2 * n)},
        compiler_params=pltpu.CompilerParams(has_side_effects=_EFFECT),
    )(*srcs, *lands, send_sems, recv_sems, local_sems, after)
    return list(out[n:])


def _g2_first(srcs, lands, send_sems, recv_sems):
    x, y, c = _my_pos()
    me = 4 * x + 2 * y + c
    peers = [(x, y, 1 - c), (1 - x, y, c), (x, 1 - y, c), (1 - x, 1 - y, c)]
    return [[pltpu.make_async_remote_copy(
        src_ref=srcs[a], dst_ref=lands[a].at[me], send_sem=send_sems.at[4 * a + k], recv_sem=recv_sems.at[4 * a + k],
        device_id=p, device_id_type=MESH) for k, p in enumerate(peers)] for a in range(len(srcs))]


def _g2_relay(lands, send_sems, recv_sems):
    x, y, c = _my_pos()
    chips = [(1 - x, y), (x, 1 - y), (1 - x, 1 - y)]
    out = []
    for a in range(len(lands)):
        row = []
        for j, (px, py) in enumerate(chips):
            slab = lands[a].at[4 * px + 2 * py + c]
            row.append(pltpu.make_async_remote_copy(
                src_ref=slab, dst_ref=slab, send_sem=send_sems.at[3 * a + j], recv_sem=recv_sems.at[3 * a + j],
                device_id=(x, y, 1 - c), device_id_type=MESH))
        out.append(row)
    return out


def gather2_start(arrs, after, name):
    n = len(arrs)

    def body(*refs):
        srcs, lands = refs[:n], refs[n:2 * n]
        send_sems, recv_sems, local_sems = refs[2 * n + 1], refs[2 * n + 2], refs[2 * n + 3]
        for row in _g2_first(srcs, lands, send_sems, recv_sems):
            for cp in row:
                cp.start()
        for cp in _own_copies(srcs, lands, local_sems, False):
            cp.start()
        refs[-1][...] = jnp.zeros_like(refs[-1])

    lands0 = [lax.empty((N_DEV,) + a.shape, a.dtype) for a in arrs]
    out = pl.pallas_call(
        body, name=name,
        out_shape=(pltpu.SemaphoreType.DMA((4 * n,)), pltpu.SemaphoreType.DMA((4 * n,)), pltpu.SemaphoreType.DMA((n,)),
                   *[pltpu.HBM(a.shape, a.dtype) for a in arrs], *[pltpu.HBM(l.shape, l.dtype) for l in lands0],
                   _sds((8, 128), F32)),
        in_specs=[_HBM] * (2 * n) + [pl.BlockSpec(memory_space=pl.ANY)],
        out_specs=(_SEM, _SEM, _SEM, *([_HBM] * (2 * n)), pl.BlockSpec(memory_space=pltpu.VMEM)),
        input_output_aliases={i: 3 + i for i in range(2 * n)},
        compiler_params=pltpu.CompilerParams(has_side_effects=_EFFECT),
    )(*[pltpu.with_memory_space_constraint(a, pltpu.HBM) for a in arrs],
      *[pltpu.with_memory_space_constraint(l, pltpu.HBM) for l in lands0], after)
    return dict(send1=out[0], recv1=out[1], local=out[2], srcs=list(out[3:3 + n]), lands=list(out[3 + n:3 + 2 * n])), out[-1]


def gather2_relay(handle, after, name):
    n = len(handle["lands"])

    def body(*refs):
        lands = refs[:n]
        send1, recv1 = refs[n], refs[n + 1]
        send2, recv2 = refs[n + 3], refs[n + 4]
        firsts = _g2_first([l.at[0] for l in lands], lands, send1, recv1)
        relays = _g2_relay(lands, send2, recv2)
        for a in range(n):
            for j in range(3):
                firsts[a][1 + j].wait_recv()
                relays[a][j].start()
        refs[-1][...] = jnp.zeros_like(refs[-1])

    out = pl.pallas_call(
        body, name=name,
        out_shape=(pltpu.SemaphoreType.DMA((3 * n,)), pltpu.SemaphoreType.DMA((3 * n,)),
                   *[pltpu.HBM(l.shape, l.dtype) for l in handle["lands"]], _sds((8, 128), F32)),
        in_specs=[_HBM] * n + [_SEM, _SEM, pl.BlockSpec(memory_space=pl.ANY)],
        out_specs=(_SEM, _SEM, *([_HBM] * n), pl.BlockSpec(memory_space=pltpu.VMEM)),
        input_output_aliases={i: 2 + i for i in range(n)},
        compiler_params=pltpu.CompilerParams(has_side_effects=_EFFECT),
    )(*handle["lands"], handle["send1"], handle["recv1"], after)
    new = dict(handle)
    new.update(send2=out[0], recv2=out[1], lands=list(out[2:2 + n]))
    return new, out[-1]


def gather2_wait(handle, after, name):
    n = len(handle["lands"])

    def body(*refs):
        srcs, lands = refs[:n], refs[n:2 * n]
        send1, recv1, local, send2, recv2 = refs[2 * n:2 * n + 5]
        for a, row in enumerate(_g2_first(srcs, lands, send1, recv1)):
            row[0].wait_recv()
            for cp in row:
                cp.wait_send()
        for row in _g2_relay(lands, send2, recv2):
            for cp in row:
                cp.wait_send()
                cp.wait_recv()
        for cp in _own_copies(srcs, lands, local, False):
            cp.wait()

    out = pl.pallas_call(
        body, name=name,
        out_shape=(*[pltpu.HBM(a.shape, a.dtype) for a in handle["srcs"]],
                   *[pltpu.HBM(l.shape, l.dtype) for l in handle["lands"]]),
        in_specs=[_HBM] * (2 * n) + [_SEM] * 5 + [pl.BlockSpec(memory_space=pl.ANY)],
        out_specs=tuple([_HBM] * (2 * n)),
        input_output_aliases={i: i for i in range(2 * n)},
        compiler_params=pltpu.CompilerParams(has_side_effects=_EFFECT),
    )(*handle["srcs"], *handle["lands"], handle["send1"], handle["recv1"], handle["local"], handle["send2"],
      handle["recv2"], after)
    return list(out[n:])


def adamw(parts, w, m, v, name, row0=0, prev=None):
    r_tot, C = w.shape
    n_parts, R = parts.shape[0], parts.shape[1]
    row_bytes = 2 * (n_parts * C * parts.dtype.itemsize + 7 * C * 4)
    tr = R
    for cand in (512, 352, 256, 176, 128, 64):
        if R % cand == 0 and row0 % cand == 0 and R > cand and cand * row_bytes <= ADAMW_VMEM_BUDGET:
            tr = cand
            break
    tc = C
    if tr == R and R * row_bytes > ADAMW_VMEM_BUDGET:
        assert row0 == 0 and R == r_tot
        tc = next(t for t in (512, 256, 128) if C % t == 0 and R * row_bytes * t // C <= ADAMW_VMEM_BUDGET)
    assert row0 % tr == 0 and (tr % 8 == 0 or (tr == r_tot and row0 == 0))
    blk0 = row0 // tr
    c1 = 1.0 / (1.0 - ADAM_B1 ** ADAM_STEP)
    c2 = 1.0 / (1.0 - ADAM_B2 ** ADAM_STEP)

    def body(p_ref, w_ref, m_ref, v_ref, *rest):
        g_ref, d_ref, nm_ref, nv_ref = rest[-4:]
        g = p_ref[0].astype(F32)
        for k in range(1, n_parts):
            g = g + p_ref[k].astype(F32)
        nm = ADAM_B1 * m_ref[...] + (1.0 - ADAM_B1) * g
        nv = ADAM_B2 * v_ref[...] + (1.0 - ADAM_B2) * (g * g)
        g_ref[...] = g
        nm_ref[...] = nm
        nv_ref[...] = nv
        d_ref[...] = -ADAM_LR * ((nm * c1) / (jnp.sqrt(nv * c2) + ADAM_EPS) + ADAM_WD * w_ref[...])

    if tc == C:
        grid = (R // tr,)
        blk = pl.BlockSpec((tr, C), lambda i: (i + blk0, 0))
        p_spec = pl.BlockSpec((n_parts, tr, C), lambda i: (0, i, 0))
    else:
        grid = (C // tc,)
        blk = pl.BlockSpec((R, tc), lambda i: (0, i))
        p_spec = pl.BlockSpec((n_parts, R, tc), lambda i: (0, 0, i))
    in_specs = [p_spec, blk, blk, blk]
    args = [parts, w, m, v]
    aliases = {}
    if prev is not None:
        in_specs += [pl.BlockSpec(memory_space=pl.ANY)] * 4
        args += list(prev)
        aliases = {4 + k: k for k in range(4)}
    return pl.pallas_call(
        body, name=name, grid=grid, in_specs=in_specs, out_specs=(blk, blk, blk, blk),
        out_shape=tuple(_sds((r_tot, C), F32) for _ in range(4)), input_output_aliases=aliases,
        compiler_params=pltpu.CompilerParams(vmem_limit_bytes=VMEM_LIMIT))(*args)


def _ffn_fwd(x, nw, sh, sc, g, wt_gu, w_dn):
    h = norm_mod(x, nw, sh, sc, "ffn_norm")
    gp, up, a = ffn_up(h, wt_gu)
    y, xn = matmul(a, w_dn, "nn", BF, "ffn_down", res=x, gate=g, coef=0.5)
    return xn, (x, h, gp, up, a, y)


def _ffn_bwd(dxo, saved, nw, sc, g, wt_gu, w_dn):
    x, h, gp, up, a, y = saved
    dy, acc1 = resid_gate_bwd(dxo, y, g, 0.5, "ffn_gate_bwd")
    d_wdn = matmul(a, dy, "tn", BF, "ffn_down_wgrad")
    dg, du = ffn_down_dgrad(dy, w_dn, gp, up)
    d_wt = ffn_up_wgrad(dg, du, h)
    dh = ffn_up_dgrad(dg, du, wt_gu)
    dx, acc2 = norm_mod_bwd(x, dh, dxo, nw, sc, "ffn_norm_bwd")
    return dx, d_wt, d_wdn, (acc2[0], acc2[1], acc1[0]), acc2[2]


def _group_layout(a):
    L = a.shape[0]
    return a[:, :SSM_HEADS].reshape(L, SSM_GROUPS, _HPG).transpose(1, 0, 2)


def _ungroup_layout(a):
    L = a.shape[1]
    return jnp.pad(a.transpose(1, 0, 2).reshape(L, SSM_HEADS), ((0, 0), (0, 128 - SSM_HEADS)))


def _pad_row(vec, n=128):
    return jnp.pad(vec.reshape(1, -1), ((0, 0), (0, n - vec.shape[-1])))


def _mamba_fwd(x, nw, sh, sc, g, p):
    h = norm_mod(x, nw, sh, sc, "mix_norm")
    zx = matmul(h, p["w_in_t"], "nt", F32, "ssm_in")
    xc = conv_fwd(zx, p["conv_w"], p["conv_b"])
    dt, acs = dt_prep(zx, p["dt_bias"], p["a_log"])
    dt_g, acs_g = _group_layout(dt), _group_layout(acs)
    acs_t = acs_g.transpose(0, 2, 1)
    y, pst = ssd_fwd(xc, dt_g, acs_g, acs_t, p["d_exp"])
    yn = gate_norm(y, zx, p["norm_w"])
    yo, xn = matmul(yn, p["w_out"], "nn", BF, "ssm_out", res=x, gate=g, coef=1.0)
    return xn, (x, h, zx, xc, dt, dt_g, acs_g, acs_t, y, pst, yn, yo)


def _mamba_bwd(dxo, saved, nw, sc, g, p):
    x, h, zx, xc, dt, dt_g, acs_g, acs_t, y, pst, yn, yo = saved
    dyo, acc1 = resid_gate_bwd(dxo, yo, g, 1.0, "mix_gate_bwd")
    d_wout = matmul(yn, dyo, "tn", BF, "ssm_out_wgrad")
    dyn = matmul(dyo, p["w_out"], "nt", F32, "ssm_out_dgrad")
    dy, dz, accn = gate_norm_bwd(dyn, y, zx, p["norm_w"])
    dxs, dB, dC, ddt_g, da_g, dd = ssd_bwd(dy, xc, dt_g, acs_g, acs_t, pst, p["d_exp"])
    dxc = jnp.concatenate([dxs, dB, dC], axis=1)
    du, accc = conv_bwd(dxc, zx, p["conv_w"], p["conv_b"])
    draw, accdt = dt_bwd(_ungroup_layout(ddt_g), _ungroup_layout(da_g), dt, zx, p["dt_bias"], p["a_log"])
    dzx = jnp.concatenate([dz, du, draw], axis=1)
    d_win = matmul(dzx, h, "tn", BF, "ssm_in_wgrad")[:IN_PROJ]
    dh = matmul(dzx, p["w_in_t"], "nn", F32, "ssm_in_dgrad")
    dx, acc2 = norm_mod_bwd(x, dh, dxo, nw, sc, "mix_norm_bwd")
    small = dict(conv_w=accc[:CONV_WIDTH], conv_b=accc[CONV_WIDTH], dt_bias=accdt[0, :SSM_HEADS],
                 a_log=accdt[1, :SSM_HEADS], d=dd.reshape(SSM_HEADS, SSM_HEADDIM).sum(-1), norm_w=accn[0])
    return dx, d_win, d_wout, (acc2[0], acc2[1], acc1[0]), acc2[2], small


def _attn_layer_fwd(x, nw, sh, sc, g, p, kv):
    h = norm_mod(x, nw, sh, sc, "mix_norm")
    q = matmul(h, p["w_q"], "nn", F32, "attn_q", bias=p["b_q"])
    o = attn_fwd(q, kv, p["sinks"])
    yo, xn = matmul(o, p["w_o"], "nn", BF, "attn_o", bias=p["b_o"], res=x, gate=g, coef=1.0)
    return xn, (x, h, q, o, yo)


def _attn_layer_bwd(dxo, saved, nw, sc, g, p, kv):
    x, h, q, o, yo = saved
    dyo, acc1 = resid_gate_bwd(dxo, yo, g, 1.0, "mix_gate_bwd")
    d_wo = matmul(o, dyo, "tn", BF, "attn_o_wgrad")
    do = matmul(dyo, p["w_o"], "nt", F32, "attn_o_dgrad")
    dq, dkv_c, dkv_p, acca = attn_bwd(q, kv, do, p["sinks"])
    d_wq = matmul(h, dq, "tn", BF, "attn_q_wgrad")
    dh = matmul(dq, p["w_q"], "nt", F32, "attn_q_dgrad")
    dx, acc2 = norm_mod_bwd(x, dh, dxo, nw, sc, "mix_norm_bwd")
    small = dict(b_q=acca[0], sinks=acca[1, :ATT_HEADS], b_o=acc1[1])
    return dx, d_wq, d_wo, (acc2[0], acc2[1], acc1[0]), acc2[2], small, (dkv_c, dkv_p)


def _pack_rows(pieces):
    rows, spans, off = [], [], 0
    for a in pieces:
        flat = a.reshape(-1).astype(F32)
        n = -(-flat.shape[0] // D_MODEL)
        rows.append(jnp.pad(flat, (0, n * D_MODEL - flat.shape[0])).reshape(n, D_MODEL))
        spans.append((off, a.shape))
        off += n
    pad = -off % 8
    if pad:
        rows.append(jnp.zeros((pad, D_MODEL), F32))
    return jnp.concatenate(rows, axis=0), spans, off + pad


def _unpack_rows(g, spans):
    out = []
    for off, shape in spans:
        size = 1
        for s in shape:
            size *= s
        n = -(-size // D_MODEL)
        out.append(g[:, off:off + n].reshape(N_DEV, n * D_MODEL)[:, :size].reshape((N_DEV,) + tuple(shape)))
    return out


def _unshard_last(g):
    nd = g.ndim
    perm = tuple(range(1, nd - 1)) + (0, nd - 1)
    t = g.transpose(perm)
    return t.reshape(t.shape[:-2] + (N_DEV * g.shape[-1],))


def _shard_last(a, me):
    s = a.shape[-1] // N_DEV
    return lax.dynamic_slice_in_dim(a, me * s, s, axis=a.ndim - 1)


def kernel(x, c, ffn_norm_w, ffn_w_gu, ffn_w_down, mod_w, mod_b, mix_norm_w, ssm_w_in, ssm_conv_w, ssm_conv_b, ssm_dt_bias, ssm_a_log, ssm_d, ssm_norm_w, ssm_w_out, kv_norm_w, kv_mod_w, kv_mod_b, w_kv, b_kv, attn_w_q, attn_b_q, attn_sinks, attn_w_o, attn_b_o, final_norm_w, loss_target, m_ffn_norm_w, m_ffn_w_gu, m_ffn_w_down, m_mod_w, m_mod_b, m_mix_norm_w, m_ssm_w_in, m_ssm_conv_w, m_ssm_conv_b, m_ssm_dt_bias, m_ssm_a_log, m_ssm_d, m_ssm_norm_w, m_ssm_w_out, m_kv_norm_w, m_kv_mod_w, m_kv_mod_b, m_w_kv, m_b_kv, m_attn_w_q, m_attn_b_q, m_attn_sinks, m_attn_w_o, m_attn_b_o, m_final_norm_w, v_ffn_norm_w, v_ffn_w_gu, v_ffn_w_down, v_mod_w, v_mod_b, v_mix_norm_w, v_ssm_w_in, v_ssm_conv_w, v_ssm_conv_b, v_ssm_dt_bias, v_ssm_a_log, v_ssm_d, v_ssm_norm_w, v_ssm_w_out, v_kv_norm_w, v_kv_mod_w, v_kv_mod_b, v_w_kv, v_b_kv, v_attn_w_q, v_attn_b_q, v_attn_sinks, v_attn_w_o, v_attn_b_o, v_final_norm_w):
    D = D_MODEL
    me = 4 * lax.axis_index("x") + 2 * lax.axis_index("y") + lax.axis_index("c")
    xs = x[0]
    target = loss_target[0]
    mod_cols = mod_w.shape[-1]
    kvm_cols = kv_mod_w.shape[-1]

    def fence(arrs):
        tot = jnp.zeros((1, 1), F32)
        for a in arrs:
            tot = tot + lax.slice(a, (0,) * a.ndim, (1,) * a.ndim).reshape(1, 1).astype(F32)
        return jnp.broadcast_to(tot, (8, 128))

    packed, spans, _ = _pack_rows([c, ffn_norm_w, ssm_conv_w, ssm_conv_b, ssm_norm_w])
    nrow = packed.shape[0]
    g1 = small_all_gather(packed).reshape(N_DEV, nrow, D)
    c_all, fnw_g, cw_g, cb_g, snw_g = _unpack_rows(g1, spans)
    c_all = c_all.reshape(N_DEV, D)
    ffn_nw = _unshard_last(fnw_g)
    conv_w = _unshard_last(cw_g)
    conv_b = _unshard_last(cb_g)
    ssm_nw = _unshard_last(snw_g)

    mod_b_loc = lax.dynamic_slice_in_dim(mod_b, me * mod_cols, mod_cols, axis=1).reshape(DEPTH, 1, mod_cols)
    kvb_loc = lax.dynamic_slice_in_dim(kv_mod_b, me * kvm_cols, kvm_cols, axis=0).reshape(1, 1, kvm_cols)
    modp, c_act = mod_fwd(c_all, mod_w, mod_b_loc, "mod_fwd")
    kvmp, _ = mod_fwd(c_all, kv_mod_w.reshape(1, D, kvm_cols), kvb_loc, "kv_mod_fwd")
    packed2, spans2, _ = _pack_rows([modp, kvmp])
    nrow2 = packed2.shape[0]
    g2 = small_all_gather(packed2).reshape(N_DEV, nrow2, D)
    modp_g, kvmp_g = _unpack_rows(g2, spans2)
    mod_all = modp_g.transpose(1, 2, 0, 3).reshape(DEPTH, N_DEV, N_MOD * D)
    kvm_all = kvmp_g.transpose(1, 2, 0, 3).reshape(N_DEV, 2 * D)
    mod_me = lax.dynamic_index_in_dim(mod_all, me, axis=1, keepdims=False).reshape(DEPTH, N_MOD, 1, D)
    kvm_me = lax.dynamic_index_in_dim(kvm_all, me, axis=0, keepdims=False).reshape(2, 1, D)

    gu_t = jnp.swapaxes(ffn_w_gu, 2, 3)
    win_t = jnp.transpose(ssm_w_in, (2, 0, 1))
    S = gu_t.shape[2]
    s_in = win_t.shape[0]
    r_dn = ffn_w_down.shape[2]
    r_mix = ssm_w_out.shape[1]
    r_at = attn_w_q.shape[1]

    def layer_pack(k):
        arrs = [gu_t[k, 0].astype(BF), gu_t[k, 1].astype(BF), ffn_w_down[k, 0].astype(BF), ffn_w_down[k, 1].astype(BF)]
        if k < N_A:
            arrs += [ssm_w_out[k].astype(BF), win_t[:, k].astype(BF)]
        else:
            arrs += [attn_w_q[k - N_A].astype(BF), attn_w_o[k - N_A].astype(BF)]
        if k == N_A:
            arrs.append(w_kv.astype(BF))
        return arrs

    packs = [layer_pack(k) for k in range(DEPTH)]
    gathered = [None] * DEPTH
    first = big_all_gather([packs[0][0], packs[0][2]])
    pending = [None] * DEPTH
    pend_mix, tok_next = gather2_start(packs[0][4:6], fence([g2, first[0]]), "gather_start_0m")
    pending[0], tok_next = gather2_start([packs[0][1], packs[0][3]], tok_next, "gather_start_0")
    for k in range(1, DEPTH):
        pending[k], tok_next = gather2_start(packs[k], tok_next, "gather_start_%d" % k)
    gathered[0] = [first[0], None, first[1], None, None, None]

    def wt_gu_full(i, j):
        return gathered[i][j].reshape(N_DEV * S, D)

    def w_dn_full(i, j):
        return gathered[i][2 + j].reshape(D_FF, D)

    def mix_rows(i, a):
        return gathered[i][4 + a].reshape(-1, D)

    def mamba_params(j):
        w_in_t = jnp.pad(mix_rows(j, 1), ((0, IN_PROJ_PAD - IN_PROJ), (0, 0)))
        return dict(w_in_t=w_in_t, w_out=mix_rows(j, 0), conv_w=conv_w[j], conv_b=conv_b[j].reshape(1, -1),
                    dt_bias=_pad_row(ssm_dt_bias[j]), a_log=_pad_row(ssm_a_log[j]),
                    d_exp=jnp.repeat(ssm_d[j], SSM_HEADDIM).reshape(SSM_GROUPS, 1, _GW),
                    norm_w=ssm_nw[j].reshape(1, -1))

    def attn_params(j):
        return dict(w_q=mix_rows(N_A + j, 0), w_o=mix_rows(N_A + j, 1),
                    b_q=attn_b_q[j].reshape(1, -1), b_o=attn_b_o[j].reshape(1, -1), sinks=_pad_row(attn_sinks[j]))

    saved = []
    kv = None
    kv_saved = None
    w_kv_full = None
    xcur = xs
    for i in range(DEPTH):
        if i >= 1:
            gathered[i] = gather2_wait(pending[i], xcur, "gather_wait_%d" % i)
        md = mod_me[i]
        if i == 0:
            md = md + tok_next[0, 0]
        if i == N_A:
            w_kv_full = gathered[N_A][6].reshape(D, KV_DIM)
            h_kv = norm_mod(xcur, kv_norm_w.reshape(1, D), kvm_me[0], kvm_me[1], "kv_norm")
            kv = matmul(h_kv, w_kv_full, "nn", F32, "kv_proj", bias=b_kv.reshape(1, -1))
            kv_saved = (xcur, h_kv)
        x1, s1 = _ffn_fwd(xcur, ffn_nw[i, 0].reshape(1, D), md[0], md[1], md[2], wt_gu_full(i, 0), w_dn_full(i, 0))
        gm = md[5]
        if i == 0:
            pend_mix, tok_r = gather2_relay(pend_mix, x1, "gather_relay_0m")
            gathered[0][4:6] = gather2_wait(pend_mix, tok_r, "gather_wait_0m")
            pending[0], tok_r = gather2_relay(pending[0], x1, "gather_relay_0")
            gm = gm + tok_r[0, 0]
        if i < N_A:
            pm = mamba_params(i)
            x2, s2 = _mamba_fwd(x1, mix_norm_w[i].reshape(1, D), md[3], md[4], gm, pm)
        else:
            pm = attn_params(i - N_A)
            x2, s2 = _attn_layer_fwd(x1, mix_norm_w[i].reshape(1, D), md[3], md[4], gm, pm, kv)
        g2f = md[8]
        if i + 1 < DEPTH:
            pending[i + 1], tok_r = gather2_relay(pending[i + 1], x2, "gather_relay_%d" % (i + 1))
            g2f = g2f + tok_r[0, 0]
        if i == 0:
            rest = gather2_wait(pending[0], x2, "gather_wait_0")
            gathered[0] = [first[0], rest[0], first[1], rest[1]] + gathered[0][4:6]
        x3, s3 = _ffn_fwd(x2, ffn_nw[i, 1].reshape(1, D), md[6], md[7], g2f, wt_gu_full(i, 1), w_dn_full(i, 1))
        saved.append((s1, s2, s3, pm))
        xcur = x3

    dx, accf = final_loss(xcur, final_norm_w.reshape(1, D), target)
    d_mod = [None] * DEPTH
    d_ffn_nw = [[None, None] for _ in range(DEPTH)]
    d_mix_nw = [None] * DEPTH
    sm_m, sm_a = [None] * N_A, [None] * N_A
    kv_parts = [None] * N_A
    d_kvm = d_kv_nw = d_bkv = None
    exchanges = []
    tok = None

    def send(arrs, tag, after=None):
        handle, t = xfer_start(arrs, True, dx if after is None else after, "exch_start_%s" % tag)
        exchanges.append((handle, tag))
        return t

    def ffn_slabs(d_wt, d_wdn):
        return [d_wt.reshape(N_DEV, S, D), d_wdn.reshape(N_DEV, r_dn, D)]

    for i in reversed(range(DEPTH)):
        md = mod_me[i]
        s1, s2, s3, pm = saved[i]
        g2 = md[8] if tok is None else md[8] + tok[0, 0]
        dx, d_wt, d_wdn, m2, d_ffn_nw[i][1] = _ffn_bwd(
            dx, s3, ffn_nw[i, 1].reshape(1, D), md[7], g2, wt_gu_full(i, 1), w_dn_full(i, 1))
        tok = send(ffn_slabs(d_wt, d_wdn), "f%d1" % i)
        gm = md[5] + tok[0, 0]
        if i < N_A:
            dx, d_in, d_out, mm_, d_mix_nw[i], sm_m[i] = _mamba_bwd(dx, s2, mix_norm_w[i].reshape(1, D), md[4], gm, pm)
            tok = send([d_in.reshape(N_DEV, s_in, D), d_out.reshape(N_DEV, r_mix, D)], "m%d" % i)
        else:
            j = i - N_A
            dx, d_q, d_o, mm_, d_mix_nw[i], sm_a[j], kv_parts[j] = _attn_layer_bwd(
                dx, s2, mix_norm_w[i].reshape(1, D), md[4], gm, pm, kv)
            tok = send([d_q.reshape(N_DEV, r_at, D), d_o.reshape(N_DEV, r_at, D)], "m%d" % i)
        g1 = md[2] + tok[0, 0]
        dx, d_wt, d_wdn, m1, d_ffn_nw[i][0] = _ffn_bwd(
            dx, s1, ffn_nw[i, 0].reshape(1, D), md[1], g1, wt_gu_full(i, 0), w_dn_full(i, 0))
        d_mod[i] = jnp.concatenate(list(m1) + list(mm_) + list(m2), axis=0)
        last = ffn_slabs(d_wt, d_wdn)
        if i == N_A:
            x_kv, h_kv = kv_saved
            dkv, acck = kv_grad_combine(kv_parts)
            d_bkv = acck[0]
            d_kv_w = matmul(h_kv, dkv, "tn", BF, "kv_wgrad")
            dh_kv = matmul(dkv, w_kv_full, "nt", F32, "kv_dgrad")
            dx, acc_kv = norm_mod_bwd(x_kv, dh_kv, dx, kv_norm_w.reshape(1, D), kvm_me[1], "kv_norm_bwd")
            d_kvm = jnp.concatenate([acc_kv[0], acc_kv[1]], axis=0)
            d_kv_nw = acc_kv[2]
            last.append(d_kv_w.reshape(N_DEV, -1, KV_DIM))
        if i > 0:
            tok = send(last, "f%d0" % i)
    grad_x = dx.reshape(x.shape)

    small_list = [
        jnp.stack(d_mod, 0), d_kvm,
        jnp.stack([jnp.stack(r, 0) for r in d_ffn_nw], 0),
        jnp.stack(d_mix_nw, 0),
        jnp.stack([s["conv_w"] for s in sm_m], 0), jnp.stack([s["conv_b"] for s in sm_m], 0),
        jnp.stack([s["dt_bias"] for s in sm_m], 0), jnp.stack([s["a_log"] for s in sm_m], 0),
        jnp.stack([s["d"] for s in sm_m], 0), jnp.stack([s["norm_w"] for s in sm_m], 0),
        d_kv_nw, d_bkv,
        jnp.stack([s["b_q"] for s in sm_a], 0), jnp.stack([s["sinks"] for s in sm_a], 0),
        jnp.stack([s["b_o"] for s in sm_a], 0), accf[0], accf[1],
    ]
    packed3, spans3, _ = _pack_rows(small_list)
    nrow3 = packed3.shape[0]
    g3 = small_all_gather(packed3).reshape(N_DEV, nrow3, D)
    tok_last = send(last, "f00", after=g3)
    (p_mod, p_kvm, p_fnw, p_mnw, p_cw, p_cb, p_dtb, p_al, p_d, p_snw, p_kvnw, p_bkv, p_bq, p_sk, p_bo, p_fin,
     p_loss) = _unpack_rows(g3, spans3)

    loss = 0.5 / D * jnp.sum(p_loss)

    c_act_t = c_act.T
    dmod_loc = _shard_last(p_mod, me).transpose(1, 0, 2)
    dkvm_loc = _shard_last(p_kvm, me).reshape(1, N_DEV, kvm_cols) + tok_last[0, 0]
    gp_mod_w = mod_wgrad(c_act_t, dmod_loc, "mod_wgrad")
    gp_kvm_w = mod_wgrad(c_act_t, dkvm_loc, "kv_mod_wgrad")[0]

    def as_parts_single(a):
        return a[None]

    def upd(name, parts, w, m, v):
        shp = w.shape
        c_last = shp[-1]
        out = adamw(parts.reshape(parts.shape[0], -1, c_last), w.reshape(-1, c_last), m.reshape(-1, c_last),
                    v.reshape(-1, c_last), "adamw_" + name)
        return tuple(o.reshape(shp) for o in out)

    views = {
        "ffn_w_gu": [jnp.swapaxes(t, 2, 3).reshape(-1, D) for t in (ffn_w_gu, m_ffn_w_gu, v_ffn_w_gu)],
        "ffn_w_down": [t.reshape(-1, D) for t in (ffn_w_down, m_ffn_w_down, v_ffn_w_down)],
        "ssm_w_out": [t.reshape(-1, D) for t in (ssm_w_out, m_ssm_w_out, v_ssm_w_out)],
        "attn_w_q": [t.reshape(-1, D) for t in (attn_w_q, m_attn_w_q, v_attn_w_q)],
        "attn_w_o": [t.reshape(-1, D) for t in (attn_w_o, m_attn_w_o, v_attn_w_o)],
    }
    filled = {k: None for k in views}

    def upd_rows(name, parts, row0):
        w, m, v = views[name]
        filled[name] = adamw(parts, w, m, v, "adamw_" + name, row0=row0, prev=filled[name])
        return filled[name][3]

    res = {}
    res["ffn_norm_w"] = upd("ffn_norm_w", _shard_last(p_fnw, me), ffn_norm_w, m_ffn_norm_w, v_ffn_norm_w)
    res["mod_w"] = upd("mod_w", as_parts_single(gp_mod_w), mod_w, m_mod_w, v_mod_w)
    res["mod_b"] = upd("mod_b", p_mod, mod_b, m_mod_b, v_mod_b)
    res["mix_norm_w"] = upd("mix_norm_w", p_mnw, mix_norm_w, m_mix_norm_w, v_mix_norm_w)
    res["ssm_conv_w"] = upd("ssm_conv_w", _shard_last(p_cw, me), ssm_conv_w, m_ssm_conv_w, v_ssm_conv_w)
    res["ssm_conv_b"] = upd("ssm_conv_b", _shard_last(p_cb, me), ssm_conv_b, m_ssm_conv_b, v_ssm_conv_b)
    res["ssm_dt_bias"] = upd("ssm_dt_bias", p_dtb, ssm_dt_bias, m_ssm_dt_bias, v_ssm_dt_bias)
    res["ssm_a_log"] = upd("ssm_a_log", p_al, ssm_a_log, m_ssm_a_log, v_ssm_a_log)
    res["ssm_d"] = upd("ssm_d", p_d, ssm_d, m_ssm_d, v_ssm_d)
    res["ssm_norm_w"] = upd("ssm_norm_w", _shard_last(p_snw, me), ssm_norm_w, m_ssm_norm_w, v_ssm_norm_w)
    res["kv_norm_w"] = upd("kv_norm_w", p_kvnw.reshape(N_DEV, 1, D), kv_norm_w.reshape(1, D),
                           m_kv_norm_w.reshape(1, D), v_kv_norm_w.reshape(1, D))
    res["kv_mod_w"] = upd("kv_mod_w", as_parts_single(gp_kvm_w), kv_mod_w, m_kv_mod_w, v_kv_mod_w)
    res["kv_mod_b"] = upd("kv_mod_b", p_kvm.reshape(N_DEV, 1, 2 * D), kv_mod_b.reshape(1, -1),
                          m_kv_mod_b.reshape(1, -1), v_kv_mod_b.reshape(1, -1))
    res["b_kv"] = upd("b_kv", p_bkv.reshape(N_DEV, 1, KV_DIM), b_kv.reshape(1, -1), m_b_kv.reshape(1, -1),
                      v_b_kv.reshape(1, -1))
    res["attn_b_q"] = upd("attn_b_q", p_bq, attn_b_q, m_attn_b_q, v_attn_b_q)
    res["attn_sinks"] = upd("attn_sinks", p_sk, attn_sinks, m_attn_sinks, v_attn_sinks)
    res["attn_b_o"] = upd("attn_b_o", p_bo, attn_b_o, m_attn_b_o, v_attn_b_o)
    res["final_norm_w"] = upd("final_norm_w", p_fin.reshape(N_DEV, 1, D), final_norm_w.reshape(1, D),
                              m_final_norm_w.reshape(1, D), v_final_norm_w.reshape(1, D))

    chain = fence([dx, tok_last] + [t[3] for t in res.values()])
    r_in_parts = [None] * N_A
    r_kv = None
    for handle, tag in exchanges:
        got = xfer_wait(handle, chain, "exch_wait_%s" % tag)
        i = int(tag[1])
        if tag[0] == "f":
            jf = int(tag[2])
            done = [upd_rows("ffn_w_gu", got[0], (2 * i + jf) * S), upd_rows("ffn_w_down", got[1], (2 * i + jf) * r_dn)]
            if len(got) > 2:
                res["w_kv"] = upd("w_kv", got[2], w_kv, m_w_kv, v_w_kv)
                done.append(res["w_kv"][3])
        elif i < N_A:
            r_in_parts[i] = got[0]
            done = [upd_rows("ssm_w_out", got[1], i * r_mix)]
            if i == 0:
                win_out = adamw(jnp.stack(r_in_parts, axis=2).reshape(N_DEV, s_in * N_A, D),
                                *[jnp.transpose(t, (2, 0, 1)).reshape(-1, D) for t in (ssm_w_in, m_ssm_w_in, v_ssm_w_in)],
                                "adamw_ssm_w_in")
                res["ssm_w_in"] = tuple(jnp.transpose(t.reshape(win_t.shape), (1, 2, 0)) for t in win_out)
                done.append(win_out[3])
        else:
            done = [upd_rows("attn_w_q", got[0], (i - N_A) * r_at), upd_rows("attn_w_o", got[1], (i - N_A) * r_at)]
        chain = fence(done)

    res["ffn_w_gu"] = tuple(jnp.swapaxes(t.reshape(gu_t.shape), 2, 3) for t in filled["ffn_w_gu"])
    res["ffn_w_down"] = tuple(t.reshape(ffn_w_down.shape) for t in filled["ffn_w_down"])
    res["ssm_w_out"] = tuple(t.reshape(ssm_w_out.shape) for t in filled["ssm_w_out"])
    res["attn_w_q"] = tuple(t.reshape(attn_w_q.shape) for t in filled["attn_w_q"])
    res["attn_w_o"] = tuple(t.reshape(attn_w_o.shape) for t in filled["attn_w_o"])

    names = ["ffn_norm_w", "ffn_w_gu", "ffn_w_down", "mod_w", "mod_b", "mix_norm_w", "ssm_w_in", "ssm_conv_w",
             "ssm_conv_b", "ssm_dt_bias", "ssm_a_log", "ssm_d", "ssm_norm_w", "ssm_w_out", "kv_norm_w", "kv_mod_w",
             "kv_mod_b", "w_kv", "b_kv", "attn_w_q", "attn_b_q", "attn_sinks", "attn_w_o", "attn_b_o", "final_norm_w"]
    vec_shapes = {"kv_norm_w": (D,), "kv_mod_b": (2 * D,), "b_kv": (KV_DIM,), "final_norm_w": (D,)}
    outs = [loss, grad_x]
    for k in range(4):
        for nme in names:
            t = res[nme][k]
            if nme in vec_shapes:
                t = t.reshape(vec_shapes[nme])
            outs.append(t)
    return tuple(outs)
```

```python
import functools

import jax
import jax.numpy as jnp
from jax import lax
from jax.experimental import pallas as pl
from jax.experimental.pallas import tpu as pltpu

F32 = jnp.float32
BF = jnp.bfloat16
MESH = pl.DeviceIdType.MESH

N_DEV = 8
D_MODEL = 1024
DEPTH = 4
N_A = 2
EPS = 1e-5
N_MOD = 9
D_FF = 2816
D_INNER = 2048
SSM_HEADDIM = 64
SSM_HEADS = 32
SSM_GROUPS = 8
SSM_STATE = 128
CONV_WIDTH = 4
CHUNK = 512
CONV_DIM = D_INNER + 2 * SSM_GROUPS * SSM_STATE
IN_PROJ = D_INNER + CONV_DIM + SSM_HEADS
IN_PROJ_PAD = D_INNER + CONV_DIM + 128
ATT_HEADS = 16
KV_HEADS = 4
HEAD_DIM = 64
WINDOW = 128
KV_DIM = 2 * KV_HEADS * HEAD_DIM

ADAM_LR = 0.001
ADAM_B1 = 0.9
ADAM_B2 = 0.999
ADAM_EPS = 1e-08
ADAM_WD = 0.01
ADAM_STEP = 10

VMEM_LIMIT = 48 * 2 ** 20
ADAMW_VMEM_BUDGET = 24 * 2 ** 20
NEG = -1e30


def _call(body, name, grid, in_specs, out_specs, out_shape, scratch=()):
    return pl.pallas_call(
        body, name=name, grid=grid, in_specs=in_specs, out_specs=out_specs, out_shape=out_shape,
        scratch_shapes=list(scratch),
        compiler_params=pltpu.CompilerParams(vmem_limit_bytes=VMEM_LIMIT))


def _tile(n, cap):
    t = (cap // 128) * 128
    while t >= 128:
        if n % t == 0:
            return t
        t -= 128
    return n


def _sds(shape, dtype):
    return jax.ShapeDtypeStruct(shape, dtype)


def _sigmoid(v):
    return 1.0 / (1.0 + jnp.exp(-v))


def _dot(a, b, dims):
    return lax.dot_general(a, b, (dims, ((), ())), preferred_element_type=F32)


def _dot_nn(a, b):
    return _dot(a.astype(BF), b.astype(BF), ((1,), (0,)))


def _dot_nt(a, b):
    return _dot(a.astype(BF), b.astype(BF), ((1,), (1,)))


def _dot_tn(a, b):
    return _dot(a.astype(BF), b.astype(BF), ((0,), (0,)))


def matmul(a, b, mode, out_dtype, name, bias=None, res=None, gate=None, coef=1.0):
    if mode == "nn":
        (M, K), (_, N) = a.shape, b.shape
    elif mode == "nt":
        (M, K), (N, _) = a.shape, b.shape
    else:
        (K, M), (_, N) = a.shape, b.shape
    cap_n = 512 if K > 4096 else 1024
    tm = _tile(M, 1024 if (mode == "tn" or K <= D_FF) else 512)
    tn = _tile(N, cap_n)
    if mode != "tn" and tm * tn > 1024 * 896:
        tn = _tile(N, 512)
    if mode == "nn":
        a_spec = pl.BlockSpec((tm, K), lambda i, j: (i, 0))
        b_spec = pl.BlockSpec((K, tn), lambda i, j: (0, j))
        fn = _dot_nn
    elif mode == "nt":
        a_spec = pl.BlockSpec((tm, K), lambda i, j: (i, 0))
        b_spec = pl.BlockSpec((tn, K), lambda i, j: (j, 0))
        fn = _dot_nt
    else:
        a_spec = pl.BlockSpec((K, tm), lambda i, j: (0, i))
        b_spec = pl.BlockSpec((K, tn), lambda i, j: (0, j))
        fn = _dot_tn
    has_bias, has_res = bias is not None, res is not None
    o_spec = pl.BlockSpec((tm, tn), lambda i, j: (i, j))
    v_spec = pl.BlockSpec((1, tn), lambda i, j: (0, j))
    in_specs, args = [a_spec, b_spec], [a, b]
    if has_bias:
        in_specs.append(v_spec)
        args.append(bias)
    if has_res:
        in_specs += [o_spec, v_spec]
        args += [res, gate]

    def body(*refs):
        a_ref, b_ref = refs[0], refs[1]
        k = 2
        y = fn(a_ref[...], b_ref[...])
        if has_bias:
            y = y + refs[k][...]
            k += 1
        if has_res:
            res_ref, gate_ref = refs[k], refs[k + 1]
            refs[k + 2][...] = y.astype(out_dtype)
            refs[k + 3][...] = res_ref[...] + coef * gate_ref[...] * y
        else:
            refs[k][...] = y.astype(out_dtype)

    if has_res:
        out_shape = (_sds((M, N), out_dtype), _sds((M, N), F32))
        out_specs = (o_spec, o_spec)
    else:
        out_shape = _sds((M, N), out_dtype)
        out_specs = o_spec
    return _call(body, name, (M // tm, N // tn), in_specs, out_specs, out_shape)(*args)


def norm_mod(x, nw, sh, sc, name):
    L, D = x.shape
    tm = _tile(L, 512)

    def body(x_ref, nw_ref, sh_ref, sc_ref, h_ref):
        xf = x_ref[...]
        r = lax.rsqrt(jnp.mean(xf * xf, axis=-1, keepdims=True) + EPS)
        n = xf * r * nw_ref[...]
        h_ref[...] = (n * (1.0 + sc_ref[...]) + sh_ref[...]).astype(BF)

    row = pl.BlockSpec((tm, D), lambda i: (i, 0))
    vec = pl.BlockSpec((1, D), lambda i: (0, 0))
    return _call(body, name, (L // tm,), [row, vec, vec, vec], row, _sds((L, D), BF))(x, nw, sh, sc)


def norm_mod_bwd(x, dh, dres, nw, sc, name):
    L, D = x.shape
    tm = _tile(L, 512)

    def body(x_ref, dh_ref, dres_ref, nw_ref, sc_ref, dx_ref, acc_ref):
        @pl.when(pl.program_id(0) == 0)
        def _():
            acc_ref[...] = jnp.zeros_like(acc_ref)

        xf = x_ref[...]
        dhf = dh_ref[...].astype(F32)
        r = lax.rsqrt(jnp.mean(xf * xf, axis=-1, keepdims=True) + EPS)
        xhat = xf * r
        nwv = nw_ref[...]
        dn = dhf * (1.0 + sc_ref[...])
        dxhat = dn * nwv
        proj = jnp.mean(dxhat * xhat, axis=-1, keepdims=True)
        dx_ref[...] = dres_ref[...] + r * (dxhat - xhat * proj)
        acc_ref[0:1, :] += jnp.sum(dhf, axis=0, keepdims=True)
        acc_ref[1:2, :] += jnp.sum(dhf * xhat * nwv, axis=0, keepdims=True)
        acc_ref[2:3, :] += jnp.sum(dn * xhat, axis=0, keepdims=True)

    row = pl.BlockSpec((tm, D), lambda i: (i, 0))
    vec = pl.BlockSpec((1, D), lambda i: (0, 0))
    acc = pl.BlockSpec((8, D), lambda i: (0, 0))
    return _call(body, name, (L // tm,), [row, row, row, vec, vec], (row, acc),
                 (_sds((L, D), F32), _sds((8, D), F32)))(x, dh, dres, nw, sc)


def final_loss(x, nw, target):
    L, D = x.shape
    tm = _tile(L, 512)

    def body(x_ref, nw_ref, t_ref, dx_ref, acc_ref):
        @pl.when(pl.program_id(0) == 0)
        def _():
            acc_ref[...] = jnp.zeros_like(acc_ref)

        xf = x_ref[...]
        r = lax.rsqrt(jnp.mean(xf * xf, axis=-1, keepdims=True) + EPS)
        xhat = xf * r
        nwv = nw_ref[...]
        err = xhat * nwv - t_ref[...]
        dy = err * (1.0 / D)
        dxhat = dy * nwv
        proj = jnp.mean(dxhat * xhat, axis=-1, keepdims=True)
        dx_ref[...] = r * (dxhat - xhat * proj)
        acc_ref[0:1, :] += jnp.sum(dy * xhat, axis=0, keepdims=True)
        acc_ref[1:2, :] += jnp.sum(err * err, axis=0, keepdims=True)

    row = pl.BlockSpec((tm, D), lambda i: (i, 0))
    vec = pl.BlockSpec((1, D), lambda i: (0, 0))
    acc = pl.BlockSpec((8, D), lambda i: (0, 0))
    return _call(body, "final_loss", (L // tm,), [row, vec, row], (row, acc),
                 (_sds((L, D), F32), _sds((8, D), F32)))(x, nw, target)


def resid_gate_bwd(dxo, y, gate, coef, name):
    L, D = dxo.shape
    tm = _tile(L, 512)

    def body(dxo_ref, y_ref, g_ref, dy_ref, acc_ref):
        @pl.when(pl.program_id(0) == 0)
        def _():
            acc_ref[...] = jnp.zeros_like(acc_ref)

        d = dxo_ref[...]
        dy = coef * g_ref[...] * d
        dy_ref[...] = dy.astype(BF)
        acc_ref[0:1, :] += coef * jnp.sum(d * y_ref[...].astype(F32), axis=0, keepdims=True)
        acc_ref[1:2, :] += jnp.sum(dy, axis=0, keepdims=True)

    row = pl.BlockSpec((tm, D), lambda i: (i, 0))
    vec = pl.BlockSpec((1, D), lambda i: (0, 0))
    acc = pl.BlockSpec((8, D), lambda i: (0, 0))
    return _call(body, name, (L // tm,), [row, row, vec], (row, acc),
                 (_sds((L, D), BF), _sds((8, D), F32)))(dxo, y, gate)


def ffn_up(h, wt):
    L, D = h.shape
    F = wt.shape[0] // 2
    tm, tn = _tile(L, 2048), _tile(F, 256)
    nj = F // tn

    def body(h_ref, wg_ref, wu_ref, g_ref, u_ref, a_ref):
        hv = h_ref[...]
        g = _dot_nt(hv, wg_ref[...])
        u = _dot_nt(hv, wu_ref[...])
        g_ref[...] = g.astype(BF)
        u_ref[...] = u.astype(BF)
        a_ref[...] = (g * _sigmoid(g) * u).astype(BF)

    o = pl.BlockSpec((tm, tn), lambda i, n: (i, n))
    return _call(body, "ffn_up", (L // tm, nj),
                 [pl.BlockSpec((tm, D), lambda i, n: (i, 0)),
                  pl.BlockSpec((tn, D), lambda i, n: (n, 0)),
                  pl.BlockSpec((tn, D), lambda i, n: (n + nj, 0))],
                 (o, o, o), tuple(_sds((L, F), BF) for _ in range(3)))(h, wt, wt)


def ffn_down_dgrad(dy, wd, g, u):
    L, D = dy.shape
    F = wd.shape[0]
    tm, tn = _tile(L, 2048), _tile(F, 256)

    def body(dy_ref, w_ref, g_ref, u_ref, dg_ref, du_ref):
        da = _dot_nt(dy_ref[...], w_ref[...])
        gv = g_ref[...].astype(F32)
        uv = u_ref[...].astype(F32)
        s = _sigmoid(gv)
        dg_ref[...] = (da * uv * s * (1.0 + gv * (1.0 - s))).astype(BF)
        du_ref[...] = (da * gv * s).astype(BF)

    o = pl.BlockSpec((tm, tn), lambda i, n: (i, n))
    return _call(body, "ffn_down_dgrad", (L // tm, F // tn),
                 [pl.BlockSpec((tm, D), lambda i, n: (i, 0)), pl.BlockSpec((tn, D), lambda i, n: (n, 0)), o, o],
                 (o, o), (_sds((L, F), BF), _sds((L, F), BF)))(dy, wd, g, u)


def ffn_up_wgrad(dg, du, h):
    L, F = dg.shape
    D = h.shape[1]
    tm = _tile(F, 256)
    nblk = F // tm

    def half(d, off, prev):
        def body(d_ref, h_ref, *rest):
            rest[-1][...] = _dot_tn(d_ref[...], h_ref[...]).astype(BF)

        in_specs = [pl.BlockSpec((L, tm), lambda i: (0, i)), pl.BlockSpec((L, D), lambda i: (0, 0))]
        args = [d, h]
        aliases = {}
        if prev is not None:
            in_specs.append(pl.BlockSpec(memory_space=pl.ANY))
            args.append(prev)
            aliases = {2: 0}
        return pl.pallas_call(
            body, name="ffn_up_wgrad", grid=(nblk,), in_specs=in_specs,
            out_specs=pl.BlockSpec((tm, D), lambda i: (i + off * nblk, 0)),
            out_shape=_sds((2 * F, D), BF), input_output_aliases=aliases,
            compiler_params=pltpu.CompilerParams(vmem_limit_bytes=VMEM_LIMIT))(*args)

    return half(du, 1, half(dg, 0, None))


def ffn_up_dgrad(dg, du, wt):
    L, F = dg.shape
    D = wt.shape[1]
    tm, tn = _tile(L, 1024), _tile(D, 512)

    def body(dg_ref, du_ref, wg_ref, wu_ref, o_ref):
        o_ref[...] = _dot_nn(dg_ref[...], wg_ref[...]) + _dot_nn(du_ref[...], wu_ref[...])

    a = pl.BlockSpec((tm, F), lambda i, n: (i, 0))
    return _call(body, "ffn_up_dgrad", (L // tm, D // tn),
                 [a, a, pl.BlockSpec((F, tn), lambda i, n: (0, n)), pl.BlockSpec((F, tn), lambda i, n: (1, n))],
                 pl.BlockSpec((tm, tn), lambda i, n: (i, n)), _sds((L, D), F32))(dg, du, wt, wt)


_HALO = 8


def _shift_down(cur, prev8, k):
    out = pltpu.roll(cur, k, 0)
    rows8 = lax.broadcasted_iota(jnp.int32, prev8.shape, 0)
    head = jnp.where(rows8 < k, pltpu.roll(prev8, k, 0), out[0:_HALO])
    if cur.shape[0] == _HALO:
        return head
    return jnp.concatenate([head, out[_HALO:]], axis=0)


def _shift_up(cur, next8, k):
    n = cur.shape[0]
    out = pltpu.roll(cur, n - k, 0)
    rows8 = lax.broadcasted_iota(jnp.int32, next8.shape, 0)
    tail = jnp.where(rows8 >= _HALO - k, pltpu.roll(next8, _HALO - k, 0), out[n - _HALO:])
    return jnp.concatenate([out[:n - _HALO], tail], axis=0)


def _conv_pre(cur, prev8, w_ref, b_ref):
    shifted = [_shift_down(cur, prev8, k) for k in range(1, CONV_WIDTH)]
    s = cur * w_ref[CONV_WIDTH - 1:CONV_WIDTH, :] + b_ref[...]
    for k in range(1, CONV_WIDTH):
        s = s + shifted[k - 1] * w_ref[CONV_WIDTH - 1 - k:CONV_WIDTH - k, :]
    return s, shifted


def _silu_grad(s):
    sg = _sigmoid(s)
    return sg * (1.0 + s * (1.0 - sg))


_XBC_COL0 = D_INNER // 512


def conv_fwd(zx, w, b):
    L = zx.shape[0]
    tm, tc = _tile(L, 256), 512
    hb = tm // _HALO

    def body(cur_ref, prev_ref, w_ref, b_ref, o_ref):
        prev8 = jnp.where(pl.program_id(1) > 0, prev_ref[...], 0.0)
        s, _ = _conv_pre(cur_ref[...], prev8, w_ref, b_ref)
        o_ref[...] = s * _sigmoid(s)

    return _call(body, "conv_fwd", (CONV_DIM // tc, L // tm),
                 [pl.BlockSpec((tm, tc), lambda j, i: (i, _XBC_COL0 + j)),
                  pl.BlockSpec((_HALO, tc), lambda j, i: (jnp.maximum(i * hb - 1, 0), _XBC_COL0 + j)),
                  pl.BlockSpec((CONV_WIDTH, tc), lambda j, i: (0, j)),
                  pl.BlockSpec((1, tc), lambda j, i: (0, j))],
                 pl.BlockSpec((tm, tc), lambda j, i: (i, j)), _sds((L, CONV_DIM), F32))(zx, zx, w, b)


def conv_bwd(dxc, zx, w, b):
    L = zx.shape[0]
    tm, tc = _tile(L, 256), 512
    nblk = L // tm
    hb = tm // _HALO

    def body(d_ref, dn_ref, cur_ref, prev_ref, next_ref, w_ref, b_ref, du_ref, acc_ref):
        i = pl.program_id(1)

        @pl.when(i == 0)
        def _():
            acc_ref[...] = jnp.zeros_like(acc_ref)

        cur = cur_ref[...]
        prev8 = jnp.where(i > 0, prev_ref[...], 0.0)
        s, shifted = _conv_pre(cur, prev8, w_ref, b_ref)
        ds_c = d_ref[...] * _silu_grad(s)
        s_n, _ = _conv_pre(next_ref[...], cur[tm - _HALO:], w_ref, b_ref)
        ds_n = jnp.where(i < nblk - 1, dn_ref[...] * _silu_grad(s_n), 0.0)
        du = ds_c * w_ref[CONV_WIDTH - 1:CONV_WIDTH, :]
        acc_ref[CONV_WIDTH - 1:CONV_WIDTH, :] += jnp.sum(ds_c * cur, axis=0, keepdims=True)
        for k in range(1, CONV_WIDTH):
            du = du + _shift_up(ds_c, ds_n, k) * w_ref[CONV_WIDTH - 1 - k:CONV_WIDTH - k, :]
            acc_ref[CONV_WIDTH - 1 - k:CONV_WIDTH - k, :] += jnp.sum(ds_c * shifted[k - 1], axis=0, keepdims=True)
        acc_ref[CONV_WIDTH:CONV_WIDTH + 1, :] += jnp.sum(ds_c, axis=0, keepdims=True)
        du_ref[...] = du.astype(BF)

    nxt = lambda j, i: (jnp.minimum((i + 1) * hb, L // _HALO - 1), j)
    return _call(body, "conv_bwd", (CONV_DIM // tc, nblk),
                 [pl.BlockSpec((tm, tc), lambda j, i: (i, j)),
                  pl.BlockSpec((_HALO, tc), nxt),
                  pl.BlockSpec((tm, tc), lambda j, i: (i, _XBC_COL0 + j)),
                  pl.BlockSpec((_HALO, tc), lambda j, i: (jnp.maximum(i * hb - 1, 0), _XBC_COL0 + j)),
                  pl.BlockSpec((_HALO, tc), lambda j, i: (jnp.minimum((i + 1) * hb, L // _HALO - 1), _XBC_COL0 + j)),
                  pl.BlockSpec((CONV_WIDTH, tc), lambda j, i: (0, j)),
                  pl.BlockSpec((1, tc), lambda j, i: (0, j))],
                 (pl.BlockSpec((tm, tc), lambda j, i: (i, j)), pl.BlockSpec((8, tc), lambda j, i: (0, j))),
                 (_sds((L, CONV_DIM), BF), _sds((8, CONV_DIM), F32)))(dxc, dxc, zx, zx, zx, w, b)


_DT_COL = (D_INNER + CONV_DIM) // 128


def dt_prep(zx, bias_pad, alog_pad):
    L = zx.shape[0]

    def body(raw_ref, b_ref, al_ref, dt_ref, acs_ref):
        v = raw_ref[...] + b_ref[...]
        dt = jnp.maximum(v, 0.0) + jnp.log(1.0 + jnp.exp(-jnp.abs(v)))
        dt_ref[...] = dt
        acs = dt * (-jnp.exp(al_ref[...]))
        rows = lax.broadcasted_iota(jnp.int32, acs.shape, 0)
        s = 1
        while s < CHUNK:
            acs = acs + jnp.where(rows >= s, pltpu.roll(acs, s, 0), 0.0)
            s *= 2
        acs_ref[...] = acs

    blk = pl.BlockSpec((CHUNK, 128), lambda i: (i, 0))
    vec = pl.BlockSpec((1, 128), lambda i: (0, 0))
    return _call(body, "dt_prep", (L // CHUNK,),
                 [pl.BlockSpec((CHUNK, 128), lambda i: (i, _DT_COL)), vec, vec], (blk, blk),
                 (_sds((L, 128), F32), _sds((L, 128), F32)))(zx, bias_pad, alog_pad)


def dt_bwd(ddt, da, dt, zx, bias_pad, alog_pad):
    L = zx.shape[0]
    tm = _tile(L, 512)

    def body(ddt_ref, da_ref, dt_ref, raw_ref, b_ref, al_ref, o_ref, acc_ref):
        @pl.when(pl.program_id(0) == 0)
        def _():
            acc_ref[...] = jnp.zeros_like(acc_ref)

        A = -jnp.exp(al_ref[...])
        dav = da_ref[...]
        dd = ddt_ref[...] + dav * A
        draw = dd * _sigmoid(raw_ref[...] + b_ref[...])
        o_ref[...] = draw.astype(BF)
        acc_ref[0:1, :] += jnp.sum(draw, axis=0, keepdims=True)
        acc_ref[1:2, :] += jnp.sum(dav * dt_ref[...], axis=0, keepdims=True) * A

    blk = pl.BlockSpec((tm, 128), lambda i: (i, 0))
    vec = pl.BlockSpec((1, 128), lambda i: (0, 0))
    return _call(body, "dt_bwd", (L // tm,),
                 [blk, blk, blk, pl.BlockSpec((tm, 128), lambda i: (i, _DT_COL)), vec, vec],
                 (blk, pl.BlockSpec((8, 128), lambda i: (0, 0))),
                 (_sds((L, 128), BF), _sds((8, 128), F32)))(ddt, da, dt, zx, bias_pad, alog_pad)


_HPG = SSM_HEADS // SSM_GROUPS
_GW = _HPG * SSM_HEADDIM
_B_COL0 = D_INNER // SSM_STATE
_C_COL0 = (D_INNER + SSM_GROUPS * SSM_STATE) // SSM_STATE


def _ssd_head(x, dtc, ac, ar, r, causal):
    xh = x[:, SSM_HEADDIM * r:SSM_HEADDIM * (r + 1)]
    acol = ac[:, r:r + 1]
    arow = ar[r:r + 1, :]
    alast = ar[r:r + 1, CHUNK - 1:CHUNK]
    lm = jnp.exp(jnp.where(causal, acol - arow, NEG))
    return xh, xh * dtc[:, r:r + 1], acol, alast, lm


def ssd_fwd(xc, dt_g, acs_g, acsT_g, d_exp):
    L = xc.shape[0]
    nc = L // CHUNK

    def body(x_ref, b_ref, c_ref, dt_ref, ac_ref, ar_ref, d_ref, y_ref, pst_ref, st_ref):
        @pl.when(pl.program_id(1) == 0)
        def _():
            st_ref[...] = jnp.zeros_like(st_ref)

        x, Bm, Cm = x_ref[...], b_ref[...], c_ref[...]
        dtc, ac, ar = dt_ref[...], ac_ref[...], ar_ref[...]
        causal = lax.broadcasted_iota(jnp.int32, (CHUNK, CHUNK), 0) >= lax.broadcasted_iota(jnp.int32, (CHUNK, CHUNK), 1)
        CB = _dot_nt(Cm, Bm)
        for r in range(_HPG):
            xh, xd, acol, alast, lm = _ssd_head(x, dtc, ac, ar, r, causal)
            P = st_ref[r]
            y = _dot_nn(CB * lm, xd) + jnp.exp(acol) * _dot_nt(Cm, P)
            y_ref[:, SSM_HEADDIM * r:SSM_HEADDIM * (r + 1)] = y + d_ref[:, SSM_HEADDIM * r:SSM_HEADDIM * (r + 1)] * xh
            pst_ref[r] = P
            st_ref[r] = P * jnp.exp(alast) + _dot_tn(xd * jnp.exp(alast - acol), Bm)

    return _call(
        body, "ssd_fwd", (SSM_GROUPS, nc),
        [pl.BlockSpec((CHUNK, _GW), lambda g, c: (c, g)),
         pl.BlockSpec((CHUNK, SSM_STATE), lambda g, c: (c, _B_COL0 + g)),
         pl.BlockSpec((CHUNK, SSM_STATE), lambda g, c: (c, _C_COL0 + g)),
         pl.BlockSpec((None, CHUNK, _HPG), lambda g, c: (g, c, 0)),
         pl.BlockSpec((None, CHUNK, _HPG), lambda g, c: (g, c, 0)),
         pl.BlockSpec((None, _HPG, CHUNK), lambda g, c: (g, 0, c)),
         pl.BlockSpec((None, 1, _GW), lambda g, c: (g, 0, 0))],
        (pl.BlockSpec((CHUNK, _GW), lambda g, c: (c, g)),
         pl.BlockSpec((None, None, _HPG, SSM_HEADDIM, SSM_STATE), lambda g, c: (c, g, 0, 0, 0))),
        (_sds((L, D_INNER), F32), _sds((nc, SSM_GROUPS, _HPG, SSM_HEADDIM, SSM_STATE), F32)),
        scratch=[pltpu.VMEM((_HPG, SSM_HEADDIM, SSM_STATE), F32)],
    )(xc, xc, xc, dt_g, acs_g, acsT_g, d_exp)


def ssd_bwd(dy, xc, dt_g, acs_g, acsT_g, pst, d_exp):
    L = xc.shape[0]
    nc = L // CHUNK

    def body(dy_ref, x_ref, b_ref, c_ref, dt_ref, ac_ref, ar_ref, pst_ref, d_ref,
             dx_ref, db_ref, dc_ref, ddt_ref, da_ref, dd_ref, dp_ref):
        @pl.when(pl.program_id(1) == 0)
        def _():
            dp_ref[...] = jnp.zeros_like(dp_ref)
            dd_ref[...] = jnp.zeros_like(dd_ref)

        dyv, x, Bm, Cm = dy_ref[...], x_ref[...], b_ref[...], c_ref[...]
        dtc, ac, ar = dt_ref[...], ac_ref[...], ar_ref[...]
        ri = lax.broadcasted_iota(jnp.int32, (CHUNK, CHUNK), 0)
        ci = lax.broadcasted_iota(jnp.int32, (CHUNK, CHUNK), 1)
        causal = ri >= ci
        lane4 = lax.broadcasted_iota(jnp.int32, (CHUNK, _HPG), 1)
        CB = _dot_nt(Cm, Bm)
        dB = jnp.zeros((CHUNK, SSM_STATE), F32)
        dC = jnp.zeros((CHUNK, SSM_STATE), F32)
        dCB = jnp.zeros((CHUNK, CHUNK), F32)
        ddt_blk = jnp.zeros((CHUNK, _HPG), F32)
        da_blk = jnp.zeros((CHUNK, _HPG), F32)
        for r in range(_HPG):
            sl = slice(SSM_HEADDIM * r, SSM_HEADDIM * (r + 1))
            xh, xd, acol, alast, lm = _ssd_head(x, dtc, ac, ar, r, causal)
            dyh = dyv[:, sl]
            P = pst_ref[r]
            dPn = dp_ref[r]
            eA = jnp.exp(acol)
            cd = jnp.exp(alast)
            dte = jnp.exp(alast - acol)
            G = CB * lm
            dZ = eA * dyh
            dzp = _dot_nn(dZ, P)
            dC = dC + dzp
            dp_ref[r] = dPn * cd + _dot_tn(dZ, Cm)
            dA_col = jnp.sum(dzp * Cm, axis=1, keepdims=True)
            BdS = _dot_nt(Bm, dPn)
            dxd = dte * BdS
            dB = dB + dte * _dot_nn(xd, dPn)
            t = jnp.sum(xd * BdS, axis=1, keepdims=True) * dte
            dA_col = dA_col - t
            dA_last = jnp.sum(t, axis=0, keepdims=True) + jnp.sum(
                jnp.sum(dPn * P, axis=1, keepdims=True), axis=0, keepdims=True) * cd
            dG = _dot_nt(dyh, xd)
            dxd = dxd + _dot_tn(G, dyh)
            dCB = dCB + dG * lm
            W = dG * G
            dA_col = dA_col + jnp.sum(W, axis=1, keepdims=True)
            dA_row = jnp.sum(jnp.where(ri == ci, dA_col, 0.0), axis=0, keepdims=True) - jnp.sum(W, axis=0, keepdims=True)
            da_col = jnp.sum(jnp.where(ci >= ri, dA_row, 0.0), axis=1, keepdims=True) + dA_last
            da_blk = jnp.where(lane4 == r, da_col, da_blk)
            ddt_blk = jnp.where(lane4 == r, jnp.sum(dxd * xh, axis=1, keepdims=True), ddt_blk)
            dx_ref[:, sl] = dxd * dtc[:, r:r + 1] + d_ref[:, sl] * dyh
        dc_ref[...] = dC + _dot_nn(dCB, Bm)
        db_ref[...] = dB + _dot_tn(dCB, Cm)
        ddt_ref[...] = ddt_blk
        da_ref[...] = da_blk
        dd_ref[...] += jnp.sum(dyv * x, axis=0, keepdims=True)

    rc = lambda g, c: (nc - 1 - c, g)
    small = pl.BlockSpec((None, CHUNK, _HPG), lambda g, c: (g, nc - 1 - c, 0))
    return _call(
        body, "ssd_bwd", (SSM_GROUPS, nc),
        [pl.BlockSpec((CHUNK, _GW), rc),
         pl.BlockSpec((CHUNK, _GW), rc),
         pl.BlockSpec((CHUNK, SSM_STATE), lambda g, c: (nc - 1 - c, _B_COL0 + g)),
         pl.BlockSpec((CHUNK, SSM_STATE), lambda g, c: (nc - 1 - c, _C_COL0 + g)),
         small, small,
         pl.BlockSpec((None, _HPG, CHUNK), lambda g, c: (g, 0, nc - 1 - c)),
         pl.BlockSpec((None, None, _HPG, SSM_HEADDIM, SSM_STATE), lambda g, c: (nc - 1 - c, g, 0, 0, 0)),
         pl.BlockSpec((None, 1, _GW), lambda g, c: (g, 0, 0))],
        (pl.BlockSpec((CHUNK, _GW), rc),
         pl.BlockSpec((CHUNK, SSM_STATE), rc),
         pl.BlockSpec((CHUNK, SSM_STATE), rc),
         small, small,
         pl.BlockSpec((None, 1, _GW), lambda g, c: (g, 0, 0))),
        (_sds((L, D_INNER), F32), _sds((L, SSM_GROUPS * SSM_STATE), F32), _sds((L, SSM_GROUPS * SSM_STATE), F32),
         _sds((SSM_GROUPS, L, _HPG), F32), _sds((SSM_GROUPS, L, _HPG), F32), _sds((SSM_GROUPS, 1, _GW), F32)),
        scratch=[pltpu.VMEM((_HPG, SSM_HEADDIM, SSM_STATE), F32)],
    )(dy, xc, xc, xc, dt_g, acs_g, acsT_g, pst, d_exp)


_NGW = D_INNER // SSM_GROUPS


def gate_norm(y, zx, nw):
    L = y.shape[0]
    tm = _tile(L, 256)

    def body(y_ref, z_ref, nw_ref, o_ref):
        for g in range(SSM_GROUPS):
            sl = slice(_NGW * g, _NGW * (g + 1))
            z = z_ref[:, sl]
            y2 = y_ref[:, sl] * (z * _sigmoid(z))
            r = lax.rsqrt(jnp.mean(y2 * y2, axis=-1, keepdims=True) + EPS)
            o_ref[:, sl] = (y2 * r * nw_ref[:, sl]).astype(BF)

    row = pl.BlockSpec((tm, D_INNER), lambda i: (i, 0))
    return _call(body, "gate_norm", (L // tm,), [row, row, pl.BlockSpec((1, D_INNER), lambda i: (0, 0))],
                 row, _sds((L, D_INNER), BF))(y, zx, nw)


def gate_norm_bwd(dyn, y, zx, nw):
    L = y.shape[0]
    tm = _tile(L, 256)

    def body(d_ref, y_ref, z_ref, nw_ref, dy_ref, dz_ref, acc_ref):
        @pl.when(pl.program_id(0) == 0)
        def _():
            acc_ref[...] = jnp.zeros_like(acc_ref)

        for g in range(SSM_GROUPS):
            sl = slice(_NGW * g, _NGW * (g + 1))
            z = z_ref[:, sl]
            yv = y_ref[:, sl]
            sg = _sigmoid(z)
            sz = z * sg
            y2 = yv * sz
            r = lax.rsqrt(jnp.mean(y2 * y2, axis=-1, keepdims=True) + EPS)
            yh = y2 * r
            d = d_ref[:, sl]
            dn = d * nw_ref[:, sl]
            dy2 = r * (dn - yh * jnp.mean(dn * yh, axis=-1, keepdims=True))
            dy_ref[:, sl] = dy2 * sz
            dz_ref[:, sl] = (dy2 * yv * sg * (1.0 + z * (1.0 - sg))).astype(BF)
            acc_ref[0:1, sl] += jnp.sum(d * yh, axis=0, keepdims=True)

    row = pl.BlockSpec((tm, D_INNER), lambda i: (i, 0))
    return _call(body, "gate_norm_bwd", (L // tm,), [row, row, row, pl.BlockSpec((1, D_INNER), lambda i: (0, 0))],
                 (row, row, pl.BlockSpec((8, D_INNER), lambda i: (0, 0))),
                 (_sds((L, D_INNER), F32), _sds((L, D_INNER), BF), _sds((8, D_INNER), F32)))(dyn, y, zx, nw)


_SCALE = HEAD_DIM ** -0.5
_REP = ATT_HEADS // KV_HEADS
_V_OFF = KV_HEADS * HEAD_DIM


def _stack_heads(ref, k):
    return jnp.concatenate([ref[:, HEAD_DIM * (k * _REP + r):HEAD_DIM * (k * _REP + r + 1)] for r in range(_REP)],
                           axis=0)


def _stack_sinks(s_ref, k):
    return jnp.concatenate([jnp.broadcast_to(s_ref[:, k * _REP + r:k * _REP + r + 1], (WINDOW, 1))
                            for r in range(_REP)], axis=0)


def _attn_probs(q4, kp, kc, sink, first):
    shape = (_REP * WINDOW, WINDOW)
    rows = jnp.bitwise_and(lax.broadcasted_iota(jnp.int32, shape, 0), WINDOW - 1)
    cols = lax.broadcasted_iota(jnp.int32, shape, 1)
    sp = jnp.where(jnp.logical_and(cols > rows, jnp.logical_not(first)), _dot_nt(q4, kp) * _SCALE, NEG)
    sc = jnp.where(cols <= rows, _dot_nt(q4, kc) * _SCALE, NEG)
    m = jnp.maximum(jnp.maximum(jnp.max(sp, axis=1, keepdims=True), jnp.max(sc, axis=1, keepdims=True)), sink)
    pp = jnp.exp(sp - m)
    pc = jnp.exp(sc - m)
    ps = jnp.exp(sink - m)
    inv = 1.0 / (jnp.sum(pp, axis=1, keepdims=True) + jnp.sum(pc, axis=1, keepdims=True) + ps)
    return pp * inv, pc * inv, ps * inv


def attn_fwd(q, kv, sinks_pad):
    L = q.shape[0]
    nb = L // WINDOW

    def body(q_ref, kc_ref, kp_ref, s_ref, o_ref):
        first = pl.program_id(0) == 0
        for k in range(KV_HEADS):
            ks = slice(HEAD_DIM * k, HEAD_DIM * (k + 1))
            vs = slice(_V_OFF + HEAD_DIM * k, _V_OFF + HEAD_DIM * (k + 1))
            pp, pc, _ = _attn_probs(_stack_heads(q_ref, k), kp_ref[:, ks], kc_ref[:, ks], _stack_sinks(s_ref, k), first)
            o4 = _dot_nn(pp, kp_ref[:, vs]) + _dot_nn(pc, kc_ref[:, vs])
            for r in range(_REP):
                h = k * _REP + r
                o_ref[:, HEAD_DIM * h:HEAD_DIM * (h + 1)] = o4[WINDOW * r:WINDOW * (r + 1)].astype(BF)

    qspec = pl.BlockSpec((WINDOW, D_MODEL), lambda i: (i, 0))
    return _call(body, "attn_fwd", (nb,),
                 [qspec, pl.BlockSpec((WINDOW, KV_DIM), lambda i: (i, 0)),
                  pl.BlockSpec((WINDOW, KV_DIM), lambda i: (jnp.maximum(i - 1, 0), 0)),
                  pl.BlockSpec((1, 128), lambda i: (0, 0))],
                 qspec, _sds((L, D_MODEL), BF))(q, kv, kv, sinks_pad)


def attn_bwd(q, kv, do, sinks_pad):
    L = q.shape[0]
    nb = L // WINDOW

    def body(q_ref, kc_ref, kp_ref, do_ref, s_ref, dq_ref, dc_ref, dp_ref, acc_ref):
        first = pl.program_id(0) == 0

        @pl.when(first)
        def _():
            acc_ref[...] = jnp.zeros_like(acc_ref)

        lane = lax.broadcasted_iota(jnp.int32, (1, 128), 1)
        dsink = jnp.zeros((1, 128), F32)
        for k in range(KV_HEADS):
            ks = slice(HEAD_DIM * k, HEAD_DIM * (k + 1))
            vs = slice(_V_OFF + HEAD_DIM * k, _V_OFF + HEAD_DIM * (k + 1))
            kp, kc, vp, vc = kp_ref[:, ks], kc_ref[:, ks], kp_ref[:, vs], kc_ref[:, vs]
            q4 = _stack_heads(q_ref, k)
            do4 = _stack_heads(do_ref, k)
            pp, pc, ps = _attn_probs(q4, kp, kc, _stack_sinks(s_ref, k), first)
            dpp = _dot_nt(do4, vp)
            dpc = _dot_nt(do4, vc)
            delta = jnp.sum(pp * dpp, axis=1, keepdims=True) + jnp.sum(pc * dpc, axis=1, keepdims=True)
            dsp = pp * (dpp - delta) * _SCALE
            dsc = pc * (dpc - delta) * _SCALE
            dq4 = _dot_nn(dsp, kp) + _dot_nn(dsc, kc)
            psd = ps * delta
            for r in range(_REP):
                h = k * _REP + r
                rs = slice(WINDOW * r, WINDOW * (r + 1))
                dq_ref[:, HEAD_DIM * h:HEAD_DIM * (h + 1)] = dq4[rs]
                dsink = dsink + jnp.where(lane == h, -jnp.sum(psd[rs], axis=0, keepdims=True), 0.0)
            dp_ref[:, ks] = _dot_tn(dsp, q4)
            dc_ref[:, ks] = _dot_tn(dsc, q4)
            dp_ref[:, vs] = _dot_tn(pp, do4)
            dc_ref[:, vs] = _dot_tn(pc, do4)
        acc_ref[0:1, :] += jnp.sum(dq_ref[...], axis=0, keepdims=True)
        acc_ref[1:2, 0:128] += dsink

    qspec = pl.BlockSpec((WINDOW, D_MODEL), lambda i: (i, 0))
    kspec = pl.BlockSpec((WINDOW, KV_DIM), lambda i: (i, 0))
    return _call(body, "attn_bwd", (nb,),
                 [qspec, kspec, pl.BlockSpec((WINDOW, KV_DIM), lambda i: (jnp.maximum(i - 1, 0), 0)), qspec,
                  pl.BlockSpec((1, 128), lambda i: (0, 0))],
                 (qspec, kspec, kspec, pl.BlockSpec((8, D_MODEL), lambda i: (0, 0))),
                 (_sds((L, D_MODEL), F32), _sds((L, KV_DIM), F32), _sds((L, KV_DIM), F32), _sds((8, D_MODEL), F32)),
                 )(q, kv, kv, do, sinks_pad)


def kv_grad_combine(parts):
    L = parts[0][0].shape[0]
    nb = L // WINDOW
    n = len(parts)

    def body(*refs):
        i = pl.program_id(0)
        o_ref, acc_ref = refs[2 * n], refs[2 * n + 1]

        @pl.when(i == 0)
        def _():
            acc_ref[...] = jnp.zeros_like(acc_ref)

        tot = refs[0][...]
        nxt = refs[1][...]
        for a in range(1, n):
            tot = tot + refs[2 * a][...]
            nxt = nxt + refs[2 * a + 1][...]
        tot = tot + jnp.where(i < nb - 1, nxt, 0.0)
        o_ref[...] = tot
        acc_ref[0:1, :] += jnp.sum(tot, axis=0, keepdims=True)

    cur = pl.BlockSpec((WINDOW, KV_DIM), lambda i: (i, 0))
    nxt = pl.BlockSpec((WINDOW, KV_DIM), lambda i: (jnp.minimum(i + 1, nb - 1), 0))
    args = [t for p in parts for t in p]
    return _call(body, "kv_grad_combine", (nb,), [cur, nxt] * n,
                 (cur, pl.BlockSpec((8, KV_DIM), lambda i: (0, 0))),
                 (_sds((L, KV_DIM), F32), _sds((8, KV_DIM), F32)))(*args)


def mod_fwd(c_all, w, b, name):
    n, _, C = w.shape

    def body(c_ref, w_ref, b_ref, o_ref, ca_ref):
        cv = c_ref[...]
        ca = cv * _sigmoid(cv)
        ca_ref[...] = ca
        o_ref[...] = _dot(ca, w_ref[...], ((1,), (0,))) + b_ref[...]

    return _call(body, name, (n,),
                 [pl.BlockSpec((N_DEV, D_MODEL), lambda i: (0, 0)),
                  pl.BlockSpec((None, D_MODEL, C), lambda i: (i, 0, 0)),
                  pl.BlockSpec((None, 1, C), lambda i: (i, 0, 0))],
                 (pl.BlockSpec((None, N_DEV, C), lambda i: (i, 0, 0)), pl.BlockSpec((N_DEV, D_MODEL), lambda i: (0, 0))),
                 (_sds((n, N_DEV, C), F32), _sds((N_DEV, D_MODEL), F32)))(c_all, w, b)


def mod_wgrad(c_act_t, dmod, name):
    n, _, C = dmod.shape
    tr = 256

    def body(ct_ref, d_ref, o_ref):
        acc = ct_ref[:, 0:1] * d_ref[0:1, :]
        for bidx in range(1, N_DEV):
            acc = acc + ct_ref[:, bidx:bidx + 1] * d_ref[bidx:bidx + 1, :]
        o_ref[...] = acc

    return _call(body, name, (n, D_MODEL // tr),
                 [pl.BlockSpec((tr, N_DEV), lambda i, j: (j, 0)),
                  pl.BlockSpec((None, N_DEV, C), lambda i, j: (i, 0, 0))],
                 pl.BlockSpec((None, tr, C), lambda i, j: (i, j, 0)), _sds((n, D_MODEL, C), F32))(c_act_t, dmod)


def _my_pos():
    return lax.axis_index("x"), lax.axis_index("y"), lax.axis_index("c")


def small_all_gather(v):
    m_per, n = v.shape

    def body(x_ref, out_ref, send_sems, recv_sems, local_sem):
        x, y, c = _my_pos()
        me, sibling = (x, y, c), (x, y, 1 - c)
        chips = [(1 - x, y), (x, 1 - y), (1 - x, 1 - y)]

        def rows(px, py, pc):
            return out_ref.at[pl.ds((4 * px + 2 * py + pc) * m_per, m_per), :]

        def copy(k, block, to, src=None):
            return pltpu.make_async_remote_copy(
                src_ref=rows(*block) if src is None else src, dst_ref=rows(*block),
                send_sem=send_sems.at[k], recv_sem=recv_sems.at[k], device_id=to, device_id_type=MESH)

        mine = pltpu.make_async_copy(x_ref, rows(*me), local_sem)
        mine.start()
        first = [copy(0, me, sibling, src=x_ref)]
        first += [copy(1 + j, me, (*chip, c), src=x_ref) for j, chip in enumerate(chips)]
        for cp in first:
            cp.start()
        passed = [copy(4 + j, (*chip, c), sibling) for j, chip in enumerate(chips)]
        for j, chip in enumerate(chips):
            copy(1 + j, (*chip, c), me).wait_recv()
            passed[j].start()
        copy(0, sibling, me).wait_recv()
        for j, chip in enumerate(chips):
            copy(4 + j, (*chip, 1 - c), me).wait_recv()
        for cp in first + passed:
            cp.wait_send()
        mine.wait()

    return pl.pallas_call(
        body, name="small_all_gather",
        out_shape=_sds((N_DEV * m_per, n), v.dtype),
        in_specs=[pl.BlockSpec(memory_space=pltpu.VMEM)],
        out_specs=pl.BlockSpec(memory_space=pltpu.VMEM),
        scratch_shapes=[pltpu.SemaphoreType.DMA((7,)), pltpu.SemaphoreType.DMA((7,)), pltpu.SemaphoreType.DMA],
        compiler_params=pltpu.CompilerParams(vmem_limit_bytes=VMEM_LIMIT),
    )(v)


def big_all_gather(arrs):
    n = len(arrs)

    def body(*refs):
        ins, outs = refs[:n], refs[n:2 * n]
        send_sems, recv_sems, local_sems = refs[2 * n], refs[2 * n + 1], refs[2 * n + 2]
        x, y, c = _my_pos()
        me, sibling = (x, y, c), (x, y, 1 - c)
        chips = [(1 - x, y), (x, 1 - y), (1 - x, 1 - y)]

        def slot(a, px, py, pc):
            return outs[a].at[4 * px + 2 * py + pc]

        def copy(a, k, block, to, src=None):
            return pltpu.make_async_remote_copy(
                src_ref=slot(a, *block) if src is None else src, dst_ref=slot(a, *block),
                send_sem=send_sems.at[7 * a + k], recv_sem=recv_sems.at[7 * a + k], device_id=to, device_id_type=MESH)

        mine = [pltpu.make_async_copy(ins[a], slot(a, *me), local_sems.at[a]) for a in range(n)]
        for cp in mine:
            cp.start()
        first = []
        for a in range(n):
            first.append(copy(a, 0, me, sibling, src=ins[a]))
            first += [copy(a, 1 + j, me, (*chip, c), src=ins[a]) for j, chip in enumerate(chips)]
        for cp in first:
            cp.start()
        passed = []
        for a in range(n):
            for j, chip in enumerate(chips):
                copy(a, 1 + j, (*chip, c), me).wait_recv()
                fwd = copy(a, 4 + j, (*chip, c), sibling)
                fwd.start()
                passed.append(fwd)
        for a in range(n):
            copy(a, 0, sibling, me).wait_recv()
            for j, chip in enumerate(chips):
                copy(a, 4 + j, (*chip, 1 - c), me).wait_recv()
        for cp in first + passed:
            cp.wait_send()
        for cp in mine:
            cp.wait()

    hbm = pl.BlockSpec(memory_space=pltpu.HBM)
    return pl.pallas_call(
        body, name="big_all_gather",
        out_shape=[_sds((N_DEV,) + a.shape, a.dtype) for a in arrs],
        in_specs=[hbm] * n, out_specs=[hbm] * n,
        scratch_shapes=[pltpu.SemaphoreType.DMA((7 * n,)), pltpu.SemaphoreType.DMA((7 * n,)),
                        pltpu.SemaphoreType.DMA((n,))],
    )(*arrs)


_FLIPS =[(fx, fy, fc) for fx in (0, 1) for fy in (0, 1) for fc in (0, 1)][1:]
_HBM = pl.BlockSpec(memory_space=pltpu.HBM)
_SEM = pl.BlockSpec(memory_space=pltpu.SEMAPHORE)
_EFFECT = pltpu.SideEffectType.DATAFLOW_SIDE_EFFECTING


def _flip(x, y, c, f):
    return (1 - x if f[0] else x), (1 - y if f[1] else y), (1 - c if f[2] else c)


def _xfer_copies(srcs, lands, send_sems, recv_sems, scatter):
    x, y, c = _my_pos()
    me = 4 * x + 2 * y + c
    copies = []
    for a in range(len(srcs)):
        for k, f in enumerate(_FLIPS):
            px, py, pc = _flip(x, y, c, f)
            src = srcs[a].at[4 * px + 2 * py + pc] if scatter else srcs[a]
            copies.append(pltpu.make_async_remote_copy(
                src_ref=src, dst_ref=lands[a].at[me], send_sem=send_sems.at[7 * a + k],
                recv_sem=recv_sems.at[7 * a + k], device_id=(px, py, pc), device_id_type=MESH))
    return copies


def _own_copies(srcs, lands, local_sems, scatter):
    x, y, c = _my_pos()
    me = 4 * x + 2 * y + c
    return [pltpu.make_async_copy(srcs[a].at[me] if scatter else srcs[a], lands[a].at[me], local_sems.at[a])
            for a in range(len(srcs))]


def xfer_start(arrs, scatter, after, name):
    n = len(arrs)
    land_shapes = [a.shape if scatter else (N_DEV,) + a.shape for a in arrs]

    def body(*refs):
        srcs, lands = refs[:n], refs[n:2 * n]
        send_sems, recv_sems, local_sems = refs[2 * n + 1], refs[2 * n + 2], refs[2 * n + 3]
        token = refs[-1]
        for cp in _xfer_copies(srcs, lands, send_sems, recv_sems, scatter):
            cp.start()
        for cp in _own_copies(srcs, lands, local_sems, scatter):
            cp.start()
        token[...] = jnp.zeros_like(token)

    out = pl.pallas_call(
        body, name=name,
        out_shape=(pltpu.SemaphoreType.DMA((7 * n,)), pltpu.SemaphoreType.DMA((7 * n,)),
                   pltpu.SemaphoreType.DMA((n,)),
                   *[pltpu.HBM(a.shape, a.dtype) for a in arrs],
                   *[pltpu.HBM(s, a.dtype) for s, a in zip(land_shapes, arrs)],
                   _sds((8, 128), F32)),
        in_specs=[_HBM] * (2 * n) + [pl.BlockSpec(memory_space=pl.ANY)],
        out_specs=(_SEM, _SEM, _SEM, *([_HBM] * (2 * n)), pl.BlockSpec(memory_space=pltpu.VMEM)),
        input_output_aliases={i: 3 + i for i in range(2 * n)},
        compiler_params=pltpu.CompilerParams(has_side_effects=_EFFECT),
    )(*[pltpu.with_memory_space_constraint(a, pltpu.HBM) for a in arrs],
      *[pltpu.with_memory_space_constraint(lax.empty(s, a.dtype), pltpu.HBM) for s, a in zip(land_shapes, arrs)],
      after)
    return (out[0], out[1], out[2], list(out[3:3 + n]), list(out[3 + n:3 + 2 * n]), scatter), out[-1]


def xfer_wait(handle, after, name):
    send_sems, recv_sems, local_sems, srcs, lands, scatter = handle
    n = len(srcs)

    def body(*refs):
        srcs_r, lands_r = refs[:n], refs[n:2 * n]
        ssem, rsem, lsem = refs[2 * n], refs[2 * n + 1], refs[2 * n + 2]
        for cp in _xfer_copies(srcs_r, lands_r, ssem, rsem, scatter):
            cp.wait_send()
            cp.wait_recv()
        for cp in _own_copies(srcs_r, lands_r, lsem, scatter):
            cp.wait()

    out = pl.pallas_call(
        body, name=name,
        out_shape=(*[pltpu.HBM(a.shape, a.dtype) for a in srcs], *[pltpu.HBM(a.shape, a.dtype) for a in lands]),
        in_specs=[_HBM] * (2 * n) + [_SEM, _SEM, _SEM, pl.BlockSpec(memory_space=pl.ANY)],
        out_specs=tuple([_HBM] * (2 * n)),
        input_output_aliases={i: i for i in range(2 * n)},
        compiler_params=pltpu.CompilerParams(has_side_effects=_EFFECT),
    )(*srcs, *lands, send_sems, recv_sems, local_sems, after)
    return list(out[n:])


def _g2_first(srcs, lands, send_sems, recv_sems):
    x, y, c = _my_pos()
    me = 4 * x + 2 * y + c
    peers = [(x, y, 1 - c), (1 - x, y, c), (x, 1 - y, c), (1 - x, 1 - y, c)]
    return [[pltpu.make_async_remote_copy(
        src_ref=srcs[a], dst_ref=lands[a].at[me], send_sem=send_sems.at[4 * a + k], recv_sem=recv_sems.at[4 * a + k],
        device_id=p, device_id_type=MESH) for k, p in enumerate(peers)] for a in range(len(srcs))]


def _g2_relay(lands, send_sems, recv_sems):
    x, y, c = _my_pos()
    chips = [(1 - x, y), (x, 1 - y), (1 - x, 1 - y)]
    out = []
    for a in range(len(lands)):
        row = []
        for j, (px, py) in enumerate(chips):
            slab = lands[a].at[4 * px + 2 * py + c]
            row.append(pltpu.make_async_remote_copy(
                src_ref=slab, dst_ref=slab, send_sem=send_sems.at[3 * a + j], recv_sem=recv_sems.at[3 * a + j],
                device_id=(x, y, 1 - c), device_id_type=MESH))
        out.append(row)
    return out


def gather2_start(arrs, after, name):
    n = len(arrs)

    def body(*refs):
        srcs, lands = refs[:n], refs[n:2 * n]
        send_sems, recv_sems, local_sems = refs[2 * n + 1], refs[2 * n + 2], refs[2 * n + 3]
        for row in _g2_first(srcs, lands, send_sems, recv_sems):
            for cp in row:
                cp.start()
        for cp in _own_copies(srcs, lands, local_sems, False):
            cp.start()
        refs[-1][...] = jnp.zeros_like(refs[-1])

    lands0 = [lax.empty((N_DEV,) + a.shape, a.dtype) for a in arrs]
    out = pl.pallas_call(
        body, name=name,
        out_shape=(pltpu.SemaphoreType.DMA((4 * n,)), pltpu.SemaphoreType.DMA((4 * n,)), pltpu.SemaphoreType.DMA((n,)),
                   *[pltpu.HBM(a.shape, a.dtype) for a in arrs], *[pltpu.HBM(l.shape, l.dtype) for l in lands0],
                   _sds((8, 128), F32)),
        in_specs=[_HBM] * (2 * n) + [pl.BlockSpec(memory_space=pl.ANY)],
        out_specs=(_SEM, _SEM, _SEM, *([_HBM] * (2 * n)), pl.BlockSpec(memory_space=pltpu.VMEM)),
        input_output_aliases={i: 3 + i for i in range(2 * n)},
        compiler_params=pltpu.CompilerParams(has_side_effects=_EFFECT),
    )(*[pltpu.with_memory_space_constraint(a, pltpu.HBM) for a in arrs],
      *[pltpu.with_memory_space_constraint(l, pltpu.HBM) for l in lands0], after)
    return dict(send1=out[0], recv1=out[1], local=out[2], srcs=list(out[3:3 + n]), lands=list(out[3 + n:3 + 2 * n])), out[-1]


def gather2_relay(handle, after, name):
    n = len(handle["lands"])

    def body(*refs):
        lands = refs[:n]
        send1, recv1 = refs[n], refs[n + 1]
        send2, recv2 = refs[n + 3], refs[n + 4]
        firsts = _g2_first([l.at[0] for l in lands], lands, send1, recv1)
        relays = _g2_relay(lands, send2, recv2)
        for a in range(n):
            for j in range(3):
                firsts[a][1 + j].wait_recv()
                relays[a][j].start()
        refs[-1][...] = jnp.zeros_like(refs[-1])

    out = pl.pallas_call(
        body, name=name,
        out_shape=(pltpu.SemaphoreType.DMA((3 * n,)), pltpu.SemaphoreType.DMA((3 * n,)),
                   *[pltpu.HBM(l.shape, l.dtype) for l in handle["lands"]], _sds((8, 128), F32)),
        in_specs=[_HBM] * n + [_SEM, _SEM, pl.BlockSpec(memory_space=pl.ANY)],
        out_specs=(_SEM, _SEM, *([_HBM] * n), pl.BlockSpec(memory_space=pltpu.VMEM)),
        input_output_aliases={i: 2 + i for i in range(n)},
        compiler_params=pltpu.CompilerParams(has_side_effects=_EFFECT),
    )(*handle["lands"], handle["send1"], handle["recv1"], after)
    new = dict(handle)
    new.update(send2=out[0], recv2=out[1], lands=list(out[2:2 + n]))
    return new, out[-1]


def gather2_wait(handle, after, name):
    n = len(handle["lands"])

    def body(*refs):
        srcs, lands = refs[:n], refs[n:2 * n]
        send1, recv1, local, send2, recv2 = refs[2 * n:2 * n + 5]
        for a, row in enumerate(_g2_first(srcs, lands, send1, recv1)):
            row[0].wait_recv()
            for cp in row:
                cp.wait_send()
        for row in _g2_relay(lands, send2, recv2):
            for cp in row:
                cp.wait_send()
                cp.wait_recv()
        for cp in _own_copies(srcs, lands, local, False):
            cp.wait()

    out = pl.pallas_call(
        body, name=name,
        out_shape=(*[pltpu.HBM(a.shape, a.dtype) for a in handle["srcs"]],
                   *[pltpu.HBM(l.shape, l.dtype) for l in handle["lands"]]),
        in_specs=[_HBM] * (2 * n) + [_SEM] * 5 + [pl.BlockSpec(memory_space=pl.ANY)],
        out_specs=tuple([_HBM] * (2 * n)),
        input_output_aliases={i: i for i in range(2 * n)},
        compiler_params=pltpu.CompilerParams(has_side_effects=_EFFECT),
    )(*handle["srcs"], *handle["lands"], handle["send1"], handle["recv1"], handle["local"], handle["send2"],
      handle["recv2"], after)
    return list(out[n:])


def adamw(parts, w, m, v, name, row0=0, prev=None):
    r_tot, C = w.shape
    n_parts, R = parts.shape[0], parts.shape[1]
    row_bytes = 2 * (n_parts * C * parts.dtype.itemsize + 7 * C * 4)
    tr = R
    for cand in (512, 352, 256, 176, 128, 64):
        if R % cand == 0 and row0 % cand == 0 and R > cand and cand * row_bytes <= ADAMW_VMEM_BUDGET:
            tr = cand
            break
    tc = C
    if tr == R and R * row_bytes > ADAMW_VMEM_BUDGET:
        assert row0 == 0 and R == r_tot
        tc = next(t for t in (512, 256, 128) if C % t == 0 and R * row_bytes * t // C <= ADAMW_VMEM_BUDGET)
    assert row0 % tr == 0 and (tr % 8 == 0 or (tr == r_tot and row0 == 0))
    blk0 = row0 // tr
    c1 = 1.0 / (1.0 - ADAM_B1 ** ADAM_STEP)
    c2 = 1.0 / (1.0 - ADAM_B2 ** ADAM_STEP)

    def body(p_ref, w_ref, m_ref, v_ref, *rest):
        g_ref, d_ref, nm_ref, nv_ref = rest[-4:]
        g = p_ref[0].astype(F32)
        for k in range(1, n_parts):
            g = g + p_ref[k].astype(F32)
        nm = ADAM_B1 * m_ref[...] + (1.0 - ADAM_B1) * g
        nv = ADAM_B2 * v_ref[...] + (1.0 - ADAM_B2) * (g * g)
        g_ref[...] = g
        nm_ref[...] = nm
        nv_ref[...] = nv
        d_ref[...] = -ADAM_LR * ((nm * c1) / (jnp.sqrt(nv * c2) + ADAM_EPS) + ADAM_WD * w_ref[...])

    if tc == C:
        grid = (R // tr,)
        blk = pl.BlockSpec((tr, C), lambda i: (i + blk0, 0))
        p_spec = pl.BlockSpec((n_parts, tr, C), lambda i: (0, i, 0))
    else:
        grid = (C // tc,)
        blk = pl.BlockSpec((R, tc), lambda i: (0, i))
        p_spec = pl.BlockSpec((n_parts, R, tc), lambda i: (0, 0, i))
    in_specs = [p_spec, blk, blk, blk]
    args = [parts, w, m, v]
    aliases = {}
    if prev is not None:
        in_specs += [pl.BlockSpec(memory_space=pl.ANY)] * 4
        args += list(prev)
        aliases = {4 + k: k for k in range(4)}
    return pl.pallas_call(
        body, name=name, grid=grid, in_specs=in_specs, out_specs=(blk, blk, blk, blk),
        out_shape=tuple(_sds((r_tot, C), F32) for _ in range(4)), input_output_aliases=aliases,
        compiler_params=pltpu.CompilerParams(vmem_limit_bytes=VMEM_LIMIT))(*args)


def _ffn_fwd(x, nw, sh, sc, g, wt_gu, w_dn):
    h = norm_mod(x, nw, sh, sc, "ffn_norm")
    gp, up, a = ffn_up(h, wt_gu)
    y, xn = matmul(a, w_dn, "nn", BF, "ffn_down", res=x, gate=g, coef=0.5)
    return xn, (x, h, gp, up, a, y)


def _ffn_bwd(dxo, saved, nw, sc, g, wt_gu, w_dn):
    x, h, gp, up, a, y = saved
    dy, acc1 = resid_gate_bwd(dxo, y, g, 0.5, "ffn_gate_bwd")
    d_wdn = matmul(a, dy, "tn", BF, "ffn_down_wgrad")
    dg, du = ffn_down_dgrad(dy, w_dn, gp, up)
    d_wt = ffn_up_wgrad(dg, du, h)
    dh = ffn_up_dgrad(dg, du, wt_gu)
    dx, acc2 = norm_mod_bwd(x, dh, dxo, nw, sc, "ffn_norm_bwd")
    return dx, d_wt, d_wdn, (acc2[0], acc2[1], acc1[0]), acc2[2]


def _group_layout(a):
    L = a.shape[0]
    return a[:, :SSM_HEADS].reshape(L, SSM_GROUPS, _HPG).transpose(1, 0, 2)


def _ungroup_layout(a):
    L = a.shape[1]
    return jnp.pad(a.transpose(1, 0, 2).reshape(L, SSM_HEADS), ((0, 0), (0, 128 - SSM_HEADS)))


def _pad_row(vec, n=128):
    return jnp.pad(vec.reshape(1, -1), ((0, 0), (0, n - vec.shape[-1])))


def _mamba_fwd(x, nw, sh, sc, g, p):
    h = norm_mod(x, nw, sh, sc, "mix_norm")
    zx = matmul(h, p["w_in_t"], "nt", F32, "ssm_in")
    xc = conv_fwd(zx, p["conv_w"], p["conv_b"])
    dt, acs = dt_prep(zx, p["dt_bias"], p["a_log"])
    dt_g, acs_g = _group_layout(dt), _group_layout(acs)
    acs_t = acs_g.transpose(0, 2, 1)
    y, pst = ssd_fwd(xc, dt_g, acs_g, acs_t, p["d_exp"])
    yn = gate_norm(y, zx, p["norm_w"])
    yo, xn = matmul(yn, p["w_out"], "nn", BF, "ssm_out", res=x, gate=g, coef=1.0)
    return xn, (x, h, zx, xc, dt, dt_g, acs_g, acs_t, y, pst, yn, yo)


def _mamba_bwd(dxo, saved, nw, sc, g, p):
    x, h, zx, xc, dt, dt_g, acs_g, acs_t, y, pst, yn, yo = saved
    dyo, acc1 = resid_gate_bwd(dxo, yo, g, 1.0, "mix_gate_bwd")
    d_wout = matmul(yn, dyo, "tn", BF, "ssm_out_wgrad")
    dyn = matmul(dyo, p["w_out"], "nt", F32, "ssm_out_dgrad")
    dy, dz, accn = gate_norm_bwd(dyn, y, zx, p["norm_w"])
    dxs, dB, dC, ddt_g, da_g, dd = ssd_bwd(dy, xc, dt_g, acs_g, acs_t, pst, p["d_exp"])
    dxc = jnp.concatenate([dxs, dB, dC], axis=1)
    du, accc = conv_bwd(dxc, zx, p["conv_w"], p["conv_b"])
    draw, accdt = dt_bwd(_ungroup_layout(ddt_g), _ungroup_layout(da_g), dt, zx, p["dt_bias"], p["a_log"])
    dzx = jnp.concatenate([dz, du, draw], axis=1)
    d_win = matmul(dzx, h, "tn", BF, "ssm_in_wgrad")[:IN_PROJ]
    dh = matmul(dzx, p["w_in_t"], "nn", F32, "ssm_in_dgrad")
    dx, acc2 = norm_mod_bwd(x, dh, dxo, nw, sc, "mix_norm_bwd")
    small = dict(conv_w=accc[:CONV_WIDTH], conv_b=accc[CONV_WIDTH], dt_bias=accdt[0, :SSM_HEADS],
                 a_log=accdt[1, :SSM_HEADS], d=dd.reshape(SSM_HEADS, SSM_HEADDIM).sum(-1), norm_w=accn[0])
    return dx, d_win, d_wout, (acc2[0], acc2[1], acc1[0]), acc2[2], small


def _attn_layer_fwd(x, nw, sh, sc, g, p, kv):
    h = norm_mod(x, nw, sh, sc, "mix_norm")
    q = matmul(h, p["w_q"], "nn", F32, "attn_q", bias=p["b_q"])
    o = attn_fwd(q, kv, p["sinks"])
    yo, xn = matmul(o, p["w_o"], "nn", BF, "attn_o", bias=p["b_o"], res=x, gate=g, coef=1.0)
    return xn, (x, h, q, o, yo)


def _attn_layer_bwd(dxo, saved, nw, sc, g, p, kv):
    x, h, q, o, yo = saved
    dyo, acc1 = resid_gate_bwd(dxo, yo, g, 1.0, "mix_gate_bwd")
    d_wo = matmul(o, dyo, "tn", BF, "attn_o_wgrad")
    do = matmul(dyo, p["w_o"], "nt", F32, "attn_o_dgrad")
    dq, dkv_c, dkv_p, acca = attn_bwd(q, kv, do, p["sinks"])
    d_wq = matmul(h, dq, "tn", BF, "attn_q_wgrad")
    dh = matmul(dq, p["w_q"], "nt", F32, "attn_q_dgrad")
    dx, acc2 = norm_mod_bwd(x, dh, dxo, nw, sc, "mix_norm_bwd")
    small = dict(b_q=acca[0], sinks=acca[1, :ATT_HEADS], b_o=acc1[1])
    return dx, d_wq, d_wo, (acc2[0], acc2[1], acc1[0]), acc2[2], small, (dkv_c, dkv_p)


def _pack_rows(pieces):
    rows, spans, off = [], [], 0
    for a in pieces:
        flat = a.reshape(-1).astype(F32)
        n = -(-flat.shape[0] // D_MODEL)
        rows.append(jnp.pad(flat, (0, n * D_MODEL - flat.shape[0])).reshape(n, D_MODEL))
        spans.append((off, a.shape))
        off += n
    pad = -off % 8
    if pad:
        rows.append(jnp.zeros((pad, D_MODEL), F32))
    return jnp.concatenate(rows, axis=0), spans, off + pad


def _unpack_rows(g, spans):
    out = []
    for off, shape in spans:
        size = 1
        for s in shape:
            size *= s
        n = -(-size // D_MODEL)
        out.append(g[:, off:off + n].reshape(N_DEV, n * D_MODEL)[:, :size].reshape((N_DEV,) + tuple(shape)))
    return out


def _unshard_last(g):
    nd = g.ndim
    perm = tuple(range(1, nd - 1)) + (0, nd - 1)
    t = g.transpose(perm)
    return t.reshape(t.shape[:-2] + (N_DEV * g.shape[-1],))


def _shard_last(a, me):
    s = a.shape[-1] // N_DEV
    return lax.dynamic_slice_in_dim(a, me * s, s, axis=a.ndim - 1)


def kernel(x, c, ffn_norm_w, ffn_w_gu, ffn_w_down, mod_w, mod_b, mix_norm_w, ssm_w_in, ssm_conv_w, ssm_conv_b, ssm_dt_bias, ssm_a_log, ssm_d, ssm_norm_w, ssm_w_out, kv_norm_w, kv_mod_w, kv_mod_b, w_kv, b_kv, attn_w_q, attn_b_q, attn_sinks, attn_w_o, attn_b_o, final_norm_w, loss_target, m_ffn_norm_w, m_ffn_w_gu, m_ffn_w_down, m_mod_w, m_mod_b, m_mix_norm_w, m_ssm_w_in, m_ssm_conv_w, m_ssm_conv_b, m_ssm_dt_bias, m_ssm_a_log, m_ssm_d, m_ssm_norm_w, m_ssm_w_out, m_kv_norm_w, m_kv_mod_w, m_kv_mod_b, m_w_kv, m_b_kv, m_attn_w_q, m_attn_b_q, m_attn_sinks, m_attn_w_o, m_attn_b_o, m_final_norm_w, v_ffn_norm_w, v_ffn_w_gu, v_ffn_w_down, v_mod_w, v_mod_b, v_mix_norm_w, v_ssm_w_in, v_ssm_conv_w, v_ssm_conv_b, v_ssm_dt_bias, v_ssm_a_log, v_ssm_d, v_ssm_norm_w, v_ssm_w_out, v_kv_norm_w, v_kv_mod_w, v_kv_mod_b, v_w_kv, v_b_kv, v_attn_w_q, v_attn_b_q, v_attn_sinks, v_attn_w_o, v_attn_b_o, v_final_norm_w):
    D = D_MODEL
    me = 4 * lax.axis_index("x") + 2 * lax.axis_index("y") + lax.axis_index("c")
    xs = x[0]
    target = loss_target[0]
    mod_cols = mod_w.shape[-1]
    kvm_cols = kv_mod_w.shape[-1]

    def fence(arrs):
        tot = jnp.zeros((1, 1), F32)
        for a in arrs:
            tot = tot + lax.slice(a, (0,) * a.ndim, (1,) * a.ndim).reshape(1, 1).astype(F32)
        return jnp.broadcast_to(tot, (8, 128))

    packed, spans, _ = _pack_rows([c, ffn_norm_w, ssm_conv_w, ssm_conv_b, ssm_norm_w])
    nrow = packed.shape[0]
    g1 = small_all_gather(packed).reshape(N_DEV, nrow, D)
    c_all, fnw_g, cw_g, cb_g, snw_g = _unpack_rows(g1, spans)
    c_all = c_all.reshape(N_DEV, D)
    ffn_nw = _unshard_last(fnw_g)
    conv_w = _unshard_last(cw_g)
    conv_b = _unshard_last(cb_g)
    ssm_nw = _unshard_last(snw_g)

    mod_b_loc = lax.dynamic_slice_in_dim(mod_b, me * mod_cols, mod_cols, axis=1).reshape(DEPTH, 1, mod_cols)
    kvb_loc = lax.dynamic_slice_in_dim(kv_mod_b, me * kvm_cols, kvm_cols, axis=0).reshape(1, 1, kvm_cols)
    modp, c_act = mod_fwd(c_all, mod_w, mod_b_loc, "mod_fwd")
    kvmp, _ = mod_fwd(c_all, kv_mod_w.reshape(1, D, kvm_cols), kvb_loc, "kv_mod_fwd")
    packed2, spans2, _ = _pack_rows([modp, kvmp])
    nrow2 = packed2.shape[0]
    g2 = small_all_gather(packed2).reshape(N_DEV, nrow2, D)
    modp_g, kvmp_g = _unpack_rows(g2, spans2)
    mod_all = modp_g.transpose(1, 2, 0, 3).reshape(DEPTH, N_DEV, N_MOD * D)
    kvm_all = kvmp_g.transpose(1, 2, 0, 3).reshape(N_DEV, 2 * D)
    mod_me = lax.dynamic_index_in_dim(mod_all, me, axis=1, keepdims=False).reshape(DEPTH, N_MOD, 1, D)
    kvm_me = lax.dynamic_index_in_dim(kvm_all, me, axis=0, keepdims=False).reshape(2, 1, D)

    gu_t = jnp.swapaxes(ffn_w_gu, 2, 3)
    win_t = jnp.transpose(ssm_w_in, (2, 0, 1))
    S = gu_t.shape[2]
    s_in = win_t.shape[0]
    r_dn = ffn_w_down.shape[2]
    r_mix = ssm_w_out.shape[1]
    r_at = attn_w_q.shape[1]

    def layer_pack(k):
        arrs = [gu_t[k, 0].astype(BF), gu_t[k, 1].astype(BF), ffn_w_down[k, 0].astype(BF), ffn_w_down[k, 1].astype(BF)]
        if k < N_A:
            arrs += [ssm_w_out[k].astype(BF), win_t[:, k].astype(BF)]
        else:
            arrs += [attn_w_q[k - N_A].astype(BF), attn_w_o[k - N_A].astype(BF)]
        if k == N_A:
            arrs.append(w_kv.astype(BF))
        return arrs

    packs = [layer_pack(k) for k in range(DEPTH)]
    gathered = [None] * DEPTH
    first = big_all_gather([packs[0][0], packs[0][2]])
    pending = [None] * DEPTH
    pend_mix, tok_next = gather2_start(packs[0][4:6], fence([g2, first[0]]), "gather_start_0m")
    pending[0], tok_next = gather2_start([packs[0][1], packs[0][3]], tok_next, "gather_start_0")
    for k in range(1, DEPTH):
        pending[k], tok_next = gather2_start(packs[k], tok_next, "gather_start_%d" % k)
    gathered[0] = [first[0], None, first[1], None, None, None]

    def wt_gu_full(i, j):
        return gathered[i][j].reshape(N_DEV * S, D)

    def w_dn_full(i, j):
        return gathered[i][2 + j].reshape(D_FF, D)

    def mix_rows(i, a):
        return gathered[i][4 + a].reshape(-1, D)

    def mamba_params(j):
        w_in_t = jnp.pad(mix_rows(j, 1), ((0, IN_PROJ_PAD - IN_PROJ), (0, 0)))
        return dict(w_in_t=w_in_t, w_out=mix_rows(j, 0), conv_w=conv_w[j], conv_b=conv_b[j].reshape(1, -1),
                    dt_bias=_pad_row(ssm_dt_bias[j]), a_log=_pad_row(ssm_a_log[j]),
                    d_exp=jnp.repeat(ssm_d[j], SSM_HEADDIM).reshape(SSM_GROUPS, 1, _GW),
                    norm_w=ssm_nw[j].reshape(1, -1))

    def attn_params(j):
        return dict(w_q=mix_rows(N_A + j, 0), w_o=mix_rows(N_A + j, 1),
                    b_q=attn_b_q[j].reshape(1, -1), b_o=attn_b_o[j].reshape(1, -1), sinks=_pad_row(attn_sinks[j]))

    saved = []
    kv = None
    kv_saved = None
    w_kv_full = None
    xcur = xs
    for i in range(DEPTH):
        if i >= 1:
            gathered[i] = gather2_wait(pending[i], xcur, "gather_wait_%d" % i)
        md = mod_me[i]
        if i == 0:
            md = md + tok_next[0, 0]
        if i == N_A:
            w_kv_full = gathered[N_A][6].reshape(D, KV_DIM)
            h_kv = norm_mod(xcur, kv_norm_w.reshape(1, D), kvm_me[0], kvm_me[1], "kv_norm")
            kv = matmul(h_kv, w_kv_full, "nn", F32, "kv_proj", bias=b_kv.reshape(1, -1))
            kv_saved = (xcur, h_kv)
        x1, s1 = _ffn_fwd(xcur, ffn_nw[i, 0].reshape(1, D), md[0], md[1], md[2], wt_gu_full(i, 0), w_dn_full(i, 0))
        gm = md[5]
        if i == 0:
            pend_mix, tok_r = gather2_relay(pend_mix, x1, "gather_relay_0m")
            gathered[0][4:6] = gather2_wait(pend_mix, tok_r, "gather_wait_0m")
        if i < N_A:
            pm = mamba_params(i)
            x2, s2 = _mamba_fwd(x1, mix_norm_w[i].reshape(1, D), md[3], md[4], gm, pm)
        else:
            pm = attn_params(i - N_A)
            x2, s2 = _attn_layer_fwd(x1, mix_norm_w[i].reshape(1, D), md[3], md[4], gm, pm, kv)
        g2f = md[8]
        if i + 1 < DEPTH:
            pending[i + 1], tok_r = gather2_relay(pending[i + 1], x2, "gather_relay_%d" % (i + 1))
            g2f = g2f + tok_r[0, 0]
        if i == 0:
            pending[0], tok_r = gather2_relay(pending[0], x2, "gather_relay_0")
            rest = gather2_wait(pending[0], tok_r, "gather_wait_0")
            gathered[0] = [first[0], rest[0], first[1], rest[1]] + gathered[0][4:6]
        x3, s3 = _ffn_fwd(x2, ffn_nw[i, 1].reshape(1, D), md[6], md[7], g2f, wt_gu_full(i, 1), w_dn_full(i, 1))
        saved.append((s1, s2, s3, pm))
        xcur = x3

    dx, accf = final_loss(xcur, final_norm_w.reshape(1, D), target)
    d_mod = [None] * DEPTH
    d_ffn_nw = [[None, None] for _ in range(DEPTH)]
    d_mix_nw = [None] * DEPTH
    sm_m, sm_a = [None] * N_A, [None] * N_A
    kv_parts = [None] * N_A
    d_kvm = d_kv_nw = d_bkv = None
    exchanges = []
    tok = None

    def send(arrs, tag, after=None):
        handle, t = xfer_start(arrs, True, dx if after is None else after, "exch_start_%s" % tag)
        exchanges.append((handle, tag))
        return t

    def ffn_slabs(d_wt, d_wdn):
        return [d_wt.reshape(N_DEV, S, D), d_wdn.reshape(N_DEV, r_dn, D)]

    for i in reversed(range(DEPTH)):
        md = mod_me[i]
        s1, s2, s3, pm = saved[i]
        g2 = md[8] if tok is None else md[8] + tok[0, 0]
        dx, d_wt, d_wdn, m2, d_ffn_nw[i][1] = _ffn_bwd(
            dx, s3, ffn_nw[i, 1].reshape(1, D), md[7], g2, wt_gu_full(i, 1), w_dn_full(i, 1))
        tok = send(ffn_slabs(d_wt, d_wdn), "f%d1" % i)
        gm = md[5] + tok[0, 0]
        if i < N_A:
            dx, d_in, d_out, mm_, d_mix_nw[i], sm_m[i] = _mamba_bwd(dx, s2, mix_norm_w[i].reshape(1, D), md[4], gm, pm)
            tok = send([d_in.reshape(N_DEV, s_in, D), d_out.reshape(N_DEV, r_mix, D)], "m%d" % i)
        else:
            j = i - N_A
            dx, d_q, d_o, mm_, d_mix_nw[i], sm_a[j], kv_parts[j] = _attn_layer_bwd(
                dx, s2, mix_norm_w[i].reshape(1, D), md[4], gm, pm, kv)
            tok = send([d_q.reshape(N_DEV, r_at, D), d_o.reshape(N_DEV, r_at, D)], "m%d" % i)
        g1 = md[2] + tok[0, 0]
        dx, d_wt, d_wdn, m1, d_ffn_nw[i][0] = _ffn_bwd(
            dx, s1, ffn_nw[i, 0].reshape(1, D), md[1], g1, wt_gu_full(i, 0), w_dn_full(i, 0))
        d_mod[i] = jnp.concatenate(list(m1) + list(mm_) + list(m2), axis=0)
        last = ffn_slabs(d_wt, d_wdn)
        if i == N_A:
            x_kv, h_kv = kv_saved
            dkv, acck = kv_grad_combine(kv_parts)
            d_bkv = acck[0]
            d_kv_w = matmul(h_kv, dkv, "tn", BF, "kv_wgrad")
            dh_kv = matmul(dkv, w_kv_full, "nt", F32, "kv_dgrad")
            dx, acc_kv = norm_mod_bwd(x_kv, dh_kv, dx, kv_norm_w.reshape(1, D), kvm_me[1], "kv_norm_bwd")
            d_kvm = jnp.concatenate([acc_kv[0], acc_kv[1]], axis=0)
            d_kv_nw = acc_kv[2]
            last.append(d_kv_w.reshape(N_DEV, -1, KV_DIM))
        if i > 0:
            tok = send(last, "f%d0" % i)
    grad_x = dx.reshape(x.shape)

    small_list = [
        jnp.stack(d_mod, 0), d_kvm,
        jnp.stack([jnp.stack(r, 0) for r in d_ffn_nw], 0),
        jnp.stack(d_mix_nw, 0),
        jnp.stack([s["conv_w"] for s in sm_m], 0), jnp.stack([s["conv_b"] for s in sm_m], 0),
        jnp.stack([s["dt_bias"] for s in sm_m], 0), jnp.stack([s["a_log"] for s in sm_m], 0),
        jnp.stack([s["d"] for s in sm_m], 0), jnp.stack([s["norm_w"] for s in sm_m], 0),
        d_kv_nw, d_bkv,
        jnp.stack([s["b_q"] for s in sm_a], 0), jnp.stack([s["sinks"] for s in sm_a], 0),
        jnp.stack([s["b_o"] for s in sm_a], 0), accf[0], accf[1],
    ]
    packed3, spans3, _ = _pack_rows(small_list)
    nrow3 = packed3.shape[0]
    g3 = small_all_gather(packed3).reshape(N_DEV, nrow3, D)
    tok_last = send(last, "f00", after=g3)
    (p_mod, p_kvm, p_fnw, p_mnw, p_cw, p_cb, p_dtb, p_al, p_d, p_snw, p_kvnw, p_bkv, p_bq, p_sk, p_bo, p_fin,
     p_loss) = _unpack_rows(g3, spans3)

    loss = 0.5 / D * jnp.sum(p_loss)

    c_act_t = c_act.T
    dmod_loc = _shard_last(p_mod, me).transpose(1, 0, 2)
    dkvm_loc = _shard_last(p_kvm, me).reshape(1, N_DEV, kvm_cols) + tok_last[0, 0]
    gp_mod_w = mod_wgrad(c_act_t, dmod_loc, "mod_wgrad")
    gp_kvm_w = mod_wgrad(c_act_t, dkvm_loc, "kv_mod_wgrad")[0]

    def as_parts_single(a):
        return a[None]

    def upd(name, parts, w, m, v):
        shp = w.shape
        c_last = shp[-1]
        out = adamw(parts.reshape(parts.shape[0], -1, c_last), w.reshape(-1, c_last), m.reshape(-1, c_last),
                    v.reshape(-1, c_last), "adamw_" + name)
        return tuple(o.reshape(shp) for o in out)

    views = {
        "ffn_w_gu": [jnp.swapaxes(t, 2, 3).reshape(-1, D) for t in (ffn_w_gu, m_ffn_w_gu, v_ffn_w_gu)],
        "ffn_w_down": [t.reshape(-1, D) for t in (ffn_w_down, m_ffn_w_down, v_ffn_w_down)],
        "ssm_w_out": [t.reshape(-1, D) for t in (ssm_w_out, m_ssm_w_out, v_ssm_w_out)],
        "attn_w_q": [t.reshape(-1, D) for t in (attn_w_q, m_attn_w_q, v_attn_w_q)],
        "attn_w_o": [t.reshape(-1, D) for t in (attn_w_o, m_attn_w_o, v_attn_w_o)],
    }
    filled = {k: None for k in views}

    def upd_rows(name, parts, row0):
        w, m, v = views[name]
        filled[name] = adamw(parts, w, m, v, "adamw_" + name, row0=row0, prev=filled[name])
        return filled[name][3]

    res = {}
    res["ffn_norm_w"] = upd("ffn_norm_w", _shard_last(p_fnw, me), ffn_norm_w, m_ffn_norm_w, v_ffn_norm_w)
    res["mod_w"] = upd("mod_w", as_parts_single(gp_mod_w), mod_w, m_mod_w, v_mod_w)
    res["mod_b"] = upd("mod_b", p_mod, mod_b, m_mod_b, v_mod_b)
    res["mix_norm_w"] = upd("mix_norm_w", p_mnw, mix_norm_w, m_mix_norm_w, v_mix_norm_w)
    res["ssm_conv_w"] = upd("ssm_conv_w", _shard_last(p_cw, me), ssm_conv_w, m_ssm_conv_w, v_ssm_conv_w)
    res["ssm_conv_b"] = upd("ssm_conv_b", _shard_last(p_cb, me), ssm_conv_b, m_ssm_conv_b, v_ssm_conv_b)
    res["ssm_dt_bias"] = upd("ssm_dt_bias", p_dtb, ssm_dt_bias, m_ssm_dt_bias, v_ssm_dt_bias)
    res["ssm_a_log"] = upd("ssm_a_log", p_al, ssm_a_log, m_ssm_a_log, v_ssm_a_log)
    res["ssm_d"] = upd("ssm_d", p_d, ssm_d, m_ssm_d, v_ssm_d)
    res["ssm_norm_w"] = upd("ssm_norm_w", _shard_last(p_snw, me), ssm_norm_w, m_ssm_norm_w, v_ssm_norm_w)
    res["kv_norm_w"] = upd("kv_norm_w", p_kvnw.reshape(N_DEV, 1, D), kv_norm_w.reshape(1, D),
                           m_kv_norm_w.reshape(1, D), v_kv_norm_w.reshape(1, D))
    res["kv_mod_w"] = upd("kv_mod_w", as_parts_single(gp_kvm_w), kv_mod_w, m_kv_mod_w, v_kv_mod_w)
    res["kv_mod_b"] = upd("kv_mod_b", p_kvm.reshape(N_DEV, 1, 2 * D), kv_mod_b.reshape(1, -1),
                          m_kv_mod_b.reshape(1, -1), v_kv_mod_b.reshape(1, -1))
    res["b_kv"] = upd("b_kv", p_bkv.reshape(N_DEV, 1, KV_DIM), b_kv.reshape(1, -1), m_b_kv.reshape(1, -1),
                      v_b_kv.reshape(1, -1))
    res["attn_b_q"] = upd("attn_b_q", p_bq, attn_b_q, m_attn_b_q, v_attn_b_q)
    res["attn_sinks"] = upd("attn_sinks", p_sk, attn_sinks, m_attn_sinks, v_attn_sinks)
    res["attn_b_o"] = upd("attn_b_o", p_bo, attn_b_o, m_attn_b_o, v_attn_b_o)
    res["final_norm_w"] = upd("final_norm_w", p_fin.reshape(N_DEV, 1, D), final_norm_w.reshape(1, D),
                              m_final_norm_w.reshape(1, D), v_final_norm_w.reshape(1, D))

    chain = fence([dx, tok_last] + [t[3] for t in res.values()])
    r_in_parts = [None] * N_A
    r_kv = None
    for handle, tag in exchanges:
        got = xfer_wait(handle, chain, "exch_wait_%s" % tag)
        i = int(tag[1])
        if tag[0] == "f":
            jf = int(tag[2])
            done = [upd_rows("ffn_w_gu", got[0], (2 * i + jf) * S), upd_rows("ffn_w_down", got[1], (2 * i + jf) * r_dn)]
            if len(got) > 2:
                res["w_kv"] = upd("w_kv", got[2], w_kv, m_w_kv, v_w_kv)
                done.append(res["w_kv"][3])
        elif i < N_A:
            r_in_parts[i] = got[0]
            done = [upd_rows("ssm_w_out", got[1], i * r_mix)]
            if i == 0:
                win_out = adamw(jnp.stack(r_in_parts, axis=2).reshape(N_DEV, s_in * N_A, D),
                                *[jnp.transpose(t, (2, 0, 1)).reshape(-1, D) for t in (ssm_w_in, m_ssm_w_in, v_ssm_w_in)],
                                "adamw_ssm_w_in")
                res["ssm_w_in"] = tuple(jnp.transpose(t.reshape(win_t.shape), (1, 2, 0)) for t in win_out)
                done.append(win_out[3])
        else:
            done = [upd_rows("attn_w_q", got[0], (i - N_A) * r_at), upd_rows("attn_w_o", got[1], (i - N_A) * r_at)]
        chain = fence(done)

    res["ffn_w_gu"] = tuple(jnp.swapaxes(t.reshape(gu_t.shape), 2, 3) for t in filled["ffn_w_gu"])
    res["ffn_w_down"] = tuple(t.reshape(ffn_w_down.shape) for t in filled["ffn_w_down"])
    res["ssm_w_out"] = tuple(t.reshape(ssm_w_out.shape) for t in filled["ssm_w_out"])
    res["attn_w_q"] = tuple(t.reshape(attn_w_q.shape) for t in filled["attn_w_q"])
    res["attn_w_o"] = tuple(t.reshape(attn_w_o.shape) for t in filled["attn_w_o"])

    names = ["ffn_norm_w", "ffn_w_gu", "ffn_w_down", "mod_w", "mod_b", "mix_norm_w", "ssm_w_in", "ssm_conv_w",
             "ssm_conv_b", "ssm_dt_bias", "ssm_a_log", "ssm_d", "ssm_norm_w", "ssm_w_out", "kv_norm_w", "kv_mod_w",
             "kv_mod_b", "w_kv", "b_kv", "attn_w_q", "attn_b_q", "attn_sinks", "attn_w_o", "attn_b_o", "final_norm_w"]
    vec_shapes = {"kv_norm_w": (D,), "kv_mod_b": (2 * D,), "b_kv": (KV_DIM,), "final_norm_w": (D,)}
    outs = [loss, grad_x]
    for k in range(4):
        for nme in names:
            t = res[nme][k]
            if nme in vec_shapes:
                t = t.reshape(vec_shapes[nme])
            outs.append(t)
    return tuple(outs)
```

```python
import functools

import jax
import jax.numpy as jnp
from jax import lax
from jax.experimental import pallas as pl
from jax.experimental.pallas import tpu as pltpu

F32 = jnp.float32
BF = jnp.bfloat16
MESH = pl.DeviceIdType.MESH

N_DEV = 8
D_MODEL = 1024
DEPTH = 4
N_A = 2
EPS = 1e-5
N_MOD = 9
D_FF = 2816
D_INNER = 2048
SSM_HEADDIM = 64
SSM_HEADS = 32
SSM_GROUPS = 8
SSM_STATE = 128
CONV_WIDTH = 4
CHUNK = 512
CONV_DIM = D_INNER + 2 * SSM_GROUPS * SSM_STATE
IN_PROJ = D_INNER + CONV_DIM + SSM_HEADS
IN_PROJ_PAD = D_INNER + CONV_DIM + 128
ATT_HEADS = 16
KV_HEADS = 4
HEAD_DIM = 64
WINDOW = 128
KV_DIM = 2 * KV_HEADS * HEAD_DIM

ADAM_LR = 0.001
ADAM_B1 = 0.9
ADAM_B2 = 0.999
ADAM_EPS = 1e-08
ADAM_WD = 0.01
ADAM_STEP = 10

VMEM_LIMIT = 48 * 2 ** 20
ADAMW_VMEM_BUDGET = 24 * 2 ** 20
NEG = -1e30


def _call(body, name, grid, in_specs, out_specs, out_shape, scratch=()):
    return pl.pallas_call(
        body, name=name, grid=grid, in_specs=in_specs, out_specs=out_specs, out_shape=out_shape,
        scratch_shapes=list(scratch),
        compiler_params=pltpu.CompilerParams(vmem_limit_bytes=VMEM_LIMIT))


def _tile(n, cap):
    t = (cap // 128) * 128
    while t >= 128:
        if n % t == 0:
            return t
        t -= 128
    return n


def _sds(shape, dtype):
    return jax.ShapeDtypeStruct(shape, dtype)


def _sigmoid(v):
    return 1.0 / (1.0 + jnp.exp(-v))


def _dot(a, b, dims):
    return lax.dot_general(a, b, (dims, ((), ())), preferred_element_type=F32)


def _dot_nn(a, b):
    return _dot(a.astype(BF), b.astype(BF), ((1,), (0,)))


def _dot_nt(a, b):
    return _dot(a.astype(BF), b.astype(BF), ((1,), (1,)))


def _dot_tn(a, b):
    return _dot(a.astype(BF), b.astype(BF), ((0,), (0,)))


def matmul(a, b, mode, out_dtype, name, bias=None, res=None, gate=None, coef=1.0):
    if mode == "nn":
        (M, K), (_, N) = a.shape, b.shape
    elif mode == "nt":
        (M, K), (N, _) = a.shape, b.shape
    else:
        (K, M), (_, N) = a.shape, b.shape
    cap_n = 512 if K > 4096 else 1024
    tm = _tile(M, 1024 if (mode == "tn" or K <= D_FF) else 512)
    tn = _tile(N, cap_n)
    if mode != "tn" and tm * tn > 1024 * 896:
        tn = _tile(N, 512)
    if mode == "nn":
        a_spec = pl.BlockSpec((tm, K), lambda i, j: (i, 0))
        b_spec = pl.BlockSpec((K, tn), lambda i, j: (0, j))
        fn = _dot_nn
    elif mode == "nt":
        a_spec = pl.BlockSpec((tm, K), lambda i, j: (i, 0))
        b_spec = pl.BlockSpec((tn, K), lambda i, j: (j, 0))
        fn = _dot_nt
    else:
        a_spec = pl.BlockSpec((K, tm), lambda i, j: (0, i))
        b_spec = pl.BlockSpec((K, tn), lambda i, j: (0, j))
        fn = _dot_tn
    has_bias, has_res = bias is not None, res is not None
    o_spec = pl.BlockSpec((tm, tn), lambda i, j: (i, j))
    v_spec = pl.BlockSpec((1, tn), lambda i, j: (0, j))
    in_specs, args = [a_spec, b_spec], [a, b]
    if has_bias:
        in_specs.append(v_spec)
        args.append(bias)
    if has_res:
        in_specs += [o_spec, v_spec]
        args += [res, gate]

    def body(*refs):
        a_ref, b_ref = refs[0], refs[1]
        k = 2
        y = fn(a_ref[...], b_ref[...])
        if has_bias:
            y = y + refs[k][...]
            k += 1
        if has_res:
            res_ref, gate_ref = refs[k], refs[k + 1]
            refs[k + 2][...] = y.astype(out_dtype)
            refs[k + 3][...] = res_ref[...] + coef * gate_ref[...] * y
        else:
            refs[k][...] = y.astype(out_dtype)

    if has_res:
        out_shape = (_sds((M, N), out_dtype), _sds((M, N), F32))
        out_specs = (o_spec, o_spec)
    else:
        out_shape = _sds((M, N), out_dtype)
        out_specs = o_spec
    return _call(body, name, (M // tm, N // tn), in_specs, out_specs, out_shape)(*args)


def norm_mod(x, nw, sh, sc, name):
    L, D = x.shape
    tm = _tile(L, 512)

    def body(x_ref, nw_ref, sh_ref, sc_ref, h_ref):
        xf = x_ref[...]
        r = lax.rsqrt(jnp.mean(xf * xf, axis=-1, keepdims=True) + EPS)
        n = xf * r * nw_ref[...]
        h_ref[...] = (n * (1.0 + sc_ref[...]) + sh_ref[...]).astype(BF)

    row = pl.BlockSpec((tm, D), lambda i: (i, 0))
    vec = pl.BlockSpec((1, D), lambda i: (0, 0))
    return _call(body, name, (L // tm,), [row, vec, vec, vec], row, _sds((L, D), BF))(x, nw, sh, sc)


def norm_mod_bwd(x, dh, dres, nw, sc, name):
    L, D = x.shape
    tm = _tile(L, 512)

    def body(x_ref, dh_ref, dres_ref, nw_ref, sc_ref, dx_ref, acc_ref):
        @pl.when(pl.program_id(0) == 0)
        def _():
            acc_ref[...] = jnp.zeros_like(acc_ref)

        xf = x_ref[...]
        dhf = dh_ref[...].astype(F32)
        r = lax.rsqrt(jnp.mean(xf * xf, axis=-1, keepdims=True) + EPS)
        xhat = xf * r
        nwv = nw_ref[...]
        dn = dhf * (1.0 + sc_ref[...])
        dxhat = dn * nwv
        proj = jnp.mean(dxhat * xhat, axis=-1, keepdims=True)
        dx_ref[...] = dres_ref[...] + r * (dxhat - xhat * proj)
        acc_ref[0:1, :] += jnp.sum(dhf, axis=0, keepdims=True)
        acc_ref[1:2, :] += jnp.sum(dhf * xhat * nwv, axis=0, keepdims=True)
        acc_ref[2:3, :] += jnp.sum(dn * xhat, axis=0, keepdims=True)

    row = pl.BlockSpec((tm, D), lambda i: (i, 0))
    vec = pl.BlockSpec((1, D), lambda i: (0, 0))
    acc = pl.BlockSpec((8, D), lambda i: (0, 0))
    return _call(body, name, (L // tm,), [row, row, row, vec, vec], (row, acc),
                 (_sds((L, D), F32), _sds((8, D), F32)))(x, dh, dres, nw, sc)


def final_loss(x, nw, target):
    L, D = x.shape
    tm = _tile(L, 512)

    def body(x_ref, nw_ref, t_ref, dx_ref, acc_ref):
        @pl.when(pl.program_id(0) == 0)
        def _():
            acc_ref[...] = jnp.zeros_like(acc_ref)

        xf = x_ref[...]
        r = lax.rsqrt(jnp.mean(xf * xf, axis=-1, keepdims=True) + EPS)
        xhat = xf * r
        nwv = nw_ref[...]
        err = xhat * nwv - t_ref[...]
        dy = err * (1.0 / D)
        dxhat = dy * nwv
        proj = jnp.mean(dxhat * xhat, axis=-1, keepdims=True)
        dx_ref[...] = r * (dxhat - xhat * proj)
        acc_ref[0:1, :] += jnp.sum(dy * xhat, axis=0, keepdims=True)
        acc_ref[1:2, :] += jnp.sum(err * err, axis=0, keepdims=True)

    row = pl.BlockSpec((tm, D), lambda i: (i, 0))
    vec = pl.BlockSpec((1, D), lambda i: (0, 0))
    acc = pl.BlockSpec((8, D), lambda i: (0, 0))
    return _call(body, "final_loss", (L // tm,), [row, vec, row], (row, acc),
                 (_sds((L, D), F32), _sds((8, D), F32)))(x, nw, target)


def resid_gate_bwd(dxo, y, gate, coef, name):
    L, D = dxo.shape
    tm = _tile(L, 512)

    def body(dxo_ref, y_ref, g_ref, dy_ref, acc_ref):
        @pl.when(pl.program_id(0) == 0)
        def _():
            acc_ref[...] = jnp.zeros_like(acc_ref)

        d = dxo_ref[...]
        dy = coef * g_ref[...] * d
        dy_ref[...] = dy.astype(BF)
        acc_ref[0:1, :] += coef * jnp.sum(d * y_ref[...].astype(F32), axis=0, keepdims=True)
        acc_ref[1:2, :] += jnp.sum(dy, axis=0, keepdims=True)

    row = pl.BlockSpec((tm, D), lambda i: (i, 0))
    vec = pl.BlockSpec((1, D), lambda i: (0, 0))
    acc = pl.BlockSpec((8, D), lambda i: (0, 0))
    return _call(body, name, (L // tm,), [row, row, vec], (row, acc),
                 (_sds((L, D), BF), _sds((8, D), F32)))(dxo, y, gate)


def ffn_up(x, nw, sh, sc, wt):
    L, D = x.shape
    F = wt.shape[0] // 2
    tm, tn = _tile(L, 2048), _tile(F, 256)
    nj = F // tn

    def body(x_ref, nw_ref, sh_ref, sc_ref, wg_ref, wu_ref, h_ref, g_ref, u_ref, a_ref, hs_ref):
        @pl.when(pl.program_id(1) == 0)
        def _():
            xf = x_ref[...]
            r = lax.rsqrt(jnp.mean(xf * xf, axis=-1, keepdims=True) + EPS)
            hv0 = (xf * r * nw_ref[...] * (1.0 + sc_ref[...]) + sh_ref[...]).astype(BF)
            hs_ref[...] = hv0
            h_ref[...] = hv0

        hv = hs_ref[...]
        g = _dot_nt(hv, wg_ref[...])
        u = _dot_nt(hv, wu_ref[...])
        g_ref[...] = g.astype(BF)
        u_ref[...] = u.astype(BF)
        a_ref[...] = (g * _sigmoid(g) * u).astype(BF)

    o = pl.BlockSpec((tm, tn), lambda i, n: (i, n))
    row = pl.BlockSpec((tm, D), lambda i, n: (i, 0))
    vec = pl.BlockSpec((1, D), lambda i, n: (0, 0))
    return _call(body, "ffn_up", (L // tm, nj),
                 [row, vec, vec, vec,
                  pl.BlockSpec((tn, D), lambda i, n: (n, 0)),
                  pl.BlockSpec((tn, D), lambda i, n: (n + nj, 0))],
                 (row, o, o, o), (_sds((L, D), BF),) + tuple(_sds((L, F), BF) for _ in range(3)),
                 scratch=[pltpu.VMEM((tm, D), BF)])(x, nw, sh, sc, wt, wt)


def ffn_down_dgrad(dy, wd, g, u):
    L, D = dy.shape
    F = wd.shape[0]
    tm, tn = _tile(L, 2048), _tile(F, 256)

    def body(dy_ref, w_ref, g_ref, u_ref, dg_ref, du_ref):
        da = _dot_nt(dy_ref[...], w_ref[...])
        gv = g_ref[...].astype(F32)
        uv = u_ref[...].astype(F32)
        s = _sigmoid(gv)
        dg_ref[...] = (da * uv * s * (1.0 + gv * (1.0 - s))).astype(BF)
        du_ref[...] = (da * gv * s).astype(BF)

    o = pl.BlockSpec((tm, tn), lambda i, n: (i, n))
    return _call(body, "ffn_down_dgrad", (L // tm, F // tn),
                 [pl.BlockSpec((tm, D), lambda i, n: (i, 0)), pl.BlockSpec((tn, D), lambda i, n: (n, 0)), o, o],
                 (o, o), (_sds((L, F), BF), _sds((L, F), BF)))(dy, wd, g, u)


def ffn_up_wgrad(dg, du, h):
    L, F = dg.shape
    D = h.shape[1]
    tm = _tile(F, 256)
    nblk = F // tm

    def half(d, off, prev):
        def body(d_ref, h_ref, *rest):
            rest[-1][...] = _dot_tn(d_ref[...], h_ref[...]).astype(BF)

        in_specs = [pl.BlockSpec((L, tm), lambda i: (0, i)), pl.BlockSpec((L, D), lambda i: (0, 0))]
        args = [d, h]
        aliases = {}
        if prev is not None:
            in_specs.append(pl.BlockSpec(memory_space=pl.ANY))
            args.append(prev)
            aliases = {2: 0}
        return pl.pallas_call(
            body, name="ffn_up_wgrad", grid=(nblk,), in_specs=in_specs,
            out_specs=pl.BlockSpec((tm, D), lambda i: (i + off * nblk, 0)),
            out_shape=_sds((2 * F, D), BF), input_output_aliases=aliases,
            compiler_params=pltpu.CompilerParams(vmem_limit_bytes=VMEM_LIMIT))(*args)

    return half(du, 1, half(dg, 0, None))


def ffn_up_dgrad(dg, du, wt):
    L, F = dg.shape
    D = wt.shape[1]
    tm, tn = _tile(L, 1024), _tile(D, 512)

    def body(dg_ref, du_ref, wg_ref, wu_ref, o_ref):
        o_ref[...] = _dot_nn(dg_ref[...], wg_ref[...]) + _dot_nn(du_ref[...], wu_ref[...])

    a = pl.BlockSpec((tm, F), lambda i, n: (i, 0))
    return _call(body, "ffn_up_dgrad", (L // tm, D // tn),
                 [a, a, pl.BlockSpec((F, tn), lambda i, n: (0, n)), pl.BlockSpec((F, tn), lambda i, n: (1, n))],
                 pl.BlockSpec((tm, tn), lambda i, n: (i, n)), _sds((L, D), F32))(dg, du, wt, wt)


_HALO = 8


def _shift_down(cur, prev8, k):
    out = pltpu.roll(cur, k, 0)
    rows8 = lax.broadcasted_iota(jnp.int32, prev8.shape, 0)
    head = jnp.where(rows8 < k, pltpu.roll(prev8, k, 0), out[0:_HALO])
    if cur.shape[0] == _HALO:
        return head
    return jnp.concatenate([head, out[_HALO:]], axis=0)


def _shift_up(cur, next8, k):
    n = cur.shape[0]
    out = pltpu.roll(cur, n - k, 0)
    rows8 = lax.broadcasted_iota(jnp.int32, next8.shape, 0)
    tail = jnp.where(rows8 >= _HALO - k, pltpu.roll(next8, _HALO - k, 0), out[n - _HALO:])
    return jnp.concatenate([out[:n - _HALO], tail], axis=0)


def _conv_pre(cur, prev8, w_ref, b_ref):
    shifted = [_shift_down(cur, prev8, k) for k in range(1, CONV_WIDTH)]
    s = cur * w_ref[CONV_WIDTH - 1:CONV_WIDTH, :] + b_ref[...]
    for k in range(1, CONV_WIDTH):
        s = s + shifted[k - 1] * w_ref[CONV_WIDTH - 1 - k:CONV_WIDTH - k, :]
    return s, shifted


def _silu_grad(s):
    sg = _sigmoid(s)
    return sg * (1.0 + s * (1.0 - sg))


_XBC_COL0 = D_INNER // 512


def conv_fwd(zx, w, b):
    L = zx.shape[0]
    tm, tc = _tile(L, 256), 512
    hb = tm // _HALO

    def body(cur_ref, prev_ref, w_ref, b_ref, o_ref):
        prev8 = jnp.where(pl.program_id(1) > 0, prev_ref[...], 0.0)
        s, _ = _conv_pre(cur_ref[...], prev8, w_ref, b_ref)
        o_ref[...] = s * _sigmoid(s)

    return _call(body, "conv_fwd", (CONV_DIM // tc, L // tm),
                 [pl.BlockSpec((tm, tc), lambda j, i: (i, _XBC_COL0 + j)),
                  pl.BlockSpec((_HALO, tc), lambda j, i: (jnp.maximum(i * hb - 1, 0), _XBC_COL0 + j)),
                  pl.BlockSpec((CONV_WIDTH, tc), lambda j, i: (0, j)),
                  pl.BlockSpec((1, tc), lambda j, i: (0, j))],
                 pl.BlockSpec((tm, tc), lambda j, i: (i, j)), _sds((L, CONV_DIM), F32))(zx, zx, w, b)


def conv_bwd(dxc, zx, w, b):
    L = zx.shape[0]
    tm, tc = _tile(L, 256), 512
    nblk = L // tm
    hb = tm // _HALO

    def body(d_ref, dn_ref, cur_ref, prev_ref, next_ref, w_ref, b_ref, du_ref, acc_ref):
        i = pl.program_id(1)

        @pl.when(i == 0)
        def _():
            acc_ref[...] = jnp.zeros_like(acc_ref)

        cur = cur_ref[...]
        prev8 = jnp.where(i > 0, prev_ref[...], 0.0)
        s, shifted = _conv_pre(cur, prev8, w_ref, b_ref)
        ds_c = d_ref[...] * _silu_grad(s)
        s_n, _ = _conv_pre(next_ref[...], cur[tm - _HALO:], w_ref, b_ref)
        ds_n = jnp.where(i < nblk - 1, dn_ref[...] * _silu_grad(s_n), 0.0)
        du = ds_c * w_ref[CONV_WIDTH - 1:CONV_WIDTH, :]
        acc_ref[CONV_WIDTH - 1:CONV_WIDTH, :] += jnp.sum(ds_c * cur, axis=0, keepdims=True)
        for k in range(1, CONV_WIDTH):
            du = du + _shift_up(ds_c, ds_n, k) * w_ref[CONV_WIDTH - 1 - k:CONV_WIDTH - k, :]
            acc_ref[CONV_WIDTH - 1 - k:CONV_WIDTH - k, :] += jnp.sum(ds_c * shifted[k - 1], axis=0, keepdims=True)
        acc_ref[CONV_WIDTH:CONV_WIDTH + 1, :] += jnp.sum(ds_c, axis=0, keepdims=True)
        du_ref[...] = du.astype(BF)

    nxt = lambda j, i: (jnp.minimum((i + 1) * hb, L // _HALO - 1), j)
    return _call(body, "conv_bwd", (CONV_DIM // tc, nblk),
                 [pl.BlockSpec((tm, tc), lambda j, i: (i, j)),
                  pl.BlockSpec((_HALO, tc), nxt),
                  pl.BlockSpec((tm, tc), lambda j, i: (i, _XBC_COL0 + j)),
                  pl.BlockSpec((_HALO, tc), lambda j, i: (jnp.maximum(i * hb - 1, 0), _XBC_COL0 + j)),
                  pl.BlockSpec((_HALO, tc), lambda j, i: (jnp.minimum((i + 1) * hb, L // _HALO - 1), _XBC_COL0 + j)),
                  pl.BlockSpec((CONV_WIDTH, tc), lambda j, i: (0, j)),
                  pl.BlockSpec((1, tc), lambda j, i: (0, j))],
                 (pl.BlockSpec((tm, tc), lambda j, i: (i, j)), pl.BlockSpec((8, tc), lambda j, i: (0, j))),
                 (_sds((L, CONV_DIM), BF), _sds((8, CONV_DIM), F32)))(dxc, dxc, zx, zx, zx, w, b)


_DT_COL = (D_INNER + CONV_DIM) // 128


def dt_prep(zx, bias_pad, alog_pad):
    L = zx.shape[0]

    def body(raw_ref, b_ref, al_ref, dt_ref, acs_ref):
        v = raw_ref[...] + b_ref[...]
        dt = jnp.maximum(v, 0.0) + jnp.log(1.0 + jnp.exp(-jnp.abs(v)))
        dt_ref[...] = dt
        acs = dt * (-jnp.exp(al_ref[...]))
        rows = lax.broadcasted_iota(jnp.int32, acs.shape, 0)
        s = 1
        while s < CHUNK:
            acs = acs + jnp.where(rows >= s, pltpu.roll(acs, s, 0), 0.0)
            s *= 2
        acs_ref[...] = acs

    blk = pl.BlockSpec((CHUNK, 128), lambda i: (i, 0))
    vec = pl.BlockSpec((1, 128), lambda i: (0, 0))
    return _call(body, "dt_prep", (L // CHUNK,),
                 [pl.BlockSpec((CHUNK, 128), lambda i: (i, _DT_COL)), vec, vec], (blk, blk),
                 (_sds((L, 128), F32), _sds((L, 128), F32)))(zx, bias_pad, alog_pad)


def dt_bwd(ddt, da, dt, zx, bias_pad, alog_pad):
    L = zx.shape[0]
    tm = _tile(L, 512)

    def body(ddt_ref, da_ref, dt_ref, raw_ref, b_ref, al_ref, o_ref, acc_ref):
        @pl.when(pl.program_id(0) == 0)
        def _():
            acc_ref[...] = jnp.zeros_like(acc_ref)

        A = -jnp.exp(al_ref[...])
        dav = da_ref[...]
        dd = ddt_ref[...] + dav * A
        draw = dd * _sigmoid(raw_ref[...] + b_ref[...])
        o_ref[...] = draw.astype(BF)
        acc_ref[0:1, :] += jnp.sum(draw, axis=0, keepdims=True)
        acc_ref[1:2, :] += jnp.sum(dav * dt_ref[...], axis=0, keepdims=True) * A

    blk = pl.BlockSpec((tm, 128), lambda i: (i, 0))
    vec = pl.BlockSpec((1, 128), lambda i: (0, 0))
    return _call(body, "dt_bwd", (L // tm,),
                 [blk, blk, blk, pl.BlockSpec((tm, 128), lambda i: (i, _DT_COL)), vec, vec],
                 (blk, pl.BlockSpec((8, 128), lambda i: (0, 0))),
                 (_sds((L, 128), BF), _sds((8, 128), F32)))(ddt, da, dt, zx, bias_pad, alog_pad)


_HPG = SSM_HEADS // SSM_GROUPS
_GW = _HPG * SSM_HEADDIM
_B_COL0 = D_INNER // SSM_STATE
_C_COL0 = (D_INNER + SSM_GROUPS * SSM_STATE) // SSM_STATE


def _ssd_head(x, dtc, ac, ar, r, causal):
    xh = x[:, SSM_HEADDIM * r:SSM_HEADDIM * (r + 1)]
    acol = ac[:, r:r + 1]
    arow = ar[r:r + 1, :]
    alast = ar[r:r + 1, CHUNK - 1:CHUNK]
    lm = jnp.exp(jnp.where(causal, acol - arow, NEG))
    return xh, xh * dtc[:, r:r + 1], acol, alast, lm


def ssd_fwd(xc, dt_g, acs_g, acsT_g, d_exp):
    L = xc.shape[0]
    nc = L // CHUNK

    def body(x_ref, b_ref, c_ref, dt_ref, ac_ref, ar_ref, d_ref, y_ref, pst_ref, st_ref):
        @pl.when(pl.program_id(1) == 0)
        def _():
            st_ref[...] = jnp.zeros_like(st_ref)

        x, Bm, Cm = x_ref[...], b_ref[...], c_ref[...]
        dtc, ac, ar = dt_ref[...], ac_ref[...], ar_ref[...]
        causal = lax.broadcasted_iota(jnp.int32, (CHUNK, CHUNK), 0) >= lax.broadcasted_iota(jnp.int32, (CHUNK, CHUNK), 1)
        CB = _dot_nt(Cm, Bm)
        for r in range(_HPG):
            xh, xd, acol, alast, lm = _ssd_head(x, dtc, ac, ar, r, causal)
            P = st_ref[r]
            y = _dot_nn(CB * lm, xd) + jnp.exp(acol) * _dot_nt(Cm, P)
            y_ref[:, SSM_HEADDIM * r:SSM_HEADDIM * (r + 1)] = y + d_ref[:, SSM_HEADDIM * r:SSM_HEADDIM * (r + 1)] * xh
            pst_ref[r] = P
            st_ref[r] = P * jnp.exp(alast) + _dot_tn(xd * jnp.exp(alast - acol), Bm)

    return _call(
        body, "ssd_fwd", (SSM_GROUPS, nc),
        [pl.BlockSpec((CHUNK, _GW), lambda g, c: (c, g)),
         pl.BlockSpec((CHUNK, SSM_STATE), lambda g, c: (c, _B_COL0 + g)),
         pl.BlockSpec((CHUNK, SSM_STATE), lambda g, c: (c, _C_COL0 + g)),
         pl.BlockSpec((None, CHUNK, _HPG), lambda g, c: (g, c, 0)),
         pl.BlockSpec((None, CHUNK, _HPG), lambda g, c: (g, c, 0)),
         pl.BlockSpec((None, _HPG, CHUNK), lambda g, c: (g, 0, c)),
         pl.BlockSpec((None, 1, _GW), lambda g, c: (g, 0, 0))],
        (pl.BlockSpec((CHUNK, _GW), lambda g, c: (c, g)),
         pl.BlockSpec((None, None, _HPG, SSM_HEADDIM, SSM_STATE), lambda g, c: (c, g, 0, 0, 0))),
        (_sds((L, D_INNER), F32), _sds((nc, SSM_GROUPS, _HPG, SSM_HEADDIM, SSM_STATE), F32)),
        scratch=[pltpu.VMEM((_HPG, SSM_HEADDIM, SSM_STATE), F32)],
    )(xc, xc, xc, dt_g, acs_g, acsT_g, d_exp)


def ssd_bwd(dy, xc, dt_g, acs_g, acsT_g, pst, d_exp):
    L = xc.shape[0]
    nc = L // CHUNK

    def body(dy_ref, x_ref, b_ref, c_ref, dt_ref, ac_ref, ar_ref, pst_ref, d_ref,
             dx_ref, db_ref, dc_ref, ddt_ref, da_ref, dd_ref, dp_ref):
        @pl.when(pl.program_id(1) == 0)
        def _():
            dp_ref[...] = jnp.zeros_like(dp_ref)
            dd_ref[...] = jnp.zeros_like(dd_ref)

        dyv, x, Bm, Cm = dy_ref[...], x_ref[...], b_ref[...], c_ref[...]
        dtc, ac, ar = dt_ref[...], ac_ref[...], ar_ref[...]
        ri = lax.broadcasted_iota(jnp.int32, (CHUNK, CHUNK), 0)
        ci = lax.broadcasted_iota(jnp.int32, (CHUNK, CHUNK), 1)
        causal = ri >= ci
        lane4 = lax.broadcasted_iota(jnp.int32, (CHUNK, _HPG), 1)
        CB = _dot_nt(Cm, Bm)
        dB = jnp.zeros((CHUNK, SSM_STATE), F32)
        dC = jnp.zeros((CHUNK, SSM_STATE), F32)
        dCB = jnp.zeros((CHUNK, CHUNK), F32)
        ddt_blk = jnp.zeros((CHUNK, _HPG), F32)
        da_blk = jnp.zeros((CHUNK, _HPG), F32)
        for r in range(_HPG):
            sl = slice(SSM_HEADDIM * r, SSM_HEADDIM * (r + 1))
            xh, xd, acol, alast, lm = _ssd_head(x, dtc, ac, ar, r, causal)
            dyh = dyv[:, sl]
            P = pst_ref[r]
            dPn = dp_ref[r]
            eA = jnp.exp(acol)
            cd = jnp.exp(alast)
            dte = jnp.exp(alast - acol)
            G = CB * lm
            dZ = eA * dyh
            dzp = _dot_nn(dZ, P)
            dC = dC + dzp
            dp_ref[r] = dPn * cd + _dot_tn(dZ, Cm)
            dA_col = jnp.sum(dzp * Cm, axis=1, keepdims=True)
            BdS = _dot_nt(Bm, dPn)
            dxd = dte * BdS
            dB = dB + dte * _dot_nn(xd, dPn)
            t = jnp.sum(xd * BdS, axis=1, keepdims=True) * dte
            dA_col = dA_col - t
            dA_last = jnp.sum(t, axis=0, keepdims=True) + jnp.sum(
                jnp.sum(dPn * P, axis=1, keepdims=True), axis=0, keepdims=True) * cd
            dG = _dot_nt(dyh, xd)
            dxd = dxd + _dot_tn(G, dyh)
            dCB = dCB + dG * lm
            W = dG * G
            dA_col = dA_col + jnp.sum(W, axis=1, keepdims=True)
            dA_row = jnp.sum(jnp.where(ri == ci, dA_col, 0.0), axis=0, keepdims=True) - jnp.sum(W, axis=0, keepdims=True)
            da_col = jnp.sum(jnp.where(ci >= ri, dA_row, 0.0), axis=1, keepdims=True) + dA_last
            da_blk = jnp.where(lane4 == r, da_col, da_blk)
            ddt_blk = jnp.where(lane4 == r, jnp.sum(dxd * xh, axis=1, keepdims=True), ddt_blk)
            dx_ref[:, sl] = dxd * dtc[:, r:r + 1] + d_ref[:, sl] * dyh
        dc_ref[...] = dC + _dot_nn(dCB, Bm)
        db_ref[...] = dB + _dot_tn(dCB, Cm)
        ddt_ref[...] = ddt_blk
        da_ref[...] = da_blk
        dd_ref[...] += jnp.sum(dyv * x, axis=0, keepdims=True)

    rc = lambda g, c: (nc - 1 - c, g)
    small = pl.BlockSpec((None, CHUNK, _HPG), lambda g, c: (g, nc - 1 - c, 0))
    return _call(
        body, "ssd_bwd", (SSM_GROUPS, nc),
        [pl.BlockSpec((CHUNK, _GW), rc),
         pl.BlockSpec((CHUNK, _GW), rc),
         pl.BlockSpec((CHUNK, SSM_STATE), lambda g, c: (nc - 1 - c, _B_COL0 + g)),
         pl.BlockSpec((CHUNK, SSM_STATE), lambda g, c: (nc - 1 - c, _C_COL0 + g)),
         small, small,
         pl.BlockSpec((None, _HPG, CHUNK), lambda g, c: (g, 0, nc - 1 - c)),
         pl.BlockSpec((None, None, _HPG, SSM_HEADDIM, SSM_STATE), lambda g, c: (nc - 1 - c, g, 0, 0, 0)),
         pl.BlockSpec((None, 1, _GW), lambda g, c: (g, 0, 0))],
        (pl.BlockSpec((CHUNK, _GW), rc),
         pl.BlockSpec((CHUNK, SSM_STATE), rc),
         pl.BlockSpec((CHUNK, SSM_STATE), rc),
         small, small,
         pl.BlockSpec((None, 1, _GW), lambda g, c: (g, 0, 0))),
        (_sds((L, D_INNER), F32), _sds((L, SSM_GROUPS * SSM_STATE), F32), _sds((L, SSM_GROUPS * SSM_STATE), F32),
         _sds((SSM_GROUPS, L, _HPG), F32), _sds((SSM_GROUPS, L, _HPG), F32), _sds((SSM_GROUPS, 1, _GW), F32)),
        scratch=[pltpu.VMEM((_HPG, SSM_HEADDIM, SSM_STATE), F32)],
    )(dy, xc, xc, xc, dt_g, acs_g, acsT_g, pst, d_exp)


_NGW = D_INNER // SSM_GROUPS


def gate_norm(y, zx, nw):
    L = y.shape[0]
    tm = _tile(L, 256)

    def body(y_ref, z_ref, nw_ref, o_ref):
        for g in range(SSM_GROUPS):
            sl = slice(_NGW * g, _NGW * (g + 1))
            z = z_ref[:, sl]
            y2 = y_ref[:, sl] * (z * _sigmoid(z))
            r = lax.rsqrt(jnp.mean(y2 * y2, axis=-1, keepdims=True) + EPS)
            o_ref[:, sl] = (y2 * r * nw_ref[:, sl]).astype(BF)

    row = pl.BlockSpec((tm, D_INNER), lambda i: (i, 0))
    return _call(body, "gate_norm", (L // tm,), [row, row, pl.BlockSpec((1, D_INNER), lambda i: (0, 0))],
                 row, _sds((L, D_INNER), BF))(y, zx, nw)


def gate_norm_bwd(dyn, y, zx, nw):
    L = y.shape[0]
    tm = _tile(L, 256)

    def body(d_ref, y_ref, z_ref, nw_ref, dy_ref, dz_ref, acc_ref):
        @pl.when(pl.program_id(0) == 0)
        def _():
            acc_ref[...] = jnp.zeros_like(acc_ref)

        for g in range(SSM_GROUPS):
            sl = slice(_NGW * g, _NGW * (g + 1))
            z = z_ref[:, sl]
            yv = y_ref[:, sl]
            sg = _sigmoid(z)
            sz = z * sg
            y2 = yv * sz
            r = lax.rsqrt(jnp.mean(y2 * y2, axis=-1, keepdims=True) + EPS)
            yh = y2 * r
            d = d_ref[:, sl]
            dn = d * nw_ref[:, sl]
            dy2 = r * (dn - yh * jnp.mean(dn * yh, axis=-1, keepdims=True))
            dy_ref[:, sl] = dy2 * sz
            dz_ref[:, sl] = (dy2 * yv * sg * (1.0 + z * (1.0 - sg))).astype(BF)
            acc_ref[0:1, sl] += jnp.sum(d * yh, axis=0, keepdims=True)

    row = pl.BlockSpec((tm, D_INNER), lambda i: (i, 0))
    return _call(body, "gate_norm_bwd", (L // tm,), [row, row, row, pl.BlockSpec((1, D_INNER), lambda i: (0, 0))],
                 (row, row, pl.BlockSpec((8, D_INNER), lambda i: (0, 0))),
                 (_sds((L, D_INNER), F32), _sds((L, D_INNER), BF), _sds((8, D_INNER), F32)))(dyn, y, zx, nw)


_SCALE = HEAD_DIM ** -0.5
_REP = ATT_HEADS // KV_HEADS
_V_OFF = KV_HEADS * HEAD_DIM


def _stack_heads(ref, k):
    return jnp.concatenate([ref[:, HEAD_DIM * (k * _REP + r):HEAD_DIM * (k * _REP + r + 1)] for r in range(_REP)],
                           axis=0)


def _stack_sinks(s_ref, k):
    return jnp.concatenate([jnp.broadcast_to(s_ref[:, k * _REP + r:k * _REP + r + 1], (WINDOW, 1))
                            for r in range(_REP)], axis=0)


def _attn_probs(q4, kp, kc, sink, first):
    shape = (_REP * WINDOW, WINDOW)
    rows = jnp.bitwise_and(lax.broadcasted_iota(jnp.int32, shape, 0), WINDOW - 1)
    cols = lax.broadcasted_iota(jnp.int32, shape, 1)
    sp = jnp.where(jnp.logical_and(cols > rows, jnp.logical_not(first)), _dot_nt(q4, kp) * _SCALE, NEG)
    sc = jnp.where(cols <= rows, _dot_nt(q4, kc) * _SCALE, NEG)
    m = jnp.maximum(jnp.maximum(jnp.max(sp, axis=1, keepdims=True), jnp.max(sc, axis=1, keepdims=True)), sink)
    pp = jnp.exp(sp - m)
    pc = jnp.exp(sc - m)
    ps = jnp.exp(sink - m)
    inv = 1.0 / (jnp.sum(pp, axis=1, keepdims=True) + jnp.sum(pc, axis=1, keepdims=True) + ps)
    return pp * inv, pc * inv, ps * inv


def attn_fwd(q, kv, sinks_pad):
    L = q.shape[0]
    nb = L // WINDOW

    def body(q_ref, kc_ref, kp_ref, s_ref, o_ref):
        first = pl.program_id(0) == 0
        for k in range(KV_HEADS):
            ks = slice(HEAD_DIM * k, HEAD_DIM * (k + 1))
            vs = slice(_V_OFF + HEAD_DIM * k, _V_OFF + HEAD_DIM * (k + 1))
            pp, pc, _ = _attn_probs(_stack_heads(q_ref, k), kp_ref[:, ks], kc_ref[:, ks], _stack_sinks(s_ref, k), first)
            o4 = _dot_nn(pp, kp_ref[:, vs]) + _dot_nn(pc, kc_ref[:, vs])
            for r in range(_REP):
                h = k * _REP + r
                o_ref[:, HEAD_DIM * h:HEAD_DIM * (h + 1)] = o4[WINDOW * r:WINDOW * (r + 1)].astype(BF)

    qspec = pl.BlockSpec((WINDOW, D_MODEL), lambda i: (i, 0))
    return _call(body, "attn_fwd", (nb,),
                 [qspec, pl.BlockSpec((WINDOW, KV_DIM), lambda i: (i, 0)),
                  pl.BlockSpec((WINDOW, KV_DIM), lambda i: (jnp.maximum(i - 1, 0), 0)),
                  pl.BlockSpec((1, 128), lambda i: (0, 0))],
                 qspec, _sds((L, D_MODEL), BF))(q, kv, kv, sinks_pad)


def attn_bwd(q, kv, do, sinks_pad):
    L = q.shape[0]
    nb = L // WINDOW

    def body(q_ref, kc_ref, kp_ref, do_ref, s_ref, dq_ref, dc_ref, dp_ref, acc_ref):
        first = pl.program_id(0) == 0

        @pl.when(first)
        def _():
            acc_ref[...] = jnp.zeros_like(acc_ref)

        lane = lax.broadcasted_iota(jnp.int32, (1, 128), 1)
        dsink = jnp.zeros((1, 128), F32)
        for k in range(KV_HEADS):
            ks = slice(HEAD_DIM * k, HEAD_DIM * (k + 1))
            vs = slice(_V_OFF + HEAD_DIM * k, _V_OFF + HEAD_DIM * (k + 1))
            kp, kc, vp, vc = kp_ref[:, ks], kc_ref[:, ks], kp_ref[:, vs], kc_ref[:, vs]
            q4 = _stack_heads(q_ref, k)
            do4 = _stack_heads(do_ref, k)
            pp, pc, ps = _attn_probs(q4, kp, kc, _stack_sinks(s_ref, k), first)
            dpp = _dot_nt(do4, vp)
            dpc = _dot_nt(do4, vc)
            delta = jnp.sum(pp * dpp, axis=1, keepdims=True) + jnp.sum(pc * dpc, axis=1, keepdims=True)
            dsp = pp * (dpp - delta) * _SCALE
            dsc = pc * (dpc - delta) * _SCALE
            dq4 = _dot_nn(dsp, kp) + _dot_nn(dsc, kc)
            psd = ps * delta
            for r in range(_REP):
                h = k * _REP + r
                rs = slice(WINDOW * r, WINDOW * (r + 1))
                dq_ref[:, HEAD_DIM * h:HEAD_DIM * (h + 1)] = dq4[rs]
                dsink = dsink + jnp.where(lane == h, -jnp.sum(psd[rs], axis=0, keepdims=True), 0.0)
            dp_ref[:, ks] = _dot_tn(dsp, q4)
            dc_ref[:, ks] = _dot_tn(dsc, q4)
            dp_ref[:, vs] = _dot_tn(pp, do4)
            dc_ref[:, vs] = _dot_tn(pc, do4)
        acc_ref[0:1, :] += jnp.sum(dq_ref[...], axis=0, keepdims=True)
        acc_ref[1:2, 0:128] += dsink

    qspec = pl.BlockSpec((WINDOW, D_MODEL), lambda i: (i, 0))
    kspec = pl.BlockSpec((WINDOW, KV_DIM), lambda i: (i, 0))
    return _call(body, "attn_bwd", (nb,),
                 [qspec, kspec, pl.BlockSpec((WINDOW, KV_DIM), lambda i: (jnp.maximum(i - 1, 0), 0)), qspec,
                  pl.BlockSpec((1, 128), lambda i: (0, 0))],
                 (qspec, kspec, kspec, pl.BlockSpec((8, D_MODEL), lambda i: (0, 0))),
                 (_sds((L, D_MODEL), F32), _sds((L, KV_DIM), F32), _sds((L, KV_DIM), F32), _sds((8, D_MODEL), F32)),
                 )(q, kv, kv, do, sinks_pad)


def kv_grad_combine(parts):
    L = parts[0][0].shape[0]
    nb = L // WINDOW
    n = len(parts)

    def body(*refs):
        i = pl.program_id(0)
        o_ref, acc_ref = refs[2 * n], refs[2 * n + 1]

        @pl.when(i == 0)
        def _():
            acc_ref[...] = jnp.zeros_like(acc_ref)

        tot = refs[0][...]
        nxt = refs[1][...]
        for a in range(1, n):
            tot = tot + refs[2 * a][...]
            nxt = nxt + refs[2 * a + 1][...]
        tot = tot + jnp.where(i < nb - 1, nxt, 0.0)
        o_ref[...] = tot
        acc_ref[0:1, :] += jnp.sum(tot, axis=0, keepdims=True)

    cur = pl.BlockSpec((WINDOW, KV_DIM), lambda i: (i, 0))
    nxt = pl.BlockSpec((WINDOW, KV_DIM), lambda i: (jnp.minimum(i + 1, nb - 1), 0))
    args = [t for p in parts for t in p]
    return _call(body, "kv_grad_combine", (nb,), [cur, nxt] * n,
                 (cur, pl.BlockSpec((8, KV_DIM), lambda i: (0, 0))),
                 (_sds((L, KV_DIM), F32), _sds((8, KV_DIM), F32)))(*args)


def mod_fwd(c_all, w, b, name):
    n, _, C = w.shape

    def body(c_ref, w_ref, b_ref, o_ref, ca_ref):
        cv = c_ref[...]
        ca = cv * _sigmoid(cv)
        ca_ref[...] = ca
        o_ref[...] = _dot(ca, w_ref[...], ((1,), (0,))) + b_ref[...]

    return _call(body, name, (n,),
                 [pl.BlockSpec((N_DEV, D_MODEL), lambda i: (0, 0)),
                  pl.BlockSpec((None, D_MODEL, C), lambda i: (i, 0, 0)),
                  pl.BlockSpec((None, 1, C), lambda i: (i, 0, 0))],
                 (pl.BlockSpec((None, N_DEV, C), lambda i: (i, 0, 0)), pl.BlockSpec((N_DEV, D_MODEL), lambda i: (0, 0))),
                 (_sds((n, N_DEV, C), F32), _sds((N_DEV, D_MODEL), F32)))(c_all, w, b)


def mod_wgrad(c_act_t, dmod, name):
    n, _, C = dmod.shape
    tr = 256

    def body(ct_ref, d_ref, o_ref):
        acc = ct_ref[:, 0:1] * d_ref[0:1, :]
        for bidx in range(1, N_DEV):
            acc = acc + ct_ref[:, bidx:bidx + 1] * d_ref[bidx:bidx + 1, :]
        o_ref[...] = acc

    return _call(body, name, (n, D_MODEL // tr),
                 [pl.BlockSpec((tr, N_DEV), lambda i, j: (j, 0)),
                  pl.BlockSpec((None, N_DEV, C), lambda i, j: (i, 0, 0))],
                 pl.BlockSpec((None, tr, C), lambda i, j: (i, j, 0)), _sds((n, D_MODEL, C), F32))(c_act_t, dmod)


def _my_pos():
    return lax.axis_index("x"), lax.axis_index("y"), lax.axis_index("c")


def small_all_gather(v):
    m_per, n = v.shape

    def body(x_ref, out_ref, send_sems, recv_sems, local_sem):
        x, y, c = _my_pos()
        me, sibling = (x, y, c), (x, y, 1 - c)
        chips = [(1 - x, y), (x, 1 - y), (1 - x, 1 - y)]

        def rows(px, py, pc):
            return out_ref.at[pl.ds((4 * px + 2 * py + pc) * m_per, m_per), :]

        def copy(k, block, to, src=None):
            return pltpu.make_async_remote_copy(
                src_ref=rows(*block) if src is None else src, dst_ref=rows(*block),
                send_sem=send_sems.at[k], recv_sem=recv_sems.at[k], device_id=to, device_id_type=MESH)

        mine = pltpu.make_async_copy(x_ref, rows(*me), local_sem)
        mine.start()
        first = [copy(0, me, sibling, src=x_ref)]
        first += [copy(1 + j, me, (*chip, c), src=x_ref) for j, chip in enumerate(chips)]
        for cp in first:
            cp.start()
        passed = [copy(4 + j, (*chip, c), sibling) for j, chip in enumerate(chips)]
        for j, chip in enumerate(chips):
            copy(1 + j, (*chip, c), me).wait_recv()
            passed[j].start()
        copy(0, sibling, me).wait_recv()
        for j, chip in enumerate(chips):
            copy(4 + j, (*chip, 1 - c), me).wait_recv()
        for cp in first + passed:
            cp.wait_send()
        mine.wait()

    return pl.pallas_call(
        body, name="small_all_gather",
        out_shape=_sds((N_DEV * m_per, n), v.dtype),
        in_specs=[pl.BlockSpec(memory_space=pltpu.VMEM)],
        out_specs=pl.BlockSpec(memory_space=pltpu.VMEM),
        scratch_shapes=[pltpu.SemaphoreType.DMA((7,)), pltpu.SemaphoreType.DMA((7,)), pltpu.SemaphoreType.DMA],
        compiler_params=pltpu.CompilerParams(vmem_limit_bytes=VMEM_LIMIT),
    )(v)


def big_all_gather(arrs):
    n = len(arrs)

    def body(*refs):
        ins, outs = refs[:n], refs[n:2 * n]
        send_sems, recv_sems, local_sems = refs[2 * n], refs[2 * n + 1], refs[2 * n + 2]
        x, y, c = _my_pos()
        me, sibling = (x, y, c), (x, y, 1 - c)
        chips = [(1 - x, y), (x, 1 - y), (1 - x, 1 - y)]

        def slot(a, px, py, pc):
            return outs[a].at[4 * px + 2 * py + pc]

        def copy(a, k, block, to, src=None):
            return pltpu.make_async_remote_copy(
                src_ref=slot(a, *block) if src is None else src, dst_ref=slot(a, *block),
                send_sem=send_sems.at[7 * a + k], recv_sem=recv_sems.at[7 * a + k], device_id=to, device_id_type=MESH)

        mine = [pltpu.make_async_copy(ins[a], slot(a, *me), local_sems.at[a]) for a in range(n)]
        for cp in mine:
            cp.start()
        first = []
        for a in range(n):
            first.append(copy(a, 0, me, sibling, src=ins[a]))
            first += [copy(a, 1 + j, me, (*chip, c), src=ins[a]) for j, chip in enumerate(chips)]
        for cp in first:
            cp.start()
        passed = []
        for a in range(n):
            for j, chip in enumerate(chips):
                copy(a, 1 + j, (*chip, c), me).wait_recv()
                fwd = copy(a, 4 + j, (*chip, c), sibling)
                fwd.start()
                passed.append(fwd)
        for a in range(n):
            copy(a, 0, sibling, me).wait_recv()
            for j, chip in enumerate(chips):
                copy(a, 4 + j, (*chip, 1 - c), me).wait_recv()
        for cp in first + passed:
            cp.wait_send()
        for cp in mine:
            cp.wait()

    hbm = pl.BlockSpec(memory_space=pltpu.HBM)
    return pl.pallas_call(
        body, name="big_all_gather",
        out_shape=[_sds((N_DEV,) + a.shape, a.dtype) for a in arrs],
        in_specs=[hbm] * n, out_specs=[hbm] * n,
        scratch_shapes=[pltpu.SemaphoreType.DMA((7 * n,)), pltpu.SemaphoreType.DMA((7 * n,)),
                        pltpu.SemaphoreType.DMA((n,))],
    )(*arrs)


_FLIPS =[(fx, fy, fc) for fx in (0, 1) for fy in (0, 1) for fc in (0, 1)][1:]
_HBM = pl.BlockSpec(memory_space=pltpu.HBM)
_SEM = pl.BlockSpec(memory_space=pltpu.SEMAPHORE)
_EFFECT = pltpu.SideEffectType.DATAFLOW_SIDE_EFFECTING


def _flip(x, y, c, f):
    return (1 - x if f[0] else x), (1 - y if f[1] else y), (1 - c if f[2] else c)


def _xfer_copies(srcs, lands, send_sems, recv_sems, scatter):
    x, y, c = _my_pos()
    me = 4 * x + 2 * y + c
    copies = []
    for a in range(len(srcs)):
        for k, f in enumerate(_FLIPS):
            px, py, pc = _flip(x, y, c, f)
            src = srcs[a].at[4 * px + 2 * py + pc] if scatter else srcs[a]
            copies.append(pltpu.make_async_remote_copy(
                src_ref=src, dst_ref=lands[a].at[me], send_sem=send_sems.at[7 * a + k],
                recv_sem=recv_sems.at[7 * a + k], device_id=(px, py, pc), device_id_type=MESH))
    return copies


def _own_copies(srcs, lands, local_sems, scatter):
    x, y, c = _my_pos()
    me = 4 * x + 2 * y + c
    return [pltpu.make_async_copy(srcs[a].at[me] if scatter else srcs[a], lands[a].at[me], local_sems.at[a])
            for a in range(len(srcs))]


def xfer_start(arrs, scatter, after, name):
    n = len(arrs)
    land_shapes = [a.shape if scatter else (N_DEV,) + a.shape for a in arrs]

    def body(*refs):
        srcs, lands = refs[:n], refs[n:2 * n]
        send_sems, recv_sems, local_sems = refs[2 * n + 1], refs[2 * n + 2], refs[2 * n + 3]
        token = refs[-1]
        for cp in _xfer_copies(srcs, lands, send_sems, recv_sems, scatter):
            cp.start()
        for cp in _own_copies(srcs, lands, local_sems, scatter):
            cp.start()
        token[...] = jnp.zeros_like(token)

    out = pl.pallas_call(
        body, name=name,
        out_shape=(pltpu.SemaphoreType.DMA((7 * n,)), pltpu.SemaphoreType.DMA((7 * n,)),
                   pltpu.SemaphoreType.DMA((n,)),
                   *[pltpu.HBM(a.shape, a.dtype) for a in arrs],
                   *[pltpu.HBM(s, a.dtype) for s, a in zip(land_shapes, arrs)],
                   _sds((8, 128), F32)),
        in_specs=[_HBM] * (2 * n) + [pl.BlockSpec(memory_space=pl.ANY)],
        out_specs=(_SEM, _SEM, _SEM, *([_HBM] * (2 * n)), pl.BlockSpec(memory_space=pltpu.VMEM)),
        input_output_aliases={i: 3 + i for i in range(2 * n)},
        compiler_params=pltpu.CompilerParams(has_side_effects=_EFFECT),
    )(*[pltpu.with_memory_space_constraint(a, pltpu.HBM) for a in arrs],
      *[pltpu.with_memory_space_constraint(lax.empty(s, a.dtype), pltpu.HBM) for s, a in zip(land_shapes, arrs)],
      after)
    return (out[0], out[1], out[2], list(out[3:3 + n]), list(out[3 + n:3 + 2 * n]), scatter), out[-1]


def xfer_wait(handle, after, name):
    send_sems, recv_sems, local_sems, srcs, lands, scatter = handle
    n = len(srcs)

    def body(*refs):
        srcs_r, lands_r = refs[:n], refs[n:2 * n]
        ssem, rsem, lsem = refs[2 * n], refs[2 * n + 1], refs[2 * n + 2]
        for cp in _xfer_copies(srcs_r, lands_r, ssem, rsem, scatter):
            cp.wait_send()
            cp.wait_recv()
        for cp in _own_copies(srcs_r, lands_r, lsem, scatter):
            cp.wait()

    out = pl.pallas_call(
        body, name=name,
        out_shape=(*[pltpu.HBM(a.shape, a.dtype) for a in srcs], *[pltpu.HBM(a.shape, a.dtype) for a in lands]),
        in_specs=[_HBM] * (2 * n) + [_SEM, _SEM, _SEM, pl.BlockSpec(memory_space=pl.ANY)],
        out_specs=tuple([_HBM] * (2 * n)),
        input_output_aliases={i: i for i in range(2 * n)},
        compiler_params=pltpu.CompilerParams(has_side_effects=_EFFECT),
    )(*srcs, *lands, send_sems, recv_sems, local_sems, after)
    return list(out[n:])


def _g2_first(srcs, lands, send_sems, recv_sems):
    x, y, c = _my_pos()
    me = 4 * x + 2 * y + c
    peers = [(x, y, 1 - c), (1 - x, y, c), (x, 1 - y, c), (1 - x, 1 - y, c)]
    return [[pltpu.make_async_remote_copy(
        src_ref=srcs[a], dst_ref=lands[a].at[me], send_sem=send_sems.at[4 * a + k], recv_sem=recv_sems.at[4 * a + k],
        device_id=p, device_id_type=MESH) for k, p in enumerate(peers)] for a in range(len(srcs))]


def _g2_relay(lands, send_sems, recv_sems):
    x, y, c = _my_pos()
    chips = [(1 - x, y), (x, 1 - y), (1 - x, 1 - y)]
    out = []
    for a in range(len(lands)):
        row = []
        for j, (px, py) in enumerate(chips):
            slab = lands[a].at[4 * px + 2 * py + c]
            row.append(pltpu.make_async_remote_copy(
                src_ref=slab, dst_ref=slab, send_sem=send_sems.at[3 * a + j], recv_sem=recv_sems.at[3 * a + j],
                device_id=(x, y, 1 - c), device_id_type=MESH))
        out.append(row)
    return out


def gather2_start(arrs, after, name):
    n = len(arrs)

    def body(*refs):
        srcs, lands = refs[:n], refs[n:2 * n]
        send_sems, recv_sems, local_sems = refs[2 * n + 1], refs[2 * n + 2], refs[2 * n + 3]
        for row in _g2_first(srcs, lands, send_sems, recv_sems):
            for cp in row:
                cp.start()
        for cp in _own_copies(srcs, lands, local_sems, False):
            cp.start()
        refs[-1][...] = jnp.zeros_like(refs[-1])

    lands0 = [lax.empty((N_DEV,) + a.shape, a.dtype) for a in arrs]
    out = pl.pallas_call(
        body, name=name,
        out_shape=(pltpu.SemaphoreType.DMA((4 * n,)), pltpu.SemaphoreType.DMA((4 * n,)), pltpu.SemaphoreType.DMA((n,)),
                   *[pltpu.HBM(a.shape, a.dtype) for a in arrs], *[pltpu.HBM(l.shape, l.dtype) for l in lands0],
                   _sds((8, 128), F32)),
        in_specs=[_HBM] * (2 * n) + [pl.BlockSpec(memory_space=pl.ANY)],
        out_specs=(_SEM, _SEM, _SEM, *([_HBM] * (2 * n)), pl.BlockSpec(memory_space=pltpu.VMEM)),
        input_output_aliases={i: 3 + i for i in range(2 * n)},
        compiler_params=pltpu.CompilerParams(has_side_effects=_EFFECT),
    )(*[pltpu.with_memory_space_constraint(a, pltpu.HBM) for a in arrs],
      *[pltpu.with_memory_space_constraint(l, pltpu.HBM) for l in lands0], after)
    return dict(send1=out[0], recv1=out[1], local=out[2], srcs=list(out[3:3 + n]), lands=list(out[3 + n:3 + 2 * n])), out[-1]


def gather2_relay(handle, after, name):
    n = len(handle["lands"])

    def body(*refs):
        lands = refs[:n]
        send1, recv1 = refs[n], refs[n + 1]
        send2, recv2 = refs[n + 3], refs[n + 4]
        firsts = _g2_first([l.at[0] for l in lands], lands, send1, recv1)
        relays = _g2_relay(lands, send2, recv2)
        for a in range(n):
            for j in range(3):
                firsts[a][1 + j].wait_recv()
                relays[a][j].start()
        refs[-1][...] = jnp.zeros_like(refs[-1])

    out = pl.pallas_call(
        body, name=name,
        out_shape=(pltpu.SemaphoreType.DMA((3 * n,)), pltpu.SemaphoreType.DMA((3 * n,)),
                   *[pltpu.HBM(l.shape, l.dtype) for l in handle["lands"]], _sds((8, 128), F32)),
        in_specs=[_HBM] * n + [_SEM, _SEM, pl.BlockSpec(memory_space=pl.ANY)],
        out_specs=(_SEM, _SEM, *([_HBM] * n), pl.BlockSpec(memory_space=pltpu.VMEM)),
        input_output_aliases={i: 2 + i for i in range(n)},
        compiler_params=pltpu.CompilerParams(has_side_effects=_EFFECT),
    )(*handle["lands"], handle["send1"], handle["recv1"], after)
    new = dict(handle)
    new.update(send2=out[0], recv2=out[1], lands=list(out[2:2 + n]))
    return new, out[-1]


def gather2_wait(handle, after, name):
    n = len(handle["lands"])

    def body(*refs):
        srcs, lands = refs[:n], refs[n:2 * n]
        send1, recv1, local, send2, recv2 = refs[2 * n:2 * n + 5]
        for a, row in enumerate(_g2_first(srcs, lands, send1, recv1)):
            row[0].wait_recv()
            for cp in row:
                cp.wait_send()
        for row in _g2_relay(lands, send2, recv2):
            for cp in row:
                cp.wait_send()
                cp.wait_recv()
        for cp in _own_copies(srcs, lands, local, False):
            cp.wait()

    out = pl.pallas_call(
        body, name=name,
        out_shape=(*[pltpu.HBM(a.shape, a.dtype) for a in handle["srcs"]],
                   *[pltpu.HBM(l.shape, l.dtype) for l in handle["lands"]]),
        in_specs=[_HBM] * (2 * n) + [_SEM] * 5 + [pl.BlockSpec(memory_space=pl.ANY)],
        out_specs=tuple([_HBM] * (2 * n)),
        input_output_aliases={i: i for i in range(2 * n)},
        compiler_params=pltpu.CompilerParams(has_side_effects=_EFFECT),
    )(*handle["srcs"], *handle["lands"], handle["send1"], handle["recv1"], handle["local"], handle["send2"],
      handle["recv2"], after)
    return list(out[n:])


def adamw(parts, w, m, v, name, row0=0, prev=None):
    r_tot, C = w.shape
    n_parts, R = parts.shape[0], parts.shape[1]
    row_bytes = 2 * (n_parts * C * parts.dtype.itemsize + 7 * C * 4)
    tr = R
    for cand in (512, 352, 256, 176, 128, 64):
        if R % cand == 0 and row0 % cand == 0 and R > cand and cand * row_bytes <= ADAMW_VMEM_BUDGET:
            tr = cand
            break
    tc = C
    if tr == R and R * row_bytes > ADAMW_VMEM_BUDGET:
        assert row0 == 0 and R == r_tot
        tc = next(t for t in (512, 256, 128) if C % t == 0 and R * row_bytes * t // C <= ADAMW_VMEM_BUDGET)
    assert row0 % tr == 0 and (tr % 8 == 0 or (tr == r_tot and row0 == 0))
    blk0 = row0 // tr
    c1 = 1.0 / (1.0 - ADAM_B1 ** ADAM_STEP)
    c2 = 1.0 / (1.0 - ADAM_B2 ** ADAM_STEP)

    def body(p_ref, w_ref, m_ref, v_ref, *rest):
        g_ref, d_ref, nm_ref, nv_ref = rest[-4:]
        g = p_ref[0].astype(F32)
        for k in range(1, n_parts):
            g = g + p_ref[k].astype(F32)
        nm = ADAM_B1 * m_ref[...] + (1.0 - ADAM_B1) * g
        nv = ADAM_B2 * v_ref[...] + (1.0 - ADAM_B2) * (g * g)
        g_ref[...] = g
        nm_ref[...] = nm
        nv_ref[...] = nv
        d_ref[...] = -ADAM_LR * ((nm * c1) / (jnp.sqrt(nv * c2) + ADAM_EPS) + ADAM_WD * w_ref[...])

    if tc == C:
        grid = (R // tr,)
        blk = pl.BlockSpec((tr, C), lambda i: (i + blk0, 0))
        p_spec = pl.BlockSpec((n_parts, tr, C), lambda i: (0, i, 0))
    else:
        grid = (C // tc,)
        blk = pl.BlockSpec((R, tc), lambda i: (0, i))
        p_spec = pl.BlockSpec((n_parts, R, tc), lambda i: (0, 0, i))
    in_specs = [p_spec, blk, blk, blk]
    args = [parts, w, m, v]
    aliases = {}
    if prev is not None:
        in_specs += [pl.BlockSpec(memory_space=pl.ANY)] * 4
        args += list(prev)
        aliases = {4 + k: k for k in range(4)}
    return pl.pallas_call(
        body, name=name, grid=grid, in_specs=in_specs, out_specs=(blk, blk, blk, blk),
        out_shape=tuple(_sds((r_tot, C), F32) for _ in range(4)), input_output_aliases=aliases,
        compiler_params=pltpu.CompilerParams(vmem_limit_bytes=VMEM_LIMIT))(*args)


def _ffn_fwd(x, nw, sh, sc, g, wt_gu, w_dn):
    h, gp, up, a = ffn_up(x, nw, sh, sc, wt_gu)
    y, xn = matmul(a, w_dn, "nn", BF, "ffn_down", res=x, gate=g, coef=0.5)
    return xn, (x, h, gp, up, a, y)


def _ffn_bwd(dxo, saved, nw, sc, g, wt_gu, w_dn):
    x, h, gp, up, a, y = saved
    dy, acc1 = resid_gate_bwd(dxo, y, g, 0.5, "ffn_gate_bwd")
    d_wdn = matmul(a, dy, "tn", BF, "ffn_down_wgrad")
    dg, du = ffn_down_dgrad(dy, w_dn, gp, up)
    d_wt = ffn_up_wgrad(dg, du, h)
    dh = ffn_up_dgrad(dg, du, wt_gu)
    dx, acc2 = norm_mod_bwd(x, dh, dxo, nw, sc, "ffn_norm_bwd")
    return dx, d_wt, d_wdn, (acc2[0], acc2[1], acc1[0]), acc2[2]


def _group_layout(a):
    L = a.shape[0]
    return a[:, :SSM_HEADS].reshape(L, SSM_GROUPS, _HPG).transpose(1, 0, 2)


def _ungroup_layout(a):
    L = a.shape[1]
    return jnp.pad(a.transpose(1, 0, 2).reshape(L, SSM_HEADS), ((0, 0), (0, 128 - SSM_HEADS)))


def _pad_row(vec, n=128):
    return jnp.pad(vec.reshape(1, -1), ((0, 0), (0, n - vec.shape[-1])))


def _mamba_fwd(x, nw, sh, sc, g, p):
    h = norm_mod(x, nw, sh, sc, "mix_norm")
    zx = matmul(h, p["w_in_t"], "nt", F32, "ssm_in")
    xc = conv_fwd(zx, p["conv_w"], p["conv_b"])
    dt, acs = dt_prep(zx, p["dt_bias"], p["a_log"])
    dt_g, acs_g = _group_layout(dt), _group_layout(acs)
    acs_t = acs_g.transpose(0, 2, 1)
    y, pst = ssd_fwd(xc, dt_g, acs_g, acs_t, p["d_exp"])
    yn = gate_norm(y, zx, p["norm_w"])
    yo, xn = matmul(yn, p["w_out"], "nn", BF, "ssm_out", res=x, gate=g, coef=1.0)
    return xn, (x, h, zx, xc, dt, dt_g, acs_g, acs_t, y, pst, yn, yo)


def _mamba_bwd(dxo, saved, nw, sc, g, p):
    x, h, zx, xc, dt, dt_g, acs_g, acs_t, y, pst, yn, yo = saved
    dyo, acc1 = resid_gate_bwd(dxo, yo, g, 1.0, "mix_gate_bwd")
    d_wout = matmul(yn, dyo, "tn", BF, "ssm_out_wgrad")
    dyn = matmul(dyo, p["w_out"], "nt", F32, "ssm_out_dgrad")
    dy, dz, accn = gate_norm_bwd(dyn, y, zx, p["norm_w"])
    dxs, dB, dC, ddt_g, da_g, dd = ssd_bwd(dy, xc, dt_g, acs_g, acs_t, pst, p["d_exp"])
    dxc = jnp.concatenate([dxs, dB, dC], axis=1)
    du, accc = conv_bwd(dxc, zx, p["conv_w"], p["conv_b"])
    draw, accdt = dt_bwd(_ungroup_layout(ddt_g), _ungroup_layout(da_g), dt, zx, p["dt_bias"], p["a_log"])
    dzx = jnp.concatenate([dz, du, draw], axis=1)
    d_win = matmul(dzx, h, "tn", BF, "ssm_in_wgrad")[:IN_PROJ]
    dh = matmul(dzx, p["w_in_t"], "nn", F32, "ssm_in_dgrad")
    dx, acc2 = norm_mod_bwd(x, dh, dxo, nw, sc, "mix_norm_bwd")
    small = dict(conv_w=accc[:CONV_WIDTH], conv_b=accc[CONV_WIDTH], dt_bias=accdt[0, :SSM_HEADS],
                 a_log=accdt[1, :SSM_HEADS], d=dd.reshape(SSM_HEADS, SSM_HEADDIM).sum(-1), norm_w=accn[0])
    return dx, d_win, d_wout, (acc2[0], acc2[1], acc1[0]), acc2[2], small


def _attn_layer_fwd(x, nw, sh, sc, g, p, kv):
    h = norm_mod(x, nw, sh, sc, "mix_norm")
    q = matmul(h, p["w_q"], "nn", F32, "attn_q", bias=p["b_q"])
    o = attn_fwd(q, kv, p["sinks"])
    yo, xn = matmul(o, p["w_o"], "nn", BF, "attn_o", bias=p["b_o"], res=x, gate=g, coef=1.0)
    return xn, (x, h, q, o, yo)


def _attn_layer_bwd(dxo, saved, nw, sc, g, p, kv):
    x, h, q, o, yo = saved
    dyo, acc1 = resid_gate_bwd(dxo, yo, g, 1.0, "mix_gate_bwd")
    d_wo = matmul(o, dyo, "tn", BF, "attn_o_wgrad")
    do = matmul(dyo, p["w_o"], "nt", F32, "attn_o_dgrad")
    dq, dkv_c, dkv_p, acca = attn_bwd(q, kv, do, p["sinks"])
    d_wq = matmul(h, dq, "tn", BF, "attn_q_wgrad")
    dh = matmul(dq, p["w_q"], "nt", F32, "attn_q_dgrad")
    dx, acc2 = norm_mod_bwd(x, dh, dxo, nw, sc, "mix_norm_bwd")
    small = dict(b_q=acca[0], sinks=acca[1, :ATT_HEADS], b_o=acc1[1])
    return dx, d_wq, d_wo, (acc2[0], acc2[1], acc1[0]), acc2[2], small, (dkv_c, dkv_p)


def _pack_rows(pieces):
    rows, spans, off = [], [], 0
    for a in pieces:
        flat = a.reshape(-1).astype(F32)
        n = -(-flat.shape[0] // D_MODEL)
        rows.append(jnp.pad(flat, (0, n * D_MODEL - flat.shape[0])).reshape(n, D_MODEL))
        spans.append((off, a.shape))
        off += n
    pad = -off % 8
    if pad:
        rows.append(jnp.zeros((pad, D_MODEL), F32))
    return jnp.concatenate(rows, axis=0), spans, off + pad


def _unpack_rows(g, spans):
    out = []
    for off, shape in spans:
        size = 1
        for s in shape:
            size *= s
        n = -(-size // D_MODEL)
        out.append(g[:, off:off + n].reshape(N_DEV, n * D_MODEL)[:, :size].reshape((N_DEV,) + tuple(shape)))
    return out


def _unshard_last(g):
    nd = g.ndim
    perm = tuple(range(1, nd - 1)) + (0, nd - 1)
    t = g.transpose(perm)
    return t.reshape(t.shape[:-2] + (N_DEV * g.shape[-1],))


def _shard_last(a, me):
    s = a.shape[-1] // N_DEV
    return lax.dynamic_slice_in_dim(a, me * s, s, axis=a.ndim - 1)


def kernel(x, c, ffn_norm_w, ffn_w_gu, ffn_w_down, mod_w, mod_b, mix_norm_w, ssm_w_in, ssm_conv_w, ssm_conv_b, ssm_dt_bias, ssm_a_log, ssm_d, ssm_norm_w, ssm_w_out, kv_norm_w, kv_mod_w, kv_mod_b, w_kv, b_kv, attn_w_q, attn_b_q, attn_sinks, attn_w_o, attn_b_o, final_norm_w, loss_target, m_ffn_norm_w, m_ffn_w_gu, m_ffn_w_down, m_mod_w, m_mod_b, m_mix_norm_w, m_ssm_w_in, m_ssm_conv_w, m_ssm_conv_b, m_ssm_dt_bias, m_ssm_a_log, m_ssm_d, m_ssm_norm_w, m_ssm_w_out, m_kv_norm_w, m_kv_mod_w, m_kv_mod_b, m_w_kv, m_b_kv, m_attn_w_q, m_attn_b_q, m_attn_sinks, m_attn_w_o, m_attn_b_o, m_final_norm_w, v_ffn_norm_w, v_ffn_w_gu, v_ffn_w_down, v_mod_w, v_mod_b, v_mix_norm_w, v_ssm_w_in, v_ssm_conv_w, v_ssm_conv_b, v_ssm_dt_bias, v_ssm_a_log, v_ssm_d, v_ssm_norm_w, v_ssm_w_out, v_kv_norm_w, v_kv_mod_w, v_kv_mod_b, v_w_kv, v_b_kv, v_attn_w_q, v_attn_b_q, v_attn_sinks, v_attn_w_o, v_attn_b_o, v_final_norm_w):
    D = D_MODEL
    me = 4 * lax.axis_index("x") + 2 * lax.axis_index("y") + lax.axis_index("c")
    xs = x[0]
    target = loss_target[0]
    mod_cols = mod_w.shape[-1]
    kvm_cols = kv_mod_w.shape[-1]

    def fence(arrs):
        tot = jnp.zeros((1, 1), F32)
        for a in arrs:
            tot = tot + lax.slice(a, (0,) * a.ndim, (1,) * a.ndim).reshape(1, 1).astype(F32)
        return jnp.broadcast_to(tot, (8, 128))

    packed, spans, _ = _pack_rows([c, ffn_norm_w, ssm_conv_w, ssm_conv_b, ssm_norm_w])
    nrow = packed.shape[0]
    g1 = small_all_gather(packed).reshape(N_DEV, nrow, D)
    c_all, fnw_g, cw_g, cb_g, snw_g = _unpack_rows(g1, spans)
    c_all = c_all.reshape(N_DEV, D)
    ffn_nw = _unshard_last(fnw_g)
    conv_w = _unshard_last(cw_g)
    conv_b = _unshard_last(cb_g)
    ssm_nw = _unshard_last(snw_g)

    mod_b_loc = lax.dynamic_slice_in_dim(mod_b, me * mod_cols, mod_cols, axis=1).reshape(DEPTH, 1, mod_cols)
    kvb_loc = lax.dynamic_slice_in_dim(kv_mod_b, me * kvm_cols, kvm_cols, axis=0).reshape(1, 1, kvm_cols)
    modp, c_act = mod_fwd(c_all, mod_w, mod_b_loc, "mod_fwd")
    kvmp, _ = mod_fwd(c_all, kv_mod_w.reshape(1, D, kvm_cols), kvb_loc, "kv_mod_fwd")
    packed2, spans2, _ = _pack_rows([modp, kvmp])
    nrow2 = packed2.shape[0]
    g2 = small_all_gather(packed2).reshape(N_DEV, nrow2, D)
    modp_g, kvmp_g = _unpack_rows(g2, spans2)
    mod_all = modp_g.transpose(1, 2, 0, 3).reshape(DEPTH, N_DEV, N_MOD * D)
    kvm_all = kvmp_g.transpose(1, 2, 0, 3).reshape(N_DEV, 2 * D)
    mod_me = lax.dynamic_index_in_dim(mod_all, me, axis=1, keepdims=False).reshape(DEPTH, N_MOD, 1, D)
    kvm_me = lax.dynamic_index_in_dim(kvm_all, me, axis=0, keepdims=False).reshape(2, 1, D)

    gu_t = jnp.swapaxes(ffn_w_gu, 2, 3)
    win_t = jnp.transpose(ssm_w_in, (2, 0, 1))
    S = gu_t.shape[2]
    s_in = win_t.shape[0]
    r_dn = ffn_w_down.shape[2]
    r_mix = ssm_w_out.shape[1]
    r_at = attn_w_q.shape[1]

    def layer_pack(k):
        arrs = [gu_t[k, 0].astype(BF), gu_t[k, 1].astype(BF), ffn_w_down[k, 0].astype(BF), ffn_w_down[k, 1].astype(BF)]
        if k < N_A:
            arrs += [ssm_w_out[k].astype(BF), win_t[:, k].astype(BF)]
        else:
            arrs += [attn_w_q[k - N_A].astype(BF), attn_w_o[k - N_A].astype(BF)]
        if k == N_A:
            arrs.append(w_kv.astype(BF))
        return arrs

    packs = [layer_pack(k) for k in range(DEPTH)]
    gathered = [None] * DEPTH
    first = big_all_gather([packs[0][0], packs[0][2]])
    pending = [None] * DEPTH
    pend_mix, tok_next = gather2_start(packs[0][4:6], fence([g2, first[0]]), "gather_start_0m")
    pending[0], tok_next = gather2_start([packs[0][1], packs[0][3]], tok_next, "gather_start_0")
    for k in range(1, DEPTH):
        pending[k], tok_next = gather2_start(packs[k], tok_next, "gather_start_%d" % k)
    gathered[0] = [first[0], None, first[1], None, None, None]

    def wt_gu_full(i, j):
        return gathered[i][j].reshape(N_DEV * S, D)

    def w_dn_full(i, j):
        return gathered[i][2 + j].reshape(D_FF, D)

    def mix_rows(i, a):
        return gathered[i][4 + a].reshape(-1, D)

    def mamba_params(j):
        w_in_t = jnp.pad(mix_rows(j, 1), ((0, IN_PROJ_PAD - IN_PROJ), (0, 0)))
        return dict(w_in_t=w_in_t, w_out=mix_rows(j, 0), conv_w=conv_w[j], conv_b=conv_b[j].reshape(1, -1),
                    dt_bias=_pad_row(ssm_dt_bias[j]), a_log=_pad_row(ssm_a_log[j]),
                    d_exp=jnp.repeat(ssm_d[j], SSM_HEADDIM).reshape(SSM_GROUPS, 1, _GW),
                    norm_w=ssm_nw[j].reshape(1, -1))

    def attn_params(j):
        return dict(w_q=mix_rows(N_A + j, 0), w_o=mix_rows(N_A + j, 1),
                    b_q=attn_b_q[j].reshape(1, -1), b_o=attn_b_o[j].reshape(1, -1), sinks=_pad_row(attn_sinks[j]))

    saved = []
    kv = None
    kv_saved = None
    w_kv_full = None
    xcur = xs
    for i in range(DEPTH):
        if i >= 1:
            gathered[i] = gather2_wait(pending[i], xcur, "gather_wait_%d" % i)
        md = mod_me[i]
        if i == 0:
            md = md + tok_next[0, 0]
        if i == N_A:
            w_kv_full = gathered[N_A][6].reshape(D, KV_DIM)
            h_kv = norm_mod(xcur, kv_norm_w.reshape(1, D), kvm_me[0], kvm_me[1], "kv_norm")
            kv = matmul(h_kv, w_kv_full, "nn", F32, "kv_proj", bias=b_kv.reshape(1, -1))
            kv_saved = (xcur, h_kv)
        x1, s1 = _ffn_fwd(xcur, ffn_nw[i, 0].reshape(1, D), md[0], md[1], md[2], wt_gu_full(i, 0), w_dn_full(i, 0))
        gm = md[5]
        if i == 0:
            pend_mix, tok_r = gather2_relay(pend_mix, x1, "gather_relay_0m")
            gathered[0][4:6] = gather2_wait(pend_mix, tok_r, "gather_wait_0m")
        if i < N_A:
            pm = mamba_params(i)
            x2, s2 = _mamba_fwd(x1, mix_norm_w[i].reshape(1, D), md[3], md[4], gm, pm)
        else:
            pm = attn_params(i - N_A)
            x2, s2 = _attn_layer_fwd(x1, mix_norm_w[i].reshape(1, D), md[3], md[4], gm, pm, kv)
        g2f = md[8]
        if i + 1 < DEPTH:
            pending[i + 1], tok_r = gather2_relay(pending[i + 1], x2, "gather_relay_%d" % (i + 1))
            g2f = g2f + tok_r[0, 0]
        if i == 0:
            pending[0], tok_r = gather2_relay(pending[0], x2, "gather_relay_0")
            rest = gather2_wait(pending[0], tok_r, "gather_wait_0")
            gathered[0] = [first[0], rest[0], first[1], rest[1]] + gathered[0][4:6]
        x3, s3 = _ffn_fwd(x2, ffn_nw[i, 1].reshape(1, D), md[6], md[7], g2f, wt_gu_full(i, 1), w_dn_full(i, 1))
        saved.append((s1, s2, s3, pm))
        xcur = x3

    dx, accf = final_loss(xcur, final_norm_w.reshape(1, D), target)
    d_mod = [None] * DEPTH
    d_ffn_nw = [[None, None] for _ in range(DEPTH)]
    d_mix_nw = [None] * DEPTH
    sm_m, sm_a = [None] * N_A, [None] * N_A
    kv_parts = [None] * N_A
    d_kvm = d_kv_nw = d_bkv = None
    exchanges = []
    tok = None

    def send(arrs, tag, after=None):
        handle, t = xfer_start(arrs, True, dx if after is None else after, "exch_start_%s" % tag)
        exchanges.append((handle, tag))
        return t

    def ffn_slabs(d_wt, d_wdn):
        return [d_wt.reshape(N_DEV, S, D), d_wdn.reshape(N_DEV, r_dn, D)]

    for i in reversed(range(DEPTH)):
        md = mod_me[i]
        s1, s2, s3, pm = saved[i]
        g2 = md[8] if tok is None else md[8] + tok[0, 0]
        dx, d_wt, d_wdn, m2, d_ffn_nw[i][1] = _ffn_bwd(
            dx, s3, ffn_nw[i, 1].reshape(1, D), md[7], g2, wt_gu_full(i, 1), w_dn_full(i, 1))
        tok = send(ffn_slabs(d_wt, d_wdn), "f%d1" % i)
        gm = md[5] + tok[0, 0]
        if i < N_A:
            dx, d_in, d_out, mm_, d_mix_nw[i], sm_m[i] = _mamba_bwd(dx, s2, mix_norm_w[i].reshape(1, D), md[4], gm, pm)
            tok = send([d_in.reshape(N_DEV, s_in, D), d_out.reshape(N_DEV, r_mix, D)], "m%d" % i)
        else:
            j = i - N_A
            dx, d_q, d_o, mm_, d_mix_nw[i], sm_a[j], kv_parts[j] = _attn_layer_bwd(
                dx, s2, mix_norm_w[i].reshape(1, D), md[4], gm, pm, kv)
            tok = send([d_q.reshape(N_DEV, r_at, D), d_o.reshape(N_DEV, r_at, D)], "m%d" % i)
        g1 = md[2] + tok[0, 0]
        dx, d_wt, d_wdn, m1, d_ffn_nw[i][0] = _ffn_bwd(
            dx, s1, ffn_nw[i, 0].reshape(1, D), md[1], g1, wt_gu_full(i, 0), w_dn_full(i, 0))
        d_mod[i] = jnp.concatenate(list(m1) + list(mm_) + list(m2), axis=0)
        last = ffn_slabs(d_wt, d_wdn)
        if i == N_A:
            x_kv, h_kv = kv_saved
            dkv, acck = kv_grad_combine(kv_parts)
            d_bkv = acck[0]
            d_kv_w = matmul(h_kv, dkv, "tn", BF, "kv_wgrad")
            dh_kv = matmul(dkv, w_kv_full, "nt", F32, "kv_dgrad")
            dx, acc_kv = norm_mod_bwd(x_kv, dh_kv, dx, kv_norm_w.reshape(1, D), kvm_me[1], "kv_norm_bwd")
            d_kvm = jnp.concatenate([acc_kv[0], acc_kv[1]], axis=0)
            d_kv_nw = acc_kv[2]
            last.append(d_kv_w.reshape(N_DEV, -1, KV_DIM))
        if i > 0:
            tok = send(last, "f%d0" % i)
    grad_x = dx.reshape(x.shape)

    small_list = [
        jnp.stack(d_mod, 0), d_kvm,
        jnp.stack([jnp.stack(r, 0) for r in d_ffn_nw], 0),
        jnp.stack(d_mix_nw, 0),
        jnp.stack([s["conv_w"] for s in sm_m], 0), jnp.stack([s["conv_b"] for s in sm_m], 0),
        jnp.stack([s["dt_bias"] for s in sm_m], 0), jnp.stack([s["a_log"] for s in sm_m], 0),
        jnp.stack([s["d"] for s in sm_m], 0), jnp.stack([s["norm_w"] for s in sm_m], 0),
        d_kv_nw, d_bkv,
        jnp.stack([s["b_q"] for s in sm_a], 0), jnp.stack([s["sinks"] for s in sm_a], 0),
        jnp.stack([s["b_o"] for s in sm_a], 0), accf[0], accf[1],
    ]
    packed3, spans3, _ = _pack_rows(small_list)
    nrow3 = packed3.shape[0]
    g3 = small_all_gather(packed3).reshape(N_DEV, nrow3, D)
    tok_last = send(last, "f00", after=g3)
    (p_mod, p_kvm, p_fnw, p_mnw, p_cw, p_cb, p_dtb, p_al, p_d, p_snw, p_kvnw, p_bkv, p_bq, p_sk, p_bo, p_fin,
     p_loss) = _unpack_rows(g3, spans3)

    loss = 0.5 / D * jnp.sum(p_loss)

    c_act_t = c_act.T
    dmod_loc = _shard_last(p_mod, me).transpose(1, 0, 2)
    dkvm_loc = _shard_last(p_kvm, me).reshape(1, N_DEV, kvm_cols) + tok_last[0, 0]
    gp_mod_w = mod_wgrad(c_act_t, dmod_loc, "mod_wgrad")
    gp_kvm_w = mod_wgrad(c_act_t, dkvm_loc, "kv_mod_wgrad")[0]

    def as_parts_single(a):
        return a[None]

    def upd(name, parts, w, m, v):
        shp = w.shape
        c_last = shp[-1]
        out = adamw(parts.reshape(parts.shape[0], -1, c_last), w.reshape(-1, c_last), m.reshape(-1, c_last),
                    v.reshape(-1, c_last), "adamw_" + name)
        return tuple(o.reshape(shp) for o in out)

    views = {
        "ffn_w_gu": [jnp.swapaxes(t, 2, 3).reshape(-1, D) for t in (ffn_w_gu, m_ffn_w_gu, v_ffn_w_gu)],
        "ffn_w_down": [t.reshape(-1, D) for t in (ffn_w_down, m_ffn_w_down, v_ffn_w_down)],
        "ssm_w_out": [t.reshape(-1, D) for t in (ssm_w_out, m_ssm_w_out, v_ssm_w_out)],
        "attn_w_q": [t.reshape(-1, D) for t in (attn_w_q, m_attn_w_q, v_attn_w_q)],
        "attn_w_o": [t.reshape(-1, D) for t in (attn_w_o, m_attn_w_o, v_attn_w_o)],
    }
    filled = {k: None for k in views}

    def upd_rows(name, parts, row0):
        w, m, v = views[name]
        filled[name] = adamw(parts, w, m, v, "adamw_" + name, row0=row0, prev=filled[name])
        return filled[name][3]

    res = {}
    res["ffn_norm_w"] = upd("ffn_norm_w", _shard_last(p_fnw, me), ffn_norm_w, m_ffn_norm_w, v_ffn_norm_w)
    res["mod_w"] = upd("mod_w", as_parts_single(gp_mod_w), mod_w, m_mod_w, v_mod_w)
    res["mod_b"] = upd("mod_b", p_mod, mod_b, m_mod_b, v_mod_b)
    res["mix_norm_w"] = upd("mix_norm_w", p_mnw, mix_norm_w, m_mix_norm_w, v_mix_norm_w)
    res["ssm_conv_w"] = upd("ssm_conv_w", _shard_last(p_cw, me), ssm_conv_w, m_ssm_conv_w, v_ssm_conv_w)
    res["ssm_conv_b"] = upd("ssm_conv_b", _shard_last(p_cb, me), ssm_conv_b, m_ssm_conv_b, v_ssm_conv_b)
    res["ssm_dt_bias"] = upd("ssm_dt_bias", p_dtb, ssm_dt_bias, m_ssm_dt_bias, v_ssm_dt_bias)
    res["ssm_a_log"] = upd("ssm_a_log", p_al, ssm_a_log, m_ssm_a_log, v_ssm_a_log)
    res["ssm_d"] = upd("ssm_d", p_d, ssm_d, m_ssm_d, v_ssm_d)
    res["ssm_norm_w"] = upd("ssm_norm_w", _shard_last(p_snw, me), ssm_norm_w, m_ssm_norm_w, v_ssm_norm_w)
    res["kv_norm_w"] = upd("kv_norm_w", p_kvnw.reshape(N_DEV, 1, D), kv_norm_w.reshape(1, D),
                           m_kv_norm_w.reshape(1, D), v_kv_norm_w.reshape(1, D))
    res["kv_mod_w"] = upd("kv_mod_w", as_parts_single(gp_kvm_w), kv_mod_w, m_kv_mod_w, v_kv_mod_w)
    res["kv_mod_b"] = upd("kv_mod_b", p_kvm.reshape(N_DEV, 1, 2 * D), kv_mod_b.reshape(1, -1),
                          m_kv_mod_b.reshape(1, -1), v_kv_mod_b.reshape(1, -1))
    res["b_kv"] = upd("b_kv", p_bkv.reshape(N_DEV, 1, KV_DIM), b_kv.reshape(1, -1), m_b_kv.reshape(1, -1),
                      v_b_kv.reshape(1, -1))
    res["attn_b_q"] = upd("attn_b_q", p_bq, attn_b_q, m_attn_b_q, v_attn_b_q)
    res["attn_sinks"] = upd("attn_sinks", p_sk, attn_sinks, m_attn_sinks, v_attn_sinks)
    res["attn_b_o"] = upd("attn_b_o", p_bo, attn_b_o, m_attn_b_o, v_attn_b_o)
    res["final_norm_w"] = upd("final_norm_w", p_fin.reshape(N_DEV, 1, D), final_norm_w.reshape(1, D),
                              m_final_norm_w.reshape(1, D), v_final_norm_w.reshape(1, D))

    chain = fence([dx, tok_last] + [t[3] for t in res.values()])
    r_in_parts = [None] * N_A
    r_kv = None
    for handle, tag in exchanges:
        got = xfer_wait(handle, chain, "exch_wait_%s" % tag)
        i = int(tag[1])
        if tag[0] == "f":
            jf = int(tag[2])
            done = [upd_rows("ffn_w_gu", got[0], (2 * i + jf) * S), upd_rows("ffn_w_down", got[1], (2 * i + jf) * r_dn)]
            if len(got) > 2:
                res["w_kv"] = upd("w_kv", got[2], w_kv, m_w_kv, v_w_kv)
                done.append(res["w_kv"][3])
        elif i < N_A:
            r_in_parts[i] = got[0]
            done = [upd_rows("ssm_w_out", got[1], i * r_mix)]
            if i == 0:
                win_out = adamw(jnp.stack(r_in_parts, axis=2).reshape(N_DEV, s_in * N_A, D),
                                *[jnp.transpose(t, (2, 0, 1)).reshape(-1, D) for t in (ssm_w_in, m_ssm_w_in, v_ssm_w_in)],
                                "adamw_ssm_w_in")
                res["ssm_w_in"] = tuple(jnp.transpose(t.reshape(win_t.shape), (1, 2, 0)) for t in win_out)
                done.append(win_out[3])
        else:
            done = [upd_rows("attn_w_q", got[0], (i - N_A) * r_at), upd_rows("attn_w_o", got[1], (i - N_A) * r_at)]
        chain = fence(done)

    res["ffn_w_gu"] = tuple(jnp.swapaxes(t.reshape(gu_t.shape), 2, 3) for t in filled["ffn_w_gu"])
    res["ffn_w_down"] = tuple(t.reshape(ffn_w_down.shape) for t in filled["ffn_w_down"])
    res["ssm_w_out"] = tuple(t.reshape(ssm_w_out.shape) for t in filled["ssm_w_out"])
    res["attn_w_q"] = tuple(t.reshape(attn_w_q.shape) for t in filled["attn_w_q"])
    res["attn_w_o"] = tuple(t.reshape(attn_w_o.shape) for t in filled["attn_w_o"])

    names = ["ffn_norm_w", "ffn_w_gu", "ffn_w_down", "mod_w", "mod_b", "mix_norm_w", "ssm_w_in", "ssm_conv_w",
             "ssm_conv_b", "ssm_dt_bias", "ssm_a_log", "ssm_d", "ssm_norm_w", "ssm_w_out", "kv_norm_w", "kv_mod_w",
             "kv_mod_b", "w_kv", "b_kv", "attn_w_q", "attn_b_q", "attn_sinks", "attn_w_o", "attn_b_o", "final_norm_w"]
    vec_shapes = {"kv_norm_w": (D,), "kv_mod_b": (2 * D,), "b_kv": (KV_DIM,), "final_norm_w": (D,)}
    outs = [loss, grad_x]
    for k in range(4):
        for nme in names:
            t = res[nme][k]
            if nme in vec_shapes:
                t = t.reshape(vec_shapes[nme])
            outs.append(t)
    return tuple(outs)
```

```python
import functools

import jax
import jax.numpy as jnp
from jax import lax
from jax.experimental import pallas as pl
from jax.experimental.pallas import tpu as pltpu

F32 = jnp.float32
BF = jnp.bfloat16
MESH = pl.DeviceIdType.MESH

N_DEV = 8
D_MODEL = 1024
DEPTH = 4
N_A = 2
EPS = 1e-5
N_MOD = 9
D_FF = 2816
D_INNER = 2048
SSM_HEADDIM = 64
SSM_HEADS = 32
SSM_GROUPS = 8
SSM_STATE = 128
CONV_WIDTH = 4
CHUNK = 512
CONV_DIM = D_INNER + 2 * SSM_GROUPS * SSM_STATE
IN_PROJ = D_INNER + CONV_DIM + SSM_HEADS
IN_PROJ_PAD = D_INNER + CONV_DIM + 128
ATT_HEADS = 16
KV_HEADS = 4
HEAD_DIM = 64
WINDOW = 128
KV_DIM = 2 * KV_HEADS * HEAD_DIM

ADAM_LR = 0.001
ADAM_B1 = 0.9
ADAM_B2 = 0.999
ADAM_EPS = 1e-08
ADAM_WD = 0.01
ADAM_STEP = 10

VMEM_LIMIT = 48 * 2 ** 20
ADAMW_VMEM_BUDGET = 24 * 2 ** 20
NEG = -1e30


def _call(body, name, grid, in_specs, out_specs, out_shape, scratch=()):
    return pl.pallas_call(
        body, name=name, grid=grid, in_specs=in_specs, out_specs=out_specs, out_shape=out_shape,
        scratch_shapes=list(scratch),
        compiler_params=pltpu.CompilerParams(vmem_limit_bytes=VMEM_LIMIT))


def _tile(n, cap):
    t = (cap // 128) * 128
    while t >= 128:
        if n % t == 0:
            return t
        t -= 128
    return n


def _sds(shape, dtype):
    return jax.ShapeDtypeStruct(shape, dtype)


def _sigmoid(v):
    return 1.0 / (1.0 + jnp.exp(-v))


def _dot(a, b, dims):
    return lax.dot_general(a, b, (dims, ((), ())), preferred_element_type=F32)


def _dot_nn(a, b):
    return _dot(a.astype(BF), b.astype(BF), ((1,), (0,)))


def _dot_nt(a, b):
    return _dot(a.astype(BF), b.astype(BF), ((1,), (1,)))


def _dot_tn(a, b):
    return _dot(a.astype(BF), b.astype(BF), ((0,), (0,)))


def matmul(a, b, mode, out_dtype, name, bias=None, res=None, gate=None, coef=1.0):
    if mode == "nn":
        (M, K), (_, N) = a.shape, b.shape
    elif mode == "nt":
        (M, K), (N, _) = a.shape, b.shape
    else:
        (K, M), (_, N) = a.shape, b.shape
    cap_n = 512 if K > 4096 else 1024
    tm = _tile(M, 1024 if (mode == "tn" or K <= D_FF) else 512)
    tn = _tile(N, cap_n)
    if mode != "tn" and tm * tn > 1024 * 896:
        tn = _tile(N, 512)
    if mode == "nn":
        a_spec = pl.BlockSpec((tm, K), lambda i, j: (i, 0))
        b_spec = pl.BlockSpec((K, tn), lambda i, j: (0, j))
        fn = _dot_nn
    elif mode == "nt":
        a_spec = pl.BlockSpec((tm, K), lambda i, j: (i, 0))
        b_spec = pl.BlockSpec((tn, K), lambda i, j: (j, 0))
        fn = _dot_nt
    else:
        a_spec = pl.BlockSpec((K, tm), lambda i, j: (0, i))
        b_spec = pl.BlockSpec((K, tn), lambda i, j: (0, j))
        fn = _dot_tn
    has_bias, has_res = bias is not None, res is not None
    o_spec = pl.BlockSpec((tm, tn), lambda i, j: (i, j))
    v_spec = pl.BlockSpec((1, tn), lambda i, j: (0, j))
    in_specs, args = [a_spec, b_spec], [a, b]
    if has_bias:
        in_specs.append(v_spec)
        args.append(bias)
    if has_res:
        in_specs += [o_spec, v_spec]
        args += [res, gate]

    def body(*refs):
        a_ref, b_ref = refs[0], refs[1]
        k = 2
        y = fn(a_ref[...], b_ref[...])
        if has_bias:
            y = y + refs[k][...]
            k += 1
        if has_res:
            res_ref, gate_ref = refs[k], refs[k + 1]
            refs[k + 2][...] = y.astype(out_dtype)
            refs[k + 3][...] = res_ref[...] + coef * gate_ref[...] * y
        else:
            refs[k][...] = y.astype(out_dtype)

    if has_res:
        out_shape = (_sds((M, N), out_dtype), _sds((M, N), F32))
        out_specs = (o_spec, o_spec)
    else:
        out_shape = _sds((M, N), out_dtype)
        out_specs = o_spec
    return _call(body, name, (M // tm, N // tn), in_specs, out_specs, out_shape)(*args)


def norm_mod(x, nw, sh, sc, name):
    L, D = x.shape
    tm = _tile(L, 512)

    def body(x_ref, nw_ref, sh_ref, sc_ref, h_ref):
        xf = x_ref[...]
        r = lax.rsqrt(jnp.mean(xf * xf, axis=-1, keepdims=True) + EPS)
        n = xf * r * nw_ref[...]
        h_ref[...] = (n * (1.0 + sc_ref[...]) + sh_ref[...]).astype(BF)

    row = pl.BlockSpec((tm, D), lambda i: (i, 0))
    vec = pl.BlockSpec((1, D), lambda i: (0, 0))
    return _call(body, name, (L // tm,), [row, vec, vec, vec], row, _sds((L, D), BF))(x, nw, sh, sc)


def norm_mod_bwd(x, dh, dres, nw, sc, name):
    L, D = x.shape
    tm = _tile(L, 512)

    def body(x_ref, dh_ref, dres_ref, nw_ref, sc_ref, dx_ref, acc_ref):
        @pl.when(pl.program_id(0) == 0)
        def _():
            acc_ref[...] = jnp.zeros_like(acc_ref)

        xf = x_ref[...]
        dhf = dh_ref[...].astype(F32)
        r = lax.rsqrt(jnp.mean(xf * xf, axis=-1, keepdims=True) + EPS)
        xhat = xf * r
        nwv = nw_ref[...]
        dn = dhf * (1.0 + sc_ref[...])
        dxhat = dn * nwv
        proj = jnp.mean(dxhat * xhat, axis=-1, keepdims=True)
        dx_ref[...] = dres_ref[...] + r * (dxhat - xhat * proj)
        acc_ref[0:1, :] += jnp.sum(dhf, axis=0, keepdims=True)
        acc_ref[1:2, :] += jnp.sum(dhf * xhat * nwv, axis=0, keepdims=True)
        acc_ref[2:3, :] += jnp.sum(dn * xhat, axis=0, keepdims=True)

    row = pl.BlockSpec((tm, D), lambda i: (i, 0))
    vec = pl.BlockSpec((1, D), lambda i: (0, 0))
    acc = pl.BlockSpec((8, D), lambda i: (0, 0))
    return _call(body, name, (L // tm,), [row, row, row, vec, vec], (row, acc),
                 (_sds((L, D), F32), _sds((8, D), F32)))(x, dh, dres, nw, sc)


def final_loss(x, nw, target):
    L, D = x.shape
    tm = _tile(L, 512)

    def body(x_ref, nw_ref, t_ref, dx_ref, acc_ref):
        @pl.when(pl.program_id(0) == 0)
        def _():
            acc_ref[...] = jnp.zeros_like(acc_ref)

        xf = x_ref[...]
        r = lax.rsqrt(jnp.mean(xf * xf, axis=-1, keepdims=True) + EPS)
        xhat = xf * r
        nwv = nw_ref[...]
        err = xhat * nwv - t_ref[...]
        dy = err * (1.0 / D)
        dxhat = dy * nwv
        proj = jnp.mean(dxhat * xhat, axis=-1, keepdims=True)
        dx_ref[...] = r * (dxhat - xhat * proj)
        acc_ref[0:1, :] += jnp.sum(dy * xhat, axis=0, keepdims=True)
        acc_ref[1:2, :] += jnp.sum(err * err, axis=0, keepdims=True)

    row = pl.BlockSpec((tm, D), lambda i: (i, 0))
    vec = pl.BlockSpec((1, D), lambda i: (0, 0))
    acc = pl.BlockSpec((8, D), lambda i: (0, 0))
    return _call(body, "final_loss", (L // tm,), [row, vec, row], (row, acc),
                 (_sds((L, D), F32), _sds((8, D), F32)))(x, nw, target)


def resid_gate_bwd(dxo, y, gate, coef, name):
    L, D = dxo.shape
    tm = _tile(L, 512)

    def body(dxo_ref, y_ref, g_ref, dy_ref, acc_ref):
        @pl.when(pl.program_id(0) == 0)
        def _():
            acc_ref[...] = jnp.zeros_like(acc_ref)

        d = dxo_ref[...]
        dy = coef * g_ref[...] * d
        dy_ref[...] = dy.astype(BF)
        acc_ref[0:1, :] += coef * jnp.sum(d * y_ref[...].astype(F32), axis=0, keepdims=True)
        acc_ref[1:2, :] += jnp.sum(dy, axis=0, keepdims=True)

    row = pl.BlockSpec((tm, D), lambda i: (i, 0))
    vec = pl.BlockSpec((1, D), lambda i: (0, 0))
    acc = pl.BlockSpec((8, D), lambda i: (0, 0))
    return _call(body, name, (L // tm,), [row, row, vec], (row, acc),
                 (_sds((L, D), BF), _sds((8, D), F32)))(dxo, y, gate)


def ffn_up(h, wt):
    L, D = h.shape
    F = wt.shape[0] // 2
    tm, tn = _tile(L, 2048), _tile(F, 256)
    nj = F // tn

    def body(h_ref, wg_ref, wu_ref, g_ref, u_ref, a_ref):
        hv = h_ref[...]
        g = _dot_nt(hv, wg_ref[...])
        u = _dot_nt(hv, wu_ref[...])
        g_ref[...] = g.astype(BF)
        u_ref[...] = u.astype(BF)
        a_ref[...] = (g * _sigmoid(g) * u).astype(BF)

    o = pl.BlockSpec((tm, tn), lambda i, n: (i, n))
    return _call(body, "ffn_up", (L // tm, nj),
                 [pl.BlockSpec((tm, D), lambda i, n: (i, 0)),
                  pl.BlockSpec((tn, D), lambda i, n: (n, 0)),
                  pl.BlockSpec((tn, D), lambda i, n: (n + nj, 0))],
                 (o, o, o), tuple(_sds((L, F), BF) for _ in range(3)))(h, wt, wt)


def ffn_down_dgrad(dy, wd, g, u):
    L, D = dy.shape
    F = wd.shape[0]
    tm, tn = _tile(L, 2048), _tile(F, 256)

    def body(dy_ref, w_ref, g_ref, u_ref, dg_ref, du_ref):
        da = _dot_nt(dy_ref[...], w_ref[...])
        gv = g_ref[...].astype(F32)
        uv = u_ref[...].astype(F32)
        s = _sigmoid(gv)
        dg_ref[...] = (da * uv * s * (1.0 + gv * (1.0 - s))).astype(BF)
        du_ref[...] = (da * gv * s).astype(BF)

    o = pl.BlockSpec((tm, tn), lambda i, n: (i, n))
    return _call(body, "ffn_down_dgrad", (L // tm, F // tn),
                 [pl.BlockSpec((tm, D), lambda i, n: (i, 0)), pl.BlockSpec((tn, D), lambda i, n: (n, 0)), o, o],
                 (o, o), (_sds((L, F), BF), _sds((L, F), BF)))(dy, wd, g, u)


def ffn_up_wgrad(dg, du, h):
    L, F = dg.shape
    D = h.shape[1]
    tm = _tile(F, 256)
    nblk = F // tm

    def half(d, off, prev):
        def body(d_ref, h_ref, *rest):
            rest[-1][...] = _dot_tn(d_ref[...], h_ref[...]).astype(BF)

        in_specs = [pl.BlockSpec((L, tm), lambda i: (0, i)), pl.BlockSpec((L, D), lambda i: (0, 0))]
        args = [d, h]
        aliases = {}
        if prev is not None:
            in_specs.append(pl.BlockSpec(memory_space=pl.ANY))
            args.append(prev)
            aliases = {2: 0}
        return pl.pallas_call(
            body, name="ffn_up_wgrad", grid=(nblk,), in_specs=in_specs,
            out_specs=pl.BlockSpec((tm, D), lambda i: (i + off * nblk, 0)),
            out_shape=_sds((2 * F, D), BF), input_output_aliases=aliases,
            compiler_params=pltpu.CompilerParams(vmem_limit_bytes=VMEM_LIMIT))(*args)

    return half(du, 1, half(dg, 0, None))


def ffn_up_dgrad(dg, du, wt):
    L, F = dg.shape
    D = wt.shape[1]
    tm, tn = _tile(L, 1024), _tile(D, 512)

    def body(dg_ref, du_ref, wg_ref, wu_ref, o_ref):
        o_ref[...] = _dot_nn(dg_ref[...], wg_ref[...]) + _dot_nn(du_ref[...], wu_ref[...])

    a = pl.BlockSpec((tm, F), lambda i, n: (i, 0))
    return _call(body, "ffn_up_dgrad", (L // tm, D // tn),
                 [a, a, pl.BlockSpec((F, tn), lambda i, n: (0, n)), pl.BlockSpec((F, tn), lambda i, n: (1, n))],
                 pl.BlockSpec((tm, tn), lambda i, n: (i, n)), _sds((L, D), F32))(dg, du, wt, wt)


_HALO = 8


def _shift_down(cur, prev8, k):
    out = pltpu.roll(cur, k, 0)
    rows8 = lax.broadcasted_iota(jnp.int32, prev8.shape, 0)
    head = jnp.where(rows8 < k, pltpu.roll(prev8, k, 0), out[0:_HALO])
    if cur.shape[0] == _HALO:
        return head
    return jnp.concatenate([head, out[_HALO:]], axis=0)


def _shift_up(cur, next8, k):
    n = cur.shape[0]
    out = pltpu.roll(cur, n - k, 0)
    rows8 = lax.broadcasted_iota(jnp.int32, next8.shape, 0)
    tail = jnp.where(rows8 >= _HALO - k, pltpu.roll(next8, _HALO - k, 0), out[n - _HALO:])
    return jnp.concatenate([out[:n - _HALO], tail], axis=0)


def _conv_pre(cur, prev8, w_ref, b_ref):
    shifted = [_shift_down(cur, prev8, k) for k in range(1, CONV_WIDTH)]
    s = cur * w_ref[CONV_WIDTH - 1:CONV_WIDTH, :] + b_ref[...]
    for k in range(1, CONV_WIDTH):
        s = s + shifted[k - 1] * w_ref[CONV_WIDTH - 1 - k:CONV_WIDTH - k, :]
    return s, shifted


def _silu_grad(s):
    sg = _sigmoid(s)
    return sg * (1.0 + s * (1.0 - sg))


def conv_fwd(zx, w, b):
    L = zx.shape[0]
    tm, tc = _tile(L, 512), 1024
    xbc0 = D_INNER // tc
    hb = tm // _HALO

    def body(cur_ref, prev_ref, w_ref, b_ref, o_ref):
        prev8 = jnp.where(pl.program_id(1) > 0, prev_ref[...], 0.0)
        s, _ = _conv_pre(cur_ref[...], prev8, w_ref, b_ref)
        o_ref[...] = s * _sigmoid(s)

    return _call(body, "conv_fwd", (CONV_DIM // tc, L // tm),
                 [pl.BlockSpec((tm, tc), lambda j, i: (i, xbc0 + j)),
                  pl.BlockSpec((_HALO, tc), lambda j, i: (jnp.maximum(i * hb - 1, 0), xbc0 + j)),
                  pl.BlockSpec((CONV_WIDTH, tc), lambda j, i: (0, j)),
                  pl.BlockSpec((1, tc), lambda j, i: (0, j))],
                 pl.BlockSpec((tm, tc), lambda j, i: (i, j)), _sds((L, CONV_DIM), F32))(zx, zx, w, b)


def conv_bwd(dxc, zx, w, b):
    L = zx.shape[0]
    tm, tc = _tile(L, 512), 1024
    xbc0 = D_INNER // tc
    nblk = L // tm
    hb = tm // _HALO

    def body(d_ref, dn_ref, cur_ref, prev_ref, next_ref, w_ref, b_ref, du_ref, acc_ref):
        i = pl.program_id(1)

        @pl.when(i == 0)
        def _():
            acc_ref[...] = jnp.zeros_like(acc_ref)

        cur = cur_ref[...]
        prev8 = jnp.where(i > 0, prev_ref[...], 0.0)
        s, shifted = _conv_pre(cur, prev8, w_ref, b_ref)
        ds_c = d_ref[...] * _silu_grad(s)
        s_n, _ = _conv_pre(next_ref[...], cur[tm - _HALO:], w_ref, b_ref)
        ds_n = jnp.where(i < nblk - 1, dn_ref[...] * _silu_grad(s_n), 0.0)
        du = ds_c * w_ref[CONV_WIDTH - 1:CONV_WIDTH, :]
        acc_ref[CONV_WIDTH - 1:CONV_WIDTH, :] += jnp.sum(ds_c * cur, axis=0, keepdims=True)
        for k in range(1, CONV_WIDTH):
            du = du + _shift_up(ds_c, ds_n, k) * w_ref[CONV_WIDTH - 1 - k:CONV_WIDTH - k, :]
            acc_ref[CONV_WIDTH - 1 - k:CONV_WIDTH - k, :] += jnp.sum(ds_c * shifted[k - 1], axis=0, keepdims=True)
        acc_ref[CONV_WIDTH:CONV_WIDTH + 1, :] += jnp.sum(ds_c, axis=0, keepdims=True)
        du_ref[...] = du.astype(BF)

    nxt = lambda j, i: (jnp.minimum((i + 1) * hb, L // _HALO - 1), j)
    return _call(body, "conv_bwd", (CONV_DIM // tc, nblk),
                 [pl.BlockSpec((tm, tc), lambda j, i: (i, j)),
                  pl.BlockSpec((_HALO, tc), nxt),
                  pl.BlockSpec((tm, tc), lambda j, i: (i, xbc0 + j)),
                  pl.BlockSpec((_HALO, tc), lambda j, i: (jnp.maximum(i * hb - 1, 0), xbc0 + j)),
                  pl.BlockSpec((_HALO, tc), lambda j, i: (jnp.minimum((i + 1) * hb, L // _HALO - 1), xbc0 + j)),
                  pl.BlockSpec((CONV_WIDTH, tc), lambda j, i: (0, j)),
                  pl.BlockSpec((1, tc), lambda j, i: (0, j))],
                 (pl.BlockSpec((tm, tc), lambda j, i: (i, j)), pl.BlockSpec((8, tc), lambda j, i: (0, j))),
                 (_sds((L, CONV_DIM), BF), _sds((8, CONV_DIM), F32)))(dxc, dxc, zx, zx, zx, w, b)


_DT_COL = (D_INNER + CONV_DIM) // 128


def dt_prep(zx, bias_pad, alog_pad):
    L = zx.shape[0]

    def body(raw_ref, b_ref, al_ref, dt_ref, acs_ref):
        v = raw_ref[...] + b_ref[...]
        dt = jnp.maximum(v, 0.0) + jnp.log(1.0 + jnp.exp(-jnp.abs(v)))
        dt_ref[...] = dt
        acs = dt * (-jnp.exp(al_ref[...]))
        rows = lax.broadcasted_iota(jnp.int32, acs.shape, 0)
        s = 1
        while s < CHUNK:
            acs = acs + jnp.where(rows >= s, pltpu.roll(acs, s, 0), 0.0)
            s *= 2
        acs_ref[...] = acs

    blk = pl.BlockSpec((CHUNK, 128), lambda i: (i, 0))
    vec = pl.BlockSpec((1, 128), lambda i: (0, 0))
    return _call(body, "dt_prep", (L // CHUNK,),
                 [pl.BlockSpec((CHUNK, 128), lambda i: (i, _DT_COL)), vec, vec], (blk, blk),
                 (_sds((L, 128), F32), _sds((L, 128), F32)))(zx, bias_pad, alog_pad)


def dt_bwd(ddt, da, dt, zx, bias_pad, alog_pad):
    L = zx.shape[0]
    tm = _tile(L, 512)

    def body(ddt_ref, da_ref, dt_ref, raw_ref, b_ref, al_ref, o_ref, acc_ref):
        @pl.when(pl.program_id(0) == 0)
        def _():
            acc_ref[...] = jnp.zeros_like(acc_ref)

        A = -jnp.exp(al_ref[...])
        dav = da_ref[...]
        dd = ddt_ref[...] + dav * A
        draw = dd * _sigmoid(raw_ref[...] + b_ref[...])
        o_ref[...] = draw.astype(BF)
        acc_ref[0:1, :] += jnp.sum(draw, axis=0, keepdims=True)
        acc_ref[1:2, :] += jnp.sum(dav * dt_ref[...], axis=0, keepdims=True) * A

    blk = pl.BlockSpec((tm, 128), lambda i: (i, 0))
    vec = pl.BlockSpec((1, 128), lambda i: (0, 0))
    return _call(body, "dt_bwd", (L // tm,),
                 [blk, blk, blk, pl.BlockSpec((tm, 128), lambda i: (i, _DT_COL)), vec, vec],
                 (blk, pl.BlockSpec((8, 128), lambda i: (0, 0))),
                 (_sds((L, 128), BF), _sds((8, 128), F32)))(ddt, da, dt, zx, bias_pad, alog_pad)


_HPG = SSM_HEADS // SSM_GROUPS
_GW = _HPG * SSM_HEADDIM
_B_COL0 = D_INNER // SSM_STATE
_C_COL0 = (D_INNER + SSM_GROUPS * SSM_STATE) // SSM_STATE


def _ssd_head(x, dtc, ac, ar, r, causal):
    xh = x[:, SSM_HEADDIM * r:SSM_HEADDIM * (r + 1)]
    acol = ac[:, r:r + 1]
    arow = ar[r:r + 1, :]
    alast = ar[r:r + 1, CHUNK - 1:CHUNK]
    lm = jnp.exp(jnp.where(causal, acol - arow, NEG))
    return xh, xh * dtc[:, r:r + 1], acol, alast, lm


def ssd_fwd(xc, dt_g, acs_g, acsT_g, d_exp):
    L = xc.shape[0]
    nc = L // CHUNK

    def body(x_ref, b_ref, c_ref, dt_ref, ac_ref, ar_ref, d_ref, y_ref, pst_ref, st_ref):
        @pl.when(pl.program_id(1) == 0)
        def _():
            st_ref[...] = jnp.zeros_like(st_ref)

        x, Bm, Cm = x_ref[...], b_ref[...], c_ref[...]
        dtc, ac, ar = dt_ref[...], ac_ref[...], ar_ref[...]
        causal = lax.broadcasted_iota(jnp.int32, (CHUNK, CHUNK), 0) >= lax.broadcasted_iota(jnp.int32, (CHUNK, CHUNK), 1)
        CB = _dot_nt(Cm, Bm)
        for r in range(_HPG):
            xh, xd, acol, alast, lm = _ssd_head(x, dtc, ac, ar, r, causal)
            P = st_ref[r]
            y = _dot_nn(CB * lm, xd) + jnp.exp(acol) * _dot_nt(Cm, P)
            y_ref[:, SSM_HEADDIM * r:SSM_HEADDIM * (r + 1)] = y + d_ref[:, SSM_HEADDIM * r:SSM_HEADDIM * (r + 1)] * xh
            pst_ref[r] = P
            st_ref[r] = P * jnp.exp(alast) + _dot_tn(xd * jnp.exp(alast - acol), Bm)

    return _call(
        body, "ssd_fwd", (SSM_GROUPS, nc),
        [pl.BlockSpec((CHUNK, _GW), lambda g, c: (c, g)),
         pl.BlockSpec((CHUNK, SSM_STATE), lambda g, c: (c, _B_COL0 + g)),
         pl.BlockSpec((CHUNK, SSM_STATE), lambda g, c: (c, _C_COL0 + g)),
         pl.BlockSpec((None, CHUNK, _HPG), lambda g, c: (g, c, 0)),
         pl.BlockSpec((None, CHUNK, _HPG), lambda g, c: (g, c, 0)),
         pl.BlockSpec((None, _HPG, CHUNK), lambda g, c: (g, 0, c)),
         pl.BlockSpec((None, 1, _GW), lambda g, c: (g, 0, 0))],
        (pl.BlockSpec((CHUNK, _GW), lambda g, c: (c, g)),
         pl.BlockSpec((None, None, _HPG, SSM_HEADDIM, SSM_STATE), lambda g, c: (c, g, 0, 0, 0))),
        (_sds((L, D_INNER), F32), _sds((nc, SSM_GROUPS, _HPG, SSM_HEADDIM, SSM_STATE), F32)),
        scratch=[pltpu.VMEM((_HPG, SSM_HEADDIM, SSM_STATE), F32)],
    )(xc, xc, xc, dt_g, acs_g, acsT_g, d_exp)


def ssd_bwd(dy, xc, dt_g, acs_g, acsT_g, pst, d_exp):
    L = xc.shape[0]
    nc = L // CHUNK

    def body(dy_ref, x_ref, b_ref, c_ref, dt_ref, ac_ref, ar_ref, pst_ref, d_ref,
             dx_ref, db_ref, dc_ref, ddt_ref, da_ref, dd_ref, dp_ref):
        @pl.when(pl.program_id(1) == 0)
        def _():
            dp_ref[...] = jnp.zeros_like(dp_ref)
            dd_ref[...] = jnp.zeros_like(dd_ref)

        dyv, x, Bm, Cm = dy_ref[...], x_ref[...], b_ref[...], c_ref[...]
        dtc, ac, ar = dt_ref[...], ac_ref[...], ar_ref[...]
        ri = lax.broadcasted_iota(jnp.int32, (CHUNK, CHUNK), 0)
        ci = lax.broadcasted_iota(jnp.int32, (CHUNK, CHUNK), 1)
        causal = ri >= ci
        lane4 = lax.broadcasted_iota(jnp.int32, (CHUNK, _HPG), 1)
        CB = _dot_nt(Cm, Bm)
        dB = jnp.zeros((CHUNK, SSM_STATE), F32)
        dC = jnp.zeros((CHUNK, SSM_STATE), F32)
        dCB = jnp.zeros((CHUNK, CHUNK), F32)
        ddt_blk = jnp.zeros((CHUNK, _HPG), F32)
        da_blk = jnp.zeros((CHUNK, _HPG), F32)
        for r in range(_HPG):
            sl = slice(SSM_HEADDIM * r, SSM_HEADDIM * (r + 1))
            xh, xd, acol, alast, lm = _ssd_head(x, dtc, ac, ar, r, causal)
            dyh = dyv[:, sl]
            P = pst_ref[r]
            dPn = dp_ref[r]
            eA = jnp.exp(acol)
            cd = jnp.exp(alast)
            dte = jnp.exp(alast - acol)
            G = CB * lm
            dZ = eA * dyh
            dzp = _dot_nn(dZ, P)
            dC = dC + dzp
            dp_ref[r] = dPn * cd + _dot_tn(dZ, Cm)
            dA_col = jnp.sum(dzp * Cm, axis=1, keepdims=True)
            BdS = _dot_nt(Bm, dPn)
            dxd = dte * BdS
            dB = dB + dte * _dot_nn(xd, dPn)
            t = jnp.sum(xd * BdS, axis=1, keepdims=True) * dte
            dA_col = dA_col - t
            dA_last = jnp.sum(t, axis=0, keepdims=True) + jnp.sum(
                jnp.sum(dPn * P, axis=1, keepdims=True), axis=0, keepdims=True) * cd
            dG = _dot_nt(dyh, xd)
            dxd = dxd + _dot_tn(G, dyh)
            dCB = dCB + dG * lm
            W = dG * G
            dA_col = dA_col + jnp.sum(W, axis=1, keepdims=True)
            dA_row = jnp.sum(jnp.where(ri == ci, dA_col, 0.0), axis=0, keepdims=True) - jnp.sum(W, axis=0, keepdims=True)
            da_col = jnp.sum(jnp.where(ci >= ri, dA_row, 0.0), axis=1, keepdims=True) + dA_last
            da_blk = jnp.where(lane4 == r, da_col, da_blk)
            ddt_blk = jnp.where(lane4 == r, jnp.sum(dxd * xh, axis=1, keepdims=True), ddt_blk)
            dx_ref[:, sl] = dxd * dtc[:, r:r + 1] + d_ref[:, sl] * dyh
        dc_ref[...] = dC + _dot_nn(dCB, Bm)
        db_ref[...] = dB + _dot_tn(dCB, Cm)
        ddt_ref[...] = ddt_blk
        da_ref[...] = da_blk
        dd_ref[...] += jnp.sum(dyv * x, axis=0, keepdims=True)

    rc = lambda g, c: (nc - 1 - c, g)
    small = pl.BlockSpec((None, CHUNK, _HPG), lambda g, c: (g, nc - 1 - c, 0))
    return _call(
        body, "ssd_bwd", (SSM_GROUPS, nc),
        [pl.BlockSpec((CHUNK, _GW), rc),
         pl.BlockSpec((CHUNK, _GW), rc),
         pl.BlockSpec((CHUNK, SSM_STATE), lambda g, c: (nc - 1 - c, _B_COL0 + g)),
         pl.BlockSpec((CHUNK, SSM_STATE), lambda g, c: (nc - 1 - c, _C_COL0 + g)),
         small, small,
         pl.BlockSpec((None, _HPG, CHUNK), lambda g, c: (g, 0, nc - 1 - c)),
         pl.BlockSpec((None, None, _HPG, SSM_HEADDIM, SSM_STATE), lambda g, c: (nc - 1 - c, g, 0, 0, 0)),
         pl.BlockSpec((None, 1, _GW), lambda g, c: (g, 0, 0))],
        (pl.BlockSpec((CHUNK, _GW), rc),
         pl.BlockSpec((CHUNK, SSM_STATE), rc),
         pl.BlockSpec((CHUNK, SSM_STATE), rc),
         small, small,
         pl.BlockSpec((None, 1, _GW), lambda g, c: (g, 0, 0))),
        (_sds((L, D_INNER), F32), _sds((L, SSM_GROUPS * SSM_STATE), F32), _sds((L, SSM_GROUPS * SSM_STATE), F32),
         _sds((SSM_GROUPS, L, _HPG), F32), _sds((SSM_GROUPS, L, _HPG), F32), _sds((SSM_GROUPS, 1, _GW), F32)),
        scratch=[pltpu.VMEM((_HPG, SSM_HEADDIM, SSM_STATE), F32)],
    )(dy, xc, xc, xc, dt_g, acs_g, acsT_g, pst, d_exp)


_NGW = D_INNER // SSM_GROUPS


def gate_norm(y, zx, nw):
    L = y.shape[0]
    tm = _tile(L, 256)

    def body(y_ref, z_ref, nw_ref, o_ref):
        for g in range(SSM_GROUPS):
            sl = slice(_NGW * g, _NGW * (g + 1))
            z = z_ref[:, sl]
            y2 = y_ref[:, sl] * (z * _sigmoid(z))
            r = lax.rsqrt(jnp.mean(y2 * y2, axis=-1, keepdims=True) + EPS)
            o_ref[:, sl] = (y2 * r * nw_ref[:, sl]).astype(BF)

    row = pl.BlockSpec((tm, D_INNER), lambda i: (i, 0))
    return _call(body, "gate_norm", (L // tm,), [row, row, pl.BlockSpec((1, D_INNER), lambda i: (0, 0))],
                 row, _sds((L, D_INNER), BF))(y, zx, nw)


def gate_norm_bwd(dyn, y, zx, nw):
    L = y.shape[0]
    tm = _tile(L, 256)

    def body(d_ref, y_ref, z_ref, nw_ref, dy_ref, dz_ref, acc_ref):
        @pl.when(pl.program_id(0) == 0)
        def _():
            acc_ref[...] = jnp.zeros_like(acc_ref)

        for g in range(SSM_GROUPS):
            sl = slice(_NGW * g, _NGW * (g + 1))
            z = z_ref[:, sl]
            yv = y_ref[:, sl]
            sg = _sigmoid(z)
            sz = z * sg
            y2 = yv * sz
            r = lax.rsqrt(jnp.mean(y2 * y2, axis=-1, keepdims=True) + EPS)
            yh = y2 * r
            d = d_ref[:, sl]
            dn = d * nw_ref[:, sl]
            dy2 = r * (dn - yh * jnp.mean(dn * yh, axis=-1, keepdims=True))
            dy_ref[:, sl] = dy2 * sz
            dz_ref[:, sl] = (dy2 * yv * sg * (1.0 + z * (1.0 - sg))).astype(BF)
            acc_ref[0:1, sl] += jnp.sum(d * yh, axis=0, keepdims=True)

    row = pl.BlockSpec((tm, D_INNER), lambda i: (i, 0))
    return _call(body, "gate_norm_bwd", (L // tm,), [row, row, row, pl.BlockSpec((1, D_INNER), lambda i: (0, 0))],
                 (row, row, pl.BlockSpec((8, D_INNER), lambda i: (0, 0))),
                 (_sds((L, D_INNER), F32), _sds((L, D_INNER), BF), _sds((8, D_INNER), F32)))(dyn, y, zx, nw)


_SCALE = HEAD_DIM ** -0.5
_REP = ATT_HEADS // KV_HEADS
_V_OFF = KV_HEADS * HEAD_DIM


def _stack_heads(ref, k):
    return jnp.concatenate([ref[:, HEAD_DIM * (k * _REP + r):HEAD_DIM * (k * _REP + r + 1)] for r in range(_REP)],
                           axis=0)


def _stack_sinks(s_ref, k):
    return jnp.concatenate([jnp.broadcast_to(s_ref[:, k * _REP + r:k * _REP + r + 1], (WINDOW, 1))
                            for r in range(_REP)], axis=0)


def _attn_probs(q4, kp, kc, sink, first):
    shape = (_REP * WINDOW, WINDOW)
    rows = jnp.bitwise_and(lax.broadcasted_iota(jnp.int32, shape, 0), WINDOW - 1)
    cols = lax.broadcasted_iota(jnp.int32, shape, 1)
    sp = jnp.where(jnp.logical_and(cols > rows, jnp.logical_not(first)), _dot_nt(q4, kp) * _SCALE, NEG)
    sc = jnp.where(cols <= rows, _dot_nt(q4, kc) * _SCALE, NEG)
    m = jnp.maximum(jnp.maximum(jnp.max(sp, axis=1, keepdims=True), jnp.max(sc, axis=1, keepdims=True)), sink)
    pp = jnp.exp(sp - m)
    pc = jnp.exp(sc - m)
    ps = jnp.exp(sink - m)
    inv = 1.0 / (jnp.sum(pp, axis=1, keepdims=True) + jnp.sum(pc, axis=1, keepdims=True) + ps)
    return pp * inv, pc * inv, ps * inv


def attn_fwd(q, kv, sinks_pad):
    L = q.shape[0]
    nb = L // WINDOW

    def body(q_ref, kc_ref, kp_ref, s_ref, o_ref):
        first = pl.program_id(0) == 0
        for k in range(KV_HEADS):
            ks = slice(HEAD_DIM * k, HEAD_DIM * (k + 1))
            vs = slice(_V_OFF + HEAD_DIM * k, _V_OFF + HEAD_DIM * (k + 1))
            pp, pc, _ = _attn_probs(_stack_heads(q_ref, k), kp_ref[:, ks], kc_ref[:, ks], _stack_sinks(s_ref, k), first)
            o4 = _dot_nn(pp, kp_ref[:, vs]) + _dot_nn(pc, kc_ref[:, vs])
            for r in range(_REP):
                h = k * _REP + r
                o_ref[:, HEAD_DIM * h:HEAD_DIM * (h + 1)] = o4[WINDOW * r:WINDOW * (r + 1)].astype(BF)

    qspec = pl.BlockSpec((WINDOW, D_MODEL), lambda i: (i, 0))
    return _call(body, "attn_fwd", (nb,),
                 [qspec, pl.BlockSpec((WINDOW, KV_DIM), lambda i: (i, 0)),
                  pl.BlockSpec((WINDOW, KV_DIM), lambda i: (jnp.maximum(i - 1, 0), 0)),
                  pl.BlockSpec((1, 128), lambda i: (0, 0))],
                 qspec, _sds((L, D_MODEL), BF))(q, kv, kv, sinks_pad)


def attn_bwd(q, kv, do, sinks_pad):
    L = q.shape[0]
    nb = L // WINDOW

    def body(q_ref, kc_ref, kp_ref, do_ref, s_ref, dq_ref, dc_ref, dp_ref, acc_ref):
        first = pl.program_id(0) == 0

        @pl.when(first)
        def _():
            acc_ref[...] = jnp.zeros_like(acc_ref)

        lane = lax.broadcasted_iota(jnp.int32, (1, 128), 1)
        dsink = jnp.zeros((1, 128), F32)
        for k in range(KV_HEADS):
            ks = slice(HEAD_DIM * k, HEAD_DIM * (k + 1))
            vs = slice(_V_OFF + HEAD_DIM * k, _V_OFF + HEAD_DIM * (k + 1))
            kp, kc, vp, vc = kp_ref[:, ks], kc_ref[:, ks], kp_ref[:, vs], kc_ref[:, vs]
            q4 = _stack_heads(q_ref, k)
            do4 = _stack_heads(do_ref, k)
            pp, pc, ps = _attn_probs(q4, kp, kc, _stack_sinks(s_ref, k), first)
            dpp = _dot_nt(do4, vp)
            dpc = _dot_nt(do4, vc)
            delta = jnp.sum(pp * dpp, axis=1, keepdims=True) + jnp.sum(pc * dpc, axis=1, keepdims=True)
            dsp = pp * (dpp - delta) * _SCALE
            dsc = pc * (dpc - delta) * _SCALE
            dq4 = _dot_nn(dsp, kp) + _dot_nn(dsc, kc)
            psd = ps * delta
            for r in range(_REP):
                h = k * _REP + r
                rs = slice(WINDOW * r, WINDOW * (r + 1))
                dq_ref[:, HEAD_DIM * h:HEAD_DIM * (h + 1)] = dq4[rs]
                dsink = dsink + jnp.where(lane == h, -jnp.sum(psd[rs], axis=0, keepdims=True), 0.0)
            dp_ref[:, ks] = _dot_tn(dsp, q4)
            dc_ref[:, ks] = _dot_tn(dsc, q4)
            dp_ref[:, vs] = _dot_tn(pp, do4)
            dc_ref[:, vs] = _dot_tn(pc, do4)
        acc_ref[0:1, :] += jnp.sum(dq_ref[...], axis=0, keepdims=True)
        acc_ref[1:2, 0:128] += dsink

    qspec = pl.BlockSpec((WINDOW, D_MODEL), lambda i: (i, 0))
    kspec = pl.BlockSpec((WINDOW, KV_DIM), lambda i: (i, 0))
    return _call(body, "attn_bwd", (nb,),
                 [qspec, kspec, pl.BlockSpec((WINDOW, KV_DIM), lambda i: (jnp.maximum(i - 1, 0), 0)), qspec,
                  pl.BlockSpec((1, 128), lambda i: (0, 0))],
                 (qspec, kspec, kspec, pl.BlockSpec((8, D_MODEL), lambda i: (0, 0))),
                 (_sds((L, D_MODEL), F32), _sds((L, KV_DIM), F32), _sds((L, KV_DIM), F32), _sds((8, D_MODEL), F32)),
                 )(q, kv, kv, do, sinks_pad)


def kv_grad_combine(parts):
    L = parts[0][0].shape[0]
    nb = L // WINDOW
    n = len(parts)

    def body(*refs):
        i = pl.program_id(0)
        o_ref, acc_ref = refs[2 * n], refs[2 * n + 1]

        @pl.when(i == 0)
        def _():
            acc_ref[...] = jnp.zeros_like(acc_ref)

        tot = refs[0][...]
        nxt = refs[1][...]
        for a in range(1, n):
            tot = tot + refs[2 * a][...]
            nxt = nxt + refs[2 * a + 1][...]
        tot = tot + jnp.where(i < nb - 1, nxt, 0.0)
        o_ref[...] = tot
        acc_ref[0:1, :] += jnp.sum(tot, axis=0, keepdims=True)

    cur = pl.BlockSpec((WINDOW, KV_DIM), lambda i: (i, 0))
    nxt = pl.BlockSpec((WINDOW, KV_DIM), lambda i: (jnp.minimum(i + 1, nb - 1), 0))
    args = [t for p in parts for t in p]
    return _call(body, "kv_grad_combine", (nb,), [cur, nxt] * n,
                 (cur, pl.BlockSpec((8, KV_DIM), lambda i: (0, 0))),
                 (_sds((L, KV_DIM), F32), _sds((8, KV_DIM), F32)))(*args)


def mod_fwd(c_all, w, b, name):
    n, _, C = w.shape

    def body(c_ref, w_ref, b_ref, o_ref, ca_ref):
        cv = c_ref[...]
        ca = cv * _sigmoid(cv)
        ca_ref[...] = ca
        o_ref[...] = _dot(ca, w_ref[...], ((1,), (0,))) + b_ref[...]

    return _call(body, name, (n,),
                 [pl.BlockSpec((N_DEV, D_MODEL), lambda i: (0, 0)),
                  pl.BlockSpec((None, D_MODEL, C), lambda i: (i, 0, 0)),
                  pl.BlockSpec((None, 1, C), lambda i: (i, 0, 0))],
                 (pl.BlockSpec((None, N_DEV, C), lambda i: (i, 0, 0)), pl.BlockSpec((N_DEV, D_MODEL), lambda i: (0, 0))),
                 (_sds((n, N_DEV, C), F32), _sds((N_DEV, D_MODEL), F32)))(c_all, w, b)


def mod_wgrad(c_act_t, dmod, name):
    n, _, C = dmod.shape
    tr = 256

    def body(ct_ref, d_ref, o_ref):
        acc = ct_ref[:, 0:1] * d_ref[0:1, :]
        for bidx in range(1, N_DEV):
            acc = acc + ct_ref[:, bidx:bidx + 1] * d_ref[bidx:bidx + 1, :]
        o_ref[...] = acc

    return _call(body, name, (n, D_MODEL // tr),
                 [pl.BlockSpec((tr, N_DEV), lambda i, j: (j, 0)),
                  pl.BlockSpec((None, N_DEV, C), lambda i, j: (i, 0, 0))],
                 pl.BlockSpec((None, tr, C), lambda i, j: (i, j, 0)), _sds((n, D_MODEL, C), F32))(c_act_t, dmod)


def _my_pos():
    return lax.axis_index("x"), lax.axis_index("y"), lax.axis_index("c")


def small_all_gather(v):
    m_per, n = v.shape

    def body(x_ref, out_ref, send_sems, recv_sems, local_sem):
        x, y, c = _my_pos()
        me, sibling = (x, y, c), (x, y, 1 - c)
        chips = [(1 - x, y), (x, 1 - y), (1 - x, 1 - y)]

        def rows(px, py, pc):
            return out_ref.at[pl.ds((4 * px + 2 * py + pc) * m_per, m_per), :]

        def copy(k, block, to, src=None):
            return pltpu.make_async_remote_copy(
                src_ref=rows(*block) if src is None else src, dst_ref=rows(*block),
                send_sem=send_sems.at[k], recv_sem=recv_sems.at[k], device_id=to, device_id_type=MESH)

        mine = pltpu.make_async_copy(x_ref, rows(*me), local_sem)
        mine.start()
        first = [copy(0, me, sibling, src=x_ref)]
        first += [copy(1 + j, me, (*chip, c), src=x_ref) for j, chip in enumerate(chips)]
        for cp in first:
            cp.start()
        passed = [copy(4 + j, (*chip, c), sibling) for j, chip in enumerate(chips)]
        for j, chip in enumerate(chips):
            copy(1 + j, (*chip, c), me).wait_recv()
            passed[j].start()
        copy(0, sibling, me).wait_recv()
        for j, chip in enumerate(chips):
            copy(4 + j, (*chip, 1 - c), me).wait_recv()
        for cp in first + passed:
            cp.wait_send()
        mine.wait()

    return pl.pallas_call(
        body, name="small_all_gather",
        out_shape=_sds((N_DEV * m_per, n), v.dtype),
        in_specs=[pl.BlockSpec(memory_space=pltpu.VMEM)],
        out_specs=pl.BlockSpec(memory_space=pltpu.VMEM),
        scratch_shapes=[pltpu.SemaphoreType.DMA((7,)), pltpu.SemaphoreType.DMA((7,)), pltpu.SemaphoreType.DMA],
        compiler_params=pltpu.CompilerParams(vmem_limit_bytes=VMEM_LIMIT),
    )(v)


def big_all_gather(arrs):
    n = len(arrs)

    def body(*refs):
        ins, outs = refs[:n], refs[n:2 * n]
        send_sems, recv_sems, local_sems = refs[2 * n], refs[2 * n + 1], refs[2 * n + 2]
        x, y, c = _my_pos()
        me, sibling = (x, y, c), (x, y, 1 - c)
        chips = [(1 - x, y), (x, 1 - y), (1 - x, 1 - y)]

        def slot(a, px, py, pc):
            return outs[a].at[4 * px + 2 * py + pc]

        def copy(a, k, block, to, src=None):
            return pltpu.make_async_remote_copy(
                src_ref=slot(a, *block) if src is None else src, dst_ref=slot(a, *block),
                send_sem=send_sems.at[7 * a + k], recv_sem=recv_sems.at[7 * a + k], device_id=to, device_id_type=MESH)

        mine = [pltpu.make_async_copy(ins[a], slot(a, *me), local_sems.at[a]) for a in range(n)]
        for cp in mine:
            cp.start()
        first = []
        for a in range(n):
            first.append(copy(a, 0, me, sibling, src=ins[a]))
            first += [copy(a, 1 + j, me, (*chip, c), src=ins[a]) for j, chip in enumerate(chips)]
        for cp in first:
            cp.start()
        passed = []
        for a in range(n):
            for j, chip in enumerate(chips):
                copy(a, 1 + j, (*chip, c), me).wait_recv()
                fwd = copy(a, 4 + j, (*chip, c), sibling)
                fwd.start()
                passed.append(fwd)
        for a in range(n):
            copy(a, 0, sibling, me).wait_recv()
            for j, chip in enumerate(chips):
                copy(a, 4 + j, (*chip, 1 - c), me).wait_recv()
        for cp in first + passed:
            cp.wait_send()
        for cp in mine:
            cp.wait()

    hbm = pl.BlockSpec(memory_space=pltpu.HBM)
    return pl.pallas_call(
        body, name="big_all_gather",
        out_shape=[_sds((N_DEV,) + a.shape, a.dtype) for a in arrs],
        in_specs=[hbm] * n, out_specs=[hbm] * n,
        scratch_shapes=[pltpu.SemaphoreType.DMA((7 * n,)), pltpu.SemaphoreType.DMA((7 * n,)),
                        pltpu.SemaphoreType.DMA((n,))],
    )(*arrs)


_FLIPS =[(fx, fy, fc) for fx in (0, 1) for fy in (0, 1) for fc in (0, 1)][1:]
_HBM = pl.BlockSpec(memory_space=pltpu.HBM)
_SEM = pl.BlockSpec(memory_space=pltpu.SEMAPHORE)
_EFFECT = pltpu.SideEffectType.DATAFLOW_SIDE_EFFECTING


def _flip(x, y, c, f):
    return (1 - x if f[0] else x), (1 - y if f[1] else y), (1 - c if f[2] else c)


def _xfer_copies(srcs, lands, send_sems, recv_sems, scatter):
    x, y, c = _my_pos()
    me = 4 * x + 2 * y + c
    copies = []
    for a in range(len(srcs)):
        for k, f in enumerate(_FLIPS):
            px, py, pc = _flip(x, y, c, f)
            src = srcs[a].at[4 * px + 2 * py + pc] if scatter else srcs[a]
            copies.append(pltpu.make_async_remote_copy(
                src_ref=src, dst_ref=lands[a].at[me], send_sem=send_sems.at[7 * a + k],
                recv_sem=recv_sems.at[7 * a + k], device_id=(px, py, pc), device_id_type=MESH))
    return copies


def _own_copies(srcs, lands, local_sems, scatter):
    x, y, c = _my_pos()
    me = 4 * x + 2 * y + c
    return [pltpu.make_async_copy(srcs[a].at[me] if scatter else srcs[a], lands[a].at[me], local_sems.at[a])
            for a in range(len(srcs))]


def xfer_start(arrs, scatter, after, name):
    n = len(arrs)
    land_shapes = [a.shape if scatter else (N_DEV,) + a.shape for a in arrs]

    def body(*refs):
        srcs, lands = refs[:n], refs[n:2 * n]
        send_sems, recv_sems, local_sems = refs[2 * n + 1], refs[2 * n + 2], refs[2 * n + 3]
        token = refs[-1]
        for cp in _xfer_copies(srcs, lands, send_sems, recv_sems, scatter):
            cp.start()
        for cp in _own_copies(srcs, lands, local_sems, scatter):
            cp.start()
        token[...] = jnp.zeros_like(token)

    out = pl.pallas_call(
        body, name=name,
        out_shape=(pltpu.SemaphoreType.DMA((7 * n,)), pltpu.SemaphoreType.DMA((7 * n,)),
                   pltpu.SemaphoreType.DMA((n,)),
                   *[pltpu.HBM(a.shape, a.dtype) for a in arrs],
                   *[pltpu.HBM(s, a.dtype) for s, a in zip(land_shapes, arrs)],
                   _sds((8, 128), F32)),
        in_specs=[_HBM] * (2 * n) + [pl.BlockSpec(memory_space=pl.ANY)],
        out_specs=(_SEM, _SEM, _SEM, *([_HBM] * (2 * n)), pl.BlockSpec(memory_space=pltpu.VMEM)),
        input_output_aliases={i: 3 + i for i in range(2 * n)},
        compiler_params=pltpu.CompilerParams(has_side_effects=_EFFECT),
    )(*[pltpu.with_memory_space_constraint(a, pltpu.HBM) for a in arrs],
      *[pltpu.with_memory_space_constraint(lax.empty(s, a.dtype), pltpu.HBM) for s, a in zip(land_shapes, arrs)],
      after)
    return (out[0], out[1], out[2], list(out[3:3 + n]), list(out[3 + n:3 + 2 * n]), scatter), out[-1]


def xfer_wait(handle, after, name):
    send_sems, recv_sems, local_sems, srcs, lands, scatter = handle
    n = len(srcs)

    def body(*refs):
        srcs_r, lands_r = refs[:n], refs[n:2 * n]
        ssem, rsem, lsem = refs[2 * n], refs[2 * n + 1], refs[2 * n + 2]
        for cp in _xfer_copies(srcs_r, lands_r, ssem, rsem, scatter):
            cp.wait_send()
            cp.wait_recv()
        for cp in _own_copies(srcs_r, lands_r, lsem, scatter):
            cp.wait()

    out = pl.pallas_call(
        body, name=name,
        out_shape=(*[pltpu.HBM(a.shape, a.dtype) for a in srcs], *[pltpu.HBM(a.shape, a.dtype) for a in lands]),
        in_specs=[_HBM] * (2 * n) + [_SEM, _SEM, _SEM, pl.BlockSpec(memory_space=pl.ANY)],
        out_specs=tuple([_HBM] * (2 * n)),
        input_output_aliases={i: i for i in range(2 * n)},
        compiler_params=pltpu.CompilerParams(has_side_effects=_EFFECT),
    )(*srcs, *lands, send_sems, recv_sems, local_sems, after)
    return list(out[n:])


def _g2_first(srcs, lands, send_sems, recv_sems):
    x, y, c = _my_pos()
    me = 4 * x + 2 * y + c
    peers = [(x, y, 1 - c), (1 - x, y, c), (x, 1 - y, c), (1 - x, 1 - y, c)]
    return [[pltpu.make_async_remote_copy(
        src_ref=srcs[a], dst_ref=lands[a].at[me], send_sem=send_sems.at[4 * a + k], recv_sem=recv_sems.at[4 * a + k],
        device_id=p, device_id_type=MESH) for k, p in enumerate(peers)] for a in range(len(srcs))]


def _g2_relay(lands, send_sems, recv_sems):
    x, y, c = _my_pos()
    chips = [(1 - x, y), (x, 1 - y), (1 - x, 1 - y)]
    out = []
    for a in range(len(lands)):
        row = []
        for j, (px, py) in enumerate(chips):
            slab = lands[a].at[4 * px + 2 * py + c]
            row.append(pltpu.make_async_remote_copy(
                src_ref=slab, dst_ref=slab, send_sem=send_sems.at[3 * a + j], recv_sem=recv_sems.at[3 * a + j],
                device_id=(x, y, 1 - c), device_id_type=MESH))
        out.append(row)
    return out


def gather2_start(arrs, after, name):
    n = len(arrs)

    def body(*refs):
        srcs, lands = refs[:n], refs[n:2 * n]
        send_sems, recv_sems, local_sems = refs[2 * n + 1], refs[2 * n + 2], refs[2 * n + 3]
        for row in _g2_first(srcs, lands, send_sems, recv_sems):
            for cp in row:
                cp.start()
        for cp in _own_copies(srcs, lands, local_sems, False):
            cp.start()
        refs[-1][...] = jnp.zeros_like(refs[-1])

    lands0 = [lax.empty((N_DEV,) + a.shape, a.dtype) for a in arrs]
    out = pl.pallas_call(
        body, name=name,
        out_shape=(pltpu.SemaphoreType.DMA((4 * n,)), pltpu.SemaphoreType.DMA((4 * n,)), pltpu.SemaphoreType.DMA((n,)),
                   *[pltpu.HBM(a.shape, a.dtype) for a in arrs], *[pltpu.HBM(l.shape, l.dtype) for l in lands0],
                   _sds((8, 128), F32)),
        in_specs=[_HBM] * (2 * n) + [pl.BlockSpec(memory_space=pl.ANY)],
        out_specs=(_SEM, _SEM, _SEM, *([_HBM] * (2 * n)), pl.BlockSpec(memory_space=pltpu.VMEM)),
        input_output_aliases={i: 3 + i for i in range(2 * n)},
        compiler_params=pltpu.CompilerParams(has_side_effects=_EFFECT),
    )(*[pltpu.with_memory_space_constraint(a, pltpu.HBM) for a in arrs],
      *[pltpu.with_memory_space_constraint(l, pltpu.HBM) for l in lands0], after)
    return dict(send1=out[0], recv1=out[1], local=out[2], srcs=list(out[3:3 + n]), lands=list(out[3 + n:3 + 2 * n])), out[-1]


def gather2_relay(handle, after, name):
    n = len(handle["lands"])

    def body(*refs):
        lands = refs[:n]
        send1, recv1 = refs[n], refs[n + 1]
        send2, recv2 = refs[n + 3], refs[n + 4]
        firsts = _g2_first([l.at[0] for l in lands], lands, send1, recv1)
        relays = _g2_relay(lands, send2, recv2)
        for a in range(n):
            for j in range(3):
                firsts[a][1 + j].wait_recv()
                relays[a][j].start()
        refs[-1][...] = jnp.zeros_like(refs[-1])

    out = pl.pallas_call(
        body, name=name,
        out_shape=(pltpu.SemaphoreType.DMA((3 * n,)), pltpu.SemaphoreType.DMA((3 * n,)),
                   *[pltpu.HBM(l.shape, l.dtype) for l in handle["lands"]], _sds((8, 128), F32)),
        in_specs=[_HBM] * n + [_SEM, _SEM, pl.BlockSpec(memory_space=pl.ANY)],
        out_specs=(_SEM, _SEM, *([_HBM] * n), pl.BlockSpec(memory_space=pltpu.VMEM)),
        input_output_aliases={i: 2 + i for i in range(n)},
        compiler_params=pltpu.CompilerParams(has_side_effects=_EFFECT),
    )(*handle["lands"], handle["send1"], handle["recv1"], after)
    new = dict(handle)
    new.update(send2=out[0], recv2=out[1], lands=list(out[2:2 + n]))
    return new, out[-1]


def gather2_wait(handle, after, name):
    n = len(handle["lands"])

    def body(*refs):
        srcs, lands = refs[:n], refs[n:2 * n]
        send1, recv1, local, send2, recv2 = refs[2 * n:2 * n + 5]
        for a, row in enumerate(_g2_first(srcs, lands, send1, recv1)):
            row[0].wait_recv()
            for cp in row:
                cp.wait_send()
        for row in _g2_relay(lands, send2, recv2):
            for cp in row:
                cp.wait_send()
                cp.wait_recv()
        for cp in _own_copies(srcs, lands, local, False):
            cp.wait()

    out = pl.pallas_call(
        body, name=name,
        out_shape=(*[pltpu.HBM(a.shape, a.dtype) for a in handle["srcs"]],
                   *[pltpu.HBM(l.shape, l.dtype) for l in handle["lands"]]),
        in_specs=[_HBM] * (2 * n) + [_SEM] * 5 + [pl.BlockSpec(memory_space=pl.ANY)],
        out_specs=tuple([_HBM] * (2 * n)),
        input_output_aliases={i: i for i in range(2 * n)},
        compiler_params=pltpu.CompilerParams(has_side_effects=_EFFECT),
    )(*handle["srcs"], *handle["lands"], handle["send1"], handle["recv1"], handle["local"], handle["send2"],
      handle["recv2"], after)
    return list(out[n:])


def adamw(parts, w, m, v, name, row0=0, prev=None):
    r_tot, C = w.shape
    n_parts, R = parts.shape[0], parts.shape[1]
    row_bytes = 2 * (n_parts * C * parts.dtype.itemsize + 7 * C * 4)
    tr = R
    for cand in (512, 352, 256, 176, 128, 64):
        if R % cand == 0 and row0 % cand == 0 and R > cand and cand * row_bytes <= ADAMW_VMEM_BUDGET:
            tr = cand
            break
    tc = C
    if tr == R and R * row_bytes > ADAMW_VMEM_BUDGET:
        assert row0 == 0 and R == r_tot
        tc = next(t for t in (512, 256, 128) if C % t == 0 and R * row_bytes * t // C <= ADAMW_VMEM_BUDGET)
    assert row0 % tr == 0 and (tr % 8 == 0 or (tr == r_tot and row0 == 0))
    blk0 = row0 // tr
    c1 = 1.0 / (1.0 - ADAM_B1 ** ADAM_STEP)
    c2 = 1.0 / (1.0 - ADAM_B2 ** ADAM_STEP)

    def body(p_ref, w_ref, m_ref, v_ref, *rest):
        g_ref, d_ref, nm_ref, nv_ref = rest[-4:]
        g = p_ref[0].astype(F32)
        for k in range(1, n_parts):
            g = g + p_ref[k].astype(F32)
        nm = ADAM_B1 * m_ref[...] + (1.0 - ADAM_B1) * g
        nv = ADAM_B2 * v_ref[...] + (1.0 - ADAM_B2) * (g * g)
        g_ref[...] = g
        nm_ref[...] = nm
        nv_ref[...] = nv
        d_ref[...] = -ADAM_LR * ((nm * c1) / (jnp.sqrt(nv * c2) + ADAM_EPS) + ADAM_WD * w_ref[...])

    if tc == C:
        grid = (R // tr,)
        blk = pl.BlockSpec((tr, C), lambda i: (i + blk0, 0))
        p_spec = pl.BlockSpec((n_parts, tr, C), lambda i: (0, i, 0))
    else:
        grid = (C // tc,)
        blk = pl.BlockSpec((R, tc), lambda i: (0, i))
        p_spec = pl.BlockSpec((n_parts, R, tc), lambda i: (0, 0, i))
    in_specs = [p_spec, blk, blk, blk]
    args = [parts, w, m, v]
    aliases = {}
    if prev is not None:
        in_specs += [pl.BlockSpec(memory_space=pl.ANY)] * 4
        args += list(prev)
        aliases = {4 + k: k for k in range(4)}
    return pl.pallas_call(
        body, name=name, grid=grid, in_specs=in_specs, out_specs=(blk, blk, blk, blk),
        out_shape=tuple(_sds((r_tot, C), F32) for _ in range(4)), input_output_aliases=aliases,
        compiler_params=pltpu.CompilerParams(vmem_limit_bytes=VMEM_LIMIT))(*args)


def _ffn_fwd(x, nw, sh, sc, g, wt_gu, w_dn):
    h = norm_mod(x, nw, sh, sc, "ffn_norm")
    gp, up, a = ffn_up(h, wt_gu)
    y, xn = matmul(a, w_dn, "nn", BF, "ffn_down", res=x, gate=g, coef=0.5)
    return xn, (x, h, gp, up, a, y)


def _ffn_bwd(dxo, saved, nw, sc, g, wt_gu, w_dn):
    x, h, gp, up, a, y = saved
    dy, acc1 = resid_gate_bwd(dxo, y, g, 0.5, "ffn_gate_bwd")
    d_wdn = matmul(a, dy, "tn", BF, "ffn_down_wgrad")
    dg, du = ffn_down_dgrad(dy, w_dn, gp, up)
    d_wt = ffn_up_wgrad(dg, du, h)
    dh = ffn_up_dgrad(dg, du, wt_gu)
    dx, acc2 = norm_mod_bwd(x, dh, dxo, nw, sc, "ffn_norm_bwd")
    return dx, d_wt, d_wdn, (acc2[0], acc2[1], acc1[0]), acc2[2]


def _group_layout(a):
    L = a.shape[0]
    return a[:, :SSM_HEADS].reshape(L, SSM_GROUPS, _HPG).transpose(1, 0, 2)


def _ungroup_layout(a):
    L = a.shape[1]
    return jnp.pad(a.transpose(1, 0, 2).reshape(L, SSM_HEADS), ((0, 0), (0, 128 - SSM_HEADS)))


def _pad_row(vec, n=128):
    return jnp.pad(vec.reshape(1, -1), ((0, 0), (0, n - vec.shape[-1])))


def _mamba_fwd(x, nw, sh, sc, g, p):
    h = norm_mod(x, nw, sh, sc, "mix_norm")
    zx = matmul(h, p["w_in_t"], "nt", F32, "ssm_in")
    xc = conv_fwd(zx, p["conv_w"], p["conv_b"])
    dt, acs = dt_prep(zx, p["dt_bias"], p["a_log"])
    dt_g, acs_g = _group_layout(dt), _group_layout(acs)
    acs_t = acs_g.transpose(0, 2, 1)
    y, pst = ssd_fwd(xc, dt_g, acs_g, acs_t, p["d_exp"])
    yn = gate_norm(y, zx, p["norm_w"])
    yo, xn = matmul(yn, p["w_out"], "nn", BF, "ssm_out", res=x, gate=g, coef=1.0)
    return xn, (x, h, zx, xc, dt, dt_g, acs_g, acs_t, y, pst, yn, yo)


def _mamba_bwd(dxo, saved, nw, sc, g, p):
    x, h, zx, xc, dt, dt_g, acs_g, acs_t, y, pst, yn, yo = saved
    dyo, acc1 = resid_gate_bwd(dxo, yo, g, 1.0, "mix_gate_bwd")
    d_wout = matmul(yn, dyo, "tn", BF, "ssm_out_wgrad")
    dyn = matmul(dyo, p["w_out"], "nt", F32, "ssm_out_dgrad")
    dy, dz, accn = gate_norm_bwd(dyn, y, zx, p["norm_w"])
    dxs, dB, dC, ddt_g, da_g, dd = ssd_bwd(dy, xc, dt_g, acs_g, acs_t, pst, p["d_exp"])
    dxc = jnp.concatenate([dxs, dB, dC], axis=1)
    du, accc = conv_bwd(dxc, zx, p["conv_w"], p["conv_b"])
    draw, accdt = dt_bwd(_ungroup_layout(ddt_g), _ungroup_layout(da_g), dt, zx, p["dt_bias"], p["a_log"])
    dzx = jnp.concatenate([dz, du, draw], axis=1)
    d_win = matmul(dzx, h, "tn", BF, "ssm_in_wgrad")[:IN_PROJ]
    dh = matmul(dzx, p["w_in_t"], "nn", F32, "ssm_in_dgrad")
    dx, acc2 = norm_mod_bwd(x, dh, dxo, nw, sc, "mix_norm_bwd")
    small = dict(conv_w=accc[:CONV_WIDTH], conv_b=accc[CONV_WIDTH], dt_bias=accdt[0, :SSM_HEADS],
                 a_log=accdt[1, :SSM_HEADS], d=dd.reshape(SSM_HEADS, SSM_HEADDIM).sum(-1), norm_w=accn[0])
    return dx, d_win, d_wout, (acc2[0], acc2[1], acc1[0]), acc2[2], small


def _attn_layer_fwd(x, nw, sh, sc, g, p, kv):
    h = norm_mod(x, nw, sh, sc, "mix_norm")
    q = matmul(h, p["w_q"], "nn", F32, "attn_q", bias=p["b_q"])
    o = attn_fwd(q, kv, p["sinks"])
    yo, xn = matmul(o, p["w_o"], "nn", BF, "attn_o", bias=p["b_o"], res=x, gate=g, coef=1.0)
    return xn, (x, h, q, o, yo)


def _attn_layer_bwd(dxo, saved, nw, sc, g, p, kv):
    x, h, q, o, yo = saved
    dyo, acc1 = resid_gate_bwd(dxo, yo, g, 1.0, "mix_gate_bwd")
    d_wo = matmul(o, dyo, "tn", BF, "attn_o_wgrad")
    do = matmul(dyo, p["w_o"], "nt", F32, "attn_o_dgrad")
    dq, dkv_c, dkv_p, acca = attn_bwd(q, kv, do, p["sinks"])
    d_wq = matmul(h, dq, "tn", BF, "attn_q_wgrad")
    dh = matmul(dq, p["w_q"], "nt", F32, "attn_q_dgrad")
    dx, acc2 = norm_mod_bwd(x, dh, dxo, nw, sc, "mix_norm_bwd")
    small = dict(b_q=acca[0], sinks=acca[1, :ATT_HEADS], b_o=acc1[1])
    return dx, d_wq, d_wo, (acc2[0], acc2[1], acc1[0]), acc2[2], small, (dkv_c, dkv_p)


def _pack_rows(pieces):
    rows, spans, off = [], [], 0
    for a in pieces:
        flat = a.reshape(-1).astype(F32)
        n = -(-flat.shape[0] // D_MODEL)
        rows.append(jnp.pad(flat, (0, n * D_MODEL - flat.shape[0])).reshape(n, D_MODEL))
        spans.append((off, a.shape))
        off += n
    pad = -off % 8
    if pad:
        rows.append(jnp.zeros((pad, D_MODEL), F32))
    return jnp.concatenate(rows, axis=0), spans, off + pad


def _unpack_rows(g, spans):
    out = []
    for off, shape in spans:
        size = 1
        for s in shape:
            size *= s
        n = -(-size // D_MODEL)
        out.append(g[:, off:off + n].reshape(N_DEV, n * D_MODEL)[:, :size].reshape((N_DEV,) + tuple(shape)))
    return out


def _unshard_last(g):
    nd = g.ndim
    perm = tuple(range(1, nd - 1)) + (0, nd - 1)
    t = g.transpose(perm)
    return t.reshape(t.shape[:-2] + (N_DEV * g.shape[-1],))


def _shard_last(a, me):
    s = a.shape[-1] // N_DEV
    return lax.dynamic_slice_in_dim(a, me * s, s, axis=a.ndim - 1)


def kernel(x, c, ffn_norm_w, ffn_w_gu, ffn_w_down, mod_w, mod_b, mix_norm_w, ssm_w_in, ssm_conv_w, ssm_conv_b, ssm_dt_bias, ssm_a_log, ssm_d, ssm_norm_w, ssm_w_out, kv_norm_w, kv_mod_w, kv_mod_b, w_kv, b_kv, attn_w_q, attn_b_q, attn_sinks, attn_w_o, attn_b_o, final_norm_w, loss_target, m_ffn_norm_w, m_ffn_w_gu, m_ffn_w_down, m_mod_w, m_mod_b, m_mix_norm_w, m_ssm_w_in, m_ssm_conv_w, m_ssm_conv_b, m_ssm_dt_bias, m_ssm_a_log, m_ssm_d, m_ssm_norm_w, m_ssm_w_out, m_kv_norm_w, m_kv_mod_w, m_kv_mod_b, m_w_kv, m_b_kv, m_attn_w_q, m_attn_b_q, m_attn_sinks, m_attn_w_o, m_attn_b_o, m_final_norm_w, v_ffn_norm_w, v_ffn_w_gu, v_ffn_w_down, v_mod_w, v_mod_b, v_mix_norm_w, v_ssm_w_in, v_ssm_conv_w, v_ssm_conv_b, v_ssm_dt_bias, v_ssm_a_log, v_ssm_d, v_ssm_norm_w, v_ssm_w_out, v_kv_norm_w, v_kv_mod_w, v_kv_mod_b, v_w_kv, v_b_kv, v_attn_w_q, v_attn_b_q, v_attn_sinks, v_attn_w_o, v_attn_b_o, v_final_norm_w):
    D = D_MODEL
    me = 4 * lax.axis_index("x") + 2 * lax.axis_index("y") + lax.axis_index("c")
    xs = x[0]
    target = loss_target[0]
    mod_cols = mod_w.shape[-1]
    kvm_cols = kv_mod_w.shape[-1]

    def fence(arrs):
        tot = jnp.zeros((1, 1), F32)
        for a in arrs:
            tot = tot + lax.slice(a, (0,) * a.ndim, (1,) * a.ndim).reshape(1, 1).astype(F32)
        return jnp.broadcast_to(tot, (8, 128))

    packed, spans, _ = _pack_rows([c, ffn_norm_w, ssm_conv_w, ssm_conv_b, ssm_norm_w])
    nrow = packed.shape[0]
    g1 = small_all_gather(packed).reshape(N_DEV, nrow, D)
    c_all, fnw_g, cw_g, cb_g, snw_g = _unpack_rows(g1, spans)
    c_all = c_all.reshape(N_DEV, D)
    ffn_nw = _unshard_last(fnw_g)
    conv_w = _unshard_last(cw_g)
    conv_b = _unshard_last(cb_g)
    ssm_nw = _unshard_last(snw_g)

    mod_b_loc = lax.dynamic_slice_in_dim(mod_b, me * mod_cols, mod_cols, axis=1).reshape(DEPTH, 1, mod_cols)
    kvb_loc = lax.dynamic_slice_in_dim(kv_mod_b, me * kvm_cols, kvm_cols, axis=0).reshape(1, 1, kvm_cols)
    modp, c_act = mod_fwd(c_all, mod_w, mod_b_loc, "mod_fwd")
    kvmp, _ = mod_fwd(c_all, kv_mod_w.reshape(1, D, kvm_cols), kvb_loc, "kv_mod_fwd")
    packed2, spans2, _ = _pack_rows([modp, kvmp])
    nrow2 = packed2.shape[0]
    g2 = small_all_gather(packed2).reshape(N_DEV, nrow2, D)
    modp_g, kvmp_g = _unpack_rows(g2, spans2)
    mod_all = modp_g.transpose(1, 2, 0, 3).reshape(DEPTH, N_DEV, N_MOD * D)
    kvm_all = kvmp_g.transpose(1, 2, 0, 3).reshape(N_DEV, 2 * D)
    mod_me = lax.dynamic_index_in_dim(mod_all, me, axis=1, keepdims=False).reshape(DEPTH, N_MOD, 1, D)
    kvm_me = lax.dynamic_index_in_dim(kvm_all, me, axis=0, keepdims=False).reshape(2, 1, D)

    gu_t = jnp.swapaxes(ffn_w_gu, 2, 3)
    win_t = jnp.transpose(ssm_w_in, (2, 0, 1))
    S = gu_t.shape[2]
    s_in = win_t.shape[0]
    r_dn = ffn_w_down.shape[2]
    r_mix = ssm_w_out.shape[1]
    r_at = attn_w_q.shape[1]

    def layer_pack(k):
        arrs = [gu_t[k, 0].astype(BF), gu_t[k, 1].astype(BF), ffn_w_down[k, 0].astype(BF), ffn_w_down[k, 1].astype(BF)]
        if k < N_A:
            arrs += [ssm_w_out[k].astype(BF), win_t[:, k].astype(BF)]
        else:
            arrs += [attn_w_q[k - N_A].astype(BF), attn_w_o[k - N_A].astype(BF)]
        if k == N_A:
            arrs.append(w_kv.astype(BF))
        return arrs

    packs = [layer_pack(k) for k in range(DEPTH)]
    gathered = [None] * DEPTH
    first = big_all_gather([packs[0][0], packs[0][2]])
    pending = [None] * DEPTH
    pend_mix, tok_next = gather2_start(packs[0][4:6], fence([g2, first[0]]), "gather_start_0m")
    pending[0], tok_next = gather2_start([packs[0][1], packs[0][3]], tok_next, "gather_start_0")
    for k in range(1, DEPTH):
        pending[k], tok_next = gather2_start(packs[k], tok_next, "gather_start_%d" % k)
    gathered[0] = [first[0], None, first[1], None, None, None]

    def wt_gu_full(i, j):
        return gathered[i][j].reshape(N_DEV * S, D)

    def w_dn_full(i, j):
        return gathered[i][2 + j].reshape(D_FF, D)

    def mix_rows(i, a):
        return gathered[i][4 + a].reshape(-1, D)

    def mamba_params(j):
        w_in_t = jnp.pad(mix_rows(j, 1), ((0, IN_PROJ_PAD - IN_PROJ), (0, 0)))
        return dict(w_in_t=w_in_t, w_out=mix_rows(j, 0), conv_w=conv_w[j], conv_b=conv_b[j].reshape(1, -1),
                    dt_bias=_pad_row(ssm_dt_bias[j]), a_log=_pad_row(ssm_a_log[j]),
                    d_exp=jnp.repeat(ssm_d[j], SSM_HEADDIM).reshape(SSM_GROUPS, 1, _GW),
                    norm_w=ssm_nw[j].reshape(1, -1))

    def attn_params(j):
        return dict(w_q=mix_rows(N_A + j, 0), w_o=mix_rows(N_A + j, 1),
                    b_q=attn_b_q[j].reshape(1, -1), b_o=attn_b_o[j].reshape(1, -1), sinks=_pad_row(attn_sinks[j]))

    saved = []
    kv = None
    kv_saved = None
    w_kv_full = None
    xcur = xs
    for i in range(DEPTH):
        if i >= 1:
            gathered[i] = gather2_wait(pending[i], xcur, "gather_wait_%d" % i)
        md = mod_me[i]
        if i == 0:
            md = md + tok_next[0, 0]
        if i == N_A:
            w_kv_full = gathered[N_A][6].reshape(D, KV_DIM)
            h_kv = norm_mod(xcur, kv_norm_w.reshape(1, D), kvm_me[0], kvm_me[1], "kv_norm")
            kv = matmul(h_kv, w_kv_full, "nn", F32, "kv_proj", bias=b_kv.reshape(1, -1))
            kv_saved = (xcur, h_kv)
        x1, s1 = _ffn_fwd(xcur, ffn_nw[i, 0].reshape(1, D), md[0], md[1], md[2], wt_gu_full(i, 0), w_dn_full(i, 0))
        gm = md[5]
        if i == 0:
            pend_mix, tok_r = gather2_relay(pend_mix, x1, "gather_relay_0m")
            gathered[0][4:6] = gather2_wait(pend_mix, tok_r, "gather_wait_0m")
        if i < N_A:
            pm = mamba_params(i)
            x2, s2 = _mamba_fwd(x1, mix_norm_w[i].reshape(1, D), md[3], md[4], gm, pm)
        else:
            pm = attn_params(i - N_A)
            x2, s2 = _attn_layer_fwd(x1, mix_norm_w[i].reshape(1, D), md[3], md[4], gm, pm, kv)
        g2f = md[8]
        if i + 1 < DEPTH:
            pending[i + 1], tok_r = gather2_relay(pending[i + 1], x2, "gather_relay_%d" % (i + 1))
            g2f = g2f + tok_r[0, 0]
        if i == 0:
            pending[0], tok_r = gather2_relay(pending[0], x2, "gather_relay_0")
            rest = gather2_wait(pending[0], tok_r, "gather_wait_0")
            gathered[0] = [first[0], rest[0], first[1], rest[1]] + gathered[0][4:6]
        x3, s3 = _ffn_fwd(x2, ffn_nw[i, 1].reshape(1, D), md[6], md[7], g2f, wt_gu_full(i, 1), w_dn_full(i, 1))
        saved.append((s1, s2, s3, pm))
        xcur = x3

    dx, accf = final_loss(xcur, final_norm_w.reshape(1, D), target)
    d_mod = [None] * DEPTH
    d_ffn_nw = [[None, None] for _ in range(DEPTH)]
    d_mix_nw = [None] * DEPTH
    sm_m, sm_a = [None] * N_A, [None] * N_A
    kv_parts = [None] * N_A
    d_kvm = d_kv_nw = d_bkv = None
    exchanges = []
    tok = None

    def send(arrs, tag, after=None):
        handle, t = xfer_start(arrs, True, dx if after is None else after, "exch_start_%s" % tag)
        exchanges.append((handle, tag))
        return t

    def ffn_slabs(d_wt, d_wdn):
        return [d_wt.reshape(N_DEV, S, D), d_wdn.reshape(N_DEV, r_dn, D)]

    for i in reversed(range(DEPTH)):
        md = mod_me[i]
        s1, s2, s3, pm = saved[i]
        g2 = md[8] if tok is None else md[8] + tok[0, 0]
        dx, d_wt, d_wdn, m2, d_ffn_nw[i][1] = _ffn_bwd(
            dx, s3, ffn_nw[i, 1].reshape(1, D), md[7], g2, wt_gu_full(i, 1), w_dn_full(i, 1))
        tok = send(ffn_slabs(d_wt, d_wdn), "f%d1" % i)
        gm = md[5] + tok[0, 0]
        if i < N_A:
            dx, d_in, d_out, mm_, d_mix_nw[i], sm_m[i] = _mamba_bwd(dx, s2, mix_norm_w[i].reshape(1, D), md[4], gm, pm)
            tok = send([d_in.reshape(N_DEV, s_in, D), d_out.reshape(N_DEV, r_mix, D)], "m%d" % i)
        else:
            j = i - N_A
            dx, d_q, d_o, mm_, d_mix_nw[i], sm_a[j], kv_parts[j] = _attn_layer_bwd(
                dx, s2, mix_norm_w[i].reshape(1, D), md[4], gm, pm, kv)
            tok = send([d_q.reshape(N_DEV, r_at, D), d_o.reshape(N_DEV, r_at, D)], "m%d" % i)
        g1 = md[2] + tok[0, 0]
        dx, d_wt, d_wdn, m1, d_ffn_nw[i][0] = _ffn_bwd(
            dx, s1, ffn_nw[i, 0].reshape(1, D), md[1], g1, wt_gu_full(i, 0), w_dn_full(i, 0))
        d_mod[i] = jnp.concatenate(list(m1) + list(mm_) + list(m2), axis=0)
        last = ffn_slabs(d_wt, d_wdn)
        if i == N_A:
            x_kv, h_kv = kv_saved
            dkv, acck = kv_grad_combine(kv_parts)
            d_bkv = acck[0]
            d_kv_w = matmul(h_kv, dkv, "tn", BF, "kv_wgrad")
            dh_kv = matmul(dkv, w_kv_full, "nt", F32, "kv_dgrad")
            dx, acc_kv = norm_mod_bwd(x_kv, dh_kv, dx, kv_norm_w.reshape(1, D), kvm_me[1], "kv_norm_bwd")
            d_kvm = jnp.concatenate([acc_kv[0], acc_kv[1]], axis=0)
            d_kv_nw = acc_kv[2]
            last.append(d_kv_w.reshape(N_DEV, -1, KV_DIM))
        if i > 0:
            tok = send(last, "f%d0" % i)
    grad_x = dx.reshape(x.shape)

    small_list = [
        jnp.stack(d_mod, 0), d_kvm,
        jnp.stack([jnp.stack(r, 0) for r in d_ffn_nw], 0),
        jnp.stack(d_mix_nw, 0),
        jnp.stack([s["conv_w"] for s in sm_m], 0), jnp.stack([s["conv_b"] for s in sm_m], 0),
        jnp.stack([s["dt_bias"] for s in sm_m], 0), jnp.stack([s["a_log"] for s in sm_m], 0),
        jnp.stack([s["d"] for s in sm_m], 0), jnp.stack([s["norm_w"] for s in sm_m], 0),
        d_kv_nw, d_bkv,
        jnp.stack([s["b_q"] for s in sm_a], 0), jnp.stack([s["sinks"] for s in sm_a], 0),
        jnp.stack([s["b_o"] for s in sm_a], 0), accf[0], accf[1],
    ]
    packed3, spans3, _ = _pack_rows(small_list)
    nrow3 = packed3.shape[0]
    g3 = small_all_gather(packed3).reshape(N_DEV, nrow3, D)
    tok_last = send(last, "f00", after=g3)
    (p_mod, p_kvm, p_fnw, p_mnw, p_cw, p_cb, p_dtb, p_al, p_d, p_snw, p_kvnw, p_bkv, p_bq, p_sk, p_bo, p_fin,
     p_loss) = _unpack_rows(g3, spans3)

    loss = 0.5 / D * jnp.sum(p_loss)

    c_act_t = c_act.T
    dmod_loc = _shard_last(p_mod, me).transpose(1, 0, 2)
    dkvm_loc = _shard_last(p_kvm, me).reshape(1, N_DEV, kvm_cols) + tok_last[0, 0]
    gp_mod_w = mod_wgrad(c_act_t, dmod_loc, "mod_wgrad")
    gp_kvm_w = mod_wgrad(c_act_t, dkvm_loc, "kv_mod_wgrad")[0]

    def as_parts_single(a):
        return a[None]

    def upd(name, parts, w, m, v):
        shp = w.shape
        c_last = shp[-1]
        out = adamw(parts.reshape(parts.shape[0], -1, c_last), w.reshape(-1, c_last), m.reshape(-1, c_last),
                    v.reshape(-1, c_last), "adamw_" + name)
        return tuple(o.reshape(shp) for o in out)

    views = {
        "ffn_w_gu": [jnp.swapaxes(t, 2, 3).reshape(-1, D) for t in (ffn_w_gu, m_ffn_w_gu, v_ffn_w_gu)],
        "ffn_w_down": [t.reshape(-1, D) for t in (ffn_w_down, m_ffn_w_down, v_ffn_w_down)],
        "ssm_w_out": [t.reshape(-1, D) for t in (ssm_w_out, m_ssm_w_out, v_ssm_w_out)],
        "attn_w_q": [t.reshape(-1, D) for t in (attn_w_q, m_attn_w_q, v_attn_w_q)],
        "attn_w_o": [t.reshape(-1, D) for t in (attn_w_o, m_attn_w_o, v_attn_w_o)],
    }
    filled = {k: None for k in views}

    def upd_rows(name, parts, row0):
        w, m, v = views[name]
        filled[name] = adamw(parts, w, m, v, "adamw_" + name, row0=row0, prev=filled[name])
        return filled[name][3]

    res = {}
    res["ffn_norm_w"] = upd("ffn_norm_w", _shard_last(p_fnw, me), ffn_norm_w, m_ffn_norm_w, v_ffn_norm_w)
    res["mod_w"] = upd("mod_w", as_parts_single(gp_mod_w), mod_w, m_mod_w, v_mod_w)
    res["mod_b"] = upd("mod_b", p_mod, mod_b, m_mod_b, v_mod_b)
    res["mix_norm_w"] = upd("mix_norm_w", p_mnw, mix_norm_w, m_mix_norm_w, v_mix_norm_w)
    res["ssm_conv_w"] = upd("ssm_conv_w", _shard_last(p_cw, me), ssm_conv_w, m_ssm_conv_w, v_ssm_conv_w)
    res["ssm_conv_b"] = upd("ssm_conv_b", _shard_last(p_cb, me), ssm_conv_b, m_ssm_conv_b, v_ssm_conv_b)
    res["ssm_dt_bias"] = upd("ssm_dt_bias", p_dtb, ssm_dt_bias, m_ssm_dt_bias, v_ssm_dt_bias)
    res["ssm_a_log"] = upd("ssm_a_log", p_al, ssm_a_log, m_ssm_a_log, v_ssm_a_log)
    res["ssm_d"] = upd("ssm_d", p_d, ssm_d, m_ssm_d, v_ssm_d)
    res["ssm_norm_w"] = upd("ssm_norm_w", _shard_last(p_snw, me), ssm_norm_w, m_ssm_norm_w, v_ssm_norm_w)
    res["kv_norm_w"] = upd("kv_norm_w", p_kvnw.reshape(N_DEV, 1, D), kv_norm_w.reshape(1, D),
                           m_kv_norm_w.reshape(1, D), v_kv_norm_w.reshape(1, D))
    res["kv_mod_w"] = upd("kv_mod_w", as_parts_single(gp_kvm_w), kv_mod_w, m_kv_mod_w, v_kv_mod_w)
    res["kv_mod_b"] = upd("kv_mod_b", p_kvm.reshape(N_DEV, 1, 2 * D), kv_mod_b.reshape(1, -1),
                          m_kv_mod_b.reshape(1, -1), v_kv_mod_b.reshape(1, -1))
    res["b_kv"] = upd("b_kv", p_bkv.reshape(N_DEV, 1, KV_DIM), b_kv.reshape(1, -1), m_b_kv.reshape(1, -1),
                      v_b_kv.reshape(1, -1))
    res["attn_b_q"] = upd("attn_b_q", p_bq, attn_b_q, m_attn_b_q, v_attn_b_q)
    res["attn_sinks"] = upd("attn_sinks", p_sk, attn_sinks, m_attn_sinks, v_attn_sinks)
    res["attn_b_o"] = upd("attn_b_o", p_bo, attn_b_o, m_attn_b_o, v_attn_b_o)
    res["final_norm_w"] = upd("final_norm_w", p_fin.reshape(N_DEV, 1, D), final_norm_w.reshape(1, D),
                              m_final_norm_w.reshape(1, D), v_final_norm_w.reshape(1, D))

    chain = fence([dx, tok_last] + [t[3] for t in res.values()])
    r_in_parts = [None] * N_A
    r_kv = None
    for handle, tag in exchanges:
        got = xfer_wait(handle, chain, "exch_wait_%s" % tag)
        i = int(tag[1])
        if tag[0] == "f":
            jf = int(tag[2])
            done = [upd_rows("ffn_w_gu", got[0], (2 * i + jf) * S), upd_rows("ffn_w_down", got[1], (2 * i + jf) * r_dn)]
            if len(got) > 2:
                res["w_kv"] = upd("w_kv", got[2], w_kv, m_w_kv, v_w_kv)
                done.append(res["w_kv"][3])
        elif i < N_A:
            r_in_parts[i] = got[0]
            done = [upd_rows("ssm_w_out", got[1], i * r_mix)]
            if i == 0:
                win_out = adamw(jnp.stack(r_in_parts, axis=2).reshape(N_DEV, s_in * N_A, D),
                                *[jnp.transpose(t, (2, 0, 1)).reshape(-1, D) for t in (ssm_w_in, m_ssm_w_in, v_ssm_w_in)],
                                "adamw_ssm_w_in")
                res["ssm_w_in"] = tuple(jnp.transpose(t.reshape(win_t.shape), (1, 2, 0)) for t in win_out)
                done.append(win_out[3])
        else:
            done = [upd_rows("attn_w_q", got[0], (i - N_A) * r_at), upd_rows("attn_w_o", got[1], (i - N_A) * r_at)]
        chain = fence(done)

    res["ffn_w_gu"] = tuple(jnp.swapaxes(t.reshape(gu_t.shape), 2, 3) for t in filled["ffn_w_gu"])
    res["ffn_w_down"] = tuple(t.reshape(ffn_w_down.shape) for t in filled["ffn_w_down"])
    res["ssm_w_out"] = tuple(t.reshape(ssm_w_out.shape) for t in filled["ssm_w_out"])
    res["attn_w_q"] = tuple(t.reshape(attn_w_q.shape) for t in filled["attn_w_q"])
    res["attn_w_o"] = tuple(t.reshape(attn_w_o.shape) for t in filled["attn_w_o"])

    names = ["ffn_norm_w", "ffn_w_gu", "ffn_w_down", "mod_w", "mod_b", "mix_norm_w", "ssm_w_in", "ssm_conv_w",
             "ssm_conv_b", "ssm_dt_bias", "ssm_a_log", "ssm_d", "ssm_norm_w", "ssm_w_out", "kv_norm_w", "kv_mod_w",
             "kv_mod_b", "w_kv", "b_kv", "attn_w_q", "attn_b_q", "attn_sinks", "attn_w_o", "attn_b_o", "final_norm_w"]
    vec_shapes = {"kv_norm_w": (D,), "kv_mod_b": (2 * D,), "b_kv": (KV_DIM,), "final_norm_w": (D,)}
    outs = [loss, grad_x]
    for k in range(4):
        for nme in names:
            t = res[nme][k]
            if nme in vec_shapes:
                t = t.reshape(vec_shapes[nme])
            outs.append(t)
    return tuple(outs)
```
